```python
import jax, jax.numpy as jnp
from jax import lax
import numpy as np

D_MODEL = 2048
BATCH = 8
SEQ = 2048
DEPTH = 2

N_META = 16
MLSTM_W = D_MODEL // 2
CONV_W = D_MODEL - MLSTM_W
MLSTM_HEADS = 4
DV = MLSTM_W // MLSTM_HEADS
DQK = DV // 2
QK_W = MLSTM_HEADS * DQK
CHUNK = 64
CONV_K = 3
D_FF = -(-8 * D_MODEL // (3 * 256)) * 256
GATE_CAP = 15.0
EPS = 1e-6
SPLIT_SIZES = (QK_W, QK_W, MLSTM_W, MLSTM_W, MLSTM_HEADS, MLSTM_HEADS, CONV_W, CONV_W, CONV_W)
D_IN = sum(SPLIT_SIZES)

kernel_name = "hymba_mlstm_shortconv_swiglu"


def rmsnorm(x, w):
    xf = x.astype(jnp.float32)
    y = xf * lax.rsqrt(jnp.mean(xf * xf, axis=-1, keepdims=True) + EPS)
    return (y * w.astype(jnp.float32)).astype(x.dtype)


def mlstm_chunkwise(q, k, v, log_i, log_f):
    b_, h_, t_, _ = q.shape
    nc = t_ // CHUNK

    def to_chunks(a):
        return jnp.moveaxis(a.reshape(a.shape[:2] + (nc, CHUNK) + a.shape[3:]), 2, 0)

    causal = jnp.tril(jnp.ones((CHUNK, CHUNK), dtype=bool))

    def step(carry, inp):
        c_st, n_st, m_st = carry
        qb, kb, vb, li, lf = inp
        b = jnp.cumsum(lf, axis=-1)
        dmat = jnp.where(causal, b[..., :, None] - b[..., None, :] + li[..., None, :], -jnp.inf)
        inter = b + m_st[..., None]
        m_t = jnp.maximum(inter, jnp.max(dmat, axis=-1))
        w_inter = jnp.exp(inter - m_t)
        s_w = jnp.einsum('bhtd,bhsd->bhts', qb, kb) * jnp.exp(dmat - m_t[..., None])
        num = (w_inter[..., None] * jnp.einsum('bhtd,bhde->bhte', qb, c_st)
               + jnp.einsum('bhts,bhse->bhte', s_w, vb))
        den = w_inter * jnp.einsum('bhtd,bhd->bht', qb, n_st) + jnp.sum(s_w, axis=-1)
        h = num / jnp.maximum(jnp.abs(den), jnp.exp(-m_t))[..., None]
        b_end = b[..., -1]
        decay = b_end[..., None] - b + li
        m_new = jnp.maximum(b_end + m_st, jnp.max(decay, axis=-1))
        w_old = jnp.exp(b_end + m_st - m_new)
        w_in = jnp.exp(decay - m_new[..., None])
        c_new = w_old[..., None, None] * c_st + jnp.einsum('bhs,bhsd,bhse->bhde', w_in, kb, vb)
        n_new = w_old[..., None] * n_st + jnp.einsum('bhs,bhsd->bhd', w_in, kb)
        return (c_new, n_new, m_new), h

    init = (jnp.zeros((b_, h_, q.shape[-1], v.shape[-1]), jnp.float32),
            jnp.zeros((b_, h_, q.shape[-1]), jnp.float32),
            jnp.zeros((b_, h_), jnp.float32))
    _, hs = lax.scan(step, init, tuple(map(to_chunks, (q, k, v, log_i, log_f))))
    return jnp.moveaxis(hs, 0, 2).reshape(b_, h_, t_, v.shape[-1])


def mlstm_group(q, k, v, i_raw, f_raw):
    seq_len = q.shape[1]
    pad_front = (-N_META) % CHUNK
    pad_back = (-(pad_front + seq_len)) % CHUNK
    tr = lambda a: jnp.moveaxis(a.astype(jnp.float32), 1, 2)
    q, k, v = tr(q) * (DQK ** -0.5), tr(k), tr(v)
    log_i = tr(GATE_CAP * jnp.tanh(i_raw.astype(jnp.float32) / GATE_CAP))
    log_f = jax.nn.log_sigmoid(tr(GATE_CAP * jnp.tanh(f_raw.astype(jnp.float32) / GATE_CAP)))
    pad4 = ((0, 0), (0, 0), (pad_front, pad_back), (0, 0))
    pad3 = ((0, 0), (0, 0), (pad_front, pad_back))
    q, k, v = jnp.pad(q, pad4), jnp.pad(k, pad4), jnp.pad(v, pad4)
    log_i = jnp.pad(log_i, pad3, constant_values=-jnp.inf)
    log_f = jnp.pad(log_f, pad3)
    h = mlstm_chunkwise(q, k, v, log_i, log_f)
    return h[:, :, pad_front:pad_front + seq_len]


def short_conv_group(u, gate_b, gate_c, conv_w):
    a = gate_c * u
    seq_len = a.shape[1]
    ap = jnp.pad(a, ((0, 0), (CONV_K - 1, 0), (0, 0)))
    conv = sum(ap[:, j:j + seq_len] * conv_w[j] for j in range(CONV_K))
    return gate_b * conv


def _fwd_setup_inputs(seed: int = 0) -> dict:
    key = jax.random.key(seed)
    ks = jax.random.split(key, 14)
    nrm = lambda k, shape, s: jax.random.normal(k, shape, jnp.float32) * s
    gain = lambda k, shape: 1.0 + 0.02 * jax.random.normal(k, shape, jnp.float32)
    b_i = nrm(ks[4], (DEPTH, MLSTM_HEADS), 0.1)
    b_f = 3.0 + nrm(ks[5], (DEPTH, MLSTM_HEADS), 0.5)
    return {
        "x": nrm(ks[0], (BATCH, SEQ, D_MODEL), 1.0),
        "meta_tokens": nrm(ks[1], (N_META, D_MODEL), 1.0),
        "norm_mix_w": gain(ks[2], (DEPTH, D_MODEL)),
        "w_in": nrm(ks[3], (DEPTH, D_MODEL, D_IN), D_MODEL ** -0.5),
        "b_gates": jnp.concatenate([b_i, b_f], axis=-1),
        "conv_w": nrm(ks[6], (DEPTH, CONV_K, CONV_W), CONV_K ** -0.5),
        "mlstm_norm_w": gain(ks[7], (DEPTH, MLSTM_W)),
        "w_out": nrm(ks[8], (DEPTH, D_MODEL, D_MODEL), D_MODEL ** -0.5),
        "norm_ffn_w": gain(ks[9], (DEPTH, D_MODEL)),
        "w_gate": nrm(ks[10], (DEPTH, D_MODEL, D_FF), D_MODEL ** -0.5),
        "w_up": nrm(ks[11], (DEPTH, D_MODEL, D_FF), D_MODEL ** -0.5),
        "w_down": nrm(ks[12], (DEPTH, D_FF, D_MODEL), D_FF ** -0.5),
        "norm_final_w": gain(ks[13], (D_MODEL,)),
    }


def _fwd_reference(x, meta_tokens, norm_mix_w, w_in, b_gates, conv_w, mlstm_norm_w, w_out,
              norm_ffn_w, w_gate, w_up, w_down, norm_final_w):
    bsz = x.shape[0]
    meta = jnp.broadcast_to(meta_tokens.astype(x.dtype)[None], (bsz, N_META, D_MODEL))
    h = jnp.concatenate([meta, x], axis=1)
    seq_len = h.shape[1]
    split_points = np.cumsum(SPLIT_SIZES)[:-1].tolist()
    for l in range(DEPTH):
        hn = rmsnorm(h, norm_mix_w[l])
        proj = hn @ w_in[l]
        q, k, v, og, ig, fg, u, gb, gc = jnp.split(proj, split_points, axis=-1)
        ig = ig + b_gates[l, :MLSTM_HEADS]
        fg = fg + b_gates[l, MLSTM_HEADS:]
        hm = mlstm_group(q.reshape(bsz, seq_len, MLSTM_HEADS, DQK),
                         k.reshape(bsz, seq_len, MLSTM_HEADS, DQK),
                         v.reshape(bsz, seq_len, MLSTM_HEADS, DV), ig, fg)
        hm = rmsnorm(hm, mlstm_norm_w[l].reshape(MLSTM_HEADS, 1, DV))
        hm = jnp.moveaxis(hm, 1, 2).reshape(bsz, seq_len, MLSTM_W).astype(h.dtype)
        hm = jax.nn.sigmoid(og) * hm
        hc = short_conv_group(u, gb, gc, conv_w[l])
        h = h + jnp.concatenate([hm, hc], axis=-1) @ w_out[l]
        hf = rmsnorm(h, norm_ffn_w[l])
        h = h + (jax.nn.silu(hf @ w_gate[l]) * (hf @ w_up[l])) @ w_down[l]
    out = rmsnorm(h, norm_final_w)
    return out[:, N_META:]


import jax as _jax
import jax.numpy as _jnp

TWIN_FORMAT = 'train_step'
FWD_PARAMS = ['x', 'meta_tokens', 'norm_mix_w', 'w_in', 'b_gates', 'conv_w', 'mlstm_norm_w', 'w_out', 'norm_ffn_w', 'w_gate', 'w_up', 'w_down', 'norm_final_w']
TWIN_WEIGHTS = ['meta_tokens', 'norm_mix_w', 'w_in', 'b_gates', 'conv_w', 'mlstm_norm_w', 'w_out', 'norm_ffn_w', 'w_gate', 'w_up', 'w_down', 'norm_final_w']
TWIN_DIFF_INPUT = 'x'
TWIN_INPUTS = ['x', 'meta_tokens', 'norm_mix_w', 'w_in', 'b_gates', 'conv_w', 'mlstm_norm_w', 'w_out', 'norm_ffn_w', 'w_gate', 'w_up', 'w_down', 'norm_final_w', 'loss_target', 'm_meta_tokens', 'm_norm_mix_w', 'm_w_in', 'm_b_gates', 'm_conv_w', 'm_mlstm_norm_w', 'm_w_out', 'm_norm_ffn_w', 'm_w_gate', 'm_w_up', 'm_w_down', 'm_norm_final_w', 'v_meta_tokens', 'v_norm_mix_w', 'v_w_in', 'v_b_gates', 'v_conv_w', 'v_mlstm_norm_w', 'v_w_out', 'v_norm_ffn_w', 'v_w_gate', 'v_w_up', 'v_w_down', 'v_norm_final_w']
TWIN_OUTPUTS = ['loss', 'grad_x', 'grad_meta_tokens', 'grad_norm_mix_w', 'grad_w_in', 'grad_b_gates', 'grad_conv_w', 'grad_mlstm_norm_w', 'grad_w_out', 'grad_norm_ffn_w', 'grad_w_gate', 'grad_w_up', 'grad_w_down', 'grad_norm_final_w', 'delta_meta_tokens', 'delta_norm_mix_w', 'delta_w_in', 'delta_b_gates', 'delta_conv_w', 'delta_mlstm_norm_w', 'delta_w_out', 'delta_norm_ffn_w', 'delta_w_gate', 'delta_w_up', 'delta_w_down', 'delta_norm_final_w', 'new_m_meta_tokens', 'new_m_norm_mix_w', 'new_m_w_in', 'new_m_b_gates', 'new_m_conv_w', 'new_m_mlstm_norm_w', 'new_m_w_out', 'new_m_norm_ffn_w', 'new_m_w_gate', 'new_m_w_up', 'new_m_w_down', 'new_m_norm_final_w', 'new_v_meta_tokens', 'new_v_norm_mix_w', 'new_v_w_in', 'new_v_b_gates', 'new_v_conv_w', 'new_v_mlstm_norm_w', 'new_v_w_out', 'new_v_norm_ffn_w', 'new_v_w_gate', 'new_v_w_up', 'new_v_w_down', 'new_v_norm_final_w']
TWIN_LEAF_KINDS = {'loss': 'loss', 'grad_x': 'grad_x', 'grad_meta_tokens': 'grad_w', 'grad_norm_mix_w': 'grad_w', 'grad_w_in': 'grad_w', 'grad_b_gates': 'grad_w', 'grad_conv_w': 'grad_w', 'grad_mlstm_norm_w': 'grad_w', 'grad_w_out': 'grad_w', 'grad_norm_ffn_w': 'grad_w', 'grad_w_gate': 'grad_w', 'grad_w_up': 'grad_w', 'grad_w_down': 'grad_w', 'grad_norm_final_w': 'grad_w', 'delta_meta_tokens': 'delta_w', 'delta_norm_mix_w': 'delta_w', 'delta_w_in': 'delta_w', 'delta_b_gates': 'delta_w', 'delta_conv_w': 'delta_w', 'delta_mlstm_norm_w': 'delta_w', 'delta_w_out': 'delta_w', 'delta_norm_ffn_w': 'delta_w', 'delta_w_gate': 'delta_w', 'delta_w_up': 'delta_w', 'delta_w_down': 'delta_w', 'delta_norm_final_w': 'delta_w', 'new_m_meta_tokens': 'new_m', 'new_m_norm_mix_w': 'new_m', 'new_m_w_in': 'new_m', 'new_m_b_gates': 'new_m', 'new_m_conv_w': 'new_m', 'new_m_mlstm_norm_w': 'new_m', 'new_m_w_out': 'new_m', 'new_m_norm_ffn_w': 'new_m', 'new_m_w_gate': 'new_m', 'new_m_w_up': 'new_m', 'new_m_w_down': 'new_m', 'new_m_norm_final_w': 'new_m', 'new_v_meta_tokens': 'new_v', 'new_v_norm_mix_w': 'new_v', 'new_v_w_in': 'new_v', 'new_v_b_gates': 'new_v', 'new_v_conv_w': 'new_v', 'new_v_mlstm_norm_w': 'new_v', 'new_v_w_out': 'new_v', 'new_v_norm_ffn_w': 'new_v', 'new_v_w_gate': 'new_v', 'new_v_w_up': 'new_v', 'new_v_w_down': 'new_v', 'new_v_norm_final_w': 'new_v'}


def _forward(args):
    return _fwd_reference(*[args[k] for k in FWD_PARAMS])


def _output_shape():
    out = _jax.eval_shape(lambda: _forward(_fwd_setup_inputs(0)))
    return out.shape, out.dtype

N_MICROBATCH = 1
ADAM_LR = 0.001
ADAM_B1 = 0.9
ADAM_B2 = 0.999
ADAM_EPS = 1e-08
ADAM_WD = 0.01
ADAM_STEP = 10
PER_EXAMPLE_BATCH_AXIS = {'x': 0, 'loss_target': 0}
SHARED_INPUTS = []
_WEIGHT_DTYPES = {'meta_tokens': _jnp.float32, 'norm_mix_w': _jnp.float32, 'w_in': _jnp.float32, 'b_gates': _jnp.float32, 'conv_w': _jnp.float32, 'mlstm_norm_w': _jnp.float32, 'w_out': _jnp.float32, 'norm_ffn_w': _jnp.float32, 'w_gate': _jnp.float32, 'w_up': _jnp.float32, 'w_down': _jnp.float32, 'norm_final_w': _jnp.float32}
MOMENT_SCALE = {'meta_tokens': 3.228903e-03, 'norm_mix_w': 8.853901e-02, 'w_in': 4.906772e-02, 'b_gates': 1.610501e-01, 'conv_w': 5.970595e-02, 'mlstm_norm_w': 3.157768e-02, 'w_out': 4.680375e-02, 'norm_ffn_w': 4.178777e-02, 'w_gate': 1.828683e-02, 'w_up': 1.768889e-02, 'w_down': 2.934674e-02, 'norm_final_w': 8.006354e+00}


def _to_microbatches(a, axis):
    t = _jnp.moveaxis(a, axis, 0)
    t = t.reshape((N_MICROBATCH, t.shape[0] // N_MICROBATCH) + t.shape[1:])
    return _jnp.moveaxis(t, 1, axis + 1)


def setup_inputs(seed: int = 0) -> dict:
    inp = _fwd_setup_inputs(seed)
    key = _jax.random.fold_in(_jax.random.key(seed), 7919)
    shape, _ = _output_shape()
    out = dict(inp)
    out["loss_target"] = _jax.random.normal(_jax.random.fold_in(key, 0), shape, _jnp.float32)
    for i, name in enumerate(TWIN_WEIGHTS):
        w = inp[name].astype(_jnp.float32)
        if MOMENT_SCALE is None:
            s = _jnp.sqrt(_jnp.mean(_jnp.square(w)) + 1e-30)
        else:
            s = MOMENT_SCALE[name]
        km, kv = _jax.random.split(_jax.random.fold_in(key, i + 1))
        out[name] = w
        out["m_" + name] = s * _jax.random.normal(km, w.shape, _jnp.float32)
        out["v_" + name] = (s * s) * _jax.random.uniform(kv, w.shape, _jnp.float32, 0.5, 1.5)
    if N_MICROBATCH > 1:
        for name, axis in PER_EXAMPLE_BATCH_AXIS.items():
            out[name] = _to_microbatches(out[name], axis)
    return {'x': out['x'], 'meta_tokens': out['meta_tokens'], 'norm_mix_w': out['norm_mix_w'], 'w_in': out['w_in'], 'b_gates': out['b_gates'], 'conv_w': out['conv_w'], 'mlstm_norm_w': out['mlstm_norm_w'], 'w_out': out['w_out'], 'norm_ffn_w': out['norm_ffn_w'], 'w_gate': out['w_gate'], 'w_up': out['w_up'], 'w_down': out['w_down'], 'norm_final_w': out['norm_final_w'], 'loss_target': out['loss_target'], 'm_meta_tokens': out['m_meta_tokens'], 'm_norm_mix_w': out['m_norm_mix_w'], 'm_w_in': out['m_w_in'], 'm_b_gates': out['m_b_gates'], 'm_conv_w': out['m_conv_w'], 'm_mlstm_norm_w': out['m_mlstm_norm_w'], 'm_w_out': out['m_w_out'], 'm_norm_ffn_w': out['m_norm_ffn_w'], 'm_w_gate': out['m_w_gate'], 'm_w_up': out['m_w_up'], 'm_w_down': out['m_w_down'], 'm_norm_final_w': out['m_norm_final_w'], 'v_meta_tokens': out['v_meta_tokens'], 'v_norm_mix_w': out['v_norm_mix_w'], 'v_w_in': out['v_w_in'], 'v_b_gates': out['v_b_gates'], 'v_conv_w': out['v_conv_w'], 'v_mlstm_norm_w': out['v_mlstm_norm_w'], 'v_w_out': out['v_w_out'], 'v_norm_ffn_w': out['v_norm_ffn_w'], 'v_w_gate': out['v_w_gate'], 'v_w_up': out['v_w_up'], 'v_w_down': out['v_w_down'], 'v_norm_final_w': out['v_norm_final_w']}


def _loss(weights, diff, rest, loss_target):
    with _jax.named_scope("forward"):
        args = {**rest, TWIN_DIFF_INPUT: diff, **{k: w.astype(_WEIGHT_DTYPES[k]) for k, w in weights.items()}}
        y = _forward(args)
    with _jax.named_scope("loss_head"):
        err = _jnp.square(y.astype(_jnp.float32) - loss_target)
        return 0.5 * _jnp.sum(_jnp.mean(err, axis=-1)) if err.ndim else 0.5 * err


def _adamw(w, g, m, v):
    m = ADAM_B1 * m + (1.0 - ADAM_B1) * g
    v = ADAM_B2 * v + (1.0 - ADAM_B2) * _jnp.square(g)
    m_hat = m / (1.0 - ADAM_B1 ** ADAM_STEP)
    v_hat = v / (1.0 - ADAM_B2 ** ADAM_STEP)
    delta = -ADAM_LR * (m_hat / (_jnp.sqrt(v_hat) + ADAM_EPS) + ADAM_WD * w)
    return delta, m, v


def reference(x, meta_tokens, norm_mix_w, w_in, b_gates, conv_w, mlstm_norm_w, w_out, norm_ffn_w, w_gate, w_up, w_down, norm_final_w, loss_target, m_meta_tokens, m_norm_mix_w, m_w_in, m_b_gates, m_conv_w, m_mlstm_norm_w, m_w_out, m_norm_ffn_w, m_w_gate, m_w_up, m_w_down, m_norm_final_w, v_meta_tokens, v_norm_mix_w, v_w_in, v_b_gates, v_conv_w, v_mlstm_norm_w, v_w_out, v_norm_ffn_w, v_w_gate, v_w_up, v_w_down, v_norm_final_w):
    given = dict(x=x, meta_tokens=meta_tokens, norm_mix_w=norm_mix_w, w_in=w_in, b_gates=b_gates, conv_w=conv_w, mlstm_norm_w=mlstm_norm_w, w_out=w_out, norm_ffn_w=norm_ffn_w, w_gate=w_gate, w_up=w_up, w_down=w_down, norm_final_w=norm_final_w, loss_target=loss_target, m_meta_tokens=m_meta_tokens, m_norm_mix_w=m_norm_mix_w, m_w_in=m_w_in, m_b_gates=m_b_gates, m_conv_w=m_conv_w, m_mlstm_norm_w=m_mlstm_norm_w, m_w_out=m_w_out, m_norm_ffn_w=m_norm_ffn_w, m_w_gate=m_w_gate, m_w_up=m_w_up, m_w_down=m_w_down, m_norm_final_w=m_norm_final_w, v_meta_tokens=v_meta_tokens, v_norm_mix_w=v_norm_mix_w, v_w_in=v_w_in, v_b_gates=v_b_gates, v_conv_w=v_conv_w, v_mlstm_norm_w=v_mlstm_norm_w, v_w_out=v_w_out, v_norm_ffn_w=v_norm_ffn_w, v_w_gate=v_w_gate, v_w_up=v_w_up, v_w_down=v_w_down, v_norm_final_w=v_norm_final_w)
    weights = {n: given[n] for n in TWIN_WEIGHTS}
    shared = {n: given[n] for n in SHARED_INPUTS}
    per_example = {n: given[n] for n in ['x']}
    grad_fn = _jax.value_and_grad(_loss, argnums=(0, 1))

    def one_microbatch(ex, loss_target):
        ex = dict(ex)
        diff = ex.pop(TWIN_DIFF_INPUT)
        return grad_fn(weights, diff, {**shared, **ex}, loss_target)

    if N_MICROBATCH == 1:
        loss, (grad_w, grad_x) = one_microbatch(per_example, given["loss_target"])
    else:
        def body(carry, xs):
            loss_sum, grad_sum = carry
            l_k, (gw_k, gx_k) = one_microbatch(xs[0], xs[1])
            with _jax.named_scope("update"):
                return (loss_sum + l_k, _jax.tree.map(_jnp.add, grad_sum, gw_k)), gx_k

        init = (_jnp.zeros((), _jnp.float32), _jax.tree.map(_jnp.zeros_like, weights))
        (loss, grad_w), grad_x = _jax.lax.scan(body, init, (per_example, given["loss_target"]))
    with _jax.named_scope("update"):
        delta_w, new_m, new_v = {}, {}, {}
        for n in TWIN_WEIGHTS:
            delta_w[n], new_m[n], new_v[n] = _adamw(weights[n], grad_w[n], given["m_" + n], given["v_" + n])
    return (loss, grad_x, *[grad_w[n] for n in TWIN_WEIGHTS], *[delta_w[n] for n in TWIN_WEIGHTS],
            *[new_m[n] for n in TWIN_WEIGHTS], *[new_v[n] for n in TWIN_WEIGHTS])
```

```python
import functools

import jax
import jax.numpy as jnp
from jax import lax
from jax.experimental import pallas as pl
from jax.experimental.pallas import tpu as pltpu

F32 = jnp.float32
BF16 = jnp.bfloat16

D = 2048
N_META = 16
HEADS = 4
DQK = 128
DV = 256
MW = HEADS * DV
CW = D - MW
QKW = HEADS * DQK
DFF = 5632
DIN = 6152
NCHIP = 4
DEPTH = 2
GATE_CAP = 15.0
EPS = 1e-6
QK_SCALE = DQK ** -0.5
LANE = 128
VMEM_LIMIT = 60 * 1024 * 1024

ADAM_LR = 0.001
ADAM_B1 = 0.9
ADAM_B2 = 0.999
ADAM_EPS = 1e-08
ADAM_WD = 0.01
ADAM_STEP = 10

MESH = pl.DeviceIdType.MESH
ANY = pl.BlockSpec(memory_space=pl.ANY)


def _cp(sem):
    return pltpu.CompilerParams(dimension_semantics=sem, vmem_limit_bytes=VMEM_LIMIT)


def _sds(shape, dtype):
    return jax.ShapeDtypeStruct(shape, dtype)


_DN = {"nn": (((1,), (0,)), ((), ())), "nt": (((1,), (1,)), ((), ())), "tn": (((0,), (0,)), ((), ()))}


def _mm(name, kind, a, b, a_spec, b_spec, o_spec, out_shape, grid, acc_shape=None, res=None, res_spec=None):
    nk = grid[2]
    has_res = res is not None

    def body(*refs):
        if has_res:
            a_ref, b_ref, r_ref, o_ref = refs[:4]
        else:
            a_ref, b_ref, o_ref = refs[:3]
            r_ref = None
        p = lax.dot_general(a_ref[...], b_ref[...], _DN[kind], preferred_element_type=F32)
        if nk == 1:
            if r_ref is not None:
                p = p + r_ref[...]
            o_ref[...] = p.astype(o_ref.dtype)
        else:
            acc = refs[-1]
            k = pl.program_id(2)

            @pl.when(k == 0)
            def _():
                acc[...] = p

            @pl.when(k > 0)
            def _():
                acc[...] += p

            @pl.when(k == nk - 1)
            def _():
                r = acc[...]
                if r_ref is not None:
                    r = r + r_ref[...]
                o_ref[...] = r.astype(o_ref.dtype)

    ins = [a, b] + ([res] if has_res else [])
    in_specs = [a_spec, b_spec] + ([res_spec] if has_res else [])
    scratch = [pltpu.VMEM(acc_shape, F32)] if nk > 1 else []
    return pl.pallas_call(
        body, name=name, grid=grid, in_specs=in_specs, out_specs=o_spec, out_shape=out_shape,
        scratch_shapes=scratch, compiler_params=_cp(("parallel", "parallel", "arbitrary")))(*ins)


def _norm_fwd(name, h, w):
    lp = h.shape[0]
    tm = lp // 4

    def body(h_ref, w_ref, o_ref):
        x = h_ref[...]
        r = lax.rsqrt(jnp.mean(x * x, axis=1, keepdims=True) + EPS)
        o_ref[...] = (x * r * w_ref[...]).astype(BF16)

    return pl.pallas_call(
        body, name=name, grid=(4,),
        in_specs=[pl.BlockSpec((tm, D), lambda i: (i, 0)), pl.BlockSpec((1, D), lambda i: (0, 0))],
        out_specs=pl.BlockSpec((tm, D), lambda i: (i, 0)), out_shape=_sds((lp, D), BF16),
        compiler_params=_cp(("parallel",)))(h, w)


def _norm_bwd(name, dy, h, w, dres):
    lp = h.shape[0]
    tm = lp // 8

    def body(dy_ref, h_ref, w_ref, dres_ref, dh_ref, dhb_ref, dw_ref):
        x = h_ref[...]
        r = lax.rsqrt(jnp.mean(x * x, axis=1, keepdims=True) + EPS)
        xh = x * r
        dy_v = dy_ref[...]
        dxh = dy_v * w_ref[...]
        dx = r * (dxh - xh * jnp.mean(dxh * xh, axis=1, keepdims=True))
        dh = dres_ref[...] + dx
        dh_ref[...] = dh
        dhb_ref[...] = dh.astype(BF16)

        @pl.when(pl.program_id(0) == 0)
        def _():
            dw_ref[...] = jnp.zeros_like(dw_ref)

        dw_ref[0:1, :] += jnp.sum(dy_v * xh, axis=0, keepdims=True)

    row = pl.BlockSpec((tm, D), lambda i: (i, 0))
    return pl.pallas_call(
        body, name=name, grid=(8,),
        in_specs=[row, row, pl.BlockSpec((1, D), lambda i: (0, 0)), row],
        out_specs=[row, row, pl.BlockSpec((8, D), lambda i: (0, 0))],
        out_shape=[_sds((lp, D), F32), _sds((lp, D), BF16), _sds((8, D), F32)],
        compiler_params=_cp(("arbitrary",)))(dy, h, w, dres)


def _loss_head(h, tgt, w, n_real):
    lp = h.shape[0]
    tm = lp // 8

    def body(h_ref, t_ref, w_ref, dh_ref, dhb_ref, dw_ref, loss_ref):
        i = pl.program_id(0)
        x = h_ref[...]
        r = lax.rsqrt(jnp.mean(x * x, axis=1, keepdims=True) + EPS)
        xh = x * r
        wv = w_ref[...]
        row = i * tm + lax.broadcasted_iota(jnp.int32, (tm, 1), 0)
        valid = jnp.logical_and(row >= N_META, row < n_real)
        err = jnp.where(valid, xh * wv - t_ref[...], 0.0)
        dy_v = err * (1.0 / D)
        dxh = dy_v * wv
        dx = r * (dxh - xh * jnp.mean(dxh * xh, axis=1, keepdims=True))
        dh_ref[...] = dx
        dhb_ref[...] = dx.astype(BF16)

        @pl.when(i == 0)
        def _():
            dw_ref[...] = jnp.zeros_like(dw_ref)
            loss_ref[...] = jnp.zeros_like(loss_ref)

        dw_ref[0:1, :] += jnp.sum(dy_v * xh, axis=0, keepdims=True)
        part = jnp.sum(jnp.sum(err * err, axis=1, keepdims=True), axis=0, keepdims=True) * (0.5 / D)
        loss_ref[...] += jnp.broadcast_to(part, loss_ref.shape)

    row = pl.BlockSpec((tm, D), lambda i: (i, 0))
    return pl.pallas_call(
        body, name="loss_head", grid=(8,),
        in_specs=[row, row, pl.BlockSpec((1, D), lambda i: (0, 0))],
        out_specs=[row, row, pl.BlockSpec((8, D), lambda i: (0, 0)), pl.BlockSpec((8, LANE), lambda i: (0, 0))],
        out_shape=[_sds((lp, D), F32), _sds((lp, D), BF16), _sds((8, D), F32), _sds((8, LANE), F32)],
        compiler_params=_cp(("arbitrary",)))(h, tgt, w)


def _ffn_up(name, hf, wg4, wu4, layer):
    lp = hf.shape[0]
    tm = lp // 4
    tn = DFF // NCHIP

    def body(x_ref, wg_ref, wu_ref, g_ref, u_ref, a_ref):
        x = x_ref[...]
        g = jnp.dot(x, wg_ref[...], preferred_element_type=F32)
        u = jnp.dot(x, wu_ref[...], preferred_element_type=F32)
        g_ref[...] = g.astype(BF16)
        u_ref[...] = u.astype(BF16)
        a_ref[...] = (g * jax.nn.sigmoid(g) * u).astype(BF16)

    wspec = pl.BlockSpec((None, None, D, tn), lambda j, i: (layer, j, 0, 0))
    ospec = pl.BlockSpec((tm, tn), lambda j, i: (i, j))
    o = _sds((lp, DFF), BF16)
    return pl.pallas_call(
        body, name=name, grid=(NCHIP, 4),
        in_specs=[pl.BlockSpec((tm, D), lambda j, i: (i, 0)), wspec, wspec],
        out_specs=[ospec, ospec, ospec], out_shape=[o, o, o],
        compiler_params=_cp(("parallel", "parallel")))(hf, wg4, wu4)


def _ffn_bwd_act(name, dhb, wd4, g, u, layer):
    lp = dhb.shape[0]
    tm = lp // 4
    tn = DFF // NCHIP

    def body(d_ref, w_ref, g_ref, u_ref, dg_ref, du_ref):
        da = lax.dot_general(d_ref[...], w_ref[...], _DN["nt"], preferred_element_type=F32)
        gv = g_ref[...].astype(F32)
        uv = u_ref[...].astype(F32)
        sg = jax.nn.sigmoid(gv)
        dg_ref[...] = (da * uv * (sg * (1.0 + gv * (1.0 - sg)))).astype(BF16)
        du_ref[...] = (da * (gv * sg)).astype(BF16)

    ospec = pl.BlockSpec((tm, tn), lambda j, i: (i, j))
    o = _sds((lp, DFF), BF16)
    return pl.pallas_call(
        body, name=name, grid=(NCHIP, 4),
        in_specs=[pl.BlockSpec((tm, D), lambda j, i: (i, 0)),
                  pl.BlockSpec((None, None, tn, D), lambda j, i: (layer, j, 0, 0)), ospec, ospec],
        out_specs=[ospec, ospec], out_shape=[o, o],
        compiler_params=_cp(("parallel", "parallel")))(dhb, wd4, g, u)


def _shift_rows(x, d, row):
    return jnp.where(row >= d, pltpu.roll(x, d, axis=0), 0.0)


def _scan_steps(lp):
    d = 1
    while d < lp:
        yield d
        d *= 2


def _gate_values(pre):
    t = GATE_CAP * jnp.tanh(pre * (1.0 / GATE_CAP))
    lf = jnp.minimum(t, 0.0) - jnp.log(1.0 + jnp.exp(-jnp.abs(t)))
    return t, lf


def _gate_prep(name, gates_pre, bias):
    lp = gates_pre.shape[0]

    def body(p_ref, b_ref, grow_ref, m_ref, c_ref):
        pre = p_ref[...] + b_ref[...]
        lane = lax.broadcasted_iota(jnp.int32, (lp, LANE), 1)
        row = lax.broadcasted_iota(jnp.int32, (lp, LANE), 0)
        t, lf = _gate_values(pre)
        f = jnp.where(jnp.logical_and(lane >= HEADS, lane < 2 * HEADS), lf, 0.0)
        for d in _scan_steps(lp):
            f = f + _shift_rows(f, d, row)
        fs = pltpu.roll(f, LANE - HEADS, axis=1)
        g = jnp.where(lane < HEADS, t - fs, 0.0)
        m = g
        for d in _scan_steps(lp):
            m = jnp.maximum(m, jnp.where(row >= d, pltpu.roll(m, d, axis=0), m))
        grow_ref[...] = g.T
        m_ref[...] = m
        c_ref[...] = jnp.where(lane < HEADS, -fs - m, 0.0)

    full = pl.BlockSpec((lp, LANE), lambda: (0, 0))
    return pl.pallas_call(
        body, name=name, in_specs=[full, pl.BlockSpec((1, LANE), lambda: (0, 0))],
        out_specs=[pl.BlockSpec((LANE, lp), lambda: (0, 0)), full, full],
        out_shape=[_sds((LANE, lp), F32), _sds((lp, LANE), F32), _sds((lp, LANE), F32)],
        compiler_params=pltpu.CompilerParams(vmem_limit_bytes=VMEM_LIMIT))(gates_pre, bias)


def _pick_lane(blk, h):
    lane = lax.broadcasted_iota(jnp.int32, blk.shape, 1)
    return jnp.sum(jnp.where(lane == h, blk, 0.0), axis=1, keepdims=True)


def _mlstm_weights(q, k, grow, mcol, i, bq, lp):
    s = lax.dot_general(q, k, _DN["nt"], preferred_element_type=F32) * QK_SCALE
    row = i * bq + lax.broadcasted_iota(jnp.int32, (bq, 1), 0)
    col = lax.broadcasted_iota(jnp.int32, (1, lp), 1)
    a = jnp.where(col <= row, jnp.exp(jnp.minimum(grow - mcol, 0.0)), 0.0)
    return s, a


def _mlstm_fwd(name, qkv, grow, mcol_all, ccol_all, nq):
    lp = qkv.shape[0]
    bq = lp // nq

    def body(q_ref, k_ref, v_ref, grow_ref, m_ref, c_ref, o_ref):
        h = pl.program_id(0)
        i = pl.program_id(1)
        grow_h = grow_ref[pl.ds(h, 1), :]
        mcol = _pick_lane(m_ref[...], h)
        ccol = _pick_lane(c_ref[...], h)
        s, a = _mlstm_weights(q_ref[...], k_ref[...], grow_h, mcol, i, bq, lp)
        p = a * s
        den = jnp.sum(p, axis=1, keepdims=True)
        num = jnp.dot(p.astype(BF16), v_ref[...], preferred_element_type=F32)
        o_ref[...] = num / jnp.maximum(jnp.abs(den), jnp.exp(ccol))

    return pl.pallas_call(
        body, name=name, grid=(HEADS, nq),
        in_specs=[pl.BlockSpec((bq, DQK), lambda h, i: (i, h)),
                  pl.BlockSpec((lp, DQK), lambda h, i: (0, HEADS + h)),
                  pl.BlockSpec((lp, DV), lambda h, i: (0, HEADS + h)),
                  pl.BlockSpec((8, lp), lambda h, i: (0, 0)),
                  pl.BlockSpec((bq, LANE), lambda h, i: (i, 0)),
                  pl.BlockSpec((bq, LANE), lambda h, i: (i, 0))],
        out_specs=pl.BlockSpec((bq, DV), lambda h, i: (i, h)),
        out_shape=_sds((lp, MW), F32),
        compiler_params=_cp(("parallel", "parallel")))(qkv, qkv, qkv, grow, mcol_all, ccol_all)


def _mlstm_bwd(name, qkv, grow, mcol_all, ccol_all, ht, dht, nq):
    lp = qkv.shape[0]
    bq = lp // nq

    def body(q_ref, k_ref, v_ref, grow_ref, m_ref, c_ref, ht_ref, dht_ref,
             dq_ref, dk_ref, dv_ref, dgrow_ref, dfx_ref, dkt_acc, dvt_acc):
        h = pl.program_id(0)
        i = pl.program_id(1)

        @pl.when(jnp.logical_and(h == 0, i == 0))
        def _():
            dgrow_ref[...] = jnp.zeros_like(dgrow_ref)
            dfx_ref[...] = jnp.zeros_like(dfx_ref)

        @pl.when(i == 0)
        def _():
            dkt_acc[...] = jnp.zeros_like(dkt_acc)
            dvt_acc[...] = jnp.zeros_like(dvt_acc)

        q = q_ref[...]
        k = k_ref[...]
        v = v_ref[...]
        grow_h = grow_ref[pl.ds(h, 1), :]
        mcol = _pick_lane(m_ref[...], h)
        ccol = _pick_lane(c_ref[...], h)
        s, a = _mlstm_weights(q, k, grow_h, mcol, i, bq, lp)
        p = a * s
        den = jnp.sum(p, axis=1, keepdims=True)
        clamp = jnp.exp(ccol)
        active = jnp.abs(den) < clamp
        dd = jnp.maximum(jnp.abs(den), clamp)
        dht_v = dht_ref[...]
        hdh = jnp.sum(dht_v * ht_ref[...], axis=1, keepdims=True)
        dn = (dht_v / dd).astype(BF16)
        dden = jnp.where(active, 0.0, -(hdh / dd) * jnp.sign(den))
        dp = lax.dot_general(dn, v, _DN["nt"], preferred_element_type=F32) + dden
        rmat = dp * p
        dgrow_ref[pl.ds(h, 1), :] += jnp.sum(rmat, axis=0, keepdims=True)
        ds = (dp * a * QK_SCALE).astype(BF16)
        dq_ref[...] = jnp.dot(ds, k, preferred_element_type=F32).astype(BF16)
        dkt_acc[...] += lax.dot_general(q, ds, _DN["tn"], preferred_element_type=F32)
        dvt_acc[...] += lax.dot_general(dn, p.astype(BF16), _DN["tn"], preferred_element_type=F32)
        lane = lax.broadcasted_iota(jnp.int32, (bq, LANE), 1)
        r0 = pl.multiple_of(i * bq, 16)
        dfx_ref[pl.ds(r0, bq), :] += jnp.where(lane == h, jnp.sum(rmat, axis=1, keepdims=True), 0.0)

        @pl.when(i == nq - 1)
        def _():
            dk_ref[...] = dkt_acc[...].T.astype(BF16)
            dv_ref[...] = dvt_acc[...].T.astype(BF16)

    return pl.pallas_call(
        body, name=name, grid=(HEADS, nq),
        in_specs=[pl.BlockSpec((bq, DQK), lambda h, i: (i, h)),
                  pl.BlockSpec((lp, DQK), lambda h, i: (0, HEADS + h)),
                  pl.BlockSpec((lp, DV), lambda h, i: (0, HEADS + h)),
                  pl.BlockSpec((8, lp), lambda h, i: (0, 0)),
                  pl.BlockSpec((bq, LANE), lambda h, i: (i, 0)),
                  pl.BlockSpec((bq, LANE), lambda h, i: (i, 0)),
                  pl.BlockSpec((bq, DV), lambda h, i: (i, h)),
                  pl.BlockSpec((bq, DV), lambda h, i: (i, h))],
        out_specs=[pl.BlockSpec((bq, DQK), lambda h, i: (i, h)),
                   pl.BlockSpec((lp, DQK), lambda h, i: (0, h)),
                   pl.BlockSpec((lp, DV), lambda h, i: (0, h)),
                   pl.BlockSpec((LANE, lp), lambda h, i: (0, 0)),
                   pl.BlockSpec((lp, LANE), lambda h, i: (0, 0))],
        out_shape=[_sds((lp, QKW), BF16), _sds((lp, QKW), BF16), _sds((lp, MW), BF16),
                   _sds((LANE, lp), F32), _sds((lp, LANE), F32)],
        scratch_shapes=[pltpu.VMEM((DQK, lp), F32), pltpu.VMEM((DV, lp), F32)],
        compiler_params=_cp(("arbitrary", "arbitrary")))(qkv, qkv, qkv, grow, mcol_all, ccol_all, ht, dht)


def _gate_bwd(name, gates_pre, bias, dgrow, dfx):
    lp = gates_pre.shape[0]

    def body(p_ref, b_ref, dgrow_ref, dfx_ref, dg_ref, dgb_ref, db_ref):
        pre = p_ref[...] + b_ref[...]
        lane = lax.broadcasted_iota(jnp.int32, (lp, LANE), 1)
        row = lax.broadcasted_iota(jnp.int32, (lp, LANE), 0)
        th = jnp.tanh(pre * (1.0 / GATE_CAP))
        t = GATE_CAP * th
        dgc = jnp.where(lane < HEADS, dgrow_ref[...].T, 0.0)
        df = jnp.where(lane < HEADS, dfx_ref[...] - dgc, 0.0)
        for d in _scan_steps(lp):
            df = df + jnp.where(row < lp - d, pltpu.roll(df, lp - d, axis=0), 0.0)
        dlf = pltpu.roll(df, HEADS, axis=1)
        dt = jnp.where(lane < HEADS, dgc, dlf * jax.nn.sigmoid(-t))
        dpre = jnp.where(lane < 2 * HEADS, dt * (1.0 - th * th), 0.0)
        dg_ref[...] = dpre
        dgb_ref[...] = dpre.astype(BF16)
        db_ref[...] = jnp.broadcast_to(jnp.sum(dpre, axis=0, keepdims=True), db_ref.shape)

    full = pl.BlockSpec((lp, LANE), lambda: (0, 0))
    return pl.pallas_call(
        body, name=name,
        in_specs=[full, pl.BlockSpec((1, LANE), lambda: (0, 0)), pl.BlockSpec((LANE, lp), lambda: (0, 0)), full],
        out_specs=[full, full, pl.BlockSpec((8, LANE), lambda: (0, 0))],
        out_shape=[_sds((lp, LANE), F32), _sds((lp, LANE), BF16), _sds((8, LANE), F32)],
        compiler_params=pltpu.CompilerParams(vmem_limit_bytes=VMEM_LIMIT))(gates_pre, bias, dgrow, dfx)


CB = 256


def _e_specs(lp):
    return [pl.BlockSpec((None, lp, CB), functools.partial(lambda c, j: (c, 0, j), c)) for c in range(4)]


def _mix_fwd(name, ht, e, mnw, cw):
    lp = ht.shape[0]

    def body(ht_ref, og_ref, u_ref, gb_ref, gc_ref, mnw_ref, cw_ref, o_ref):
        x = ht_ref[...]
        r = lax.rsqrt(jnp.mean(x * x, axis=1, keepdims=True) + EPS)
        o_ref[0] = (jax.nn.sigmoid(og_ref[...].astype(F32)) * (x * r * mnw_ref[...])).astype(BF16)
        row = lax.broadcasted_iota(jnp.int32, (lp, CB), 0)
        a = gc_ref[...].astype(F32) * u_ref[...].astype(F32)
        conv = cw_ref[2:3, :] * a + cw_ref[1:2, :] * _shift_rows(a, 1, row) + cw_ref[0:1, :] * _shift_rows(a, 2, row)
        o_ref[1] = (gb_ref[...].astype(F32) * conv).astype(BF16)

    col = pl.BlockSpec((lp, CB), lambda j: (0, j))
    return pl.pallas_call(
        body, name=name, grid=(4,),
        in_specs=[col] + _e_specs(lp) + [pl.BlockSpec((1, CB), lambda j: (0, j)), pl.BlockSpec((8, CB), lambda j: (0, j))],
        out_specs=pl.BlockSpec((2, lp, CB), lambda j: (0, 0, j)), out_shape=_sds((2, lp, MW), BF16),
        compiler_params=_cp(("parallel",)))(ht, e, e, e, e, mnw, cw)


def _mix_bwd(name, dmix, ht, e, mnw, cw):
    lp = ht.shape[0]

    def body(dhm_ref, dhc_ref, ht_ref, og_ref, u_ref, gb_ref, gc_ref, mnw_ref, cw_ref,
             dht_ref, de_ref, dmnw_ref, dcw_ref):
        x = ht_ref[...]
        r = lax.rsqrt(jnp.mean(x * x, axis=1, keepdims=True) + EPS)
        xh = x * r
        w = mnw_ref[...]
        sg = jax.nn.sigmoid(og_ref[...].astype(F32))
        dhm = dhm_ref[...]
        de_ref[0] = (dhm * (xh * w) * (sg * (1.0 - sg))).astype(BF16)
        dn = dhm * sg
        dmnw_ref[...] = jnp.broadcast_to(jnp.sum(dn * xh, axis=0, keepdims=True), dmnw_ref.shape)
        dxh = dn * w
        dht_ref[...] = r * (dxh - xh * jnp.mean(dxh * xh, axis=1, keepdims=True))

        row = lax.broadcasted_iota(jnp.int32, (lp, CB), 0)
        uv = u_ref[...].astype(F32)
        gcv = gc_ref[...].astype(F32)
        gbv = gb_ref[...].astype(F32)
        a = gcv * uv
        a1 = _shift_rows(a, 1, row)
        a2 = _shift_rows(a, 2, row)
        dhc = dhc_ref[...]
        conv = cw_ref[2:3, :] * a + cw_ref[1:2, :] * a1 + cw_ref[0:1, :] * a2
        de_ref[2] = (dhc * conv).astype(BF16)
        dconv = dhc * gbv
        dcw_ref[...] = jnp.zeros_like(dcw_ref)
        dcw_ref[0:1, :] = jnp.sum(dconv * a2, axis=0, keepdims=True)
        dcw_ref[1:2, :] = jnp.sum(dconv * a1, axis=0, keepdims=True)
        dcw_ref[2:3, :] = jnp.sum(dconv * a, axis=0, keepdims=True)
        up1 = jnp.where(row < lp - 1, pltpu.roll(dconv, lp - 1, axis=0), 0.0)
        up2 = jnp.where(row < lp - 2, pltpu.roll(dconv, lp - 2, axis=0), 0.0)
        da = cw_ref[2:3, :] * dconv + cw_ref[1:2, :] * up1 + cw_ref[0:1, :] * up2
        de_ref[1] = (da * gcv).astype(BF16)
        de_ref[3] = (da * uv).astype(BF16)

    col = pl.BlockSpec((lp, CB), lambda j: (0, j))
    small = pl.BlockSpec((8, CB), lambda j: (0, j))
    return pl.pallas_call(
        body, name=name, grid=(4,),
        in_specs=[col, pl.BlockSpec((lp, CB), lambda j: (0, 4 + j)), col] + _e_specs(lp)
                 + [pl.BlockSpec((1, CB), lambda j: (0, j)), small],
        out_specs=[col, pl.BlockSpec((4, lp, CB), lambda j: (0, 0, j)), small, small],
        out_shape=[_sds((lp, MW), F32), _sds((4, lp, MW), BF16), _sds((8, MW), F32), _sds((8, CW), F32)],
        compiler_params=_cp(("parallel",)))(dmix, dmix, ht, e, e, e, e, mnw, cw)


def _row_tile(r, c, itemsize, budget=1536 * 1024, mult=16):
    best = None
    for t in range(mult, r + 1, mult):
        if r % t == 0 and t * c * itemsize <= budget:
            best = t
    if best is None:
        best = r
    return best


def _cast_bf16(name, w):
    _, r, c = w.shape
    tr = _row_tile(r, c, 4)

    def body(x_ref, o_ref):
        o_ref[...] = x_ref[...].astype(BF16)

    spec = pl.BlockSpec((None, tr, c), lambda l, i: (l, i, 0))
    return pl.pallas_call(body, name=name, grid=(DEPTH, r // tr), in_specs=[spec], out_specs=spec,
                          out_shape=_sds(w.shape, BF16), compiler_params=_cp(("parallel", "parallel")))(w)


def _add2_bf16(name, a, b):
    r, c = a.shape
    tr = _row_tile(r, c, 4)

    def body(a_ref, b_ref, o_ref):
        o_ref[...] = (a_ref[...].astype(F32) + b_ref[...].astype(F32)).astype(BF16)

    spec = pl.BlockSpec((tr, c), lambda i: (i, 0))
    return pl.pallas_call(body, name=name, grid=(r // tr,), in_specs=[spec, spec], out_specs=spec,
                          out_shape=_sds((r, c), BF16), compiler_params=_cp(("parallel",)))(a, b)


def _add4_f32(name, mine, recv):
    r, c = mine.shape
    tr = _row_tile(r, c, 4)

    def body(m_ref, r_ref, o_ref):
        s = m_ref[...].astype(F32)
        for k in range(3):
            s = s + r_ref[k].astype(F32)
        o_ref[...] = s

    spec = pl.BlockSpec((tr, c), lambda i: (i, 0))
    return pl.pallas_call(body, name=name, grid=(r // tr,),
                          in_specs=[spec, pl.BlockSpec((3, tr, c), lambda i: (0, i, 0))], out_specs=spec,
                          out_shape=_sds((r, c), F32), compiler_params=_cp(("parallel",)))(mine, recv)


def _adam_math(w, g, m, v):
    m2 = ADAM_B1 * m + (1.0 - ADAM_B1) * g
    v2 = ADAM_B2 * v + (1.0 - ADAM_B2) * (g * g)
    m_hat = m2 / (1.0 - ADAM_B1 ** ADAM_STEP)
    v_hat = v2 / (1.0 - ADAM_B2 ** ADAM_STEP)
    delta = -ADAM_LR * (m_hat / (jnp.sqrt(v_hat) + ADAM_EPS) + ADAM_WD * w)
    return delta, m2, v2


def _adamw_layer(name, layer, g, w, m, v, prev):
    _, r, c = w.shape
    tr = _row_tile(r, c, 4, budget=1024 * 1024, mult=8)
    n_alias = 0 if prev is None else 4

    def body(*refs):
        g_ref, w_ref, m_ref, v_ref = refs[:4]
        go_ref, d_ref, mo_ref, vo_ref = refs[4 + n_alias:]
        gv = g_ref[...]
        delta, m2, v2 = _adam_math(w_ref[...], gv, m_ref[...], v_ref[...])
        go_ref[...] = gv
        d_ref[...] = delta
        mo_ref[...] = m2
        vo_ref[...] = v2

    slab = pl.BlockSpec((None, tr, c), lambda i: (layer, i, 0))
    ins = [g, w, m, v] + (list(prev) if prev is not None else [])
    in_specs = [pl.BlockSpec((tr, c), lambda i: (i, 0)), slab, slab, slab] + [ANY] * n_alias
    o = _sds(w.shape, F32)
    return pl.pallas_call(
        body, name=name, grid=(r // tr,), in_specs=in_specs, out_specs=[slab] * 4, out_shape=[o] * 4,
        input_output_aliases={4 + k: k for k in range(n_alias)},
        compiler_params=_cp(("parallel",)))(*ins)


def _adamw_flat(g, w, m, v):
    def body(g_ref, w_ref, m_ref, v_ref, d_ref, mo_ref, vo_ref):
        delta, m2, v2 = _adam_math(w_ref[...], g_ref[...], m_ref[...], v_ref[...])
        d_ref[...] = delta
        mo_ref[...] = m2
        vo_ref[...] = v2

    o = _sds(w.shape, F32)
    return pl.pallas_call(body, name="adamw_small", out_shape=[o, o, o])(g, w, m, v)


def _place():
    x, y, c = lax.axis_index("x"), lax.axis_index("y"), lax.axis_index("c")
    chips = [(1 - x, y), (x, 1 - y), (1 - x, 1 - y)]
    return x, y, c, chips


def _allgather(shards, smalls):
    nb, ns = len(shards), len(smalls)

    def body(*refs):
        ins = refs[:nb + ns]
        outs = refs[nb + ns:2 * (nb + ns)]
        send, recv, fsend, frecv, ssend, srecv, lsem = refs[2 * (nb + ns):]
        x, y, c, chips = _place()
        j = 2 * x + y
        locals_, sends, fwds = [], [], []
        for t in range(nb):
            half = ins[t].shape[1] // 2
            rows = pl.ds(c * half, half)
            cp = pltpu.make_async_copy(ins[t], outs[t].at[:, j], lsem.at[t])
            cp.start()
            locals_.append(cp)
            for k, chip in enumerate(chips):
                cp = pltpu.make_async_remote_copy(
                    src_ref=ins[t].at[:, rows, :], dst_ref=outs[t].at[:, j, rows, :],
                    send_sem=send.at[3 * t + k], recv_sem=recv.at[3 * t + k],
                    device_id=(*chip, c), device_id_type=MESH)
                cp.start()
                sends.append(cp)
        for t in range(ns):
            cp = pltpu.make_async_copy(ins[nb + t], outs[nb + t].at[j], lsem.at[nb + t])
            cp.start()
            locals_.append(cp)
            for k, chip in enumerate(chips):
                cp = pltpu.make_async_remote_copy(
                    src_ref=ins[nb + t], dst_ref=outs[nb + t].at[j],
                    send_sem=ssend.at[3 * t + k], recv_sem=srecv.at[3 * t + k],
                    device_id=(*chip, c), device_id_type=MESH)
                cp.start()
                sends.append(cp)
        for t in range(nb):
            half = ins[t].shape[1] // 2
            rows = pl.ds(c * half, half)
            for k, chip in enumerate(chips):
                jk = 2 * chip[0] + chip[1]
                landed = outs[t].at[:, jk, rows, :]
                pltpu.make_async_remote_copy(
                    src_ref=landed, dst_ref=landed, send_sem=send.at[3 * t + k], recv_sem=recv.at[3 * t + k],
                    device_id=(*chip, c), device_id_type=MESH).wait_recv()
                cp = pltpu.make_async_remote_copy(
                    src_ref=landed, dst_ref=landed, send_sem=fsend.at[3 * t + k], recv_sem=frecv.at[3 * t + k],
                    device_id=(x, y, 1 - c), device_id_type=MESH)
                cp.start()
                fwds.append(cp)
        for t in range(ns):
            for k, chip in enumerate(chips):
                jk = 2 * chip[0] + chip[1]
                landed = outs[nb + t].at[jk]
                pltpu.make_async_remote_copy(
                    src_ref=landed, dst_ref=landed, send_sem=ssend.at[3 * t + k], recv_sem=srecv.at[3 * t + k],
                    device_id=(*chip, c), device_id_type=MESH).wait_recv()
        for t in range(nb):
            half = ins[t].shape[1] // 2
            other = pl.ds((1 - c) * half, half)
            for k, chip in enumerate(chips):
                jk = 2 * chip[0] + chip[1]
                landed = outs[t].at[:, jk, other, :]
                pltpu.make_async_remote_copy(
                    src_ref=landed, dst_ref=landed, send_sem=fsend.at[3 * t + k], recv_sem=frecv.at[3 * t + k],
                    device_id=(x, y, 1 - c), device_id_type=MESH).wait_recv()
        for cp in sends + fwds:
            cp.wait_send()
        for cp in locals_:
            cp.wait()

    out_shape = ([_sds((DEPTH, NCHIP) + s.shape[1:], s.dtype) for s in shards]
                 + [_sds((NCHIP,) + s.shape, s.dtype) for s in smalls])
    dma = pltpu.SemaphoreType.DMA
    return pl.pallas_call(
        body, name="allgather_weights", in_specs=[ANY] * (nb + ns), out_specs=[ANY] * (nb + ns),
        out_shape=out_shape,
        scratch_shapes=[dma((3 * nb,)), dma((3 * nb,)), dma((3 * nb,)), dma((3 * nb,)),
                        dma((3 * ns,)), dma((3 * ns,)), dma((nb + ns,))],
        compiler_params=pltpu.CompilerParams(has_side_effects=True))(*shards, *smalls)


def _rs_pair(name, dws):
    n = len(dws)

    def body(*refs):
        ins = refs[:n]
        mine = refs[n:2 * n]
        got = refs[2 * n:3 * n]
        send, recv, lsem = refs[3 * n:]
        x, y, c, _ = _place()
        cps, lcs = [], []
        for t in range(n):
            half = ins[t].shape[1] // 2
            lc = pltpu.make_async_copy(ins[t].at[:, pl.ds(c * half, half), :], mine[t], lsem.at[t])
            lc.start()
            lcs.append(lc)
            cp = pltpu.make_async_remote_copy(
                src_ref=ins[t].at[:, pl.ds((1 - c) * half, half), :], dst_ref=got[t],
                send_sem=send.at[t], recv_sem=recv.at[t], device_id=(x, y, 1 - c), device_id_type=MESH)
            cp.start()
            cps.append(cp)
        for cp in cps:
            cp.wait()
        for lc in lcs:
            lc.wait()

    halves = [_sds((NCHIP, d.shape[1] // 2, d.shape[2]), BF16) for d in dws]
    dma = pltpu.SemaphoreType.DMA
    outs = pl.pallas_call(
        body, name=name, in_specs=[ANY] * n, out_specs=[ANY] * (2 * n), out_shape=halves + halves,
        scratch_shapes=[dma((n,)), dma((n,)), dma((n,))],
        compiler_params=pltpu.CompilerParams(has_side_effects=True))(*dws)
    return outs[:n], outs[n:]


def _rs_chips(name, ps):
    n = len(ps)

    def body(*refs):
        ins = refs[:n]
        mine = refs[n:2 * n]
        got = refs[2 * n:3 * n]
        send, recv, lsem = refs[3 * n:]
        x, y, c, chips = _place()
        j = 2 * x + y
        cps, lcs = [], []
        for t in range(n):
            lc = pltpu.make_async_copy(ins[t].at[j], mine[t], lsem.at[t])
            lc.start()
            lcs.append(lc)
            for k, chip in enumerate(chips):
                jk = 2 * chip[0] + chip[1]
                cp = pltpu.make_async_remote_copy(
                    src_ref=ins[t].at[jk], dst_ref=got[t].at[k],
                    send_sem=send.at[3 * t + k], recv_sem=recv.at[3 * t + k],
                    device_id=(*chip, c), device_id_type=MESH)
                cp.start()
                cps.append(cp)
        for cp in cps:
            cp.wait()
        for lc in lcs:
            lc.wait()

    dma = pltpu.SemaphoreType.DMA
    outs = pl.pallas_call(
        body, name=name, in_specs=[ANY] * n, out_specs=[ANY] * (2 * n),
        out_shape=[_sds(p.shape[1:], BF16) for p in ps] + [_sds((3,) + p.shape[1:], BF16) for p in ps],
        scratch_shapes=[dma((3 * n,)), dma((3 * n,)), dma((n,))],
        compiler_params=pltpu.CompilerParams(has_side_effects=True))(*ps)
    return outs[:n], outs[n:]


def _rs_join(name, halves):
    n = len(halves)

    def body(*refs):
        ins = refs[:n]
        outs = refs[n:2 * n]
        send, recv, lsem = refs[2 * n:]
        x, y, c, _ = _place()
        cps, lcs = [], []
        for t in range(n):
            r2 = ins[t].shape[0]
            rows = pl.ds(c * r2, r2)
            lc = pltpu.make_async_copy(ins[t], outs[t].at[rows, :], lsem.at[t])
            lc.start()
            lcs.append(lc)
            cp = pltpu.make_async_remote_copy(
                src_ref=ins[t], dst_ref=outs[t].at[rows, :], send_sem=send.at[t], recv_sem=recv.at[t],
                device_id=(x, y, 1 - c), device_id_type=MESH)
            cp.start()
            cps.append(cp)
        for t, cp in enumerate(cps):
            cp.wait_send()
            r2 = ins[t].shape[0]
            theirs = outs[t].at[pl.ds((1 - c) * r2, r2), :]
            pltpu.make_async_remote_copy(
                src_ref=theirs, dst_ref=theirs, send_sem=send.at[t], recv_sem=recv.at[t],
                device_id=(x, y, 1 - c), device_id_type=MESH).wait_recv()
        for lc in lcs:
            lc.wait()

    dma = pltpu.SemaphoreType.DMA
    return pl.pallas_call(
        body, name=name, in_specs=[ANY] * n, out_specs=[ANY] * n,
        out_shape=[_sds((2 * h.shape[0], h.shape[1]), F32) for h in halves],
        scratch_shapes=[dma((n,)), dma((n,)), dma((n,))],
        compiler_params=pltpu.CompilerParams(has_side_effects=True))(*halves)


def _allreduce_small(pack):
    r = pack.shape[0]
    flips = [(fx, fy, fc) for fx in (0, 1) for fy in (0, 1) for fc in (0, 1)][1:]

    def body(p_ref, o_ref, gat, send, recv):
        x, y, c, _ = _place()
        me = 4 * x + 2 * y + c
        gat[me] = p_ref[...]
        cps = []
        for k, (fx, fy, fc) in enumerate(flips):
            peer = ((1 - x) if fx else x, (1 - y) if fy else y, (1 - c) if fc else c)
            cp = pltpu.make_async_remote_copy(
                src_ref=p_ref, dst_ref=gat.at[me], send_sem=send.at[k], recv_sem=recv.at[k],
                device_id=peer, device_id_type=MESH)
            cp.start()
            cps.append(cp)
        for k, (fx, fy, fc) in enumerate(flips):
            peer = ((1 - x) if fx else x, (1 - y) if fy else y, (1 - c) if fc else c)
            src = 4 * peer[0] + 2 * peer[1] + peer[2]
            pltpu.make_async_remote_copy(
                src_ref=p_ref, dst_ref=gat.at[src], send_sem=send.at[k], recv_sem=recv.at[k],
                device_id=peer, device_id_type=MESH).wait_recv()
        for cp in cps:
            cp.wait_send()
        s = gat[0]
        for d in range(1, 8):
            s = s + gat[d]
        o_ref[...] = s

    dma = pltpu.SemaphoreType.DMA
    vm = pl.BlockSpec(memory_space=pltpu.VMEM)
    return pl.pallas_call(
        body, name="allreduce_small", in_specs=[vm], out_specs=vm, out_shape=_sds((r, LANE), F32),
        scratch_shapes=[pltpu.VMEM((8, r, LANE), F32), dma((7,)), dma((7,))],
        compiler_params=pltpu.CompilerParams(has_side_effects=True))(pack)


def _reduce_scatter(tag, dws):
    mine, got = _rs_pair("rs_pair_" + tag, dws)
    ps = []
    for t, (a, b) in enumerate(zip(mine, got)):
        n4, r2, c = a.shape
        ps.append(_add2_bf16("rs_add2_%s_%d" % (tag, t), a.reshape(n4 * r2, c), b.reshape(n4 * r2, c)).reshape(n4, r2, c))
    mine2, got2 = _rs_chips("rs_chips_" + tag, ps)
    halves = [_add4_f32("rs_add4_%s_%d" % (tag, t), a, b) for t, (a, b) in enumerate(zip(mine2, got2))]
    return _rs_join("rs_join_" + tag, halves)


def _in_weights(win_g, l):
    full = jnp.concatenate([win_g[l, s] for s in range(NCHIP)], axis=1)
    wqkv = full[:, :2048]
    og = full[:, 2048:3072]
    gates = jnp.pad(full[:, 3072:3080], ((0, 0), (0, LANE - 8)))
    u = full[:, 3080:4104]
    gb = full[:, 4104:5128]
    gc = full[:, 5128:6152]
    return wqkv, jnp.stack([og, u, gb, gc]), gates


def _in_grads(dwqkv, dwe, dwgt):
    full = jnp.concatenate([dwqkv, dwe[0], dwgt[:, :8], dwe[1], dwe[2], dwe[3]], axis=1)
    sw = DIN // NCHIP
    return jnp.stack([full[:, s * sw:(s + 1) * sw] for s in range(NCHIP)])


def _layer_fwd(l, h, wts, small):
    lp = h.shape[0]
    th = lp // 2
    wqkv, we, wgt, wout_g, wg_g, wu_g, wd_g = wts
    nmw, bias, mnw, cw, nfw = small
    tag = "_l%d" % l
    hn = _norm_fwd("norm_mix" + tag, h, nmw)
    qkv = _mm("proj_qkv" + tag, "nn", hn, wqkv,
              pl.BlockSpec((lp, D), lambda i, j, k: (0, 0)), pl.BlockSpec((D, 512), lambda i, j, k: (0, j)),
              pl.BlockSpec((lp, 512), lambda i, j, k: (0, j)), _sds((lp, 2048), BF16), (1, 4, 1))
    e = _mm("proj_e" + tag, "nn", hn, we,
            pl.BlockSpec((lp, D), lambda i, j, k: (0, 0)), pl.BlockSpec((None, D, 512), lambda i, j, k: (j // 2, 0, j % 2)),
            pl.BlockSpec((None, lp, 512), lambda i, j, k: (j // 2, 0, j % 2)), _sds((4, lp, 1024), BF16), (1, 8, 1))
    gpre = _mm("proj_gates" + tag, "nn", hn, wgt,
               pl.BlockSpec((lp, D), lambda i, j, k: (0, 0)), pl.BlockSpec((D, LANE), lambda i, j, k: (0, 0)),
               pl.BlockSpec((lp, LANE), lambda i, j, k: (0, 0)), _sds((lp, LANE), F32), (1, 1, 1))
    grow, mcol, ccol = _gate_prep("gate_prep" + tag, gpre, bias)
    ht = _mlstm_fwd("mlstm_fwd" + tag, qkv, grow, mcol, ccol, 4)
    mix = _mix_fwd("mix_fwd" + tag, ht, e, mnw, cw)
    wout = wout_g.reshape(DEPTH, D, D)
    h1 = _mm("out_proj" + tag, "nn", mix, wout,
             pl.BlockSpec((None, th, 1024), lambda i, j, k: (k, i, 0)),
             pl.BlockSpec((None, 1024, 1024), lambda i, j, k: (l, k, j)),
             pl.BlockSpec((th, 1024), lambda i, j, k: (i, j)), _sds((lp, D), F32), (2, 2, 2),
             acc_shape=(th, 1024), res=h, res_spec=pl.BlockSpec((th, 1024), lambda i, j, k: (i, j)))
    hf = _norm_fwd("norm_ffn" + tag, h1, nfw)
    g, u, a = _ffn_up("ffn_up" + tag, hf, wg_g, wu_g, l)
    tk = DFF // NCHIP
    h2 = _mm("ffn_down" + tag, "nn", a, wd_g,
             pl.BlockSpec((th, tk), lambda i, j, k: (i, k)),
             pl.BlockSpec((None, None, tk, 1024), lambda i, j, k: (l, k, 0, j)),
             pl.BlockSpec((th, 1024), lambda i, j, k: (i, j)), _sds((lp, D), F32), (2, 2, NCHIP),
             acc_shape=(th, 1024), res=h1, res_spec=pl.BlockSpec((th, 1024), lambda i, j, k: (i, j)))
    saved = (h, hn, qkv, e, gpre, grow, mcol, ccol, ht, mix, h1, hf, g, u, a)
    return h2, saved


def _layer_bwd(l, dh2, dh2b, saved, wts, small):
    h, hn, qkv, e, gpre, grow, mcol, ccol, ht, mix, h1, hf, g, u, a = saved
    wqkv, we, wgt, wout_g, wg_g, wu_g, wd_g = wts
    nmw, bias, mnw, cw, nfw = small
    lp = h.shape[0]
    th = lp // 2
    tk = DFF // NCHIP
    tag = "_l%d" % l
    half_rows = lambda i, j, k: (i, j)

    dwd = _mm("dw_down" + tag, "tn", a, dh2b,
              pl.BlockSpec((lp, tk), lambda i, j, k: (0, i)), pl.BlockSpec((lp, 1024), lambda i, j, k: (0, j)),
              pl.BlockSpec((None, tk, 1024), lambda i, j, k: (i, 0, j)), _sds((NCHIP, tk, D), BF16), (NCHIP, 2, 1))
    dg, du = _ffn_bwd_act("ffn_bwd_act" + tag, dh2b, wd_g, g, u, l)
    dws = []
    for nm, dact in (("gate", dg), ("up", du)):
        dws.append(_mm("dw_%s%s" % (nm, tag), "tn", hf, dact,
                       pl.BlockSpec((lp, 1024), lambda i, j, k: (0, i)), pl.BlockSpec((lp, tk), lambda i, j, k: (0, j)),
                       pl.BlockSpec((None, 1024, tk), lambda i, j, k: (j, i, 0)), _sds((NCHIP, D, tk), BF16),
                       (2, NCHIP, 1)))
    dwg, dwu = dws
    dhf = None
    for nm, dact, wfull in (("gate", dg, wg_g), ("up", du, wu_g)):
        dhf = _mm("dhf_%s%s" % (nm, tag), "nt", dact, wfull,
                  pl.BlockSpec((th, tk), lambda i, j, k: (i, k)),
                  pl.BlockSpec((None, None, 1024, tk), lambda i, j, k: (l, k, j, 0)),
                  pl.BlockSpec((th, 1024), half_rows), _sds((lp, D), F32), (2, 2, NCHIP), acc_shape=(th, 1024),
                  res=dhf, res_spec=None if dhf is None else pl.BlockSpec((th, 1024), half_rows))
    dh1, dh1b, dnfw = _norm_bwd("norm_ffn_bwd" + tag, dhf, h1, nfw, dh2)

    dwout = _mm("dw_out" + tag, "tn", mix, dh1b,
                pl.BlockSpec((None, lp, 1024), lambda i, j, k: (i, 0, 0)), pl.BlockSpec((lp, 1024), lambda i, j, k: (0, j)),
                pl.BlockSpec((1024, 1024), half_rows), _sds((D, D), BF16), (2, 2, 1))
    wout = wout_g.reshape(DEPTH, D, D)
    dmix = _mm("dmix" + tag, "nt", dh1b, wout,
               pl.BlockSpec((th, D), lambda i, j, k: (i, 0)), pl.BlockSpec((None, 1024, D), lambda i, j, k: (l, j, 0)),
               pl.BlockSpec((th, 1024), half_rows), _sds((lp, D), F32), (2, 2, 1))
    dht, de, dmnw, dcw = _mix_bwd("mix_bwd" + tag, dmix, ht, e, mnw, cw)
    dq, dk, dv, dgrow, dfx = _mlstm_bwd("mlstm_bwd" + tag, qkv, grow, mcol, ccol, ht, dht, 8)
    dgp, dgpb, dbias = _gate_bwd("gate_bwd" + tag, gpre, bias, dgrow, dfx)
    del dgp
    dqkv = jnp.concatenate([dq, dk, dv], axis=1)

    hn_cols = pl.BlockSpec((lp, 1024), lambda i, j, k: (0, i))
    dwqkv = _mm("dw_qkv" + tag, "tn", hn, dqkv, hn_cols, pl.BlockSpec((lp, 1024), lambda i, j, k: (0, j)),
                pl.BlockSpec((1024, 1024), half_rows), _sds((D, 2048), BF16), (2, 2, 1))
    dwe = _mm("dw_e" + tag, "tn", hn, de, hn_cols, pl.BlockSpec((None, lp, 1024), lambda i, j, k: (j, 0, 0)),
              pl.BlockSpec((None, 1024, 1024), lambda i, j, k: (j, i, 0)), _sds((4, D, 1024), BF16), (2, 4, 1))
    dwgt = _mm("dw_gates" + tag, "tn", hn, dgpb, hn_cols, pl.BlockSpec((lp, LANE), lambda i, j, k: (0, 0)),
               pl.BlockSpec((1024, LANE), lambda i, j, k: (i, 0)), _sds((D, LANE), BF16), (2, 1, 1))
    dwin = _in_grads(dwqkv, dwe, dwgt)

    dhn = _mm("dhn_qkv" + tag, "nt", dqkv, wqkv,
              pl.BlockSpec((th, 2048), lambda i, j, k: (i, 0)), pl.BlockSpec((1024, 2048), lambda i, j, k: (j, 0)),
              pl.BlockSpec((th, 1024), half_rows), _sds((lp, D), F32), (2, 2, 1))
    dhn = _mm("dhn_e" + tag, "nt", de, we,
              pl.BlockSpec((None, th, 1024), lambda i, j, k: (k, i, 0)),
              pl.BlockSpec((None, 1024, 1024), lambda i, j, k: (k, j, 0)),
              pl.BlockSpec((th, 1024), half_rows), _sds((lp, D), F32), (2, 2, 4), acc_shape=(th, 1024),
              res=dhn, res_spec=pl.BlockSpec((th, 1024), half_rows))
    dhn = _mm("dhn_gates" + tag, "nt", dgpb, wgt,
              pl.BlockSpec((th, LANE), lambda i, j, k: (i, 0)), pl.BlockSpec((1024, LANE), lambda i, j, k: (j, 0)),
              pl.BlockSpec((th, 1024), half_rows), _sds((lp, D), F32), (2, 2, 1),
              res=dhn, res_spec=pl.BlockSpec((th, 1024), half_rows))
    dh0, dh0b, dnmw = _norm_bwd("norm_mix_bwd" + tag, dhn, h, nmw, dh1)

    pieces = [dwin, dwout.reshape(NCHIP, D // NCHIP, D), dwg, dwu, dwd]
    smalls = (dnmw[0], dbias[0, :8], dcw[:3], dmnw[0], dnfw[0])
    return dh0, dh0b, pieces, smalls


def _pack_rows(parts):
    rows = []
    for p in parts:
        f = p.reshape(-1)
        pad = (-f.shape[0]) % LANE
        if pad:
            f = jnp.pad(f, (0, pad))
        rows.append(f.reshape(-1, LANE))
    r = jnp.concatenate(rows, axis=0)
    pad = (-r.shape[0]) % 8
    if pad:
        r = jnp.pad(r, ((0, pad), (0, 0)))
    return r


def _unpack_rows(pack, shapes):
    out, r0 = [], 0
    for s in shapes:
        n = 1
        for d in s:
            n *= d
        nr = -(-n // LANE)
        out.append(pack[r0:r0 + nr].reshape(-1)[:n].reshape(s))
        r0 += nr
    return out


def kernel(x, meta_tokens, norm_mix_w, w_in, b_gates, conv_w, mlstm_norm_w, w_out, norm_ffn_w, w_gate, w_up, w_down, norm_final_w, loss_target, m_meta_tokens, m_norm_mix_w, m_w_in, m_b_gates, m_conv_w, m_mlstm_norm_w, m_w_out, m_norm_ffn_w, m_w_gate, m_w_up, m_w_down, m_norm_final_w, v_meta_tokens, v_norm_mix_w, v_w_in, v_b_gates, v_conv_w, v_mlstm_norm_w, v_w_out, v_norm_ffn_w, v_w_gate, v_w_up, v_w_down, v_norm_final_w):
    seq = x.shape[1]
    n_real = N_META + seq
    lp = -(-n_real // LANE) * LANE
    jchip = 2 * lax.axis_index("x") + lax.axis_index("y")

    big = {"w_in": w_in, "w_out": w_out, "w_gate": w_gate, "w_up": w_up, "w_down": w_down}
    casts = [_cast_bf16("cast_" + n, w) for n, w in big.items()]
    conv_flat = jnp.pad(conv_w.reshape(DEPTH * 3, CW // NCHIP), ((0, 8 - DEPTH * 3), (0, 0)))
    gathered = _allgather(casts, [meta_tokens, conv_flat])
    win_g, wout_g, wg_g, wu_g, wd_g, meta_g, conv_g = gathered
    meta_full = jnp.concatenate([meta_g[s] for s in range(NCHIP)], axis=1)
    conv_full = jnp.concatenate([conv_g[s][:DEPTH * 3] for s in range(NCHIP)], axis=1)
    conv_full = conv_full.reshape(DEPTH, 3, CW)

    bias_rows = jnp.pad(b_gates, ((0, 0), (0, LANE - 8)))
    wts, smalls = [], []
    for l in range(DEPTH):
        wts.append(_in_weights(win_g, l) + (wout_g, wg_g, wu_g, wd_g))
        smalls.append((norm_mix_w[l][None], bias_rows[l][None], mlstm_norm_w[l][None],
                       jnp.pad(conv_full[l], ((0, 5), (0, 0))), norm_ffn_w[l][None]))

    h = jnp.concatenate([meta_full, x[0], jnp.zeros((lp - n_real, D), F32)], axis=0)
    saved = []
    for l in range(DEPTH):
        h, sv = _layer_fwd(l, h, wts[l], smalls[l])
        saved.append(sv)
    tgt = jnp.pad(loss_target[0], ((N_META, lp - n_real), (0, 0)))
    dh, dhb, dnorm_final, loss_part = _loss_head(h, tgt, norm_final_w[None], n_real)

    names = ["w_in", "w_out", "w_gate", "w_up", "w_down"]
    params = {"w_in": (w_in, m_w_in, v_w_in), "w_out": (w_out, m_w_out, v_w_out), "w_gate": (w_gate, m_w_gate, v_w_gate),
              "w_up": (w_up, m_w_up, v_w_up), "w_down": (w_down, m_w_down, v_w_down)}
    big_out = {n: None for n in names}
    small_grads = [None] * DEPTH
    for l in reversed(range(DEPTH)):
        dh, dhb, pieces, small_grads[l] = _layer_bwd(l, dh, dhb, saved[l], wts[l], smalls[l])
        reduced = _reduce_scatter("l%d" % l, pieces)
        for n, gsum in zip(names, reduced):
            w, m, v = params[n]
            big_out[n] = _adamw_layer("adamw_%s_l%d" % (n, l), l, gsum, w, m, v, big_out[n])

    dnmw = jnp.stack([small_grads[l][0] for l in range(DEPTH)])
    dbias = jnp.stack([small_grads[l][1] for l in range(DEPTH)])
    dconv = jnp.stack([small_grads[l][2] for l in range(DEPTH)])
    dmnw = jnp.stack([small_grads[l][3] for l in range(DEPTH)])
    dnfw = jnp.stack([small_grads[l][4] for l in range(DEPTH)])
    part_shapes = [(N_META, D), (DEPTH, D), (DEPTH, 8), (DEPTH, 3, CW), (DEPTH, MW), (DEPTH, D), (D,), (LANE,)]
    pack = _pack_rows([dh[:N_META], dnmw, dbias, dconv, dmnw, dnfw, dnorm_final[0], loss_part[0]])
    tot = _unpack_rows(_allreduce_small(pack), part_shapes)
    g_meta_full, g_nmw, g_bias, g_conv_full, g_mnw, g_nfw, g_final, loss_row = tot
    mcols = D // NCHIP
    ccols = CW // NCHIP
    g_meta = lax.dynamic_slice_in_dim(g_meta_full, jchip * mcols, mcols, axis=1)
    g_conv = lax.dynamic_slice_in_dim(g_conv_full, jchip * ccols, ccols, axis=2)
    sm_g = [g_meta, g_nmw, g_bias, g_conv, g_mnw, g_nfw, g_final]
    sm_w = [meta_tokens, norm_mix_w, b_gates, conv_w, mlstm_norm_w, norm_ffn_w, norm_final_w]
    sm_m = [m_meta_tokens, m_norm_mix_w, m_b_gates, m_conv_w, m_mlstm_norm_w, m_norm_ffn_w, m_norm_final_w]
    sm_v = [v_meta_tokens, v_norm_mix_w, v_b_gates, v_conv_w, v_mlstm_norm_w, v_norm_ffn_w, v_norm_final_w]
    sm_shapes = [w.shape for w in sm_w]
    d_p, m_p, v_p = _adamw_flat(_pack_rows(sm_g), _pack_rows(sm_w), _pack_rows(sm_m), _pack_rows(sm_v))
    sm_d = _unpack_rows(d_p, sm_shapes)
    sm_nm = _unpack_rows(m_p, sm_shapes)
    sm_nv = _unpack_rows(v_p, sm_shapes)

    loss = loss_row[0]
    grad_x = dh[N_META:n_real][None]

    def ordered(sm, which):
        bo = {n: big_out[n][which] for n in names}
        return [sm[0], sm[1], bo["w_in"], sm[2], sm[3], sm[4], bo["w_out"], sm[5], bo["w_gate"], bo["w_up"], bo["w_down"], sm[6]]

    return (loss, grad_x, *ordered(sm_g, 0), *ordered(sm_d, 1), *ordered(sm_nm, 2), *ordered(sm_nv, 3))
```

```python
import functools

import jax
import jax.numpy as jnp
from jax import lax
from jax.experimental import pallas as pl
from jax.experimental.pallas import tpu as pltpu

F32 = jnp.float32
BF16 = jnp.bfloat16

D = 2048
N_META = 16
HEADS = 4
DQK = 128
DV = 256
MW = HEADS * DV
CW = D - MW
QKW = HEADS * DQK
DFF = 5632
DIN = 6152
NCHIP = 4
DEPTH = 2
GATE_CAP = 15.0
EPS = 1e-6
QK_SCALE = DQK ** -0.5
LANE = 128
VMEM_LIMIT = 60 * 1024 * 1024

ADAM_LR = 0.001
ADAM_B1 = 0.9
ADAM_B2 = 0.999
ADAM_EPS = 1e-08
ADAM_WD = 0.01
ADAM_STEP = 10

MESH = pl.DeviceIdType.MESH
ANY = pl.BlockSpec(memory_space=pl.ANY)


def _cp(sem):
    return pltpu.CompilerParams(dimension_semantics=sem, vmem_limit_bytes=VMEM_LIMIT)


def _sds(shape, dtype):
    return jax.ShapeDtypeStruct(shape, dtype)


_DN = {"nn": (((1,), (0,)), ((), ())), "nt": (((1,), (1,)), ((), ())), "tn": (((0,), (0,)), ((), ()))}


def _mm(name, kind, a, b, a_spec, b_spec, o_spec, out_shape, grid, acc_shape=None, res=None, res_spec=None):
    nk = grid[2]
    has_res = res is not None

    def body(*refs):
        if has_res:
            a_ref, b_ref, r_ref, o_ref = refs[:4]
        else:
            a_ref, b_ref, o_ref = refs[:3]
            r_ref = None
        p = lax.dot_general(a_ref[...], b_ref[...], _DN[kind], preferred_element_type=F32)
        if nk == 1:
            if r_ref is not None:
                p = p + r_ref[...]
            o_ref[...] = p.astype(o_ref.dtype)
        else:
            acc = refs[-1]
            k = pl.program_id(2)

            @pl.when(k == 0)
            def _():
                acc[...] = p

            @pl.when(k > 0)
            def _():
                acc[...] += p

            @pl.when(k == nk - 1)
            def _():
                r = acc[...]
                if r_ref is not None:
                    r = r + r_ref[...]
                o_ref[...] = r.astype(o_ref.dtype)

    ins = [a, b] + ([res] if has_res else [])
    in_specs = [a_spec, b_spec] + ([res_spec] if has_res else [])
    scratch = [pltpu.VMEM(acc_shape, F32)] if nk > 1 else []
    return pl.pallas_call(
        body, name=name, grid=grid, in_specs=in_specs, out_specs=o_spec, out_shape=out_shape,
        scratch_shapes=scratch, compiler_params=_cp(("parallel", "parallel", "arbitrary")))(*ins)


def _norm_fwd(name, h, w):
    lp = h.shape[0]
    tm = lp // 4

    def body(h_ref, w_ref, o_ref):
        x = h_ref[...]
        r = lax.rsqrt(jnp.mean(x * x, axis=1, keepdims=True) + EPS)
        o_ref[...] = (x * r * w_ref[...]).astype(BF16)

    return pl.pallas_call(
        body, name=name, grid=(4,),
        in_specs=[pl.BlockSpec((tm, D), lambda i: (i, 0)), pl.BlockSpec((1, D), lambda i: (0, 0))],
        out_specs=pl.BlockSpec((tm, D), lambda i: (i, 0)), out_shape=_sds((lp, D), BF16),
        compiler_params=_cp(("parallel",)))(h, w)


def _norm_bwd(name, dy, h, w, dres):
    lp = h.shape[0]
    tm = lp // 8

    def body(dy_ref, h_ref, w_ref, dres_ref, dh_ref, dhb_ref, dw_ref):
        x = h_ref[...]
        r = lax.rsqrt(jnp.mean(x * x, axis=1, keepdims=True) + EPS)
        xh = x * r
        dy_v = dy_ref[...]
        dxh = dy_v * w_ref[...]
        dx = r * (dxh - xh * jnp.mean(dxh * xh, axis=1, keepdims=True))
        dh = dres_ref[...] + dx
        dh_ref[...] = dh
        dhb_ref[...] = dh.astype(BF16)

        @pl.when(pl.program_id(0) == 0)
        def _():
            dw_ref[...] = jnp.zeros_like(dw_ref)

        dw_ref[0:1, :] += jnp.sum(dy_v * xh, axis=0, keepdims=True)

    row = pl.BlockSpec((tm, D), lambda i: (i, 0))
    return pl.pallas_call(
        body, name=name, grid=(8,),
        in_specs=[row, row, pl.BlockSpec((1, D), lambda i: (0, 0)), row],
        out_specs=[row, row, pl.BlockSpec((8, D), lambda i: (0, 0))],
        out_shape=[_sds((lp, D), F32), _sds((lp, D), BF16), _sds((8, D), F32)],
        compiler_params=_cp(("arbitrary",)))(dy, h, w, dres)


def _loss_head(h, tgt, w, n_real):
    lp = h.shape[0]
    tm = lp // 8

    def body(h_ref, t_ref, w_ref, dh_ref, dhb_ref, dw_ref, loss_ref):
        i = pl.program_id(0)
        x = h_ref[...]
        r = lax.rsqrt(jnp.mean(x * x, axis=1, keepdims=True) + EPS)
        xh = x * r
        wv = w_ref[...]
        row = i * tm + lax.broadcasted_iota(jnp.int32, (tm, 1), 0)
        valid = jnp.logical_and(row >= N_META, row < n_real)
        err = jnp.where(valid, xh * wv - t_ref[...], 0.0)
        dy_v = err * (1.0 / D)
        dxh = dy_v * wv
        dx = r * (dxh - xh * jnp.mean(dxh * xh, axis=1, keepdims=True))
        dh_ref[...] = dx
        dhb_ref[...] = dx.astype(BF16)

        @pl.when(i == 0)
        def _():
            dw_ref[...] = jnp.zeros_like(dw_ref)
            loss_ref[...] = jnp.zeros_like(loss_ref)

        dw_ref[0:1, :] += jnp.sum(dy_v * xh, axis=0, keepdims=True)
        part = jnp.sum(jnp.sum(err * err, axis=1, keepdims=True), axis=0, keepdims=True) * (0.5 / D)
        loss_ref[...] += jnp.broadcast_to(part, loss_ref.shape)

    row = pl.BlockSpec((tm, D), lambda i: (i, 0))
    return pl.pallas_call(
        body, name="loss_head", grid=(8,),
        in_specs=[row, row, pl.BlockSpec((1, D), lambda i: (0, 0))],
        out_specs=[row, row, pl.BlockSpec((8, D), lambda i: (0, 0)), pl.BlockSpec((8, LANE), lambda i: (0, 0))],
        out_shape=[_sds((lp, D), F32), _sds((lp, D), BF16), _sds((8, D), F32), _sds((8, LANE), F32)],
        compiler_params=_cp(("arbitrary",)))(h, tgt, w)


def _ffn_up(name, hf, wg4, wu4, layer):
    lp = hf.shape[0]
    tm = lp // 4
    tn = DFF // NCHIP

    def body(x_ref, wg_ref, wu_ref, g_ref, u_ref, a_ref):
        x = x_ref[...]
        g = jnp.dot(x, wg_ref[...], preferred_element_type=F32)
        u = jnp.dot(x, wu_ref[...], preferred_element_type=F32)
        g_ref[...] = g.astype(BF16)
        u_ref[...] = u.astype(BF16)
        a_ref[...] = (g * jax.nn.sigmoid(g) * u).astype(BF16)

    wspec = pl.BlockSpec((None, None, D, tn), lambda j, i: (layer, j, 0, 0))
    ospec = pl.BlockSpec((tm, tn), lambda j, i: (i, j))
    o = _sds((lp, DFF), BF16)
    return pl.pallas_call(
        body, name=name, grid=(NCHIP, 4),
        in_specs=[pl.BlockSpec((tm, D), lambda j, i: (i, 0)), wspec, wspec],
        out_specs=[ospec, ospec, ospec], out_shape=[o, o, o],
        compiler_params=_cp(("parallel", "parallel")))(hf, wg4, wu4)


def _ffn_bwd_act(name, dhb, wd4, g, u, layer):
    lp = dhb.shape[0]
    tm = lp // 4
    tn = DFF // NCHIP

    def body(d_ref, w_ref, g_ref, u_ref, dg_ref, du_ref):
        da = lax.dot_general(d_ref[...], w_ref[...], _DN["nt"], preferred_element_type=F32)
        gv = g_ref[...].astype(F32)
        uv = u_ref[...].astype(F32)
        sg = jax.nn.sigmoid(gv)
        dg_ref[...] = (da * uv * (sg * (1.0 + gv * (1.0 - sg)))).astype(BF16)
        du_ref[...] = (da * (gv * sg)).astype(BF16)

    ospec = pl.BlockSpec((tm, tn), lambda j, i: (i, j))
    o = _sds((lp, DFF), BF16)
    return pl.pallas_call(
        body, name=name, grid=(NCHIP, 4),
        in_specs=[pl.BlockSpec((tm, D), lambda j, i: (i, 0)),
                  pl.BlockSpec((None, None, tn, D), lambda j, i: (layer, j, 0, 0)), ospec, ospec],
        out_specs=[ospec, ospec], out_shape=[o, o],
        compiler_params=_cp(("parallel", "parallel")))(dhb, wd4, g, u)


def _shift_rows(x, d, row):
    return jnp.where(row >= d, pltpu.roll(x, d, axis=0), 0.0)


def _scan_steps(lp):
    d = 1
    while d < lp:
        yield d
        d *= 2


def _gate_values(pre):
    t = GATE_CAP * jnp.tanh(pre * (1.0 / GATE_CAP))
    lf = jnp.minimum(t, 0.0) - jnp.log(1.0 + jnp.exp(-jnp.abs(t)))
    return t, lf


def _gate_prep(name, gates_pre, bias):
    lp = gates_pre.shape[0]

    def body(p_ref, b_ref, grow_ref, m_ref, c_ref):
        pre = p_ref[...] + b_ref[...]
        lane = lax.broadcasted_iota(jnp.int32, (lp, LANE), 1)
        row = lax.broadcasted_iota(jnp.int32, (lp, LANE), 0)
        t, lf = _gate_values(pre)
        f = jnp.where(jnp.logical_and(lane >= HEADS, lane < 2 * HEADS), lf, 0.0)
        for d in _scan_steps(lp):
            f = f + _shift_rows(f, d, row)
        fs = pltpu.roll(f, LANE - HEADS, axis=1)
        g = jnp.where(lane < HEADS, t - fs, 0.0)
        m = g
        for d in _scan_steps(lp):
            m = jnp.maximum(m, jnp.where(row >= d, pltpu.roll(m, d, axis=0), m))
        grow_ref[...] = g.T
        m_ref[...] = m
        c_ref[...] = jnp.where(lane < HEADS, -fs - m, 0.0)

    full = pl.BlockSpec((lp, LANE), lambda: (0, 0))
    return pl.pallas_call(
        body, name=name, in_specs=[full, pl.BlockSpec((1, LANE), lambda: (0, 0))],
        out_specs=[pl.BlockSpec((LANE, lp), lambda: (0, 0)), full, full],
        out_shape=[_sds((LANE, lp), F32), _sds((lp, LANE), F32), _sds((lp, LANE), F32)],
        compiler_params=pltpu.CompilerParams(vmem_limit_bytes=VMEM_LIMIT))(gates_pre, bias)


def _pick_lane(blk, h):
    lane = lax.broadcasted_iota(jnp.int32, blk.shape, 1)
    return jnp.sum(jnp.where(lane == h, blk, 0.0), axis=1, keepdims=True)


def _mlstm_weights(q, k, grow, mcol, i, bq, lp):
    s = lax.dot_general(q, k, _DN["nt"], preferred_element_type=F32) * QK_SCALE
    row = i * bq + lax.broadcasted_iota(jnp.int32, (bq, 1), 0)
    col = lax.broadcasted_iota(jnp.int32, (1, lp), 1)
    a = jnp.where(col <= row, jnp.exp(jnp.minimum(grow - mcol, 0.0)), 0.0)
    return s, a


def _mlstm_fwd(name, qkv, grow, mcol_all, ccol_all, nq):
    lp = qkv.shape[0]
    bq = lp // nq

    def body(q_ref, k_ref, v_ref, grow_ref, m_ref, c_ref, o_ref):
        h = pl.program_id(0)
        i = pl.program_id(1)
        grow_h = grow_ref[pl.ds(h, 1), :]
        mcol = _pick_lane(m_ref[...], h)
        ccol = _pick_lane(c_ref[...], h)
        s, a = _mlstm_weights(q_ref[...], k_ref[...], grow_h, mcol, i, bq, lp)
        p = a * s
        den = jnp.sum(p, axis=1, keepdims=True)
        num = jnp.dot(p.astype(BF16), v_ref[...], preferred_element_type=F32)
        o_ref[...] = num / jnp.maximum(jnp.abs(den), jnp.exp(ccol))

    return pl.pallas_call(
        body, name=name, grid=(HEADS, nq),
        in_specs=[pl.BlockSpec((bq, DQK), lambda h, i: (i, h)),
                  pl.BlockSpec((lp, DQK), lambda h, i: (0, HEADS + h)),
                  pl.BlockSpec((lp, DV), lambda h, i: (0, HEADS + h)),
                  pl.BlockSpec((8, lp), lambda h, i: (0, 0)),
                  pl.BlockSpec((bq, LANE), lambda h, i: (i, 0)),
                  pl.BlockSpec((bq, LANE), lambda h, i: (i, 0))],
        out_specs=pl.BlockSpec((bq, DV), lambda h, i: (i, h)),
        out_shape=_sds((lp, MW), F32),
        compiler_params=_cp(("parallel", "parallel")))(qkv, qkv, qkv, grow, mcol_all, ccol_all)


def _mlstm_bwd(name, qkv, grow, mcol_all, ccol_all, ht, dht, nq):
    lp = qkv.shape[0]
    bq = lp // nq

    def body(q_ref, k_ref, v_ref, grow_ref, m_ref, c_ref, ht_ref, dht_ref,
             dq_ref, dk_ref, dv_ref, dgrow_ref, dfx_ref, dkt_acc, dvt_acc):
        h = pl.program_id(0)
        i = pl.program_id(1)

        @pl.when(jnp.logical_and(h == 0, i == 0))
        def _():
            dgrow_ref[...] = jnp.zeros_like(dgrow_ref)
            dfx_ref[...] = jnp.zeros_like(dfx_ref)

        @pl.when(i == 0)
        def _():
            dkt_acc[...] = jnp.zeros_like(dkt_acc)
            dvt_acc[...] = jnp.zeros_like(dvt_acc)

        q = q_ref[...]
        k = k_ref[...]
        v = v_ref[...]
        grow_h = grow_ref[pl.ds(h, 1), :]
        mcol = _pick_lane(m_ref[...], h)
        ccol = _pick_lane(c_ref[...], h)
        s, a = _mlstm_weights(q, k, grow_h, mcol, i, bq, lp)
        p = a * s
        den = jnp.sum(p, axis=1, keepdims=True)
        clamp = jnp.exp(ccol)
        active = jnp.abs(den) < clamp
        dd = jnp.maximum(jnp.abs(den), clamp)
        dht_v = dht_ref[...]
        hdh = jnp.sum(dht_v * ht_ref[...], axis=1, keepdims=True)
        dn = (dht_v / dd).astype(BF16)
        dden = jnp.where(active, 0.0, -(hdh / dd) * jnp.sign(den))
        dp = lax.dot_general(dn, v, _DN["nt"], preferred_element_type=F32) + dden
        rmat = dp * p
        dgrow_ref[pl.ds(h, 1), :] += jnp.sum(rmat, axis=0, keepdims=True)
        ds = (dp * a * QK_SCALE).astype(BF16)
        dq_ref[...] = jnp.dot(ds, k, preferred_element_type=F32).astype(BF16)
        dkt_acc[...] += lax.dot_general(q, ds, _DN["tn"], preferred_element_type=F32)
        dvt_acc[...] += lax.dot_general(dn, p.astype(BF16), _DN["tn"], preferred_element_type=F32)
        lane = lax.broadcasted_iota(jnp.int32, (bq, LANE), 1)
        r0 = pl.multiple_of(i * bq, 16)
        dfx_ref[pl.ds(r0, bq), :] += jnp.where(lane == h, jnp.sum(rmat, axis=1, keepdims=True), 0.0)

        @pl.when(i == nq - 1)
        def _():
            dk_ref[...] = dkt_acc[...].T.astype(BF16)
            dv_ref[...] = dvt_acc[...].T.astype(BF16)

    return pl.pallas_call(
        body, name=name, grid=(HEADS, nq),
        in_specs=[pl.BlockSpec((bq, DQK), lambda h, i: (i, h)),
                  pl.BlockSpec((lp, DQK), lambda h, i: (0, HEADS + h)),
                  pl.BlockSpec((lp, DV), lambda h, i: (0, HEADS + h)),
                  pl.BlockSpec((8, lp), lambda h, i: (0, 0)),
                  pl.BlockSpec((bq, LANE), lambda h, i: (i, 0)),
                  pl.BlockSpec((bq, LANE), lambda h, i: (i, 0)),
                  pl.BlockSpec((bq, DV), lambda h, i: (i, h)),
                  pl.BlockSpec((bq, DV), lambda h, i: (i, h))],
        out_specs=[pl.BlockSpec((bq, DQK), lambda h, i: (i, h)),
                   pl.BlockSpec((lp, DQK), lambda h, i: (0, h)),
                   pl.BlockSpec((lp, DV), lambda h, i: (0, h)),
                   pl.BlockSpec((LANE, lp), lambda h, i: (0, 0)),
                   pl.BlockSpec((lp, LANE), lambda h, i: (0, 0))],
        out_shape=[_sds((lp, QKW), BF16), _sds((lp, QKW), BF16), _sds((lp, MW), BF16),
                   _sds((LANE, lp), F32), _sds((lp, LANE), F32)],
        scratch_shapes=[pltpu.VMEM((DQK, lp), F32), pltpu.VMEM((DV, lp), F32)],
        compiler_params=_cp(("arbitrary", "arbitrary")))(qkv, qkv, qkv, grow, mcol_all, ccol_all, ht, dht)


def _gate_bwd(name, gates_pre, bias, dgrow, dfx):
    lp = gates_pre.shape[0]

    def body(p_ref, b_ref, dgrow_ref, dfx_ref, dg_ref, dgb_ref, db_ref):
        pre = p_ref[...] + b_ref[...]
        lane = lax.broadcasted_iota(jnp.int32, (lp, LANE), 1)
        row = lax.broadcasted_iota(jnp.int32, (lp, LANE), 0)
        th = jnp.tanh(pre * (1.0 / GATE_CAP))
        t = GATE_CAP * th
        dgc = jnp.where(lane < HEADS, dgrow_ref[...].T, 0.0)
        df = jnp.where(lane < HEADS, dfx_ref[...] - dgc, 0.0)
        for d in _scan_steps(lp):
            df = df + jnp.where(row < lp - d, pltpu.roll(df, lp - d, axis=0), 0.0)
        dlf = pltpu.roll(df, HEADS, axis=1)
        dt = jnp.where(lane < HEADS, dgc, dlf * jax.nn.sigmoid(-t))
        dpre = jnp.where(lane < 2 * HEADS, dt * (1.0 - th * th), 0.0)
        dg_ref[...] = dpre
        dgb_ref[...] = dpre.astype(BF16)
        db_ref[...] = jnp.broadcast_to(jnp.sum(dpre, axis=0, keepdims=True), db_ref.shape)

    full = pl.BlockSpec((lp, LANE), lambda: (0, 0))
    return pl.pallas_call(
        body, name=name,
        in_specs=[full, pl.BlockSpec((1, LANE), lambda: (0, 0)), pl.BlockSpec((LANE, lp), lambda: (0, 0)), full],
        out_specs=[full, full, pl.BlockSpec((8, LANE), lambda: (0, 0))],
        out_shape=[_sds((lp, LANE), F32), _sds((lp, LANE), BF16), _sds((8, LANE), F32)],
        compiler_params=pltpu.CompilerParams(vmem_limit_bytes=VMEM_LIMIT))(gates_pre, bias, dgrow, dfx)


CB = 256


def _e_specs(lp):
    return [pl.BlockSpec((None, lp, CB), functools.partial(lambda c, j: (c, 0, j), c)) for c in range(4)]


def _mix_fwd(name, ht, e, mnw, cw):
    lp = ht.shape[0]

    def body(ht_ref, og_ref, u_ref, gb_ref, gc_ref, mnw_ref, cw_ref, o_ref):
        x = ht_ref[...]
        r = lax.rsqrt(jnp.mean(x * x, axis=1, keepdims=True) + EPS)
        o_ref[0] = (jax.nn.sigmoid(og_ref[...].astype(F32)) * (x * r * mnw_ref[...])).astype(BF16)
        row = lax.broadcasted_iota(jnp.int32, (lp, CB), 0)
        a = gc_ref[...].astype(F32) * u_ref[...].astype(F32)
        conv = cw_ref[2:3, :] * a + cw_ref[1:2, :] * _shift_rows(a, 1, row) + cw_ref[0:1, :] * _shift_rows(a, 2, row)
        o_ref[1] = (gb_ref[...].astype(F32) * conv).astype(BF16)

    col = pl.BlockSpec((lp, CB), lambda j: (0, j))
    return pl.pallas_call(
        body, name=name, grid=(4,),
        in_specs=[col] + _e_specs(lp) + [pl.BlockSpec((1, CB), lambda j: (0, j)), pl.BlockSpec((8, CB), lambda j: (0, j))],
        out_specs=pl.BlockSpec((2, lp, CB), lambda j: (0, 0, j)), out_shape=_sds((2, lp, MW), BF16),
        compiler_params=_cp(("parallel",)))(ht, e, e, e, e, mnw, cw)


def _mix_bwd(name, dmix, ht, e, mnw, cw):
    lp = ht.shape[0]

    def body(dhm_ref, dhc_ref, ht_ref, og_ref, u_ref, gb_ref, gc_ref, mnw_ref, cw_ref,
             dht_ref, de_ref, dmnw_ref, dcw_ref):
        x = ht_ref[...]
        r = lax.rsqrt(jnp.mean(x * x, axis=1, keepdims=True) + EPS)
        xh = x * r
        w = mnw_ref[...]
        sg = jax.nn.sigmoid(og_ref[...].astype(F32))
        dhm = dhm_ref[...]
        de_ref[0] = (dhm * (xh * w) * (sg * (1.0 - sg))).astype(BF16)
        dn = dhm * sg
        dmnw_ref[...] = jnp.broadcast_to(jnp.sum(dn * xh, axis=0, keepdims=True), dmnw_ref.shape)
        dxh = dn * w
        dht_ref[...] = r * (dxh - xh * jnp.mean(dxh * xh, axis=1, keepdims=True))

        row = lax.broadcasted_iota(jnp.int32, (lp, CB), 0)
        uv = u_ref[...].astype(F32)
        gcv = gc_ref[...].astype(F32)
        gbv = gb_ref[...].astype(F32)
        a = gcv * uv
        a1 = _shift_rows(a, 1, row)
        a2 = _shift_rows(a, 2, row)
        dhc = dhc_ref[...]
        conv = cw_ref[2:3, :] * a + cw_ref[1:2, :] * a1 + cw_ref[0:1, :] * a2
        de_ref[2] = (dhc * conv).astype(BF16)
        dconv = dhc * gbv
        dcw_ref[...] = jnp.zeros_like(dcw_ref)
        dcw_ref[0:1, :] = jnp.sum(dconv * a2, axis=0, keepdims=True)
        dcw_ref[1:2, :] = jnp.sum(dconv * a1, axis=0, keepdims=True)
        dcw_ref[2:3, :] = jnp.sum(dconv * a, axis=0, keepdims=True)
        up1 = jnp.where(row < lp - 1, pltpu.roll(dconv, lp - 1, axis=0), 0.0)
        up2 = jnp.where(row < lp - 2, pltpu.roll(dconv, lp - 2, axis=0), 0.0)
        da = cw_ref[2:3, :] * dconv + cw_ref[1:2, :] * up1 + cw_ref[0:1, :] * up2
        de_ref[1] = (da * gcv).astype(BF16)
        de_ref[3] = (da * uv).astype(BF16)

    col = pl.BlockSpec((lp, CB), lambda j: (0, j))
    small = pl.BlockSpec((8, CB), lambda j: (0, j))
    return pl.pallas_call(
        body, name=name, grid=(4,),
        in_specs=[col, pl.BlockSpec((lp, CB), lambda j: (0, 4 + j)), col] + _e_specs(lp)
                 + [pl.BlockSpec((1, CB), lambda j: (0, j)), small],
        out_specs=[col, pl.BlockSpec((4, lp, CB), lambda j: (0, 0, j)), small, small],
        out_shape=[_sds((lp, MW), F32), _sds((4, lp, MW), BF16), _sds((8, MW), F32), _sds((8, CW), F32)],
        compiler_params=_cp(("parallel",)))(dmix, dmix, ht, e, e, e, e, mnw, cw)


def _row_tile(r, c, itemsize, budget=1536 * 1024, mult=16):
    best = None
    for t in range(mult, r + 1, mult):
        if r % t == 0 and t * c * itemsize <= budget:
            best = t
    if best is None:
        best = r
    return best


def _grid_spec(grid, in_specs, out_specs, scratch=()):
    return pltpu.PrefetchScalarGridSpec(num_scalar_prefetch=1, grid=grid, in_specs=in_specs,
                                        out_specs=out_specs, scratch_shapes=list(scratch))


def _cast_into(name, w, pf):
    _, r, c = w.shape
    tr = _row_tile(r, c, 4)

    def body(pf_ref, x_ref, o_ref):
        o_ref[...] = x_ref[...].astype(BF16)

    return pl.pallas_call(
        body, name=name, out_shape=_sds((DEPTH, NCHIP, r, c), BF16),
        grid_spec=_grid_spec((DEPTH, r // tr), [pl.BlockSpec((None, tr, c), lambda l, i, pf: (l, i, 0))],
                             pl.BlockSpec((None, None, tr, c), lambda l, i, pf: (l, pf[1], i, 0))),
        compiler_params=_cp(("parallel", "parallel")))(pf, w)


def _add2_bf16(name, dw, got, pf):
    n4, r2, c = got.shape
    tr = _row_tile(r2, c, 4)
    nch = r2 // tr

    def body(pf_ref, a_ref, b_ref, o_ref):
        o_ref[...] = (a_ref[...].astype(F32) + b_ref[...].astype(F32)).astype(BF16)

    spec = pl.BlockSpec((None, tr, c), lambda s, i, pf: (s, i, 0))
    return pl.pallas_call(
        body, name=name, out_shape=_sds((n4, r2, c), BF16),
        grid_spec=_grid_spec((n4, nch), [pl.BlockSpec((None, tr, c), lambda s, i, pf: (s, pf[0] * nch + i, 0)), spec], spec),
        compiler_params=_cp(("parallel", "parallel")))(pf, dw, got)


def _adam_math(w, g, m, v):
    m2 = ADAM_B1 * m + (1.0 - ADAM_B1) * g
    v2 = ADAM_B2 * v + (1.0 - ADAM_B2) * (g * g)
    m_hat = m2 / (1.0 - ADAM_B1 ** ADAM_STEP)
    v_hat = v2 / (1.0 - ADAM_B2 ** ADAM_STEP)
    delta = -ADAM_LR * (m_hat / (jnp.sqrt(v_hat) + ADAM_EPS) + ADAM_WD * w)
    return delta, m2, v2


def _adamw_layer(name, layer, g_mine, g_theirs, w, m, v, prev, pf):
    _, r, c = w.shape
    r2 = r // 2
    tr = _row_tile(r2, c, 4, budget=1024 * 1024, mult=8)
    nch = r2 // tr
    n_alias = 0 if prev is None else 4

    def body(*refs):
        pf_ref, gm_ref, gt_ref, w_ref, m_ref, v_ref = refs[:6]
        go_ref, d_ref, mo_ref, vo_ref = refs[6 + n_alias:]
        mine = (pl.program_id(0) // nch) == pf_ref[0]
        gv = jnp.where(mine, gm_ref[...], gt_ref[...])
        delta, m2, v2 = _adam_math(w_ref[...], gv, m_ref[...], v_ref[...])
        go_ref[...] = gv
        d_ref[...] = delta
        mo_ref[...] = m2
        vo_ref[...] = v2

    slab = pl.BlockSpec((None, tr, c), lambda i, pf: (layer, i, 0))
    ins = [g_mine, g_theirs, w, m, v] + (list(prev) if prev is not None else [])
    in_specs = [pl.BlockSpec((tr, c), lambda i, pf: (jnp.clip(i - pf[0] * nch, 0, nch - 1), 0)),
                pl.BlockSpec((tr, c), lambda i, pf: (jnp.clip(i - (1 - pf[0]) * nch, 0, nch - 1), 0)),
                slab, slab, slab] + [ANY] * n_alias
    o = _sds(w.shape, F32)
    return pl.pallas_call(
        body, name=name, out_shape=[o] * 4, grid_spec=_grid_spec((2 * nch,), in_specs, [slab] * 4),
        input_output_aliases={6 + k: k for k in range(n_alias)},
        compiler_params=_cp(("parallel",)))(pf, *ins)


def _adamw_flat(g, w, m, v):
    def body(g_ref, w_ref, m_ref, v_ref, d_ref, mo_ref, vo_ref):
        delta, m2, v2 = _adam_math(w_ref[...], g_ref[...], m_ref[...], v_ref[...])
        d_ref[...] = delta
        mo_ref[...] = m2
        vo_ref[...] = v2

    o = _sds(w.shape, F32)
    return pl.pallas_call(body, name="adamw_small", out_shape=[o, o, o])(g, w, m, v)


def _place():
    x, y, c = lax.axis_index("x"), lax.axis_index("y"), lax.axis_index("c")
    chips = [(1 - x, y), (x, 1 - y), (1 - x, 1 - y)]
    return x, y, c, chips


def _rs_chips(name, ps):
    n = len(ps)

    def body(*refs):
        ins = refs[:n]
        got = refs[n:2 * n]
        send, recv = refs[2 * n:]
        x, y, c, chips = _place()
        cps = []
        for t in range(n):
            for k, chip in enumerate(chips):
                jk = 2 * chip[0] + chip[1]
                cp = pltpu.make_async_remote_copy(
                    src_ref=ins[t].at[jk], dst_ref=got[t].at[k],
                    send_sem=send.at[3 * t + k], recv_sem=recv.at[3 * t + k],
                    device_id=(*chip, c), device_id_type=MESH)
                cp.start()
                cps.append(cp)
        for cp in cps:
            cp.wait()

    dma = pltpu.SemaphoreType.DMA
    return pl.pallas_call(
        body, name=name, in_specs=[ANY] * n, out_specs=[ANY] * n,
        out_shape=[_sds((3,) + p.shape[1:], BF16) for p in ps],
        scratch_shapes=[dma((3 * n,)), dma((3 * n,))],
        compiler_params=pltpu.CompilerParams(has_side_effects=True))(*ps)


def _sibling():
    x, y, c, _ = _place()
    return (x, y, 1 - c)


def _pair_send(name, dw, pf):
    n4, r, c = dw.shape
    half = r // 2

    def body(pf_ref, x_ref, got_ref, ssem, rsem):
        s = pl.program_id(0)
        cp = pltpu.make_async_remote_copy(src_ref=x_ref, dst_ref=got_ref.at[pl.ds(s, 1)], send_sem=ssem,
                                          recv_sem=rsem, device_id=_sibling(), device_id_type=MESH)
        cp.start()
        cp.wait_send()

        @pl.when(s == n4 - 1)
        def _():
            pltpu.make_async_remote_copy(src_ref=got_ref, dst_ref=got_ref, send_sem=ssem, recv_sem=rsem,
                                         device_id=_sibling(), device_id_type=MESH).wait_recv()

    dma = pltpu.SemaphoreType.DMA
    return pl.pallas_call(
        body, name=name, out_shape=_sds((n4, half, c), BF16),
        grid_spec=_grid_spec((n4,), [pl.BlockSpec((1, half, c), lambda s, pf: (s, 1 - pf[0], 0))], ANY,
                             scratch=[dma(()), dma(())]),
        compiler_params=pltpu.CompilerParams(dimension_semantics=("arbitrary",), has_side_effects=True,
                                             vmem_limit_bytes=VMEM_LIMIT))(pf, dw)


def _add4_join(name, p, got, pf):
    n4, r2, c = p.shape
    tr = _row_tile(r2, c, 4)
    nch = r2 // tr

    def body(pf_ref, p_ref, g_ref, mine_ref, theirs_ref, ssem, rsem):
        i = pl.program_id(0)
        s = p_ref[...].astype(F32)
        for k in range(3):
            s = s + g_ref[k].astype(F32)
        mine_ref[...] = s
        r0 = pl.multiple_of(i * tr, 8)
        cp = pltpu.make_async_remote_copy(src_ref=mine_ref, dst_ref=theirs_ref.at[pl.ds(r0, tr), :],
                                          send_sem=ssem, recv_sem=rsem, device_id=_sibling(), device_id_type=MESH)
        cp.start()
        cp.wait_send()

        @pl.when(i == nch - 1)
        def _():
            pltpu.make_async_remote_copy(src_ref=theirs_ref, dst_ref=theirs_ref, send_sem=ssem, recv_sem=rsem,
                                         device_id=_sibling(), device_id_type=MESH).wait_recv()

    dma = pltpu.SemaphoreType.DMA
    o = _sds((r2, c), F32)
    return pl.pallas_call(
        body, name=name, out_shape=[o, o],
        grid_spec=_grid_spec((nch,), [pl.BlockSpec((None, tr, c), lambda i, pf: (pf[1], i, 0)),
                                      pl.BlockSpec((3, tr, c), lambda i, pf: (0, i, 0))],
                             [pl.BlockSpec((tr, c), lambda i, pf: (i, 0)), ANY], scratch=[dma(()), dma(())]),
        compiler_params=pltpu.CompilerParams(dimension_semantics=("arbitrary",), has_side_effects=True,
                                             vmem_limit_bytes=VMEM_LIMIT))(pf, p, got)


def _allgather(gs, smalls):
    nb, ns = len(gs), len(smalls)
    halves = [g.shape[2] // 2 for g in gs]

    def body(*refs):
        s_in = refs[nb:nb + ns]
        g = refs[nb + ns:2 * nb + ns]
        s_out = refs[2 * nb + ns:2 * (nb + ns)]
        scr = refs[2 * (nb + ns):]
        bufs = scr[:nb]
        send, recv, fsend, frecv, lsem, ssend, srecv, slsem = scr[nb:]
        x, y, c, chips = _place()
        j = 2 * x + y
        sends, slocal = [], []
        for t in range(nb):
            rows = pl.ds(c * halves[t], halves[t])
            piece = g[t].at[:, j, rows, :]
            for k, chip in enumerate(chips):
                cp = pltpu.make_async_remote_copy(
                    src_ref=piece, dst_ref=piece, send_sem=send.at[3 * t + k], recv_sem=recv.at[3 * t + k],
                    device_id=(*chip, c), device_id_type=MESH)
                cp.start()
                sends.append(cp)
        for t in range(ns):
            cp = pltpu.make_async_copy(s_in[t], s_out[t].at[j], slsem.at[t])
            cp.start()
            slocal.append(cp)
            for k, chip in enumerate(chips):
                cp = pltpu.make_async_remote_copy(
                    src_ref=s_in[t], dst_ref=s_out[t].at[j], send_sem=ssend.at[3 * t + k],
                    recv_sem=srecv.at[3 * t + k], device_id=(*chip, c), device_id_type=MESH)
                cp.start()
                sends.append(cp)
        for t in range(nb):
            rows = pl.ds(c * halves[t], halves[t])
            pend = [None, None]
            n = 0
            for k, chip in enumerate(chips):
                jk = 2 * chip[0] + chip[1]
                landed = g[t].at[:, jk, rows, :]
                pltpu.make_async_remote_copy(
                    src_ref=landed, dst_ref=landed, send_sem=send.at[3 * t + k], recv_sem=recv.at[3 * t + k],
                    device_id=(*chip, c), device_id_type=MESH).wait_recv()
                for l in range(DEPTH):
                    slot = n % 2
                    if pend[slot] is not None:
                        pend[slot].wait_send()
                    part = g[t].at[l, jk, rows, :]
                    ld = pltpu.make_async_copy(part, bufs[t].at[slot], lsem.at[2 * t + slot])
                    ld.start()
                    ld.wait()
                    cp = pltpu.make_async_remote_copy(
                        src_ref=bufs[t].at[slot], dst_ref=part, send_sem=fsend.at[2 * t + slot],
                        recv_sem=frecv.at[t], device_id=(x, y, 1 - c), device_id_type=MESH)
                    cp.start()
                    pend[slot] = cp
                    n += 1
            for cp in pend:
                cp.wait_send()
        for t in range(ns):
            for k, chip in enumerate(chips):
                jk = 2 * chip[0] + chip[1]
                landed = s_out[t].at[jk]
                pltpu.make_async_remote_copy(
                    src_ref=landed, dst_ref=landed, send_sem=ssend.at[3 * t + k], recv_sem=srecv.at[3 * t + k],
                    device_id=(*chip, c), device_id_type=MESH).wait_recv()
        for t in range(nb):
            passed = g[t].at[:, pl.ds(0, 3), pl.ds((1 - c) * halves[t], halves[t]), :]
            pltpu.make_async_remote_copy(
                src_ref=passed, dst_ref=passed, send_sem=fsend.at[2 * t], recv_sem=frecv.at[t],
                device_id=(x, y, 1 - c), device_id_type=MESH).wait_recv()
        for cp in sends:
            cp.wait_send()
        for cp in slocal:
            cp.wait()

    dma = pltpu.SemaphoreType.DMA
    out_shape = [_sds(g.shape, g.dtype) for g in gs] + [_sds((NCHIP,) + s.shape, s.dtype) for s in smalls]
    scratch = [pltpu.VMEM((2, halves[t], gs[t].shape[3]), BF16) for t in range(nb)]
    scratch += [dma((3 * nb,)), dma((3 * nb,)), dma((2 * nb,)), dma((nb,)), dma((2 * nb,)),
                dma((3 * ns,)), dma((3 * ns,)), dma((ns,))]
    return pl.pallas_call(
        body, name="allgather_weights", in_specs=[ANY] * (nb + ns), out_specs=[ANY] * (nb + ns),
        out_shape=out_shape, scratch_shapes=scratch, input_output_aliases={t: t for t in range(nb)},
        compiler_params=pltpu.CompilerParams(has_side_effects=True, vmem_limit_bytes=VMEM_LIMIT))(*gs, *smalls)


def _reduce_scatter(tag, dws, pf):
    ps = []
    for t, dw in enumerate(dws):
        got = _pair_send("rs_pair_%s_%d" % (tag, t), dw, pf)
        ps.append(_add2_bf16("rs_add2_%s_%d" % (tag, t), dw, got, pf))
    got2 = _rs_chips("rs_chips_" + tag, ps)
    return [_add4_join("rs_add4_%s_%d" % (tag, t), p, g2, pf) for t, (p, g2) in enumerate(zip(ps, got2))]


def _allreduce_small(pack):
    r = pack.shape[0]
    flips = [(fx, fy, fc) for fx in (0, 1) for fy in (0, 1) for fc in (0, 1)][1:]

    def body(p_ref, o_ref, gat, send, recv):
        x, y, c, _ = _place()
        me = 4 * x + 2 * y + c
        gat[me] = p_ref[...]
        cps = []
        for k, (fx, fy, fc) in enumerate(flips):
            peer = ((1 - x) if fx else x, (1 - y) if fy else y, (1 - c) if fc else c)
            cp = pltpu.make_async_remote_copy(
                src_ref=p_ref, dst_ref=gat.at[me], send_sem=send.at[k], recv_sem=recv.at[k],
                device_id=peer, device_id_type=MESH)
            cp.start()
            cps.append(cp)
        for k, (fx, fy, fc) in enumerate(flips):
            peer = ((1 - x) if fx else x, (1 - y) if fy else y, (1 - c) if fc else c)
            src = 4 * peer[0] + 2 * peer[1] + peer[2]
            pltpu.make_async_remote_copy(
                src_ref=p_ref, dst_ref=gat.at[src], send_sem=send.at[k], recv_sem=recv.at[k],
                device_id=peer, device_id_type=MESH).wait_recv()
        for cp in cps:
            cp.wait_send()
        s = gat[0]
        for d in range(1, 8):
            s = s + gat[d]
        o_ref[...] = s

    dma = pltpu.SemaphoreType.DMA
    vm = pl.BlockSpec(memory_space=pltpu.VMEM)
    return pl.pallas_call(
        body, name="allreduce_small", in_specs=[vm], out_specs=vm, out_shape=_sds((r, LANE), F32),
        scratch_shapes=[pltpu.VMEM((8, r, LANE), F32), dma((7,)), dma((7,))],
        compiler_params=pltpu.CompilerParams(has_side_effects=True))(pack)


def _in_weights(win_g, l):
    full = jnp.concatenate([win_g[l, s] for s in range(NCHIP)], axis=1)
    wqkv = full[:, :2048]
    og = full[:, 2048:3072]
    gates = jnp.pad(full[:, 3072:3080], ((0, 0), (0, LANE - 8)))
    u = full[:, 3080:4104]
    gb = full[:, 4104:5128]
    gc = full[:, 5128:6152]
    return wqkv, jnp.stack([og, u, gb, gc]), gates


def _in_grads(dwqkv, dwe, dwgt):
    full = jnp.concatenate([dwqkv, dwe[0], dwgt[:, :8], dwe[1], dwe[2], dwe[3]], axis=1)
    sw = DIN // NCHIP
    return jnp.stack([full[:, s * sw:(s + 1) * sw] for s in range(NCHIP)])


def _layer_fwd(l, h, wts, small):
    lp = h.shape[0]
    th = lp // 2
    wqkv, we, wgt, wout_g, wg_g, wu_g, wd_g = wts
    nmw, bias, mnw, cw, nfw = small
    tag = "_l%d" % l
    hn = _norm_fwd("norm_mix" + tag, h, nmw)
    qkv = _mm("proj_qkv" + tag, "nn", hn, wqkv,
              pl.BlockSpec((lp, D), lambda i, j, k: (0, 0)), pl.BlockSpec((D, 512), lambda i, j, k: (0, j)),
              pl.BlockSpec((lp, 512), lambda i, j, k: (0, j)), _sds((lp, 2048), BF16), (1, 4, 1))
    e = _mm("proj_e" + tag, "nn", hn, we,
            pl.BlockSpec((lp, D), lambda i, j, k: (0, 0)), pl.BlockSpec((None, D, 512), lambda i, j, k: (j // 2, 0, j % 2)),
            pl.BlockSpec((None, lp, 512), lambda i, j, k: (j // 2, 0, j % 2)), _sds((4, lp, 1024), BF16), (1, 8, 1))
    gpre = _mm("proj_gates" + tag, "nn", hn, wgt,
               pl.BlockSpec((lp, D), lambda i, j, k: (0, 0)), pl.BlockSpec((D, LANE), lambda i, j, k: (0, 0)),
               pl.BlockSpec((lp, LANE), lambda i, j, k: (0, 0)), _sds((lp, LANE), F32), (1, 1, 1))
    grow, mcol, ccol = _gate_prep("gate_prep" + tag, gpre, bias)
    ht = _mlstm_fwd("mlstm_fwd" + tag, qkv, grow, mcol, ccol, 4)
    mix = _mix_fwd("mix_fwd" + tag, ht, e, mnw, cw)
    wout = wout_g.reshape(DEPTH, D, D)
    h1 = _mm("out_proj" + tag, "nn", mix, wout,
             pl.BlockSpec((None, th, 1024), lambda i, j, k: (k, i, 0)),
             pl.BlockSpec((None, 1024, 1024), lambda i, j, k: (l, k, j)),
             pl.BlockSpec((th, 1024), lambda i, j, k: (i, j)), _sds((lp, D), F32), (2, 2, 2),
             acc_shape=(th, 1024), res=h, res_spec=pl.BlockSpec((th, 1024), lambda i, j, k: (i, j)))
    hf = _norm_fwd("norm_ffn" + tag, h1, nfw)
    g, u, a = _ffn_up("ffn_up" + tag, hf, wg_g, wu_g, l)
    tk = DFF // NCHIP
    h2 = _mm("ffn_down" + tag, "nn", a, wd_g,
             pl.BlockSpec((th, tk), lambda i, j, k: (i, k)),
             pl.BlockSpec((None, None, tk, 1024), lambda i, j, k: (l, k, 0, j)),
             pl.BlockSpec((th, 1024), lambda i, j, k: (i, j)), _sds((lp, D), F32), (2, 2, NCHIP),
             acc_shape=(th, 1024), res=h1, res_spec=pl.BlockSpec((th, 1024), lambda i, j, k: (i, j)))
    saved = (h, hn, qkv, e, gpre, grow, mcol, ccol, ht, mix, h1, hf, g, u, a)
    return h2, saved


def _layer_bwd(l, dh2, dh2b, saved, wts, small):
    h, hn, qkv, e, gpre, grow, mcol, ccol, ht, mix, h1, hf, g, u, a = saved
    wqkv, we, wgt, wout_g, wg_g, wu_g, wd_g = wts
    nmw, bias, mnw, cw, nfw = small
    lp = h.shape[0]
    th = lp // 2
    tk = DFF // NCHIP
    tag = "_l%d" % l
    half_rows = lambda i, j, k: (i, j)

    dwd = _mm("dw_down" + tag, "tn", a, dh2b,
              pl.BlockSpec((lp, tk), lambda i, j, k: (0, i)), pl.BlockSpec((lp, 1024), lambda i, j, k: (0, j)),
              pl.BlockSpec((None, tk, 1024), lambda i, j, k: (i, 0, j)), _sds((NCHIP, tk, D), BF16), (NCHIP, 2, 1))
    dg, du = _ffn_bwd_act("ffn_bwd_act" + tag, dh2b, wd_g, g, u, l)
    dws = []
    for nm, dact in (("gate", dg), ("up", du)):
        dws.append(_mm("dw_%s%s" % (nm, tag), "tn", hf, dact,
                       pl.BlockSpec((lp, 1024), lambda i, j, k: (0, i)), pl.BlockSpec((lp, tk), lambda i, j, k: (0, j)),
                       pl.BlockSpec((None, 1024, tk), lambda i, j, k: (j, i, 0)), _sds((NCHIP, D, tk), BF16),
                       (2, NCHIP, 1)))
    dwg, dwu = dws
    dhf = None
    for nm, dact, wfull in (("gate", dg, wg_g), ("up", du, wu_g)):
        dhf = _mm("dhf_%s%s" % (nm, tag), "nt", dact, wfull,
                  pl.BlockSpec((th, tk), lambda i, j, k: (i, k)),
                  pl.BlockSpec((None, None, 1024, tk), lambda i, j, k: (l, k, j, 0)),
                  pl.BlockSpec((th, 1024), half_rows), _sds((lp, D), F32), (2, 2, NCHIP), acc_shape=(th, 1024),
                  res=dhf, res_spec=None if dhf is None else pl.BlockSpec((th, 1024), half_rows))
    dh1, dh1b, dnfw = _norm_bwd("norm_ffn_bwd" + tag, dhf, h1, nfw, dh2)

    dwout = _mm("dw_out" + tag, "tn", mix, dh1b,
                pl.BlockSpec((None, lp, 1024), lambda i, j, k: (i, 0, 0)), pl.BlockSpec((lp, 1024), lambda i, j, k: (0, j)),
                pl.BlockSpec((1024, 1024), half_rows), _sds((D, D), BF16), (2, 2, 1))
    wout = wout_g.reshape(DEPTH, D, D)
    dmix = _mm("dmix" + tag, "nt", dh1b, wout,
               pl.BlockSpec((th, D), lambda i, j, k: (i, 0)), pl.BlockSpec((None, 1024, D), lambda i, j, k: (l, j, 0)),
               pl.BlockSpec((th, 1024), half_rows), _sds((lp, D), F32), (2, 2, 1))
    dht, de, dmnw, dcw = _mix_bwd("mix_bwd" + tag, dmix, ht, e, mnw, cw)
    dq, dk, dv, dgrow, dfx = _mlstm_bwd("mlstm_bwd" + tag, qkv, grow, mcol, ccol, ht, dht, 8)
    dgp, dgpb, dbias = _gate_bwd("gate_bwd" + tag, gpre, bias, dgrow, dfx)
    del dgp
    dqkv = jnp.concatenate([dq, dk, dv], axis=1)

    hn_cols = pl.BlockSpec((lp, 1024), lambda i, j, k: (0, i))
    dwqkv = _mm("dw_qkv" + tag, "tn", hn, dqkv, hn_cols, pl.BlockSpec((lp, 1024), lambda i, j, k: (0, j)),
                pl.BlockSpec((1024, 1024), half_rows), _sds((D, 2048), BF16), (2, 2, 1))
    dwe = _mm("dw_e" + tag, "tn", hn, de, hn_cols, pl.BlockSpec((None, lp, 1024), lambda i, j, k: (j, 0, 0)),
              pl.BlockSpec((None, 1024, 1024), lambda i, j, k: (j, i, 0)), _sds((4, D, 1024), BF16), (2, 4, 1))
    dwgt = _mm("dw_gates" + tag, "tn", hn, dgpb, hn_cols, pl.BlockSpec((lp, LANE), lambda i, j, k: (0, 0)),
               pl.BlockSpec((1024, LANE), lambda i, j, k: (i, 0)), _sds((D, LANE), BF16), (2, 1, 1))
    dwin = _in_grads(dwqkv, dwe, dwgt)

    dhn = _mm("dhn_qkv" + tag, "nt", dqkv, wqkv,
              pl.BlockSpec((th, 2048), lambda i, j, k: (i, 0)), pl.BlockSpec((1024, 2048), lambda i, j, k: (j, 0)),
              pl.BlockSpec((th, 1024), half_rows), _sds((lp, D), F32), (2, 2, 1))
    dhn = _mm("dhn_e" + tag, "nt", de, we,
              pl.BlockSpec((None, th, 1024), lambda i, j, k: (k, i, 0)),
              pl.BlockSpec((None, 1024, 1024), lambda i, j, k: (k, j, 0)),
              pl.BlockSpec((th, 1024), half_rows), _sds((lp, D), F32), (2, 2, 4), acc_shape=(th, 1024),
              res=dhn, res_spec=pl.BlockSpec((th, 1024), half_rows))
    dhn = _mm("dhn_gates" + tag, "nt", dgpb, wgt,
              pl.BlockSpec((th, LANE), lambda i, j, k: (i, 0)), pl.BlockSpec((1024, LANE), lambda i, j, k: (j, 0)),
              pl.BlockSpec((th, 1024), half_rows), _sds((lp, D), F32), (2, 2, 1),
              res=dhn, res_spec=pl.BlockSpec((th, 1024), half_rows))
    dh0, dh0b, dnmw = _norm_bwd("norm_mix_bwd" + tag, dhn, h, nmw, dh1)

    pieces = [dwin, dwout.reshape(NCHIP, D // NCHIP, D), dwg, dwu, dwd]
    smalls = (dnmw[0], dbias[0, :8], dcw[:3], dmnw[0], dnfw[0])
    return dh0, dh0b, pieces, smalls


def _pack_rows(parts):
    rows = []
    for p in parts:
        f = p.reshape(-1)
        pad = (-f.shape[0]) % LANE
        if pad:
            f = jnp.pad(f, (0, pad))
        rows.append(f.reshape(-1, LANE))
    r = jnp.concatenate(rows, axis=0)
    pad = (-r.shape[0]) % 8
    if pad:
        r = jnp.pad(r, ((0, pad), (0, 0)))
    return r


def _unpack_rows(pack, shapes):
    out, r0 = [], 0
    for s in shapes:
        n = 1
        for d in s:
            n *= d
        nr = -(-n // LANE)
        out.append(pack[r0:r0 + nr].reshape(-1)[:n].reshape(s))
        r0 += nr
    return out


def kernel(x, meta_tokens, norm_mix_w, w_in, b_gates, conv_w, mlstm_norm_w, w_out, norm_ffn_w, w_gate, w_up, w_down, norm_final_w, loss_target, m_meta_tokens, m_norm_mix_w, m_w_in, m_b_gates, m_conv_w, m_mlstm_norm_w, m_w_out, m_norm_ffn_w, m_w_gate, m_w_up, m_w_down, m_norm_final_w, v_meta_tokens, v_norm_mix_w, v_w_in, v_b_gates, v_conv_w, v_mlstm_norm_w, v_w_out, v_norm_ffn_w, v_w_gate, v_w_up, v_w_down, v_norm_final_w):
    seq = x.shape[1]
    n_real = N_META + seq
    lp = -(-n_real // LANE) * LANE
    xi, yi, ci = lax.axis_index("x"), lax.axis_index("y"), lax.axis_index("c")
    jchip = 2 * xi + yi
    pf = jnp.stack([ci, jchip, 2 * (1 - xi) + yi, 2 * xi + (1 - yi), 2 * (1 - xi) + (1 - yi)]).astype(jnp.int32)

    big = {"w_in": w_in, "w_out": w_out, "w_gate": w_gate, "w_up": w_up, "w_down": w_down}
    casts = [_cast_into("cast_" + n, w, pf) for n, w in big.items()]
    conv_flat = jnp.pad(conv_w.reshape(DEPTH * 3, CW // NCHIP), ((0, 8 - DEPTH * 3), (0, 0)))
    gathered = _allgather(casts, [meta_tokens, conv_flat])
    win_g, wout_g, wg_g, wu_g, wd_g, meta_g, conv_g = gathered
    meta_full = jnp.concatenate([meta_g[s] for s in range(NCHIP)], axis=1)
    conv_full = jnp.concatenate([conv_g[s][:DEPTH * 3] for s in range(NCHIP)], axis=1)
    conv_full = conv_full.reshape(DEPTH, 3, CW)

    bias_rows = jnp.pad(b_gates, ((0, 0), (0, LANE - 8)))
    wts, smalls = [], []
    for l in range(DEPTH):
        wts.append(_in_weights(win_g, l) + (wout_g, wg_g, wu_g, wd_g))
        smalls.append((norm_mix_w[l][None], bias_rows[l][None], mlstm_norm_w[l][None],
                       jnp.pad(conv_full[l], ((0, 5), (0, 0))), norm_ffn_w[l][None]))

    h = jnp.concatenate([meta_full, x[0], jnp.zeros((lp - n_real, D), F32)], axis=0)
    saved = []
    for l in range(DEPTH):
        h, sv = _layer_fwd(l, h, wts[l], smalls[l])
        saved.append(sv)
    tgt = jnp.pad(loss_target[0], ((N_META, lp - n_real), (0, 0)))
    dh, dhb, dnorm_final, loss_part = _loss_head(h, tgt, norm_final_w[None], n_real)

    names = ["w_in", "w_out", "w_gate", "w_up", "w_down"]
    params = {"w_in": (w_in, m_w_in, v_w_in), "w_out": (w_out, m_w_out, v_w_out), "w_gate": (w_gate, m_w_gate, v_w_gate),
              "w_up": (w_up, m_w_up, v_w_up), "w_down": (w_down, m_w_down, v_w_down)}
    big_out = {n: None for n in names}
    small_grads = [None] * DEPTH
    for l in reversed(range(DEPTH)):
        dh, dhb, pieces, small_grads[l] = _layer_bwd(l, dh, dhb, saved[l], wts[l], smalls[l])
        reduced = _reduce_scatter("l%d" % l, pieces, pf)
        for n, (g_mine, g_theirs) in zip(names, reduced):
            w, m, v = params[n]
            big_out[n] = _adamw_layer("adamw_%s_l%d" % (n, l), l, g_mine, g_theirs, w, m, v, big_out[n], pf)

    dnmw = jnp.stack([small_grads[l][0] for l in range(DEPTH)])
    dbias = jnp.stack([small_grads[l][1] for l in range(DEPTH)])
    dconv = jnp.stack([small_grads[l][2] for l in range(DEPTH)])
    dmnw = jnp.stack([small_grads[l][3] for l in range(DEPTH)])
    dnfw = jnp.stack([small_grads[l][4] for l in range(DEPTH)])
    part_shapes = [(N_META, D), (DEPTH, D), (DEPTH, 8), (DEPTH, 3, CW), (DEPTH, MW), (DEPTH, D), (D,), (LANE,)]
    pack = _pack_rows([dh[:N_META], dnmw, dbias, dconv, dmnw, dnfw, dnorm_final[0], loss_part[0]])
    tot = _unpack_rows(_allreduce_small(pack), part_shapes)
    g_meta_full, g_nmw, g_bias, g_conv_full, g_mnw, g_nfw, g_final, loss_row = tot
    mcols = D // NCHIP
    ccols = CW // NCHIP
    g_meta = lax.dynamic_slice_in_dim(g_meta_full, jchip * mcols, mcols, axis=1)
    g_conv = lax.dynamic_slice_in_dim(g_conv_full, jchip * ccols, ccols, axis=2)
    sm_g = [g_meta, g_nmw, g_bias, g_conv, g_mnw, g_nfw, g_final]
    sm_w = [meta_tokens, norm_mix_w, b_gates, conv_w, mlstm_norm_w, norm_ffn_w, norm_final_w]
    sm_m = [m_meta_tokens, m_norm_mix_w, m_b_gates, m_conv_w, m_mlstm_norm_w, m_norm_ffn_w, m_norm_final_w]
    sm_v = [v_meta_tokens, v_norm_mix_w, v_b_gates, v_conv_w, v_mlstm_norm_w, v_norm_ffn_w, v_norm_final_w]
    sm_shapes = [w.shape for w in sm_w]
    d_p, m_p, v_p = _adamw_flat(_pack_rows(sm_g), _pack_rows(sm_w), _pack_rows(sm_m), _pack_rows(sm_v))
    sm_d = _unpack_rows(d_p, sm_shapes)
    sm_nm = _unpack_rows(m_p, sm_shapes)
    sm_nv = _unpack_rows(v_p, sm_shapes)

    loss = loss_row[0]
    grad_x = dh[N_META:n_real][None]

    def ordered(sm, which):
        bo = {n: big_out[n][which] for n in names}
        return [sm[0], sm[1], bo["w_in"], sm[2], sm[3], sm[4], bo["w_out"], sm[5], bo["w_gate"], bo["w_up"], bo["w_down"], sm[6]]

    return (loss, grad_x, *ordered(sm_g, 0), *ordered(sm_d, 1), *ordered(sm_nm, 2), *ordered(sm_nv, 3))
```

```python
import functools

import jax
import jax.numpy as jnp
from jax import lax
from jax.experimental import pallas as pl
from jax.experimental.pallas import tpu as pltpu

F32 = jnp.float32
BF16 = jnp.bfloat16

D = 2048
N_META = 16
HEADS = 4
DQK = 128
DV = 256
MW = HEADS * DV
CW = D - MW
QKW = HEADS * DQK
DFF = 5632
DIN = 6152
NCHIP = 4
DEPTH = 2
GATE_CAP = 15.0
EPS = 1e-6
QK_SCALE = DQK ** -0.5
LANE = 128
VMEM_LIMIT = 60 * 1024 * 1024

ADAM_LR = 0.001
ADAM_B1 = 0.9
ADAM_B2 = 0.999
ADAM_EPS = 1e-08
ADAM_WD = 0.01
ADAM_STEP = 10

MESH = pl.DeviceIdType.MESH
ANY = pl.BlockSpec(memory_space=pl.ANY)


def _cp(sem):
    return pltpu.CompilerParams(dimension_semantics=sem, vmem_limit_bytes=VMEM_LIMIT)


def _sds(shape, dtype):
    return jax.ShapeDtypeStruct(shape, dtype)


_DN = {"nn": (((1,), (0,)), ((), ())), "nt": (((1,), (1,)), ((), ())), "tn": (((0,), (0,)), ((), ()))}


def _mm(name, kind, a, b, a_spec, b_spec, o_spec, out_shape, grid, acc_shape=None, res=None, res_spec=None):
    nk = grid[2]
    has_res = res is not None

    def body(*refs):
        if has_res:
            a_ref, b_ref, r_ref, o_ref = refs[:4]
        else:
            a_ref, b_ref, o_ref = refs[:3]
            r_ref = None
        p = lax.dot_general(a_ref[...], b_ref[...], _DN[kind], preferred_element_type=F32)
        if nk == 1:
            if r_ref is not None:
                p = p + r_ref[...]
            o_ref[...] = p.astype(o_ref.dtype)
        else:
            acc = refs[-1]
            k = pl.program_id(2)

            @pl.when(k == 0)
            def _():
                acc[...] = p

            @pl.when(k > 0)
            def _():
                acc[...] += p

            @pl.when(k == nk - 1)
            def _():
                r = acc[...]
                if r_ref is not None:
                    r = r + r_ref[...]
                o_ref[...] = r.astype(o_ref.dtype)

    ins = [a, b] + ([res] if has_res else [])
    in_specs = [a_spec, b_spec] + ([res_spec] if has_res else [])
    scratch = [pltpu.VMEM(acc_shape, F32)] if nk > 1 else []
    return pl.pallas_call(
        body, name=name, grid=grid, in_specs=in_specs, out_specs=o_spec, out_shape=out_shape,
        scratch_shapes=scratch, compiler_params=_cp(("parallel", "parallel", "arbitrary")))(*ins)


def _norm_fwd(name, h, w):
    lp = h.shape[0]
    tm = lp // 4

    def body(h_ref, w_ref, o_ref):
        x = h_ref[...]
        r = lax.rsqrt(jnp.mean(x * x, axis=1, keepdims=True) + EPS)
        o_ref[...] = (x * r * w_ref[...]).astype(BF16)

    return pl.pallas_call(
        body, name=name, grid=(4,),
        in_specs=[pl.BlockSpec((tm, D), lambda i: (i, 0)), pl.BlockSpec((1, D), lambda i: (0, 0))],
        out_specs=pl.BlockSpec((tm, D), lambda i: (i, 0)), out_shape=_sds((lp, D), BF16),
        compiler_params=_cp(("parallel",)))(h, w)


def _norm_bwd(name, dy, h, w, dres):
    lp = h.shape[0]
    tm = lp // 8

    def body(dy_ref, h_ref, w_ref, dres_ref, dh_ref, dhb_ref, dw_ref):
        x = h_ref[...]
        r = lax.rsqrt(jnp.mean(x * x, axis=1, keepdims=True) + EPS)
        xh = x * r
        dy_v = dy_ref[...]
        dxh = dy_v * w_ref[...]
        dx = r * (dxh - xh * jnp.mean(dxh * xh, axis=1, keepdims=True))
        dh = dres_ref[...] + dx
        dh_ref[...] = dh
        dhb_ref[...] = dh.astype(BF16)

        @pl.when(pl.program_id(0) == 0)
        def _():
            dw_ref[...] = jnp.zeros_like(dw_ref)

        dw_ref[0:1, :] += jnp.sum(dy_v * xh, axis=0, keepdims=True)

    row = pl.BlockSpec((tm, D), lambda i: (i, 0))
    return pl.pallas_call(
        body, name=name, grid=(8,),
        in_specs=[row, row, pl.BlockSpec((1, D), lambda i: (0, 0)), row],
        out_specs=[row, row, pl.BlockSpec((8, D), lambda i: (0, 0))],
        out_shape=[_sds((lp, D), F32), _sds((lp, D), BF16), _sds((8, D), F32)],
        compiler_params=_cp(("arbitrary",)))(dy, h, w, dres)


def _loss_head(h, tgt, w, n_real):
    lp = h.shape[0]
    tm = lp // 8

    def body(h_ref, t_ref, w_ref, dh_ref, dhb_ref, dw_ref, loss_ref):
        i = pl.program_id(0)
        x = h_ref[...]
        r = lax.rsqrt(jnp.mean(x * x, axis=1, keepdims=True) + EPS)
        xh = x * r
        wv = w_ref[...]
        row = i * tm + lax.broadcasted_iota(jnp.int32, (tm, 1), 0)
        valid = jnp.logical_and(row >= N_META, row < n_real)
        err = jnp.where(valid, xh * wv - t_ref[...], 0.0)
        dy_v = err * (1.0 / D)
        dxh = dy_v * wv
        dx = r * (dxh - xh * jnp.mean(dxh * xh, axis=1, keepdims=True))
        dh_ref[...] = dx
        dhb_ref[...] = dx.astype(BF16)

        @pl.when(i == 0)
        def _():
            dw_ref[...] = jnp.zeros_like(dw_ref)
            loss_ref[...] = jnp.zeros_like(loss_ref)

        dw_ref[0:1, :] += jnp.sum(dy_v * xh, axis=0, keepdims=True)
        part = jnp.sum(jnp.sum(err * err, axis=1, keepdims=True), axis=0, keepdims=True) * (0.5 / D)
        loss_ref[...] += jnp.broadcast_to(part, loss_ref.shape)

    row = pl.BlockSpec((tm, D), lambda i: (i, 0))
    return pl.pallas_call(
        body, name="loss_head", grid=(8,),
        in_specs=[row, row, pl.BlockSpec((1, D), lambda i: (0, 0))],
        out_specs=[row, row, pl.BlockSpec((8, D), lambda i: (0, 0)), pl.BlockSpec((8, LANE), lambda i: (0, 0))],
        out_shape=[_sds((lp, D), F32), _sds((lp, D), BF16), _sds((8, D), F32), _sds((8, LANE), F32)],
        compiler_params=_cp(("arbitrary",)))(h, tgt, w)


def _ffn_up(name, hf, wg4, wu4, layer):
    lp = hf.shape[0]
    tm = lp // 4
    tn = DFF // NCHIP

    def body(x_ref, wg_ref, wu_ref, g_ref, u_ref, a_ref):
        x = x_ref[...]
        g = jnp.dot(x, wg_ref[...], preferred_element_type=F32)
        u = jnp.dot(x, wu_ref[...], preferred_element_type=F32)
        g_ref[...] = g.astype(BF16)
        u_ref[...] = u.astype(BF16)
        a_ref[...] = (g * jax.nn.sigmoid(g) * u).astype(BF16)

    wspec = pl.BlockSpec((None, None, D, tn), lambda j, i: (layer, j, 0, 0))
    ospec = pl.BlockSpec((tm, tn), lambda j, i: (i, j))
    o = _sds((lp, DFF), BF16)
    return pl.pallas_call(
        body, name=name, grid=(NCHIP, 4),
        in_specs=[pl.BlockSpec((tm, D), lambda j, i: (i, 0)), wspec, wspec],
        out_specs=[ospec, ospec, ospec], out_shape=[o, o, o],
        compiler_params=_cp(("parallel", "parallel")))(hf, wg4, wu4)


def _ffn_bwd_act(name, dhb, wd4, g, u, layer):
    lp = dhb.shape[0]
    tm = lp // 4
    tn = DFF // NCHIP

    def body(d_ref, w_ref, g_ref, u_ref, dg_ref, du_ref):
        da = lax.dot_general(d_ref[...], w_ref[...], _DN["nt"], preferred_element_type=F32)
        gv = g_ref[...].astype(F32)
        uv = u_ref[...].astype(F32)
        sg = jax.nn.sigmoid(gv)
        dg_ref[...] = (da * uv * (sg * (1.0 + gv * (1.0 - sg)))).astype(BF16)
        du_ref[...] = (da * (gv * sg)).astype(BF16)

    ospec = pl.BlockSpec((tm, tn), lambda j, i: (i, j))
    o = _sds((lp, DFF), BF16)
    return pl.pallas_call(
        body, name=name, grid=(NCHIP, 4),
        in_specs=[pl.BlockSpec((tm, D), lambda j, i: (i, 0)),
                  pl.BlockSpec((None, None, tn, D), lambda j, i: (layer, j, 0, 0)), ospec, ospec],
        out_specs=[ospec, ospec], out_shape=[o, o],
        compiler_params=_cp(("parallel", "parallel")))(dhb, wd4, g, u)


def _shift_rows(x, d, row):
    return jnp.where(row >= d, pltpu.roll(x, d, axis=0), 0.0)


def _scan_steps(lp):
    d = 1
    while d < lp:
        yield d
        d *= 2


def _gate_values(pre):
    t = GATE_CAP * jnp.tanh(pre * (1.0 / GATE_CAP))
    lf = jnp.minimum(t, 0.0) - jnp.log(1.0 + jnp.exp(-jnp.abs(t)))
    return t, lf


def _gate_prep(name, gates_pre, bias):
    lp = gates_pre.shape[0]

    def body(p_ref, b_ref, grow_ref, m_ref, c_ref):
        pre = p_ref[...] + b_ref[...]
        lane = lax.broadcasted_iota(jnp.int32, (lp, LANE), 1)
        row = lax.broadcasted_iota(jnp.int32, (lp, LANE), 0)
        t, lf = _gate_values(pre)
        f = jnp.where(jnp.logical_and(lane >= HEADS, lane < 2 * HEADS), lf, 0.0)
        for d in _scan_steps(lp):
            f = f + _shift_rows(f, d, row)
        fs = pltpu.roll(f, LANE - HEADS, axis=1)
        g = jnp.where(lane < HEADS, t - fs, 0.0)
        m = g
        for d in _scan_steps(lp):
            m = jnp.maximum(m, jnp.where(row >= d, pltpu.roll(m, d, axis=0), m))
        grow_ref[...] = g.T
        m_ref[...] = m
        c_ref[...] = jnp.where(lane < HEADS, -fs - m, 0.0)

    full = pl.BlockSpec((lp, LANE), lambda: (0, 0))
    return pl.pallas_call(
        body, name=name, in_specs=[full, pl.BlockSpec((1, LANE), lambda: (0, 0))],
        out_specs=[pl.BlockSpec((LANE, lp), lambda: (0, 0)), full, full],
        out_shape=[_sds((LANE, lp), F32), _sds((lp, LANE), F32), _sds((lp, LANE), F32)],
        compiler_params=pltpu.CompilerParams(vmem_limit_bytes=VMEM_LIMIT))(gates_pre, bias)


def _pick_lane(blk, h):
    lane = lax.broadcasted_iota(jnp.int32, blk.shape, 1)
    return jnp.sum(jnp.where(lane == h, blk, 0.0), axis=1, keepdims=True)


def _mlstm_weights(q, k, grow, mcol, i, bq, lp):
    s = lax.dot_general(q, k, _DN["nt"], preferred_element_type=F32) * QK_SCALE
    row = i * bq + lax.broadcasted_iota(jnp.int32, (bq, 1), 0)
    col = lax.broadcasted_iota(jnp.int32, (1, lp), 1)
    a = jnp.where(col <= row, jnp.exp(jnp.minimum(grow - mcol, 0.0)), 0.0)
    return s, a


def _mlstm_fwd(name, qkv, grow, mcol_all, ccol_all, nq):
    lp = qkv.shape[0]
    bq = lp // nq

    def body(q_ref, k_ref, v_ref, grow_ref, m_ref, c_ref, o_ref):
        h = pl.program_id(0)
        i = pl.program_id(1)
        grow_h = grow_ref[pl.ds(h, 1), :]
        mcol = _pick_lane(m_ref[...], h)
        ccol = _pick_lane(c_ref[...], h)
        s, a = _mlstm_weights(q_ref[...], k_ref[...], grow_h, mcol, i, bq, lp)
        p = a * s
        den = jnp.sum(p, axis=1, keepdims=True)
        num = jnp.dot(p.astype(BF16), v_ref[...], preferred_element_type=F32)
        o_ref[...] = num / jnp.maximum(jnp.abs(den), jnp.exp(ccol))

    return pl.pallas_call(
        body, name=name, grid=(HEADS, nq),
        in_specs=[pl.BlockSpec((bq, DQK), lambda h, i: (i, h)),
                  pl.BlockSpec((lp, DQK), lambda h, i: (0, HEADS + h)),
                  pl.BlockSpec((lp, DV), lambda h, i: (0, HEADS + h)),
                  pl.BlockSpec((8, lp), lambda h, i: (0, 0)),
                  pl.BlockSpec((bq, LANE), lambda h, i: (i, 0)),
                  pl.BlockSpec((bq, LANE), lambda h, i: (i, 0))],
        out_specs=pl.BlockSpec((bq, DV), lambda h, i: (i, h)),
        out_shape=_sds((lp, MW), F32),
        compiler_params=_cp(("parallel", "parallel")))(qkv, qkv, qkv, grow, mcol_all, ccol_all)


def _mlstm_bwd(name, qkv, grow, mcol_all, ccol_all, ht, dht, nq):
    lp = qkv.shape[0]
    bq = lp // nq

    def body(q_ref, k_ref, v_ref, grow_ref, m_ref, c_ref, ht_ref, dht_ref,
             dq_ref, dk_ref, dv_ref, dgrow_ref, dfx_ref, dkt_acc, dvt_acc):
        h = pl.program_id(0)
        i = pl.program_id(1)

        @pl.when(jnp.logical_and(h == 0, i == 0))
        def _():
            dgrow_ref[...] = jnp.zeros_like(dgrow_ref)
            dfx_ref[...] = jnp.zeros_like(dfx_ref)

        @pl.when(i == 0)
        def _():
            dkt_acc[...] = jnp.zeros_like(dkt_acc)
            dvt_acc[...] = jnp.zeros_like(dvt_acc)

        q = q_ref[...]
        k = k_ref[...]
        v = v_ref[...]
        grow_h = grow_ref[pl.ds(h, 1), :]
        mcol = _pick_lane(m_ref[...], h)
        ccol = _pick_lane(c_ref[...], h)
        s, a = _mlstm_weights(q, k, grow_h, mcol, i, bq, lp)
        p = a * s
        den = jnp.sum(p, axis=1, keepdims=True)
        clamp = jnp.exp(ccol)
        active = jnp.abs(den) < clamp
        dd = jnp.maximum(jnp.abs(den), clamp)
        dht_v = dht_ref[...]
        hdh = jnp.sum(dht_v * ht_ref[...], axis=1, keepdims=True)
        dn = (dht_v / dd).astype(BF16)
        dden = jnp.where(active, 0.0, -(hdh / dd) * jnp.sign(den))
        dp = lax.dot_general(dn, v, _DN["nt"], preferred_element_type=F32) + dden
        rmat = dp * p
        dgrow_ref[pl.ds(h, 1), :] += jnp.sum(rmat, axis=0, keepdims=True)
        ds = (dp * a * QK_SCALE).astype(BF16)
        dq_ref[...] = jnp.dot(ds, k, preferred_element_type=F32).astype(BF16)
        dkt_acc[...] += lax.dot_general(q, ds, _DN["tn"], preferred_element_type=F32)
        dvt_acc[...] += lax.dot_general(dn, p.astype(BF16), _DN["tn"], preferred_element_type=F32)
        lane = lax.broadcasted_iota(jnp.int32, (bq, LANE), 1)
        r0 = pl.multiple_of(i * bq, 16)
        dfx_ref[pl.ds(r0, bq), :] += jnp.where(lane == h, jnp.sum(rmat, axis=1, keepdims=True), 0.0)

        @pl.when(i == nq - 1)
        def _():
            dk_ref[...] = dkt_acc[...].T.astype(BF16)
            dv_ref[...] = dvt_acc[...].T.astype(BF16)

    return pl.pallas_call(
        body, name=name, grid=(HEADS, nq),
        in_specs=[pl.BlockSpec((bq, DQK), lambda h, i: (i, h)),
                  pl.BlockSpec((lp, DQK), lambda h, i: (0, HEADS + h)),
                  pl.BlockSpec((lp, DV), lambda h, i: (0, HEADS + h)),
                  pl.BlockSpec((8, lp), lambda h, i: (0, 0)),
                  pl.BlockSpec((bq, LANE), lambda h, i: (i, 0)),
                  pl.BlockSpec((bq, LANE), lambda h, i: (i, 0)),
                  pl.BlockSpec((bq, DV), lambda h, i: (i, h)),
                  pl.BlockSpec((bq, DV), lambda h, i: (i, h))],
        out_specs=[pl.BlockSpec((bq, DQK), lambda h, i: (i, h)),
                   pl.BlockSpec((lp, DQK), lambda h, i: (0, h)),
                   pl.BlockSpec((lp, DV), lambda h, i: (0, h)),
                   pl.BlockSpec((LANE, lp), lambda h, i: (0, 0)),
                   pl.BlockSpec((lp, LANE), lambda h, i: (0, 0))],
        out_shape=[_sds((lp, QKW), BF16), _sds((lp, QKW), BF16), _sds((lp, MW), BF16),
                   _sds((LANE, lp), F32), _sds((lp, LANE), F32)],
        scratch_shapes=[pltpu.VMEM((DQK, lp), F32), pltpu.VMEM((DV, lp), F32)],
        compiler_params=_cp(("arbitrary", "arbitrary")))(qkv, qkv, qkv, grow, mcol_all, ccol_all, ht, dht)


def _gate_bwd(name, gates_pre, bias, dgrow, dfx):
    lp = gates_pre.shape[0]

    def body(p_ref, b_ref, dgrow_ref, dfx_ref, dg_ref, dgb_ref, db_ref):
        pre = p_ref[...] + b_ref[...]
        lane = lax.broadcasted_iota(jnp.int32, (lp, LANE), 1)
        row = lax.broadcasted_iota(jnp.int32, (lp, LANE), 0)
        th = jnp.tanh(pre * (1.0 / GATE_CAP))
        t = GATE_CAP * th
        dgc = jnp.where(lane < HEADS, dgrow_ref[...].T, 0.0)
        df = jnp.where(lane < HEADS, dfx_ref[...] - dgc, 0.0)
        for d in _scan_steps(lp):
            df = df + jnp.where(row < lp - d, pltpu.roll(df, lp - d, axis=0), 0.0)
        dlf = pltpu.roll(df, HEADS, axis=1)
        dt = jnp.where(lane < HEADS, dgc, dlf * jax.nn.sigmoid(-t))
        dpre = jnp.where(lane < 2 * HEADS, dt * (1.0 - th * th), 0.0)
        dg_ref[...] = dpre
        dgb_ref[...] = dpre.astype(BF16)
        db_ref[...] = jnp.broadcast_to(jnp.sum(dpre, axis=0, keepdims=True), db_ref.shape)

    full = pl.BlockSpec((lp, LANE), lambda: (0, 0))
    return pl.pallas_call(
        body, name=name,
        in_specs=[full, pl.BlockSpec((1, LANE), lambda: (0, 0)), pl.BlockSpec((LANE, lp), lambda: (0, 0)), full],
        out_specs=[full, full, pl.BlockSpec((8, LANE), lambda: (0, 0))],
        out_shape=[_sds((lp, LANE), F32), _sds((lp, LANE), BF16), _sds((8, LANE), F32)],
        compiler_params=pltpu.CompilerParams(vmem_limit_bytes=VMEM_LIMIT))(gates_pre, bias, dgrow, dfx)


CB = 256


def _e_specs(lp):
    return [pl.BlockSpec((None, lp, CB), functools.partial(lambda c, j: (c, 0, j), c)) for c in range(4)]


def _mix_fwd(name, ht, e, mnw, cw):
    lp = ht.shape[0]

    def body(ht_ref, og_ref, u_ref, gb_ref, gc_ref, mnw_ref, cw_ref, o_ref):
        x = ht_ref[...]
        r = lax.rsqrt(jnp.mean(x * x, axis=1, keepdims=True) + EPS)
        o_ref[0] = (jax.nn.sigmoid(og_ref[...].astype(F32)) * (x * r * mnw_ref[...])).astype(BF16)
        row = lax.broadcasted_iota(jnp.int32, (lp, CB), 0)
        a = gc_ref[...].astype(F32) * u_ref[...].astype(F32)
        conv = cw_ref[2:3, :] * a + cw_ref[1:2, :] * _shift_rows(a, 1, row) + cw_ref[0:1, :] * _shift_rows(a, 2, row)
        o_ref[1] = (gb_ref[...].astype(F32) * conv).astype(BF16)

    col = pl.BlockSpec((lp, CB), lambda j: (0, j))
    return pl.pallas_call(
        body, name=name, grid=(4,),
        in_specs=[col] + _e_specs(lp) + [pl.BlockSpec((1, CB), lambda j: (0, j)), pl.BlockSpec((8, CB), lambda j: (0, j))],
        out_specs=pl.BlockSpec((2, lp, CB), lambda j: (0, 0, j)), out_shape=_sds((2, lp, MW), BF16),
        compiler_params=_cp(("parallel",)))(ht, e, e, e, e, mnw, cw)


def _mix_bwd(name, dmix, ht, e, mnw, cw):
    lp = ht.shape[0]

    def body(dhm_ref, dhc_ref, ht_ref, og_ref, u_ref, gb_ref, gc_ref, mnw_ref, cw_ref,
             dht_ref, de_ref, dmnw_ref, dcw_ref):
        x = ht_ref[...]
        r = lax.rsqrt(jnp.mean(x * x, axis=1, keepdims=True) + EPS)
        xh = x * r
        w = mnw_ref[...]
        sg = jax.nn.sigmoid(og_ref[...].astype(F32))
        dhm = dhm_ref[...]
        de_ref[0] = (dhm * (xh * w) * (sg * (1.0 - sg))).astype(BF16)
        dn = dhm * sg
        dmnw_ref[...] = jnp.broadcast_to(jnp.sum(dn * xh, axis=0, keepdims=True), dmnw_ref.shape)
        dxh = dn * w
        dht_ref[...] = r * (dxh - xh * jnp.mean(dxh * xh, axis=1, keepdims=True))

        row = lax.broadcasted_iota(jnp.int32, (lp, CB), 0)
        uv = u_ref[...].astype(F32)
        gcv = gc_ref[...].astype(F32)
        gbv = gb_ref[...].astype(F32)
        a = gcv * uv
        a1 = _shift_rows(a, 1, row)
        a2 = _shift_rows(a, 2, row)
        dhc = dhc_ref[...]
        conv = cw_ref[2:3, :] * a + cw_ref[1:2, :] * a1 + cw_ref[0:1, :] * a2
        de_ref[2] = (dhc * conv).astype(BF16)
        dconv = dhc * gbv
        dcw_ref[...] = jnp.zeros_like(dcw_ref)
        dcw_ref[0:1, :] = jnp.sum(dconv * a2, axis=0, keepdims=True)
        dcw_ref[1:2, :] = jnp.sum(dconv * a1, axis=0, keepdims=True)
        dcw_ref[2:3, :] = jnp.sum(dconv * a, axis=0, keepdims=True)
        up1 = jnp.where(row < lp - 1, pltpu.roll(dconv, lp - 1, axis=0), 0.0)
        up2 = jnp.where(row < lp - 2, pltpu.roll(dconv, lp - 2, axis=0), 0.0)
        da = cw_ref[2:3, :] * dconv + cw_ref[1:2, :] * up1 + cw_ref[0:1, :] * up2
        de_ref[1] = (da * gcv).astype(BF16)
        de_ref[3] = (da * uv).astype(BF16)

    col = pl.BlockSpec((lp, CB), lambda j: (0, j))
    small = pl.BlockSpec((8, CB), lambda j: (0, j))
    return pl.pallas_call(
        body, name=name, grid=(4,),
        in_specs=[col, pl.BlockSpec((lp, CB), lambda j: (0, 4 + j)), col] + _e_specs(lp)
                 + [pl.BlockSpec((1, CB), lambda j: (0, j)), small],
        out_specs=[col, pl.BlockSpec((4, lp, CB), lambda j: (0, 0, j)), small, small],
        out_shape=[_sds((lp, MW), F32), _sds((4, lp, MW), BF16), _sds((8, MW), F32), _sds((8, CW), F32)],
        compiler_params=_cp(("parallel",)))(dmix, dmix, ht, e, e, e, e, mnw, cw)


def _row_tile(r, c, itemsize, budget=1536 * 1024, mult=16):
    best = None
    for t in range(mult, r + 1, mult):
        if r % t == 0 and t * c * itemsize <= budget:
            best = t
    if best is None:
        best = r
    return best


def _grid_spec(grid, in_specs, out_specs, scratch=()):
    return pltpu.PrefetchScalarGridSpec(num_scalar_prefetch=1, grid=grid, in_specs=in_specs,
                                        out_specs=out_specs, scratch_shapes=list(scratch))


def _cast_into(name, w, pf):
    _, r, c = w.shape
    tr = _row_tile(r, c, 4)

    def body(pf_ref, x_ref, o_ref):
        o_ref[...] = x_ref[...].astype(BF16)

    return pl.pallas_call(
        body, name=name, out_shape=_sds((DEPTH, NCHIP, r, c), BF16),
        grid_spec=_grid_spec((DEPTH, r // tr), [pl.BlockSpec((None, tr, c), lambda l, i, pf: (l, i, 0))],
                             pl.BlockSpec((None, None, tr, c), lambda l, i, pf: (l, pf[1], i, 0))),
        compiler_params=_cp(("parallel", "parallel")))(pf, w)


def _add2_bf16(name, dw, got, pf):
    n4, r2, c = got.shape
    tr = _row_tile(r2, c, 4)
    nch = r2 // tr

    def body(pf_ref, a_ref, b_ref, o_ref):
        o_ref[...] = (a_ref[...].astype(F32) + b_ref[...].astype(F32)).astype(BF16)

    spec = pl.BlockSpec((None, tr, c), lambda s, i, pf: (s, i, 0))
    return pl.pallas_call(
        body, name=name, out_shape=_sds((n4, r2, c), BF16),
        grid_spec=_grid_spec((n4, nch), [pl.BlockSpec((None, tr, c), lambda s, i, pf: (s, pf[0] * nch + i, 0)), spec], spec),
        compiler_params=_cp(("parallel", "parallel")))(pf, dw, got)


def _adam_math(w, g, m, v):
    m2 = ADAM_B1 * m + (1.0 - ADAM_B1) * g
    v2 = ADAM_B2 * v + (1.0 - ADAM_B2) * (g * g)
    m_hat = m2 / (1.0 - ADAM_B1 ** ADAM_STEP)
    v_hat = v2 / (1.0 - ADAM_B2 ** ADAM_STEP)
    delta = -ADAM_LR * (m_hat / (jnp.sqrt(v_hat) + ADAM_EPS) + ADAM_WD * w)
    return delta, m2, v2


def _adamw_layer(name, layer, g_mine, g_theirs, w, m, v, prev, pf):
    _, r, c = w.shape
    r2 = r // 2
    tr = _row_tile(r2, c, 4, budget=1024 * 1024, mult=8)
    nch = r2 // tr
    n_alias = 0 if prev is None else 4

    def body(*refs):
        pf_ref, gm_ref, gt_ref, w_ref, m_ref, v_ref = refs[:6]
        go_ref, d_ref, mo_ref, vo_ref = refs[6 + n_alias:]
        mine = (pl.program_id(0) // nch) == pf_ref[0]
        gv = jnp.where(mine, gm_ref[...], gt_ref[...])
        delta, m2, v2 = _adam_math(w_ref[...], gv, m_ref[...], v_ref[...])
        go_ref[...] = gv
        d_ref[...] = delta
        mo_ref[...] = m2
        vo_ref[...] = v2

    slab = pl.BlockSpec((None, tr, c), lambda i, pf: (layer, i, 0))
    ins = [g_mine, g_theirs, w, m, v] + (list(prev) if prev is not None else [])
    in_specs = [pl.BlockSpec((tr, c), lambda i, pf: (jnp.clip(i - pf[0] * nch, 0, nch - 1), 0)),
                pl.BlockSpec((tr, c), lambda i, pf: (jnp.clip(i - (1 - pf[0]) * nch, 0, nch - 1), 0)),
                slab, slab, slab] + [ANY] * n_alias
    o = _sds(w.shape, F32)
    return pl.pallas_call(
        body, name=name, out_shape=[o] * 4, grid_spec=_grid_spec((2 * nch,), in_specs, [slab] * 4),
        input_output_aliases={6 + k: k for k in range(n_alias)},
        compiler_params=_cp(("parallel",)))(pf, *ins)


def _adamw_flat(g, w, m, v):
    def body(g_ref, w_ref, m_ref, v_ref, d_ref, mo_ref, vo_ref):
        delta, m2, v2 = _adam_math(w_ref[...], g_ref[...], m_ref[...], v_ref[...])
        d_ref[...] = delta
        mo_ref[...] = m2
        vo_ref[...] = v2

    o = _sds(w.shape, F32)
    return pl.pallas_call(body, name="adamw_small", out_shape=[o, o, o])(g, w, m, v)


def _place():
    x, y, c = lax.axis_index("x"), lax.axis_index("y"), lax.axis_index("c")
    chips = [(1 - x, y), (x, 1 - y), (1 - x, 1 - y)]
    return x, y, c, chips


def _rs_chips(name, ps):
    n = len(ps)

    def body(*refs):
        ins = refs[:n]
        got = refs[n:2 * n]
        send, recv = refs[2 * n:]
        x, y, c, chips = _place()
        cps = []
        for t in range(n):
            for k, chip in enumerate(chips):
                jk = 2 * chip[0] + chip[1]
                cp = pltpu.make_async_remote_copy(
                    src_ref=ins[t].at[jk], dst_ref=got[t].at[k],
                    send_sem=send.at[3 * t + k], recv_sem=recv.at[3 * t + k],
                    device_id=(*chip, c), device_id_type=MESH)
                cp.start()
                cps.append(cp)
        for cp in cps:
            cp.wait()

    dma = pltpu.SemaphoreType.DMA
    return pl.pallas_call(
        body, name=name, in_specs=[ANY] * n, out_specs=[ANY] * n,
        out_shape=[_sds((3,) + p.shape[1:], BF16) for p in ps],
        scratch_shapes=[dma((3 * n,)), dma((3 * n,))],
        compiler_params=pltpu.CompilerParams(has_side_effects=True))(*ps)


HBM = pl.BlockSpec(memory_space=pltpu.HBM)
SEM = pl.BlockSpec(memory_space=pltpu.SEMAPHORE)
EFFECT = pltpu.SideEffectType.DATAFLOW_SIDE_EFFECTING


def _in_hbm(a):
    return pltpu.with_memory_space_constraint(a, pltpu.HBM)


def _rs_chips_copies(ins, lands, send, recv):
    x, y, c, chips = _place()
    cps = []
    for t in range(len(ins)):
        for k, chip in enumerate(chips):
            jk = 2 * chip[0] + chip[1]
            cps.append(pltpu.make_async_remote_copy(
                src_ref=ins[t].at[jk], dst_ref=lands[t].at[k], send_sem=send.at[3 * t + k],
                recv_sem=recv.at[3 * t + k], device_id=(*chip, c), device_id_type=MESH))
    return cps


def _rs_chips_start(name, ps):
    n = len(ps)

    def body(*refs):
        ins, lands = refs[:n], refs[n:2 * n]
        send, recv = refs[2 * n], refs[2 * n + 1]
        token = refs[-1]
        for cp in _rs_chips_copies(ins, lands, send, recv):
            cp.start()
        token[...] = jnp.zeros_like(token)

    dma = pltpu.SemaphoreType.DMA
    lands = [lax.empty((3,) + p.shape[1:], BF16) for p in ps]
    out_shape = ([dma((3 * n,)), dma((3 * n,))] + [pltpu.HBM(p.shape, BF16) for p in ps]
                 + [pltpu.HBM(z.shape, BF16) for z in lands] + [_sds((8, LANE), F32)])
    outs = pl.pallas_call(
        body, name=name, out_shape=out_shape, in_specs=[HBM] * (2 * n),
        out_specs=[SEM, SEM] + [HBM] * (2 * n) + [pl.BlockSpec(memory_space=pltpu.VMEM)],
        input_output_aliases={i: 2 + i for i in range(2 * n)},
        compiler_params=pltpu.CompilerParams(has_side_effects=EFFECT))(
            *[_in_hbm(p) for p in ps], *[_in_hbm(z) for z in lands])
    return outs[0], outs[1], outs[2:2 + n], outs[2 + n:2 + 2 * n], outs[-1]


def _rs_chips_wait(name, send, recv, ps, lands, after):
    n = len(ps)

    def body(*refs):
        ins, zones = refs[:n], refs[n:2 * n]
        send_ref, recv_ref = refs[2 * n], refs[2 * n + 1]
        for cp in _rs_chips_copies(ins, zones, send_ref, recv_ref):
            cp.wait_send()
            cp.wait_recv()

    outs = pl.pallas_call(
        body, name=name, out_shape=[pltpu.HBM(p.shape, BF16) for p in ps] + [pltpu.HBM(z.shape, BF16) for z in lands],
        in_specs=[HBM] * (2 * n) + [SEM, SEM, ANY], out_specs=[HBM] * (2 * n),
        input_output_aliases={i: i for i in range(2 * n)},
        compiler_params=pltpu.CompilerParams(has_side_effects=EFFECT))(*ps, *lands, send, recv, after)
    return outs[:n], outs[n:]


def _sibling():
    x, y, c, _ = _place()
    return (x, y, 1 - c)


def _pair_send(name, dw, pf):
    n4, r, c = dw.shape
    half = r // 2

    def body(pf_ref, x_ref, got_ref, ssem, rsem):
        s = pl.program_id(0)
        cp = pltpu.make_async_remote_copy(src_ref=x_ref, dst_ref=got_ref.at[pl.ds(s, 1)], send_sem=ssem,
                                          recv_sem=rsem, device_id=_sibling(), device_id_type=MESH)
        cp.start()
        cp.wait_send()

        @pl.when(s == n4 - 1)
        def _():
            pltpu.make_async_remote_copy(src_ref=got_ref, dst_ref=got_ref, send_sem=ssem, recv_sem=rsem,
                                         device_id=_sibling(), device_id_type=MESH).wait_recv()

    dma = pltpu.SemaphoreType.DMA
    return pl.pallas_call(
        body, name=name, out_shape=_sds((n4, half, c), BF16),
        grid_spec=_grid_spec((n4,), [pl.BlockSpec((1, half, c), lambda s, pf: (s, 1 - pf[0], 0))], ANY,
                             scratch=[dma(()), dma(())]),
        compiler_params=pltpu.CompilerParams(dimension_semantics=("arbitrary",), has_side_effects=True,
                                             vmem_limit_bytes=VMEM_LIMIT))(pf, dw)


def _add4_join(name, p, got, pf):
    n4, r2, c = p.shape
    tr = _row_tile(r2, c, 4)
    nch = r2 // tr

    def body(pf_ref, p_ref, g_ref, mine_ref, theirs_ref, ssem, rsem):
        i = pl.program_id(0)
        s = p_ref[...].astype(F32)
        for k in range(3):
            s = s + g_ref[k].astype(F32)
        mine_ref[...] = s
        r0 = pl.multiple_of(i * tr, 8)
        cp = pltpu.make_async_remote_copy(src_ref=mine_ref, dst_ref=theirs_ref.at[pl.ds(r0, tr), :],
                                          send_sem=ssem, recv_sem=rsem, device_id=_sibling(), device_id_type=MESH)
        cp.start()
        cp.wait_send()

        @pl.when(i == nch - 1)
        def _():
            pltpu.make_async_remote_copy(src_ref=theirs_ref, dst_ref=theirs_ref, send_sem=ssem, recv_sem=rsem,
                                         device_id=_sibling(), device_id_type=MESH).wait_recv()

    dma = pltpu.SemaphoreType.DMA
    o = _sds((r2, c), F32)
    return pl.pallas_call(
        body, name=name, out_shape=[o, o],
        grid_spec=_grid_spec((nch,), [pl.BlockSpec((None, tr, c), lambda i, pf: (pf[1], i, 0)),
                                      pl.BlockSpec((3, tr, c), lambda i, pf: (0, i, 0))],
                             [pl.BlockSpec((tr, c), lambda i, pf: (i, 0)), ANY], scratch=[dma(()), dma(())]),
        compiler_params=pltpu.CompilerParams(dimension_semantics=("arbitrary",), has_side_effects=True,
                                             vmem_limit_bytes=VMEM_LIMIT))(pf, p, got)


def _allgather(gs, smalls):
    nb, ns = len(gs), len(smalls)
    halves = [g.shape[2] // 2 for g in gs]

    def body(*refs):
        s_in = refs[nb:nb + ns]
        g = refs[nb + ns:2 * nb + ns]
        s_out = refs[2 * nb + ns:2 * (nb + ns)]
        scr = refs[2 * (nb + ns):]
        bufs = scr[:nb]
        send, recv, fsend, frecv, lsem, ssend, srecv, slsem = scr[nb:]
        x, y, c, chips = _place()
        j = 2 * x + y
        sends, slocal = [], []
        for t in range(nb):
            rows = pl.ds(c * halves[t], halves[t])
            piece = g[t].at[:, j, rows, :]
            for k, chip in enumerate(chips):
                cp = pltpu.make_async_remote_copy(
                    src_ref=piece, dst_ref=piece, send_sem=send.at[3 * t + k], recv_sem=recv.at[3 * t + k],
                    device_id=(*chip, c), device_id_type=MESH)
                cp.start()
                sends.append(cp)
        for t in range(ns):
            cp = pltpu.make_async_copy(s_in[t], s_out[t].at[j], slsem.at[t])
            cp.start()
            slocal.append(cp)
            for k, chip in enumerate(chips):
                cp = pltpu.make_async_remote_copy(
                    src_ref=s_in[t], dst_ref=s_out[t].at[j], send_sem=ssend.at[3 * t + k],
                    recv_sem=srecv.at[3 * t + k], device_id=(*chip, c), device_id_type=MESH)
                cp.start()
                sends.append(cp)
        for t in range(nb):
            rows = pl.ds(c * halves[t], halves[t])
            pend = [None, None]
            n = 0
            for k, chip in enumerate(chips):
                jk = 2 * chip[0] + chip[1]
                landed = g[t].at[:, jk, rows, :]
                pltpu.make_async_remote_copy(
                    src_ref=landed, dst_ref=landed, send_sem=send.at[3 * t + k], recv_sem=recv.at[3 * t + k],
                    device_id=(*chip, c), device_id_type=MESH).wait_recv()
                for l in range(DEPTH):
                    slot = n % 2
                    if pend[slot] is not None:
                        pend[slot].wait_send()
                    part = g[t].at[l, jk, rows, :]
                    ld = pltpu.make_async_copy(part, bufs[t].at[slot], lsem.at[2 * t + slot])
                    ld.start()
                    ld.wait()
                    cp = pltpu.make_async_remote_copy(
                        src_ref=bufs[t].at[slot], dst_ref=part, send_sem=fsend.at[2 * t + slot],
                        recv_sem=frecv.at[t], device_id=(x, y, 1 - c), device_id_type=MESH)
                    cp.start()
                    pend[slot] = cp
                    n += 1
            for cp in pend:
                cp.wait_send()
        for t in range(ns):
            for k, chip in enumerate(chips):
                jk = 2 * chip[0] + chip[1]
                landed = s_out[t].at[jk]
                pltpu.make_async_remote_copy(
                    src_ref=landed, dst_ref=landed, send_sem=ssend.at[3 * t + k], recv_sem=srecv.at[3 * t + k],
                    device_id=(*chip, c), device_id_type=MESH).wait_recv()
        for t in range(nb):
            passed = g[t].at[:, pl.ds(0, 3), pl.ds((1 - c) * halves[t], halves[t]), :]
            pltpu.make_async_remote_copy(
                src_ref=passed, dst_ref=passed, send_sem=fsend.at[2 * t], recv_sem=frecv.at[t],
                device_id=(x, y, 1 - c), device_id_type=MESH).wait_recv()
        for cp in sends:
            cp.wait_send()
        for cp in slocal:
            cp.wait()

    dma = pltpu.SemaphoreType.DMA
    out_shape = [_sds(g.shape, g.dtype) for g in gs] + [_sds((NCHIP,) + s.shape, s.dtype) for s in smalls]
    scratch = [pltpu.VMEM((2, halves[t], gs[t].shape[3]), BF16) for t in range(nb)]
    scratch += [dma((3 * nb,)), dma((3 * nb,)), dma((2 * nb,)), dma((nb,)), dma((2 * nb,)),
                dma((3 * ns,)), dma((3 * ns,)), dma((ns,))]
    return pl.pallas_call(
        body, name="allgather_weights", in_specs=[ANY] * (nb + ns), out_specs=[ANY] * (nb + ns),
        out_shape=out_shape, scratch_shapes=scratch, input_output_aliases={t: t for t in range(nb)},
        compiler_params=pltpu.CompilerParams(has_side_effects=True, vmem_limit_bytes=VMEM_LIMIT))(*gs, *smalls)


def _reduce_scatter(tag, dws, pf):
    ps = []
    for t, dw in enumerate(dws):
        got = _pair_send("rs_pair_%s_%d" % (tag, t), dw, pf)
        ps.append(_add2_bf16("rs_add2_%s_%d" % (tag, t), dw, got, pf))
    got2 = _rs_chips("rs_chips_" + tag, ps)
    return [_add4_join("rs_add4_%s_%d" % (tag, t), p, g2, pf) for t, (p, g2) in enumerate(zip(ps, got2))]


def _rs_begin(tag, dws, pf):
    ps = []
    for t, dw in enumerate(dws):
        got = _pair_send("rs_pair_%s_%d" % (tag, t), dw, pf)
        ps.append(_add2_bf16("rs_add2_%s_%d" % (tag, t), dw, got, pf))
    send, recv, ps_thru, lands, token = _rs_chips_start("rs_chips_start_" + tag, ps)
    return (tag, send, recv, ps_thru, lands), token


def _rs_end(handle, after, pf):
    tag, send, recv, ps, lands = handle
    ps, got2 = _rs_chips_wait("rs_chips_wait_" + tag, send, recv, ps, lands, after)
    return [_add4_join("rs_add4_%s_%d" % (tag, t), p, g2, pf) for t, (p, g2) in enumerate(zip(ps, got2))]


def _allreduce_small(pack):
    r = pack.shape[0]
    flips = [(fx, fy, fc) for fx in (0, 1) for fy in (0, 1) for fc in (0, 1)][1:]

    def body(p_ref, o_ref, gat, send, recv):
        x, y, c, _ = _place()
        me = 4 * x + 2 * y + c
        gat[me] = p_ref[...]
        cps = []
        for k, (fx, fy, fc) in enumerate(flips):
            peer = ((1 - x) if fx else x, (1 - y) if fy else y, (1 - c) if fc else c)
            cp = pltpu.make_async_remote_copy(
                src_ref=p_ref, dst_ref=gat.at[me], send_sem=send.at[k], recv_sem=recv.at[k],
                device_id=peer, device_id_type=MESH)
            cp.start()
            cps.append(cp)
        for k, (fx, fy, fc) in enumerate(flips):
            peer = ((1 - x) if fx else x, (1 - y) if fy else y, (1 - c) if fc else c)
            src = 4 * peer[0] + 2 * peer[1] + peer[2]
            pltpu.make_async_remote_copy(
                src_ref=p_ref, dst_ref=gat.at[src], send_sem=send.at[k], recv_sem=recv.at[k],
                device_id=peer, device_id_type=MESH).wait_recv()
        for cp in cps:
            cp.wait_send()
        s = gat[0]
        for d in range(1, 8):
            s = s + gat[d]
        o_ref[...] = s

    dma = pltpu.SemaphoreType.DMA
    vm = pl.BlockSpec(memory_space=pltpu.VMEM)
    return pl.pallas_call(
        body, name="allreduce_small", in_specs=[vm], out_specs=vm, out_shape=_sds((r, LANE), F32),
        scratch_shapes=[pltpu.VMEM((8, r, LANE), F32), dma((7,)), dma((7,))],
        compiler_params=pltpu.CompilerParams(has_side_effects=True))(pack)


def _in_weights(win_g, l):
    full = jnp.concatenate([win_g[l, s] for s in range(NCHIP)], axis=1)
    wqkv = full[:, :2048]
    og = full[:, 2048:3072]
    gates = jnp.pad(full[:, 3072:3080], ((0, 0), (0, LANE - 8)))
    u = full[:, 3080:4104]
    gb = full[:, 4104:5128]
    gc = full[:, 5128:6152]
    return wqkv, jnp.stack([og, u, gb, gc]), gates


def _in_grads(dwqkv, dwe, dwgt):
    full = jnp.concatenate([dwqkv, dwe[0], dwgt[:, :8], dwe[1], dwe[2], dwe[3]], axis=1)
    sw = DIN // NCHIP
    return jnp.stack([full[:, s * sw:(s + 1) * sw] for s in range(NCHIP)])


def _layer_fwd(l, h, wts, small):
    lp = h.shape[0]
    th = lp // 2
    wqkv, we, wgt, wout_g, wg_g, wu_g, wd_g = wts
    nmw, bias, mnw, cw, nfw = small
    tag = "_l%d" % l
    hn = _norm_fwd("norm_mix" + tag, h, nmw)
    qkv = _mm("proj_qkv" + tag, "nn", hn, wqkv,
              pl.BlockSpec((lp, D), lambda i, j, k: (0, 0)), pl.BlockSpec((D, 512), lambda i, j, k: (0, j)),
              pl.BlockSpec((lp, 512), lambda i, j, k: (0, j)), _sds((lp, 2048), BF16), (1, 4, 1))
    e = _mm("proj_e" + tag, "nn", hn, we,
            pl.BlockSpec((lp, D), lambda i, j, k: (0, 0)), pl.BlockSpec((None, D, 512), lambda i, j, k: (j // 2, 0, j % 2)),
            pl.BlockSpec((None, lp, 512), lambda i, j, k: (j // 2, 0, j % 2)), _sds((4, lp, 1024), BF16), (1, 8, 1))
    gpre = _mm("proj_gates" + tag, "nn", hn, wgt,
               pl.BlockSpec((lp, D), lambda i, j, k: (0, 0)), pl.BlockSpec((D, LANE), lambda i, j, k: (0, 0)),
               pl.BlockSpec((lp, LANE), lambda i, j, k: (0, 0)), _sds((lp, LANE), F32), (1, 1, 1))
    grow, mcol, ccol = _gate_prep("gate_prep" + tag, gpre, bias)
    ht = _mlstm_fwd("mlstm_fwd" + tag, qkv, grow, mcol, ccol, 4)
    mix = _mix_fwd("mix_fwd" + tag, ht, e, mnw, cw)
    wout = wout_g.reshape(DEPTH, D, D)
    h1 = _mm("out_proj" + tag, "nn", mix, wout,
             pl.BlockSpec((None, th, 1024), lambda i, j, k: (k, i, 0)),
             pl.BlockSpec((None, 1024, 1024), lambda i, j, k: (l, k, j)),
             pl.BlockSpec((th, 1024), lambda i, j, k: (i, j)), _sds((lp, D), F32), (2, 2, 2),
             acc_shape=(th, 1024), res=h, res_spec=pl.BlockSpec((th, 1024), lambda i, j, k: (i, j)))
    hf = _norm_fwd("norm_ffn" + tag, h1, nfw)
    g, u, a = _ffn_up("ffn_up" + tag, hf, wg_g, wu_g, l)
    tk = DFF // NCHIP
    h2 = _mm("ffn_down" + tag, "nn", a, wd_g,
             pl.BlockSpec((th, tk), lambda i, j, k: (i, k)),
             pl.BlockSpec((None, None, tk, 1024), lambda i, j, k: (l, k, 0, j)),
             pl.BlockSpec((th, 1024), lambda i, j, k: (i, j)), _sds((lp, D), F32), (2, 2, NCHIP),
             acc_shape=(th, 1024), res=h1, res_spec=pl.BlockSpec((th, 1024), lambda i, j, k: (i, j)))
    saved = (h, hn, qkv, e, gpre, grow, mcol, ccol, ht, mix, h1, hf, g, u, a)
    return h2, saved


def _layer_bwd(l, dh2, dh2b, saved, wts, small):
    h, hn, qkv, e, gpre, grow, mcol, ccol, ht, mix, h1, hf, g, u, a = saved
    wqkv, we, wgt, wout_g, wg_g, wu_g, wd_g = wts
    nmw, bias, mnw, cw, nfw = small
    lp = h.shape[0]
    th = lp // 2
    tk = DFF // NCHIP
    tag = "_l%d" % l
    half_rows = lambda i, j, k: (i, j)

    dwd = _mm("dw_down" + tag, "tn", a, dh2b,
              pl.BlockSpec((lp, tk), lambda i, j, k: (0, i)), pl.BlockSpec((lp, 1024), lambda i, j, k: (0, j)),
              pl.BlockSpec((None, tk, 1024), lambda i, j, k: (i, 0, j)), _sds((NCHIP, tk, D), BF16), (NCHIP, 2, 1))
    dg, du = _ffn_bwd_act("ffn_bwd_act" + tag, dh2b, wd_g, g, u, l)
    dws = []
    for nm, dact in (("gate", dg), ("up", du)):
        dws.append(_mm("dw_%s%s" % (nm, tag), "tn", hf, dact,
                       pl.BlockSpec((lp, 1024), lambda i, j, k: (0, i)), pl.BlockSpec((lp, tk), lambda i, j, k: (0, j)),
                       pl.BlockSpec((None, 1024, tk), lambda i, j, k: (j, i, 0)), _sds((NCHIP, D, tk), BF16),
                       (2, NCHIP, 1)))
    dwg, dwu = dws
    dhf = None
    for nm, dact, wfull in (("gate", dg, wg_g), ("up", du, wu_g)):
        dhf = _mm("dhf_%s%s" % (nm, tag), "nt", dact, wfull,
                  pl.BlockSpec((th, tk), lambda i, j, k: (i, k)),
                  pl.BlockSpec((None, None, 1024, tk), lambda i, j, k: (l, k, j, 0)),
                  pl.BlockSpec((th, 1024), half_rows), _sds((lp, D), F32), (2, 2, NCHIP), acc_shape=(th, 1024),
                  res=dhf, res_spec=None if dhf is None else pl.BlockSpec((th, 1024), half_rows))
    dh1, dh1b, dnfw = _norm_bwd("norm_ffn_bwd" + tag, dhf, h1, nfw, dh2)

    dwout = _mm("dw_out" + tag, "tn", mix, dh1b,
                pl.BlockSpec((None, lp, 1024), lambda i, j, k: (i, 0, 0)), pl.BlockSpec((lp, 1024), lambda i, j, k: (0, j)),
                pl.BlockSpec((1024, 1024), half_rows), _sds((D, D), BF16), (2, 2, 1))
    wout = wout_g.reshape(DEPTH, D, D)
    dmix = _mm("dmix" + tag, "nt", dh1b, wout,
               pl.BlockSpec((th, D), lambda i, j, k: (i, 0)), pl.BlockSpec((None, 1024, D), lambda i, j, k: (l, j, 0)),
               pl.BlockSpec((th, 1024), half_rows), _sds((lp, D), F32), (2, 2, 1))
    dht, de, dmnw, dcw = _mix_bwd("mix_bwd" + tag, dmix, ht, e, mnw, cw)
    dq, dk, dv, dgrow, dfx = _mlstm_bwd("mlstm_bwd" + tag, qkv, grow, mcol, ccol, ht, dht, 8)
    dgp, dgpb, dbias = _gate_bwd("gate_bwd" + tag, gpre, bias, dgrow, dfx)
    del dgp
    dqkv = jnp.concatenate([dq, dk, dv], axis=1)

    hn_cols = pl.BlockSpec((lp, 1024), lambda i, j, k: (0, i))
    dwqkv = _mm("dw_qkv" + tag, "tn", hn, dqkv, hn_cols, pl.BlockSpec((lp, 1024), lambda i, j, k: (0, j)),
                pl.BlockSpec((1024, 1024), half_rows), _sds((D, 2048), BF16), (2, 2, 1))
    dwe = _mm("dw_e" + tag, "tn", hn, de, hn_cols, pl.BlockSpec((None, lp, 1024), lambda i, j, k: (j, 0, 0)),
              pl.BlockSpec((None, 1024, 1024), lambda i, j, k: (j, i, 0)), _sds((4, D, 1024), BF16), (2, 4, 1))
    dwgt = _mm("dw_gates" + tag, "tn", hn, dgpb, hn_cols, pl.BlockSpec((lp, LANE), lambda i, j, k: (0, 0)),
               pl.BlockSpec((1024, LANE), lambda i, j, k: (i, 0)), _sds((D, LANE), BF16), (2, 1, 1))
    dwin = _in_grads(dwqkv, dwe, dwgt)

    dhn = _mm("dhn_qkv" + tag, "nt", dqkv, wqkv,
              pl.BlockSpec((th, 2048), lambda i, j, k: (i, 0)), pl.BlockSpec((1024, 2048), lambda i, j, k: (j, 0)),
              pl.BlockSpec((th, 1024), half_rows), _sds((lp, D), F32), (2, 2, 1))
    dhn = _mm("dhn_e" + tag, "nt", de, we,
              pl.BlockSpec((None, th, 1024), lambda i, j, k: (k, i, 0)),
              pl.BlockSpec((None, 1024, 1024), lambda i, j, k: (k, j, 0)),
              pl.BlockSpec((th, 1024), half_rows), _sds((lp, D), F32), (2, 2, 4), acc_shape=(th, 1024),
              res=dhn, res_spec=pl.BlockSpec((th, 1024), half_rows))
    dhn = _mm("dhn_gates" + tag, "nt", dgpb, wgt,
              pl.BlockSpec((th, LANE), lambda i, j, k: (i, 0)), pl.BlockSpec((1024, LANE), lambda i, j, k: (j, 0)),
              pl.BlockSpec((th, 1024), half_rows), _sds((lp, D), F32), (2, 2, 1),
              res=dhn, res_spec=pl.BlockSpec((th, 1024), half_rows))
    dh0, dh0b, dnmw = _norm_bwd("norm_mix_bwd" + tag, dhn, h, nmw, dh1)

    pieces = [dwin, dwout.reshape(NCHIP, D // NCHIP, D), dwg, dwu, dwd]
    smalls = (dnmw[0], dbias[0, :8], dcw[:3], dmnw[0], dnfw[0])
    return dh0, dh0b, pieces, smalls


def _pack_rows(parts):
    rows = []
    for p in parts:
        f = p.reshape(-1)
        pad = (-f.shape[0]) % LANE
        if pad:
            f = jnp.pad(f, (0, pad))
        rows.append(f.reshape(-1, LANE))
    r = jnp.concatenate(rows, axis=0)
    pad = (-r.shape[0]) % 8
    if pad:
        r = jnp.pad(r, ((0, pad), (0, 0)))
    return r


def _unpack_rows(pack, shapes):
    out, r0 = [], 0
    for s in shapes:
        n = 1
        for d in s:
            n *= d
        nr = -(-n // LANE)
        out.append(pack[r0:r0 + nr].reshape(-1)[:n].reshape(s))
        r0 += nr
    return out


def kernel(x, meta_tokens, norm_mix_w, w_in, b_gates, conv_w, mlstm_norm_w, w_out, norm_ffn_w, w_gate, w_up, w_down, norm_final_w, loss_target, m_meta_tokens, m_norm_mix_w, m_w_in, m_b_gates, m_conv_w, m_mlstm_norm_w, m_w_out, m_norm_ffn_w, m_w_gate, m_w_up, m_w_down, m_norm_final_w, v_meta_tokens, v_norm_mix_w, v_w_in, v_b_gates, v_conv_w, v_mlstm_norm_w, v_w_out, v_norm_ffn_w, v_w_gate, v_w_up, v_w_down, v_norm_final_w):
    seq = x.shape[1]
    n_real = N_META + seq
    lp = -(-n_real // LANE) * LANE
    xi, yi, ci = lax.axis_index("x"), lax.axis_index("y"), lax.axis_index("c")
    jchip = 2 * xi + yi
    pf = jnp.stack([ci, jchip, 2 * (1 - xi) + yi, 2 * xi + (1 - yi), 2 * (1 - xi) + (1 - yi)]).astype(jnp.int32)

    big = {"w_in": w_in, "w_out": w_out, "w_gate": w_gate, "w_up": w_up, "w_down": w_down}
    casts = [_cast_into("cast_" + n, w, pf) for n, w in big.items()]
    conv_flat = jnp.pad(conv_w.reshape(DEPTH * 3, CW // NCHIP), ((0, 8 - DEPTH * 3), (0, 0)))
    gathered = _allgather(casts, [meta_tokens, conv_flat])
    win_g, wout_g, wg_g, wu_g, wd_g, meta_g, conv_g = gathered
    meta_full = jnp.concatenate([meta_g[s] for s in range(NCHIP)], axis=1)
    conv_full = jnp.concatenate([conv_g[s][:DEPTH * 3] for s in range(NCHIP)], axis=1)
    conv_full = conv_full.reshape(DEPTH, 3, CW)

    bias_rows = jnp.pad(b_gates, ((0, 0), (0, LANE - 8)))
    wts, smalls = [], []
    for l in range(DEPTH):
        wts.append(_in_weights(win_g, l) + (wout_g, wg_g, wu_g, wd_g))
        smalls.append((norm_mix_w[l][None], bias_rows[l][None], mlstm_norm_w[l][None],
                       jnp.pad(conv_full[l], ((0, 5), (0, 0))), norm_ffn_w[l][None]))

    h = jnp.concatenate([meta_full, x[0], jnp.zeros((lp - n_real, D), F32)], axis=0)
    saved = []
    for l in range(DEPTH):
        h, sv = _layer_fwd(l, h, wts[l], smalls[l])
        saved.append(sv)
    tgt = jnp.pad(loss_target[0], ((N_META, lp - n_real), (0, 0)))
    dh, dhb, dnorm_final, loss_part = _loss_head(h, tgt, norm_final_w[None], n_real)

    names = ["w_in", "w_out", "w_gate", "w_up", "w_down"]
    params = {"w_in": (w_in, m_w_in, v_w_in), "w_out": (w_out, m_w_out, v_w_out), "w_gate": (w_gate, m_w_gate, v_w_gate),
              "w_up": (w_up, m_w_up, v_w_up), "w_down": (w_down, m_w_down, v_w_down)}
    big_out = {n: None for n in names}
    small_grads = [None] * DEPTH
    def finish(l, handle, after):
        for n, (g_mine, g_theirs) in zip(names, _rs_end(handle, after, pf)):
            w, m, v = params[n]
            big_out[n] = _adamw_layer("adamw_%s_l%d" % (n, l), l, g_mine, g_theirs, w, m, v, big_out[n], pf)

    pending = None
    for l in reversed(range(DEPTH)):
        dh, dhb, pieces, small_grads[l] = _layer_bwd(l, dh, dhb, saved[l], wts[l], smalls[l])
        handle, token = _rs_begin("l%d" % l, pieces, pf)
        if pending is not None:
            finish(pending[0], pending[1], token)
        pending = (l, handle)
        dhb = dhb + token[0, 0].astype(BF16)
    finish(pending[0], pending[1], big_out[names[-1]][3])

    dnmw = jnp.stack([small_grads[l][0] for l in range(DEPTH)])
    dbias = jnp.stack([small_grads[l][1] for l in range(DEPTH)])
    dconv = jnp.stack([small_grads[l][2] for l in range(DEPTH)])
    dmnw = jnp.stack([small_grads[l][3] for l in range(DEPTH)])
    dnfw = jnp.stack([small_grads[l][4] for l in range(DEPTH)])
    part_shapes = [(N_META, D), (DEPTH, D), (DEPTH, 8), (DEPTH, 3, CW), (DEPTH, MW), (DEPTH, D), (D,), (LANE,)]
    pack = _pack_rows([dh[:N_META], dnmw, dbias, dconv, dmnw, dnfw, dnorm_final[0], loss_part[0]])
    tot = _unpack_rows(_allreduce_small(pack), part_shapes)
    g_meta_full, g_nmw, g_bias, g_conv_full, g_mnw, g_nfw, g_final, loss_row = tot
    mcols = D // NCHIP
    ccols = CW // NCHIP
    g_meta = lax.dynamic_slice_in_dim(g_meta_full, jchip * mcols, mcols, axis=1)
    g_conv = lax.dynamic_slice_in_dim(g_conv_full, jchip * ccols, ccols, axis=2)
    sm_g = [g_meta, g_nmw, g_bias, g_conv, g_mnw, g_nfw, g_final]
    sm_w = [meta_tokens, norm_mix_w, b_gates, conv_w, mlstm_norm_w, norm_ffn_w, norm_final_w]
    sm_m = [m_meta_tokens, m_norm_mix_w, m_b_gates, m_conv_w, m_mlstm_norm_w, m_norm_ffn_w, m_norm_final_w]
    sm_v = [v_meta_tokens, v_norm_mix_w, v_b_gates, v_conv_w, v_mlstm_norm_w, v_norm_ffn_w, v_norm_final_w]
    sm_shapes = [w.shape for w in sm_w]
    d_p, m_p, v_p = _adamw_flat(_pack_rows(sm_g), _pack_rows(sm_w), _pack_rows(sm_m), _pack_rows(sm_v))
    sm_d = _unpack_rows(d_p, sm_shapes)
    sm_nm = _unpack_rows(m_p, sm_shapes)
    sm_nv = _unpack_rows(v_p, sm_shapes)

    loss = loss_row[0]
    grad_x = dh[N_META:n_real][None]

    def ordered(sm, which):
        bo = {n: big_out[n][which] for n in names}
        return [sm[0], sm[1], bo["w_in"], sm[2], sm[3], sm[4], bo["w_out"], sm[5], bo["w_gate"], bo["w_up"], bo["w_down"], sm[6]]

    return (loss, grad_x, *ordered(sm_g, 0), *ordered(sm_d, 1), *ordered(sm_nm, 2), *ordered(sm_nv, 3))
```

```python
import functools

import jax
import jax.numpy as jnp
from jax import lax
from jax.experimental import pallas as pl
from jax.experimental.pallas import tpu as pltpu

F32 = jnp.float32
BF16 = jnp.bfloat16

D = 2048
N_META = 16
HEADS = 4
DQK = 128
DV = 256
MW = HEADS * DV
CW = D - MW
QKW = HEADS * DQK
DFF = 5632
DIN = 6152
NCHIP = 4
DEPTH = 2
GATE_CAP = 15.0
EPS = 1e-6
QK_SCALE = DQK ** -0.5
LANE = 128
VMEM_LIMIT = 60 * 1024 * 1024

ADAM_LR = 0.001
ADAM_B1 = 0.9
ADAM_B2 = 0.999
ADAM_EPS = 1e-08
ADAM_WD = 0.01
ADAM_STEP = 10

MESH = pl.DeviceIdType.MESH
ANY = pl.BlockSpec(memory_space=pl.ANY)


def _cp(sem):
    return pltpu.CompilerParams(dimension_semantics=sem, vmem_limit_bytes=VMEM_LIMIT)


def _sds(shape, dtype):
    return jax.ShapeDtypeStruct(shape, dtype)


_DN = {"nn": (((1,), (0,)), ((), ())), "nt": (((1,), (1,)), ((), ())), "tn": (((0,), (0,)), ((), ()))}


def _mm(name, kind, a, b, a_spec, b_spec, o_spec, out_shape, grid, acc_shape=None, res=None, res_spec=None):
    nk = grid[2]
    has_res = res is not None

    def body(*refs):
        if has_res:
            a_ref, b_ref, r_ref, o_ref = refs[:4]
        else:
            a_ref, b_ref, o_ref = refs[:3]
            r_ref = None
        p = lax.dot_general(a_ref[...], b_ref[...], _DN[kind], preferred_element_type=F32)
        if nk == 1:
            if r_ref is not None:
                p = p + r_ref[...]
            o_ref[...] = p.astype(o_ref.dtype)
        else:
            acc = refs[-1]
            k = pl.program_id(2)

            @pl.when(k == 0)
            def _():
                acc[...] = p

            @pl.when(k > 0)
            def _():
                acc[...] += p

            @pl.when(k == nk - 1)
            def _():
                r = acc[...]
                if r_ref is not None:
                    r = r + r_ref[...]
                o_ref[...] = r.astype(o_ref.dtype)

    ins = [a, b] + ([res] if has_res else [])
    in_specs = [a_spec, b_spec] + ([res_spec] if has_res else [])
    scratch = [pltpu.VMEM(acc_shape, F32)] if nk > 1 else []
    return pl.pallas_call(
        body, name=name, grid=grid, in_specs=in_specs, out_specs=o_spec, out_shape=out_shape,
        scratch_shapes=scratch, compiler_params=_cp(("parallel", "parallel", "arbitrary")))(*ins)


def _norm_fwd(name, h, w):
    lp = h.shape[0]
    tm = lp // 4

    def body(h_ref, w_ref, o_ref):
        x = h_ref[...]
        r = lax.rsqrt(jnp.mean(x * x, axis=1, keepdims=True) + EPS)
        o_ref[...] = (x * r * w_ref[...]).astype(BF16)

    return pl.pallas_call(
        body, name=name, grid=(4,),
        in_specs=[pl.BlockSpec((tm, D), lambda i: (i, 0)), pl.BlockSpec((1, D), lambda i: (0, 0))],
        out_specs=pl.BlockSpec((tm, D), lambda i: (i, 0)), out_shape=_sds((lp, D), BF16),
        compiler_params=_cp(("parallel",)))(h, w)


def _norm_bwd(name, dy, h, w, dres):
    lp = h.shape[0]
    tm = lp // 8

    def body(dy_ref, h_ref, w_ref, dres_ref, dh_ref, dhb_ref, dw_ref):
        x = h_ref[...]
        r = lax.rsqrt(jnp.mean(x * x, axis=1, keepdims=True) + EPS)
        xh = x * r
        dy_v = dy_ref[...]
        dxh = dy_v * w_ref[...]
        dx = r * (dxh - xh * jnp.mean(dxh * xh, axis=1, keepdims=True))
        dh = dres_ref[...] + dx
        dh_ref[...] = dh
        dhb_ref[...] = dh.astype(BF16)

        @pl.when(pl.program_id(0) == 0)
        def _():
            dw_ref[...] = jnp.zeros_like(dw_ref)

        dw_ref[0:1, :] += jnp.sum(dy_v * xh, axis=0, keepdims=True)

    row = pl.BlockSpec((tm, D), lambda i: (i, 0))
    return pl.pallas_call(
        body, name=name, grid=(8,),
        in_specs=[row, row, pl.BlockSpec((1, D), lambda i: (0, 0)), row],
        out_specs=[row, row, pl.BlockSpec((8, D), lambda i: (0, 0))],
        out_shape=[_sds((lp, D), F32), _sds((lp, D), BF16), _sds((8, D), F32)],
        compiler_params=_cp(("arbitrary",)))(dy, h, w, dres)


def _loss_head(h, tgt, w, n_real):
    lp = h.shape[0]
    tm = lp // 8

    def body(h_ref, t_ref, w_ref, dh_ref, dhb_ref, dw_ref, loss_ref):
        i = pl.program_id(0)
        x = h_ref[...]
        r = lax.rsqrt(jnp.mean(x * x, axis=1, keepdims=True) + EPS)
        xh = x * r
        wv = w_ref[...]
        row = i * tm + lax.broadcasted_iota(jnp.int32, (tm, 1), 0)
        valid = jnp.logical_and(row >= N_META, row < n_real)
        err = jnp.where(valid, xh * wv - t_ref[...], 0.0)
        dy_v = err * (1.0 / D)
        dxh = dy_v * wv
        dx = r * (dxh - xh * jnp.mean(dxh * xh, axis=1, keepdims=True))
        dh_ref[...] = dx
        dhb_ref[...] = dx.astype(BF16)

        @pl.when(i == 0)
        def _():
            dw_ref[...] = jnp.zeros_like(dw_ref)
            loss_ref[...] = jnp.zeros_like(loss_ref)

        dw_ref[0:1, :] += jnp.sum(dy_v * xh, axis=0, keepdims=True)
        part = jnp.sum(jnp.sum(err * err, axis=1, keepdims=True), axis=0, keepdims=True) * (0.5 / D)
        loss_ref[...] += jnp.broadcast_to(part, loss_ref.shape)

    row = pl.BlockSpec((tm, D), lambda i: (i, 0))
    return pl.pallas_call(
        body, name="loss_head", grid=(8,),
        in_specs=[row, row, pl.BlockSpec((1, D), lambda i: (0, 0))],
        out_specs=[row, row, pl.BlockSpec((8, D), lambda i: (0, 0)), pl.BlockSpec((8, LANE), lambda i: (0, 0))],
        out_shape=[_sds((lp, D), F32), _sds((lp, D), BF16), _sds((8, D), F32), _sds((8, LANE), F32)],
        compiler_params=_cp(("arbitrary",)))(h, tgt, w)


def _ffn_up(name, hf, wg4, wu4, layer):
    lp = hf.shape[0]
    tm = lp // 4
    tn = DFF // NCHIP

    def body(x_ref, wg_ref, wu_ref, g_ref, u_ref, a_ref):
        x = x_ref[...]
        g = jnp.dot(x, wg_ref[...], preferred_element_type=F32)
        u = jnp.dot(x, wu_ref[...], preferred_element_type=F32)
        g_ref[...] = g.astype(BF16)
        u_ref[...] = u.astype(BF16)
        a_ref[...] = (g * jax.nn.sigmoid(g) * u).astype(BF16)

    wspec = pl.BlockSpec((None, None, D, tn), lambda j, i: (layer, j, 0, 0))
    ospec = pl.BlockSpec((tm, tn), lambda j, i: (i, j))
    o = _sds((lp, DFF), BF16)
    return pl.pallas_call(
        body, name=name, grid=(NCHIP, 4),
        in_specs=[pl.BlockSpec((tm, D), lambda j, i: (i, 0)), wspec, wspec],
        out_specs=[ospec, ospec, ospec], out_shape=[o, o, o],
        compiler_params=_cp(("parallel", "parallel")))(hf, wg4, wu4)


def _ffn_bwd_act(name, dhb, wd4, g, u, layer):
    lp = dhb.shape[0]
    tm = lp // 4
    tn = DFF // NCHIP

    def body(d_ref, w_ref, g_ref, u_ref, dg_ref, du_ref):
        da = lax.dot_general(d_ref[...], w_ref[...], _DN["nt"], preferred_element_type=F32)
        gv = g_ref[...].astype(F32)
        uv = u_ref[...].astype(F32)
        sg = jax.nn.sigmoid(gv)
        dg_ref[...] = (da * uv * (sg * (1.0 + gv * (1.0 - sg)))).astype(BF16)
        du_ref[...] = (da * (gv * sg)).astype(BF16)

    ospec = pl.BlockSpec((tm, tn), lambda j, i: (i, j))
    o = _sds((lp, DFF), BF16)
    return pl.pallas_call(
        body, name=name, grid=(NCHIP, 4),
        in_specs=[pl.BlockSpec((tm, D), lambda j, i: (i, 0)),
                  pl.BlockSpec((None, None, tn, D), lambda j, i: (layer, j, 0, 0)), ospec, ospec],
        out_specs=[ospec, ospec], out_shape=[o, o],
        compiler_params=_cp(("parallel", "parallel")))(dhb, wd4, g, u)


def _shift_rows(x, d, row):
    return jnp.where(row >= d, pltpu.roll(x, d, axis=0), 0.0)


def _scan_steps(lp):
    d = 1
    while d < lp:
        yield d
        d *= 2


def _gate_values(pre):
    t = GATE_CAP * jnp.tanh(pre * (1.0 / GATE_CAP))
    lf = jnp.minimum(t, 0.0) - jnp.log(1.0 + jnp.exp(-jnp.abs(t)))
    return t, lf


def _gate_prep(name, gates_pre, bias):
    lp = gates_pre.shape[0]

    def body(p_ref, b_ref, grow_ref, m_ref, c_ref):
        pre = p_ref[...] + b_ref[...]
        lane = lax.broadcasted_iota(jnp.int32, (lp, LANE), 1)
        row = lax.broadcasted_iota(jnp.int32, (lp, LANE), 0)
        t, lf = _gate_values(pre)
        f = jnp.where(jnp.logical_and(lane >= HEADS, lane < 2 * HEADS), lf, 0.0)
        for d in _scan_steps(lp):
            f = f + _shift_rows(f, d, row)
        fs = pltpu.roll(f, LANE - HEADS, axis=1)
        g = jnp.where(lane < HEADS, t - fs, 0.0)
        m = g
        for d in _scan_steps(lp):
            m = jnp.maximum(m, jnp.where(row >= d, pltpu.roll(m, d, axis=0), m))
        grow_ref[...] = g.T
        m_ref[...] = m
        c_ref[...] = jnp.where(lane < HEADS, -fs - m, 0.0)

    full = pl.BlockSpec((lp, LANE), lambda: (0, 0))
    return pl.pallas_call(
        body, name=name, in_specs=[full, pl.BlockSpec((1, LANE), lambda: (0, 0))],
        out_specs=[pl.BlockSpec((LANE, lp), lambda: (0, 0)), full, full],
        out_shape=[_sds((LANE, lp), F32), _sds((lp, LANE), F32), _sds((lp, LANE), F32)],
        compiler_params=pltpu.CompilerParams(vmem_limit_bytes=VMEM_LIMIT))(gates_pre, bias)


def _pick_lane(blk, h):
    lane = lax.broadcasted_iota(jnp.int32, blk.shape, 1)
    return jnp.sum(jnp.where(lane == h, blk, 0.0), axis=1, keepdims=True)


def _mlstm_weights(q, k, grow, mcol, i, bq, lp):
    s = lax.dot_general(q, k, _DN["nt"], preferred_element_type=F32) * QK_SCALE
    row = i * bq + lax.broadcasted_iota(jnp.int32, (bq, 1), 0)
    col = lax.broadcasted_iota(jnp.int32, (1, lp), 1)
    a = jnp.where(col <= row, jnp.exp(jnp.minimum(grow - mcol, 0.0)), 0.0)
    return s, a


def _mlstm_fwd(name, qkv, grow, mcol_all, ccol_all, nq):
    lp = qkv.shape[0]
    bq = lp // nq

    def body(q_ref, k_ref, v_ref, grow_ref, m_ref, c_ref, o_ref):
        h = pl.program_id(0)
        i = pl.program_id(1)
        grow_h = grow_ref[pl.ds(h, 1), :]
        mcol = _pick_lane(m_ref[...], h)
        ccol = _pick_lane(c_ref[...], h)
        s, a = _mlstm_weights(q_ref[...], k_ref[...], grow_h, mcol, i, bq, lp)
        p = a * s
        den = jnp.sum(p, axis=1, keepdims=True)
        num = jnp.dot(p.astype(BF16), v_ref[...], preferred_element_type=F32)
        o_ref[...] = num / jnp.maximum(jnp.abs(den), jnp.exp(ccol))

    return pl.pallas_call(
        body, name=name, grid=(HEADS, nq),
        in_specs=[pl.BlockSpec((bq, DQK), lambda h, i: (i, h)),
                  pl.BlockSpec((lp, DQK), lambda h, i: (0, HEADS + h)),
                  pl.BlockSpec((lp, DV), lambda h, i: (0, HEADS + h)),
                  pl.BlockSpec((8, lp), lambda h, i: (0, 0)),
                  pl.BlockSpec((bq, LANE), lambda h, i: (i, 0)),
                  pl.BlockSpec((bq, LANE), lambda h, i: (i, 0))],
        out_specs=pl.BlockSpec((bq, DV), lambda h, i: (i, h)),
        out_shape=_sds((lp, MW), F32),
        compiler_params=_cp(("parallel", "parallel")))(qkv, qkv, qkv, grow, mcol_all, ccol_all)


def _mlstm_bwd(name, qkv, grow, mcol_all, ccol_all, ht, dht, nq):
    lp = qkv.shape[0]
    bq = lp // nq

    def body(q_ref, k_ref, v_ref, grow_ref, m_ref, c_ref, ht_ref, dht_ref,
             dq_ref, dk_ref, dv_ref, dgrow_ref, dfx_ref, dkt_acc, dvt_acc):
        h = pl.program_id(0)
        i = pl.program_id(1)

        @pl.when(jnp.logical_and(h == 0, i == 0))
        def _():
            dgrow_ref[...] = jnp.zeros_like(dgrow_ref)
            dfx_ref[...] = jnp.zeros_like(dfx_ref)

        @pl.when(i == 0)
        def _():
            dkt_acc[...] = jnp.zeros_like(dkt_acc)
            dvt_acc[...] = jnp.zeros_like(dvt_acc)

        q = q_ref[...]
        k = k_ref[...]
        v = v_ref[...]
        grow_h = grow_ref[pl.ds(h, 1), :]
        mcol = _pick_lane(m_ref[...], h)
        ccol = _pick_lane(c_ref[...], h)
        s, a = _mlstm_weights(q, k, grow_h, mcol, i, bq, lp)
        p = a * s
        den = jnp.sum(p, axis=1, keepdims=True)
        clamp = jnp.exp(ccol)
        active = jnp.abs(den) < clamp
        dd = jnp.maximum(jnp.abs(den), clamp)
        dht_v = dht_ref[...]
        hdh = jnp.sum(dht_v * ht_ref[...], axis=1, keepdims=True)
        dn = (dht_v / dd).astype(BF16)
        dden = jnp.where(active, 0.0, -(hdh / dd) * jnp.sign(den))
        dp = lax.dot_general(dn, v, _DN["nt"], preferred_element_type=F32) + dden
        rmat = dp * p
        dgrow_ref[pl.ds(h, 1), :] += jnp.sum(rmat, axis=0, keepdims=True)
        ds = (dp * a * QK_SCALE).astype(BF16)
        dq_ref[...] = jnp.dot(ds, k, preferred_element_type=F32).astype(BF16)
        dkt_acc[...] += lax.dot_general(q, ds, _DN["tn"], preferred_element_type=F32)
        dvt_acc[...] += lax.dot_general(dn, p.astype(BF16), _DN["tn"], preferred_element_type=F32)
        lane = lax.broadcasted_iota(jnp.int32, (bq, LANE), 1)
        r0 = pl.multiple_of(i * bq, 16)
        dfx_ref[pl.ds(r0, bq), :] += jnp.where(lane == h, jnp.sum(rmat, axis=1, keepdims=True), 0.0)

        @pl.when(i == nq - 1)
        def _():
            dk_ref[...] = dkt_acc[...].T.astype(BF16)
            dv_ref[...] = dvt_acc[...].T.astype(BF16)

    return pl.pallas_call(
        body, name=name, grid=(HEADS, nq),
        in_specs=[pl.BlockSpec((bq, DQK), lambda h, i: (i, h)),
                  pl.BlockSpec((lp, DQK), lambda h, i: (0, HEADS + h)),
                  pl.BlockSpec((lp, DV), lambda h, i: (0, HEADS + h)),
                  pl.BlockSpec((8, lp), lambda h, i: (0, 0)),
                  pl.BlockSpec((bq, LANE), lambda h, i: (i, 0)),
                  pl.BlockSpec((bq, LANE), lambda h, i: (i, 0)),
                  pl.BlockSpec((bq, DV), lambda h, i: (i, h)),
                  pl.BlockSpec((bq, DV), lambda h, i: (i, h))],
        out_specs=[pl.BlockSpec((bq, DQK), lambda h, i: (i, h)),
                   pl.BlockSpec((lp, DQK), lambda h, i: (0, h)),
                   pl.BlockSpec((lp, DV), lambda h, i: (0, h)),
                   pl.BlockSpec((LANE, lp), lambda h, i: (0, 0)),
                   pl.BlockSpec((lp, LANE), lambda h, i: (0, 0))],
        out_shape=[_sds((lp, QKW), BF16), _sds((lp, QKW), BF16), _sds((lp, MW), BF16),
                   _sds((LANE, lp), F32), _sds((lp, LANE), F32)],
        scratch_shapes=[pltpu.VMEM((DQK, lp), F32), pltpu.VMEM((DV, lp), F32)],
        compiler_params=_cp(("arbitrary", "arbitrary")))(qkv, qkv, qkv, grow, mcol_all, ccol_all, ht, dht)


def _gate_bwd(name, gates_pre, bias, dgrow, dfx):
    lp = gates_pre.shape[0]

    def body(p_ref, b_ref, dgrow_ref, dfx_ref, dg_ref, dgb_ref, db_ref):
        pre = p_ref[...] + b_ref[...]
        lane = lax.broadcasted_iota(jnp.int32, (lp, LANE), 1)
        row = lax.broadcasted_iota(jnp.int32, (lp, LANE), 0)
        th = jnp.tanh(pre * (1.0 / GATE_CAP))
        t = GATE_CAP * th
        dgc = jnp.where(lane < HEADS, dgrow_ref[...].T, 0.0)
        df = jnp.where(lane < HEADS, dfx_ref[...] - dgc, 0.0)
        for d in _scan_steps(lp):
            df = df + jnp.where(row < lp - d, pltpu.roll(df, lp - d, axis=0), 0.0)
        dlf = pltpu.roll(df, HEADS, axis=1)
        dt = jnp.where(lane < HEADS, dgc, dlf * jax.nn.sigmoid(-t))
        dpre = jnp.where(lane < 2 * HEADS, dt * (1.0 - th * th), 0.0)
        dg_ref[...] = dpre
        dgb_ref[...] = dpre.astype(BF16)
        db_ref[...] = jnp.broadcast_to(jnp.sum(dpre, axis=0, keepdims=True), db_ref.shape)

    full = pl.BlockSpec((lp, LANE), lambda: (0, 0))
    return pl.pallas_call(
        body, name=name,
        in_specs=[full, pl.BlockSpec((1, LANE), lambda: (0, 0)), pl.BlockSpec((LANE, lp), lambda: (0, 0)), full],
        out_specs=[full, full, pl.BlockSpec((8, LANE), lambda: (0, 0))],
        out_shape=[_sds((lp, LANE), F32), _sds((lp, LANE), BF16), _sds((8, LANE), F32)],
        compiler_params=pltpu.CompilerParams(vmem_limit_bytes=VMEM_LIMIT))(gates_pre, bias, dgrow, dfx)


CB = 256


def _e_specs(lp):
    return [pl.BlockSpec((None, lp, CB), functools.partial(lambda c, j: (c, 0, j), c)) for c in range(4)]


def _mix_fwd(name, ht, e, mnw, cw):
    lp = ht.shape[0]

    def body(ht_ref, og_ref, u_ref, gb_ref, gc_ref, mnw_ref, cw_ref, o_ref):
        x = ht_ref[...]
        r = lax.rsqrt(jnp.mean(x * x, axis=1, keepdims=True) + EPS)
        o_ref[0] = (jax.nn.sigmoid(og_ref[...].astype(F32)) * (x * r * mnw_ref[...])).astype(BF16)
        row = lax.broadcasted_iota(jnp.int32, (lp, CB), 0)
        a = gc_ref[...].astype(F32) * u_ref[...].astype(F32)
        conv = cw_ref[2:3, :] * a + cw_ref[1:2, :] * _shift_rows(a, 1, row) + cw_ref[0:1, :] * _shift_rows(a, 2, row)
        o_ref[1] = (gb_ref[...].astype(F32) * conv).astype(BF16)

    col = pl.BlockSpec((lp, CB), lambda j: (0, j))
    return pl.pallas_call(
        body, name=name, grid=(4,),
        in_specs=[col] + _e_specs(lp) + [pl.BlockSpec((1, CB), lambda j: (0, j)), pl.BlockSpec((8, CB), lambda j: (0, j))],
        out_specs=pl.BlockSpec((2, lp, CB), lambda j: (0, 0, j)), out_shape=_sds((2, lp, MW), BF16),
        compiler_params=_cp(("parallel",)))(ht, e, e, e, e, mnw, cw)


def _mix_bwd(name, dmix, ht, e, mnw, cw):
    lp = ht.shape[0]

    def body(dhm_ref, dhc_ref, ht_ref, og_ref, u_ref, gb_ref, gc_ref, mnw_ref, cw_ref,
             dht_ref, de_ref, dmnw_ref, dcw_ref):
        x = ht_ref[...]
        r = lax.rsqrt(jnp.mean(x * x, axis=1, keepdims=True) + EPS)
        xh = x * r
        w = mnw_ref[...]
        sg = jax.nn.sigmoid(og_ref[...].astype(F32))
        dhm = dhm_ref[...]
        de_ref[0] = (dhm * (xh * w) * (sg * (1.0 - sg))).astype(BF16)
        dn = dhm * sg
        dmnw_ref[...] = jnp.broadcast_to(jnp.sum(dn * xh, axis=0, keepdims=True), dmnw_ref.shape)
        dxh = dn * w
        dht_ref[...] = r * (dxh - xh * jnp.mean(dxh * xh, axis=1, keepdims=True))

        row = lax.broadcasted_iota(jnp.int32, (lp, CB), 0)
        uv = u_ref[...].astype(F32)
        gcv = gc_ref[...].astype(F32)
        gbv = gb_ref[...].astype(F32)
        a = gcv * uv
        a1 = _shift_rows(a, 1, row)
        a2 = _shift_rows(a, 2, row)
        dhc = dhc_ref[...]
        conv = cw_ref[2:3, :] * a + cw_ref[1:2, :] * a1 + cw_ref[0:1, :] * a2
        de_ref[2] = (dhc * conv).astype(BF16)
        dconv = dhc * gbv
        dcw_ref[...] = jnp.zeros_like(dcw_ref)
        dcw_ref[0:1, :] = jnp.sum(dconv * a2, axis=0, keepdims=True)
        dcw_ref[1:2, :] = jnp.sum(dconv * a1, axis=0, keepdims=True)
        dcw_ref[2:3, :] = jnp.sum(dconv * a, axis=0, keepdims=True)
        up1 = jnp.where(row < lp - 1, pltpu.roll(dconv, lp - 1, axis=0), 0.0)
        up2 = jnp.where(row < lp - 2, pltpu.roll(dconv, lp - 2, axis=0), 0.0)
        da = cw_ref[2:3, :] * dconv + cw_ref[1:2, :] * up1 + cw_ref[0:1, :] * up2
        de_ref[1] = (da * gcv).astype(BF16)
        de_ref[3] = (da * uv).astype(BF16)

    col = pl.BlockSpec((lp, CB), lambda j: (0, j))
    small = pl.BlockSpec((8, CB), lambda j: (0, j))
    return pl.pallas_call(
        body, name=name, grid=(4,),
        in_specs=[col, pl.BlockSpec((lp, CB), lambda j: (0, 4 + j)), col] + _e_specs(lp)
                 + [pl.BlockSpec((1, CB), lambda j: (0, j)), small],
        out_specs=[col, pl.BlockSpec((4, lp, CB), lambda j: (0, 0, j)), small, small],
        out_shape=[_sds((lp, MW), F32), _sds((4, lp, MW), BF16), _sds((8, MW), F32), _sds((8, CW), F32)],
        compiler_params=_cp(("parallel",)))(dmix, dmix, ht, e, e, e, e, mnw, cw)


def _row_tile(r, c, itemsize, budget=1536 * 1024, mult=16):
    best = None
    for t in range(mult, r + 1, mult):
        if r % t == 0 and t * c * itemsize <= budget:
            best = t
    if best is None:
        best = r
    return best


def _grid_spec(grid, in_specs, out_specs, scratch=()):
    return pltpu.PrefetchScalarGridSpec(num_scalar_prefetch=1, grid=grid, in_specs=in_specs,
                                        out_specs=out_specs, scratch_shapes=list(scratch))


def _cast_into(name, w, pf):
    _, r, c = w.shape
    tr = _row_tile(r, c, 4)

    def body(pf_ref, x_ref, o_ref):
        o_ref[...] = x_ref[...].astype(BF16)

    return pl.pallas_call(
        body, name=name, out_shape=_sds((DEPTH, NCHIP, r, c), BF16),
        grid_spec=_grid_spec((DEPTH, r // tr), [pl.BlockSpec((None, tr, c), lambda l, i, pf: (l, i, 0))],
                             pl.BlockSpec((None, None, tr, c), lambda l, i, pf: (l, pf[1], i, 0))),
        compiler_params=_cp(("parallel", "parallel")))(pf, w)


def _add2_bf16(name, dw, got, pf):
    n4, r2, c = got.shape
    tr = _row_tile(r2, c, 4)
    nch = r2 // tr

    def body(pf_ref, a_ref, b_ref, o_ref):
        o_ref[...] = (a_ref[...].astype(F32) + b_ref[...].astype(F32)).astype(BF16)

    spec = pl.BlockSpec((None, tr, c), lambda s, i, pf: (s, i, 0))
    return pl.pallas_call(
        body, name=name, out_shape=_sds((n4, r2, c), BF16),
        grid_spec=_grid_spec((n4, nch), [pl.BlockSpec((None, tr, c), lambda s, i, pf: (s, pf[0] * nch + i, 0)), spec], spec),
        compiler_params=_cp(("parallel", "parallel")))(pf, dw, got)


def _adam_math(w, g, m, v):
    m2 = ADAM_B1 * m + (1.0 - ADAM_B1) * g
    v2 = ADAM_B2 * v + (1.0 - ADAM_B2) * (g * g)
    m_hat = m2 / (1.0 - ADAM_B1 ** ADAM_STEP)
    v_hat = v2 / (1.0 - ADAM_B2 ** ADAM_STEP)
    delta = -ADAM_LR * (m_hat / (jnp.sqrt(v_hat) + ADAM_EPS) + ADAM_WD * w)
    return delta, m2, v2


def _adamw_layer(name, layer, g_mine, g_theirs, w, m, v, prev, pf):
    _, r, c = w.shape
    r2 = r // 2
    tr = _row_tile(r2, c, 4, budget=1024 * 1024, mult=8)
    nch = r2 // tr
    n_alias = 0 if prev is None else 4

    def body(*refs):
        pf_ref, gm_ref, gt_ref, w_ref, m_ref, v_ref = refs[:6]
        go_ref, d_ref, mo_ref, vo_ref = refs[6 + n_alias:]
        mine = (pl.program_id(0) // nch) == pf_ref[0]
        gv = jnp.where(mine, gm_ref[...], gt_ref[...])
        delta, m2, v2 = _adam_math(w_ref[...], gv, m_ref[...], v_ref[...])
        go_ref[...] = gv
        d_ref[...] = delta
        mo_ref[...] = m2
        vo_ref[...] = v2

    slab = pl.BlockSpec((None, tr, c), lambda i, pf: (layer, i, 0))
    ins = [g_mine, g_theirs, w, m, v] + (list(prev) if prev is not None else [])
    in_specs = [pl.BlockSpec((tr, c), lambda i, pf: (jnp.clip(i - pf[0] * nch, 0, nch - 1), 0)),
                pl.BlockSpec((tr, c), lambda i, pf: (jnp.clip(i - (1 - pf[0]) * nch, 0, nch - 1), 0)),
                slab, slab, slab] + [ANY] * n_alias
    o = _sds(w.shape, F32)
    return pl.pallas_call(
        body, name=name, out_shape=[o] * 4, grid_spec=_grid_spec((2 * nch,), in_specs, [slab] * 4),
        input_output_aliases={6 + k: k for k in range(n_alias)},
        compiler_params=_cp(("parallel",)))(pf, *ins)


def _adamw_flat(g, w, m, v):
    def body(g_ref, w_ref, m_ref, v_ref, d_ref, mo_ref, vo_ref):
        delta, m2, v2 = _adam_math(w_ref[...], g_ref[...], m_ref[...], v_ref[...])
        d_ref[...] = delta
        mo_ref[...] = m2
        vo_ref[...] = v2

    o = _sds(w.shape, F32)
    return pl.pallas_call(body, name="adamw_small", out_shape=[o, o, o])(g, w, m, v)


def _place():
    x, y, c = lax.axis_index("x"), lax.axis_index("y"), lax.axis_index("c")
    chips = [(1 - x, y), (x, 1 - y), (1 - x, 1 - y)]
    return x, y, c, chips


def _rs_chips(name, ps):
    n = len(ps)

    def body(*refs):
        ins = refs[:n]
        got = refs[n:2 * n]
        send, recv = refs[2 * n:]
        x, y, c, chips = _place()
        cps = []
        for t in range(n):
            for k, chip in enumerate(chips):
                jk = 2 * chip[0] + chip[1]
                cp = pltpu.make_async_remote_copy(
                    src_ref=ins[t].at[jk], dst_ref=got[t].at[k],
                    send_sem=send.at[3 * t + k], recv_sem=recv.at[3 * t + k],
                    device_id=(*chip, c), device_id_type=MESH)
                cp.start()
                cps.append(cp)
        for cp in cps:
            cp.wait()

    dma = pltpu.SemaphoreType.DMA
    return pl.pallas_call(
        body, name=name, in_specs=[ANY] * n, out_specs=[ANY] * n,
        out_shape=[_sds((3,) + p.shape[1:], BF16) for p in ps],
        scratch_shapes=[dma((3 * n,)), dma((3 * n,))],
        compiler_params=pltpu.CompilerParams(has_side_effects=True))(*ps)


HBM = pl.BlockSpec(memory_space=pltpu.HBM)
SEM = pl.BlockSpec(memory_space=pltpu.SEMAPHORE)
EFFECT = pltpu.SideEffectType.DATAFLOW_SIDE_EFFECTING


def _in_hbm(a):
    return pltpu.with_memory_space_constraint(a, pltpu.HBM)


def _rs_chips_copies(ins, lands, send, recv):
    x, y, c, chips = _place()
    cps = []
    for t in range(len(ins)):
        for k, chip in enumerate(chips):
            jk = 2 * chip[0] + chip[1]
            cps.append(pltpu.make_async_remote_copy(
                src_ref=ins[t].at[jk], dst_ref=lands[t].at[k], send_sem=send.at[3 * t + k],
                recv_sem=recv.at[3 * t + k], device_id=(*chip, c), device_id_type=MESH))
    return cps


def _rs_chips_start(name, ps):
    n = len(ps)

    def body(*refs):
        ins, lands = refs[:n], refs[n:2 * n]
        send, recv = refs[2 * n], refs[2 * n + 1]
        token = refs[-1]
        for cp in _rs_chips_copies(ins, lands, send, recv):
            cp.start()
        token[...] = jnp.zeros_like(token)

    dma = pltpu.SemaphoreType.DMA
    lands = [lax.empty((3,) + p.shape[1:], BF16) for p in ps]
    out_shape = ([dma((3 * n,)), dma((3 * n,))] + [pltpu.HBM(p.shape, BF16) for p in ps]
                 + [pltpu.HBM(z.shape, BF16) for z in lands] + [_sds((8, LANE), F32)])
    outs = pl.pallas_call(
        body, name=name, out_shape=out_shape, in_specs=[HBM] * (2 * n),
        out_specs=[SEM, SEM] + [HBM] * (2 * n) + [pl.BlockSpec(memory_space=pltpu.VMEM)],
        input_output_aliases={i: 2 + i for i in range(2 * n)},
        compiler_params=pltpu.CompilerParams(has_side_effects=EFFECT))(
            *[_in_hbm(p) for p in ps], *[_in_hbm(z) for z in lands])
    return outs[0], outs[1], outs[2:2 + n], outs[2 + n:2 + 2 * n], outs[-1]


def _rs_chips_wait(name, send, recv, ps, lands, afters):
    n = len(ps)

    def body(*refs):
        ins, zones = refs[:n], refs[n:2 * n]
        send_ref, recv_ref = refs[2 * n], refs[2 * n + 1]
        for cp in _rs_chips_copies(ins, zones, send_ref, recv_ref):
            cp.wait_send()
            cp.wait_recv()

    outs = pl.pallas_call(
        body, name=name, out_shape=[pltpu.HBM(p.shape, BF16) for p in ps] + [pltpu.HBM(z.shape, BF16) for z in lands],
        in_specs=[HBM] * (2 * n) + [SEM, SEM] + [ANY] * len(afters), out_specs=[HBM] * (2 * n),
        input_output_aliases={i: i for i in range(2 * n)},
        compiler_params=pltpu.CompilerParams(has_side_effects=EFFECT))(*ps, *lands, send, recv, *afters)
    return outs[:n], outs[n:]


def _sibling():
    x, y, c, _ = _place()
    return (x, y, 1 - c)


def _pair_send(name, dw, pf):
    n4, r, c = dw.shape
    half = r // 2

    def body(pf_ref, x_ref, got_ref, ssem, rsem):
        s = pl.program_id(0)
        cp = pltpu.make_async_remote_copy(src_ref=x_ref, dst_ref=got_ref.at[pl.ds(s, 1)], send_sem=ssem,
                                          recv_sem=rsem, device_id=_sibling(), device_id_type=MESH)
        cp.start()
        cp.wait_send()

        @pl.when(s == n4 - 1)
        def _():
            pltpu.make_async_remote_copy(src_ref=got_ref, dst_ref=got_ref, send_sem=ssem, recv_sem=rsem,
                                         device_id=_sibling(), device_id_type=MESH).wait_recv()

    dma = pltpu.SemaphoreType.DMA
    return pl.pallas_call(
        body, name=name, out_shape=_sds((n4, half, c), BF16),
        grid_spec=_grid_spec((n4,), [pl.BlockSpec((1, half, c), lambda s, pf: (s, 1 - pf[0], 0))], ANY,
                             scratch=[dma(()), dma(())]),
        compiler_params=pltpu.CompilerParams(dimension_semantics=("arbitrary",), has_side_effects=True,
                                             vmem_limit_bytes=VMEM_LIMIT))(pf, dw)


def _add4_join(name, p, got, pf):
    n4, r2, c = p.shape
    tr = _row_tile(r2, c, 4)
    nch = r2 // tr

    def body(pf_ref, p_ref, g_ref, mine_ref, theirs_ref, ssem, rsem):
        i = pl.program_id(0)
        s = p_ref[...].astype(F32)
        for k in range(3):
            s = s + g_ref[k].astype(F32)
        mine_ref[...] = s
        r0 = pl.multiple_of(i * tr, 8)
        cp = pltpu.make_async_remote_copy(src_ref=mine_ref, dst_ref=theirs_ref.at[pl.ds(r0, tr), :],
                                          send_sem=ssem, recv_sem=rsem, device_id=_sibling(), device_id_type=MESH)
        cp.start()
        cp.wait_send()

        @pl.when(i == nch - 1)
        def _():
            pltpu.make_async_remote_copy(src_ref=theirs_ref, dst_ref=theirs_ref, send_sem=ssem, recv_sem=rsem,
                                         device_id=_sibling(), device_id_type=MESH).wait_recv()

    dma = pltpu.SemaphoreType.DMA
    o = _sds((r2, c), F32)
    return pl.pallas_call(
        body, name=name, out_shape=[o, o],
        grid_spec=_grid_spec((nch,), [pl.BlockSpec((None, tr, c), lambda i, pf: (pf[1], i, 0)),
                                      pl.BlockSpec((3, tr, c), lambda i, pf: (0, i, 0))],
                             [pl.BlockSpec((tr, c), lambda i, pf: (i, 0)), ANY], scratch=[dma(()), dma(())]),
        compiler_params=pltpu.CompilerParams(dimension_semantics=("arbitrary",), has_side_effects=True,
                                             vmem_limit_bytes=VMEM_LIMIT))(pf, p, got)


def _allgather(gs, smalls):
    nb, ns = len(gs), len(smalls)
    halves = [g.shape[2] // 2 for g in gs]

    def body(*refs):
        s_in = refs[nb:nb + ns]
        g = refs[nb + ns:2 * nb + ns]
        s_out = refs[2 * nb + ns:2 * (nb + ns)]
        scr = refs[2 * (nb + ns):]
        bufs = scr[:nb]
        send, recv, fsend, frecv, lsem, ssend, srecv, slsem = scr[nb:]
        x, y, c, chips = _place()
        j = 2 * x + y
        sends, slocal = [], []
        for t in range(nb):
            rows = pl.ds(c * halves[t], halves[t])
            piece = g[t].at[:, j, rows, :]
            for k, chip in enumerate(chips):
                cp = pltpu.make_async_remote_copy(
                    src_ref=piece, dst_ref=piece, send_sem=send.at[3 * t + k], recv_sem=recv.at[3 * t + k],
                    device_id=(*chip, c), device_id_type=MESH)
                cp.start()
                sends.append(cp)
        for t in range(ns):
            cp = pltpu.make_async_copy(s_in[t], s_out[t].at[j], slsem.at[t])
            cp.start()
            slocal.append(cp)
            for k, chip in enumerate(chips):
                cp = pltpu.make_async_remote_copy(
                    src_ref=s_in[t], dst_ref=s_out[t].at[j], send_sem=ssend.at[3 * t + k],
                    recv_sem=srecv.at[3 * t + k], device_id=(*chip, c), device_id_type=MESH)
                cp.start()
                sends.append(cp)
        for t in range(nb):
            rows = pl.ds(c * halves[t], halves[t])
            pend = [None, None]
            n = 0
            for k, chip in enumerate(chips):
                jk = 2 * chip[0] + chip[1]
                landed = g[t].at[:, jk, rows, :]
                pltpu.make_async_remote_copy(
                    src_ref=landed, dst_ref=landed, send_sem=send.at[3 * t + k], recv_sem=recv.at[3 * t + k],
                    device_id=(*chip, c), device_id_type=MESH).wait_recv()
                for l in range(DEPTH):
                    slot = n % 2
                    if pend[slot] is not None:
                        pend[slot].wait_send()
                    part = g[t].at[l, jk, rows, :]
                    ld = pltpu.make_async_copy(part, bufs[t].at[slot], lsem.at[2 * t + slot])
                    ld.start()
                    ld.wait()
                    cp = pltpu.make_async_remote_copy(
                        src_ref=bufs[t].at[slot], dst_ref=part, send_sem=fsend.at[2 * t + slot],
                        recv_sem=frecv.at[t], device_id=(x, y, 1 - c), device_id_type=MESH)
                    cp.start()
                    pend[slot] = cp
                    n += 1
            for cp in pend:
                cp.wait_send()
        for t in range(ns):
            for k, chip in enumerate(chips):
                jk = 2 * chip[0] + chip[1]
                landed = s_out[t].at[jk]
                pltpu.make_async_remote_copy(
                    src_ref=landed, dst_ref=landed, send_sem=ssend.at[3 * t + k], recv_sem=srecv.at[3 * t + k],
                    device_id=(*chip, c), device_id_type=MESH).wait_recv()
        for t in range(nb):
            passed = g[t].at[:, pl.ds(0, 3), pl.ds((1 - c) * halves[t], halves[t]), :]
            pltpu.make_async_remote_copy(
                src_ref=passed, dst_ref=passed, send_sem=fsend.at[2 * t], recv_sem=frecv.at[t],
                device_id=(x, y, 1 - c), device_id_type=MESH).wait_recv()
        for cp in sends:
            cp.wait_send()
        for cp in slocal:
            cp.wait()

    dma = pltpu.SemaphoreType.DMA
    out_shape = [_sds(g.shape, g.dtype) for g in gs] + [_sds((NCHIP,) + s.shape, s.dtype) for s in smalls]
    scratch = [pltpu.VMEM((2, halves[t], gs[t].shape[3]), BF16) for t in range(nb)]
    scratch += [dma((3 * nb,)), dma((3 * nb,)), dma((2 * nb,)), dma((nb,)), dma((2 * nb,)),
                dma((3 * ns,)), dma((3 * ns,)), dma((ns,))]
    return pl.pallas_call(
        body, name="allgather_weights", in_specs=[ANY] * (nb + ns), out_specs=[ANY] * (nb + ns),
        out_shape=out_shape, scratch_shapes=scratch, input_output_aliases={t: t for t in range(nb)},
        compiler_params=pltpu.CompilerParams(has_side_effects=True, vmem_limit_bytes=VMEM_LIMIT))(*gs, *smalls)


def _reduce_scatter(tag, dws, pf):
    ps = []
    for t, dw in enumerate(dws):
        got = _pair_send("rs_pair_%s_%d" % (tag, t), dw, pf)
        ps.append(_add2_bf16("rs_add2_%s_%d" % (tag, t), dw, got, pf))
    got2 = _rs_chips("rs_chips_" + tag, ps)
    return [_add4_join("rs_add4_%s_%d" % (tag, t), p, g2, pf) for t, (p, g2) in enumerate(zip(ps, got2))]


def _rs_begin(tag, dws, pf):
    ps = []
    for t, dw in enumerate(dws):
        got = _pair_send("rs_pair_%s_%d" % (tag, t), dw, pf)
        ps.append(_add2_bf16("rs_add2_%s_%d" % (tag, t), dw, got, pf))
    send, recv, ps_thru, lands, token = _rs_chips_start("rs_chips_start_" + tag, ps)
    return (tag, send, recv, ps_thru, lands), token


def _rs_end(handle, afters, pf):
    tag, send, recv, ps, lands = handle
    ps, got2 = _rs_chips_wait("rs_chips_wait_" + tag, send, recv, ps, lands, afters)
    return [_add4_join("rs_add4_%s_%d" % (tag, t), p, g2, pf) for t, (p, g2) in enumerate(zip(ps, got2))]


def _allreduce_small(pack):
    r = pack.shape[0]
    flips = [(fx, fy, fc) for fx in (0, 1) for fy in (0, 1) for fc in (0, 1)][1:]

    def body(p_ref, o_ref, gat, send, recv):
        x, y, c, _ = _place()
        me = 4 * x + 2 * y + c
        gat[me] = p_ref[...]
        cps = []
        for k, (fx, fy, fc) in enumerate(flips):
            peer = ((1 - x) if fx else x, (1 - y) if fy else y, (1 - c) if fc else c)
            cp = pltpu.make_async_remote_copy(
                src_ref=p_ref, dst_ref=gat.at[me], send_sem=send.at[k], recv_sem=recv.at[k],
                device_id=peer, device_id_type=MESH)
            cp.start()
            cps.append(cp)
        for k, (fx, fy, fc) in enumerate(flips):
            peer = ((1 - x) if fx else x, (1 - y) if fy else y, (1 - c) if fc else c)
            src = 4 * peer[0] + 2 * peer[1] + peer[2]
            pltpu.make_async_remote_copy(
                src_ref=p_ref, dst_ref=gat.at[src], send_sem=send.at[k], recv_sem=recv.at[k],
                device_id=peer, device_id_type=MESH).wait_recv()
        for cp in cps:
            cp.wait_send()
        s = gat[0]
        for d in range(1, 8):
            s = s + gat[d]
        o_ref[...] = s

    dma = pltpu.SemaphoreType.DMA
    vm = pl.BlockSpec(memory_space=pltpu.VMEM)
    return pl.pallas_call(
        body, name="allreduce_small", in_specs=[vm], out_specs=vm, out_shape=_sds((r, LANE), F32),
        scratch_shapes=[pltpu.VMEM((8, r, LANE), F32), dma((7,)), dma((7,))],
        compiler_params=pltpu.CompilerParams(has_side_effects=True))(pack)


def _in_weights(win_g, l):
    full = jnp.concatenate([win_g[l, s] for s in range(NCHIP)], axis=1)
    wqkv = full[:, :2048]
    og = full[:, 2048:3072]
    gates = jnp.pad(full[:, 3072:3080], ((0, 0), (0, LANE - 8)))
    u = full[:, 3080:4104]
    gb = full[:, 4104:5128]
    gc = full[:, 5128:6152]
    return wqkv, jnp.stack([og, u, gb, gc]), gates


def _in_grads(dwqkv, dwe, dwgt):
    full = jnp.concatenate([dwqkv, dwe[0], dwgt[:, :8], dwe[1], dwe[2], dwe[3]], axis=1)
    sw = DIN // NCHIP
    return jnp.stack([full[:, s * sw:(s + 1) * sw] for s in range(NCHIP)])


def _layer_fwd(l, h, wts, small):
    lp = h.shape[0]
    th = lp // 2
    wqkv, we, wgt, wout_g, wg_g, wu_g, wd_g = wts
    nmw, bias, mnw, cw, nfw = small
    tag = "_l%d" % l
    hn = _norm_fwd("norm_mix" + tag, h, nmw)
    qkv = _mm("proj_qkv" + tag, "nn", hn, wqkv,
              pl.BlockSpec((lp, D), lambda i, j, k: (0, 0)), pl.BlockSpec((D, 512), lambda i, j, k: (0, j)),
              pl.BlockSpec((lp, 512), lambda i, j, k: (0, j)), _sds((lp, 2048), BF16), (1, 4, 1))
    e = _mm("proj_e" + tag, "nn", hn, we,
            pl.BlockSpec((lp, D), lambda i, j, k: (0, 0)), pl.BlockSpec((None, D, 512), lambda i, j, k: (j // 2, 0, j % 2)),
            pl.BlockSpec((None, lp, 512), lambda i, j, k: (j // 2, 0, j % 2)), _sds((4, lp, 1024), BF16), (1, 8, 1))
    gpre = _mm("proj_gates" + tag, "nn", hn, wgt,
               pl.BlockSpec((lp, D), lambda i, j, k: (0, 0)), pl.BlockSpec((D, LANE), lambda i, j, k: (0, 0)),
               pl.BlockSpec((lp, LANE), lambda i, j, k: (0, 0)), _sds((lp, LANE), F32), (1, 1, 1))
    grow, mcol, ccol = _gate_prep("gate_prep" + tag, gpre, bias)
    ht = _mlstm_fwd("mlstm_fwd" + tag, qkv, grow, mcol, ccol, 4)
    mix = _mix_fwd("mix_fwd" + tag, ht, e, mnw, cw)
    wout = wout_g.reshape(DEPTH, D, D)
    h1 = _mm("out_proj" + tag, "nn", mix, wout,
             pl.BlockSpec((None, th, 1024), lambda i, j, k: (k, i, 0)),
             pl.BlockSpec((None, 1024, 1024), lambda i, j, k: (l, k, j)),
             pl.BlockSpec((th, 1024), lambda i, j, k: (i, j)), _sds((lp, D), F32), (2, 2, 2),
             acc_shape=(th, 1024), res=h, res_spec=pl.BlockSpec((th, 1024), lambda i, j, k: (i, j)))
    hf = _norm_fwd("norm_ffn" + tag, h1, nfw)
    g, u, a = _ffn_up("ffn_up" + tag, hf, wg_g, wu_g, l)
    tk = DFF // NCHIP
    h2 = _mm("ffn_down" + tag, "nn", a, wd_g,
             pl.BlockSpec((th, tk), lambda i, j, k: (i, k)),
             pl.BlockSpec((None, None, tk, 1024), lambda i, j, k: (l, k, 0, j)),
             pl.BlockSpec((th, 1024), lambda i, j, k: (i, j)), _sds((lp, D), F32), (2, 2, NCHIP),
             acc_shape=(th, 1024), res=h1, res_spec=pl.BlockSpec((th, 1024), lambda i, j, k: (i, j)))
    saved = (h, hn, qkv, e, gpre, grow, mcol, ccol, ht, mix, h1, hf, g, u, a)
    return h2, saved


def _layer_bwd(l, dh2, dh2b, saved, wts, small, ffn_done):
    h, hn, qkv, e, gpre, grow, mcol, ccol, ht, mix, h1, hf, g, u, a = saved
    wqkv, we, wgt, wout_g, wg_g, wu_g, wd_g = wts
    nmw, bias, mnw, cw, nfw = small
    lp = h.shape[0]
    th = lp // 2
    tk = DFF // NCHIP
    tag = "_l%d" % l
    half_rows = lambda i, j, k: (i, j)

    dwd = _mm("dw_down" + tag, "tn", a, dh2b,
              pl.BlockSpec((lp, tk), lambda i, j, k: (0, i)), pl.BlockSpec((lp, 1024), lambda i, j, k: (0, j)),
              pl.BlockSpec((None, tk, 1024), lambda i, j, k: (i, 0, j)), _sds((NCHIP, tk, D), BF16), (NCHIP, 2, 1))
    dg, du = _ffn_bwd_act("ffn_bwd_act" + tag, dh2b, wd_g, g, u, l)
    dws = []
    for nm, dact in (("gate", dg), ("up", du)):
        dws.append(_mm("dw_%s%s" % (nm, tag), "tn", hf, dact,
                       pl.BlockSpec((lp, 1024), lambda i, j, k: (0, i)), pl.BlockSpec((lp, tk), lambda i, j, k: (0, j)),
                       pl.BlockSpec((None, 1024, tk), lambda i, j, k: (j, i, 0)), _sds((NCHIP, D, tk), BF16),
                       (2, NCHIP, 1)))
    dwg, dwu = dws
    dhf = None
    for nm, dact, wfull in (("gate", dg, wg_g), ("up", du, wu_g)):
        dhf = _mm("dhf_%s%s" % (nm, tag), "nt", dact, wfull,
                  pl.BlockSpec((th, tk), lambda i, j, k: (i, k)),
                  pl.BlockSpec((None, None, 1024, tk), lambda i, j, k: (l, k, j, 0)),
                  pl.BlockSpec((th, 1024), half_rows), _sds((lp, D), F32), (2, 2, NCHIP), acc_shape=(th, 1024),
                  res=dhf, res_spec=None if dhf is None else pl.BlockSpec((th, 1024), half_rows))
    dh1, dh1b, dnfw = _norm_bwd("norm_ffn_bwd" + tag, dhf, h1, nfw, dh2)
    dh1b = ffn_done([dwg, dwu, dwd], dh1b)

    dwout = _mm("dw_out" + tag, "tn", mix, dh1b,
                pl.BlockSpec((None, lp, 1024), lambda i, j, k: (i, 0, 0)), pl.BlockSpec((lp, 1024), lambda i, j, k: (0, j)),
                pl.BlockSpec((1024, 1024), half_rows), _sds((D, D), BF16), (2, 2, 1))
    wout = wout_g.reshape(DEPTH, D, D)
    dmix = _mm("dmix" + tag, "nt", dh1b, wout,
               pl.BlockSpec((th, D), lambda i, j, k: (i, 0)), pl.BlockSpec((None, 1024, D), lambda i, j, k: (l, j, 0)),
               pl.BlockSpec((th, 1024), half_rows), _sds((lp, D), F32), (2, 2, 1))
    dht, de, dmnw, dcw = _mix_bwd("mix_bwd" + tag, dmix, ht, e, mnw, cw)
    dq, dk, dv, dgrow, dfx = _mlstm_bwd("mlstm_bwd" + tag, qkv, grow, mcol, ccol, ht, dht, 8)
    dgp, dgpb, dbias = _gate_bwd("gate_bwd" + tag, gpre, bias, dgrow, dfx)
    del dgp
    dqkv = jnp.concatenate([dq, dk, dv], axis=1)

    hn_cols = pl.BlockSpec((lp, 1024), lambda i, j, k: (0, i))
    dwqkv = _mm("dw_qkv" + tag, "tn", hn, dqkv, hn_cols, pl.BlockSpec((lp, 1024), lambda i, j, k: (0, j)),
                pl.BlockSpec((1024, 1024), half_rows), _sds((D, 2048), BF16), (2, 2, 1))
    dwe = _mm("dw_e" + tag, "tn", hn, de, hn_cols, pl.BlockSpec((None, lp, 1024), lambda i, j, k: (j, 0, 0)),
              pl.BlockSpec((None, 1024, 1024), lambda i, j, k: (j, i, 0)), _sds((4, D, 1024), BF16), (2, 4, 1))
    dwgt = _mm("dw_gates" + tag, "tn", hn, dgpb, hn_cols, pl.BlockSpec((lp, LANE), lambda i, j, k: (0, 0)),
               pl.BlockSpec((1024, LANE), lambda i, j, k: (i, 0)), _sds((D, LANE), BF16), (2, 1, 1))
    dwin = _in_grads(dwqkv, dwe, dwgt)

    dhn = _mm("dhn_qkv" + tag, "nt", dqkv, wqkv,
              pl.BlockSpec((th, 2048), lambda i, j, k: (i, 0)), pl.BlockSpec((1024, 2048), lambda i, j, k: (j, 0)),
              pl.BlockSpec((th, 1024), half_rows), _sds((lp, D), F32), (2, 2, 1))
    dhn = _mm("dhn_e" + tag, "nt", de, we,
              pl.BlockSpec((None, th, 1024), lambda i, j, k: (k, i, 0)),
              pl.BlockSpec((None, 1024, 1024), lambda i, j, k: (k, j, 0)),
              pl.BlockSpec((th, 1024), half_rows), _sds((lp, D), F32), (2, 2, 4), acc_shape=(th, 1024),
              res=dhn, res_spec=pl.BlockSpec((th, 1024), half_rows))
    dhn = _mm("dhn_gates" + tag, "nt", dgpb, wgt,
              pl.BlockSpec((th, LANE), lambda i, j, k: (i, 0)), pl.BlockSpec((1024, LANE), lambda i, j, k: (j, 0)),
              pl.BlockSpec((th, 1024), half_rows), _sds((lp, D), F32), (2, 2, 1),
              res=dhn, res_spec=pl.BlockSpec((th, 1024), half_rows))
    dh0, dh0b, dnmw = _norm_bwd("norm_mix_bwd" + tag, dhn, h, nmw, dh1)

    pieces = [dwin, dwout.reshape(NCHIP, D // NCHIP, D)]
    smalls = (dnmw[0], dbias[0, :8], dcw[:3], dmnw[0], dnfw[0])
    return dh0, dh0b, pieces, smalls


def _pack_rows(parts):
    rows = []
    for p in parts:
        f = p.reshape(-1)
        pad = (-f.shape[0]) % LANE
        if pad:
            f = jnp.pad(f, (0, pad))
        rows.append(f.reshape(-1, LANE))
    r = jnp.concatenate(rows, axis=0)
    pad = (-r.shape[0]) % 8
    if pad:
        r = jnp.pad(r, ((0, pad), (0, 0)))
    return r


def _unpack_rows(pack, shapes):
    out, r0 = [], 0
    for s in shapes:
        n = 1
        for d in s:
            n *= d
        nr = -(-n // LANE)
        out.append(pack[r0:r0 + nr].reshape(-1)[:n].reshape(s))
        r0 += nr
    return out


def kernel(x, meta_tokens, norm_mix_w, w_in, b_gates, conv_w, mlstm_norm_w, w_out, norm_ffn_w, w_gate, w_up, w_down, norm_final_w, loss_target, m_meta_tokens, m_norm_mix_w, m_w_in, m_b_gates, m_conv_w, m_mlstm_norm_w, m_w_out, m_norm_ffn_w, m_w_gate, m_w_up, m_w_down, m_norm_final_w, v_meta_tokens, v_norm_mix_w, v_w_in, v_b_gates, v_conv_w, v_mlstm_norm_w, v_w_out, v_norm_ffn_w, v_w_gate, v_w_up, v_w_down, v_norm_final_w):
    seq = x.shape[1]
    n_real = N_META + seq
    lp = -(-n_real // LANE) * LANE
    xi, yi, ci = lax.axis_index("x"), lax.axis_index("y"), lax.axis_index("c")
    jchip = 2 * xi + yi
    pf = jnp.stack([ci, jchip, 2 * (1 - xi) + yi, 2 * xi + (1 - yi), 2 * (1 - xi) + (1 - yi)]).astype(jnp.int32)

    big = {"w_in": w_in, "w_out": w_out, "w_gate": w_gate, "w_up": w_up, "w_down": w_down}
    casts = [_cast_into("cast_" + n, w, pf) for n, w in big.items()]
    conv_flat = jnp.pad(conv_w.reshape(DEPTH * 3, CW // NCHIP), ((0, 8 - DEPTH * 3), (0, 0)))
    gathered = _allgather(casts, [meta_tokens, conv_flat])
    win_g, wout_g, wg_g, wu_g, wd_g, meta_g, conv_g = gathered
    meta_full = jnp.concatenate([meta_g[s] for s in range(NCHIP)], axis=1)
    conv_full = jnp.concatenate([conv_g[s][:DEPTH * 3] for s in range(NCHIP)], axis=1)
    conv_full = conv_full.reshape(DEPTH, 3, CW)

    bias_rows = jnp.pad(b_gates, ((0, 0), (0, LANE - 8)))
    wts, smalls = [], []
    for l in range(DEPTH):
        wts.append(_in_weights(win_g, l) + (wout_g, wg_g, wu_g, wd_g))
        smalls.append((norm_mix_w[l][None], bias_rows[l][None], mlstm_norm_w[l][None],
                       jnp.pad(conv_full[l], ((0, 5), (0, 0))), norm_ffn_w[l][None]))

    h = jnp.concatenate([meta_full, x[0], jnp.zeros((lp - n_real, D), F32)], axis=0)
    saved = []
    for l in range(DEPTH):
        h, sv = _layer_fwd(l, h, wts[l], smalls[l])
        saved.append(sv)
    tgt = jnp.pad(loss_target[0], ((N_META, lp - n_real), (0, 0)))
    dh, dhb, dnorm_final, loss_part = _loss_head(h, tgt, norm_final_w[None], n_real)

    names = ["w_in", "w_out", "w_gate", "w_up", "w_down"]
    params = {"w_in": (w_in, m_w_in, v_w_in), "w_out": (w_out, m_w_out, v_w_out), "w_gate": (w_gate, m_w_gate, v_w_gate),
              "w_up": (w_up, m_w_up, v_w_up), "w_down": (w_down, m_w_down, v_w_down)}
    big_out = {n: None for n in names}
    small_grads = [None] * DEPTH
    def finish(l, group, handle, afters):
        for n, (g_mine, g_theirs) in zip(group, _rs_end(handle, afters, pf)):
            w, m, v = params[n]
            big_out[n] = _adamw_layer("adamw_%s_l%d" % (n, l), l, g_mine, g_theirs, w, m, v, big_out[n], pf)
        return [big_out[n][3] for n in group]

    groups = []
    token = None
    for l in reversed(range(DEPTH)):
        def ffn_done(pieces, dh1b, l=l):
            handle, tok = _rs_begin("l%df" % l, pieces, pf)
            groups.append((l, names[2:], handle))
            return dh1b + tok[0, 0].astype(BF16)

        dh, dhb, pieces, small_grads[l] = _layer_bwd(l, dh, dhb, saved[l], wts[l], smalls[l], ffn_done)
        handle, token = _rs_begin("l%dm" % l, pieces, pf)
        groups.append((l, names[:2], handle))
        if l > 0:
            dhb = dhb + token[0, 0].astype(BF16)
    afters = [token]
    for l, group, handle in groups:
        afters = finish(l, group, handle, afters)

    dnmw = jnp.stack([small_grads[l][0] for l in range(DEPTH)])
    dbias = jnp.stack([small_grads[l][1] for l in range(DEPTH)])
    dconv = jnp.stack([small_grads[l][2] for l in range(DEPTH)])
    dmnw = jnp.stack([small_grads[l][3] for l in range(DEPTH)])
    dnfw = jnp.stack([small_grads[l][4] for l in range(DEPTH)])
    part_shapes = [(N_META, D), (DEPTH, D), (DEPTH, 8), (DEPTH, 3, CW), (DEPTH, MW), (DEPTH, D), (D,), (LANE,)]
    pack = _pack_rows([dh[:N_META], dnmw, dbias, dconv, dmnw, dnfw, dnorm_final[0], loss_part[0]])
    tot = _unpack_rows(_allreduce_small(pack), part_shapes)
    g_meta_full, g_nmw, g_bias, g_conv_full, g_mnw, g_nfw, g_final, loss_row = tot
    mcols = D // NCHIP
    ccols = CW // NCHIP
    g_meta = lax.dynamic_slice_in_dim(g_meta_full, jchip * mcols, mcols, axis=1)
    g_conv = lax.dynamic_slice_in_dim(g_conv_full, jchip * ccols, ccols, axis=2)
    sm_g = [g_meta, g_nmw, g_bias, g_conv, g_mnw, g_nfw, g_final]
    sm_w = [meta_tokens, norm_mix_w, b_gates, conv_w, mlstm_norm_w, norm_ffn_w, norm_final_w]
    sm_m = [m_meta_tokens, m_norm_mix_w, m_b_gates, m_conv_w, m_mlstm_norm_w, m_norm_ffn_w, m_norm_final_w]
    sm_v = [v_meta_tokens, v_norm_mix_w, v_b_gates, v_conv_w, v_mlstm_norm_w, v_norm_ffn_w, v_norm_final_w]
    sm_shapes = [w.shape for w in sm_w]
    d_p, m_p, v_p = _adamw_flat(_pack_rows(sm_g), _pack_rows(sm_w), _pack_rows(sm_m), _pack_rows(sm_v))
    sm_d = _unpack_rows(d_p, sm_shapes)
    sm_nm = _unpack_rows(m_p, sm_shapes)
    sm_nv = _unpack_rows(v_p, sm_shapes)

    loss = loss_row[0]
    grad_x = dh[N_META:n_real][None]

    def ordered(sm, which):
        bo = {n: big_out[n][which] for n in names}
        return [sm[0], sm[1], bo["w_in"], sm[2], sm[3], sm[4], bo["w_out"], sm[5], bo["w_gate"], bo["w_up"], bo["w_down"], sm[6]]

    return (loss, grad_x, *ordered(sm_g, 0), *ordered(sm_d, 1), *ordered(sm_nm, 2), *ordered(sm_nv, 3))
```

```python
import functools

import jax
import jax.numpy as jnp
from jax import lax
from jax.experimental import pallas as pl
from jax.experimental.pallas import tpu as pltpu

F32 = jnp.float32
BF16 = jnp.bfloat16

D = 2048
N_META = 16
HEADS = 4
DQK = 128
DV = 256
MW = HEADS * DV
CW = D - MW
QKW = HEADS * DQK
DFF = 5632
DIN = 6152
NCHIP = 4
DEPTH = 2
GATE_CAP = 15.0
EPS = 1e-6
QK_SCALE = DQK ** -0.5
LANE = 128
VMEM_LIMIT = 60 * 1024 * 1024

ADAM_LR = 0.001
ADAM_B1 = 0.9
ADAM_B2 = 0.999
ADAM_EPS = 1e-08
ADAM_WD = 0.01
ADAM_STEP = 10

MESH = pl.DeviceIdType.MESH
ANY = pl.BlockSpec(memory_space=pl.ANY)


def _cp(sem):
    return pltpu.CompilerParams(dimension_semantics=sem, vmem_limit_bytes=VMEM_LIMIT)


def _sds(shape, dtype):
    return jax.ShapeDtypeStruct(shape, dtype)


_DN = {"nn": (((1,), (0,)), ((), ())), "nt": (((1,), (1,)), ((), ())), "tn": (((0,), (0,)), ((), ()))}


def _mm(name, kind, a, b, a_spec, b_spec, o_spec, out_shape, grid, acc_shape=None, res=None, res_spec=None):
    nk = grid[2]
    has_res = res is not None

    def body(*refs):
        if has_res:
            a_ref, b_ref, r_ref, o_ref = refs[:4]
        else:
            a_ref, b_ref, o_ref = refs[:3]
            r_ref = None
        p = lax.dot_general(a_ref[...], b_ref[...], _DN[kind], preferred_element_type=F32)
        if nk == 1:
            if r_ref is not None:
                p = p + r_ref[...]
            o_ref[...] = p.astype(o_ref.dtype)
        else:
            acc = refs[-1]
            k = pl.program_id(2)

            @pl.when(k == 0)
            def _():
                acc[...] = p

            @pl.when(k > 0)
            def _():
                acc[...] += p

            @pl.when(k == nk - 1)
            def _():
                r = acc[...]
                if r_ref is not None:
                    r = r + r_ref[...]
                o_ref[...] = r.astype(o_ref.dtype)

    ins = [a, b] + ([res] if has_res else [])
    in_specs = [a_spec, b_spec] + ([res_spec] if has_res else [])
    scratch = [pltpu.VMEM(acc_shape, F32)] if nk > 1 else []
    return pl.pallas_call(
        body, name=name, grid=grid, in_specs=in_specs, out_specs=o_spec, out_shape=out_shape,
        scratch_shapes=scratch, compiler_params=_cp(("parallel", "parallel", "arbitrary")))(*ins)


def _norm_fwd(name, h, w):
    lp = h.shape[0]
    tm = lp // 4

    def body(h_ref, w_ref, o_ref):
        x = h_ref[...]
        r = lax.rsqrt(jnp.mean(x * x, axis=1, keepdims=True) + EPS)
        o_ref[...] = (x * r * w_ref[...]).astype(BF16)

    return pl.pallas_call(
        body, name=name, grid=(4,),
        in_specs=[pl.BlockSpec((tm, D), lambda i: (i, 0)), pl.BlockSpec((1, D), lambda i: (0, 0))],
        out_specs=pl.BlockSpec((tm, D), lambda i: (i, 0)), out_shape=_sds((lp, D), BF16),
        compiler_params=_cp(("parallel",)))(h, w)


def _norm_bwd(name, dy, h, w, dres):
    lp = h.shape[0]
    tm = lp // 8

    def body(dy_ref, h_ref, w_ref, dres_ref, dh_ref, dhb_ref, dw_ref):
        x = h_ref[...]
        r = lax.rsqrt(jnp.mean(x * x, axis=1, keepdims=True) + EPS)
        xh = x * r
        dy_v = dy_ref[...]
        dxh = dy_v * w_ref[...]
        dx = r * (dxh - xh * jnp.mean(dxh * xh, axis=1, keepdims=True))
        dh = dres_ref[...] + dx
        dh_ref[...] = dh
        dhb_ref[...] = dh.astype(BF16)

        @pl.when(pl.program_id(0) == 0)
        def _():
            dw_ref[...] = jnp.zeros_like(dw_ref)

        dw_ref[0:1, :] += jnp.sum(dy_v * xh, axis=0, keepdims=True)

    row = pl.BlockSpec((tm, D), lambda i: (i, 0))
    return pl.pallas_call(
        body, name=name, grid=(8,),
        in_specs=[row, row, pl.BlockSpec((1, D), lambda i: (0, 0)), row],
        out_specs=[row, row, pl.BlockSpec((8, D), lambda i: (0, 0))],
        out_shape=[_sds((lp, D), F32), _sds((lp, D), BF16), _sds((8, D), F32)],
        compiler_params=_cp(("arbitrary",)))(dy, h, w, dres)


def _loss_head(h, tgt, w, n_real):
    lp = h.shape[0]
    tm = lp // 8

    def body(h_ref, t_ref, w_ref, dh_ref, dhb_ref, dw_ref, loss_ref):
        i = pl.program_id(0)
        x = h_ref[...]
        r = lax.rsqrt(jnp.mean(x * x, axis=1, keepdims=True) + EPS)
        xh = x * r
        wv = w_ref[...]
        row = i * tm + lax.broadcasted_iota(jnp.int32, (tm, 1), 0)
        valid = jnp.logical_and(row >= N_META, row < n_real)
        err = jnp.where(valid, xh * wv - t_ref[...], 0.0)
        dy_v = err * (1.0 / D)
        dxh = dy_v * wv
        dx = r * (dxh - xh * jnp.mean(dxh * xh, axis=1, keepdims=True))
        dh_ref[...] = dx
        dhb_ref[...] = dx.astype(BF16)

        @pl.when(i == 0)
        def _():
            dw_ref[...] = jnp.zeros_like(dw_ref)
            loss_ref[...] = jnp.zeros_like(loss_ref)

        dw_ref[0:1, :] += jnp.sum(dy_v * xh, axis=0, keepdims=True)
        part = jnp.sum(jnp.sum(err * err, axis=1, keepdims=True), axis=0, keepdims=True) * (0.5 / D)
        loss_ref[...] += jnp.broadcast_to(part, loss_ref.shape)

    row = pl.BlockSpec((tm, D), lambda i: (i, 0))
    return pl.pallas_call(
        body, name="loss_head", grid=(8,),
        in_specs=[row, row, pl.BlockSpec((1, D), lambda i: (0, 0))],
        out_specs=[row, row, pl.BlockSpec((8, D), lambda i: (0, 0)), pl.BlockSpec((8, LANE), lambda i: (0, 0))],
        out_shape=[_sds((lp, D), F32), _sds((lp, D), BF16), _sds((8, D), F32), _sds((8, LANE), F32)],
        compiler_params=_cp(("arbitrary",)))(h, tgt, w)


def _ffn_up(name, hf, wg4, wu4):
    lp = hf.shape[0]
    tm = lp // 4
    tn = DFF // NCHIP

    def body(x_ref, wg_ref, wu_ref, g_ref, u_ref, a_ref):
        x = x_ref[...]
        g = jnp.dot(x, wg_ref[...], preferred_element_type=F32)
        u = jnp.dot(x, wu_ref[...], preferred_element_type=F32)
        g_ref[...] = g.astype(BF16)
        u_ref[...] = u.astype(BF16)
        a_ref[...] = (g * jax.nn.sigmoid(g) * u).astype(BF16)

    wspec = pl.BlockSpec((None, D, tn), lambda j, i: (j, 0, 0))
    ospec = pl.BlockSpec((tm, tn), lambda j, i: (i, j))
    o = _sds((lp, DFF), BF16)
    return pl.pallas_call(
        body, name=name, grid=(NCHIP, 4),
        in_specs=[pl.BlockSpec((tm, D), lambda j, i: (i, 0)), wspec, wspec],
        out_specs=[ospec, ospec, ospec], out_shape=[o, o, o],
        compiler_params=_cp(("parallel", "parallel")))(hf, wg4, wu4)


def _ffn_bwd_act(name, dhb, wd4, g, u):
    lp = dhb.shape[0]
    tm = lp // 4
    tn = DFF // NCHIP

    def body(d_ref, w_ref, g_ref, u_ref, dg_ref, du_ref):
        da = lax.dot_general(d_ref[...], w_ref[...], _DN["nt"], preferred_element_type=F32)
        gv = g_ref[...].astype(F32)
        uv = u_ref[...].astype(F32)
        sg = jax.nn.sigmoid(gv)
        dg_ref[...] = (da * uv * (sg * (1.0 + gv * (1.0 - sg)))).astype(BF16)
        du_ref[...] = (da * (gv * sg)).astype(BF16)

    ospec = pl.BlockSpec((tm, tn), lambda j, i: (i, j))
    o = _sds((lp, DFF), BF16)
    return pl.pallas_call(
        body, name=name, grid=(NCHIP, 4),
        in_specs=[pl.BlockSpec((tm, D), lambda j, i: (i, 0)),
                  pl.BlockSpec((None, tn, D), lambda j, i: (j, 0, 0)), ospec, ospec],
        out_specs=[ospec, ospec], out_shape=[o, o],
        compiler_params=_cp(("parallel", "parallel")))(dhb, wd4, g, u)


def _shift_rows(x, d, row):
    return jnp.where(row >= d, pltpu.roll(x, d, axis=0), 0.0)


def _scan_steps(lp):
    d = 1
    while d < lp:
        yield d
        d *= 2


def _gate_values(pre):
    t = GATE_CAP * jnp.tanh(pre * (1.0 / GATE_CAP))
    lf = jnp.minimum(t, 0.0) - jnp.log(1.0 + jnp.exp(-jnp.abs(t)))
    return t, lf


def _gate_prep(name, gates_pre, bias):
    lp = gates_pre.shape[0]

    def body(p_ref, b_ref, grow_ref, m_ref, c_ref):
        pre = p_ref[...] + b_ref[...]
        lane = lax.broadcasted_iota(jnp.int32, (lp, LANE), 1)
        row = lax.broadcasted_iota(jnp.int32, (lp, LANE), 0)
        t, lf = _gate_values(pre)
        f = jnp.where(jnp.logical_and(lane >= HEADS, lane < 2 * HEADS), lf, 0.0)
        for d in _scan_steps(lp):
            f = f + _shift_rows(f, d, row)
        fs = pltpu.roll(f, LANE - HEADS, axis=1)
        g = jnp.where(lane < HEADS, t - fs, 0.0)
        m = g
        for d in _scan_steps(lp):
            m = jnp.maximum(m, jnp.where(row >= d, pltpu.roll(m, d, axis=0), m))
        grow_ref[...] = g.T
        m_ref[...] = m
        c_ref[...] = jnp.where(lane < HEADS, -fs - m, 0.0)

    full = pl.BlockSpec((lp, LANE), lambda: (0, 0))
    return pl.pallas_call(
        body, name=name, in_specs=[full, pl.BlockSpec((1, LANE), lambda: (0, 0))],
        out_specs=[pl.BlockSpec((LANE, lp), lambda: (0, 0)), full, full],
        out_shape=[_sds((LANE, lp), F32), _sds((lp, LANE), F32), _sds((lp, LANE), F32)],
        compiler_params=pltpu.CompilerParams(vmem_limit_bytes=VMEM_LIMIT))(gates_pre, bias)


def _pick_lane(blk, h):
    lane = lax.broadcasted_iota(jnp.int32, blk.shape, 1)
    return jnp.sum(jnp.where(lane == h, blk, 0.0), axis=1, keepdims=True)


def _mlstm_weights(q, k, grow, mcol, i, bq, lp):
    s = lax.dot_general(q, k, _DN["nt"], preferred_element_type=F32) * QK_SCALE
    row = i * bq + lax.broadcasted_iota(jnp.int32, (bq, 1), 0)
    col = lax.broadcasted_iota(jnp.int32, (1, lp), 1)
    a = jnp.where(col <= row, jnp.exp(jnp.minimum(grow - mcol, 0.0)), 0.0)
    return s, a


def _mlstm_fwd(name, qkv, grow, mcol_all, ccol_all, nq):
    lp = qkv.shape[0]
    bq = lp // nq

    def body(q_ref, k_ref, v_ref, grow_ref, m_ref, c_ref, o_ref):
        h = pl.program_id(0)
        i = pl.program_id(1)
        grow_h = grow_ref[pl.ds(h, 1), :]
        mcol = _pick_lane(m_ref[...], h)
        ccol = _pick_lane(c_ref[...], h)
        s, a = _mlstm_weights(q_ref[...], k_ref[...], grow_h, mcol, i, bq, lp)
        p = a * s
        den = jnp.sum(p, axis=1, keepdims=True)
        num = jnp.dot(p.astype(BF16), v_ref[...], preferred_element_type=F32)
        o_ref[...] = num / jnp.maximum(jnp.abs(den), jnp.exp(ccol))

    return pl.pallas_call(
        body, name=name, grid=(HEADS, nq),
        in_specs=[pl.BlockSpec((bq, DQK), lambda h, i: (i, h)),
                  pl.BlockSpec((lp, DQK), lambda h, i: (0, HEADS + h)),
                  pl.BlockSpec((lp, DV), lambda h, i: (0, HEADS + h)),
                  pl.BlockSpec((8, lp), lambda h, i: (0, 0)),
                  pl.BlockSpec((bq, LANE), lambda h, i: (i, 0)),
                  pl.BlockSpec((bq, LANE), lambda h, i: (i, 0))],
        out_specs=pl.BlockSpec((bq, DV), lambda h, i: (i, h)),
        out_shape=_sds((lp, MW), F32),
        compiler_params=_cp(("parallel", "parallel")))(qkv, qkv, qkv, grow, mcol_all, ccol_all)


def _mlstm_bwd(name, qkv, grow, mcol_all, ccol_all, ht, dht, nq):
    lp = qkv.shape[0]
    bq = lp // nq

    def body(q_ref, k_ref, v_ref, grow_ref, m_ref, c_ref, ht_ref, dht_ref,
             dq_ref, dk_ref, dv_ref, dgrow_ref, dfx_ref, dkt_acc, dvt_acc):
        h = pl.program_id(0)
        i = pl.program_id(1)

        @pl.when(jnp.logical_and(h == 0, i == 0))
        def _():
            dgrow_ref[...] = jnp.zeros_like(dgrow_ref)
            dfx_ref[...] = jnp.zeros_like(dfx_ref)

        @pl.when(i == 0)
        def _():
            dkt_acc[...] = jnp.zeros_like(dkt_acc)
            dvt_acc[...] = jnp.zeros_like(dvt_acc)

        q = q_ref[...]
        k = k_ref[...]
        v = v_ref[...]
        grow_h = grow_ref[pl.ds(h, 1), :]
        mcol = _pick_lane(m_ref[...], h)
        ccol = _pick_lane(c_ref[...], h)
        s, a = _mlstm_weights(q, k, grow_h, mcol, i, bq, lp)
        p = a * s
        den = jnp.sum(p, axis=1, keepdims=True)
        clamp = jnp.exp(ccol)
        active = jnp.abs(den) < clamp
        dd = jnp.maximum(jnp.abs(den), clamp)
        dht_v = dht_ref[...]
        hdh = jnp.sum(dht_v * ht_ref[...], axis=1, keepdims=True)
        dn = (dht_v / dd).astype(BF16)
        dden = jnp.where(active, 0.0, -(hdh / dd) * jnp.sign(den))
        dp = lax.dot_general(dn, v, _DN["nt"], preferred_element_type=F32) + dden
        rmat = dp * p
        dgrow_ref[pl.ds(h, 1), :] += jnp.sum(rmat, axis=0, keepdims=True)
        ds = (dp * a * QK_SCALE).astype(BF16)
        dq_ref[...] = jnp.dot(ds, k, preferred_element_type=F32).astype(BF16)
        dkt_acc[...] += lax.dot_general(q, ds, _DN["tn"], preferred_element_type=F32)
        dvt_acc[...] += lax.dot_general(dn, p.astype(BF16), _DN["tn"], preferred_element_type=F32)
        lane = lax.broadcasted_iota(jnp.int32, (bq, LANE), 1)
        r0 = pl.multiple_of(i * bq, 16)
        dfx_ref[pl.ds(r0, bq), :] += jnp.where(lane == h, jnp.sum(rmat, axis=1, keepdims=True), 0.0)

        @pl.when(i == nq - 1)
        def _():
            dk_ref[...] = dkt_acc[...].T.astype(BF16)
            dv_ref[...] = dvt_acc[...].T.astype(BF16)

    return pl.pallas_call(
        body, name=name, grid=(HEADS, nq),
        in_specs=[pl.BlockSpec((bq, DQK), lambda h, i: (i, h)),
                  pl.BlockSpec((lp, DQK), lambda h, i: (0, HEADS + h)),
                  pl.BlockSpec((lp, DV), lambda h, i: (0, HEADS + h)),
                  pl.BlockSpec((8, lp), lambda h, i: (0, 0)),
                  pl.BlockSpec((bq, LANE), lambda h, i: (i, 0)),
                  pl.BlockSpec((bq, LANE), lambda h, i: (i, 0)),
                  pl.BlockSpec((bq, DV), lambda h, i: (i, h)),
                  pl.BlockSpec((bq, DV), lambda h, i: (i, h))],
        out_specs=[pl.BlockSpec((bq, DQK), lambda h, i: (i, h)),
                   pl.BlockSpec((lp, DQK), lambda h, i: (0, h)),
                   pl.BlockSpec((lp, DV), lambda h, i: (0, h)),
                   pl.BlockSpec((LANE, lp), lambda h, i: (0, 0)),
                   pl.BlockSpec((lp, LANE), lambda h, i: (0, 0))],
        out_shape=[_sds((lp, QKW), BF16), _sds((lp, QKW), BF16), _sds((lp, MW), BF16),
                   _sds((LANE, lp), F32), _sds((lp, LANE), F32)],
        scratch_shapes=[pltpu.VMEM((DQK, lp), F32), pltpu.VMEM((DV, lp), F32)],
        compiler_params=_cp(("arbitrary", "arbitrary")))(qkv, qkv, qkv, grow, mcol_all, ccol_all, ht, dht)


def _gate_bwd(name, gates_pre, bias, dgrow, dfx):
    lp = gates_pre.shape[0]

    def body(p_ref, b_ref, dgrow_ref, dfx_ref, dg_ref, dgb_ref, db_ref):
        pre = p_ref[...] + b_ref[...]
        lane = lax.broadcasted_iota(jnp.int32, (lp, LANE), 1)
        row = lax.broadcasted_iota(jnp.int32, (lp, LANE), 0)
        th = jnp.tanh(pre * (1.0 / GATE_CAP))
        t = GATE_CAP * th
        dgc = jnp.where(lane < HEADS, dgrow_ref[...].T, 0.0)
        df = jnp.where(lane < HEADS, dfx_ref[...] - dgc, 0.0)
        for d in _scan_steps(lp):
            df = df + jnp.where(row < lp - d, pltpu.roll(df, lp - d, axis=0), 0.0)
        dlf = pltpu.roll(df, HEADS, axis=1)
        dt = jnp.where(lane < HEADS, dgc, dlf * jax.nn.sigmoid(-t))
        dpre = jnp.where(lane < 2 * HEADS, dt * (1.0 - th * th), 0.0)
        dg_ref[...] = dpre
        dgb_ref[...] = dpre.astype(BF16)
        db_ref[...] = jnp.broadcast_to(jnp.sum(dpre, axis=0, keepdims=True), db_ref.shape)

    full = pl.BlockSpec((lp, LANE), lambda: (0, 0))
    return pl.pallas_call(
        body, name=name,
        in_specs=[full, pl.BlockSpec((1, LANE), lambda: (0, 0)), pl.BlockSpec((LANE, lp), lambda: (0, 0)), full],
        out_specs=[full, full, pl.BlockSpec((8, LANE), lambda: (0, 0))],
        out_shape=[_sds((lp, LANE), F32), _sds((lp, LANE), BF16), _sds((8, LANE), F32)],
        compiler_params=pltpu.CompilerParams(vmem_limit_bytes=VMEM_LIMIT))(gates_pre, bias, dgrow, dfx)


CB = 256


def _e_specs(lp):
    return [pl.BlockSpec((None, lp, CB), functools.partial(lambda c, j: (c, 0, j), c)) for c in range(4)]


def _mix_fwd(name, ht, e, mnw, cw):
    lp = ht.shape[0]

    def body(ht_ref, og_ref, u_ref, gb_ref, gc_ref, mnw_ref, cw_ref, o_ref):
        x = ht_ref[...]
        r = lax.rsqrt(jnp.mean(x * x, axis=1, keepdims=True) + EPS)
        o_ref[0] = (jax.nn.sigmoid(og_ref[...].astype(F32)) * (x * r * mnw_ref[...])).astype(BF16)
        row = lax.broadcasted_iota(jnp.int32, (lp, CB), 0)
        a = gc_ref[...].astype(F32) * u_ref[...].astype(F32)
        conv = cw_ref[2:3, :] * a + cw_ref[1:2, :] * _shift_rows(a, 1, row) + cw_ref[0:1, :] * _shift_rows(a, 2, row)
        o_ref[1] = (gb_ref[...].astype(F32) * conv).astype(BF16)

    col = pl.BlockSpec((lp, CB), lambda j: (0, j))
    return pl.pallas_call(
        body, name=name, grid=(4,),
        in_specs=[col] + _e_specs(lp) + [pl.BlockSpec((1, CB), lambda j: (0, j)), pl.BlockSpec((8, CB), lambda j: (0, j))],
        out_specs=pl.BlockSpec((2, lp, CB), lambda j: (0, 0, j)), out_shape=_sds((2, lp, MW), BF16),
        compiler_params=_cp(("parallel",)))(ht, e, e, e, e, mnw, cw)


def _mix_bwd(name, dmix, ht, e, mnw, cw):
    lp = ht.shape[0]

    def body(dhm_ref, dhc_ref, ht_ref, og_ref, u_ref, gb_ref, gc_ref, mnw_ref, cw_ref,
             dht_ref, de_ref, dmnw_ref, dcw_ref):
        x = ht_ref[...]
        r = lax.rsqrt(jnp.mean(x * x, axis=1, keepdims=True) + EPS)
        xh = x * r
        w = mnw_ref[...]
        sg = jax.nn.sigmoid(og_ref[...].astype(F32))
        dhm = dhm_ref[...]
        de_ref[0] = (dhm * (xh * w) * (sg * (1.0 - sg))).astype(BF16)
        dn = dhm * sg
        dmnw_ref[...] = jnp.broadcast_to(jnp.sum(dn * xh, axis=0, keepdims=True), dmnw_ref.shape)
        dxh = dn * w
        dht_ref[...] = r * (dxh - xh * jnp.mean(dxh * xh, axis=1, keepdims=True))

        row = lax.broadcasted_iota(jnp.int32, (lp, CB), 0)
        uv = u_ref[...].astype(F32)
        gcv = gc_ref[...].astype(F32)
        gbv = gb_ref[...].astype(F32)
        a = gcv * uv
        a1 = _shift_rows(a, 1, row)
        a2 = _shift_rows(a, 2, row)
        dhc = dhc_ref[...]
        conv = cw_ref[2:3, :] * a + cw_ref[1:2, :] * a1 + cw_ref[0:1, :] * a2
        de_ref[2] = (dhc * conv).astype(BF16)
        dconv = dhc * gbv
        dcw_ref[...] = jnp.zeros_like(dcw_ref)
        dcw_ref[0:1, :] = jnp.sum(dconv * a2, axis=0, keepdims=True)
        dcw_ref[1:2, :] = jnp.sum(dconv * a1, axis=0, keepdims=True)
        dcw_ref[2:3, :] = jnp.sum(dconv * a, axis=0, keepdims=True)
        up1 = jnp.where(row < lp - 1, pltpu.roll(dconv, lp - 1, axis=0), 0.0)
        up2 = jnp.where(row < lp - 2, pltpu.roll(dconv, lp - 2, axis=0), 0.0)
        da = cw_ref[2:3, :] * dconv + cw_ref[1:2, :] * up1 + cw_ref[0:1, :] * up2
        de_ref[1] = (da * gcv).astype(BF16)
        de_ref[3] = (da * uv).astype(BF16)

    col = pl.BlockSpec((lp, CB), lambda j: (0, j))
    small = pl.BlockSpec((8, CB), lambda j: (0, j))
    return pl.pallas_call(
        body, name=name, grid=(4,),
        in_specs=[col, pl.BlockSpec((lp, CB), lambda j: (0, 4 + j)), col] + _e_specs(lp)
                 + [pl.BlockSpec((1, CB), lambda j: (0, j)), small],
        out_specs=[col, pl.BlockSpec((4, lp, CB), lambda j: (0, 0, j)), small, small],
        out_shape=[_sds((lp, MW), F32), _sds((4, lp, MW), BF16), _sds((8, MW), F32), _sds((8, CW), F32)],
        compiler_params=_cp(("parallel",)))(dmix, dmix, ht, e, e, e, e, mnw, cw)


def _row_tile(r, c, itemsize, budget=1536 * 1024, mult=16):
    best = None
    for t in range(mult, r + 1, mult):
        if r % t == 0 and t * c * itemsize <= budget:
            best = t
    if best is None:
        best = r
    return best


def _grid_spec(grid, in_specs, out_specs, scratch=()):
    return pltpu.PrefetchScalarGridSpec(num_scalar_prefetch=1, grid=grid, in_specs=in_specs,
                                        out_specs=out_specs, scratch_shapes=list(scratch))


def _cast_into(name, w, layer, pf):
    _, r, c = w.shape
    tr = _row_tile(r, c, 4)

    def body(pf_ref, x_ref, o_ref):
        o_ref[...] = x_ref[...].astype(BF16)

    return pl.pallas_call(
        body, name=name, out_shape=_sds((NCHIP, r, c), BF16),
        grid_spec=_grid_spec((r // tr,), [pl.BlockSpec((None, tr, c), lambda i, pf: (layer, i, 0))],
                             pl.BlockSpec((None, tr, c), lambda i, pf: (pf[1], i, 0))),
        compiler_params=_cp(("parallel",)))(pf, w)


def _add2_bf16(name, dw, got, pf):
    n4, r2, c = got.shape
    tr = _row_tile(r2, c, 4)
    nch = r2 // tr

    def body(pf_ref, a_ref, b_ref, o_ref):
        o_ref[...] = (a_ref[...].astype(F32) + b_ref[...].astype(F32)).astype(BF16)

    spec = pl.BlockSpec((None, tr, c), lambda s, i, pf: (s, i, 0))
    return pl.pallas_call(
        body, name=name, out_shape=_sds((n4, r2, c), BF16),
        grid_spec=_grid_spec((n4, nch), [pl.BlockSpec((None, tr, c), lambda s, i, pf: (s, pf[0] * nch + i, 0)), spec], spec),
        compiler_params=_cp(("parallel", "parallel")))(pf, dw, got)


def _adam_math(w, g, m, v):
    m2 = ADAM_B1 * m + (1.0 - ADAM_B1) * g
    v2 = ADAM_B2 * v + (1.0 - ADAM_B2) * (g * g)
    m_hat = m2 / (1.0 - ADAM_B1 ** ADAM_STEP)
    v_hat = v2 / (1.0 - ADAM_B2 ** ADAM_STEP)
    delta = -ADAM_LR * (m_hat / (jnp.sqrt(v_hat) + ADAM_EPS) + ADAM_WD * w)
    return delta, m2, v2


def _adamw_layer(name, layer, g_mine, g_theirs, w, m, v, prev, pf):
    _, r, c = w.shape
    r2 = r // 2
    tr = _row_tile(r2, c, 4, budget=1024 * 1024, mult=8)
    nch = r2 // tr
    n_alias = 0 if prev is None else 4

    def body(*refs):
        pf_ref, gm_ref, gt_ref, w_ref, m_ref, v_ref = refs[:6]
        go_ref, d_ref, mo_ref, vo_ref = refs[6 + n_alias:]
        mine = (pl.program_id(0) // nch) == pf_ref[0]
        gv = jnp.where(mine, gm_ref[...], gt_ref[...])
        delta, m2, v2 = _adam_math(w_ref[...], gv, m_ref[...], v_ref[...])
        go_ref[...] = gv
        d_ref[...] = delta
        mo_ref[...] = m2
        vo_ref[...] = v2

    slab = pl.BlockSpec((None, tr, c), lambda i, pf: (layer, i, 0))
    ins = [g_mine, g_theirs, w, m, v] + (list(prev) if prev is not None else [])
    in_specs = [pl.BlockSpec((tr, c), lambda i, pf: (jnp.clip(i - pf[0] * nch, 0, nch - 1), 0)),
                pl.BlockSpec((tr, c), lambda i, pf: (jnp.clip(i - (1 - pf[0]) * nch, 0, nch - 1), 0)),
                slab, slab, slab] + [ANY] * n_alias
    o = _sds(w.shape, F32)
    return pl.pallas_call(
        body, name=name, out_shape=[o] * 4, grid_spec=_grid_spec((2 * nch,), in_specs, [slab] * 4),
        input_output_aliases={6 + k: k for k in range(n_alias)},
        compiler_params=_cp(("parallel",)))(pf, *ins)


def _adamw_flat(g, w, m, v):
    def body(g_ref, w_ref, m_ref, v_ref, d_ref, mo_ref, vo_ref):
        delta, m2, v2 = _adam_math(w_ref[...], g_ref[...], m_ref[...], v_ref[...])
        d_ref[...] = delta
        mo_ref[...] = m2
        vo_ref[...] = v2

    o = _sds(w.shape, F32)
    return pl.pallas_call(body, name="adamw_small", out_shape=[o, o, o])(g, w, m, v)


def _place():
    x, y, c = lax.axis_index("x"), lax.axis_index("y"), lax.axis_index("c")
    chips = [(1 - x, y), (x, 1 - y), (1 - x, 1 - y)]
    return x, y, c, chips


def _rs_chips(name, ps):
    n = len(ps)

    def body(*refs):
        ins = refs[:n]
        got = refs[n:2 * n]
        send, recv = refs[2 * n:]
        x, y, c, chips = _place()
        cps = []
        for t in range(n):
            for k, chip in enumerate(chips):
                jk = 2 * chip[0] + chip[1]
                cp = pltpu.make_async_remote_copy(
                    src_ref=ins[t].at[jk], dst_ref=got[t].at[k],
                    send_sem=send.at[3 * t + k], recv_sem=recv.at[3 * t + k],
                    device_id=(*chip, c), device_id_type=MESH)
                cp.start()
                cps.append(cp)
        for cp in cps:
            cp.wait()

    dma = pltpu.SemaphoreType.DMA
    return pl.pallas_call(
        body, name=name, in_specs=[ANY] * n, out_specs=[ANY] * n,
        out_shape=[_sds((3,) + p.shape[1:], BF16) for p in ps],
        scratch_shapes=[dma((3 * n,)), dma((3 * n,))],
        compiler_params=pltpu.CompilerParams(has_side_effects=True))(*ps)


HBM = pl.BlockSpec(memory_space=pltpu.HBM)
SEM = pl.BlockSpec(memory_space=pltpu.SEMAPHORE)
EFFECT = pltpu.SideEffectType.DATAFLOW_SIDE_EFFECTING


def _in_hbm(a):
    return pltpu.with_memory_space_constraint(a, pltpu.HBM)


def _rs_chips_copies(ins, lands, send, recv):
    x, y, c, chips = _place()
    cps = []
    for t in range(len(ins)):
        for k, chip in enumerate(chips):
            jk = 2 * chip[0] + chip[1]
            cps.append(pltpu.make_async_remote_copy(
                src_ref=ins[t].at[jk], dst_ref=lands[t].at[k], send_sem=send.at[3 * t + k],
                recv_sem=recv.at[3 * t + k], device_id=(*chip, c), device_id_type=MESH))
    return cps


def _rs_chips_start(name, ps):
    n = len(ps)

    def body(*refs):
        ins, lands = refs[:n], refs[n:2 * n]
        send, recv = refs[2 * n], refs[2 * n + 1]
        token = refs[-1]
        for cp in _rs_chips_copies(ins, lands, send, recv):
            cp.start()
        token[...] = jnp.zeros_like(token)

    dma = pltpu.SemaphoreType.DMA
    lands = [lax.empty((3,) + p.shape[1:], BF16) for p in ps]
    out_shape = ([dma((3 * n,)), dma((3 * n,))] + [pltpu.HBM(p.shape, BF16) for p in ps]
                 + [pltpu.HBM(z.shape, BF16) for z in lands] + [_sds((8, LANE), F32)])
    outs = pl.pallas_call(
        body, name=name, out_shape=out_shape, in_specs=[HBM] * (2 * n),
        out_specs=[SEM, SEM] + [HBM] * (2 * n) + [pl.BlockSpec(memory_space=pltpu.VMEM)],
        input_output_aliases={i: 2 + i for i in range(2 * n)},
        compiler_params=pltpu.CompilerParams(has_side_effects=EFFECT))(
            *[_in_hbm(p) for p in ps], *[_in_hbm(z) for z in lands])
    return outs[0], outs[1], outs[2:2 + n], outs[2 + n:2 + 2 * n], outs[-1]


def _rs_chips_wait(name, send, recv, ps, lands, afters):
    n = len(ps)

    def body(*refs):
        ins, zones = refs[:n], refs[n:2 * n]
        send_ref, recv_ref = refs[2 * n], refs[2 * n + 1]
        for cp in _rs_chips_copies(ins, zones, send_ref, recv_ref):
            cp.wait_send()
            cp.wait_recv()

    outs = pl.pallas_call(
        body, name=name, out_shape=[pltpu.HBM(p.shape, BF16) for p in ps] + [pltpu.HBM(z.shape, BF16) for z in lands],
        in_specs=[HBM] * (2 * n) + [SEM, SEM] + [ANY] * len(afters), out_specs=[HBM] * (2 * n),
        input_output_aliases={i: i for i in range(2 * n)},
        compiler_params=pltpu.CompilerParams(has_side_effects=EFFECT))(*ps, *lands, send, recv, *afters)
    return outs[:n], outs[n:]


def _sibling():
    x, y, c, _ = _place()
    return (x, y, 1 - c)


def _pair_send(name, dw, pf):
    n4, r, c = dw.shape
    half = r // 2

    def body(pf_ref, x_ref, got_ref, ssem, rsem):
        s = pl.program_id(0)
        cp = pltpu.make_async_remote_copy(src_ref=x_ref, dst_ref=got_ref.at[pl.ds(s, 1)], send_sem=ssem,
                                          recv_sem=rsem, device_id=_sibling(), device_id_type=MESH)
        cp.start()
        cp.wait_send()

        @pl.when(s == n4 - 1)
        def _():
            pltpu.make_async_remote_copy(src_ref=got_ref, dst_ref=got_ref, send_sem=ssem, recv_sem=rsem,
                                         device_id=_sibling(), device_id_type=MESH).wait_recv()

    dma = pltpu.SemaphoreType.DMA
    return pl.pallas_call(
        body, name=name, out_shape=_sds((n4, half, c), BF16),
        grid_spec=_grid_spec((n4,), [pl.BlockSpec((1, half, c), lambda s, pf: (s, 1 - pf[0], 0))], ANY,
                             scratch=[dma(()), dma(())]),
        compiler_params=pltpu.CompilerParams(dimension_semantics=("arbitrary",), has_side_effects=True,
                                             vmem_limit_bytes=VMEM_LIMIT))(pf, dw)


def _add4_join(name, p, got, pf):
    n4, r2, c = p.shape
    tr = _row_tile(r2, c, 4)
    nch = r2 // tr

    def body(pf_ref, p_ref, g_ref, mine_ref, theirs_ref, ssem, rsem):
        i = pl.program_id(0)
        s = p_ref[...].astype(F32)
        for k in range(3):
            s = s + g_ref[k].astype(F32)
        mine_ref[...] = s
        r0 = pl.multiple_of(i * tr, 8)
        cp = pltpu.make_async_remote_copy(src_ref=mine_ref, dst_ref=theirs_ref.at[pl.ds(r0, tr), :],
                                          send_sem=ssem, recv_sem=rsem, device_id=_sibling(), device_id_type=MESH)
        cp.start()
        cp.wait_send()

        @pl.when(i == nch - 1)
        def _():
            pltpu.make_async_remote_copy(src_ref=theirs_ref, dst_ref=theirs_ref, send_sem=ssem, recv_sem=rsem,
                                         device_id=_sibling(), device_id_type=MESH).wait_recv()

    dma = pltpu.SemaphoreType.DMA
    o = _sds((r2, c), F32)
    return pl.pallas_call(
        body, name=name, out_shape=[o, o],
        grid_spec=_grid_spec((nch,), [pl.BlockSpec((None, tr, c), lambda i, pf: (pf[1], i, 0)),
                                      pl.BlockSpec((3, tr, c), lambda i, pf: (0, i, 0))],
                             [pl.BlockSpec((tr, c), lambda i, pf: (i, 0)), ANY], scratch=[dma(()), dma(())]),
        compiler_params=pltpu.CompilerParams(dimension_semantics=("arbitrary",), has_side_effects=True,
                                             vmem_limit_bytes=VMEM_LIMIT))(pf, p, got)


def _ag_copies(arrs, split, send, recv):
    x, y, c, chips = _place()
    j = 2 * x + y
    cps = []
    for t, g in enumerate(arrs):
        half = g.shape[1] // 2
        piece = g.at[j, pl.ds(c * half, half), :] if split[t] else g.at[j]
        for k, chip in enumerate(chips):
            cps.append(pltpu.make_async_remote_copy(
                src_ref=piece, dst_ref=piece, send_sem=send.at[3 * t + k], recv_sem=recv.at[3 * t + k],
                device_id=(*chip, c), device_id_type=MESH))
    return cps


def _ag_start(groups, splits):
    sizes = [len(g) for g in groups]
    flat = [a for g in groups for a in g]
    n = len(flat)

    def body(*refs):
        ins = refs[:n]
        sems = refs[n:n + 2 * len(groups)]
        o = 0
        for gi, sz in enumerate(sizes):
            for cp in _ag_copies(ins[o:o + sz], splits[gi], sems[2 * gi], sems[2 * gi + 1]):
                cp.start()
            o += sz

    dma = pltpu.SemaphoreType.DMA
    sem_shapes = [dma((3 * sz,)) for sz in sizes for _ in range(2)]
    outs = pl.pallas_call(
        body, name="ag_start", out_shape=sem_shapes + [pltpu.HBM(a.shape, a.dtype) for a in flat],
        in_specs=[HBM] * n, out_specs=[SEM] * len(sem_shapes) + [HBM] * n,
        input_output_aliases={i: len(sem_shapes) + i for i in range(n)},
        compiler_params=pltpu.CompilerParams(has_side_effects=EFFECT))(*[_in_hbm(a) for a in flat])
    sems, arrs, o = [], [], len(sem_shapes)
    for gi, sz in enumerate(sizes):
        sems.append((outs[2 * gi], outs[2 * gi + 1]))
        arrs.append(list(outs[o:o + sz]))
        o += sz
    return sems, arrs


def _ag_wait(name, arrs, split, send, recv, afters):
    n = len(arrs)

    def body(*refs):
        for cp in _ag_copies(refs[:n], split, refs[n], refs[n + 1]):
            cp.wait_send()
            cp.wait_recv()

    return pl.pallas_call(
        body, name=name, out_shape=[pltpu.HBM(a.shape, a.dtype) for a in arrs],
        in_specs=[HBM] * n + [SEM, SEM] + [ANY] * len(afters), out_specs=[HBM] * n,
        input_output_aliases={i: i for i in range(n)},
        compiler_params=pltpu.CompilerParams(has_side_effects=EFFECT))(*arrs, send, recv, *afters)


def _ag_forward(name, arrs):
    n = len(arrs)
    halves = [a.shape[1] // 2 for a in arrs]

    def body(*refs):
        g = refs[n:2 * n]
        bufs = refs[2 * n:3 * n]
        fsend, frecv, lsem = refs[3 * n:]
        x, y, c, chips = _place()
        for t in range(n):
            rows = pl.ds(c * halves[t], halves[t])
            pend = [None, None]
            for k, chip in enumerate(chips):
                jk = 2 * chip[0] + chip[1]
                slot = k % 2
                if pend[slot] is not None:
                    pend[slot].wait_send()
                part = g[t].at[jk, rows, :]
                ld = pltpu.make_async_copy(part, bufs[t].at[slot], lsem.at[2 * t + slot])
                ld.start()
                ld.wait()
                cp = pltpu.make_async_remote_copy(
                    src_ref=bufs[t].at[slot], dst_ref=part, send_sem=fsend.at[2 * t + slot],
                    recv_sem=frecv.at[t], device_id=(x, y, 1 - c), device_id_type=MESH)
                cp.start()
                pend[slot] = cp
            for cp in pend:
                cp.wait_send()
        for t in range(n):
            passed = g[t].at[pl.ds(0, 3), pl.ds((1 - c) * halves[t], halves[t]), :]
            pltpu.make_async_remote_copy(
                src_ref=passed, dst_ref=passed, send_sem=fsend.at[2 * t], recv_sem=frecv.at[t],
                device_id=(x, y, 1 - c), device_id_type=MESH).wait_recv()

    dma = pltpu.SemaphoreType.DMA
    scratch = [pltpu.VMEM((2, halves[t], arrs[t].shape[2]), BF16) for t in range(n)]
    scratch += [dma((2 * n,)), dma((n,)), dma((2 * n,))]
    return pl.pallas_call(
        body, name=name, in_specs=[ANY] * n, out_specs=[ANY] * n, out_shape=[_sds(a.shape, a.dtype) for a in arrs],
        scratch_shapes=scratch, input_output_aliases={t: t for t in range(n)},
        compiler_params=pltpu.CompilerParams(has_side_effects=True, vmem_limit_bytes=VMEM_LIMIT))(*arrs)


def _allgather_blocking(gs, smalls):
    nb, ns = len(gs), len(smalls)
    halves = [g.shape[2] // 2 for g in gs]

    def body(*refs):
        s_in = refs[nb:nb + ns]
        g = refs[nb + ns:2 * nb + ns]
        s_out = refs[2 * nb + ns:2 * (nb + ns)]
        scr = refs[2 * (nb + ns):]
        bufs = scr[:nb]
        send, recv, fsend, frecv, lsem, ssend, srecv, slsem = scr[nb:]
        x, y, c, chips = _place()
        j = 2 * x + y
        sends, slocal = [], []
        for t in range(nb):
            rows = pl.ds(c * halves[t], halves[t])
            piece = g[t].at[:, j, rows, :]
            for k, chip in enumerate(chips):
                cp = pltpu.make_async_remote_copy(
                    src_ref=piece, dst_ref=piece, send_sem=send.at[3 * t + k], recv_sem=recv.at[3 * t + k],
                    device_id=(*chip, c), device_id_type=MESH)
                cp.start()
                sends.append(cp)
        for t in range(ns):
            cp = pltpu.make_async_copy(s_in[t], s_out[t].at[j], slsem.at[t])
            cp.start()
            slocal.append(cp)
            for k, chip in enumerate(chips):
                cp = pltpu.make_async_remote_copy(
                    src_ref=s_in[t], dst_ref=s_out[t].at[j], send_sem=ssend.at[3 * t + k],
                    recv_sem=srecv.at[3 * t + k], device_id=(*chip, c), device_id_type=MESH)
                cp.start()
                sends.append(cp)
        for t in range(nb):
            rows = pl.ds(c * halves[t], halves[t])
            pend = [None, None]
            n = 0
            for k, chip in enumerate(chips):
                jk = 2 * chip[0] + chip[1]
                landed = g[t].at[:, jk, rows, :]
                pltpu.make_async_remote_copy(
                    src_ref=landed, dst_ref=landed, send_sem=send.at[3 * t + k], recv_sem=recv.at[3 * t + k],
                    device_id=(*chip, c), device_id_type=MESH).wait_recv()
                for l in range(DEPTH):
                    slot = n % 2
                    if pend[slot] is not None:
                        pend[slot].wait_send()
                    part = g[t].at[l, jk, rows, :]
                    ld = pltpu.make_async_copy(part, bufs[t].at[slot], lsem.at[2 * t + slot])
                    ld.start()
                    ld.wait()
                    cp = pltpu.make_async_remote_copy(
                        src_ref=bufs[t].at[slot], dst_ref=part, send_sem=fsend.at[2 * t + slot],
                        recv_sem=frecv.at[t], device_id=(x, y, 1 - c), device_id_type=MESH)
                    cp.start()
                    pend[slot] = cp
                    n += 1
            for cp in pend:
                cp.wait_send()
        for t in range(ns):
            for k, chip in enumerate(chips):
                jk = 2 * chip[0] + chip[1]
                landed = s_out[t].at[jk]
                pltpu.make_async_remote_copy(
                    src_ref=landed, dst_ref=landed, send_sem=ssend.at[3 * t + k], recv_sem=srecv.at[3 * t + k],
                    device_id=(*chip, c), device_id_type=MESH).wait_recv()
        for t in range(nb):
            passed = g[t].at[:, pl.ds(0, 3), pl.ds((1 - c) * halves[t], halves[t]), :]
            pltpu.make_async_remote_copy(
                src_ref=passed, dst_ref=passed, send_sem=fsend.at[2 * t], recv_sem=frecv.at[t],
                device_id=(x, y, 1 - c), device_id_type=MESH).wait_recv()
        for cp in sends:
            cp.wait_send()
        for cp in slocal:
            cp.wait()

    dma = pltpu.SemaphoreType.DMA
    out_shape = [_sds(g.shape, g.dtype) for g in gs] + [_sds((NCHIP,) + s.shape, s.dtype) for s in smalls]
    scratch = [pltpu.VMEM((2, halves[t], gs[t].shape[3]), BF16) for t in range(nb)]
    scratch += [dma((3 * nb,)), dma((3 * nb,)), dma((2 * nb,)), dma((nb,)), dma((2 * nb,)),
                dma((3 * ns,)), dma((3 * ns,)), dma((ns,))]
    return pl.pallas_call(
        body, name="allgather_weights", in_specs=[ANY] * (nb + ns), out_specs=[ANY] * (nb + ns),
        out_shape=out_shape, scratch_shapes=scratch, input_output_aliases={t: t for t in range(nb)},
        compiler_params=pltpu.CompilerParams(has_side_effects=True, vmem_limit_bytes=VMEM_LIMIT))(*gs, *smalls)


def _reduce_scatter(tag, dws, pf):
    ps = []
    for t, dw in enumerate(dws):
        got = _pair_send("rs_pair_%s_%d" % (tag, t), dw, pf)
        ps.append(_add2_bf16("rs_add2_%s_%d" % (tag, t), dw, got, pf))
    got2 = _rs_chips("rs_chips_" + tag, ps)
    return [_add4_join("rs_add4_%s_%d" % (tag, t), p, g2, pf) for t, (p, g2) in enumerate(zip(ps, got2))]


def _rs_begin(tag, dws, pf):
    ps = []
    for t, dw in enumerate(dws):
        got = _pair_send("rs_pair_%s_%d" % (tag, t), dw, pf)
        ps.append(_add2_bf16("rs_add2_%s_%d" % (tag, t), dw, got, pf))
    send, recv, ps_thru, lands, token = _rs_chips_start("rs_chips_start_" + tag, ps)
    return (tag, send, recv, ps_thru, lands), token


def _rs_end(handle, afters, pf):
    tag, send, recv, ps, lands = handle
    ps, got2 = _rs_chips_wait("rs_chips_wait_" + tag, send, recv, ps, lands, afters)
    return [_add4_join("rs_add4_%s_%d" % (tag, t), p, g2, pf) for t, (p, g2) in enumerate(zip(ps, got2))]


def _allreduce_small(pack):
    r = pack.shape[0]
    flips = [(fx, fy, fc) for fx in (0, 1) for fy in (0, 1) for fc in (0, 1)][1:]

    def body(p_ref, o_ref, gat, send, recv):
        x, y, c, _ = _place()
        me = 4 * x + 2 * y + c
        gat[me] = p_ref[...]
        cps = []
        for k, (fx, fy, fc) in enumerate(flips):
            peer = ((1 - x) if fx else x, (1 - y) if fy else y, (1 - c) if fc else c)
            cp = pltpu.make_async_remote_copy(
                src_ref=p_ref, dst_ref=gat.at[me], send_sem=send.at[k], recv_sem=recv.at[k],
                device_id=peer, device_id_type=MESH)
            cp.start()
            cps.append(cp)
        for k, (fx, fy, fc) in enumerate(flips):
            peer = ((1 - x) if fx else x, (1 - y) if fy else y, (1 - c) if fc else c)
            src = 4 * peer[0] + 2 * peer[1] + peer[2]
            pltpu.make_async_remote_copy(
                src_ref=p_ref, dst_ref=gat.at[src], send_sem=send.at[k], recv_sem=recv.at[k],
                device_id=peer, device_id_type=MESH).wait_recv()
        for cp in cps:
            cp.wait_send()
        s = gat[0]
        for d in range(1, 8):
            s = s + gat[d]
        o_ref[...] = s

    dma = pltpu.SemaphoreType.DMA
    vm = pl.BlockSpec(memory_space=pltpu.VMEM)
    return pl.pallas_call(
        body, name="allreduce_small", in_specs=[vm], out_specs=vm, out_shape=_sds((r, LANE), F32),
        scratch_shapes=[pltpu.VMEM((8, r, LANE), F32), dma((7,)), dma((7,))],
        compiler_params=pltpu.CompilerParams(has_side_effects=True))(pack)


def _in_weights(win_g):
    full = jnp.concatenate([win_g[s] for s in range(NCHIP)], axis=1)
    wqkv = full[:, :2048]
    og = full[:, 2048:3072]
    gates = jnp.pad(full[:, 3072:3080], ((0, 0), (0, LANE - 8)))
    u = full[:, 3080:4104]
    gb = full[:, 4104:5128]
    gc = full[:, 5128:6152]
    return wqkv, jnp.stack([og, u, gb, gc]), gates


def _in_grads(dwqkv, dwe, dwgt):
    full = jnp.concatenate([dwqkv, dwe[0], dwgt[:, :8], dwe[1], dwe[2], dwe[3]], axis=1)
    sw = DIN // NCHIP
    return jnp.stack([full[:, s * sw:(s + 1) * sw] for s in range(NCHIP)])


def _layer_fwd(l, h, get_mix, get_ffn, small):
    lp = h.shape[0]
    th = lp // 2
    wqkv, we, wgt, wout_g = get_mix(h)
    nmw, bias, mnw, cw, nfw = small
    tag = "_l%d" % l
    hn = _norm_fwd("norm_mix" + tag, h, nmw)
    qkv = _mm("proj_qkv" + tag, "nn", hn, wqkv,
              pl.BlockSpec((lp, D), lambda i, j, k: (0, 0)), pl.BlockSpec((D, 512), lambda i, j, k: (0, j)),
              pl.BlockSpec((lp, 512), lambda i, j, k: (0, j)), _sds((lp, 2048), BF16), (1, 4, 1))
    e = _mm("proj_e" + tag, "nn", hn, we,
            pl.BlockSpec((lp, D), lambda i, j, k: (0, 0)), pl.BlockSpec((None, D, 512), lambda i, j, k: (j // 2, 0, j % 2)),
            pl.BlockSpec((None, lp, 512), lambda i, j, k: (j // 2, 0, j % 2)), _sds((4, lp, 1024), BF16), (1, 8, 1))
    gpre = _mm("proj_gates" + tag, "nn", hn, wgt,
               pl.BlockSpec((lp, D), lambda i, j, k: (0, 0)), pl.BlockSpec((D, LANE), lambda i, j, k: (0, 0)),
               pl.BlockSpec((lp, LANE), lambda i, j, k: (0, 0)), _sds((lp, LANE), F32), (1, 1, 1))
    grow, mcol, ccol = _gate_prep("gate_prep" + tag, gpre, bias)
    ht = _mlstm_fwd("mlstm_fwd" + tag, qkv, grow, mcol, ccol, 4)
    mix = _mix_fwd("mix_fwd" + tag, ht, e, mnw, cw)
    wout = wout_g.reshape(D, D)
    h1 = _mm("out_proj" + tag, "nn", mix, wout,
             pl.BlockSpec((None, th, 1024), lambda i, j, k: (k, i, 0)),
             pl.BlockSpec((1024, 1024), lambda i, j, k: (k, j)),
             pl.BlockSpec((th, 1024), lambda i, j, k: (i, j)), _sds((lp, D), F32), (2, 2, 2),
             acc_shape=(th, 1024), res=h, res_spec=pl.BlockSpec((th, 1024), lambda i, j, k: (i, j)))
    wg_g, wu_g, wd_g = get_ffn(h1)
    hf = _norm_fwd("norm_ffn" + tag, h1, nfw)
    g, u, a = _ffn_up("ffn_up" + tag, hf, wg_g, wu_g)
    tk = DFF // NCHIP
    h2 = _mm("ffn_down" + tag, "nn", a, wd_g,
             pl.BlockSpec((th, tk), lambda i, j, k: (i, k)),
             pl.BlockSpec((None, tk, 1024), lambda i, j, k: (k, 0, j)),
             pl.BlockSpec((th, 1024), lambda i, j, k: (i, j)), _sds((lp, D), F32), (2, 2, NCHIP),
             acc_shape=(th, 1024), res=h1, res_spec=pl.BlockSpec((th, 1024), lambda i, j, k: (i, j)))
    saved = (h, hn, qkv, e, gpre, grow, mcol, ccol, ht, mix, h1, hf, g, u, a)
    return h2, saved, (wqkv, we, wgt, wout_g, wg_g, wu_g, wd_g)


def _layer_bwd(l, dh2, dh2b, saved, wts, small, ffn_done):
    h, hn, qkv, e, gpre, grow, mcol, ccol, ht, mix, h1, hf, g, u, a = saved
    wqkv, we, wgt, wout_g, wg_g, wu_g, wd_g = wts
    nmw, bias, mnw, cw, nfw = small
    lp = h.shape[0]
    th = lp // 2
    tk = DFF // NCHIP
    tag = "_l%d" % l
    half_rows = lambda i, j, k: (i, j)

    dwd = _mm("dw_down" + tag, "tn", a, dh2b,
              pl.BlockSpec((lp, tk), lambda i, j, k: (0, i)), pl.BlockSpec((lp, 1024), lambda i, j, k: (0, j)),
              pl.BlockSpec((None, tk, 1024), lambda i, j, k: (i, 0, j)), _sds((NCHIP, tk, D), BF16), (NCHIP, 2, 1))
    dg, du = _ffn_bwd_act("ffn_bwd_act" + tag, dh2b, wd_g, g, u)
    dws = []
    for nm, dact in (("gate", dg), ("up", du)):
        dws.append(_mm("dw_%s%s" % (nm, tag), "tn", hf, dact,
                       pl.BlockSpec((lp, 1024), lambda i, j, k: (0, i)), pl.BlockSpec((lp, tk), lambda i, j, k: (0, j)),
                       pl.BlockSpec((None, 1024, tk), lambda i, j, k: (j, i, 0)), _sds((NCHIP, D, tk), BF16),
                       (2, NCHIP, 1)))
    dwg, dwu = dws
    dhf = None
    for nm, dact, wfull in (("gate", dg, wg_g), ("up", du, wu_g)):
        dhf = _mm("dhf_%s%s" % (nm, tag), "nt", dact, wfull,
                  pl.BlockSpec((th, tk), lambda i, j, k: (i, k)),
                  pl.BlockSpec((None, 1024, tk), lambda i, j, k: (k, j, 0)),
                  pl.BlockSpec((th, 1024), half_rows), _sds((lp, D), F32), (2, 2, NCHIP), acc_shape=(th, 1024),
                  res=dhf, res_spec=None if dhf is None else pl.BlockSpec((th, 1024), half_rows))
    dh1, dh1b, dnfw = _norm_bwd("norm_ffn_bwd" + tag, dhf, h1, nfw, dh2)
    dh1b = ffn_done([dwg, dwu, dwd], dh1b)

    dwout = _mm("dw_out" + tag, "tn", mix, dh1b,
                pl.BlockSpec((None, lp, 1024), lambda i, j, k: (i, 0, 0)), pl.BlockSpec((lp, 1024), lambda i, j, k: (0, j)),
                pl.BlockSpec((1024, 1024), half_rows), _sds((D, D), BF16), (2, 2, 1))
    wout = wout_g.reshape(D, D)
    dmix = _mm("dmix" + tag, "nt", dh1b, wout,
               pl.BlockSpec((th, D), lambda i, j, k: (i, 0)), pl.BlockSpec((1024, D), lambda i, j, k: (j, 0)),
               pl.BlockSpec((th, 1024), half_rows), _sds((lp, D), F32), (2, 2, 1))
    dht, de, dmnw, dcw = _mix_bwd("mix_bwd" + tag, dmix, ht, e, mnw, cw)
    dq, dk, dv, dgrow, dfx = _mlstm_bwd("mlstm_bwd" + tag, qkv, grow, mcol, ccol, ht, dht, 8)
    dgp, dgpb, dbias = _gate_bwd("gate_bwd" + tag, gpre, bias, dgrow, dfx)
    del dgp
    dqkv = jnp.concatenate([dq, dk, dv], axis=1)

    hn_cols = pl.BlockSpec((lp, 1024), lambda i, j, k: (0, i))
    dwqkv = _mm("dw_qkv" + tag, "tn", hn, dqkv, hn_cols, pl.BlockSpec((lp, 1024), lambda i, j, k: (0, j)),
                pl.BlockSpec((1024, 1024), half_rows), _sds((D, 2048), BF16), (2, 2, 1))
    dwe = _mm("dw_e" + tag, "tn", hn, de, hn_cols, pl.BlockSpec((None, lp, 1024), lambda i, j, k: (j, 0, 0)),
              pl.BlockSpec((None, 1024, 1024), lambda i, j, k: (j, i, 0)), _sds((4, D, 1024), BF16), (2, 4, 1))
    dwgt = _mm("dw_gates" + tag, "tn", hn, dgpb, hn_cols, pl.BlockSpec((lp, LANE), lambda i, j, k: (0, 0)),
               pl.BlockSpec((1024, LANE), lambda i, j, k: (i, 0)), _sds((D, LANE), BF16), (2, 1, 1))
    dwin = _in_grads(dwqkv, dwe, dwgt)

    dhn = _mm("dhn_qkv" + tag, "nt", dqkv, wqkv,
              pl.BlockSpec((th, 2048), lambda i, j, k: (i, 0)), pl.BlockSpec((1024, 2048), lambda i, j, k: (j, 0)),
              pl.BlockSpec((th, 1024), half_rows), _sds((lp, D), F32), (2, 2, 1))
    dhn = _mm("dhn_e" + tag, "nt", de, we,
              pl.BlockSpec((None, th, 1024), lambda i, j, k: (k, i, 0)),
              pl.BlockSpec((None, 1024, 1024), lambda i, j, k: (k, j, 0)),
              pl.BlockSpec((th, 1024), half_rows), _sds((lp, D), F32), (2, 2, 4), acc_shape=(th, 1024),
              res=dhn, res_spec=pl.BlockSpec((th, 1024), half_rows))
    dhn = _mm("dhn_gates" + tag, "nt", dgpb, wgt,
              pl.BlockSpec((th, LANE), lambda i, j, k: (i, 0)), pl.BlockSpec((1024, LANE), lambda i, j, k: (j, 0)),
              pl.BlockSpec((th, 1024), half_rows), _sds((lp, D), F32), (2, 2, 1),
              res=dhn, res_spec=pl.BlockSpec((th, 1024), half_rows))
    dh0, dh0b, dnmw = _norm_bwd("norm_mix_bwd" + tag, dhn, h, nmw, dh1)

    pieces = [dwin, dwout.reshape(NCHIP, D // NCHIP, D)]
    smalls = (dnmw[0], dbias[0, :8], dcw[:3], dmnw[0], dnfw[0])
    return dh0, dh0b, pieces, smalls


def _pack_rows(parts):
    rows = []
    for p in parts:
        f = p.reshape(-1)
        pad = (-f.shape[0]) % LANE
        if pad:
            f = jnp.pad(f, (0, pad))
        rows.append(f.reshape(-1, LANE))
    r = jnp.concatenate(rows, axis=0)
    pad = (-r.shape[0]) % 8
    if pad:
        r = jnp.pad(r, ((0, pad), (0, 0)))
    return r


def _unpack_rows(pack, shapes):
    out, r0 = [], 0
    for s in shapes:
        n = 1
        for d in s:
            n *= d
        nr = -(-n // LANE)
        out.append(pack[r0:r0 + nr].reshape(-1)[:n].reshape(s))
        r0 += nr
    return out


def kernel(x, meta_tokens, norm_mix_w, w_in, b_gates, conv_w, mlstm_norm_w, w_out, norm_ffn_w, w_gate, w_up, w_down, norm_final_w, loss_target, m_meta_tokens, m_norm_mix_w, m_w_in, m_b_gates, m_conv_w, m_mlstm_norm_w, m_w_out, m_norm_ffn_w, m_w_gate, m_w_up, m_w_down, m_norm_final_w, v_meta_tokens, v_norm_mix_w, v_w_in, v_b_gates, v_conv_w, v_mlstm_norm_w, v_w_out, v_norm_ffn_w, v_w_gate, v_w_up, v_w_down, v_norm_final_w):
    seq = x.shape[1]
    n_real = N_META + seq
    lp = -(-n_real // LANE) * LANE
    xi, yi, ci = lax.axis_index("x"), lax.axis_index("y"), lax.axis_index("c")
    jchip = 2 * xi + yi
    pf = jnp.stack([ci, jchip, 2 * (1 - xi) + yi, 2 * xi + (1 - yi), 2 * (1 - xi) + (1 - yi)]).astype(jnp.int32)

    big = {"w_in": w_in, "w_out": w_out, "w_gate": w_gate, "w_up": w_up, "w_down": w_down}
    cast = {n: [_cast_into("cast_%s_l%d" % (n, l), w, l, pf) for l in range(DEPTH)] for n, w in big.items()}
    conv_flat = jnp.pad(conv_w.reshape(DEPTH * 3, CW // NCHIP), ((0, 8 - DEPTH * 3), (0, 0)))

    def own_slot(a):
        return lax.dynamic_update_slice(jnp.zeros((NCHIP,) + a.shape, a.dtype), a[None], (jchip, 0, 0))

    groups, splits = [], []
    for l in range(DEPTH):
        groups.append([cast["w_in"][l], cast["w_out"][l]] + ([own_slot(meta_tokens), own_slot(conv_flat)] if l == 0 else []))
        splits.append([True, True] + ([False, False] if l == 0 else []))
        groups.append([cast[n][l] for n in ("w_gate", "w_up", "w_down")])
        splits.append([True, True, True])
    sems, arrs = _ag_start(groups, splits)

    def gathered(gi, afters):
        got = _ag_wait("ag_wait_%d" % gi, arrs[gi], splits[gi], sems[gi][0], sems[gi][1], afters)
        nsplit = sum(splits[gi])
        return list(_ag_forward("ag_forward_%d" % gi, got[:nsplit])) + list(got[nsplit:])

    win0_g, wout0_g, meta_g, conv_g = gathered(0, [])
    meta_full = jnp.concatenate([meta_g[s] for s in range(NCHIP)], axis=1)
    conv_full = jnp.concatenate([conv_g[s][:DEPTH * 3] for s in range(NCHIP)], axis=1)
    conv_full = conv_full.reshape(DEPTH, 3, CW)

    bias_rows = jnp.pad(b_gates, ((0, 0), (0, LANE - 8)))
    smalls = []
    for l in range(DEPTH):
        smalls.append((norm_mix_w[l][None], bias_rows[l][None], mlstm_norm_w[l][None],
                       jnp.pad(conv_full[l], ((0, 5), (0, 0))), norm_ffn_w[l][None]))

    h = jnp.concatenate([meta_full, x[0], jnp.zeros((lp - n_real, D), F32)], axis=0)
    saved, wts = [], []
    for l in range(DEPTH):
        def get_mix(h_in, l=l):
            win_g, wout_g = (win0_g, wout0_g) if l == 0 else gathered(2 * l, [h_in])
            return _in_weights(win_g) + (wout_g,)

        def get_ffn(h1, l=l):
            return tuple(gathered(2 * l + 1, [h1]))

        h, sv, wt = _layer_fwd(l, h, get_mix, get_ffn, smalls[l])
        saved.append(sv)
        wts.append(wt)
    tgt = jnp.pad(loss_target[0], ((N_META, lp - n_real), (0, 0)))
    dh, dhb, dnorm_final, loss_part = _loss_head(h, tgt, norm_final_w[None], n_real)

    names = ["w_in", "w_out", "w_gate", "w_up", "w_down"]
    params = {"w_in": (w_in, m_w_in, v_w_in), "w_out": (w_out, m_w_out, v_w_out), "w_gate": (w_gate, m_w_gate, v_w_gate),
              "w_up": (w_up, m_w_up, v_w_up), "w_down": (w_down, m_w_down, v_w_down)}
    big_out = {n: None for n in names}
    small_grads = [None] * DEPTH
    def finish(l, group, handle, afters):
        for n, (g_mine, g_theirs) in zip(group, _rs_end(handle, afters, pf)):
            w, m, v = params[n]
            big_out[n] = _adamw_layer("adamw_%s_l%d" % (n, l), l, g_mine, g_theirs, w, m, v, big_out[n], pf)
        return [big_out[n][3] for n in group]

    groups = []
    token = None
    for l in reversed(range(DEPTH)):
        def ffn_done(pieces, dh1b, l=l):
            handle, tok = _rs_begin("l%df" % l, pieces, pf)
            groups.append((l, names[2:], handle))
            return dh1b + tok[0, 0].astype(BF16)

        dh, dhb, pieces, small_grads[l] = _layer_bwd(l, dh, dhb, saved[l], wts[l], smalls[l], ffn_done)
        handle, token = _rs_begin("l%dm" % l, pieces, pf)
        groups.append((l, names[:2], handle))
        if l > 0:
            dhb = dhb + token[0, 0].astype(BF16)
    afters = [token]
    for l, group, handle in groups:
        afters = finish(l, group, handle, afters)

    dnmw = jnp.stack([small_grads[l][0] for l in range(DEPTH)])
    dbias = jnp.stack([small_grads[l][1] for l in range(DEPTH)])
    dconv = jnp.stack([small_grads[l][2] for l in range(DEPTH)])
    dmnw = jnp.stack([small_grads[l][3] for l in range(DEPTH)])
    dnfw = jnp.stack([small_grads[l][4] for l in range(DEPTH)])
    part_shapes = [(N_META, D), (DEPTH, D), (DEPTH, 8), (DEPTH, 3, CW), (DEPTH, MW), (DEPTH, D), (D,), (LANE,)]
    pack = _pack_rows([dh[:N_META], dnmw, dbias, dconv, dmnw, dnfw, dnorm_final[0], loss_part[0]])
    tot = _unpack_rows(_allreduce_small(pack), part_shapes)
    g_meta_full, g_nmw, g_bias, g_conv_full, g_mnw, g_nfw, g_final, loss_row = tot
    mcols = D // NCHIP
    ccols = CW // NCHIP
    g_meta = lax.dynamic_slice_in_dim(g_meta_full, jchip * mcols, mcols, axis=1)
    g_conv = lax.dynamic_slice_in_dim(g_conv_full, jchip * ccols, ccols, axis=2)
    sm_g = [g_meta, g_nmw, g_bias, g_conv, g_mnw, g_nfw, g_final]
    sm_w = [meta_tokens, norm_mix_w, b_gates, conv_w, mlstm_norm_w, norm_ffn_w, norm_final_w]
    sm_m = [m_meta_tokens, m_norm_mix_w, m_b_gates, m_conv_w, m_mlstm_norm_w, m_norm_ffn_w, m_norm_final_w]
    sm_v = [v_meta_tokens, v_norm_mix_w, v_b_gates, v_conv_w, v_mlstm_norm_w, v_norm_ffn_w, v_norm_final_w]
    sm_shapes = [w.shape for w in sm_w]
    d_p, m_p, v_p = _adamw_flat(_pack_rows(sm_g), _pack_rows(sm_w), _pack_rows(sm_m), _pack_rows(sm_v))
    sm_d = _unpack_rows(d_p, sm_shapes)
    sm_nm = _unpack_rows(m_p, sm_shapes)
    sm_nv = _unpack_rows(v_p, sm_shapes)

    loss = loss_row[0]
    grad_x = dh[N_META:n_real][None]

    def ordered(sm, which):
        bo = {n: big_out[n][which] for n in names}
        return [sm[0], sm[1], bo["w_in"], sm[2], sm[3], sm[4], bo["w_out"], sm[5], bo["w_gate"], bo["w_up"], bo["w_down"], sm[6]]

    return (loss, grad_x, *ordered(sm_g, 0), *ordered(sm_d, 1), *ordered(sm_nm, 2), *ordered(sm_nv, 3))
```

```python
import functools

import jax
import jax.numpy as jnp
from jax import lax
from jax.experimental import pallas as pl
from jax.experimental.pallas import tpu as pltpu

F32 = jnp.float32
BF16 = jnp.bfloat16

D = 2048
N_META = 16
HEADS = 4
DQK = 128
DV = 256
MW = HEADS * DV
CW = D - MW
QKW = HEADS * DQK
DFF = 5632
DIN = 6152
NCHIP = 4
DEPTH = 2
GATE_CAP = 15.0
EPS = 1e-6
QK_SCALE = DQK ** -0.5
LANE = 128
VMEM_LIMIT = 60 * 1024 * 1024

ADAM_LR = 0.001
ADAM_B1 = 0.9
ADAM_B2 = 0.999
ADAM_EPS = 1e-08
ADAM_WD = 0.01
ADAM_STEP = 10

MESH = pl.DeviceIdType.MESH
ANY = pl.BlockSpec(memory_space=pl.ANY)


def _cp(sem):
    return pltpu.CompilerParams(dimension_semantics=sem, vmem_limit_bytes=VMEM_LIMIT)


def _sds(shape, dtype):
    return jax.ShapeDtypeStruct(shape, dtype)


_DN = {"nn": (((1,), (0,)), ((), ())), "nt": (((1,), (1,)), ((), ())), "tn": (((0,), (0,)), ((), ()))}


def _mm(name, kind, a, b, a_spec, b_spec, o_spec, out_shape, grid, acc_shape=None, res=None, res_spec=None):
    nk = grid[2]
    has_res = res is not None

    def body(*refs):
        if has_res:
            a_ref, b_ref, r_ref, o_ref = refs[:4]
        else:
            a_ref, b_ref, o_ref = refs[:3]
            r_ref = None
        p = lax.dot_general(a_ref[...], b_ref[...], _DN[kind], preferred_element_type=F32)
        if nk == 1:
            if r_ref is not None:
                p = p + r_ref[...]
            o_ref[...] = p.astype(o_ref.dtype)
        else:
            acc = refs[-1]
            k = pl.program_id(2)

            @pl.when(k == 0)
            def _():
                acc[...] = p

            @pl.when(k > 0)
            def _():
                acc[...] += p

            @pl.when(k == nk - 1)
            def _():
                r = acc[...]
                if r_ref is not None:
                    r = r + r_ref[...]
                o_ref[...] = r.astype(o_ref.dtype)

    ins = [a, b] + ([res] if has_res else [])
    in_specs = [a_spec, b_spec] + ([res_spec] if has_res else [])
    scratch = [pltpu.VMEM(acc_shape, F32)] if nk > 1 else []
    return pl.pallas_call(
        body, name=name, grid=grid, in_specs=in_specs, out_specs=o_spec, out_shape=out_shape,
        scratch_shapes=scratch, compiler_params=_cp(("parallel", "parallel", "arbitrary")))(*ins)


def _norm_fwd(name, h, w):
    lp = h.shape[0]
    tm = lp // 4

    def body(h_ref, w_ref, o_ref):
        x = h_ref[...]
        r = lax.rsqrt(jnp.mean(x * x, axis=1, keepdims=True) + EPS)
        o_ref[...] = (x * r * w_ref[...]).astype(BF16)

    return pl.pallas_call(
        body, name=name, grid=(4,),
        in_specs=[pl.BlockSpec((tm, D), lambda i: (i, 0)), pl.BlockSpec((1, D), lambda i: (0, 0))],
        out_specs=pl.BlockSpec((tm, D), lambda i: (i, 0)), out_shape=_sds((lp, D), BF16),
        compiler_params=_cp(("parallel",)))(h, w)


def _norm_bwd(name, dy, h, w, dres):
    lp = h.shape[0]
    tm = lp // 8

    def body(dy_ref, h_ref, w_ref, dres_ref, dh_ref, dhb_ref, dw_ref):
        x = h_ref[...]
        r = lax.rsqrt(jnp.mean(x * x, axis=1, keepdims=True) + EPS)
        xh = x * r
        dy_v = dy_ref[...]
        dxh = dy_v * w_ref[...]
        dx = r * (dxh - xh * jnp.mean(dxh * xh, axis=1, keepdims=True))
        dh = dres_ref[...] + dx
        dh_ref[...] = dh
        dhb_ref[...] = dh.astype(BF16)

        @pl.when(pl.program_id(0) == 0)
        def _():
            dw_ref[...] = jnp.zeros_like(dw_ref)

        dw_ref[0:1, :] += jnp.sum(dy_v * xh, axis=0, keepdims=True)

    row = pl.BlockSpec((tm, D), lambda i: (i, 0))
    return pl.pallas_call(
        body, name=name, grid=(8,),
        in_specs=[row, row, pl.BlockSpec((1, D), lambda i: (0, 0)), row],
        out_specs=[row, row, pl.BlockSpec((8, D), lambda i: (0, 0))],
        out_shape=[_sds((lp, D), F32), _sds((lp, D), BF16), _sds((8, D), F32)],
        compiler_params=_cp(("arbitrary",)))(dy, h, w, dres)


def _loss_head(h, tgt, w, n_real):
    lp = h.shape[0]
    tm = lp // 8

    def body(h_ref, t_ref, w_ref, dh_ref, dhb_ref, dw_ref, loss_ref):
        i = pl.program_id(0)
        x = h_ref[...]
        r = lax.rsqrt(jnp.mean(x * x, axis=1, keepdims=True) + EPS)
        xh = x * r
        wv = w_ref[...]
        row = i * tm + lax.broadcasted_iota(jnp.int32, (tm, 1), 0)
        valid = jnp.logical_and(row >= N_META, row < n_real)
        err = jnp.where(valid, xh * wv - t_ref[...], 0.0)
        dy_v = err * (1.0 / D)
        dxh = dy_v * wv
        dx = r * (dxh - xh * jnp.mean(dxh * xh, axis=1, keepdims=True))
        dh_ref[...] = dx
        dhb_ref[...] = dx.astype(BF16)

        @pl.when(i == 0)
        def _():
            dw_ref[...] = jnp.zeros_like(dw_ref)
            loss_ref[...] = jnp.zeros_like(loss_ref)

        dw_ref[0:1, :] += jnp.sum(dy_v * xh, axis=0, keepdims=True)
        part = jnp.sum(jnp.sum(err * err, axis=1, keepdims=True), axis=0, keepdims=True) * (0.5 / D)
        loss_ref[...] += jnp.broadcast_to(part, loss_ref.shape)

    row = pl.BlockSpec((tm, D), lambda i: (i, 0))
    return pl.pallas_call(
        body, name="loss_head", grid=(8,),
        in_specs=[row, row, pl.BlockSpec((1, D), lambda i: (0, 0))],
        out_specs=[row, row, pl.BlockSpec((8, D), lambda i: (0, 0)), pl.BlockSpec((8, LANE), lambda i: (0, 0))],
        out_shape=[_sds((lp, D), F32), _sds((lp, D), BF16), _sds((8, D), F32), _sds((8, LANE), F32)],
        compiler_params=_cp(("arbitrary",)))(h, tgt, w)


def _ffn_up(name, hf, wg4, wu4):
    lp = hf.shape[0]
    tm = lp // 4
    tn = DFF // NCHIP

    def body(x_ref, wg_ref, wu_ref, g_ref, u_ref, a_ref):
        x = x_ref[...]
        g = jnp.dot(x, wg_ref[...], preferred_element_type=F32)
        u = jnp.dot(x, wu_ref[...], preferred_element_type=F32)
        g_ref[...] = g.astype(BF16)
        u_ref[...] = u.astype(BF16)
        a_ref[...] = (g * jax.nn.sigmoid(g) * u).astype(BF16)

    wspec = pl.BlockSpec((None, D, tn), lambda j, i: (j, 0, 0))
    ospec = pl.BlockSpec((tm, tn), lambda j, i: (i, j))
    o = _sds((lp, DFF), BF16)
    return pl.pallas_call(
        body, name=name, grid=(NCHIP, 4),
        in_specs=[pl.BlockSpec((tm, D), lambda j, i: (i, 0)), wspec, wspec],
        out_specs=[ospec, ospec, ospec], out_shape=[o, o, o],
        compiler_params=_cp(("parallel", "parallel")))(hf, wg4, wu4)


def _ffn_bwd_act(name, dhb, wd4, g, u):
    lp = dhb.shape[0]
    tm = lp // 4
    tn = DFF // NCHIP

    def body(d_ref, w_ref, g_ref, u_ref, dg_ref, du_ref):
        da = lax.dot_general(d_ref[...], w_ref[...], _DN["nt"], preferred_element_type=F32)
        gv = g_ref[...].astype(F32)
        uv = u_ref[...].astype(F32)
        sg = jax.nn.sigmoid(gv)
        dg_ref[...] = (da * uv * (sg * (1.0 + gv * (1.0 - sg)))).astype(BF16)
        du_ref[...] = (da * (gv * sg)).astype(BF16)

    ospec = pl.BlockSpec((tm, tn), lambda j, i: (i, j))
    o = _sds((lp, DFF), BF16)
    return pl.pallas_call(
        body, name=name, grid=(NCHIP, 4),
        in_specs=[pl.BlockSpec((tm, D), lambda j, i: (i, 0)),
                  pl.BlockSpec((None, tn, D), lambda j, i: (j, 0, 0)), ospec, ospec],
        out_specs=[ospec, ospec], out_shape=[o, o],
        compiler_params=_cp(("parallel", "parallel")))(dhb, wd4, g, u)


def _shift_rows(x, d, row):
    return jnp.where(row >= d, pltpu.roll(x, d, axis=0), 0.0)


def _scan_steps(lp):
    d = 1
    while d < lp:
        yield d
        d *= 2


def _gate_values(pre):
    t = GATE_CAP * jnp.tanh(pre * (1.0 / GATE_CAP))
    lf = jnp.minimum(t, 0.0) - jnp.log(1.0 + jnp.exp(-jnp.abs(t)))
    return t, lf


def _gate_prep(name, gates_pre, bias):
    lp = gates_pre.shape[0]

    def body(p_ref, b_ref, grow_ref, m_ref, c_ref):
        pre = p_ref[...] + b_ref[...]
        lane = lax.broadcasted_iota(jnp.int32, (lp, LANE), 1)
        row = lax.broadcasted_iota(jnp.int32, (lp, LANE), 0)
        t, lf = _gate_values(pre)
        f = jnp.where(jnp.logical_and(lane >= HEADS, lane < 2 * HEADS), lf, 0.0)
        for d in _scan_steps(lp):
            f = f + _shift_rows(f, d, row)
        fs = pltpu.roll(f, LANE - HEADS, axis=1)
        g = jnp.where(lane < HEADS, t - fs, 0.0)
        m = g
        for d in _scan_steps(lp):
            m = jnp.maximum(m, jnp.where(row >= d, pltpu.roll(m, d, axis=0), m))
        grow_ref[...] = g.T
        m_ref[...] = m
        c_ref[...] = jnp.where(lane < HEADS, -fs - m, 0.0)

    full = pl.BlockSpec((lp, LANE), lambda: (0, 0))
    return pl.pallas_call(
        body, name=name, in_specs=[full, pl.BlockSpec((1, LANE), lambda: (0, 0))],
        out_specs=[pl.BlockSpec((LANE, lp), lambda: (0, 0)), full, full],
        out_shape=[_sds((LANE, lp), F32), _sds((lp, LANE), F32), _sds((lp, LANE), F32)],
        compiler_params=pltpu.CompilerParams(vmem_limit_bytes=VMEM_LIMIT))(gates_pre, bias)


def _pick_lane(blk, h):
    lane = lax.broadcasted_iota(jnp.int32, blk.shape, 1)
    return jnp.sum(jnp.where(lane == h, blk, 0.0), axis=1, keepdims=True)


def _mlstm_weights(q, k, grow, mcol, i, bq, lp):
    s = lax.dot_general(q, k, _DN["nt"], preferred_element_type=F32) * QK_SCALE
    row = i * bq + lax.broadcasted_iota(jnp.int32, (bq, 1), 0)
    col = lax.broadcasted_iota(jnp.int32, (1, lp), 1)
    a = jnp.where(col <= row, jnp.exp(jnp.minimum(grow - mcol, 0.0)), 0.0)
    return s, a


def _mlstm_fwd(name, qkv, grow, mcol_all, ccol_all, nq):
    lp = qkv.shape[0]
    bq = lp // nq

    def body(q_ref, k_ref, v_ref, grow_ref, m_ref, c_ref, o_ref):
        h = pl.program_id(0)
        i = pl.program_id(1)
        grow_h = grow_ref[pl.ds(h, 1), :]
        mcol = _pick_lane(m_ref[...], h)
        ccol = _pick_lane(c_ref[...], h)
        s, a = _mlstm_weights(q_ref[...], k_ref[...], grow_h, mcol, i, bq, lp)
        p = a * s
        den = jnp.sum(p, axis=1, keepdims=True)
        num = jnp.dot(p.astype(BF16), v_ref[...], preferred_element_type=F32)
        o_ref[...] = num / jnp.maximum(jnp.abs(den), jnp.exp(ccol))

    return pl.pallas_call(
        body, name=name, grid=(HEADS, nq),
        in_specs=[pl.BlockSpec((bq, DQK), lambda h, i: (i, h)),
                  pl.BlockSpec((lp, DQK), lambda h, i: (0, HEADS + h)),
                  pl.BlockSpec((lp, DV), lambda h, i: (0, HEADS + h)),
                  pl.BlockSpec((8, lp), lambda h, i: (0, 0)),
                  pl.BlockSpec((bq, LANE), lambda h, i: (i, 0)),
                  pl.BlockSpec((bq, LANE), lambda h, i: (i, 0))],
        out_specs=pl.BlockSpec((bq, DV), lambda h, i: (i, h)),
        out_shape=_sds((lp, MW), F32),
        compiler_params=_cp(("parallel", "parallel")))(qkv, qkv, qkv, grow, mcol_all, ccol_all)


def _mlstm_bwd(name, qkv, grow, mcol_all, ccol_all, ht, dht, nq):
    lp = qkv.shape[0]
    bq = lp // nq

    def body(q_ref, k_ref, v_ref, grow_ref, m_ref, c_ref, ht_ref, dht_ref,
             dq_ref, dk_ref, dv_ref, dgrow_ref, dfx_ref, dkt_acc, dvt_acc):
        h = pl.program_id(0)
        i = pl.program_id(1)

        @pl.when(jnp.logical_and(h == 0, i == 0))
        def _():
            dgrow_ref[...] = jnp.zeros_like(dgrow_ref)
            dfx_ref[...] = jnp.zeros_like(dfx_ref)

        @pl.when(i == 0)
        def _():
            dkt_acc[...] = jnp.zeros_like(dkt_acc)
            dvt_acc[...] = jnp.zeros_like(dvt_acc)

        q = q_ref[...]
        k = k_ref[...]
        v = v_ref[...]
        grow_h = grow_ref[pl.ds(h, 1), :]
        mcol = _pick_lane(m_ref[...], h)
        ccol = _pick_lane(c_ref[...], h)
        s, a = _mlstm_weights(q, k, grow_h, mcol, i, bq, lp)
        p = a * s
        den = jnp.sum(p, axis=1, keepdims=True)
        clamp = jnp.exp(ccol)
        active = jnp.abs(den) < clamp
        dd = jnp.maximum(jnp.abs(den), clamp)
        dht_v = dht_ref[...]
        hdh = jnp.sum(dht_v * ht_ref[...], axis=1, keepdims=True)
        dn = (dht_v / dd).astype(BF16)
        dden = jnp.where(active, 0.0, -(hdh / dd) * jnp.sign(den))
        dp = lax.dot_general(dn, v, _DN["nt"], preferred_element_type=F32) + dden
        rmat = dp * p
        dgrow_ref[pl.ds(h, 1), :] += jnp.sum(rmat, axis=0, keepdims=True)
        ds = (dp * a * QK_SCALE).astype(BF16)
        dq_ref[...] = jnp.dot(ds, k, preferred_element_type=F32).astype(BF16)
        dkt_acc[...] += lax.dot_general(q, ds, _DN["tn"], preferred_element_type=F32)
        dvt_acc[...] += lax.dot_general(dn, p.astype(BF16), _DN["tn"], preferred_element_type=F32)
        lane = lax.broadcasted_iota(jnp.int32, (bq, LANE), 1)
        r0 = pl.multiple_of(i * bq, 16)
        dfx_ref[pl.ds(r0, bq), :] += jnp.where(lane == h, jnp.sum(rmat, axis=1, keepdims=True), 0.0)

        @pl.when(i == nq - 1)
        def _():
            dk_ref[...] = dkt_acc[...].T.astype(BF16)
            dv_ref[...] = dvt_acc[...].T.astype(BF16)

    return pl.pallas_call(
        body, name=name, grid=(HEADS, nq),
        in_specs=[pl.BlockSpec((bq, DQK), lambda h, i: (i, h)),
                  pl.BlockSpec((lp, DQK), lambda h, i: (0, HEADS + h)),
                  pl.BlockSpec((lp, DV), lambda h, i: (0, HEADS + h)),
                  pl.BlockSpec((8, lp), lambda h, i: (0, 0)),
                  pl.BlockSpec((bq, LANE), lambda h, i: (i, 0)),
                  pl.BlockSpec((bq, LANE), lambda h, i: (i, 0)),
                  pl.BlockSpec((bq, DV), lambda h, i: (i, h)),
                  pl.BlockSpec((bq, DV), lambda h, i: (i, h))],
        out_specs=[pl.BlockSpec((bq, DQK), lambda h, i: (i, h)),
                   pl.BlockSpec((lp, DQK), lambda h, i: (0, h)),
                   pl.BlockSpec((lp, DV), lambda h, i: (0, h)),
                   pl.BlockSpec((LANE, lp), lambda h, i: (0, 0)),
                   pl.BlockSpec((lp, LANE), lambda h, i: (0, 0))],
        out_shape=[_sds((lp, QKW), BF16), _sds((lp, QKW), BF16), _sds((lp, MW), BF16),
                   _sds((LANE, lp), F32), _sds((lp, LANE), F32)],
        scratch_shapes=[pltpu.VMEM((DQK, lp), F32), pltpu.VMEM((DV, lp), F32)],
        compiler_params=_cp(("arbitrary", "arbitrary")))(qkv, qkv, qkv, grow, mcol_all, ccol_all, ht, dht)


def _gate_bwd(name, gates_pre, bias, dgrow, dfx):
    lp = gates_pre.shape[0]

    def body(p_ref, b_ref, dgrow_ref, dfx_ref, dg_ref, dgb_ref, db_ref):
        pre = p_ref[...] + b_ref[...]
        lane = lax.broadcasted_iota(jnp.int32, (lp, LANE), 1)
        row = lax.broadcasted_iota(jnp.int32, (lp, LANE), 0)
        th = jnp.tanh(pre * (1.0 / GATE_CAP))
        t = GATE_CAP * th
        dgc = jnp.where(lane < HEADS, dgrow_ref[...].T, 0.0)
        df = jnp.where(lane < HEADS, dfx_ref[...] - dgc, 0.0)
        for d in _scan_steps(lp):
            df = df + jnp.where(row < lp - d, pltpu.roll(df, lp - d, axis=0), 0.0)
        dlf = pltpu.roll(df, HEADS, axis=1)
        dt = jnp.where(lane < HEADS, dgc, dlf * jax.nn.sigmoid(-t))
        dpre = jnp.where(lane < 2 * HEADS, dt * (1.0 - th * th), 0.0)
        dg_ref[...] = dpre
        dgb_ref[...] = dpre.astype(BF16)
        db_ref[...] = jnp.broadcast_to(jnp.sum(dpre, axis=0, keepdims=True), db_ref.shape)

    full = pl.BlockSpec((lp, LANE), lambda: (0, 0))
    return pl.pallas_call(
        body, name=name,
        in_specs=[full, pl.BlockSpec((1, LANE), lambda: (0, 0)), pl.BlockSpec((LANE, lp), lambda: (0, 0)), full],
        out_specs=[full, full, pl.BlockSpec((8, LANE), lambda: (0, 0))],
        out_shape=[_sds((lp, LANE), F32), _sds((lp, LANE), BF16), _sds((8, LANE), F32)],
        compiler_params=pltpu.CompilerParams(vmem_limit_bytes=VMEM_LIMIT))(gates_pre, bias, dgrow, dfx)


CB = 256


def _e_specs(lp):
    return [pl.BlockSpec((None, lp, CB), functools.partial(lambda c, j: (c, 0, j), c)) for c in range(4)]


def _mix_fwd(name, ht, e, mnw, cw):
    lp = ht.shape[0]

    def body(ht_ref, og_ref, u_ref, gb_ref, gc_ref, mnw_ref, cw_ref, o_ref):
        x = ht_ref[...]
        r = lax.rsqrt(jnp.mean(x * x, axis=1, keepdims=True) + EPS)
        o_ref[0] = (jax.nn.sigmoid(og_ref[...].astype(F32)) * (x * r * mnw_ref[...])).astype(BF16)
        row = lax.broadcasted_iota(jnp.int32, (lp, CB), 0)
        a = gc_ref[...].astype(F32) * u_ref[...].astype(F32)
        conv = cw_ref[2:3, :] * a + cw_ref[1:2, :] * _shift_rows(a, 1, row) + cw_ref[0:1, :] * _shift_rows(a, 2, row)
        o_ref[1] = (gb_ref[...].astype(F32) * conv).astype(BF16)

    col = pl.BlockSpec((lp, CB), lambda j: (0, j))
    return pl.pallas_call(
        body, name=name, grid=(4,),
        in_specs=[col] + _e_specs(lp) + [pl.BlockSpec((1, CB), lambda j: (0, j)), pl.BlockSpec((8, CB), lambda j: (0, j))],
        out_specs=pl.BlockSpec((2, lp, CB), lambda j: (0, 0, j)), out_shape=_sds((2, lp, MW), BF16),
        compiler_params=_cp(("parallel",)))(ht, e, e, e, e, mnw, cw)


def _mix_bwd(name, dmix, ht, e, mnw, cw):
    lp = ht.shape[0]

    def body(dhm_ref, dhc_ref, ht_ref, og_ref, u_ref, gb_ref, gc_ref, mnw_ref, cw_ref,
             dht_ref, de_ref, dmnw_ref, dcw_ref):
        x = ht_ref[...]
        r = lax.rsqrt(jnp.mean(x * x, axis=1, keepdims=True) + EPS)
        xh = x * r
        w = mnw_ref[...]
        sg = jax.nn.sigmoid(og_ref[...].astype(F32))
        dhm = dhm_ref[...]
        de_ref[0] = (dhm * (xh * w) * (sg * (1.0 - sg))).astype(BF16)
        dn = dhm * sg
        dmnw_ref[...] = jnp.broadcast_to(jnp.sum(dn * xh, axis=0, keepdims=True), dmnw_ref.shape)
        dxh = dn * w
        dht_ref[...] = r * (dxh - xh * jnp.mean(dxh * xh, axis=1, keepdims=True))

        row = lax.broadcasted_iota(jnp.int32, (lp, CB), 0)
        uv = u_ref[...].astype(F32)
        gcv = gc_ref[...].astype(F32)
        gbv = gb_ref[...].astype(F32)
        a = gcv * uv
        a1 = _shift_rows(a, 1, row)
        a2 = _shift_rows(a, 2, row)
        dhc = dhc_ref[...]
        conv = cw_ref[2:3, :] * a + cw_ref[1:2, :] * a1 + cw_ref[0:1, :] * a2
        de_ref[2] = (dhc * conv).astype(BF16)
        dconv = dhc * gbv
        dcw_ref[...] = jnp.zeros_like(dcw_ref)
        dcw_ref[0:1, :] = jnp.sum(dconv * a2, axis=0, keepdims=True)
        dcw_ref[1:2, :] = jnp.sum(dconv * a1, axis=0, keepdims=True)
        dcw_ref[2:3, :] = jnp.sum(dconv * a, axis=0, keepdims=True)
        up1 = jnp.where(row < lp - 1, pltpu.roll(dconv, lp - 1, axis=0), 0.0)
        up2 = jnp.where(row < lp - 2, pltpu.roll(dconv, lp - 2, axis=0), 0.0)
        da = cw_ref[2:3, :] * dconv + cw_ref[1:2, :] * up1 + cw_ref[0:1, :] * up2
        de_ref[1] = (da * gcv).astype(BF16)
        de_ref[3] = (da * uv).astype(BF16)

    col = pl.BlockSpec((lp, CB), lambda j: (0, j))
    small = pl.BlockSpec((8, CB), lambda j: (0, j))
    return pl.pallas_call(
        body, name=name, grid=(4,),
        in_specs=[col, pl.BlockSpec((lp, CB), lambda j: (0, 4 + j)), col] + _e_specs(lp)
                 + [pl.BlockSpec((1, CB), lambda j: (0, j)), small],
        out_specs=[col, pl.BlockSpec((4, lp, CB), lambda j: (0, 0, j)), small, small],
        out_shape=[_sds((lp, MW), F32), _sds((4, lp, MW), BF16), _sds((8, MW), F32), _sds((8, CW), F32)],
        compiler_params=_cp(("parallel",)))(dmix, dmix, ht, e, e, e, e, mnw, cw)


def _row_tile(r, c, itemsize, budget=1536 * 1024, mult=16):
    best = None
    for t in range(mult, r + 1, mult):
        if r % t == 0 and t * c * itemsize <= budget:
            best = t
    if best is None:
        best = r
    return best


def _grid_spec(grid, in_specs, out_specs, scratch=()):
    return pltpu.PrefetchScalarGridSpec(num_scalar_prefetch=1, grid=grid, in_specs=in_specs,
                                        out_specs=out_specs, scratch_shapes=list(scratch))


def _cast_into(name, w, layer, pf):
    _, r, c = w.shape
    tr = _row_tile(r, c, 4)

    def body(pf_ref, x_ref, o_ref):
        o_ref[...] = x_ref[...].astype(BF16)

    return pl.pallas_call(
        body, name=name, out_shape=_sds((NCHIP, r, c), BF16),
        grid_spec=_grid_spec((r // tr,), [pl.BlockSpec((None, tr, c), lambda i, pf: (layer, i, 0))],
                             pl.BlockSpec((None, tr, c), lambda i, pf: (pf[1], i, 0))),
        compiler_params=_cp(("parallel",)))(pf, w)


TCOL = 256


def _cast_into_t(name, w_t, layer, pf):
    c, nl, r = w_t.shape

    def body(pf_ref, x_ref, o_ref):
        o_ref[...] = x_ref[:, layer, :].astype(BF16)

    return pl.pallas_call(
        body, name=name, out_shape=_sds((NCHIP, c, r), BF16),
        grid_spec=_grid_spec((r // TCOL,), [pl.BlockSpec((c, nl, TCOL), lambda i, pf: (0, 0, i))],
                             pl.BlockSpec((None, c, TCOL), lambda i, pf: (pf[1], 0, i))),
        compiler_params=_cp(("parallel",)))(pf, w_t)


def _add2_bf16(name, dw, got, pf, by_cols=False):
    n4, r2, c2 = got.shape

    def body(pf_ref, a_ref, b_ref, o_ref):
        o_ref[...] = (a_ref[...].astype(F32) + b_ref[...].astype(F32)).astype(BF16)

    if by_cols:
        nch = c2 // TCOL
        spec = pl.BlockSpec((None, r2, TCOL), lambda s, i, pf: (s, 0, i))
        mine = pl.BlockSpec((None, r2, TCOL), lambda s, i, pf: (s, 0, pf[0] * nch + i))
    else:
        tr = _row_tile(r2, c2, 4)
        nch = r2 // tr
        spec = pl.BlockSpec((None, tr, c2), lambda s, i, pf: (s, i, 0))
        mine = pl.BlockSpec((None, tr, c2), lambda s, i, pf: (s, pf[0] * nch + i, 0))
    return pl.pallas_call(
        body, name=name, out_shape=_sds((n4, r2, c2), BF16),
        grid_spec=_grid_spec((n4, nch), [mine, spec], spec),
        compiler_params=_cp(("parallel", "parallel")))(pf, dw, got)


def _adam_math(w, g, m, v):
    m2 = ADAM_B1 * m + (1.0 - ADAM_B1) * g
    v2 = ADAM_B2 * v + (1.0 - ADAM_B2) * (g * g)
    m_hat = m2 / (1.0 - ADAM_B1 ** ADAM_STEP)
    v_hat = v2 / (1.0 - ADAM_B2 ** ADAM_STEP)
    delta = -ADAM_LR * (m_hat / (jnp.sqrt(v_hat) + ADAM_EPS) + ADAM_WD * w)
    return delta, m2, v2


def _adamw_layer(name, layer, g_mine, g_theirs, w, m, v, prev, pf):
    _, r, c = w.shape
    r2 = r // 2
    tr = _row_tile(r2, c, 4, budget=1024 * 1024, mult=8)
    nch = r2 // tr
    n_alias = 0 if prev is None else 4

    def body(*refs):
        pf_ref, gm_ref, gt_ref, w_ref, m_ref, v_ref = refs[:6]
        go_ref, d_ref, mo_ref, vo_ref = refs[6 + n_alias:]
        mine = (pl.program_id(0) // nch) == pf_ref[0]
        gv = jnp.where(mine, gm_ref[...], gt_ref[...])
        delta, m2, v2 = _adam_math(w_ref[...], gv, m_ref[...], v_ref[...])
        go_ref[...] = gv
        d_ref[...] = delta
        mo_ref[...] = m2
        vo_ref[...] = v2

    slab = pl.BlockSpec((None, tr, c), lambda i, pf: (layer, i, 0))
    ins = [g_mine, g_theirs, w, m, v] + (list(prev) if prev is not None else [])
    in_specs = [pl.BlockSpec((tr, c), lambda i, pf: (jnp.clip(i - pf[0] * nch, 0, nch - 1), 0)),
                pl.BlockSpec((tr, c), lambda i, pf: (jnp.clip(i - (1 - pf[0]) * nch, 0, nch - 1), 0)),
                slab, slab, slab] + [ANY] * n_alias
    o = _sds(w.shape, F32)
    return pl.pallas_call(
        body, name=name, out_shape=[o] * 4, grid_spec=_grid_spec((2 * nch,), in_specs, [slab] * 4),
        input_output_aliases={6 + k: k for k in range(n_alias)},
        compiler_params=_cp(("parallel",)))(pf, *ins)


def _adamw_t(name, gs, w_t, m_t, v_t, pf):
    c, nl, r = w_t.shape
    ta = LANE
    nch = (r // 2) // ta

    def body(*refs):
        pf_ref = refs[0]
        g_refs = refs[1:1 + 2 * nl]
        w_ref, m_ref, v_ref, go_ref, d_ref, mo_ref, vo_ref = refs[1 + 2 * nl:]
        mine = (pl.program_id(0) // nch) == pf_ref[0]
        for l in range(nl):
            gv = jnp.where(mine, g_refs[2 * l][...], g_refs[2 * l + 1][...])
            delta, m2, v2 = _adam_math(w_ref[:, l, :], gv, m_ref[:, l, :], v_ref[:, l, :])
            go_ref[:, l, :] = gv
            d_ref[:, l, :] = delta
            mo_ref[:, l, :] = m2
            vo_ref[:, l, :] = v2

    both = pl.BlockSpec((c, nl, ta), lambda i, pf: (0, 0, i))
    g_specs = []
    for l in range(nl):
        g_specs += [pl.BlockSpec((c, ta), lambda i, pf: (0, jnp.clip(i - pf[0] * nch, 0, nch - 1))),
                    pl.BlockSpec((c, ta), lambda i, pf: (0, jnp.clip(i - (1 - pf[0]) * nch, 0, nch - 1)))]
    o = _sds(w_t.shape, F32)
    flat_g = [a for pair in gs for a in pair]
    return pl.pallas_call(
        body, name=name, out_shape=[o] * 4, grid_spec=_grid_spec((2 * nch,), g_specs + [both] * 3, [both] * 4),
        compiler_params=_cp(("parallel",)))(pf, *flat_g, w_t, m_t, v_t)


def _adamw_flat(g, w, m, v):
    def body(g_ref, w_ref, m_ref, v_ref, d_ref, mo_ref, vo_ref):
        delta, m2, v2 = _adam_math(w_ref[...], g_ref[...], m_ref[...], v_ref[...])
        d_ref[...] = delta
        mo_ref[...] = m2
        vo_ref[...] = v2

    o = _sds(w.shape, F32)
    return pl.pallas_call(body, name="adamw_small", out_shape=[o, o, o])(g, w, m, v)


def _place():
    x, y, c = lax.axis_index("x"), lax.axis_index("y"), lax.axis_index("c")
    chips = [(1 - x, y), (x, 1 - y), (1 - x, 1 - y)]
    return x, y, c, chips


def _rs_chips(name, ps):
    n = len(ps)

    def body(*refs):
        ins = refs[:n]
        got = refs[n:2 * n]
        send, recv = refs[2 * n:]
        x, y, c, chips = _place()
        cps = []
        for t in range(n):
            for k, chip in enumerate(chips):
                jk = 2 * chip[0] + chip[1]
                cp = pltpu.make_async_remote_copy(
                    src_ref=ins[t].at[jk], dst_ref=got[t].at[k],
                    send_sem=send.at[3 * t + k], recv_sem=recv.at[3 * t + k],
                    device_id=(*chip, c), device_id_type=MESH)
                cp.start()
                cps.append(cp)
        for cp in cps:
            cp.wait()

    dma = pltpu.SemaphoreType.DMA
    return pl.pallas_call(
        body, name=name, in_specs=[ANY] * n, out_specs=[ANY] * n,
        out_shape=[_sds((3,) + p.shape[1:], BF16) for p in ps],
        scratch_shapes=[dma((3 * n,)), dma((3 * n,))],
        compiler_params=pltpu.CompilerParams(has_side_effects=True))(*ps)


HBM = pl.BlockSpec(memory_space=pltpu.HBM)
SEM = pl.BlockSpec(memory_space=pltpu.SEMAPHORE)
EFFECT = pltpu.SideEffectType.DATAFLOW_SIDE_EFFECTING


def _in_hbm(a):
    return pltpu.with_memory_space_constraint(a, pltpu.HBM)


def _rs_chips_copies(ins, lands, send, recv):
    x, y, c, chips = _place()
    cps = []
    for t in range(len(ins)):
        for k, chip in enumerate(chips):
            jk = 2 * chip[0] + chip[1]
            cps.append(pltpu.make_async_remote_copy(
                src_ref=ins[t].at[jk], dst_ref=lands[t].at[k], send_sem=send.at[3 * t + k],
                recv_sem=recv.at[3 * t + k], device_id=(*chip, c), device_id_type=MESH))
    return cps


def _rs_chips_start(name, ps):
    n = len(ps)

    def body(*refs):
        ins, lands = refs[:n], refs[n:2 * n]
        send, recv = refs[2 * n], refs[2 * n + 1]
        token = refs[-1]
        for cp in _rs_chips_copies(ins, lands, send, recv):
            cp.start()
        token[...] = jnp.zeros_like(token)

    dma = pltpu.SemaphoreType.DMA
    lands = [lax.empty((3,) + p.shape[1:], BF16) for p in ps]
    out_shape = ([dma((3 * n,)), dma((3 * n,))] + [pltpu.HBM(p.shape, BF16) for p in ps]
                 + [pltpu.HBM(z.shape, BF16) for z in lands] + [_sds((8, LANE), F32)])
    outs = pl.pallas_call(
        body, name=name, out_shape=out_shape, in_specs=[HBM] * (2 * n),
        out_specs=[SEM, SEM] + [HBM] * (2 * n) + [pl.BlockSpec(memory_space=pltpu.VMEM)],
        input_output_aliases={i: 2 + i for i in range(2 * n)},
        compiler_params=pltpu.CompilerParams(has_side_effects=EFFECT))(
            *[_in_hbm(p) for p in ps], *[_in_hbm(z) for z in lands])
    return outs[0], outs[1], outs[2:2 + n], outs[2 + n:2 + 2 * n], outs[-1]


def _rs_chips_wait(name, send, recv, ps, lands, afters):
    n = len(ps)

    def body(*refs):
        ins, zones = refs[:n], refs[n:2 * n]
        send_ref, recv_ref = refs[2 * n], refs[2 * n + 1]
        for cp in _rs_chips_copies(ins, zones, send_ref, recv_ref):
            cp.wait_send()
            cp.wait_recv()

    outs = pl.pallas_call(
        body, name=name, out_shape=[pltpu.HBM(p.shape, BF16) for p in ps] + [pltpu.HBM(z.shape, BF16) for z in lands],
        in_specs=[HBM] * (2 * n) + [SEM, SEM] + [ANY] * len(afters), out_specs=[HBM] * (2 * n),
        input_output_aliases={i: i for i in range(2 * n)},
        compiler_params=pltpu.CompilerParams(has_side_effects=EFFECT))(*ps, *lands, send, recv, *afters)
    return outs[:n], outs[n:]


def _sibling():
    x, y, c, _ = _place()
    return (x, y, 1 - c)


def _pair_send(name, dw, pf, by_cols=False):
    n4, r, c = dw.shape
    blk = (1, r, c // 2) if by_cols else (1, r // 2, c)
    idx = (lambda s, pf: (s, 0, 1 - pf[0])) if by_cols else (lambda s, pf: (s, 1 - pf[0], 0))

    def body(pf_ref, x_ref, got_ref, ssem, rsem):
        s = pl.program_id(0)
        cp = pltpu.make_async_remote_copy(src_ref=x_ref, dst_ref=got_ref.at[pl.ds(s, 1)], send_sem=ssem,
                                          recv_sem=rsem, device_id=_sibling(), device_id_type=MESH)
        cp.start()
        cp.wait_send()

        @pl.when(s == n4 - 1)
        def _():
            pltpu.make_async_remote_copy(src_ref=got_ref, dst_ref=got_ref, send_sem=ssem, recv_sem=rsem,
                                         device_id=_sibling(), device_id_type=MESH).wait_recv()

    dma = pltpu.SemaphoreType.DMA
    return pl.pallas_call(
        body, name=name, out_shape=_sds((n4,) + blk[1:], BF16),
        grid_spec=_grid_spec((n4,), [pl.BlockSpec(blk, idx)], ANY, scratch=[dma(()), dma(())]),
        compiler_params=pltpu.CompilerParams(dimension_semantics=("arbitrary",), has_side_effects=True,
                                             vmem_limit_bytes=VMEM_LIMIT))(pf, dw)


def _add4_join(name, p, got, pf, by_cols=False):
    n4, r2, c = p.shape
    if by_cols:
        tr, nch = r2, c // TCOL
        blk, idx = (r2, TCOL), (lambda i: (0, i))
    else:
        tr = _row_tile(r2, c, 4)
        nch = r2 // tr
        blk, idx = (tr, c), (lambda i: (i, 0))

    def body(pf_ref, p_ref, g_ref, mine_ref, theirs_ref, ssem, rsem):
        i = pl.program_id(0)
        s = p_ref[...].astype(F32)
        for k in range(3):
            s = s + g_ref[k].astype(F32)
        mine_ref[...] = s
        if by_cols:
            dst = theirs_ref.at[:, pl.ds(pl.multiple_of(i * TCOL, LANE), TCOL)]
        else:
            dst = theirs_ref.at[pl.ds(pl.multiple_of(i * tr, 8), tr), :]
        cp = pltpu.make_async_remote_copy(src_ref=mine_ref, dst_ref=dst,
                                          send_sem=ssem, recv_sem=rsem, device_id=_sibling(), device_id_type=MESH)
        cp.start()
        cp.wait_send()

        @pl.when(i == nch - 1)
        def _():
            pltpu.make_async_remote_copy(src_ref=theirs_ref, dst_ref=theirs_ref, send_sem=ssem, recv_sem=rsem,
                                         device_id=_sibling(), device_id_type=MESH).wait_recv()

    dma = pltpu.SemaphoreType.DMA
    o = _sds((r2, c), F32)
    return pl.pallas_call(
        body, name=name, out_shape=[o, o],
        grid_spec=_grid_spec((nch,), [pl.BlockSpec((None,) + blk, lambda i, pf: (pf[1],) + idx(i)),
                                      pl.BlockSpec((3,) + blk, lambda i, pf: (0,) + idx(i))],
                             [pl.BlockSpec(blk, lambda i, pf: idx(i)), ANY], scratch=[dma(()), dma(())]),
        compiler_params=pltpu.CompilerParams(dimension_semantics=("arbitrary",), has_side_effects=True,
                                             vmem_limit_bytes=VMEM_LIMIT))(pf, p, got)


def _half_of(g, slot, which, axis):
    half = g.shape[axis] // 2
    if axis == 1:
        return g.at[slot, pl.ds(which * half, half), :]
    return g.at[slot, :, pl.ds(which * half, half)]


def _ag_copies(arrs, split, send, recv):
    x, y, c, chips = _place()
    j = 2 * x + y
    cps = []
    for t, g in enumerate(arrs):
        piece = _half_of(g, j, c, split[t]) if split[t] else g.at[j]
        for k, chip in enumerate(chips):
            cps.append(pltpu.make_async_remote_copy(
                src_ref=piece, dst_ref=piece, send_sem=send.at[3 * t + k], recv_sem=recv.at[3 * t + k],
                device_id=(*chip, c), device_id_type=MESH))
    return cps


def _ag_start(groups, splits):
    sizes = [len(g) for g in groups]
    flat = [a for g in groups for a in g]
    n = len(flat)

    def body(*refs):
        ins = refs[:n]
        sems = refs[n:n + 2 * len(groups)]
        o = 0
        for gi, sz in enumerate(sizes):
            for cp in _ag_copies(ins[o:o + sz], splits[gi], sems[2 * gi], sems[2 * gi + 1]):
                cp.start()
            o += sz

    dma = pltpu.SemaphoreType.DMA
    sem_shapes = [dma((3 * sz,)) for sz in sizes for _ in range(2)]
    outs = pl.pallas_call(
        body, name="ag_start", out_shape=sem_shapes + [pltpu.HBM(a.shape, a.dtype) for a in flat],
        in_specs=[HBM] * n, out_specs=[SEM] * len(sem_shapes) + [HBM] * n,
        input_output_aliases={i: len(sem_shapes) + i for i in range(n)},
        compiler_params=pltpu.CompilerParams(has_side_effects=EFFECT))(*[_in_hbm(a) for a in flat])
    sems, arrs, o = [], [], len(sem_shapes)
    for gi, sz in enumerate(sizes):
        sems.append((outs[2 * gi], outs[2 * gi + 1]))
        arrs.append(list(outs[o:o + sz]))
        o += sz
    return sems, arrs


def _ag_wait(name, arrs, split, send, recv, afters):
    n = len(arrs)

    def body(*refs):
        for cp in _ag_copies(refs[:n], split, refs[n], refs[n + 1]):
            cp.wait_send()
            cp.wait_recv()

    return pl.pallas_call(
        body, name=name, out_shape=[pltpu.HBM(a.shape, a.dtype) for a in arrs],
        in_specs=[HBM] * n + [SEM, SEM] + [ANY] * len(afters), out_specs=[HBM] * n,
        input_output_aliases={i: i for i in range(n)},
        compiler_params=pltpu.CompilerParams(has_side_effects=EFFECT))(*arrs, send, recv, *afters)


def _ag_forward(name, arrs, axes):
    n = len(arrs)

    def half_shape(t):
        _, r, cc = arrs[t].shape
        return (r // 2, cc) if axes[t] == 1 else (r, cc // 2)

    def body(*refs):
        g = refs[n:2 * n]
        bufs = refs[2 * n:3 * n]
        fsend, frecv, lsem = refs[3 * n:]
        x, y, c, chips = _place()
        for t in range(n):
            pend = [None, None]
            for k, chip in enumerate(chips):
                jk = 2 * chip[0] + chip[1]
                slot = k % 2
                if pend[slot] is not None:
                    pend[slot].wait_send()
                part = _half_of(g[t], jk, c, axes[t])
                ld = pltpu.make_async_copy(part, bufs[t].at[slot], lsem.at[2 * t + slot])
                ld.start()
                ld.wait()
                cp = pltpu.make_async_remote_copy(
                    src_ref=bufs[t].at[slot], dst_ref=part, send_sem=fsend.at[2 * t + slot],
                    recv_sem=frecv.at[t], device_id=(x, y, 1 - c), device_id_type=MESH)
                cp.start()
                pend[slot] = cp
            for cp in pend:
                cp.wait_send()
        for t in range(n):
            hr, hc = half_shape(t)
            passed = g[t].at[pl.ds(0, 3), pl.ds(0, hr), pl.ds(0, hc)]
            pltpu.make_async_remote_copy(
                src_ref=passed, dst_ref=passed, send_sem=fsend.at[2 * t], recv_sem=frecv.at[t],
                device_id=(x, y, 1 - c), device_id_type=MESH).wait_recv()

    dma = pltpu.SemaphoreType.DMA
    scratch = [pltpu.VMEM((2,) + half_shape(t), BF16) for t in range(n)]
    scratch += [dma((2 * n,)), dma((n,)), dma((2 * n,))]
    return pl.pallas_call(
        body, name=name, in_specs=[ANY] * n, out_specs=[ANY] * n, out_shape=[_sds(a.shape, a.dtype) for a in arrs],
        scratch_shapes=scratch, input_output_aliases={t: t for t in range(n)},
        compiler_params=pltpu.CompilerParams(has_side_effects=True, vmem_limit_bytes=VMEM_LIMIT))(*arrs)


def _allgather_blocking(gs, smalls):
    nb, ns = len(gs), len(smalls)
    halves = [g.shape[2] // 2 for g in gs]

    def body(*refs):
        s_in = refs[nb:nb + ns]
        g = refs[nb + ns:2 * nb + ns]
        s_out = refs[2 * nb + ns:2 * (nb + ns)]
        scr = refs[2 * (nb + ns):]
        bufs = scr[:nb]
        send, recv, fsend, frecv, lsem, ssend, srecv, slsem = scr[nb:]
        x, y, c, chips = _place()
        j = 2 * x + y
        sends, slocal = [], []
        for t in range(nb):
            rows = pl.ds(c * halves[t], halves[t])
            piece = g[t].at[:, j, rows, :]
            for k, chip in enumerate(chips):
                cp = pltpu.make_async_remote_copy(
                    src_ref=piece, dst_ref=piece, send_sem=send.at[3 * t + k], recv_sem=recv.at[3 * t + k],
                    device_id=(*chip, c), device_id_type=MESH)
                cp.start()
                sends.append(cp)
        for t in range(ns):
            cp = pltpu.make_async_copy(s_in[t], s_out[t].at[j], slsem.at[t])
            cp.start()
            slocal.append(cp)
            for k, chip in enumerate(chips):
                cp = pltpu.make_async_remote_copy(
                    src_ref=s_in[t], dst_ref=s_out[t].at[j], send_sem=ssend.at[3 * t + k],
                    recv_sem=srecv.at[3 * t + k], device_id=(*chip, c), device_id_type=MESH)
                cp.start()
                sends.append(cp)
        for t in range(nb):
            rows = pl.ds(c * halves[t], halves[t])
            pend = [None, None]
            n = 0
            for k, chip in enumerate(chips):
                jk = 2 * chip[0] + chip[1]
                landed = g[t].at[:, jk, rows, :]
                pltpu.make_async_remote_copy(
                    src_ref=landed, dst_ref=landed, send_sem=send.at[3 * t + k], recv_sem=recv.at[3 * t + k],
                    device_id=(*chip, c), device_id_type=MESH).wait_recv()
                for l in range(DEPTH):
                    slot = n % 2
                    if pend[slot] is not None:
                        pend[slot].wait_send()
                    part = g[t].at[l, jk, rows, :]
                    ld = pltpu.make_async_copy(part, bufs[t].at[slot], lsem.at[2 * t + slot])
                    ld.start()
                    ld.wait()
                    cp = pltpu.make_async_remote_copy(
                        src_ref=bufs[t].at[slot], dst_ref=part, send_sem=fsend.at[2 * t + slot],
                        recv_sem=frecv.at[t], device_id=(x, y, 1 - c), device_id_type=MESH)
                    cp.start()
                    pend[slot] = cp
                    n += 1
            for cp in pend:
                cp.wait_send()
        for t in range(ns):
            for k, chip in enumerate(chips):
                jk = 2 * chip[0] + chip[1]
                landed = s_out[t].at[jk]
                pltpu.make_async_remote_copy(
                    src_ref=landed, dst_ref=landed, send_sem=ssend.at[3 * t + k], recv_sem=srecv.at[3 * t + k],
                    device_id=(*chip, c), device_id_type=MESH).wait_recv()
        for t in range(nb):
            passed = g[t].at[:, pl.ds(0, 3), pl.ds((1 - c) * halves[t], halves[t]), :]
            pltpu.make_async_remote_copy(
                src_ref=passed, dst_ref=passed, send_sem=fsend.at[2 * t], recv_sem=frecv.at[t],
                device_id=(x, y, 1 - c), device_id_type=MESH).wait_recv()
        for cp in sends:
            cp.wait_send()
        for cp in slocal:
            cp.wait()

    dma = pltpu.SemaphoreType.DMA
    out_shape = [_sds(g.shape, g.dtype) for g in gs] + [_sds((NCHIP,) + s.shape, s.dtype) for s in smalls]
    scratch = [pltpu.VMEM((2, halves[t], gs[t].shape[3]), BF16) for t in range(nb)]
    scratch += [dma((3 * nb,)), dma((3 * nb,)), dma((2 * nb,)), dma((nb,)), dma((2 * nb,)),
                dma((3 * ns,)), dma((3 * ns,)), dma((ns,))]
    return pl.pallas_call(
        body, name="allgather_weights", in_specs=[ANY] * (nb + ns), out_specs=[ANY] * (nb + ns),
        out_shape=out_shape, scratch_shapes=scratch, input_output_aliases={t: t for t in range(nb)},
        compiler_params=pltpu.CompilerParams(has_side_effects=True, vmem_limit_bytes=VMEM_LIMIT))(*gs, *smalls)


def _reduce_scatter(tag, dws, pf):
    ps = []
    for t, dw in enumerate(dws):
        got = _pair_send("rs_pair_%s_%d" % (tag, t), dw, pf)
        ps.append(_add2_bf16("rs_add2_%s_%d" % (tag, t), dw, got, pf))
    got2 = _rs_chips("rs_chips_" + tag, ps)
    return [_add4_join("rs_add4_%s_%d" % (tag, t), p, g2, pf) for t, (p, g2) in enumerate(zip(ps, got2))]


def _rs_begin(tag, dws, by_cols, pf):
    ps = []
    for t, dw in enumerate(dws):
        got = _pair_send("rs_pair_%s_%d" % (tag, t), dw, pf, by_cols[t])
        ps.append(_add2_bf16("rs_add2_%s_%d" % (tag, t), dw, got, pf, by_cols[t]))
    send, recv, ps_thru, lands, token = _rs_chips_start("rs_chips_start_" + tag, ps)
    return (tag, send, recv, ps_thru, lands, by_cols), token


def _rs_end(handle, afters, pf):
    tag, send, recv, ps, lands, by_cols = handle
    ps, got2 = _rs_chips_wait("rs_chips_wait_" + tag, send, recv, ps, lands, afters)
    return [_add4_join("rs_add4_%s_%d" % (tag, t), p, g2, pf, by_cols[t]) for t, (p, g2) in enumerate(zip(ps, got2))]


def _allreduce_small(pack):
    r = pack.shape[0]
    flips = [(fx, fy, fc) for fx in (0, 1) for fy in (0, 1) for fc in (0, 1)][1:]

    def body(p_ref, o_ref, gat, send, recv):
        x, y, c, _ = _place()
        me = 4 * x + 2 * y + c
        gat[me] = p_ref[...]
        cps = []
        for k, (fx, fy, fc) in enumerate(flips):
            peer = ((1 - x) if fx else x, (1 - y) if fy else y, (1 - c) if fc else c)
            cp = pltpu.make_async_remote_copy(
                src_ref=p_ref, dst_ref=gat.at[me], send_sem=send.at[k], recv_sem=recv.at[k],
                device_id=peer, device_id_type=MESH)
            cp.start()
            cps.append(cp)
        for k, (fx, fy, fc) in enumerate(flips):
            peer = ((1 - x) if fx else x, (1 - y) if fy else y, (1 - c) if fc else c)
            src = 4 * peer[0] + 2 * peer[1] + peer[2]
            pltpu.make_async_remote_copy(
                src_ref=p_ref, dst_ref=gat.at[src], send_sem=send.at[k], recv_sem=recv.at[k],
                device_id=peer, device_id_type=MESH).wait_recv()
        for cp in cps:
            cp.wait_send()
        s = gat[0]
        for d in range(1, 8):
            s = s + gat[d]
        o_ref[...] = s

    dma = pltpu.SemaphoreType.DMA
    vm = pl.BlockSpec(memory_space=pltpu.VMEM)
    return pl.pallas_call(
        body, name="allreduce_small", in_specs=[vm], out_specs=vm, out_shape=_sds((r, LANE), F32),
        scratch_shapes=[pltpu.VMEM((8, r, LANE), F32), dma((7,)), dma((7,))],
        compiler_params=pltpu.CompilerParams(has_side_effects=True))(pack)


def _in_weights(win_g):
    full = jnp.concatenate([win_g[s] for s in range(NCHIP)], axis=0)
    wqkv = full[:2048]
    og = full[2048:3072]
    gates = jnp.pad(full[3072:3080], ((0, LANE - 8), (0, 0)))
    u = full[3080:4104]
    gb = full[4104:5128]
    gc = full[5128:6152]
    return wqkv, jnp.stack([og, u, gb, gc]), gates


def _in_grads(dwqkv, dwe, dwgt):
    full = jnp.concatenate([dwqkv, dwe[0], dwgt[:8], dwe[1], dwe[2], dwe[3]], axis=0)
    sw = DIN // NCHIP
    return jnp.stack([full[s * sw:(s + 1) * sw] for s in range(NCHIP)])


def _layer_fwd(l, h, get_mix, get_ffn, small):
    lp = h.shape[0]
    th = lp // 2
    wqkv, we, wgt, wout_g = get_mix(h)
    nmw, bias, mnw, cw, nfw = small
    tag = "_l%d" % l
    hn = _norm_fwd("norm_mix" + tag, h, nmw)
    qkv = _mm("proj_qkv" + tag, "nt", hn, wqkv,
              pl.BlockSpec((lp, D), lambda i, j, k: (0, 0)), pl.BlockSpec((512, D), lambda i, j, k: (j, 0)),
              pl.BlockSpec((lp, 512), lambda i, j, k: (0, j)), _sds((lp, 2048), BF16), (1, 4, 1))
    e = _mm("proj_e" + tag, "nt", hn, we,
            pl.BlockSpec((lp, D), lambda i, j, k: (0, 0)), pl.BlockSpec((None, 512, D), lambda i, j, k: (j // 2, j % 2, 0)),
            pl.BlockSpec((None, lp, 512), lambda i, j, k: (j // 2, 0, j % 2)), _sds((4, lp, 1024), BF16), (1, 8, 1))
    gpre = _mm("proj_gates" + tag, "nt", hn, wgt,
               pl.BlockSpec((lp, D), lambda i, j, k: (0, 0)), pl.BlockSpec((LANE, D), lambda i, j, k: (0, 0)),
               pl.BlockSpec((lp, LANE), lambda i, j, k: (0, 0)), _sds((lp, LANE), F32), (1, 1, 1))
    grow, mcol, ccol = _gate_prep("gate_prep" + tag, gpre, bias)
    ht = _mlstm_fwd("mlstm_fwd" + tag, qkv, grow, mcol, ccol, 4)
    mix = _mix_fwd("mix_fwd" + tag, ht, e, mnw, cw)
    wout = wout_g.reshape(D, D)
    h1 = _mm("out_proj" + tag, "nn", mix, wout,
             pl.BlockSpec((None, th, 1024), lambda i, j, k: (k, i, 0)),
             pl.BlockSpec((1024, 1024), lambda i, j, k: (k, j)),
             pl.BlockSpec((th, 1024), lambda i, j, k: (i, j)), _sds((lp, D), F32), (2, 2, 2),
             acc_shape=(th, 1024), res=h, res_spec=pl.BlockSpec((th, 1024), lambda i, j, k: (i, j)))
    wg_g, wu_g, wd_g = get_ffn(h1)
    hf = _norm_fwd("norm_ffn" + tag, h1, nfw)
    g, u, a = _ffn_up("ffn_up" + tag, hf, wg_g, wu_g)
    tk = DFF // NCHIP
    h2 = _mm("ffn_down" + tag, "nn", a, wd_g,
             pl.BlockSpec((th, tk), lambda i, j, k: (i, k)),
             pl.BlockSpec((None, tk, 1024), lambda i, j, k: (k, 0, j)),
             pl.BlockSpec((th, 1024), lambda i, j, k: (i, j)), _sds((lp, D), F32), (2, 2, NCHIP),
             acc_shape=(th, 1024), res=h1, res_spec=pl.BlockSpec((th, 1024), lambda i, j, k: (i, j)))
    saved = (h, hn, qkv, e, gpre, grow, mcol, ccol, ht, mix, h1, hf, g, u, a)
    return h2, saved, (wqkv, we, wgt, wout_g, wg_g, wu_g, wd_g)


def _layer_bwd(l, dh2, dh2b, saved, wts, small, ffn_done):
    h, hn, qkv, e, gpre, grow, mcol, ccol, ht, mix, h1, hf, g, u, a = saved
    wqkv, we, wgt, wout_g, wg_g, wu_g, wd_g = wts
    nmw, bias, mnw, cw, nfw = small
    lp = h.shape[0]
    th = lp // 2
    tk = DFF // NCHIP
    tag = "_l%d" % l
    half_rows = lambda i, j, k: (i, j)

    dwd = _mm("dw_down" + tag, "tn", a, dh2b,
              pl.BlockSpec((lp, tk), lambda i, j, k: (0, i)), pl.BlockSpec((lp, 1024), lambda i, j, k: (0, j)),
              pl.BlockSpec((None, tk, 1024), lambda i, j, k: (i, 0, j)), _sds((NCHIP, tk, D), BF16), (NCHIP, 2, 1))
    dg, du = _ffn_bwd_act("ffn_bwd_act" + tag, dh2b, wd_g, g, u)
    dws = []
    for nm, dact in (("gate", dg), ("up", du)):
        dws.append(_mm("dw_%s%s" % (nm, tag), "tn", hf, dact,
                       pl.BlockSpec((lp, 1024), lambda i, j, k: (0, i)), pl.BlockSpec((lp, tk), lambda i, j, k: (0, j)),
                       pl.BlockSpec((None, 1024, tk), lambda i, j, k: (j, i, 0)), _sds((NCHIP, D, tk), BF16),
                       (2, NCHIP, 1)))
    dwg, dwu = dws
    dhf = None
    for nm, dact, wfull in (("gate", dg, wg_g), ("up", du, wu_g)):
        dhf = _mm("dhf_%s%s" % (nm, tag), "nt", dact, wfull,
                  pl.BlockSpec((th, tk), lambda i, j, k: (i, k)),
                  pl.BlockSpec((None, 1024, tk), lambda i, j, k: (k, j, 0)),
                  pl.BlockSpec((th, 1024), half_rows), _sds((lp, D), F32), (2, 2, NCHIP), acc_shape=(th, 1024),
                  res=dhf, res_spec=None if dhf is None else pl.BlockSpec((th, 1024), half_rows))
    dh1, dh1b, dnfw = _norm_bwd("norm_ffn_bwd" + tag, dhf, h1, nfw, dh2)
    dh1b = ffn_done([dwg, dwu, dwd], dh1b)

    dwout = _mm("dw_out" + tag, "tn", mix, dh1b,
                pl.BlockSpec((None, lp, 1024), lambda i, j, k: (i, 0, 0)), pl.BlockSpec((lp, 1024), lambda i, j, k: (0, j)),
                pl.BlockSpec((1024, 1024), half_rows), _sds((D, D), BF16), (2, 2, 1))
    wout = wout_g.reshape(D, D)
    dmix = _mm("dmix" + tag, "nt", dh1b, wout,
               pl.BlockSpec((th, D), lambda i, j, k: (i, 0)), pl.BlockSpec((1024, D), lambda i, j, k: (j, 0)),
               pl.BlockSpec((th, 1024), half_rows), _sds((lp, D), F32), (2, 2, 1))
    dht, de, dmnw, dcw = _mix_bwd("mix_bwd" + tag, dmix, ht, e, mnw, cw)
    dq, dk, dv, dgrow, dfx = _mlstm_bwd("mlstm_bwd" + tag, qkv, grow, mcol, ccol, ht, dht, 8)
    dgp, dgpb, dbias = _gate_bwd("gate_bwd" + tag, gpre, bias, dgrow, dfx)
    del dgp
    dqkv = jnp.concatenate([dq, dk, dv], axis=1)

    hn_cols = pl.BlockSpec((lp, 1024), lambda i, j, k: (0, j))
    dwqkv = _mm("dw_qkv" + tag, "tn", dqkv, hn, pl.BlockSpec((lp, 1024), lambda i, j, k: (0, i)), hn_cols,
                pl.BlockSpec((1024, 1024), half_rows), _sds((2048, D), BF16), (2, 2, 1))
    dwe = _mm("dw_e" + tag, "tn", de, hn, pl.BlockSpec((None, lp, 1024), lambda i, j, k: (i, 0, 0)), hn_cols,
              pl.BlockSpec((None, 1024, 1024), lambda i, j, k: (i, 0, j)), _sds((4, 1024, D), BF16), (4, 2, 1))
    dwgt = _mm("dw_gates" + tag, "tn", dgpb, hn, pl.BlockSpec((lp, LANE), lambda i, j, k: (0, 0)), hn_cols,
               pl.BlockSpec((LANE, 1024), lambda i, j, k: (0, j)), _sds((LANE, D), BF16), (1, 2, 1))
    dwin = _in_grads(dwqkv, dwe, dwgt)

    dhn = _mm("dhn_qkv" + tag, "nn", dqkv, wqkv,
              pl.BlockSpec((th, 2048), lambda i, j, k: (i, 0)), pl.BlockSpec((2048, 1024), lambda i, j, k: (0, j)),
              pl.BlockSpec((th, 1024), half_rows), _sds((lp, D), F32), (2, 2, 1))
    dhn = _mm("dhn_e" + tag, "nn", de, we,
              pl.BlockSpec((None, th, 1024), lambda i, j, k: (k, i, 0)),
              pl.BlockSpec((None, 1024, 1024), lambda i, j, k: (k, 0, j)),
              pl.BlockSpec((th, 1024), half_rows), _sds((lp, D), F32), (2, 2, 4), acc_shape=(th, 1024),
              res=dhn, res_spec=pl.BlockSpec((th, 1024), half_rows))
    dhn = _mm("dhn_gates" + tag, "nn", dgpb, wgt,
              pl.BlockSpec((th, LANE), lambda i, j, k: (i, 0)), pl.BlockSpec((LANE, 1024), lambda i, j, k: (0, j)),
              pl.BlockSpec((th, 1024), half_rows), _sds((lp, D), F32), (2, 2, 1),
              res=dhn, res_spec=pl.BlockSpec((th, 1024), half_rows))
    dh0, dh0b, dnmw = _norm_bwd("norm_mix_bwd" + tag, dhn, h, nmw, dh1)

    pieces = [dwin, dwout.reshape(NCHIP, D // NCHIP, D)]
    smalls = (dnmw[0], dbias[0, :8], dcw[:3], dmnw[0], dnfw[0])
    return dh0, dh0b, pieces, smalls


def _pack_rows(parts):
    rows = []
    for p in parts:
        f = p.reshape(-1)
        pad = (-f.shape[0]) % LANE
        if pad:
            f = jnp.pad(f, (0, pad))
        rows.append(f.reshape(-1, LANE))
    r = jnp.concatenate(rows, axis=0)
    pad = (-r.shape[0]) % 8
    if pad:
        r = jnp.pad(r, ((0, pad), (0, 0)))
    return r


def _unpack_rows(pack, shapes):
    out, r0 = [], 0
    for s in shapes:
        n = 1
        for d in s:
            n *= d
        nr = -(-n // LANE)
        out.append(pack[r0:r0 + nr].reshape(-1)[:n].reshape(s))
        r0 += nr
    return out


def kernel(x, meta_tokens, norm_mix_w, w_in, b_gates, conv_w, mlstm_norm_w, w_out, norm_ffn_w, w_gate, w_up, w_down, norm_final_w, loss_target, m_meta_tokens, m_norm_mix_w, m_w_in, m_b_gates, m_conv_w, m_mlstm_norm_w, m_w_out, m_norm_ffn_w, m_w_gate, m_w_up, m_w_down, m_norm_final_w, v_meta_tokens, v_norm_mix_w, v_w_in, v_b_gates, v_conv_w, v_mlstm_norm_w, v_w_out, v_norm_ffn_w, v_w_gate, v_w_up, v_w_down, v_norm_final_w):
    seq = x.shape[1]
    n_real = N_META + seq
    lp = -(-n_real // LANE) * LANE
    xi, yi, ci = lax.axis_index("x"), lax.axis_index("y"), lax.axis_index("c")
    jchip = 2 * xi + yi
    pf = jnp.stack([ci, jchip, 2 * (1 - xi) + yi, 2 * xi + (1 - yi), 2 * (1 - xi) + (1 - yi)]).astype(jnp.int32)

    big = {"w_in": w_in, "w_out": w_out, "w_gate": w_gate, "w_up": w_up, "w_down": w_down}
    cast = {n: [_cast_into("cast_%s_l%d" % (n, l), w, l, pf) for l in range(DEPTH)] for n, w in big.items()
            if n != "w_in"}
    in_t = lambda a: jnp.transpose(a, (2, 0, 1))
    cast["w_in"] = [_cast_into_t("cast_w_in_l%d" % l, in_t(w_in), l, pf) for l in range(DEPTH)]
    conv_flat = jnp.pad(conv_w.reshape(DEPTH * 3, CW // NCHIP), ((0, 8 - DEPTH * 3), (0, 0)))

    def own_slot(a):
        return lax.dynamic_update_slice(jnp.zeros((NCHIP,) + a.shape, a.dtype), a[None], (jchip, 0, 0))

    groups, splits = [], []
    for l in range(DEPTH):
        groups.append([cast["w_in"][l], cast["w_out"][l]] + ([own_slot(meta_tokens), own_slot(conv_flat)] if l == 0 else []))
        splits.append([2, 1] + ([0, 0] if l == 0 else []))
        groups.append([cast[n][l] for n in ("w_gate", "w_up", "w_down")])
        splits.append([1, 1, 1])
    sems, arrs = _ag_start(groups, splits)

    def gathered(gi, afters):
        got = _ag_wait("ag_wait_%d" % gi, arrs[gi], splits[gi], sems[gi][0], sems[gi][1], afters)
        axes = [s for s in splits[gi] if s]
        return list(_ag_forward("ag_forward_%d" % gi, got[:len(axes)], axes)) + list(got[len(axes):])

    win0_g, wout0_g, meta_g, conv_g = gathered(0, [])
    meta_full = jnp.concatenate([meta_g[s] for s in range(NCHIP)], axis=1)
    conv_full = jnp.concatenate([conv_g[s][:DEPTH * 3] for s in range(NCHIP)], axis=1)
    conv_full = conv_full.reshape(DEPTH, 3, CW)

    bias_rows = jnp.pad(b_gates, ((0, 0), (0, LANE - 8)))
    smalls = []
    for l in range(DEPTH):
        smalls.append((norm_mix_w[l][None], bias_rows[l][None], mlstm_norm_w[l][None],
                       jnp.pad(conv_full[l], ((0, 5), (0, 0))), norm_ffn_w[l][None]))

    h = jnp.concatenate([meta_full, x[0], jnp.zeros((lp - n_real, D), F32)], axis=0)
    saved, wts = [], []
    for l in range(DEPTH):
        def get_mix(h_in, l=l):
            win_g, wout_g = (win0_g, wout0_g) if l == 0 else gathered(2 * l, [h_in])
            return _in_weights(win_g) + (wout_g,)

        def get_ffn(h1, l=l):
            return tuple(gathered(2 * l + 1, [h1]))

        h, sv, wt = _layer_fwd(l, h, get_mix, get_ffn, smalls[l])
        saved.append(sv)
        wts.append(wt)
    tgt = jnp.pad(loss_target[0], ((N_META, lp - n_real), (0, 0)))
    dh, dhb, dnorm_final, loss_part = _loss_head(h, tgt, norm_final_w[None], n_real)

    names = ["w_in", "w_out", "w_gate", "w_up", "w_down"]
    params = {"w_in": (w_in, m_w_in, v_w_in), "w_out": (w_out, m_w_out, v_w_out), "w_gate": (w_gate, m_w_gate, v_w_gate),
              "w_up": (w_up, m_w_up, v_w_up), "w_down": (w_down, m_w_down, v_w_down)}
    big_out = {n: None for n in names}
    small_grads = [None] * DEPTH
    g_in = [None] * DEPTH

    def finish(l, group, handle, afters):
        for n, (g_mine, g_theirs) in zip(group, _rs_end(handle, afters, pf)):
            if n == "w_in":
                g_in[l] = (g_mine, g_theirs)
                continue
            w, m, v = params[n]
            big_out[n] = _adamw_layer("adamw_%s_l%d" % (n, l), l, g_mine, g_theirs, w, m, v, big_out[n], pf)
        return [big_out[n][3] for n in group if n != "w_in"]

    groups = []
    token = None
    for l in reversed(range(DEPTH)):
        def ffn_done(pieces, dh1b, l=l):
            handle, tok = _rs_begin("l%df" % l, pieces, [False] * 3, pf)
            groups.append((l, names[2:], handle))
            return dh1b + tok[0, 0].astype(BF16)

        dh, dhb, pieces, small_grads[l] = _layer_bwd(l, dh, dhb, saved[l], wts[l], smalls[l], ffn_done)
        handle, token = _rs_begin("l%dm" % l, pieces, [True, False], pf)
        groups.append((l, names[:2], handle))
        if l > 0:
            dhb = dhb + token[0, 0].astype(BF16)
    afters = [token]
    for l, group, handle in groups:
        afters = finish(l, group, handle, afters)
    out_t = lambda a: jnp.transpose(a, (1, 2, 0))
    big_out["w_in"] = [out_t(a) for a in _adamw_t("adamw_w_in", g_in, in_t(w_in), in_t(m_w_in), in_t(v_w_in), pf)]

    dnmw = jnp.stack([small_grads[l][0] for l in range(DEPTH)])
    dbias = jnp.stack([small_grads[l][1] for l in range(DEPTH)])
    dconv = jnp.stack([small_grads[l][2] for l in range(DEPTH)])
    dmnw = jnp.stack([small_grads[l][3] for l in range(DEPTH)])
    dnfw = jnp.stack([small_grads[l][4] for l in range(DEPTH)])
    part_shapes = [(N_META, D), (DEPTH, D), (DEPTH, 8), (DEPTH, 3, CW), (DEPTH, MW), (DEPTH, D), (D,), (LANE,)]
    pack = _pack_rows([dh[:N_META], dnmw, dbias, dconv, dmnw, dnfw, dnorm_final[0], loss_part[0]])
    tot = _unpack_rows(_allreduce_small(pack), part_shapes)
    g_meta_full, g_nmw, g_bias, g_conv_full, g_mnw, g_nfw, g_final, loss_row = tot
    mcols = D // NCHIP
    ccols = CW // NCHIP
    g_meta = lax.dynamic_slice_in_dim(g_meta_full, jchip * mcols, mcols, axis=1)
    g_conv = lax.dynamic_slice_in_dim(g_conv_full, jchip * ccols, ccols, axis=2)
    sm_g = [g_meta, g_nmw, g_bias, g_conv, g_mnw, g_nfw, g_final]
    sm_w = [meta_tokens, norm_mix_w, b_gates, conv_w, mlstm_norm_w, norm_ffn_w, norm_final_w]
    sm_m = [m_meta_tokens, m_norm_mix_w, m_b_gates, m_conv_w, m_mlstm_norm_w, m_norm_ffn_w, m_norm_final_w]
    sm_v = [v_meta_tokens, v_norm_mix_w, v_b_gates, v_conv_w, v_mlstm_norm_w, v_norm_ffn_w, v_norm_final_w]
    sm_shapes = [w.shape for w in sm_w]
    d_p, m_p, v_p = _adamw_flat(_pack_rows(sm_g), _pack_rows(sm_w), _pack_rows(sm_m), _pack_rows(sm_v))
    sm_d = _unpack_rows(d_p, sm_shapes)
    sm_nm = _unpack_rows(m_p, sm_shapes)
    sm_nv = _unpack_rows(v_p, sm_shapes)

    loss = loss_row[0]
    grad_x = dh[N_META:n_real][None]

    def ordered(sm, which):
        bo = {n: big_out[n][which] for n in names}
        return [sm[0], sm[1], bo["w_in"], sm[2], sm[3], sm[4], bo["w_out"], sm[5], bo["w_gate"], bo["w_up"], bo["w_down"], sm[6]]

    return (loss, grad_x, *ordered(sm_g, 0), *ordered(sm_d, 1), *ordered(sm_nm, 2), *ordered(sm_nv, 3))
```

```python
import functools

import jax
import jax.numpy as jnp
from jax import lax
from jax.experimental import pallas as pl
from jax.experimental.pallas import tpu as pltpu

F32 = jnp.float32
BF16 = jnp.bfloat16

D = 2048
N_META = 16
HEADS = 4
DQK = 128
DV = 256
MW = HEADS * DV
CW = D - MW
QKW = HEADS * DQK
DFF = 5632
DIN = 6152
NCHIP = 4
DEPTH = 2
GATE_CAP = 15.0
EPS = 1e-6
QK_SCALE = DQK ** -0.5
LANE = 128
VMEM_LIMIT = 60 * 1024 * 1024

ADAM_LR = 0.001
ADAM_B1 = 0.9
ADAM_B2 = 0.999
ADAM_EPS = 1e-08
ADAM_WD = 0.01
ADAM_STEP = 10

MESH = pl.DeviceIdType.MESH
ANY = pl.BlockSpec(memory_space=pl.ANY)


def _cp(sem):
    return pltpu.CompilerParams(dimension_semantics=sem, vmem_limit_bytes=VMEM_LIMIT)


def _sds(shape, dtype):
    return jax.ShapeDtypeStruct(shape, dtype)


_DN = {"nn": (((1,), (0,)), ((), ())), "nt": (((1,), (1,)), ((), ())), "tn": (((0,), (0,)), ((), ()))}


def _mm(name, kind, a, b, a_spec, b_spec, o_spec, out_shape, grid, acc_shape=None, res=None, res_spec=None):
    nk = grid[2]
    has_res = res is not None

    def body(*refs):
        if has_res:
            a_ref, b_ref, r_ref, o_ref = refs[:4]
        else:
            a_ref, b_ref, o_ref = refs[:3]
            r_ref = None
        p = lax.dot_general(a_ref[...], b_ref[...], _DN[kind], preferred_element_type=F32)
        if nk == 1:
            if r_ref is not None:
                p = p + r_ref[...]
            o_ref[...] = p.astype(o_ref.dtype)
        else:
            acc = refs[-1]
            k = pl.program_id(2)

            @pl.when(k == 0)
            def _():
                acc[...] = p

            @pl.when(k > 0)
            def _():
                acc[...] += p

            @pl.when(k == nk - 1)
            def _():
                r = acc[...]
                if r_ref is not None:
                    r = r + r_ref[...]
                o_ref[...] = r.astype(o_ref.dtype)

    ins = [a, b] + ([res] if has_res else [])
    in_specs = [a_spec, b_spec] + ([res_spec] if has_res else [])
    scratch = [pltpu.VMEM(acc_shape, F32)] if nk > 1 else []
    return pl.pallas_call(
        body, name=name, grid=grid, in_specs=in_specs, out_specs=o_spec, out_shape=out_shape,
        scratch_shapes=scratch, compiler_params=_cp(("parallel", "parallel", "arbitrary")))(*ins)


def _norm_fwd(name, h, w):
    lp = h.shape[0]
    tm = lp // 4

    def body(h_ref, w_ref, o_ref):
        x = h_ref[...]
        r = lax.rsqrt(jnp.mean(x * x, axis=1, keepdims=True) + EPS)
        o_ref[...] = (x * r * w_ref[...]).astype(BF16)

    return pl.pallas_call(
        body, name=name, grid=(4,),
        in_specs=[pl.BlockSpec((tm, D), lambda i: (i, 0)), pl.BlockSpec((1, D), lambda i: (0, 0))],
        out_specs=pl.BlockSpec((tm, D), lambda i: (i, 0)), out_shape=_sds((lp, D), BF16),
        compiler_params=_cp(("parallel",)))(h, w)


def _norm_bwd(name, dy, h, w, dres):
    lp = h.shape[0]
    tm = lp // 8

    def body(dy_ref, h_ref, w_ref, dres_ref, dh_ref, dhb_ref, dw_ref):
        x = h_ref[...]
        r = lax.rsqrt(jnp.mean(x * x, axis=1, keepdims=True) + EPS)
        xh = x * r
        dy_v = dy_ref[...]
        dxh = dy_v * w_ref[...]
        dx = r * (dxh - xh * jnp.mean(dxh * xh, axis=1, keepdims=True))
        dh = dres_ref[...] + dx
        dh_ref[...] = dh
        dhb_ref[...] = dh.astype(BF16)

        @pl.when(pl.program_id(0) == 0)
        def _():
            dw_ref[...] = jnp.zeros_like(dw_ref)

        dw_ref[0:1, :] += jnp.sum(dy_v * xh, axis=0, keepdims=True)

    row = pl.BlockSpec((tm, D), lambda i: (i, 0))
    return pl.pallas_call(
        body, name=name, grid=(8,),
        in_specs=[row, row, pl.BlockSpec((1, D), lambda i: (0, 0)), row],
        out_specs=[row, row, pl.BlockSpec((8, D), lambda i: (0, 0))],
        out_shape=[_sds((lp, D), F32), _sds((lp, D), BF16), _sds((8, D), F32)],
        compiler_params=_cp(("arbitrary",)))(dy, h, w, dres)


def _loss_head(h, tgt, w, n_real):
    lp = h.shape[0]
    tm = lp // 8

    def body(h_ref, t_ref, w_ref, dh_ref, dhb_ref, dw_ref, loss_ref):
        i = pl.program_id(0)
        x = h_ref[...]
        r = lax.rsqrt(jnp.mean(x * x, axis=1, keepdims=True) + EPS)
        xh = x * r
        wv = w_ref[...]
        row = i * tm + lax.broadcasted_iota(jnp.int32, (tm, 1), 0)
        valid = jnp.logical_and(row >= N_META, row < n_real)
        err = jnp.where(valid, xh * wv - t_ref[...], 0.0)
        dy_v = err * (1.0 / D)
        dxh = dy_v * wv
        dx = r * (dxh - xh * jnp.mean(dxh * xh, axis=1, keepdims=True))
        dh_ref[...] = dx
        dhb_ref[...] = dx.astype(BF16)

        @pl.when(i == 0)
        def _():
            dw_ref[...] = jnp.zeros_like(dw_ref)
            loss_ref[...] = jnp.zeros_like(loss_ref)

        dw_ref[0:1, :] += jnp.sum(dy_v * xh, axis=0, keepdims=True)
        part = jnp.sum(jnp.sum(err * err, axis=1, keepdims=True), axis=0, keepdims=True) * (0.5 / D)
        loss_ref[...] += jnp.broadcast_to(part, loss_ref.shape)

    row = pl.BlockSpec((tm, D), lambda i: (i, 0))
    return pl.pallas_call(
        body, name="loss_head", grid=(8,),
        in_specs=[row, row, pl.BlockSpec((1, D), lambda i: (0, 0))],
        out_specs=[row, row, pl.BlockSpec((8, D), lambda i: (0, 0)), pl.BlockSpec((8, LANE), lambda i: (0, 0))],
        out_shape=[_sds((lp, D), F32), _sds((lp, D), BF16), _sds((8, D), F32), _sds((8, LANE), F32)],
        compiler_params=_cp(("arbitrary",)))(h, tgt, w)


def _ffn_up(name, hf, wg4, wu4):
    lp = hf.shape[0]
    tm = lp // 4
    tn = DFF // NCHIP

    def body(x_ref, wg_ref, wu_ref, g_ref, u_ref, a_ref):
        x = x_ref[...]
        g = jnp.dot(x, wg_ref[...], preferred_element_type=F32)
        u = jnp.dot(x, wu_ref[...], preferred_element_type=F32)
        g_ref[...] = g.astype(BF16)
        u_ref[...] = u.astype(BF16)
        a_ref[...] = (g * jax.nn.sigmoid(g) * u).astype(BF16)

    wspec = pl.BlockSpec((None, D, tn), lambda j, i: (j, 0, 0))
    ospec = pl.BlockSpec((tm, tn), lambda j, i: (i, j))
    o = _sds((lp, DFF), BF16)
    return pl.pallas_call(
        body, name=name, grid=(NCHIP, 4),
        in_specs=[pl.BlockSpec((tm, D), lambda j, i: (i, 0)), wspec, wspec],
        out_specs=[ospec, ospec, ospec], out_shape=[o, o, o],
        compiler_params=_cp(("parallel", "parallel")))(hf, wg4, wu4)


def _ffn_bwd_act(name, dhb, wd4, g, u):
    lp = dhb.shape[0]
    tm = lp // 4
    tn = DFF // NCHIP

    def body(d_ref, w_ref, g_ref, u_ref, dg_ref, du_ref):
        da = lax.dot_general(d_ref[...], w_ref[...], _DN["nt"], preferred_element_type=F32)
        gv = g_ref[...].astype(F32)
        uv = u_ref[...].astype(F32)
        sg = jax.nn.sigmoid(gv)
        dg_ref[...] = (da * uv * (sg * (1.0 + gv * (1.0 - sg)))).astype(BF16)
        du_ref[...] = (da * (gv * sg)).astype(BF16)

    ospec = pl.BlockSpec((tm, tn), lambda j, i: (i, j))
    o = _sds((lp, DFF), BF16)
    return pl.pallas_call(
        body, name=name, grid=(NCHIP, 4),
        in_specs=[pl.BlockSpec((tm, D), lambda j, i: (i, 0)),
                  pl.BlockSpec((None, tn, D), lambda j, i: (j, 0, 0)), ospec, ospec],
        out_specs=[ospec, ospec], out_shape=[o, o],
        compiler_params=_cp(("parallel", "parallel")))(dhb, wd4, g, u)


def _shift_rows(x, d, row):
    return jnp.where(row >= d, pltpu.roll(x, d, axis=0), 0.0)


def _scan_steps(lp):
    d = 1
    while d < lp:
        yield d
        d *= 2


def _gate_values(pre):
    t = GATE_CAP * jnp.tanh(pre * (1.0 / GATE_CAP))
    lf = jnp.minimum(t, 0.0) - jnp.log(1.0 + jnp.exp(-jnp.abs(t)))
    return t, lf


def _gate_prep(name, gates_pre, bias):
    lp = gates_pre.shape[0]

    def body(p_ref, b_ref, grow_ref, m_ref, c_ref):
        pre = p_ref[...] + b_ref[...]
        lane = lax.broadcasted_iota(jnp.int32, (lp, LANE), 1)
        row = lax.broadcasted_iota(jnp.int32, (lp, LANE), 0)
        t, lf = _gate_values(pre)
        f = jnp.where(jnp.logical_and(lane >= HEADS, lane < 2 * HEADS), lf, 0.0)
        for d in _scan_steps(lp):
            f = f + _shift_rows(f, d, row)
        fs = pltpu.roll(f, LANE - HEADS, axis=1)
        g = jnp.where(lane < HEADS, t - fs, 0.0)
        m = g
        for d in _scan_steps(lp):
            m = jnp.maximum(m, jnp.where(row >= d, pltpu.roll(m, d, axis=0), m))
        grow_ref[...] = g.T
        m_ref[...] = m
        c_ref[...] = jnp.where(lane < HEADS, -fs - m, 0.0)

    full = pl.BlockSpec((lp, LANE), lambda: (0, 0))
    return pl.pallas_call(
        body, name=name, in_specs=[full, pl.BlockSpec((1, LANE), lambda: (0, 0))],
        out_specs=[pl.BlockSpec((LANE, lp), lambda: (0, 0)), full, full],
        out_shape=[_sds((LANE, lp), F32), _sds((lp, LANE), F32), _sds((lp, LANE), F32)],
        compiler_params=pltpu.CompilerParams(vmem_limit_bytes=VMEM_LIMIT))(gates_pre, bias)


def _pick_lane(blk, h):
    lane = lax.broadcasted_iota(jnp.int32, blk.shape, 1)
    return jnp.sum(jnp.where(lane == h, blk, 0.0), axis=1, keepdims=True)


def _mlstm_weights(q, k, grow, mcol, i, bq, nk):
    s = lax.dot_general(q, k, _DN["nt"], preferred_element_type=F32) * QK_SCALE
    row = i * bq + lax.broadcasted_iota(jnp.int32, (bq, 1), 0)
    col = lax.broadcasted_iota(jnp.int32, (1, nk), 1)
    a = jnp.where(col <= row, jnp.exp(jnp.minimum(grow - mcol, 0.0)), 0.0)
    return s, a


def _per_query_tile(i, nq, bq, lp, compute):
    for ii in range(nq):
        nk = min(lp, -(-((ii + 1) * bq) // LANE) * LANE)
        pl.when(i == ii)(functools.partial(compute, nk))


def _mlstm_fwd(name, qkv, grow, mcol_all, ccol_all, nq):
    lp = qkv.shape[0]
    bq = lp // nq

    def body(q_ref, k_ref, v_ref, grow_ref, m_ref, c_ref, o_ref):
        h = pl.program_id(0)
        i = pl.program_id(1)
        mcol = _pick_lane(m_ref[...], h)
        ccol = _pick_lane(c_ref[...], h)

        def compute(nk):
            grow_h = grow_ref[pl.ds(h, 1), 0:nk]
            s, a = _mlstm_weights(q_ref[...], k_ref[0:nk, :], grow_h, mcol, i, bq, nk)
            p = a * s
            den = jnp.sum(p, axis=1, keepdims=True)
            num = jnp.dot(p.astype(BF16), v_ref[0:nk, :], preferred_element_type=F32)
            o_ref[...] = num / jnp.maximum(jnp.abs(den), jnp.exp(ccol))

        _per_query_tile(i, nq, bq, lp, compute)

    return pl.pallas_call(
        body, name=name, grid=(HEADS, nq),
        in_specs=[pl.BlockSpec((bq, DQK), lambda h, i: (i, h)),
                  pl.BlockSpec((lp, DQK), lambda h, i: (0, HEADS + h)),
                  pl.BlockSpec((lp, DV), lambda h, i: (0, HEADS + h)),
                  pl.BlockSpec((8, lp), lambda h, i: (0, 0)),
                  pl.BlockSpec((bq, LANE), lambda h, i: (i, 0)),
                  pl.BlockSpec((bq, LANE), lambda h, i: (i, 0))],
        out_specs=pl.BlockSpec((bq, DV), lambda h, i: (i, h)),
        out_shape=_sds((lp, MW), F32),
        compiler_params=_cp(("parallel", "parallel")))(qkv, qkv, qkv, grow, mcol_all, ccol_all)


def _mlstm_bwd(name, qkv, grow, mcol_all, ccol_all, ht, dht, nq):
    lp = qkv.shape[0]
    bq = lp // nq

    def body(q_ref, k_ref, v_ref, grow_ref, m_ref, c_ref, ht_ref, dht_ref,
             dq_ref, dk_ref, dv_ref, dgrow_ref, dfx_ref, dkt_acc, dvt_acc):
        h = pl.program_id(0)
        i = pl.program_id(1)

        @pl.when(jnp.logical_and(h == 0, i == 0))
        def _():
            dgrow_ref[...] = jnp.zeros_like(dgrow_ref)
            dfx_ref[...] = jnp.zeros_like(dfx_ref)

        @pl.when(i == 0)
        def _():
            dkt_acc[...] = jnp.zeros_like(dkt_acc)
            dvt_acc[...] = jnp.zeros_like(dvt_acc)

        mcol = _pick_lane(m_ref[...], h)
        ccol = _pick_lane(c_ref[...], h)

        def compute(nk):
            q = q_ref[...]
            k = k_ref[0:nk, :]
            v = v_ref[0:nk, :]
            grow_h = grow_ref[pl.ds(h, 1), 0:nk]
            s, a = _mlstm_weights(q, k, grow_h, mcol, i, bq, nk)
            p = a * s
            den = jnp.sum(p, axis=1, keepdims=True)
            clamp = jnp.exp(ccol)
            active = jnp.abs(den) < clamp
            dd = jnp.maximum(jnp.abs(den), clamp)
            dht_v = dht_ref[...]
            hdh = jnp.sum(dht_v * ht_ref[...], axis=1, keepdims=True)
            dn = (dht_v / dd).astype(BF16)
            dden = jnp.where(active, 0.0, -(hdh / dd) * jnp.sign(den))
            dp = lax.dot_general(dn, v, _DN["nt"], preferred_element_type=F32) + dden
            rmat = dp * p
            dgrow_ref[pl.ds(h, 1), 0:nk] += jnp.sum(rmat, axis=0, keepdims=True)
            ds = (dp * a * QK_SCALE).astype(BF16)
            dq_ref[...] = jnp.dot(ds, k, preferred_element_type=F32).astype(BF16)
            dkt_acc[:, 0:nk] += lax.dot_general(q, ds, _DN["tn"], preferred_element_type=F32)
            dvt_acc[:, 0:nk] += lax.dot_general(dn, p.astype(BF16), _DN["tn"], preferred_element_type=F32)
            lane = lax.broadcasted_iota(jnp.int32, (bq, LANE), 1)
            r0 = pl.multiple_of(i * bq, 16)
            dfx_ref[pl.ds(r0, bq), :] += jnp.where(lane == h, jnp.sum(rmat, axis=1, keepdims=True), 0.0)

        _per_query_tile(i, nq, bq, lp, compute)

        @pl.when(i == nq - 1)
        def _():
            dk_ref[...] = dkt_acc[...].T.astype(BF16)
            dv_ref[...] = dvt_acc[...].T.astype(BF16)

    return pl.pallas_call(
        body, name=name, grid=(HEADS, nq),
        in_specs=[pl.BlockSpec((bq, DQK), lambda h, i: (i, h)),
                  pl.BlockSpec((lp, DQK), lambda h, i: (0, HEADS + h)),
                  pl.BlockSpec((lp, DV), lambda h, i: (0, HEADS + h)),
                  pl.BlockSpec((8, lp), lambda h, i: (0, 0)),
                  pl.BlockSpec((bq, LANE), lambda h, i: (i, 0)),
                  pl.BlockSpec((bq, LANE), lambda h, i: (i, 0)),
                  pl.BlockSpec((bq, DV), lambda h, i: (i, h)),
                  pl.BlockSpec((bq, DV), lambda h, i: (i, h))],
        out_specs=[pl.BlockSpec((bq, DQK), lambda h, i: (i, h)),
                   pl.BlockSpec((lp, DQK), lambda h, i: (0, h)),
                   pl.BlockSpec((lp, DV), lambda h, i: (0, h)),
                   pl.BlockSpec((LANE, lp), lambda h, i: (0, 0)),
                   pl.BlockSpec((lp, LANE), lambda h, i: (0, 0))],
        out_shape=[_sds((lp, QKW), BF16), _sds((lp, QKW), BF16), _sds((lp, MW), BF16),
                   _sds((LANE, lp), F32), _sds((lp, LANE), F32)],
        scratch_shapes=[pltpu.VMEM((DQK, lp), F32), pltpu.VMEM((DV, lp), F32)],
        compiler_params=_cp(("arbitrary", "arbitrary")))(qkv, qkv, qkv, grow, mcol_all, ccol_all, ht, dht)


def _gate_bwd(name, gates_pre, bias, dgrow, dfx):
    lp = gates_pre.shape[0]

    def body(p_ref, b_ref, dgrow_ref, dfx_ref, dg_ref, dgb_ref, db_ref):
        pre = p_ref[...] + b_ref[...]
        lane = lax.broadcasted_iota(jnp.int32, (lp, LANE), 1)
        row = lax.broadcasted_iota(jnp.int32, (lp, LANE), 0)
        th = jnp.tanh(pre * (1.0 / GATE_CAP))
        t = GATE_CAP * th
        dgc = jnp.where(lane < HEADS, dgrow_ref[...].T, 0.0)
        df = jnp.where(lane < HEADS, dfx_ref[...] - dgc, 0.0)
        for d in _scan_steps(lp):
            df = df + jnp.where(row < lp - d, pltpu.roll(df, lp - d, axis=0), 0.0)
        dlf = pltpu.roll(df, HEADS, axis=1)
        dt = jnp.where(lane < HEADS, dgc, dlf * jax.nn.sigmoid(-t))
        dpre = jnp.where(lane < 2 * HEADS, dt * (1.0 - th * th), 0.0)
        dg_ref[...] = dpre
        dgb_ref[...] = dpre.astype(BF16)
        db_ref[...] = jnp.broadcast_to(jnp.sum(dpre, axis=0, keepdims=True), db_ref.shape)

    full = pl.BlockSpec((lp, LANE), lambda: (0, 0))
    return pl.pallas_call(
        body, name=name,
        in_specs=[full, pl.BlockSpec((1, LANE), lambda: (0, 0)), pl.BlockSpec((LANE, lp), lambda: (0, 0)), full],
        out_specs=[full, full, pl.BlockSpec((8, LANE), lambda: (0, 0))],
        out_shape=[_sds((lp, LANE), F32), _sds((lp, LANE), BF16), _sds((8, LANE), F32)],
        compiler_params=pltpu.CompilerParams(vmem_limit_bytes=VMEM_LIMIT))(gates_pre, bias, dgrow, dfx)


CB = 256


def _e_specs(lp):
    return [pl.BlockSpec((None, lp, CB), functools.partial(lambda c, j: (c, 0, j), c)) for c in range(4)]


def _mix_fwd(name, ht, e, mnw, cw):
    lp = ht.shape[0]

    def body(ht_ref, og_ref, u_ref, gb_ref, gc_ref, mnw_ref, cw_ref, o_ref):
        x = ht_ref[...]
        r = lax.rsqrt(jnp.mean(x * x, axis=1, keepdims=True) + EPS)
        o_ref[0] = (jax.nn.sigmoid(og_ref[...].astype(F32)) * (x * r * mnw_ref[...])).astype(BF16)
        row = lax.broadcasted_iota(jnp.int32, (lp, CB), 0)
        a = gc_ref[...].astype(F32) * u_ref[...].astype(F32)
        conv = cw_ref[2:3, :] * a + cw_ref[1:2, :] * _shift_rows(a, 1, row) + cw_ref[0:1, :] * _shift_rows(a, 2, row)
        o_ref[1] = (gb_ref[...].astype(F32) * conv).astype(BF16)

    col = pl.BlockSpec((lp, CB), lambda j: (0, j))
    return pl.pallas_call(
        body, name=name, grid=(4,),
        in_specs=[col] + _e_specs(lp) + [pl.BlockSpec((1, CB), lambda j: (0, j)), pl.BlockSpec((8, CB), lambda j: (0, j))],
        out_specs=pl.BlockSpec((2, lp, CB), lambda j: (0, 0, j)), out_shape=_sds((2, lp, MW), BF16),
        compiler_params=_cp(("parallel",)))(ht, e, e, e, e, mnw, cw)


def _mix_bwd(name, dmix, ht, e, mnw, cw):
    lp = ht.shape[0]

    def body(dhm_ref, dhc_ref, ht_ref, og_ref, u_ref, gb_ref, gc_ref, mnw_ref, cw_ref,
             dht_ref, de_ref, dmnw_ref, dcw_ref):
        x = ht_ref[...]
        r = lax.rsqrt(jnp.mean(x * x, axis=1, keepdims=True) + EPS)
        xh = x * r
        w = mnw_ref[...]
        sg = jax.nn.sigmoid(og_ref[...].astype(F32))
        dhm = dhm_ref[...]
        de_ref[0] = (dhm * (xh * w) * (sg * (1.0 - sg))).astype(BF16)
        dn = dhm * sg
        dmnw_ref[...] = jnp.broadcast_to(jnp.sum(dn * xh, axis=0, keepdims=True), dmnw_ref.shape)
        dxh = dn * w
        dht_ref[...] = r * (dxh - xh * jnp.mean(dxh * xh, axis=1, keepdims=True))

        row = lax.broadcasted_iota(jnp.int32, (lp, CB), 0)
        uv = u_ref[...].astype(F32)
        gcv = gc_ref[...].astype(F32)
        gbv = gb_ref[...].astype(F32)
        a = gcv * uv
        a1 = _shift_rows(a, 1, row)
        a2 = _shift_rows(a, 2, row)
        dhc = dhc_ref[...]
        conv = cw_ref[2:3, :] * a + cw_ref[1:2, :] * a1 + cw_ref[0:1, :] * a2
        de_ref[2] = (dhc * conv).astype(BF16)
        dconv = dhc * gbv
        dcw_ref[...] = jnp.zeros_like(dcw_ref)
        dcw_ref[0:1, :] = jnp.sum(dconv * a2, axis=0, keepdims=True)
        dcw_ref[1:2, :] = jnp.sum(dconv * a1, axis=0, keepdims=True)
        dcw_ref[2:3, :] = jnp.sum(dconv * a, axis=0, keepdims=True)
        up1 = jnp.where(row < lp - 1, pltpu.roll(dconv, lp - 1, axis=0), 0.0)
        up2 = jnp.where(row < lp - 2, pltpu.roll(dconv, lp - 2, axis=0), 0.0)
        da = cw_ref[2:3, :] * dconv + cw_ref[1:2, :] * up1 + cw_ref[0:1, :] * up2
        de_ref[1] = (da * gcv).astype(BF16)
        de_ref[3] = (da * uv).astype(BF16)

    col = pl.BlockSpec((lp, CB), lambda j: (0, j))
    small = pl.BlockSpec((8, CB), lambda j: (0, j))
    return pl.pallas_call(
        body, name=name, grid=(4,),
        in_specs=[col, pl.BlockSpec((lp, CB), lambda j: (0, 4 + j)), col] + _e_specs(lp)
                 + [pl.BlockSpec((1, CB), lambda j: (0, j)), small],
        out_specs=[col, pl.BlockSpec((4, lp, CB), lambda j: (0, 0, j)), small, small],
        out_shape=[_sds((lp, MW), F32), _sds((4, lp, MW), BF16), _sds((8, MW), F32), _sds((8, CW), F32)],
        compiler_params=_cp(("parallel",)))(dmix, dmix, ht, e, e, e, e, mnw, cw)


def _row_tile(r, c, itemsize, budget=1536 * 1024, mult=16):
    best = None
    for t in range(mult, r + 1, mult):
        if r % t == 0 and t * c * itemsize <= budget:
            best = t
    if best is None:
        best = r
    return best


def _grid_spec(grid, in_specs, out_specs, scratch=()):
    return pltpu.PrefetchScalarGridSpec(num_scalar_prefetch=1, grid=grid, in_specs=in_specs,
                                        out_specs=out_specs, scratch_shapes=list(scratch))


def _cast_into(name, w, layer, pf):
    _, r, c = w.shape
    tr = _row_tile(r, c, 4)

    def body(pf_ref, x_ref, o_ref):
        o_ref[...] = x_ref[...].astype(BF16)

    return pl.pallas_call(
        body, name=name, out_shape=_sds((NCHIP, r, c), BF16),
        grid_spec=_grid_spec((r // tr,), [pl.BlockSpec((None, tr, c), lambda i, pf: (layer, i, 0))],
                             pl.BlockSpec((None, tr, c), lambda i, pf: (pf[1], i, 0))),
        compiler_params=_cp(("parallel",)))(pf, w)


TCOL = 256


def _cast_into_t(name, w_t, layer, pf):
    c, nl, r = w_t.shape

    def body(pf_ref, x_ref, o_ref):
        o_ref[...] = x_ref[:, layer, :].astype(BF16)

    return pl.pallas_call(
        body, name=name, out_shape=_sds((NCHIP, c, r), BF16),
        grid_spec=_grid_spec((r // TCOL,), [pl.BlockSpec((c, nl, TCOL), lambda i, pf: (0, 0, i))],
                             pl.BlockSpec((None, c, TCOL), lambda i, pf: (pf[1], 0, i))),
        compiler_params=_cp(("parallel",)))(pf, w_t)


def _add2_bf16(name, dw, got, pf, by_cols=False):
    n4, r2, c2 = got.shape

    def body(pf_ref, a_ref, b_ref, o_ref):
        o_ref[...] = (a_ref[...].astype(F32) + b_ref[...].astype(F32)).astype(BF16)

    if by_cols:
        nch = c2 // TCOL
        spec = pl.BlockSpec((None, r2, TCOL), lambda s, i, pf: (s, 0, i))
        mine = pl.BlockSpec((None, r2, TCOL), lambda s, i, pf: (s, 0, pf[0] * nch + i))
    else:
        tr = _row_tile(r2, c2, 4)
        nch = r2 // tr
        spec = pl.BlockSpec((None, tr, c2), lambda s, i, pf: (s, i, 0))
        mine = pl.BlockSpec((None, tr, c2), lambda s, i, pf: (s, pf[0] * nch + i, 0))
    return pl.pallas_call(
        body, name=name, out_shape=_sds((n4, r2, c2), BF16),
        grid_spec=_grid_spec((n4, nch), [mine, spec], spec),
        compiler_params=_cp(("parallel", "parallel")))(pf, dw, got)


def _adam_math(w, g, m, v):
    m2 = ADAM_B1 * m + (1.0 - ADAM_B1) * g
    v2 = ADAM_B2 * v + (1.0 - ADAM_B2) * (g * g)
    m_hat = m2 / (1.0 - ADAM_B1 ** ADAM_STEP)
    v_hat = v2 / (1.0 - ADAM_B2 ** ADAM_STEP)
    delta = -ADAM_LR * (m_hat / (jnp.sqrt(v_hat) + ADAM_EPS) + ADAM_WD * w)
    return delta, m2, v2


def _adamw_layer(name, layer, g_mine, g_theirs, w, m, v, prev, pf):
    _, r, c = w.shape
    r2 = r // 2
    tr = _row_tile(r2, c, 4, budget=1024 * 1024, mult=8)
    nch = r2 // tr
    n_alias = 0 if prev is None else 4

    def body(*refs):
        pf_ref, gm_ref, gt_ref, w_ref, m_ref, v_ref = refs[:6]
        go_ref, d_ref, mo_ref, vo_ref = refs[6 + n_alias:]
        mine = (pl.program_id(0) // nch) == pf_ref[0]
        gv = jnp.where(mine, gm_ref[...], gt_ref[...])
        delta, m2, v2 = _adam_math(w_ref[...], gv, m_ref[...], v_ref[...])
        go_ref[...] = gv
        d_ref[...] = delta
        mo_ref[...] = m2
        vo_ref[...] = v2

    slab = pl.BlockSpec((None, tr, c), lambda i, pf: (layer, i, 0))
    ins = [g_mine, g_theirs, w, m, v] + (list(prev) if prev is not None else [])
    in_specs = [pl.BlockSpec((tr, c), lambda i, pf: (jnp.clip(i - pf[0] * nch, 0, nch - 1), 0)),
                pl.BlockSpec((tr, c), lambda i, pf: (jnp.clip(i - (1 - pf[0]) * nch, 0, nch - 1), 0)),
                slab, slab, slab] + [ANY] * n_alias
    o = _sds(w.shape, F32)
    return pl.pallas_call(
        body, name=name, out_shape=[o] * 4, grid_spec=_grid_spec((2 * nch,), in_specs, [slab] * 4),
        input_output_aliases={6 + k: k for k in range(n_alias)},
        compiler_params=_cp(("parallel",)))(pf, *ins)


def _adamw_t(name, gs, w_t, m_t, v_t, pf):
    c, nl, r = w_t.shape
    ta = LANE
    nch = (r // 2) // ta

    def body(*refs):
        pf_ref = refs[0]
        g_refs = refs[1:1 + 2 * nl]
        w_ref, m_ref, v_ref, go_ref, d_ref, mo_ref, vo_ref = refs[1 + 2 * nl:]
        mine = (pl.program_id(0) // nch) == pf_ref[0]
        gv = jnp.stack([jnp.where(mine, g_refs[2 * l][...], g_refs[2 * l + 1][...]) for l in range(nl)], axis=1)
        delta, m2, v2 = _adam_math(w_ref[...], gv, m_ref[...], v_ref[...])
        go_ref[...] = gv
        d_ref[...] = delta
        mo_ref[...] = m2
        vo_ref[...] = v2

    both = pl.BlockSpec((c, nl, ta), lambda i, pf: (0, 0, i))
    g_specs = []
    for l in range(nl):
        g_specs += [pl.BlockSpec((c, ta), lambda i, pf: (0, jnp.clip(i - pf[0] * nch, 0, nch - 1))),
                    pl.BlockSpec((c, ta), lambda i, pf: (0, jnp.clip(i - (1 - pf[0]) * nch, 0, nch - 1)))]
    o = _sds(w_t.shape, F32)
    flat_g = [a for pair in gs for a in pair]
    return pl.pallas_call(
        body, name=name, out_shape=[o] * 4, grid_spec=_grid_spec((2 * nch,), g_specs + [both] * 3, [both] * 4),
        compiler_params=_cp(("parallel",)))(pf, *flat_g, w_t, m_t, v_t)


def _adamw_flat(g, w, m, v):
    def body(g_ref, w_ref, m_ref, v_ref, d_ref, mo_ref, vo_ref):
        delta, m2, v2 = _adam_math(w_ref[...], g_ref[...], m_ref[...], v_ref[...])
        d_ref[...] = delta
        mo_ref[...] = m2
        vo_ref[...] = v2

    o = _sds(w.shape, F32)
    return pl.pallas_call(body, name="adamw_small", out_shape=[o, o, o])(g, w, m, v)


def _place():
    x, y, c = lax.axis_index("x"), lax.axis_index("y"), lax.axis_index("c")
    chips = [(1 - x, y), (x, 1 - y), (1 - x, 1 - y)]
    return x, y, c, chips


def _rs_chips(name, ps):
    n = len(ps)

    def body(*refs):
        ins = refs[:n]
        got = refs[n:2 * n]
        send, recv = refs[2 * n:]
        x, y, c, chips = _place()
        cps = []
        for t in range(n):
            for k, chip in enumerate(chips):
                jk = 2 * chip[0] + chip[1]
                cp = pltpu.make_async_remote_copy(
                    src_ref=ins[t].at[jk], dst_ref=got[t].at[k],
                    send_sem=send.at[3 * t + k], recv_sem=recv.at[3 * t + k],
                    device_id=(*chip, c), device_id_type=MESH)
                cp.start()
                cps.append(cp)
        for cp in cps:
            cp.wait()

    dma = pltpu.SemaphoreType.DMA
    return pl.pallas_call(
        body, name=name, in_specs=[ANY] * n, out_specs=[ANY] * n,
        out_shape=[_sds((3,) + p.shape[1:], BF16) for p in ps],
        scratch_shapes=[dma((3 * n,)), dma((3 * n,))],
        compiler_params=pltpu.CompilerParams(has_side_effects=True))(*ps)


HBM = pl.BlockSpec(memory_space=pltpu.HBM)
SEM = pl.BlockSpec(memory_space=pltpu.SEMAPHORE)
EFFECT = pltpu.SideEffectType.DATAFLOW_SIDE_EFFECTING


def _in_hbm(a):
    return pltpu.with_memory_space_constraint(a, pltpu.HBM)


def _rs_chips_copies(ins, lands, send, recv):
    x, y, c, chips = _place()
    cps = []
    for t in range(len(ins)):
        for k, chip in enumerate(chips):
            jk = 2 * chip[0] + chip[1]
            cps.append(pltpu.make_async_remote_copy(
                src_ref=ins[t].at[jk], dst_ref=lands[t].at[k], send_sem=send.at[3 * t + k],
                recv_sem=recv.at[3 * t + k], device_id=(*chip, c), device_id_type=MESH))
    return cps


def _rs_chips_start(name, ps):
    n = len(ps)

    def body(*refs):
        ins, lands = refs[:n], refs[n:2 * n]
        send, recv = refs[2 * n], refs[2 * n + 1]
        token = refs[-1]
        for cp in _rs_chips_copies(ins, lands, send, recv):
            cp.start()
        token[...] = jnp.zeros_like(token)

    dma = pltpu.SemaphoreType.DMA
    lands = [lax.empty((3,) + p.shape[1:], BF16) for p in ps]
    out_shape = ([dma((3 * n,)), dma((3 * n,))] + [pltpu.HBM(p.shape, BF16) for p in ps]
                 + [pltpu.HBM(z.shape, BF16) for z in lands] + [_sds((8, LANE), F32)])
    outs = pl.pallas_call(
        body, name=name, out_shape=out_shape, in_specs=[HBM] * (2 * n),
        out_specs=[SEM, SEM] + [HBM] * (2 * n) + [pl.BlockSpec(memory_space=pltpu.VMEM)],
        input_output_aliases={i: 2 + i for i in range(2 * n)},
        compiler_params=pltpu.CompilerParams(has_side_effects=EFFECT))(
            *[_in_hbm(p) for p in ps], *[_in_hbm(z) for z in lands])
    return outs[0], outs[1], outs[2:2 + n], outs[2 + n:2 + 2 * n], outs[-1]


def _rs_chips_wait(name, send, recv, ps, lands, afters):
    n = len(ps)

    def body(*refs):
        ins, zones = refs[:n], refs[n:2 * n]
        send_ref, recv_ref = refs[2 * n], refs[2 * n + 1]
        for cp in _rs_chips_copies(ins, zones, send_ref, recv_ref):
            cp.wait_send()
            cp.wait_recv()

    outs = pl.pallas_call(
        body, name=name, out_shape=[pltpu.HBM(p.shape, BF16) for p in ps] + [pltpu.HBM(z.shape, BF16) for z in lands],
        in_specs=[HBM] * (2 * n) + [SEM, SEM] + [ANY] * len(afters), out_specs=[HBM] * (2 * n),
        input_output_aliases={i: i for i in range(2 * n)},
        compiler_params=pltpu.CompilerParams(has_side_effects=EFFECT))(*ps, *lands, send, recv, *afters)
    return outs[:n], outs[n:]


def _sibling():
    x, y, c, _ = _place()
    return (x, y, 1 - c)


def _pair_send(name, dw, pf, by_cols=False):
    n4, r, c = dw.shape
    blk = (1, r, c // 2) if by_cols else (1, r // 2, c)
    idx = (lambda s, pf: (s, 0, 1 - pf[0])) if by_cols else (lambda s, pf: (s, 1 - pf[0], 0))

    def body(pf_ref, x_ref, got_ref, ssem, rsem):
        s = pl.program_id(0)
        cp = pltpu.make_async_remote_copy(src_ref=x_ref, dst_ref=got_ref.at[pl.ds(s, 1)], send_sem=ssem,
                                          recv_sem=rsem, device_id=_sibling(), device_id_type=MESH)
        cp.start()
        cp.wait_send()

        @pl.when(s == n4 - 1)
        def _():
            pltpu.make_async_remote_copy(src_ref=got_ref, dst_ref=got_ref, send_sem=ssem, recv_sem=rsem,
                                         device_id=_sibling(), device_id_type=MESH).wait_recv()

    dma = pltpu.SemaphoreType.DMA
    return pl.pallas_call(
        body, name=name, out_shape=_sds((n4,) + blk[1:], BF16),
        grid_spec=_grid_spec((n4,), [pl.BlockSpec(blk, idx)], ANY, scratch=[dma(()), dma(())]),
        compiler_params=pltpu.CompilerParams(dimension_semantics=("arbitrary",), has_side_effects=True,
                                             vmem_limit_bytes=VMEM_LIMIT))(pf, dw)


def _add4_join(name, p, got, pf, by_cols=False):
    n4, r2, c = p.shape
    if by_cols:
        tr, nch = r2, c // TCOL
        blk, idx = (r2, TCOL), (lambda i: (0, i))
    else:
        tr = _row_tile(r2, c, 4)
        nch = r2 // tr
        blk, idx = (tr, c), (lambda i: (i, 0))

    def body(pf_ref, p_ref, g_ref, mine_ref, theirs_ref, ssem, rsem):
        i = pl.program_id(0)
        s = p_ref[...].astype(F32)
        for k in range(3):
            s = s + g_ref[k].astype(F32)
        mine_ref[...] = s
        if by_cols:
            dst = theirs_ref.at[:, pl.ds(pl.multiple_of(i * TCOL, LANE), TCOL)]
        else:
            dst = theirs_ref.at[pl.ds(pl.multiple_of(i * tr, 8), tr), :]
        cp = pltpu.make_async_remote_copy(src_ref=mine_ref, dst_ref=dst,
                                          send_sem=ssem, recv_sem=rsem, device_id=_sibling(), device_id_type=MESH)
        cp.start()
        cp.wait_send()

        @pl.when(i == nch - 1)
        def _():
            pltpu.make_async_remote_copy(src_ref=theirs_ref, dst_ref=theirs_ref, send_sem=ssem, recv_sem=rsem,
                                         device_id=_sibling(), device_id_type=MESH).wait_recv()

    dma = pltpu.SemaphoreType.DMA
    o = _sds((r2, c), F32)
    return pl.pallas_call(
        body, name=name, out_shape=[o, o],
        grid_spec=_grid_spec((nch,), [pl.BlockSpec((None,) + blk, lambda i, pf: (pf[1],) + idx(i)),
                                      pl.BlockSpec((3,) + blk, lambda i, pf: (0,) + idx(i))],
                             [pl.BlockSpec(blk, lambda i, pf: idx(i)), ANY], scratch=[dma(()), dma(())]),
        compiler_params=pltpu.CompilerParams(dimension_semantics=("arbitrary",), has_side_effects=True,
                                             vmem_limit_bytes=VMEM_LIMIT))(pf, p, got)


def _half_of(g, slot, which, axis):
    half = g.shape[axis] // 2
    if axis == 1:
        return g.at[slot, pl.ds(which * half, half), :]
    return g.at[slot, :, pl.ds(which * half, half)]


def _ag_copies(arrs, split, send, recv):
    x, y, c, chips = _place()
    j = 2 * x + y
    cps = []
    for t, g in enumerate(arrs):
        piece = _half_of(g, j, c, split[t]) if split[t] else g.at[j]
        for k, chip in enumerate(chips):
            cps.append(pltpu.make_async_remote_copy(
                src_ref=piece, dst_ref=piece, send_sem=send.at[3 * t + k], recv_sem=recv.at[3 * t + k],
                device_id=(*chip, c), device_id_type=MESH))
    return cps


def _ag_start(name, groups, splits):
    sizes = [len(g) for g in groups]
    flat = [a for g in groups for a in g]
    n = len(flat)

    def body(*refs):
        ins = refs[:n]
        sems = refs[n:n + 2 * len(groups)]
        o = 0
        for gi, sz in enumerate(sizes):
            for cp in _ag_copies(ins[o:o + sz], splits[gi], sems[2 * gi], sems[2 * gi + 1]):
                cp.start()
            o += sz

    dma = pltpu.SemaphoreType.DMA
    sem_shapes = [dma((3 * sz,)) for sz in sizes for _ in range(2)]
    outs = pl.pallas_call(
        body, name=name, out_shape=sem_shapes + [pltpu.HBM(a.shape, a.dtype) for a in flat],
        in_specs=[HBM] * n, out_specs=[SEM] * len(sem_shapes) + [HBM] * n,
        input_output_aliases={i: len(sem_shapes) + i for i in range(n)},
        compiler_params=pltpu.CompilerParams(has_side_effects=EFFECT))(*[_in_hbm(a) for a in flat])
    sems, arrs, o = [], [], len(sem_shapes)
    for gi, sz in enumerate(sizes):
        sems.append((outs[2 * gi], outs[2 * gi + 1]))
        arrs.append(list(outs[o:o + sz]))
        o += sz
    return sems, arrs


def _ag_wait(name, arrs, split, send, recv, afters):
    n = len(arrs)

    def body(*refs):
        for cp in _ag_copies(refs[:n], split, refs[n], refs[n + 1]):
            cp.wait_send()
            cp.wait_recv()

    return pl.pallas_call(
        body, name=name, out_shape=[pltpu.HBM(a.shape, a.dtype) for a in arrs],
        in_specs=[HBM] * n + [SEM, SEM] + [ANY] * len(afters), out_specs=[HBM] * n,
        input_output_aliases={i: i for i in range(n)},
        compiler_params=pltpu.CompilerParams(has_side_effects=EFFECT))(*arrs, send, recv, *afters)


def _ag_forward(name, arrs, axes):
    n = len(arrs)

    def half_shape(t):
        _, r, cc = arrs[t].shape
        return (r // 2, cc) if axes[t] == 1 else (r, cc // 2)

    def body(*refs):
        g = refs[n:2 * n]
        bufs = refs[2 * n:3 * n]
        fsend, frecv, lsem = refs[3 * n:]
        x, y, c, chips = _place()
        for t in range(n):
            pend = [None, None]
            for k, chip in enumerate(chips):
                jk = 2 * chip[0] + chip[1]
                slot = k % 2
                if pend[slot] is not None:
                    pend[slot].wait_send()
                part = _half_of(g[t], jk, c, axes[t])
                ld = pltpu.make_async_copy(part, bufs[t].at[slot], lsem.at[2 * t + slot])
                ld.start()
                ld.wait()
                cp = pltpu.make_async_remote_copy(
                    src_ref=bufs[t].at[slot], dst_ref=part, send_sem=fsend.at[2 * t + slot],
                    recv_sem=frecv.at[t], device_id=(x, y, 1 - c), device_id_type=MESH)
                cp.start()
                pend[slot] = cp
            for cp in pend:
                cp.wait_send()
        for t in range(n):
            hr, hc = half_shape(t)
            passed = g[t].at[pl.ds(0, 3), pl.ds(0, hr), pl.ds(0, hc)]
            pltpu.make_async_remote_copy(
                src_ref=passed, dst_ref=passed, send_sem=fsend.at[2 * t], recv_sem=frecv.at[t],
                device_id=(x, y, 1 - c), device_id_type=MESH).wait_recv()

    dma = pltpu.SemaphoreType.DMA
    scratch = [pltpu.VMEM((2,) + half_shape(t), BF16) for t in range(n)]
    scratch += [dma((2 * n,)), dma((n,)), dma((2 * n,))]
    return pl.pallas_call(
        body, name=name, in_specs=[ANY] * n, out_specs=[ANY] * n, out_shape=[_sds(a.shape, a.dtype) for a in arrs],
        scratch_shapes=scratch, input_output_aliases={t: t for t in range(n)},
        compiler_params=pltpu.CompilerParams(has_side_effects=True, vmem_limit_bytes=VMEM_LIMIT))(*arrs)


def _allgather_blocking(gs, smalls):
    nb, ns = len(gs), len(smalls)
    halves = [g.shape[2] // 2 for g in gs]

    def body(*refs):
        s_in = refs[nb:nb + ns]
        g = refs[nb + ns:2 * nb + ns]
        s_out = refs[2 * nb + ns:2 * (nb + ns)]
        scr = refs[2 * (nb + ns):]
        bufs = scr[:nb]
        send, recv, fsend, frecv, lsem, ssend, srecv, slsem = scr[nb:]
        x, y, c, chips = _place()
        j = 2 * x + y
        sends, slocal = [], []
        for t in range(nb):
            rows = pl.ds(c * halves[t], halves[t])
            piece = g[t].at[:, j, rows, :]
            for k, chip in enumerate(chips):
                cp = pltpu.make_async_remote_copy(
                    src_ref=piece, dst_ref=piece, send_sem=send.at[3 * t + k], recv_sem=recv.at[3 * t + k],
                    device_id=(*chip, c), device_id_type=MESH)
                cp.start()
                sends.append(cp)
        for t in range(ns):
            cp = pltpu.make_async_copy(s_in[t], s_out[t].at[j], slsem.at[t])
            cp.start()
            slocal.append(cp)
            for k, chip in enumerate(chips):
                cp = pltpu.make_async_remote_copy(
                    src_ref=s_in[t], dst_ref=s_out[t].at[j], send_sem=ssend.at[3 * t + k],
                    recv_sem=srecv.at[3 * t + k], device_id=(*chip, c), device_id_type=MESH)
                cp.start()
                sends.append(cp)
        for t in range(nb):
            rows = pl.ds(c * halves[t], halves[t])
            pend = [None, None]
            n = 0
            for k, chip in enumerate(chips):
                jk = 2 * chip[0] + chip[1]
                landed = g[t].at[:, jk, rows, :]
                pltpu.make_async_remote_copy(
                    src_ref=landed, dst_ref=landed, send_sem=send.at[3 * t + k], recv_sem=recv.at[3 * t + k],
                    device_id=(*chip, c), device_id_type=MESH).wait_recv()
                for l in range(DEPTH):
                    slot = n % 2
                    if pend[slot] is not None:
                        pend[slot].wait_send()
                    part = g[t].at[l, jk, rows, :]
                    ld = pltpu.make_async_copy(part, bufs[t].at[slot], lsem.at[2 * t + slot])
                    ld.start()
                    ld.wait()
                    cp = pltpu.make_async_remote_copy(
                        src_ref=bufs[t].at[slot], dst_ref=part, send_sem=fsend.at[2 * t + slot],
                        recv_sem=frecv.at[t], device_id=(x, y, 1 - c), device_id_type=MESH)
                    cp.start()
                    pend[slot] = cp
                    n += 1
            for cp in pend:
                cp.wait_send()
        for t in range(ns):
            for k, chip in enumerate(chips):
                jk = 2 * chip[0] + chip[1]
                landed = s_out[t].at[jk]
                pltpu.make_async_remote_copy(
                    src_ref=landed, dst_ref=landed, send_sem=ssend.at[3 * t + k], recv_sem=srecv.at[3 * t + k],
                    device_id=(*chip, c), device_id_type=MESH).wait_recv()
        for t in range(nb):
            passed = g[t].at[:, pl.ds(0, 3), pl.ds((1 - c) * halves[t], halves[t]), :]
            pltpu.make_async_remote_copy(
                src_ref=passed, dst_ref=passed, send_sem=fsend.at[2 * t], recv_sem=frecv.at[t],
                device_id=(x, y, 1 - c), device_id_type=MESH).wait_recv()
        for cp in sends:
            cp.wait_send()
        for cp in slocal:
            cp.wait()

    dma = pltpu.SemaphoreType.DMA
    out_shape = [_sds(g.shape, g.dtype) for g in gs] + [_sds((NCHIP,) + s.shape, s.dtype) for s in smalls]
    scratch = [pltpu.VMEM((2, halves[t], gs[t].shape[3]), BF16) for t in range(nb)]
    scratch += [dma((3 * nb,)), dma((3 * nb,)), dma((2 * nb,)), dma((nb,)), dma((2 * nb,)),
                dma((3 * ns,)), dma((3 * ns,)), dma((ns,))]
    return pl.pallas_call(
        body, name="allgather_weights", in_specs=[ANY] * (nb + ns), out_specs=[ANY] * (nb + ns),
        out_shape=out_shape, scratch_shapes=scratch, input_output_aliases={t: t for t in range(nb)},
        compiler_params=pltpu.CompilerParams(has_side_effects=True, vmem_limit_bytes=VMEM_LIMIT))(*gs, *smalls)


def _reduce_scatter(tag, dws, pf):
    ps = []
    for t, dw in enumerate(dws):
        got = _pair_send("rs_pair_%s_%d" % (tag, t), dw, pf)
        ps.append(_add2_bf16("rs_add2_%s_%d" % (tag, t), dw, got, pf))
    got2 = _rs_chips("rs_chips_" + tag, ps)
    return [_add4_join("rs_add4_%s_%d" % (tag, t), p, g2, pf) for t, (p, g2) in enumerate(zip(ps, got2))]


def _rs_begin(tag, dws, by_cols, pf):
    ps = []
    for t, dw in enumerate(dws):
        got = _pair_send("rs_pair_%s_%d" % (tag, t), dw, pf, by_cols[t])
        ps.append(_add2_bf16("rs_add2_%s_%d" % (tag, t), dw, got, pf, by_cols[t]))
    send, recv, ps_thru, lands, token = _rs_chips_start("rs_chips_start_" + tag, ps)
    return (tag, send, recv, ps_thru, lands, by_cols), token


def _rs_end(handle, afters, pf):
    tag, send, recv, ps, lands, by_cols = handle
    ps, got2 = _rs_chips_wait("rs_chips_wait_" + tag, send, recv, ps, lands, afters)
    return [_add4_join("rs_add4_%s_%d" % (tag, t), p, g2, pf, by_cols[t]) for t, (p, g2) in enumerate(zip(ps, got2))]


def _allreduce_small(pack):
    r = pack.shape[0]
    flips = [(fx, fy, fc) for fx in (0, 1) for fy in (0, 1) for fc in (0, 1)][1:]

    def body(p_ref, o_ref, gat, send, recv):
        x, y, c, _ = _place()
        me = 4 * x + 2 * y + c
        gat[me] = p_ref[...]
        cps = []
        for k, (fx, fy, fc) in enumerate(flips):
            peer = ((1 - x) if fx else x, (1 - y) if fy else y, (1 - c) if fc else c)
            cp = pltpu.make_async_remote_copy(
                src_ref=p_ref, dst_ref=gat.at[me], send_sem=send.at[k], recv_sem=recv.at[k],
                device_id=peer, device_id_type=MESH)
            cp.start()
            cps.append(cp)
        for k, (fx, fy, fc) in enumerate(flips):
            peer = ((1 - x) if fx else x, (1 - y) if fy else y, (1 - c) if fc else c)
            src = 4 * peer[0] + 2 * peer[1] + peer[2]
            pltpu.make_async_remote_copy(
                src_ref=p_ref, dst_ref=gat.at[src], send_sem=send.at[k], recv_sem=recv.at[k],
                device_id=peer, device_id_type=MESH).wait_recv()
        for cp in cps:
            cp.wait_send()
        s = gat[0]
        for d in range(1, 8):
            s = s + gat[d]
        o_ref[...] = s

    dma = pltpu.SemaphoreType.DMA
    vm = pl.BlockSpec(memory_space=pltpu.VMEM)
    return pl.pallas_call(
        body, name="allreduce_small", in_specs=[vm], out_specs=vm, out_shape=_sds((r, LANE), F32),
        scratch_shapes=[pltpu.VMEM((8, r, LANE), F32), dma((7,)), dma((7,))],
        compiler_params=pltpu.CompilerParams(has_side_effects=True))(pack)


def _in_weights(win_g):
    full = jnp.concatenate([win_g[s] for s in range(NCHIP)], axis=0)
    wqkv = full[:2048]
    og = full[2048:3072]
    gates = jnp.pad(full[3072:3080], ((0, LANE - 8), (0, 0)))
    u = full[3080:4104]
    gb = full[4104:5128]
    gc = full[5128:6152]
    return wqkv, jnp.stack([og, u, gb, gc]), gates


def _in_grads(dwqkv, dwe, dwgt):
    full = jnp.concatenate([dwqkv, dwe[0], dwgt[:8], dwe[1], dwe[2], dwe[3]], axis=0)
    sw = DIN // NCHIP
    return jnp.stack([full[s * sw:(s + 1) * sw] for s in range(NCHIP)])


def _layer_fwd(l, h, get_mix, get_ffn, small):
    lp = h.shape[0]
    th = lp // 2
    wqkv, we, wgt, wout_g = get_mix(h)
    nmw, bias, mnw, cw, nfw = small
    tag = "_l%d" % l
    hn = _norm_fwd("norm_mix" + tag, h, nmw)
    qkv = _mm("proj_qkv" + tag, "nt", hn, wqkv,
              pl.BlockSpec((lp, D), lambda i, j, k: (0, 0)), pl.BlockSpec((512, D), lambda i, j, k: (j, 0)),
              pl.BlockSpec((lp, 512), lambda i, j, k: (0, j)), _sds((lp, 2048), BF16), (1, 4, 1))
    e = _mm("proj_e" + tag, "nt", hn, we,
            pl.BlockSpec((lp, D), lambda i, j, k: (0, 0)), pl.BlockSpec((None, 512, D), lambda i, j, k: (j // 2, j % 2, 0)),
            pl.BlockSpec((None, lp, 512), lambda i, j, k: (j // 2, 0, j % 2)), _sds((4, lp, 1024), BF16), (1, 8, 1))
    gpre = _mm("proj_gates" + tag, "nt", hn, wgt,
               pl.BlockSpec((lp, D), lambda i, j, k: (0, 0)), pl.BlockSpec((LANE, D), lambda i, j, k: (0, 0)),
               pl.BlockSpec((lp, LANE), lambda i, j, k: (0, 0)), _sds((lp, LANE), F32), (1, 1, 1))
    grow, mcol, ccol = _gate_prep("gate_prep" + tag, gpre, bias)
    ht = _mlstm_fwd("mlstm_fwd" + tag, qkv, grow, mcol, ccol, 4)
    mix = _mix_fwd("mix_fwd" + tag, ht, e, mnw, cw)
    wout = wout_g.reshape(D, D)
    h1 = _mm("out_proj" + tag, "nn", mix, wout,
             pl.BlockSpec((None, th, 1024), lambda i, j, k: (k, i, 0)),
             pl.BlockSpec((1024, 1024), lambda i, j, k: (k, j)),
             pl.BlockSpec((th, 1024), lambda i, j, k: (i, j)), _sds((lp, D), F32), (2, 2, 2),
             acc_shape=(th, 1024), res=h, res_spec=pl.BlockSpec((th, 1024), lambda i, j, k: (i, j)))
    wg_g, wu_g, wd_g = get_ffn(h1)
    hf = _norm_fwd("norm_ffn" + tag, h1, nfw)
    g, u, a = _ffn_up("ffn_up" + tag, hf, wg_g, wu_g)
    tk = DFF // NCHIP
    h2 = _mm("ffn_down" + tag, "nn", a, wd_g,
             pl.BlockSpec((th, tk), lambda i, j, k: (i, k)),
             pl.BlockSpec((None, tk, 1024), lambda i, j, k: (k, 0, j)),
             pl.BlockSpec((th, 1024), lambda i, j, k: (i, j)), _sds((lp, D), F32), (2, 2, NCHIP),
             acc_shape=(th, 1024), res=h1, res_spec=pl.BlockSpec((th, 1024), lambda i, j, k: (i, j)))
    saved = (h, hn, qkv, e, gpre, grow, mcol, ccol, ht, mix, h1, hf, g, u, a)
    return h2, saved, (wqkv, we, wgt, wout_g, wg_g, wu_g, wd_g)


def _layer_bwd(l, dh2, dh2b, saved, wts, small, ffn_done):
    h, hn, qkv, e, gpre, grow, mcol, ccol, ht, mix, h1, hf, g, u, a = saved
    wqkv, we, wgt, wout_g, wg_g, wu_g, wd_g = wts
    nmw, bias, mnw, cw, nfw = small
    lp = h.shape[0]
    th = lp // 2
    tk = DFF // NCHIP
    tag = "_l%d" % l
    half_rows = lambda i, j, k: (i, j)

    dwd = _mm("dw_down" + tag, "tn", a, dh2b,
              pl.BlockSpec((lp, tk), lambda i, j, k: (0, i)), pl.BlockSpec((lp, 1024), lambda i, j, k: (0, j)),
              pl.BlockSpec((None, tk, 1024), lambda i, j, k: (i, 0, j)), _sds((NCHIP, tk, D), BF16), (NCHIP, 2, 1))
    dg, du = _ffn_bwd_act("ffn_bwd_act" + tag, dh2b, wd_g, g, u)
    dws = []
    for nm, dact in (("gate", dg), ("up", du)):
        dws.append(_mm("dw_%s%s" % (nm, tag), "tn", hf, dact,
                       pl.BlockSpec((lp, 1024), lambda i, j, k: (0, i)), pl.BlockSpec((lp, tk), lambda i, j, k: (0, j)),
                       pl.BlockSpec((None, 1024, tk), lambda i, j, k: (j, i, 0)), _sds((NCHIP, D, tk), BF16),
                       (2, NCHIP, 1)))
    dwg, dwu = dws
    dhf = None
    for nm, dact, wfull in (("gate", dg, wg_g), ("up", du, wu_g)):
        dhf = _mm("dhf_%s%s" % (nm, tag), "nt", dact, wfull,
                  pl.BlockSpec((th, tk), lambda i, j, k: (i, k)),
                  pl.BlockSpec((None, 1024, tk), lambda i, j, k: (k, j, 0)),
                  pl.BlockSpec((th, 1024), half_rows), _sds((lp, D), F32), (2, 2, NCHIP), acc_shape=(th, 1024),
                  res=dhf, res_spec=None if dhf is None else pl.BlockSpec((th, 1024), half_rows))
    dh1, dh1b, dnfw = _norm_bwd("norm_ffn_bwd" + tag, dhf, h1, nfw, dh2)
    dh1b = ffn_done([dwg, dwu, dwd], dh1b)

    dwout = _mm("dw_out" + tag, "tn", mix, dh1b,
                pl.BlockSpec((None, lp, 1024), lambda i, j, k: (i, 0, 0)), pl.BlockSpec((lp, 1024), lambda i, j, k: (0, j)),
                pl.BlockSpec((1024, 1024), half_rows), _sds((D, D), BF16), (2, 2, 1))
    wout = wout_g.reshape(D, D)
    dmix = _mm("dmix" + tag, "nt", dh1b, wout,
               pl.BlockSpec((th, D), lambda i, j, k: (i, 0)), pl.BlockSpec((1024, D), lambda i, j, k: (j, 0)),
               pl.BlockSpec((th, 1024), half_rows), _sds((lp, D), F32), (2, 2, 1))
    dht, de, dmnw, dcw = _mix_bwd("mix_bwd" + tag, dmix, ht, e, mnw, cw)
    dq, dk, dv, dgrow, dfx = _mlstm_bwd("mlstm_bwd" + tag, qkv, grow, mcol, ccol, ht, dht, 8)
    dgp, dgpb, dbias = _gate_bwd("gate_bwd" + tag, gpre, bias, dgrow, dfx)
    del dgp
    dqkv = jnp.concatenate([dq, dk, dv], axis=1)

    hn_cols = pl.BlockSpec((lp, 1024), lambda i, j, k: (0, j))
    dwqkv = _mm("dw_qkv" + tag, "tn", dqkv, hn, pl.BlockSpec((lp, 1024), lambda i, j, k: (0, i)), hn_cols,
                pl.BlockSpec((1024, 1024), half_rows), _sds((2048, D), BF16), (2, 2, 1))
    dwe = _mm("dw_e" + tag, "tn", de, hn, pl.BlockSpec((None, lp, 1024), lambda i, j, k: (i, 0, 0)), hn_cols,
              pl.BlockSpec((None, 1024, 1024), lambda i, j, k: (i, 0, j)), _sds((4, 1024, D), BF16), (4, 2, 1))
    dwgt = _mm("dw_gates" + tag, "tn", dgpb, hn, pl.BlockSpec((lp, LANE), lambda i, j, k: (0, 0)), hn_cols,
               pl.BlockSpec((LANE, 1024), lambda i, j, k: (0, j)), _sds((LANE, D), BF16), (1, 2, 1))
    dwin = _in_grads(dwqkv, dwe, dwgt)

    dhn = _mm("dhn_qkv" + tag, "nn", dqkv, wqkv,
              pl.BlockSpec((th, 2048), lambda i, j, k: (i, 0)), pl.BlockSpec((2048, 1024), lambda i, j, k: (0, j)),
              pl.BlockSpec((th, 1024), half_rows), _sds((lp, D), F32), (2, 2, 1))
    dhn = _mm("dhn_e" + tag, "nn", de, we,
              pl.BlockSpec((None, th, 1024), lambda i, j, k: (k, i, 0)),
              pl.BlockSpec((None, 1024, 1024), lambda i, j, k: (k, 0, j)),
              pl.BlockSpec((th, 1024), half_rows), _sds((lp, D), F32), (2, 2, 4), acc_shape=(th, 1024),
              res=dhn, res_spec=pl.BlockSpec((th, 1024), half_rows))
    dhn = _mm("dhn_gates" + tag, "nn", dgpb, wgt,
              pl.BlockSpec((th, LANE), lambda i, j, k: (i, 0)), pl.BlockSpec((LANE, 1024), lambda i, j, k: (0, j)),
              pl.BlockSpec((th, 1024), half_rows), _sds((lp, D), F32), (2, 2, 1),
              res=dhn, res_spec=pl.BlockSpec((th, 1024), half_rows))
    dh0, dh0b, dnmw = _norm_bwd("norm_mix_bwd" + tag, dhn, h, nmw, dh1)

    pieces = [dwin, dwout.reshape(NCHIP, D // NCHIP, D)]
    smalls = (dnmw[0], dbias[0, :8], dcw[:3], dmnw[0], dnfw[0])
    return dh0, dh0b, pieces, smalls


def _pack_rows(parts):
    rows = []
    for p in parts:
        f = p.reshape(-1)
        pad = (-f.shape[0]) % LANE
        if pad:
            f = jnp.pad(f, (0, pad))
        rows.append(f.reshape(-1, LANE))
    r = jnp.concatenate(rows, axis=0)
    pad = (-r.shape[0]) % 8
    if pad:
        r = jnp.pad(r, ((0, pad), (0, 0)))
    return r


def _unpack_rows(pack, shapes):
    out, r0 = [], 0
    for s in shapes:
        n = 1
        for d in s:
            n *= d
        nr = -(-n // LANE)
        out.append(pack[r0:r0 + nr].reshape(-1)[:n].reshape(s))
        r0 += nr
    return out


def kernel(x, meta_tokens, norm_mix_w, w_in, b_gates, conv_w, mlstm_norm_w, w_out, norm_ffn_w, w_gate, w_up, w_down, norm_final_w, loss_target, m_meta_tokens, m_norm_mix_w, m_w_in, m_b_gates, m_conv_w, m_mlstm_norm_w, m_w_out, m_norm_ffn_w, m_w_gate, m_w_up, m_w_down, m_norm_final_w, v_meta_tokens, v_norm_mix_w, v_w_in, v_b_gates, v_conv_w, v_mlstm_norm_w, v_w_out, v_norm_ffn_w, v_w_gate, v_w_up, v_w_down, v_norm_final_w):
    seq = x.shape[1]
    n_real = N_META + seq
    lp = -(-n_real // LANE) * LANE
    xi, yi, ci = lax.axis_index("x"), lax.axis_index("y"), lax.axis_index("c")
    jchip = 2 * xi + yi
    pf = jnp.stack([ci, jchip, 2 * (1 - xi) + yi, 2 * xi + (1 - yi), 2 * (1 - xi) + (1 - yi)]).astype(jnp.int32)

    big = {"w_in": w_in, "w_out": w_out, "w_gate": w_gate, "w_up": w_up, "w_down": w_down}
    cast = {n: [_cast_into("cast_%s_l%d" % (n, l), w, l, pf) for l in range(DEPTH)] for n, w in big.items()
            if n != "w_in"}
    in_t = lambda a: jnp.transpose(a, (2, 0, 1))
    cast["w_in"] = [_cast_into_t("cast_w_in_l%d" % l, in_t(w_in), l, pf) for l in range(DEPTH)]
    conv_flat = jnp.pad(conv_w.reshape(DEPTH * 3, CW // NCHIP), ((0, 8 - DEPTH * 3), (0, 0)))

    def own_slot(a):
        return lax.dynamic_update_slice(jnp.zeros((NCHIP,) + a.shape, a.dtype), a[None], (jchip, 0, 0))

    groups, splits = [], []
    for l in range(DEPTH):
        groups.append([cast["w_in"][l], cast["w_out"][l]] + ([own_slot(meta_tokens), own_slot(conv_flat)] if l == 0 else []))
        splits.append([2, 1] + ([0, 0] if l == 0 else []))
        groups.append([cast[n][l] for n in ("w_gate", "w_up", "w_down")])
        splits.append([1, 1, 1])
    sems, arrs = _ag_start("ag_start_0", groups[:1], splits[:1])
    sems_rest, arrs_rest = _ag_start("ag_start_1", groups[1:], splits[1:])
    sems, arrs = sems + sems_rest, arrs + arrs_rest

    def gathered(gi, afters):
        got = _ag_wait("ag_wait_%d" % gi, arrs[gi], splits[gi], sems[gi][0], sems[gi][1], afters)
        axes = [s for s in splits[gi] if s]
        return list(_ag_forward("ag_forward_%d" % gi, got[:len(axes)], axes)) + list(got[len(axes):])

    win0_g, wout0_g, meta_g, conv_g = gathered(0, [])
    meta_full = jnp.concatenate([meta_g[s] for s in range(NCHIP)], axis=1)
    conv_full = jnp.concatenate([conv_g[s][:DEPTH * 3] for s in range(NCHIP)], axis=1)
    conv_full = conv_full.reshape(DEPTH, 3, CW)

    bias_rows = jnp.pad(b_gates, ((0, 0), (0, LANE - 8)))
    smalls = []
    for l in range(DEPTH):
        smalls.append((norm_mix_w[l][None], bias_rows[l][None], mlstm_norm_w[l][None],
                       jnp.pad(conv_full[l], ((0, 5), (0, 0))), norm_ffn_w[l][None]))

    h = jnp.concatenate([meta_full, x[0], jnp.zeros((lp - n_real, D), F32)], axis=0)
    saved, wts = [], []
    for l in range(DEPTH):
        def get_mix(h_in, l=l):
            win_g, wout_g = (win0_g, wout0_g) if l == 0 else gathered(2 * l, [h_in])
            return _in_weights(win_g) + (wout_g,)

        def get_ffn(h1, l=l):
            return tuple(gathered(2 * l + 1, [h1]))

        h, sv, wt = _layer_fwd(l, h, get_mix, get_ffn, smalls[l])
        saved.append(sv)
        wts.append(wt)
    tgt = jnp.pad(loss_target[0], ((N_META, lp - n_real), (0, 0)))
    dh, dhb, dnorm_final, loss_part = _loss_head(h, tgt, norm_final_w[None], n_real)

    names = ["w_in", "w_out", "w_gate", "w_up", "w_down"]
    params = {"w_in": (w_in, m_w_in, v_w_in), "w_out": (w_out, m_w_out, v_w_out), "w_gate": (w_gate, m_w_gate, v_w_gate),
              "w_up": (w_up, m_w_up, v_w_up), "w_down": (w_down, m_w_down, v_w_down)}
    big_out = {n: None for n in names}
    small_grads = [None] * DEPTH
    g_in = [None] * DEPTH

    def finish(l, group, handle, afters):
        for n, (g_mine, g_theirs) in zip(group, _rs_end(handle, afters, pf)):
            if n == "w_in":
                g_in[l] = (g_mine, g_theirs)
                continue
            w, m, v = params[n]
            big_out[n] = _adamw_layer("adamw_%s_l%d" % (n, l), l, g_mine, g_theirs, w, m, v, big_out[n], pf)
        return [big_out[n][3] for n in group if n != "w_in"]

    groups = []
    token = None
    for l in reversed(range(DEPTH)):
        def ffn_done(pieces, dh1b, l=l):
            handle, tok = _rs_begin("l%df" % l, pieces, [False] * 3, pf)
            groups.append((l, names[2:], handle))
            return dh1b + tok[0, 0].astype(BF16)

        dh, dhb, pieces, small_grads[l] = _layer_bwd(l, dh, dhb, saved[l], wts[l], smalls[l], ffn_done)
        handle, token = _rs_begin("l%dm" % l, pieces, [True, False], pf)
        groups.append((l, names[:2], handle))
        if l > 0:
            dhb = dhb + token[0, 0].astype(BF16)
    afters = [token]
    for l, group, handle in groups:
        afters = finish(l, group, handle, afters)
    big_out["w_in"] = [jnp.transpose(a, (1, 2, 0))
                       for a in _adamw_t("adamw_w_in", g_in, in_t(w_in), in_t(m_w_in), in_t(v_w_in), pf)]

    dnmw = jnp.stack([small_grads[l][0] for l in range(DEPTH)])
    dbias = jnp.stack([small_grads[l][1] for l in range(DEPTH)])
    dconv = jnp.stack([small_grads[l][2] for l in range(DEPTH)])
    dmnw = jnp.stack([small_grads[l][3] for l in range(DEPTH)])
    dnfw = jnp.stack([small_grads[l][4] for l in range(DEPTH)])
    part_shapes = [(N_META, D), (DEPTH, D), (DEPTH, 8), (DEPTH, 3, CW), (DEPTH, MW), (DEPTH, D), (D,), (LANE,)]
    pack = _pack_rows([dh[:N_META], dnmw, dbias, dconv, dmnw, dnfw, dnorm_final[0], loss_part[0]])
    tot = _unpack_rows(_allreduce_small(pack), part_shapes)
    g_meta_full, g_nmw, g_bias, g_conv_full, g_mnw, g_nfw, g_final, loss_row = tot
    mcols = D // NCHIP
    ccols = CW // NCHIP
    g_meta = lax.dynamic_slice_in_dim(g_meta_full, jchip * mcols, mcols, axis=1)
    g_conv = lax.dynamic_slice_in_dim(g_conv_full, jchip * ccols, ccols, axis=2)
    sm_g = [g_meta, g_nmw, g_bias, g_conv, g_mnw, g_nfw, g_final]
    sm_w = [meta_tokens, norm_mix_w, b_gates, conv_w, mlstm_norm_w, norm_ffn_w, norm_final_w]
    sm_m = [m_meta_tokens, m_norm_mix_w, m_b_gates, m_conv_w, m_mlstm_norm_w, m_norm_ffn_w, m_norm_final_w]
    sm_v = [v_meta_tokens, v_norm_mix_w, v_b_gates, v_conv_w, v_mlstm_norm_w, v_norm_ffn_w, v_norm_final_w]
    sm_shapes = [w.shape for w in sm_w]
    d_p, m_p, v_p = _adamw_flat(_pack_rows(sm_g), _pack_rows(sm_w), _pack_rows(sm_m), _pack_rows(sm_v))
    sm_d = _unpack_rows(d_p, sm_shapes)
    sm_nm = _unpack_rows(m_p, sm_shapes)
    sm_nv = _unpack_rows(v_p, sm_shapes)

    loss = loss_row[0]
    grad_x = dh[N_META:n_real][None]

    def ordered(sm, which):
        bo = {n: big_out[n][which] for n in names}
        return [sm[0], sm[1], bo["w_in"], sm[2], sm[3], sm[4], bo["w_out"], sm[5], bo["w_gate"], bo["w_up"], bo["w_down"], sm[6]]

    return (loss, grad_x, *ordered(sm_g, 0), *ordered(sm_d, 1), *ordered(sm_nm, 2), *ordered(sm_nv, 3))
```

```python
import functools

import jax
import jax.numpy as jnp
from jax import lax
from jax.experimental import pallas as pl
from jax.experimental.pallas import tpu as pltpu

F32 = jnp.float32
BF16 = jnp.bfloat16

D = 2048
N_META = 16
HEADS = 4
DQK = 128
DV = 256
MW = HEADS * DV
CW = D - MW
QKW = HEADS * DQK
DFF = 5632
DIN = 6152
NCHIP = 4
DEPTH = 2
GATE_CAP = 15.0
EPS = 1e-6
QK_SCALE = DQK ** -0.5
LANE = 128
VMEM_LIMIT = 60 * 1024 * 1024

ADAM_LR = 0.001
ADAM_B1 = 0.9
ADAM_B2 = 0.999
ADAM_EPS = 1e-08
ADAM_WD = 0.01
ADAM_STEP = 10

MESH = pl.DeviceIdType.MESH
ANY = pl.BlockSpec(memory_space=pl.ANY)


def _cp(sem):
    return pltpu.CompilerParams(dimension_semantics=sem, vmem_limit_bytes=VMEM_LIMIT)


def _sds(shape, dtype):
    return jax.ShapeDtypeStruct(shape, dtype)


_DN = {"nn": (((1,), (0,)), ((), ())), "nt": (((1,), (1,)), ((), ())), "tn": (((0,), (0,)), ((), ()))}


def _mm(name, kind, a, b, a_spec, b_spec, o_spec, out_shape, grid, acc_shape=None, res=None, res_spec=None):
    nk = grid[2]
    has_res = res is not None

    def body(*refs):
        if has_res:
            a_ref, b_ref, r_ref, o_ref = refs[:4]
        else:
            a_ref, b_ref, o_ref = refs[:3]
            r_ref = None
        p = lax.dot_general(a_ref[...], b_ref[...], _DN[kind], preferred_element_type=F32)
        if nk == 1:
            if r_ref is not None:
                p = p + r_ref[...]
            o_ref[...] = p.astype(o_ref.dtype)
        else:
            acc = refs[-1]
            k = pl.program_id(2)

            @pl.when(k == 0)
            def _():
                acc[...] = p

            @pl.when(k > 0)
            def _():
                acc[...] += p

            @pl.when(k == nk - 1)
            def _():
                r = acc[...]
                if r_ref is not None:
                    r = r + r_ref[...]
                o_ref[...] = r.astype(o_ref.dtype)

    ins = [a, b] + ([res] if has_res else [])
    in_specs = [a_spec, b_spec] + ([res_spec] if has_res else [])
    scratch = [pltpu.VMEM(acc_shape, F32)] if nk > 1 else []
    return pl.pallas_call(
        body, name=name, grid=grid, in_specs=in_specs, out_specs=o_spec, out_shape=out_shape,
        scratch_shapes=scratch, compiler_params=_cp(("parallel", "parallel", "arbitrary")))(*ins)


def _norm_fwd(name, h, w):
    lp = h.shape[0]
    tm = lp // 4

    def body(h_ref, w_ref, o_ref):
        x = h_ref[...]
        r = lax.rsqrt(jnp.mean(x * x, axis=1, keepdims=True) + EPS)
        o_ref[...] = (x * r * w_ref[...]).astype(BF16)

    return pl.pallas_call(
        body, name=name, grid=(4,),
        in_specs=[pl.BlockSpec((tm, D), lambda i: (i, 0)), pl.BlockSpec((1, D), lambda i: (0, 0))],
        out_specs=pl.BlockSpec((tm, D), lambda i: (i, 0)), out_shape=_sds((lp, D), BF16),
        compiler_params=_cp(("parallel",)))(h, w)


def _norm_bwd(name, dy, h, w, dres):
    lp = h.shape[0]
    tm = lp // 8

    def body(dy_ref, h_ref, w_ref, dres_ref, dh_ref, dhb_ref, dw_ref):
        x = h_ref[...]
        r = lax.rsqrt(jnp.mean(x * x, axis=1, keepdims=True) + EPS)
        xh = x * r
        dy_v = dy_ref[...]
        dxh = dy_v * w_ref[...]
        dx = r * (dxh - xh * jnp.mean(dxh * xh, axis=1, keepdims=True))
        dh = dres_ref[...] + dx
        dh_ref[...] = dh
        dhb_ref[...] = dh.astype(BF16)

        @pl.when(pl.program_id(0) == 0)
        def _():
            dw_ref[...] = jnp.zeros_like(dw_ref)

        dw_ref[0:1, :] += jnp.sum(dy_v * xh, axis=0, keepdims=True)

    row = pl.BlockSpec((tm, D), lambda i: (i, 0))
    return pl.pallas_call(
        body, name=name, grid=(8,),
        in_specs=[row, row, pl.BlockSpec((1, D), lambda i: (0, 0)), row],
        out_specs=[row, row, pl.BlockSpec((8, D), lambda i: (0, 0))],
        out_shape=[_sds((lp, D), F32), _sds((lp, D), BF16), _sds((8, D), F32)],
        compiler_params=_cp(("arbitrary",)))(dy, h, w, dres)


def _loss_head(h, tgt, w, n_real):
    lp = h.shape[0]
    tm = lp // 8

    def body(h_ref, t_ref, w_ref, dh_ref, dhb_ref, dw_ref, loss_ref):
        i = pl.program_id(0)
        x = h_ref[...]
        r = lax.rsqrt(jnp.mean(x * x, axis=1, keepdims=True) + EPS)
        xh = x * r
        wv = w_ref[...]
        row = i * tm + lax.broadcasted_iota(jnp.int32, (tm, 1), 0)
        valid = jnp.logical_and(row >= N_META, row < n_real)
        err = jnp.where(valid, xh * wv - t_ref[...], 0.0)
        dy_v = err * (1.0 / D)
        dxh = dy_v * wv
        dx = r * (dxh - xh * jnp.mean(dxh * xh, axis=1, keepdims=True))
        dh_ref[...] = dx
        dhb_ref[...] = dx.astype(BF16)

        @pl.when(i == 0)
        def _():
            dw_ref[...] = jnp.zeros_like(dw_ref)
            loss_ref[...] = jnp.zeros_like(loss_ref)

        dw_ref[0:1, :] += jnp.sum(dy_v * xh, axis=0, keepdims=True)
        part = jnp.sum(jnp.sum(err * err, axis=1, keepdims=True), axis=0, keepdims=True) * (0.5 / D)
        loss_ref[...] += jnp.broadcast_to(part, loss_ref.shape)

    row = pl.BlockSpec((tm, D), lambda i: (i, 0))
    return pl.pallas_call(
        body, name="loss_head", grid=(8,),
        in_specs=[row, row, pl.BlockSpec((1, D), lambda i: (0, 0))],
        out_specs=[row, row, pl.BlockSpec((8, D), lambda i: (0, 0)), pl.BlockSpec((8, LANE), lambda i: (0, 0))],
        out_shape=[_sds((lp, D), F32), _sds((lp, D), BF16), _sds((8, D), F32), _sds((8, LANE), F32)],
        compiler_params=_cp(("arbitrary",)))(h, tgt, w)


def _ffn_up(name, hf, wg4, wu4):
    lp = hf.shape[0]
    tm = lp // 4
    tn = DFF // NCHIP

    def body(x_ref, wg_ref, wu_ref, g_ref, u_ref, a_ref):
        x = x_ref[...]
        g = jnp.dot(x, wg_ref[...], preferred_element_type=F32)
        u = jnp.dot(x, wu_ref[...], preferred_element_type=F32)
        g_ref[...] = g.astype(BF16)
        u_ref[...] = u.astype(BF16)
        a_ref[...] = (g * jax.nn.sigmoid(g) * u).astype(BF16)

    wspec = pl.BlockSpec((None, D, tn), lambda j, i: (j, 0, 0))
    ospec = pl.BlockSpec((tm, tn), lambda j, i: (i, j))
    o = _sds((lp, DFF), BF16)
    return pl.pallas_call(
        body, name=name, grid=(NCHIP, 4),
        in_specs=[pl.BlockSpec((tm, D), lambda j, i: (i, 0)), wspec, wspec],
        out_specs=[ospec, ospec, ospec], out_shape=[o, o, o],
        compiler_params=_cp(("parallel", "parallel")))(hf, wg4, wu4)


def _ffn_bwd_act(name, dhb, wd4, g, u):
    lp = dhb.shape[0]
    tm = lp // 4
    tn = DFF // NCHIP

    def body(d_ref, w_ref, g_ref, u_ref, dg_ref, du_ref):
        da = lax.dot_general(d_ref[...], w_ref[...], _DN["nt"], preferred_element_type=F32)
        gv = g_ref[...].astype(F32)
        uv = u_ref[...].astype(F32)
        sg = jax.nn.sigmoid(gv)
        dg_ref[...] = (da * uv * (sg * (1.0 + gv * (1.0 - sg)))).astype(BF16)
        du_ref[...] = (da * (gv * sg)).astype(BF16)

    ospec = pl.BlockSpec((tm, tn), lambda j, i: (i, j))
    o = _sds((lp, DFF), BF16)
    return pl.pallas_call(
        body, name=name, grid=(NCHIP, 4),
        in_specs=[pl.BlockSpec((tm, D), lambda j, i: (i, 0)),
                  pl.BlockSpec((None, tn, D), lambda j, i: (j, 0, 0)), ospec, ospec],
        out_specs=[ospec, ospec], out_shape=[o, o],
        compiler_params=_cp(("parallel", "parallel")))(dhb, wd4, g, u)


def _shift_rows(x, d, row):
    return jnp.where(row >= d, pltpu.roll(x, d, axis=0), 0.0)


def _scan_steps(lp):
    d = 1
    while d < lp:
        yield d
        d *= 2


def _gate_values(pre):
    t = GATE_CAP * jnp.tanh(pre * (1.0 / GATE_CAP))
    lf = jnp.minimum(t, 0.0) - jnp.log(1.0 + jnp.exp(-jnp.abs(t)))
    return t, lf


def _gate_prep(name, gates_pre, bias):
    lp = gates_pre.shape[0]

    def body(p_ref, b_ref, grow_ref, m_ref, c_ref):
        pre = p_ref[...] + b_ref[...]
        lane = lax.broadcasted_iota(jnp.int32, (lp, LANE), 1)
        row = lax.broadcasted_iota(jnp.int32, (lp, LANE), 0)
        t, lf = _gate_values(pre)
        f = jnp.where(jnp.logical_and(lane >= HEADS, lane < 2 * HEADS), lf, 0.0)
        for d in _scan_steps(lp):
            f = f + _shift_rows(f, d, row)
        fs = pltpu.roll(f, LANE - HEADS, axis=1)
        g = jnp.where(lane < HEADS, t - fs, 0.0)
        m = g
        for d in _scan_steps(lp):
            m = jnp.maximum(m, jnp.where(row >= d, pltpu.roll(m, d, axis=0), m))
        grow_ref[...] = g.T
        m_ref[...] = m
        c_ref[...] = jnp.where(lane < HEADS, -fs - m, 0.0)

    full = pl.BlockSpec((lp, LANE), lambda: (0, 0))
    return pl.pallas_call(
        body, name=name, in_specs=[full, pl.BlockSpec((1, LANE), lambda: (0, 0))],
        out_specs=[pl.BlockSpec((LANE, lp), lambda: (0, 0)), full, full],
        out_shape=[_sds((LANE, lp), F32), _sds((lp, LANE), F32), _sds((lp, LANE), F32)],
        compiler_params=pltpu.CompilerParams(vmem_limit_bytes=VMEM_LIMIT))(gates_pre, bias)


def _pick_lane(blk, h):
    lane = lax.broadcasted_iota(jnp.int32, blk.shape, 1)
    return jnp.sum(jnp.where(lane == h, blk, 0.0), axis=1, keepdims=True)


def _mlstm_weights(q, k, grow, mcol, i, bq, nk):
    s = lax.dot_general(q, k, _DN["nt"], preferred_element_type=F32) * QK_SCALE
    row = i * bq + lax.broadcasted_iota(jnp.int32, (bq, 1), 0)
    col = lax.broadcasted_iota(jnp.int32, (1, nk), 1)
    a = jnp.where(col <= row, jnp.exp(jnp.minimum(grow - mcol, 0.0)), 0.0)
    return s, a


def _per_query_tile(i, nq, bq, lp, compute):
    for ii in range(nq):
        nk = min(lp, -(-((ii + 1) * bq) // LANE) * LANE)
        pl.when(i == ii)(functools.partial(compute, nk))


def _mlstm_fwd(name, qkv, grow, mcol_all, ccol_all, nq):
    lp = qkv.shape[0]
    bq = lp // nq

    def body(q_ref, k_ref, v_ref, grow_ref, m_ref, c_ref, o_ref):
        h = pl.program_id(0)
        i = pl.program_id(1)
        mcol = _pick_lane(m_ref[...], h)
        ccol = _pick_lane(c_ref[...], h)

        def compute(nk):
            grow_h = grow_ref[pl.ds(h, 1), 0:nk]
            s, a = _mlstm_weights(q_ref[...], k_ref[0:nk, :], grow_h, mcol, i, bq, nk)
            p = a * s
            den = jnp.sum(p, axis=1, keepdims=True)
            num = jnp.dot(p.astype(BF16), v_ref[0:nk, :], preferred_element_type=F32)
            o_ref[...] = num / jnp.maximum(jnp.abs(den), jnp.exp(ccol))

        _per_query_tile(i, nq, bq, lp, compute)

    return pl.pallas_call(
        body, name=name, grid=(HEADS, nq),
        in_specs=[pl.BlockSpec((bq, DQK), lambda h, i: (i, h)),
                  pl.BlockSpec((lp, DQK), lambda h, i: (0, HEADS + h)),
                  pl.BlockSpec((lp, DV), lambda h, i: (0, HEADS + h)),
                  pl.BlockSpec((8, lp), lambda h, i: (0, 0)),
                  pl.BlockSpec((bq, LANE), lambda h, i: (i, 0)),
                  pl.BlockSpec((bq, LANE), lambda h, i: (i, 0))],
        out_specs=pl.BlockSpec((bq, DV), lambda h, i: (i, h)),
        out_shape=_sds((lp, MW), F32),
        compiler_params=_cp(("parallel", "parallel")))(qkv, qkv, qkv, grow, mcol_all, ccol_all)


def _mlstm_bwd(name, qkv, grow, mcol_all, ccol_all, ht, dht, nq):
    lp = qkv.shape[0]
    bq = lp // nq

    def body(q_ref, k_ref, v_ref, grow_ref, m_ref, c_ref, ht_ref, dht_ref,
             dq_ref, dk_ref, dv_ref, dgrow_ref, dfx_ref, dkt_acc, dvt_acc):
        h = pl.program_id(0)
        i = pl.program_id(1)

        @pl.when(jnp.logical_and(h == 0, i == 0))
        def _():
            dgrow_ref[...] = jnp.zeros_like(dgrow_ref)
            dfx_ref[...] = jnp.zeros_like(dfx_ref)

        @pl.when(i == 0)
        def _():
            dkt_acc[...] = jnp.zeros_like(dkt_acc)
            dvt_acc[...] = jnp.zeros_like(dvt_acc)

        mcol = _pick_lane(m_ref[...], h)
        ccol = _pick_lane(c_ref[...], h)

        def compute(nk):
            q = q_ref[...]
            k = k_ref[0:nk, :]
            v = v_ref[0:nk, :]
            grow_h = grow_ref[pl.ds(h, 1), 0:nk]
            s, a = _mlstm_weights(q, k, grow_h, mcol, i, bq, nk)
            p = a * s
            den = jnp.sum(p, axis=1, keepdims=True)
            clamp = jnp.exp(ccol)
            active = jnp.abs(den) < clamp
            dd = jnp.maximum(jnp.abs(den), clamp)
            dht_v = dht_ref[...]
            hdh = jnp.sum(dht_v * ht_ref[...], axis=1, keepdims=True)
            dn = (dht_v / dd).astype(BF16)
            dden = jnp.where(active, 0.0, -(hdh / dd) * jnp.sign(den))
            dp = lax.dot_general(dn, v, _DN["nt"], preferred_element_type=F32) + dden
            rmat = dp * p
            dgrow_ref[pl.ds(h, 1), 0:nk] += jnp.sum(rmat, axis=0, keepdims=True)
            ds = (dp * a * QK_SCALE).astype(BF16)
            dq_ref[...] = jnp.dot(ds, k, preferred_element_type=F32).astype(BF16)
            dkt_acc[:, 0:nk] += lax.dot_general(q, ds, _DN["tn"], preferred_element_type=F32)
            dvt_acc[:, 0:nk] += lax.dot_general(dn, p.astype(BF16), _DN["tn"], preferred_element_type=F32)
            lane = lax.broadcasted_iota(jnp.int32, (bq, LANE), 1)
            r0 = pl.multiple_of(i * bq, 16)
            dfx_ref[pl.ds(r0, bq), :] += jnp.where(lane == h, jnp.sum(rmat, axis=1, keepdims=True), 0.0)

        _per_query_tile(i, nq, bq, lp, compute)

        @pl.when(i == nq - 1)
        def _():
            dk_ref[...] = dkt_acc[...].T.astype(BF16)
            dv_ref[...] = dvt_acc[...].T.astype(BF16)

    return pl.pallas_call(
        body, name=name, grid=(HEADS, nq),
        in_specs=[pl.BlockSpec((bq, DQK), lambda h, i: (i, h)),
                  pl.BlockSpec((lp, DQK), lambda h, i: (0, HEADS + h)),
                  pl.BlockSpec((lp, DV), lambda h, i: (0, HEADS + h)),
                  pl.BlockSpec((8, lp), lambda h, i: (0, 0)),
                  pl.BlockSpec((bq, LANE), lambda h, i: (i, 0)),
                  pl.BlockSpec((bq, LANE), lambda h, i: (i, 0)),
                  pl.BlockSpec((bq, DV), lambda h, i: (i, h)),
                  pl.BlockSpec((bq, DV), lambda h, i: (i, h))],
        out_specs=[pl.BlockSpec((bq, DQK), lambda h, i: (i, h)),
                   pl.BlockSpec((lp, DQK), lambda h, i: (0, h)),
                   pl.BlockSpec((lp, DV), lambda h, i: (0, h)),
                   pl.BlockSpec((LANE, lp), lambda h, i: (0, 0)),
                   pl.BlockSpec((lp, LANE), lambda h, i: (0, 0))],
        out_shape=[_sds((lp, QKW), BF16), _sds((lp, QKW), BF16), _sds((lp, MW), BF16),
                   _sds((LANE, lp), F32), _sds((lp, LANE), F32)],
        scratch_shapes=[pltpu.VMEM((DQK, lp), F32), pltpu.VMEM((DV, lp), F32)],
        compiler_params=_cp(("arbitrary", "arbitrary")))(qkv, qkv, qkv, grow, mcol_all, ccol_all, ht, dht)


def _gate_bwd(name, gates_pre, bias, dgrow, dfx):
    lp = gates_pre.shape[0]

    def body(p_ref, b_ref, dgrow_ref, dfx_ref, dg_ref, dgb_ref, db_ref):
        pre = p_ref[...] + b_ref[...]
        lane = lax.broadcasted_iota(jnp.int32, (lp, LANE), 1)
        row = lax.broadcasted_iota(jnp.int32, (lp, LANE), 0)
        th = jnp.tanh(pre * (1.0 / GATE_CAP))
        t = GATE_CAP * th
        dgc = jnp.where(lane < HEADS, dgrow_ref[...].T, 0.0)
        df = jnp.where(lane < HEADS, dfx_ref[...] - dgc, 0.0)
        for d in _scan_steps(lp):
            df = df + jnp.where(row < lp - d, pltpu.roll(df, lp - d, axis=0), 0.0)
        dlf = pltpu.roll(df, HEADS, axis=1)
        dt = jnp.where(lane < HEADS, dgc, dlf * jax.nn.sigmoid(-t))
        dpre = jnp.where(lane < 2 * HEADS, dt * (1.0 - th * th), 0.0)
        dg_ref[...] = dpre
        dgb_ref[...] = dpre.astype(BF16)
        db_ref[...] = jnp.broadcast_to(jnp.sum(dpre, axis=0, keepdims=True), db_ref.shape)

    full = pl.BlockSpec((lp, LANE), lambda: (0, 0))
    return pl.pallas_call(
        body, name=name,
        in_specs=[full, pl.BlockSpec((1, LANE), lambda: (0, 0)), pl.BlockSpec((LANE, lp), lambda: (0, 0)), full],
        out_specs=[full, full, pl.BlockSpec((8, LANE), lambda: (0, 0))],
        out_shape=[_sds((lp, LANE), F32), _sds((lp, LANE), BF16), _sds((8, LANE), F32)],
        compiler_params=pltpu.CompilerParams(vmem_limit_bytes=VMEM_LIMIT))(gates_pre, bias, dgrow, dfx)


CB = 256


def _e_specs(lp):
    return [pl.BlockSpec((None, lp, CB), functools.partial(lambda c, j: (c, 0, j), c)) for c in range(4)]


def _mix_fwd(name, ht, e, mnw, cw):
    lp = ht.shape[0]

    def body(ht_ref, og_ref, u_ref, gb_ref, gc_ref, mnw_ref, cw_ref, o_ref):
        x = ht_ref[...]
        r = lax.rsqrt(jnp.mean(x * x, axis=1, keepdims=True) + EPS)
        o_ref[0] = (jax.nn.sigmoid(og_ref[...].astype(F32)) * (x * r * mnw_ref[...])).astype(BF16)
        row = lax.broadcasted_iota(jnp.int32, (lp, CB), 0)
        a = gc_ref[...].astype(F32) * u_ref[...].astype(F32)
        conv = cw_ref[2:3, :] * a + cw_ref[1:2, :] * _shift_rows(a, 1, row) + cw_ref[0:1, :] * _shift_rows(a, 2, row)
        o_ref[1] = (gb_ref[...].astype(F32) * conv).astype(BF16)

    col = pl.BlockSpec((lp, CB), lambda j: (0, j))
    return pl.pallas_call(
        body, name=name, grid=(4,),
        in_specs=[col] + _e_specs(lp) + [pl.BlockSpec((1, CB), lambda j: (0, j)), pl.BlockSpec((8, CB), lambda j: (0, j))],
        out_specs=pl.BlockSpec((2, lp, CB), lambda j: (0, 0, j)), out_shape=_sds((2, lp, MW), BF16),
        compiler_params=_cp(("parallel",)))(ht, e, e, e, e, mnw, cw)


def _mix_bwd(name, dmix, ht, e, mnw, cw):
    lp = ht.shape[0]

    def body(dhm_ref, dhc_ref, ht_ref, og_ref, u_ref, gb_ref, gc_ref, mnw_ref, cw_ref,
             dht_ref, de_ref, dmnw_ref, dcw_ref):
        x = ht_ref[...]
        r = lax.rsqrt(jnp.mean(x * x, axis=1, keepdims=True) + EPS)
        xh = x * r
        w = mnw_ref[...]
        sg = jax.nn.sigmoid(og_ref[...].astype(F32))
        dhm = dhm_ref[...]
        de_ref[0] = (dhm * (xh * w) * (sg * (1.0 - sg))).astype(BF16)
        dn = dhm * sg
        dmnw_ref[...] = jnp.broadcast_to(jnp.sum(dn * xh, axis=0, keepdims=True), dmnw_ref.shape)
        dxh = dn * w
        dht_ref[...] = r * (dxh - xh * jnp.mean(dxh * xh, axis=1, keepdims=True))

        row = lax.broadcasted_iota(jnp.int32, (lp, CB), 0)
        uv = u_ref[...].astype(F32)
        gcv = gc_ref[...].astype(F32)
        gbv = gb_ref[...].astype(F32)
        a = gcv * uv
        a1 = _shift_rows(a, 1, row)
        a2 = _shift_rows(a, 2, row)
        dhc = dhc_ref[...]
        conv = cw_ref[2:3, :] * a + cw_ref[1:2, :] * a1 + cw_ref[0:1, :] * a2
        de_ref[2] = (dhc * conv).astype(BF16)
        dconv = dhc * gbv
        dcw_ref[...] = jnp.zeros_like(dcw_ref)
        dcw_ref[0:1, :] = jnp.sum(dconv * a2, axis=0, keepdims=True)
        dcw_ref[1:2, :] = jnp.sum(dconv * a1, axis=0, keepdims=True)
        dcw_ref[2:3, :] = jnp.sum(dconv * a, axis=0, keepdims=True)
        up1 = jnp.where(row < lp - 1, pltpu.roll(dconv, lp - 1, axis=0), 0.0)
        up2 = jnp.where(row < lp - 2, pltpu.roll(dconv, lp - 2, axis=0), 0.0)
        da = cw_ref[2:3, :] * dconv + cw_ref[1:2, :] * up1 + cw_ref[0:1, :] * up2
        de_ref[1] = (da * gcv).astype(BF16)
        de_ref[3] = (da * uv).astype(BF16)

    col = pl.BlockSpec((lp, CB), lambda j: (0, j))
    small = pl.BlockSpec((8, CB), lambda j: (0, j))
    return pl.pallas_call(
        body, name=name, grid=(4,),
        in_specs=[col, pl.BlockSpec((lp, CB), lambda j: (0, 4 + j)), col] + _e_specs(lp)
                 + [pl.BlockSpec((1, CB), lambda j: (0, j)), small],
        out_specs=[col, pl.BlockSpec((4, lp, CB), lambda j: (0, 0, j)), small, small],
        out_shape=[_sds((lp, MW), F32), _sds((4, lp, MW), BF16), _sds((8, MW), F32), _sds((8, CW), F32)],
        compiler_params=_cp(("parallel",)))(dmix, dmix, ht, e, e, e, e, mnw, cw)


def _row_tile(r, c, itemsize, budget=1536 * 1024, mult=16):
    best = None
    for t in range(mult, r + 1, mult):
        if r % t == 0 and t * c * itemsize <= budget:
            best = t
    if best is None:
        best = r
    return best


def _grid_spec(grid, in_specs, out_specs, scratch=()):
    return pltpu.PrefetchScalarGridSpec(num_scalar_prefetch=1, grid=grid, in_specs=in_specs,
                                        out_specs=out_specs, scratch_shapes=list(scratch))


def _cast_into(name, w, layer, pf):
    _, r, c = w.shape
    tr = _row_tile(r, c, 4)

    def body(pf_ref, x_ref, o_ref):
        o_ref[...] = x_ref[...].astype(BF16)

    return pl.pallas_call(
        body, name=name, out_shape=_sds((NCHIP, r, c), BF16),
        grid_spec=_grid_spec((r // tr,), [pl.BlockSpec((None, tr, c), lambda i, pf: (layer, i, 0))],
                             pl.BlockSpec((None, tr, c), lambda i, pf: (pf[1], i, 0))),
        compiler_params=_cp(("parallel",)))(pf, w)


TCOL = 256


def _cast_into_t(name, w_t, layer, pf):
    c, nl, r = w_t.shape

    def body(pf_ref, x_ref, o_ref):
        o_ref[...] = x_ref[:, layer, :].astype(BF16)

    return pl.pallas_call(
        body, name=name, out_shape=_sds((NCHIP, c, r), BF16),
        grid_spec=_grid_spec((r // TCOL,), [pl.BlockSpec((c, nl, TCOL), lambda i, pf: (0, 0, i))],
                             pl.BlockSpec((None, c, TCOL), lambda i, pf: (pf[1], 0, i))),
        compiler_params=_cp(("parallel",)))(pf, w_t)


def _add2_bf16(name, dw, got, pf, by_cols=False):
    n4, r2, c2 = got.shape

    def body(pf_ref, a_ref, b_ref, o_ref):
        o_ref[...] = (a_ref[...].astype(F32) + b_ref[...].astype(F32)).astype(BF16)

    if by_cols:
        nch = c2 // TCOL
        spec = pl.BlockSpec((None, r2, TCOL), lambda s, i, pf: (s, 0, i))
        mine = pl.BlockSpec((None, r2, TCOL), lambda s, i, pf: (s, 0, pf[0] * nch + i))
    else:
        tr = _row_tile(r2, c2, 4)
        nch = r2 // tr
        spec = pl.BlockSpec((None, tr, c2), lambda s, i, pf: (s, i, 0))
        mine = pl.BlockSpec((None, tr, c2), lambda s, i, pf: (s, pf[0] * nch + i, 0))
    return pl.pallas_call(
        body, name=name, out_shape=_sds((n4, r2, c2), BF16),
        grid_spec=_grid_spec((n4, nch), [mine, spec], spec),
        compiler_params=_cp(("parallel", "parallel")))(pf, dw, got)


def _adam_math(w, g, m, v):
    m2 = ADAM_B1 * m + (1.0 - ADAM_B1) * g
    v2 = ADAM_B2 * v + (1.0 - ADAM_B2) * (g * g)
    m_hat = m2 / (1.0 - ADAM_B1 ** ADAM_STEP)
    v_hat = v2 / (1.0 - ADAM_B2 ** ADAM_STEP)
    delta = -ADAM_LR * (m_hat / (jnp.sqrt(v_hat) + ADAM_EPS) + ADAM_WD * w)
    return delta, m2, v2


def _adamw_layer(name, layer, g_mine, g_theirs, w, m, v, prev, pf):
    _, r, c = w.shape
    r2 = r // 2
    tr = _row_tile(r2, c, 4, budget=1024 * 1024, mult=8)
    nch = r2 // tr
    n_alias = 0 if prev is None else 4

    def body(*refs):
        pf_ref, gm_ref, gt_ref, w_ref, m_ref, v_ref = refs[:6]
        go_ref, d_ref, mo_ref, vo_ref = refs[6 + n_alias:]
        mine = (pl.program_id(0) // nch) == pf_ref[0]
        gv = jnp.where(mine, gm_ref[...], gt_ref[...])
        delta, m2, v2 = _adam_math(w_ref[...], gv, m_ref[...], v_ref[...])
        go_ref[...] = gv
        d_ref[...] = delta
        mo_ref[...] = m2
        vo_ref[...] = v2

    slab = pl.BlockSpec((None, tr, c), lambda i, pf: (layer, i, 0))
    ins = [g_mine, g_theirs, w, m, v] + (list(prev) if prev is not None else [])
    in_specs = [pl.BlockSpec((tr, c), lambda i, pf: (jnp.clip(i - pf[0] * nch, 0, nch - 1), 0)),
                pl.BlockSpec((tr, c), lambda i, pf: (jnp.clip(i - (1 - pf[0]) * nch, 0, nch - 1), 0)),
                slab, slab, slab] + [ANY] * n_alias
    o = _sds(w.shape, F32)
    return pl.pallas_call(
        body, name=name, out_shape=[o] * 4, grid_spec=_grid_spec((2 * nch,), in_specs, [slab] * 4),
        input_output_aliases={6 + k: k for k in range(n_alias)},
        compiler_params=_cp(("parallel",)))(pf, *ins)


def _adamw_t(name, gs, w_t, m_t, v_t, pf):
    c, nl, r = w_t.shape
    ta = LANE
    nch = (r // 2) // ta

    def body(*refs):
        pf_ref = refs[0]
        g_refs = refs[1:1 + 2 * nl]
        w_ref, m_ref, v_ref, go_ref, d_ref, mo_ref, vo_ref = refs[1 + 2 * nl:]
        mine = (pl.program_id(0) // nch) == pf_ref[0]
        gv = jnp.stack([jnp.where(mine, g_refs[2 * l][...], g_refs[2 * l + 1][...]) for l in range(nl)], axis=1)
        delta, m2, v2 = _adam_math(w_ref[...], gv, m_ref[...], v_ref[...])
        go_ref[...] = gv
        d_ref[...] = delta
        mo_ref[...] = m2
        vo_ref[...] = v2

    both = pl.BlockSpec((c, nl, ta), lambda i, pf: (0, 0, i))
    g_specs = []
    for l in range(nl):
        g_specs += [pl.BlockSpec((c, ta), lambda i, pf: (0, jnp.clip(i - pf[0] * nch, 0, nch - 1))),
                    pl.BlockSpec((c, ta), lambda i, pf: (0, jnp.clip(i - (1 - pf[0]) * nch, 0, nch - 1)))]
    o = _sds(w_t.shape, F32)
    flat_g = [a for pair in gs for a in pair]
    return pl.pallas_call(
        body, name=name, out_shape=[o] * 4, grid_spec=_grid_spec((2 * nch,), g_specs + [both] * 3, [both] * 4),
        compiler_params=_cp(("parallel",)))(pf, *flat_g, w_t, m_t, v_t)


def _adamw_flat(g, w, m, v):
    def body(g_ref, w_ref, m_ref, v_ref, d_ref, mo_ref, vo_ref):
        delta, m2, v2 = _adam_math(w_ref[...], g_ref[...], m_ref[...], v_ref[...])
        d_ref[...] = delta
        mo_ref[...] = m2
        vo_ref[...] = v2

    o = _sds(w.shape, F32)
    return pl.pallas_call(body, name="adamw_small", out_shape=[o, o, o])(g, w, m, v)


def _place():
    x, y, c = lax.axis_index("x"), lax.axis_index("y"), lax.axis_index("c")
    chips = [(1 - x, y), (x, 1 - y), (1 - x, 1 - y)]
    return x, y, c, chips


def _rs_chips(name, ps):
    n = len(ps)

    def body(*refs):
        ins = refs[:n]
        got = refs[n:2 * n]
        send, recv = refs[2 * n:]
        x, y, c, chips = _place()
        cps = []
        for t in range(n):
            for k, chip in enumerate(chips):
                jk = 2 * chip[0] + chip[1]
                cp = pltpu.make_async_remote_copy(
                    src_ref=ins[t].at[jk], dst_ref=got[t].at[k],
                    send_sem=send.at[3 * t + k], recv_sem=recv.at[3 * t + k],
                    device_id=(*chip, c), device_id_type=MESH)
                cp.start()
                cps.append(cp)
        for cp in cps:
            cp.wait()

    dma = pltpu.SemaphoreType.DMA
    return pl.pallas_call(
        body, name=name, in_specs=[ANY] * n, out_specs=[ANY] * n,
        out_shape=[_sds((3,) + p.shape[1:], BF16) for p in ps],
        scratch_shapes=[dma((3 * n,)), dma((3 * n,))],
        compiler_params=pltpu.CompilerParams(has_side_effects=True))(*ps)


HBM = pl.BlockSpec(memory_space=pltpu.HBM)
SEM = pl.BlockSpec(memory_space=pltpu.SEMAPHORE)
EFFECT = pltpu.SideEffectType.DATAFLOW_SIDE_EFFECTING


def _in_hbm(a):
    return pltpu.with_memory_space_constraint(a, pltpu.HBM)


def _rs_chips_copies(ins, lands, send, recv):
    x, y, c, chips = _place()
    cps = []
    for t in range(len(ins)):
        for k, chip in enumerate(chips):
            jk = 2 * chip[0] + chip[1]
            cps.append(pltpu.make_async_remote_copy(
                src_ref=ins[t].at[jk], dst_ref=lands[t].at[k], send_sem=send.at[3 * t + k],
                recv_sem=recv.at[3 * t + k], device_id=(*chip, c), device_id_type=MESH))
    return cps


def _rs_chips_start(name, ps):
    n = len(ps)

    def body(*refs):
        ins, lands = refs[:n], refs[n:2 * n]
        send, recv = refs[2 * n], refs[2 * n + 1]
        token = refs[-1]
        for cp in _rs_chips_copies(ins, lands, send, recv):
            cp.start()
        token[...] = jnp.zeros_like(token)

    dma = pltpu.SemaphoreType.DMA
    lands = [lax.empty((3,) + p.shape[1:], BF16) for p in ps]
    out_shape = ([dma((3 * n,)), dma((3 * n,))] + [pltpu.HBM(p.shape, BF16) for p in ps]
                 + [pltpu.HBM(z.shape, BF16) for z in lands] + [_sds((8, LANE), F32)])
    outs = pl.pallas_call(
        body, name=name, out_shape=out_shape, in_specs=[HBM] * (2 * n),
        out_specs=[SEM, SEM] + [HBM] * (2 * n) + [pl.BlockSpec(memory_space=pltpu.VMEM)],
        input_output_aliases={i: 2 + i for i in range(2 * n)},
        compiler_params=pltpu.CompilerParams(has_side_effects=EFFECT))(
            *[_in_hbm(p) for p in ps], *[_in_hbm(z) for z in lands])
    return outs[0], outs[1], outs[2:2 + n], outs[2 + n:2 + 2 * n], outs[-1]


def _rs_chips_wait(name, send, recv, ps, lands, afters):
    n = len(ps)

    def body(*refs):
        ins, zones = refs[:n], refs[n:2 * n]
        send_ref, recv_ref = refs[2 * n], refs[2 * n + 1]
        for cp in _rs_chips_copies(ins, zones, send_ref, recv_ref):
            cp.wait_send()
            cp.wait_recv()

    outs = pl.pallas_call(
        body, name=name, out_shape=[pltpu.HBM(p.shape, BF16) for p in ps] + [pltpu.HBM(z.shape, BF16) for z in lands],
        in_specs=[HBM] * (2 * n) + [SEM, SEM] + [ANY] * len(afters), out_specs=[HBM] * (2 * n),
        input_output_aliases={i: i for i in range(2 * n)},
        compiler_params=pltpu.CompilerParams(has_side_effects=EFFECT))(*ps, *lands, send, recv, *afters)
    return outs[:n], outs[n:]


def _sibling():
    x, y, c, _ = _place()
    return (x, y, 1 - c)


def _pair_send(name, dw, pf, by_cols=False):
    n4, r, c = dw.shape
    blk = (1, r, c // 2) if by_cols else (1, r // 2, c)
    idx = (lambda s, pf: (s, 0, 1 - pf[0])) if by_cols else (lambda s, pf: (s, 1 - pf[0], 0))

    def body(pf_ref, x_ref, got_ref, ssem, rsem):
        s = pl.program_id(0)
        cp = pltpu.make_async_remote_copy(src_ref=x_ref, dst_ref=got_ref.at[pl.ds(s, 1)], send_sem=ssem,
                                          recv_sem=rsem, device_id=_sibling(), device_id_type=MESH)
        cp.start()
        cp.wait_send()

        @pl.when(s == n4 - 1)
        def _():
            pltpu.make_async_remote_copy(src_ref=got_ref, dst_ref=got_ref, send_sem=ssem, recv_sem=rsem,
                                         device_id=_sibling(), device_id_type=MESH).wait_recv()

    dma = pltpu.SemaphoreType.DMA
    return pl.pallas_call(
        body, name=name, out_shape=_sds((n4,) + blk[1:], BF16),
        grid_spec=_grid_spec((n4,), [pl.BlockSpec(blk, idx)], ANY, scratch=[dma(()), dma(())]),
        compiler_params=pltpu.CompilerParams(dimension_semantics=("arbitrary",), has_side_effects=True,
                                             vmem_limit_bytes=VMEM_LIMIT))(pf, dw)


def _add4_join(name, p, got, pf, by_cols=False):
    n4, r2, c = p.shape
    if by_cols:
        tr, nch = r2, c // TCOL
        blk, idx = (r2, TCOL), (lambda i: (0, i))
    else:
        tr = _row_tile(r2, c, 4)
        nch = r2 // tr
        blk, idx = (tr, c), (lambda i: (i, 0))

    def body(pf_ref, p_ref, g_ref, mine_ref, theirs_ref, ssem, rsem):
        i = pl.program_id(0)
        s = p_ref[...].astype(F32)
        for k in range(3):
            s = s + g_ref[k].astype(F32)
        mine_ref[...] = s
        if by_cols:
            dst = theirs_ref.at[:, pl.ds(pl.multiple_of(i * TCOL, LANE), TCOL)]
        else:
            dst = theirs_ref.at[pl.ds(pl.multiple_of(i * tr, 8), tr), :]
        cp = pltpu.make_async_remote_copy(src_ref=mine_ref, dst_ref=dst,
                                          send_sem=ssem, recv_sem=rsem, device_id=_sibling(), device_id_type=MESH)
        cp.start()
        cp.wait_send()

        @pl.when(i == nch - 1)
        def _():
            pltpu.make_async_remote_copy(src_ref=theirs_ref, dst_ref=theirs_ref, send_sem=ssem, recv_sem=rsem,
                                         device_id=_sibling(), device_id_type=MESH).wait_recv()

    dma = pltpu.SemaphoreType.DMA
    o = _sds((r2, c), F32)
    return pl.pallas_call(
        body, name=name, out_shape=[o, o],
        grid_spec=_grid_spec((nch,), [pl.BlockSpec((None,) + blk, lambda i, pf: (pf[1],) + idx(i)),
                                      pl.BlockSpec((3,) + blk, lambda i, pf: (0,) + idx(i))],
                             [pl.BlockSpec(blk, lambda i, pf: idx(i)), ANY], scratch=[dma(()), dma(())]),
        compiler_params=pltpu.CompilerParams(dimension_semantics=("arbitrary",), has_side_effects=True,
                                             vmem_limit_bytes=VMEM_LIMIT))(pf, p, got)


def _half_of(g, slot, which, axis):
    half = g.shape[axis] // 2
    if axis == 1:
        return g.at[slot, pl.ds(which * half, half), :]
    return g.at[slot, :, pl.ds(which * half, half)]


def _ag_copies(arrs, split, send, recv):
    x, y, c, chips = _place()
    j = 2 * x + y
    cps = []
    for t, g in enumerate(arrs):
        piece = _half_of(g, j, c, split[t]) if split[t] else g.at[j]
        for k, chip in enumerate(chips):
            cps.append(pltpu.make_async_remote_copy(
                src_ref=piece, dst_ref=piece, send_sem=send.at[3 * t + k], recv_sem=recv.at[3 * t + k],
                device_id=(*chip, c), device_id_type=MESH))
    return cps


def _ag_start(name, groups, splits):
    sizes = [len(g) for g in groups]
    flat = [a for g in groups for a in g]
    n = len(flat)

    def body(*refs):
        ins = refs[:n]
        sems = refs[n:n + 2 * len(groups)]
        o = 0
        for gi, sz in enumerate(sizes):
            for cp in _ag_copies(ins[o:o + sz], splits[gi], sems[2 * gi], sems[2 * gi + 1]):
                cp.start()
            o += sz
        refs[-1][...] = jnp.zeros_like(refs[-1])

    dma = pltpu.SemaphoreType.DMA
    sem_shapes = [dma((3 * sz,)) for sz in sizes for _ in range(2)]
    outs = pl.pallas_call(
        body, name=name, out_shape=sem_shapes + [pltpu.HBM(a.shape, a.dtype) for a in flat] + [_sds((8, LANE), F32)],
        in_specs=[HBM] * n,
        out_specs=[SEM] * len(sem_shapes) + [HBM] * n + [pl.BlockSpec(memory_space=pltpu.VMEM)],
        input_output_aliases={i: len(sem_shapes) + i for i in range(n)},
        compiler_params=pltpu.CompilerParams(has_side_effects=EFFECT))(*[_in_hbm(a) for a in flat])
    sems, arrs, o = [], [], len(sem_shapes)
    for gi, sz in enumerate(sizes):
        sems.append((outs[2 * gi], outs[2 * gi + 1]))
        arrs.append(list(outs[o:o + sz]))
        o += sz
    return sems, arrs, outs[-1]


def _ag_wait(name, arrs, split, send, recv, afters):
    n = len(arrs)

    def body(*refs):
        for cp in _ag_copies(refs[:n], split, refs[n], refs[n + 1]):
            cp.wait_send()
            cp.wait_recv()

    return pl.pallas_call(
        body, name=name, out_shape=[pltpu.HBM(a.shape, a.dtype) for a in arrs],
        in_specs=[HBM] * n + [SEM, SEM] + [ANY] * len(afters), out_specs=[HBM] * n,
        input_output_aliases={i: i for i in range(n)},
        compiler_params=pltpu.CompilerParams(has_side_effects=EFFECT))(*arrs, send, recv, *afters)


def _ag_forward(name, arrs, axes):
    n = len(arrs)

    def half_shape(t):
        _, r, cc = arrs[t].shape
        return (r // 2, cc) if axes[t] == 1 else (r, cc // 2)

    def body(*refs):
        g = refs[n:2 * n]
        bufs = refs[2 * n:3 * n]
        fsend, frecv, lsem = refs[3 * n:]
        x, y, c, chips = _place()
        for t in range(n):
            pend = [None, None]
            for k, chip in enumerate(chips):
                jk = 2 * chip[0] + chip[1]
                slot = k % 2
                if pend[slot] is not None:
                    pend[slot].wait_send()
                part = _half_of(g[t], jk, c, axes[t])
                ld = pltpu.make_async_copy(part, bufs[t].at[slot], lsem.at[2 * t + slot])
                ld.start()
                ld.wait()
                cp = pltpu.make_async_remote_copy(
                    src_ref=bufs[t].at[slot], dst_ref=part, send_sem=fsend.at[2 * t + slot],
                    recv_sem=frecv.at[t], device_id=(x, y, 1 - c), device_id_type=MESH)
                cp.start()
                pend[slot] = cp
            for cp in pend:
                cp.wait_send()
        for t in range(n):
            hr, hc = half_shape(t)
            passed = g[t].at[pl.ds(0, 3), pl.ds(0, hr), pl.ds(0, hc)]
            pltpu.make_async_remote_copy(
                src_ref=passed, dst_ref=passed, send_sem=fsend.at[2 * t], recv_sem=frecv.at[t],
                device_id=(x, y, 1 - c), device_id_type=MESH).wait_recv()

    dma = pltpu.SemaphoreType.DMA
    scratch = [pltpu.VMEM((2,) + half_shape(t), BF16) for t in range(n)]
    scratch += [dma((2 * n,)), dma((n,)), dma((2 * n,))]
    return pl.pallas_call(
        body, name=name, in_specs=[ANY] * n, out_specs=[ANY] * n, out_shape=[_sds(a.shape, a.dtype) for a in arrs],
        scratch_shapes=scratch, input_output_aliases={t: t for t in range(n)},
        compiler_params=pltpu.CompilerParams(has_side_effects=True, vmem_limit_bytes=VMEM_LIMIT))(*arrs)


def _allgather_blocking(gs, smalls):
    nb, ns = len(gs), len(smalls)
    halves = [g.shape[2] // 2 for g in gs]

    def body(*refs):
        s_in = refs[nb:nb + ns]
        g = refs[nb + ns:2 * nb + ns]
        s_out = refs[2 * nb + ns:2 * (nb + ns)]
        scr = refs[2 * (nb + ns):]
        bufs = scr[:nb]
        send, recv, fsend, frecv, lsem, ssend, srecv, slsem = scr[nb:]
        x, y, c, chips = _place()
        j = 2 * x + y
        sends, slocal = [], []
        for t in range(nb):
            rows = pl.ds(c * halves[t], halves[t])
            piece = g[t].at[:, j, rows, :]
            for k, chip in enumerate(chips):
                cp = pltpu.make_async_remote_copy(
                    src_ref=piece, dst_ref=piece, send_sem=send.at[3 * t + k], recv_sem=recv.at[3 * t + k],
                    device_id=(*chip, c), device_id_type=MESH)
                cp.start()
                sends.append(cp)
        for t in range(ns):
            cp = pltpu.make_async_copy(s_in[t], s_out[t].at[j], slsem.at[t])
            cp.start()
            slocal.append(cp)
            for k, chip in enumerate(chips):
                cp = pltpu.make_async_remote_copy(
                    src_ref=s_in[t], dst_ref=s_out[t].at[j], send_sem=ssend.at[3 * t + k],
                    recv_sem=srecv.at[3 * t + k], device_id=(*chip, c), device_id_type=MESH)
                cp.start()
                sends.append(cp)
        for t in range(nb):
            rows = pl.ds(c * halves[t], halves[t])
            pend = [None, None]
            n = 0
            for k, chip in enumerate(chips):
                jk = 2 * chip[0] + chip[1]
                landed = g[t].at[:, jk, rows, :]
                pltpu.make_async_remote_copy(
                    src_ref=landed, dst_ref=landed, send_sem=send.at[3 * t + k], recv_sem=recv.at[3 * t + k],
                    device_id=(*chip, c), device_id_type=MESH).wait_recv()
                for l in range(DEPTH):
                    slot = n % 2
                    if pend[slot] is not None:
                        pend[slot].wait_send()
                    part = g[t].at[l, jk, rows, :]
                    ld = pltpu.make_async_copy(part, bufs[t].at[slot], lsem.at[2 * t + slot])
                    ld.start()
                    ld.wait()
                    cp = pltpu.make_async_remote_copy(
                        src_ref=bufs[t].at[slot], dst_ref=part, send_sem=fsend.at[2 * t + slot],
                        recv_sem=frecv.at[t], device_id=(x, y, 1 - c), device_id_type=MESH)
                    cp.start()
                    pend[slot] = cp
                    n += 1
            for cp in pend:
                cp.wait_send()
        for t in range(ns):
            for k, chip in enumerate(chips):
                jk = 2 * chip[0] + chip[1]
                landed = s_out[t].at[jk]
                pltpu.make_async_remote_copy(
                    src_ref=landed, dst_ref=landed, send_sem=ssend.at[3 * t + k], recv_sem=srecv.at[3 * t + k],
                    device_id=(*chip, c), device_id_type=MESH).wait_recv()
        for t in range(nb):
            passed = g[t].at[:, pl.ds(0, 3), pl.ds((1 - c) * halves[t], halves[t]), :]
            pltpu.make_async_remote_copy(
                src_ref=passed, dst_ref=passed, send_sem=fsend.at[2 * t], recv_sem=frecv.at[t],
                device_id=(x, y, 1 - c), device_id_type=MESH).wait_recv()
        for cp in sends:
            cp.wait_send()
        for cp in slocal:
            cp.wait()

    dma = pltpu.SemaphoreType.DMA
    out_shape = [_sds(g.shape, g.dtype) for g in gs] + [_sds((NCHIP,) + s.shape, s.dtype) for s in smalls]
    scratch = [pltpu.VMEM((2, halves[t], gs[t].shape[3]), BF16) for t in range(nb)]
    scratch += [dma((3 * nb,)), dma((3 * nb,)), dma((2 * nb,)), dma((nb,)), dma((2 * nb,)),
                dma((3 * ns,)), dma((3 * ns,)), dma((ns,))]
    return pl.pallas_call(
        body, name="allgather_weights", in_specs=[ANY] * (nb + ns), out_specs=[ANY] * (nb + ns),
        out_shape=out_shape, scratch_shapes=scratch, input_output_aliases={t: t for t in range(nb)},
        compiler_params=pltpu.CompilerParams(has_side_effects=True, vmem_limit_bytes=VMEM_LIMIT))(*gs, *smalls)


def _reduce_scatter(tag, dws, pf):
    ps = []
    for t, dw in enumerate(dws):
        got = _pair_send("rs_pair_%s_%d" % (tag, t), dw, pf)
        ps.append(_add2_bf16("rs_add2_%s_%d" % (tag, t), dw, got, pf))
    got2 = _rs_chips("rs_chips_" + tag, ps)
    return [_add4_join("rs_add4_%s_%d" % (tag, t), p, g2, pf) for t, (p, g2) in enumerate(zip(ps, got2))]


def _rs_begin(tag, dws, by_cols, pf):
    ps = []
    for t, dw in enumerate(dws):
        got = _pair_send("rs_pair_%s_%d" % (tag, t), dw, pf, by_cols[t])
        ps.append(_add2_bf16("rs_add2_%s_%d" % (tag, t), dw, got, pf, by_cols[t]))
    send, recv, ps_thru, lands, token = _rs_chips_start("rs_chips_start_" + tag, ps)
    return (tag, send, recv, ps_thru, lands, by_cols), token


def _rs_end(handle, afters, pf):
    tag, send, recv, ps, lands, by_cols = handle
    ps, got2 = _rs_chips_wait("rs_chips_wait_" + tag, send, recv, ps, lands, afters)
    return [_add4_join("rs_add4_%s_%d" % (tag, t), p, g2, pf, by_cols[t]) for t, (p, g2) in enumerate(zip(ps, got2))]


def _allreduce_small(pack):
    r = pack.shape[0]
    flips = [(fx, fy, fc) for fx in (0, 1) for fy in (0, 1) for fc in (0, 1)][1:]

    def body(p_ref, o_ref, gat, send, recv):
        x, y, c, _ = _place()
        me = 4 * x + 2 * y + c
        gat[me] = p_ref[...]
        cps = []
        for k, (fx, fy, fc) in enumerate(flips):
            peer = ((1 - x) if fx else x, (1 - y) if fy else y, (1 - c) if fc else c)
            cp = pltpu.make_async_remote_copy(
                src_ref=p_ref, dst_ref=gat.at[me], send_sem=send.at[k], recv_sem=recv.at[k],
                device_id=peer, device_id_type=MESH)
            cp.start()
            cps.append(cp)
        for k, (fx, fy, fc) in enumerate(flips):
            peer = ((1 - x) if fx else x, (1 - y) if fy else y, (1 - c) if fc else c)
            src = 4 * peer[0] + 2 * peer[1] + peer[2]
            pltpu.make_async_remote_copy(
                src_ref=p_ref, dst_ref=gat.at[src], send_sem=send.at[k], recv_sem=recv.at[k],
                device_id=peer, device_id_type=MESH).wait_recv()
        for cp in cps:
            cp.wait_send()
        s = gat[0]
        for d in range(1, 8):
            s = s + gat[d]
        o_ref[...] = s

    dma = pltpu.SemaphoreType.DMA
    vm = pl.BlockSpec(memory_space=pltpu.VMEM)
    return pl.pallas_call(
        body, name="allreduce_small", in_specs=[vm], out_specs=vm, out_shape=_sds((r, LANE), F32),
        scratch_shapes=[pltpu.VMEM((8, r, LANE), F32), dma((7,)), dma((7,))],
        compiler_params=pltpu.CompilerParams(has_side_effects=True))(pack)


def _in_weights(win_g):
    full = jnp.concatenate([win_g[s] for s in range(NCHIP)], axis=0)
    wqkv = full[:2048]
    og = full[2048:3072]
    gates = jnp.pad(full[3072:3080], ((0, LANE - 8), (0, 0)))
    u = full[3080:4104]
    gb = full[4104:5128]
    gc = full[5128:6152]
    return wqkv, jnp.stack([og, u, gb, gc]), gates


def _in_grads(dwqkv, dwe, dwgt):
    full = jnp.concatenate([dwqkv, dwe[0], dwgt[:8], dwe[1], dwe[2], dwe[3]], axis=0)
    sw = DIN // NCHIP
    return jnp.stack([full[s * sw:(s + 1) * sw] for s in range(NCHIP)])


def _layer_fwd(l, h, get_mix, get_ffn, small):
    lp = h.shape[0]
    th = lp // 2
    wqkv, we, wgt, wout_g = get_mix(h)
    nmw, bias, mnw, cw, nfw = small
    tag = "_l%d" % l
    hn = _norm_fwd("norm_mix" + tag, h, nmw)
    qkv = _mm("proj_qkv" + tag, "nt", hn, wqkv,
              pl.BlockSpec((lp, D), lambda i, j, k: (0, 0)), pl.BlockSpec((512, D), lambda i, j, k: (j, 0)),
              pl.BlockSpec((lp, 512), lambda i, j, k: (0, j)), _sds((lp, 2048), BF16), (1, 4, 1))
    e = _mm("proj_e" + tag, "nt", hn, we,
            pl.BlockSpec((lp, D), lambda i, j, k: (0, 0)), pl.BlockSpec((None, 512, D), lambda i, j, k: (j // 2, j % 2, 0)),
            pl.BlockSpec((None, lp, 512), lambda i, j, k: (j // 2, 0, j % 2)), _sds((4, lp, 1024), BF16), (1, 8, 1))
    gpre = _mm("proj_gates" + tag, "nt", hn, wgt,
               pl.BlockSpec((lp, D), lambda i, j, k: (0, 0)), pl.BlockSpec((LANE, D), lambda i, j, k: (0, 0)),
               pl.BlockSpec((lp, LANE), lambda i, j, k: (0, 0)), _sds((lp, LANE), F32), (1, 1, 1))
    grow, mcol, ccol = _gate_prep("gate_prep" + tag, gpre, bias)
    ht = _mlstm_fwd("mlstm_fwd" + tag, qkv, grow, mcol, ccol, 4)
    mix = _mix_fwd("mix_fwd" + tag, ht, e, mnw, cw)
    wout = wout_g.reshape(D, D)
    h1 = _mm("out_proj" + tag, "nn", mix, wout,
             pl.BlockSpec((None, th, 1024), lambda i, j, k: (k, i, 0)),
             pl.BlockSpec((1024, 1024), lambda i, j, k: (k, j)),
             pl.BlockSpec((th, 1024), lambda i, j, k: (i, j)), _sds((lp, D), F32), (2, 2, 2),
             acc_shape=(th, 1024), res=h, res_spec=pl.BlockSpec((th, 1024), lambda i, j, k: (i, j)))
    wg_g, wu_g, wd_g = get_ffn(h1)
    hf = _norm_fwd("norm_ffn" + tag, h1, nfw)
    g, u, a = _ffn_up("ffn_up" + tag, hf, wg_g, wu_g)
    tk = DFF // NCHIP
    h2 = _mm("ffn_down" + tag, "nn", a, wd_g,
             pl.BlockSpec((th, tk), lambda i, j, k: (i, k)),
             pl.BlockSpec((None, tk, 1024), lambda i, j, k: (k, 0, j)),
             pl.BlockSpec((th, 1024), lambda i, j, k: (i, j)), _sds((lp, D), F32), (2, 2, NCHIP),
             acc_shape=(th, 1024), res=h1, res_spec=pl.BlockSpec((th, 1024), lambda i, j, k: (i, j)))
    saved = (h, hn, qkv, e, gpre, grow, mcol, ccol, ht, mix, h1, hf, g, u, a)
    return h2, saved, (wqkv, we, wgt, wout_g, wg_g, wu_g, wd_g)


def _layer_bwd(l, dh2, dh2b, saved, wts, small, ffn_done):
    h, hn, qkv, e, gpre, grow, mcol, ccol, ht, mix, h1, hf, g, u, a = saved
    wqkv, we, wgt, wout_g, wg_g, wu_g, wd_g = wts
    nmw, bias, mnw, cw, nfw = small
    lp = h.shape[0]
    th = lp // 2
    tk = DFF // NCHIP
    tag = "_l%d" % l
    half_rows = lambda i, j, k: (i, j)

    dwd = _mm("dw_down" + tag, "tn", a, dh2b,
              pl.BlockSpec((lp, tk), lambda i, j, k: (0, i)), pl.BlockSpec((lp, 1024), lambda i, j, k: (0, j)),
              pl.BlockSpec((None, tk, 1024), lambda i, j, k: (i, 0, j)), _sds((NCHIP, tk, D), BF16), (NCHIP, 2, 1))
    dg, du = _ffn_bwd_act("ffn_bwd_act" + tag, dh2b, wd_g, g, u)
    dws = []
    for nm, dact in (("gate", dg), ("up", du)):
        dws.append(_mm("dw_%s%s" % (nm, tag), "tn", hf, dact,
                       pl.BlockSpec((lp, 1024), lambda i, j, k: (0, i)), pl.BlockSpec((lp, tk), lambda i, j, k: (0, j)),
                       pl.BlockSpec((None, 1024, tk), lambda i, j, k: (j, i, 0)), _sds((NCHIP, D, tk), BF16),
                       (2, NCHIP, 1)))
    dwg, dwu = dws
    dhf = None
    for nm, dact, wfull in (("gate", dg, wg_g), ("up", du, wu_g)):
        dhf = _mm("dhf_%s%s" % (nm, tag), "nt", dact, wfull,
                  pl.BlockSpec((th, tk), lambda i, j, k: (i, k)),
                  pl.BlockSpec((None, 1024, tk), lambda i, j, k: (k, j, 0)),
                  pl.BlockSpec((th, 1024), half_rows), _sds((lp, D), F32), (2, 2, NCHIP), acc_shape=(th, 1024),
                  res=dhf, res_spec=None if dhf is None else pl.BlockSpec((th, 1024), half_rows))
    dh1, dh1b, dnfw = _norm_bwd("norm_ffn_bwd" + tag, dhf, h1, nfw, dh2)
    dh1b = ffn_done([dwg, dwu, dwd], dh1b)

    dwout = _mm("dw_out" + tag, "tn", mix, dh1b,
                pl.BlockSpec((None, lp, 1024), lambda i, j, k: (i, 0, 0)), pl.BlockSpec((lp, 1024), lambda i, j, k: (0, j)),
                pl.BlockSpec((1024, 1024), half_rows), _sds((D, D), BF16), (2, 2, 1))
    wout = wout_g.reshape(D, D)
    dmix = _mm("dmix" + tag, "nt", dh1b, wout,
               pl.BlockSpec((th, D), lambda i, j, k: (i, 0)), pl.BlockSpec((1024, D), lambda i, j, k: (j, 0)),
               pl.BlockSpec((th, 1024), half_rows), _sds((lp, D), F32), (2, 2, 1))
    dht, de, dmnw, dcw = _mix_bwd("mix_bwd" + tag, dmix, ht, e, mnw, cw)
    dq, dk, dv, dgrow, dfx = _mlstm_bwd("mlstm_bwd" + tag, qkv, grow, mcol, ccol, ht, dht, 8)
    dgp, dgpb, dbias = _gate_bwd("gate_bwd" + tag, gpre, bias, dgrow, dfx)
    del dgp
    dqkv = jnp.concatenate([dq, dk, dv], axis=1)

    hn_cols = pl.BlockSpec((lp, 1024), lambda i, j, k: (0, j))
    dwqkv = _mm("dw_qkv" + tag, "tn", dqkv, hn, pl.BlockSpec((lp, 1024), lambda i, j, k: (0, i)), hn_cols,
                pl.BlockSpec((1024, 1024), half_rows), _sds((2048, D), BF16), (2, 2, 1))
    dwe = _mm("dw_e" + tag, "tn", de, hn, pl.BlockSpec((None, lp, 1024), lambda i, j, k: (i, 0, 0)), hn_cols,
              pl.BlockSpec((None, 1024, 1024), lambda i, j, k: (i, 0, j)), _sds((4, 1024, D), BF16), (4, 2, 1))
    dwgt = _mm("dw_gates" + tag, "tn", dgpb, hn, pl.BlockSpec((lp, LANE), lambda i, j, k: (0, 0)), hn_cols,
               pl.BlockSpec((LANE, 1024), lambda i, j, k: (0, j)), _sds((LANE, D), BF16), (1, 2, 1))
    dwin = _in_grads(dwqkv, dwe, dwgt)

    dhn = _mm("dhn_qkv" + tag, "nn", dqkv, wqkv,
              pl.BlockSpec((th, 2048), lambda i, j, k: (i, 0)), pl.BlockSpec((2048, 1024), lambda i, j, k: (0, j)),
              pl.BlockSpec((th, 1024), half_rows), _sds((lp, D), F32), (2, 2, 1))
    dhn = _mm("dhn_e" + tag, "nn", de, we,
              pl.BlockSpec((None, th, 1024), lambda i, j, k: (k, i, 0)),
              pl.BlockSpec((None, 1024, 1024), lambda i, j, k: (k, 0, j)),
              pl.BlockSpec((th, 1024), half_rows), _sds((lp, D), F32), (2, 2, 4), acc_shape=(th, 1024),
              res=dhn, res_spec=pl.BlockSpec((th, 1024), half_rows))
    dhn = _mm("dhn_gates" + tag, "nn", dgpb, wgt,
              pl.BlockSpec((th, LANE), lambda i, j, k: (i, 0)), pl.BlockSpec((LANE, 1024), lambda i, j, k: (0, j)),
              pl.BlockSpec((th, 1024), half_rows), _sds((lp, D), F32), (2, 2, 1),
              res=dhn, res_spec=pl.BlockSpec((th, 1024), half_rows))
    dh0, dh0b, dnmw = _norm_bwd("norm_mix_bwd" + tag, dhn, h, nmw, dh1)

    pieces = [dwin, dwout.reshape(NCHIP, D // NCHIP, D)]
    smalls = (dnmw[0], dbias[0, :8], dcw[:3], dmnw[0], dnfw[0])
    return dh0, dh0b, pieces, smalls


def _pack_rows(parts):
    rows = []
    for p in parts:
        f = p.reshape(-1)
        pad = (-f.shape[0]) % LANE
        if pad:
            f = jnp.pad(f, (0, pad))
        rows.append(f.reshape(-1, LANE))
    r = jnp.concatenate(rows, axis=0)
    pad = (-r.shape[0]) % 8
    if pad:
        r = jnp.pad(r, ((0, pad), (0, 0)))
    return r


def _unpack_rows(pack, shapes):
    out, r0 = [], 0
    for s in shapes:
        n = 1
        for d in s:
            n *= d
        nr = -(-n // LANE)
        out.append(pack[r0:r0 + nr].reshape(-1)[:n].reshape(s))
        r0 += nr
    return out


def kernel(x, meta_tokens, norm_mix_w, w_in, b_gates, conv_w, mlstm_norm_w, w_out, norm_ffn_w, w_gate, w_up, w_down, norm_final_w, loss_target, m_meta_tokens, m_norm_mix_w, m_w_in, m_b_gates, m_conv_w, m_mlstm_norm_w, m_w_out, m_norm_ffn_w, m_w_gate, m_w_up, m_w_down, m_norm_final_w, v_meta_tokens, v_norm_mix_w, v_w_in, v_b_gates, v_conv_w, v_mlstm_norm_w, v_w_out, v_norm_ffn_w, v_w_gate, v_w_up, v_w_down, v_norm_final_w):
    seq = x.shape[1]
    n_real = N_META + seq
    lp = -(-n_real // LANE) * LANE
    xi, yi, ci = lax.axis_index("x"), lax.axis_index("y"), lax.axis_index("c")
    jchip = 2 * xi + yi
    pf = jnp.stack([ci, jchip, 2 * (1 - xi) + yi, 2 * xi + (1 - yi), 2 * (1 - xi) + (1 - yi)]).astype(jnp.int32)

    big = {"w_in": w_in, "w_out": w_out, "w_gate": w_gate, "w_up": w_up, "w_down": w_down}
    cast = {n: [_cast_into("cast_%s_l%d" % (n, l), w, l, pf) for l in range(DEPTH)] for n, w in big.items()
            if n != "w_in"}
    in_t = lambda a: jnp.transpose(a, (2, 0, 1))
    cast["w_in"] = [_cast_into_t("cast_w_in_l%d" % l, in_t(w_in), l, pf) for l in range(DEPTH)]
    conv_flat = jnp.pad(conv_w.reshape(DEPTH * 3, CW // NCHIP), ((0, 8 - DEPTH * 3), (0, 0)))

    def own_slot(a):
        return lax.dynamic_update_slice(jnp.zeros((NCHIP,) + a.shape, a.dtype), a[None], (jchip, 0, 0))

    groups, splits = [], []
    for l in range(DEPTH):
        groups.append([cast["w_in"][l], cast["w_out"][l]] + ([own_slot(meta_tokens), own_slot(conv_flat)] if l == 0 else []))
        splits.append([2, 1] + ([0, 0] if l == 0 else []))
        groups.append([cast[n][l] for n in ("w_gate", "w_up", "w_down")])
        splits.append([1, 1, 1])
    sems, arrs, _ = _ag_start("ag_start_0", groups[:1], splits[:1])
    sems_rest, arrs_rest, all_started = _ag_start("ag_start_1", groups[1:], splits[1:])
    sems, arrs = sems + sems_rest, arrs + arrs_rest

    def gathered(gi, afters):
        got = _ag_wait("ag_wait_%d" % gi, arrs[gi], splits[gi], sems[gi][0], sems[gi][1], afters)
        axes = [s for s in splits[gi] if s]
        return list(_ag_forward("ag_forward_%d" % gi, got[:len(axes)], axes)) + list(got[len(axes):])

    win0_g, wout0_g, meta_g, conv_g = gathered(0, [all_started])
    meta_full = jnp.concatenate([meta_g[s] for s in range(NCHIP)], axis=1)
    conv_full = jnp.concatenate([conv_g[s][:DEPTH * 3] for s in range(NCHIP)], axis=1)
    conv_full = conv_full.reshape(DEPTH, 3, CW)

    bias_rows = jnp.pad(b_gates, ((0, 0), (0, LANE - 8)))
    smalls = []
    for l in range(DEPTH):
        smalls.append((norm_mix_w[l][None], bias_rows[l][None], mlstm_norm_w[l][None],
                       jnp.pad(conv_full[l], ((0, 5), (0, 0))), norm_ffn_w[l][None]))

    h = jnp.concatenate([meta_full, x[0], jnp.zeros((lp - n_real, D), F32)], axis=0)
    saved, wts = [], []
    for l in range(DEPTH):
        def get_mix(h_in, l=l):
            win_g, wout_g = (win0_g, wout0_g) if l == 0 else gathered(2 * l, [h_in])
            return _in_weights(win_g) + (wout_g,)

        def get_ffn(h1, l=l):
            return tuple(gathered(2 * l + 1, [h1]))

        h, sv, wt = _layer_fwd(l, h, get_mix, get_ffn, smalls[l])
        saved.append(sv)
        wts.append(wt)
    tgt = jnp.pad(loss_target[0], ((N_META, lp - n_real), (0, 0)))
    dh, dhb, dnorm_final, loss_part = _loss_head(h, tgt, norm_final_w[None], n_real)

    names = ["w_in", "w_out", "w_gate", "w_up", "w_down"]
    params = {"w_in": (w_in, m_w_in, v_w_in), "w_out": (w_out, m_w_out, v_w_out), "w_gate": (w_gate, m_w_gate, v_w_gate),
              "w_up": (w_up, m_w_up, v_w_up), "w_down": (w_down, m_w_down, v_w_down)}
    big_out = {n: None for n in names}
    small_grads = [None] * DEPTH
    g_in = [None] * DEPTH

    def finish(l, group, handle, afters):
        for n, (g_mine, g_theirs) in zip(group, _rs_end(handle, afters, pf)):
            if n == "w_in":
                g_in[l] = (g_mine, g_theirs)
                continue
            w, m, v = params[n]
            big_out[n] = _adamw_layer("adamw_%s_l%d" % (n, l), l, g_mine, g_theirs, w, m, v, big_out[n], pf)
        return [big_out[n][3] for n in group if n != "w_in"]

    groups = []
    token = None
    for l in reversed(range(DEPTH)):
        def ffn_done(pieces, dh1b, l=l):
            handle, tok = _rs_begin("l%df" % l, pieces, [False] * 3, pf)
            groups.append((l, names[2:], handle))
            return dh1b + tok[0, 0].astype(BF16)

        dh, dhb, pieces, small_grads[l] = _layer_bwd(l, dh, dhb, saved[l], wts[l], smalls[l], ffn_done)
        handle, token = _rs_begin("l%dm" % l, pieces, [True, False], pf)
        groups.append((l, names[:2], handle))
        if l > 0:
            dhb = dhb + token[0, 0].astype(BF16)
    afters = [token]
    for l, group, handle in groups:
        afters = finish(l, group, handle, afters)
    big_out["w_in"] = [jnp.transpose(a, (1, 2, 0))
                       for a in _adamw_t("adamw_w_in", g_in, in_t(w_in), in_t(m_w_in), in_t(v_w_in), pf)]

    dnmw = jnp.stack([small_grads[l][0] for l in range(DEPTH)])
    dbias = jnp.stack([small_grads[l][1] for l in range(DEPTH)])
    dconv = jnp.stack([small_grads[l][2] for l in range(DEPTH)])
    dmnw = jnp.stack([small_grads[l][3] for l in range(DEPTH)])
    dnfw = jnp.stack([small_grads[l][4] for l in range(DEPTH)])
    part_shapes = [(N_META, D), (DEPTH, D), (DEPTH, 8), (DEPTH, 3, CW), (DEPTH, MW), (DEPTH, D), (D,), (LANE,)]
    pack = _pack_rows([dh[:N_META], dnmw, dbias, dconv, dmnw, dnfw, dnorm_final[0], loss_part[0]])
    tot = _unpack_rows(_allreduce_small(pack), part_shapes)
    g_meta_full, g_nmw, g_bias, g_conv_full, g_mnw, g_nfw, g_final, loss_row = tot
    mcols = D // NCHIP
    ccols = CW // NCHIP
    g_meta = lax.dynamic_slice_in_dim(g_meta_full, jchip * mcols, mcols, axis=1)
    g_conv = lax.dynamic_slice_in_dim(g_conv_full, jchip * ccols, ccols, axis=2)
    sm_g = [g_meta, g_nmw, g_bias, g_conv, g_mnw, g_nfw, g_final]
    sm_w = [meta_tokens, norm_mix_w, b_gates, conv_w, mlstm_norm_w, norm_ffn_w, norm_final_w]
    sm_m = [m_meta_tokens, m_norm_mix_w, m_b_gates, m_conv_w, m_mlstm_norm_w, m_norm_ffn_w, m_norm_final_w]
    sm_v = [v_meta_tokens, v_norm_mix_w, v_b_gates, v_conv_w, v_mlstm_norm_w, v_norm_ffn_w, v_norm_final_w]
    sm_shapes = [w.shape for w in sm_w]
    d_p, m_p, v_p = _adamw_flat(_pack_rows(sm_g), _pack_rows(sm_w), _pack_rows(sm_m), _pack_rows(sm_v))
    sm_d = _unpack_rows(d_p, sm_shapes)
    sm_nm = _unpack_rows(m_p, sm_shapes)
    sm_nv = _unpack_rows(v_p, sm_shapes)

    loss = loss_row[0]
    grad_x = dh[N_META:n_real][None]

    def ordered(sm, which):
        bo = {n: big_out[n][which] for n in names}
        return [sm[0], sm[1], bo["w_in"], sm[2], sm[3], sm[4], bo["w_out"], sm[5], bo["w_gate"], bo["w_up"], bo["w_down"], sm[6]]

    return (loss, grad_x, *ordered(sm_g, 0), *ordered(sm_d, 1), *ordered(sm_nm, 2), *ordered(sm_nv, 3))
```

```python
import functools

import jax
import jax.numpy as jnp
from jax import lax
from jax.experimental import pallas as pl
from jax.experimental.pallas import tpu as pltpu

F32 = jnp.float32
BF16 = jnp.bfloat16

D = 2048
N_META = 16
HEADS = 4
DQK = 128
DV = 256
MW = HEADS * DV
CW = D - MW
QKW = HEADS * DQK
DFF = 5632
DIN = 6152
NCHIP = 4
DEPTH = 2
GATE_CAP = 15.0
EPS = 1e-6
QK_SCALE = DQK ** -0.5
LANE = 128
VMEM_LIMIT = 60 * 1024 * 1024

ADAM_LR = 0.001
ADAM_B1 = 0.9
ADAM_B2 = 0.999
ADAM_EPS = 1e-08
ADAM_WD = 0.01
ADAM_STEP = 10

MESH = pl.DeviceIdType.MESH
ANY = pl.BlockSpec(memory_space=pl.ANY)


def _cp(sem):
    return pltpu.CompilerParams(dimension_semantics=sem, vmem_limit_bytes=VMEM_LIMIT)


def _sds(shape, dtype):
    return jax.ShapeDtypeStruct(shape, dtype)


_DN = {"nn": (((1,), (0,)), ((), ())), "nt": (((1,), (1,)), ((), ())), "tn": (((0,), (0,)), ((), ()))}


def _mm(name, kind, a, b, a_spec, b_spec, o_spec, out_shape, grid, acc_shape=None, res=None, res_spec=None):
    nk = grid[2]
    has_res = res is not None

    def body(*refs):
        if has_res:
            a_ref, b_ref, r_ref, o_ref = refs[:4]
        else:
            a_ref, b_ref, o_ref = refs[:3]
            r_ref = None
        p = lax.dot_general(a_ref[...], b_ref[...], _DN[kind], preferred_element_type=F32)
        if nk == 1:
            if r_ref is not None:
                p = p + r_ref[...]
            o_ref[...] = p.astype(o_ref.dtype)
        else:
            acc = refs[-1]
            k = pl.program_id(2)

            @pl.when(k == 0)
            def _():
                acc[...] = p

            @pl.when(k > 0)
            def _():
                acc[...] += p

            @pl.when(k == nk - 1)
            def _():
                r = acc[...]
                if r_ref is not None:
                    r = r + r_ref[...]
                o_ref[...] = r.astype(o_ref.dtype)

    ins = [a, b] + ([res] if has_res else [])
    in_specs = [a_spec, b_spec] + ([res_spec] if has_res else [])
    scratch = [pltpu.VMEM(acc_shape, F32)] if nk > 1 else []
    return pl.pallas_call(
        body, name=name, grid=grid, in_specs=in_specs, out_specs=o_spec, out_shape=out_shape,
        scratch_shapes=scratch, compiler_params=_cp(("parallel", "parallel", "arbitrary")))(*ins)


def _norm_fwd(name, h, w):
    lp = h.shape[0]
    tm = lp // 4

    def body(h_ref, w_ref, o_ref):
        x = h_ref[...]
        r = lax.rsqrt(jnp.mean(x * x, axis=1, keepdims=True) + EPS)
        o_ref[...] = (x * r * w_ref[...]).astype(BF16)

    return pl.pallas_call(
        body, name=name, grid=(4,),
        in_specs=[pl.BlockSpec((tm, D), lambda i: (i, 0)), pl.BlockSpec((1, D), lambda i: (0, 0))],
        out_specs=pl.BlockSpec((tm, D), lambda i: (i, 0)), out_shape=_sds((lp, D), BF16),
        compiler_params=_cp(("parallel",)))(h, w)


def _norm_bwd(name, dy, h, w, dres):
    lp = h.shape[0]
    tm = lp // 8

    def body(dy_ref, h_ref, w_ref, dres_ref, dh_ref, dhb_ref, dw_ref):
        x = h_ref[...]
        r = lax.rsqrt(jnp.mean(x * x, axis=1, keepdims=True) + EPS)
        xh = x * r
        dy_v = dy_ref[...]
        dxh = dy_v * w_ref[...]
        dx = r * (dxh - xh * jnp.mean(dxh * xh, axis=1, keepdims=True))
        dh = dres_ref[...] + dx
        dh_ref[...] = dh
        dhb_ref[...] = dh.astype(BF16)

        @pl.when(pl.program_id(0) == 0)
        def _():
            dw_ref[...] = jnp.zeros_like(dw_ref)

        dw_ref[0:1, :] += jnp.sum(dy_v * xh, axis=0, keepdims=True)

    row = pl.BlockSpec((tm, D), lambda i: (i, 0))
    return pl.pallas_call(
        body, name=name, grid=(8,),
        in_specs=[row, row, pl.BlockSpec((1, D), lambda i: (0, 0)), row],
        out_specs=[row, row, pl.BlockSpec((8, D), lambda i: (0, 0))],
        out_shape=[_sds((lp, D), F32), _sds((lp, D), BF16), _sds((8, D), F32)],
        compiler_params=_cp(("arbitrary",)))(dy, h, w, dres)


def _loss_head(h, tgt, w, n_real):
    lp = h.shape[0]
    tm = lp // 8

    def body(h_ref, t_ref, w_ref, dh_ref, dhb_ref, dw_ref, loss_ref):
        i = pl.program_id(0)
        x = h_ref[...]
        r = lax.rsqrt(jnp.mean(x * x, axis=1, keepdims=True) + EPS)
        xh = x * r
        wv = w_ref[...]
        row = i * tm + lax.broadcasted_iota(jnp.int32, (tm, 1), 0)
        valid = jnp.logical_and(row >= N_META, row < n_real)
        err = jnp.where(valid, xh * wv - t_ref[...], 0.0)
        dy_v = err * (1.0 / D)
        dxh = dy_v * wv
        dx = r * (dxh - xh * jnp.mean(dxh * xh, axis=1, keepdims=True))
        dh_ref[...] = dx
        dhb_ref[...] = dx.astype(BF16)

        @pl.when(i == 0)
        def _():
            dw_ref[...] = jnp.zeros_like(dw_ref)
            loss_ref[...] = jnp.zeros_like(loss_ref)

        dw_ref[0:1, :] += jnp.sum(dy_v * xh, axis=0, keepdims=True)
        part = jnp.sum(jnp.sum(err * err, axis=1, keepdims=True), axis=0, keepdims=True) * (0.5 / D)
        loss_ref[...] += jnp.broadcast_to(part, loss_ref.shape)

    row = pl.BlockSpec((tm, D), lambda i: (i, 0))
    return pl.pallas_call(
        body, name="loss_head", grid=(8,),
        in_specs=[row, row, pl.BlockSpec((1, D), lambda i: (0, 0))],
        out_specs=[row, row, pl.BlockSpec((8, D), lambda i: (0, 0)), pl.BlockSpec((8, LANE), lambda i: (0, 0))],
        out_shape=[_sds((lp, D), F32), _sds((lp, D), BF16), _sds((8, D), F32), _sds((8, LANE), F32)],
        compiler_params=_cp(("arbitrary",)))(h, tgt, w)


def _ffn_up(name, hf, wg4, wu4):
    lp = hf.shape[0]
    tm = lp // 4
    tn = DFF // NCHIP

    def body(x_ref, wg_ref, wu_ref, g_ref, u_ref, a_ref):
        x = x_ref[...]
        g = jnp.dot(x, wg_ref[...], preferred_element_type=F32)
        u = jnp.dot(x, wu_ref[...], preferred_element_type=F32)
        g_ref[...] = g.astype(BF16)
        u_ref[...] = u.astype(BF16)
        a_ref[...] = (g * jax.nn.sigmoid(g) * u).astype(BF16)

    wspec = pl.BlockSpec((None, D, tn), lambda j, i: (j, 0, 0))
    ospec = pl.BlockSpec((tm, tn), lambda j, i: (i, j))
    o = _sds((lp, DFF), BF16)
    return pl.pallas_call(
        body, name=name, grid=(NCHIP, 4),
        in_specs=[pl.BlockSpec((tm, D), lambda j, i: (i, 0)), wspec, wspec],
        out_specs=[ospec, ospec, ospec], out_shape=[o, o, o],
        compiler_params=_cp(("parallel", "parallel")))(hf, wg4, wu4)


def _ffn_bwd_act(name, dhb, wd4, g, u):
    lp = dhb.shape[0]
    tm = lp // 4
    tn = DFF // NCHIP

    def body(d_ref, w_ref, g_ref, u_ref, dg_ref, du_ref):
        da = lax.dot_general(d_ref[...], w_ref[...], _DN["nt"], preferred_element_type=F32)
        gv = g_ref[...].astype(F32)
        uv = u_ref[...].astype(F32)
        sg = jax.nn.sigmoid(gv)
        dg_ref[...] = (da * uv * (sg * (1.0 + gv * (1.0 - sg)))).astype(BF16)
        du_ref[...] = (da * (gv * sg)).astype(BF16)

    ospec = pl.BlockSpec((tm, tn), lambda j, i: (i, j))
    o = _sds((lp, DFF), BF16)
    return pl.pallas_call(
        body, name=name, grid=(NCHIP, 4),
        in_specs=[pl.BlockSpec((tm, D), lambda j, i: (i, 0)),
                  pl.BlockSpec((None, tn, D), lambda j, i: (j, 0, 0)), ospec, ospec],
        out_specs=[ospec, ospec], out_shape=[o, o],
        compiler_params=_cp(("parallel", "parallel")))(dhb, wd4, g, u)


def _shift_rows(x, d, row):
    return jnp.where(row >= d, pltpu.roll(x, d, axis=0), 0.0)


def _scan_steps(lp):
    d = 1
    while d < lp:
        yield d
        d *= 2


def _gate_values(pre):
    t = GATE_CAP * jnp.tanh(pre * (1.0 / GATE_CAP))
    lf = jnp.minimum(t, 0.0) - jnp.log(1.0 + jnp.exp(-jnp.abs(t)))
    return t, lf


def _gate_prep(name, gates_pre, bias):
    lp = gates_pre.shape[0]

    def body(p_ref, b_ref, grow_ref, m_ref, c_ref):
        pre = p_ref[...] + b_ref[...]
        lane = lax.broadcasted_iota(jnp.int32, (lp, LANE), 1)
        row = lax.broadcasted_iota(jnp.int32, (lp, LANE), 0)
        t, lf = _gate_values(pre)
        f = jnp.where(jnp.logical_and(lane >= HEADS, lane < 2 * HEADS), lf, 0.0)
        for d in _scan_steps(lp):
            f = f + _shift_rows(f, d, row)
        fs = pltpu.roll(f, LANE - HEADS, axis=1)
        g = jnp.where(lane < HEADS, t - fs, 0.0)
        m = g
        for d in _scan_steps(lp):
            m = jnp.maximum(m, jnp.where(row >= d, pltpu.roll(m, d, axis=0), m))
        grow_ref[...] = g.T
        m_ref[...] = m
        c_ref[...] = jnp.where(lane < HEADS, -fs - m, 0.0)

    full = pl.BlockSpec((lp, LANE), lambda: (0, 0))
    return pl.pallas_call(
        body, name=name, in_specs=[full, pl.BlockSpec((1, LANE), lambda: (0, 0))],
        out_specs=[pl.BlockSpec((LANE, lp), lambda: (0, 0)), full, full],
        out_shape=[_sds((LANE, lp), F32), _sds((lp, LANE), F32), _sds((lp, LANE), F32)],
        compiler_params=pltpu.CompilerParams(vmem_limit_bytes=VMEM_LIMIT))(gates_pre, bias)


def _pick_lane(blk, h):
    lane = lax.broadcasted_iota(jnp.int32, blk.shape, 1)
    return jnp.sum(jnp.where(lane == h, blk, 0.0), axis=1, keepdims=True)


def _mlstm_weights(q, k, grow, mcol, i, bq, nk):
    s = lax.dot_general(q, k, _DN["nt"], preferred_element_type=F32) * QK_SCALE
    row = i * bq + lax.broadcasted_iota(jnp.int32, (bq, 1), 0)
    col = lax.broadcasted_iota(jnp.int32, (1, nk), 1)
    a = jnp.where(col <= row, jnp.exp(jnp.minimum(grow - mcol, 0.0)), 0.0)
    return s, a


def _per_query_tile(i, nq, bq, lp, compute):
    for ii in range(nq):
        nk = min(lp, -(-((ii + 1) * bq) // LANE) * LANE)
        pl.when(i == ii)(functools.partial(compute, nk))


def _mlstm_fwd(name, qkv, grow, mcol_all, ccol_all, nq):
    lp = qkv.shape[0]
    bq = lp // nq

    def body(q_ref, k_ref, v_ref, grow_ref, m_ref, c_ref, o_ref):
        h = pl.program_id(0)
        i = pl.program_id(1)
        mcol = _pick_lane(m_ref[...], h)
        ccol = _pick_lane(c_ref[...], h)

        def compute(nk):
            grow_h = grow_ref[pl.ds(h, 1), 0:nk]
            s, a = _mlstm_weights(q_ref[...], k_ref[0:nk, :], grow_h, mcol, i, bq, nk)
            p = a * s
            den = jnp.sum(p, axis=1, keepdims=True)
            num = jnp.dot(p.astype(BF16), v_ref[0:nk, :], preferred_element_type=F32)
            o_ref[...] = num / jnp.maximum(jnp.abs(den), jnp.exp(ccol))

        _per_query_tile(i, nq, bq, lp, compute)

    return pl.pallas_call(
        body, name=name, grid=(HEADS, nq),
        in_specs=[pl.BlockSpec((bq, DQK), lambda h, i: (i, h)),
                  pl.BlockSpec((lp, DQK), lambda h, i: (0, HEADS + h)),
                  pl.BlockSpec((lp, DV), lambda h, i: (0, HEADS + h)),
                  pl.BlockSpec((8, lp), lambda h, i: (0, 0)),
                  pl.BlockSpec((bq, LANE), lambda h, i: (i, 0)),
                  pl.BlockSpec((bq, LANE), lambda h, i: (i, 0))],
        out_specs=pl.BlockSpec((bq, DV), lambda h, i: (i, h)),
        out_shape=_sds((lp, MW), F32),
        compiler_params=_cp(("parallel", "parallel")))(qkv, qkv, qkv, grow, mcol_all, ccol_all)


def _mlstm_bwd(name, qkv, grow, mcol_all, ccol_all, ht, dht, nq):
    lp = qkv.shape[0]
    bq = lp // nq

    def body(q_ref, k_ref, v_ref, grow_ref, m_ref, c_ref, ht_ref, dht_ref,
             dq_ref, dk_ref, dv_ref, dgrow_ref, dfx_ref, dkt_acc, dvt_acc):
        h = pl.program_id(0)
        i = pl.program_id(1)

        @pl.when(jnp.logical_and(h == 0, i == 0))
        def _():
            dgrow_ref[...] = jnp.zeros_like(dgrow_ref)
            dfx_ref[...] = jnp.zeros_like(dfx_ref)

        @pl.when(i == 0)
        def _():
            dkt_acc[...] = jnp.zeros_like(dkt_acc)
            dvt_acc[...] = jnp.zeros_like(dvt_acc)

        mcol = _pick_lane(m_ref[...], h)
        ccol = _pick_lane(c_ref[...], h)

        def compute(nk):
            q = q_ref[...]
            k = k_ref[0:nk, :]
            v = v_ref[0:nk, :]
            grow_h = grow_ref[pl.ds(h, 1), 0:nk]
            s, a = _mlstm_weights(q, k, grow_h, mcol, i, bq, nk)
            p = a * s
            den = jnp.sum(p, axis=1, keepdims=True)
            clamp = jnp.exp(ccol)
            active = jnp.abs(den) < clamp
            dd = jnp.maximum(jnp.abs(den), clamp)
            dht_v = dht_ref[...]
            hdh = jnp.sum(dht_v * ht_ref[...], axis=1, keepdims=True)
            dn = (dht_v / dd).astype(BF16)
            dden = jnp.where(active, 0.0, -(hdh / dd) * jnp.sign(den))
            dp = lax.dot_general(dn, v, _DN["nt"], preferred_element_type=F32) + dden
            rmat = dp * p
            dgrow_ref[pl.ds(h, 1), 0:nk] += jnp.sum(rmat, axis=0, keepdims=True)
            ds = (dp * a * QK_SCALE).astype(BF16)
            dq_ref[...] = jnp.dot(ds, k, preferred_element_type=F32).astype(BF16)
            dkt_acc[:, 0:nk] += lax.dot_general(q, ds, _DN["tn"], preferred_element_type=F32)
            dvt_acc[:, 0:nk] += lax.dot_general(dn, p.astype(BF16), _DN["tn"], preferred_element_type=F32)
            lane = lax.broadcasted_iota(jnp.int32, (bq, LANE), 1)
            r0 = pl.multiple_of(i * bq, 16)
            dfx_ref[pl.ds(r0, bq), :] += jnp.where(lane == h, jnp.sum(rmat, axis=1, keepdims=True), 0.0)

        _per_query_tile(i, nq, bq, lp, compute)

        @pl.when(i == nq - 1)
        def _():
            dk_ref[...] = dkt_acc[...].T.astype(BF16)
            dv_ref[...] = dvt_acc[...].T.astype(BF16)

    return pl.pallas_call(
        body, name=name, grid=(HEADS, nq),
        in_specs=[pl.BlockSpec((bq, DQK), lambda h, i: (i, h)),
                  pl.BlockSpec((lp, DQK), lambda h, i: (0, HEADS + h)),
                  pl.BlockSpec((lp, DV), lambda h, i: (0, HEADS + h)),
                  pl.BlockSpec((8, lp), lambda h, i: (0, 0)),
                  pl.BlockSpec((bq, LANE), lambda h, i: (i, 0)),
                  pl.BlockSpec((bq, LANE), lambda h, i: (i, 0)),
                  pl.BlockSpec((bq, DV), lambda h, i: (i, h)),
                  pl.BlockSpec((bq, DV), lambda h, i: (i, h))],
        out_specs=[pl.BlockSpec((bq, DQK), lambda h, i: (i, h)),
                   pl.BlockSpec((lp, DQK), lambda h, i: (0, h)),
                   pl.BlockSpec((lp, DV), lambda h, i: (0, h)),
                   pl.BlockSpec((LANE, lp), lambda h, i: (0, 0)),
                   pl.BlockSpec((lp, LANE), lambda h, i: (0, 0))],
        out_shape=[_sds((lp, QKW), BF16), _sds((lp, QKW), BF16), _sds((lp, MW), BF16),
                   _sds((LANE, lp), F32), _sds((lp, LANE), F32)],
        scratch_shapes=[pltpu.VMEM((DQK, lp), F32), pltpu.VMEM((DV, lp), F32)],
        compiler_params=_cp(("arbitrary", "arbitrary")))(qkv, qkv, qkv, grow, mcol_all, ccol_all, ht, dht)


def _gate_bwd(name, gates_pre, bias, dgrow, dfx):
    lp = gates_pre.shape[0]

    def body(p_ref, b_ref, dgrow_ref, dfx_ref, dg_ref, dgb_ref, db_ref):
        pre = p_ref[...] + b_ref[...]
        lane = lax.broadcasted_iota(jnp.int32, (lp, LANE), 1)
        row = lax.broadcasted_iota(jnp.int32, (lp, LANE), 0)
        th = jnp.tanh(pre * (1.0 / GATE_CAP))
        t = GATE_CAP * th
        dgc = jnp.where(lane < HEADS, dgrow_ref[...].T, 0.0)
        df = jnp.where(lane < HEADS, dfx_ref[...] - dgc, 0.0)
        for d in _scan_steps(lp):
            df = df + jnp.where(row < lp - d, pltpu.roll(df, lp - d, axis=0), 0.0)
        dlf = pltpu.roll(df, HEADS, axis=1)
        dt = jnp.where(lane < HEADS, dgc, dlf * jax.nn.sigmoid(-t))
        dpre = jnp.where(lane < 2 * HEADS, dt * (1.0 - th * th), 0.0)
        dg_ref[...] = dpre
        dgb_ref[...] = dpre.astype(BF16)
        db_ref[...] = jnp.broadcast_to(jnp.sum(dpre, axis=0, keepdims=True), db_ref.shape)

    full = pl.BlockSpec((lp, LANE), lambda: (0, 0))
    return pl.pallas_call(
        body, name=name,
        in_specs=[full, pl.BlockSpec((1, LANE), lambda: (0, 0)), pl.BlockSpec((LANE, lp), lambda: (0, 0)), full],
        out_specs=[full, full, pl.BlockSpec((8, LANE), lambda: (0, 0))],
        out_shape=[_sds((lp, LANE), F32), _sds((lp, LANE), BF16), _sds((8, LANE), F32)],
        compiler_params=pltpu.CompilerParams(vmem_limit_bytes=VMEM_LIMIT))(gates_pre, bias, dgrow, dfx)


CB = 256


def _e_specs(lp):
    return [pl.BlockSpec((None, lp, CB), functools.partial(lambda c, j: (c, 0, j), c)) for c in range(4)]


def _mix_fwd(name, ht, e, mnw, cw):
    lp = ht.shape[0]

    def body(ht_ref, og_ref, u_ref, gb_ref, gc_ref, mnw_ref, cw_ref, o_ref):
        x = ht_ref[...]
        r = lax.rsqrt(jnp.mean(x * x, axis=1, keepdims=True) + EPS)
        o_ref[0] = (jax.nn.sigmoid(og_ref[...].astype(F32)) * (x * r * mnw_ref[...])).astype(BF16)
        row = lax.broadcasted_iota(jnp.int32, (lp, CB), 0)
        a = gc_ref[...].astype(F32) * u_ref[...].astype(F32)
        conv = cw_ref[2:3, :] * a + cw_ref[1:2, :] * _shift_rows(a, 1, row) + cw_ref[0:1, :] * _shift_rows(a, 2, row)
        o_ref[1] = (gb_ref[...].astype(F32) * conv).astype(BF16)

    col = pl.BlockSpec((lp, CB), lambda j: (0, j))
    return pl.pallas_call(
        body, name=name, grid=(4,),
        in_specs=[col] + _e_specs(lp) + [pl.BlockSpec((1, CB), lambda j: (0, j)), pl.BlockSpec((8, CB), lambda j: (0, j))],
        out_specs=pl.BlockSpec((2, lp, CB), lambda j: (0, 0, j)), out_shape=_sds((2, lp, MW), BF16),
        compiler_params=_cp(("parallel",)))(ht, e, e, e, e, mnw, cw)


def _mix_bwd(name, dmix, ht, e, mnw, cw):
    lp = ht.shape[0]

    def body(dhm_ref, dhc_ref, ht_ref, og_ref, u_ref, gb_ref, gc_ref, mnw_ref, cw_ref,
             dht_ref, de_ref, dmnw_ref, dcw_ref):
        x = ht_ref[...]
        r = lax.rsqrt(jnp.mean(x * x, axis=1, keepdims=True) + EPS)
        xh = x * r
        w = mnw_ref[...]
        sg = jax.nn.sigmoid(og_ref[...].astype(F32))
        dhm = dhm_ref[...]
        de_ref[0] = (dhm * (xh * w) * (sg * (1.0 - sg))).astype(BF16)
        dn = dhm * sg
        dmnw_ref[...] = jnp.broadcast_to(jnp.sum(dn * xh, axis=0, keepdims=True), dmnw_ref.shape)
        dxh = dn * w
        dht_ref[...] = r * (dxh - xh * jnp.mean(dxh * xh, axis=1, keepdims=True))

        row = lax.broadcasted_iota(jnp.int32, (lp, CB), 0)
        uv = u_ref[...].astype(F32)
        gcv = gc_ref[...].astype(F32)
        gbv = gb_ref[...].astype(F32)
        a = gcv * uv
        a1 = _shift_rows(a, 1, row)
        a2 = _shift_rows(a, 2, row)
        dhc = dhc_ref[...]
        conv = cw_ref[2:3, :] * a + cw_ref[1:2, :] * a1 + cw_ref[0:1, :] * a2
        de_ref[2] = (dhc * conv).astype(BF16)
        dconv = dhc * gbv
        dcw_ref[...] = jnp.zeros_like(dcw_ref)
        dcw_ref[0:1, :] = jnp.sum(dconv * a2, axis=0, keepdims=True)
        dcw_ref[1:2, :] = jnp.sum(dconv * a1, axis=0, keepdims=True)
        dcw_ref[2:3, :] = jnp.sum(dconv * a, axis=0, keepdims=True)
        up1 = jnp.where(row < lp - 1, pltpu.roll(dconv, lp - 1, axis=0), 0.0)
        up2 = jnp.where(row < lp - 2, pltpu.roll(dconv, lp - 2, axis=0), 0.0)
        da = cw_ref[2:3, :] * dconv + cw_ref[1:2, :] * up1 + cw_ref[0:1, :] * up2
        de_ref[1] = (da * gcv).astype(BF16)
        de_ref[3] = (da * uv).astype(BF16)

    col = pl.BlockSpec((lp, CB), lambda j: (0, j))
    small = pl.BlockSpec((8, CB), lambda j: (0, j))
    return pl.pallas_call(
        body, name=name, grid=(4,),
        in_specs=[col, pl.BlockSpec((lp, CB), lambda j: (0, 4 + j)), col] + _e_specs(lp)
                 + [pl.BlockSpec((1, CB), lambda j: (0, j)), small],
        out_specs=[col, pl.BlockSpec((4, lp, CB), lambda j: (0, 0, j)), small, small],
        out_shape=[_sds((lp, MW), F32), _sds((4, lp, MW), BF16), _sds((8, MW), F32), _sds((8, CW), F32)],
        compiler_params=_cp(("parallel",)))(dmix, dmix, ht, e, e, e, e, mnw, cw)


def _row_tile(r, c, itemsize, budget=1536 * 1024, mult=16):
    best = None
    for t in range(mult, r + 1, mult):
        if r % t == 0 and t * c * itemsize <= budget:
            best = t
    if best is None:
        best = r
    return best


def _grid_spec(grid, in_specs, out_specs, scratch=()):
    return pltpu.PrefetchScalarGridSpec(num_scalar_prefetch=1, grid=grid, in_specs=in_specs,
                                        out_specs=out_specs, scratch_shapes=list(scratch))


def _cast_into(name, w, layer, pf):
    _, r, c = w.shape
    tr = _row_tile(r, c, 4)

    def body(pf_ref, x_ref, o_ref):
        o_ref[...] = x_ref[...].astype(BF16)

    return pl.pallas_call(
        body, name=name, out_shape=_sds((NCHIP, r, c), BF16),
        grid_spec=_grid_spec((r // tr,), [pl.BlockSpec((None, tr, c), lambda i, pf: (layer, i, 0))],
                             pl.BlockSpec((None, tr, c), lambda i, pf: (pf[1], i, 0))),
        compiler_params=_cp(("parallel",)))(pf, w)


TCOL = 256


def _cast_into_t(name, w_t, layer, pf):
    c, nl, r = w_t.shape

    def body(pf_ref, x_ref, o_ref):
        o_ref[...] = x_ref[:, layer, :].astype(BF16)

    return pl.pallas_call(
        body, name=name, out_shape=_sds((NCHIP, c, r), BF16),
        grid_spec=_grid_spec((r // TCOL,), [pl.BlockSpec((c, nl, TCOL), lambda i, pf: (0, 0, i))],
                             pl.BlockSpec((None, c, TCOL), lambda i, pf: (pf[1], 0, i))),
        compiler_params=_cp(("parallel",)))(pf, w_t)


def _add2_bf16(name, dw, got, pf, by_cols=False):
    n4, r2, c2 = got.shape

    def body(pf_ref, a_ref, b_ref, o_ref):
        o_ref[...] = (a_ref[...].astype(F32) + b_ref[...].astype(F32)).astype(BF16)

    if by_cols:
        nch = c2 // TCOL
        spec = pl.BlockSpec((None, r2, TCOL), lambda s, i, pf: (s, 0, i))
        mine = pl.BlockSpec((None, r2, TCOL), lambda s, i, pf: (s, 0, pf[0] * nch + i))
    else:
        tr = _row_tile(r2, c2, 4)
        nch = r2 // tr
        spec = pl.BlockSpec((None, tr, c2), lambda s, i, pf: (s, i, 0))
        mine = pl.BlockSpec((None, tr, c2), lambda s, i, pf: (s, pf[0] * nch + i, 0))
    return pl.pallas_call(
        body, name=name, out_shape=_sds((n4, r2, c2), BF16),
        grid_spec=_grid_spec((n4, nch), [mine, spec], spec),
        compiler_params=_cp(("parallel", "parallel")))(pf, dw, got)


def _adam_math(w, g, m, v):
    m2 = ADAM_B1 * m + (1.0 - ADAM_B1) * g
    v2 = ADAM_B2 * v + (1.0 - ADAM_B2) * (g * g)
    m_hat = m2 / (1.0 - ADAM_B1 ** ADAM_STEP)
    v_hat = v2 / (1.0 - ADAM_B2 ** ADAM_STEP)
    delta = -ADAM_LR * (m_hat / (jnp.sqrt(v_hat) + ADAM_EPS) + ADAM_WD * w)
    return delta, m2, v2


def _adamw_layer(name, layer, g_mine, g_theirs, w, m, v, prev, pf):
    _, r, c = w.shape
    r2 = r // 2
    tr = _row_tile(r2, c, 4, budget=1024 * 1024, mult=8)
    nch = r2 // tr
    n_alias = 0 if prev is None else 4

    def body(*refs):
        pf_ref, gm_ref, gt_ref, w_ref, m_ref, v_ref = refs[:6]
        go_ref, d_ref, mo_ref, vo_ref = refs[6 + n_alias:]
        mine = (pl.program_id(0) // nch) == pf_ref[0]
        gv = jnp.where(mine, gm_ref[...], gt_ref[...])
        delta, m2, v2 = _adam_math(w_ref[...], gv, m_ref[...], v_ref[...])
        go_ref[...] = gv
        d_ref[...] = delta
        mo_ref[...] = m2
        vo_ref[...] = v2

    slab = pl.BlockSpec((None, tr, c), lambda i, pf: (layer, i, 0))
    ins = [g_mine, g_theirs, w, m, v] + (list(prev) if prev is not None else [])
    in_specs = [pl.BlockSpec((tr, c), lambda i, pf: (jnp.clip(i - pf[0] * nch, 0, nch - 1), 0)),
                pl.BlockSpec((tr, c), lambda i, pf: (jnp.clip(i - (1 - pf[0]) * nch, 0, nch - 1), 0)),
                slab, slab, slab] + [ANY] * n_alias
    o = _sds(w.shape, F32)
    return pl.pallas_call(
        body, name=name, out_shape=[o] * 4, grid_spec=_grid_spec((2 * nch,), in_specs, [slab] * 4),
        input_output_aliases={6 + k: k for k in range(n_alias)},
        compiler_params=_cp(("parallel",)))(pf, *ins)


def _adamw_t(name, gs, w_t, m_t, v_t, pf):
    c, nl, r = w_t.shape
    ta = LANE
    nch = (r // 2) // ta

    def body(*refs):
        pf_ref = refs[0]
        g_refs = refs[1:1 + 2 * nl]
        w_ref, m_ref, v_ref, go_ref, d_ref, mo_ref, vo_ref = refs[1 + 2 * nl:]
        mine = (pl.program_id(0) // nch) == pf_ref[0]
        gv = jnp.stack([jnp.where(mine, g_refs[2 * l][...], g_refs[2 * l + 1][...]) for l in range(nl)], axis=1)
        delta, m2, v2 = _adam_math(w_ref[...], gv, m_ref[...], v_ref[...])
        go_ref[...] = gv
        d_ref[...] = delta
        mo_ref[...] = m2
        vo_ref[...] = v2

    both = pl.BlockSpec((c, nl, ta), lambda i, pf: (0, 0, i))
    g_specs = []
    for l in range(nl):
        g_specs += [pl.BlockSpec((c, ta), lambda i, pf: (0, jnp.clip(i - pf[0] * nch, 0, nch - 1))),
                    pl.BlockSpec((c, ta), lambda i, pf: (0, jnp.clip(i - (1 - pf[0]) * nch, 0, nch - 1)))]
    o = _sds(w_t.shape, F32)
    flat_g = [a for pair in gs for a in pair]
    return pl.pallas_call(
        body, name=name, out_shape=[o] * 4, grid_spec=_grid_spec((2 * nch,), g_specs + [both] * 3, [both] * 4),
        compiler_params=_cp(("parallel",)))(pf, *flat_g, w_t, m_t, v_t)


def _adamw_flat(g, w, m, v):
    def body(g_ref, w_ref, m_ref, v_ref, d_ref, mo_ref, vo_ref):
        delta, m2, v2 = _adam_math(w_ref[...], g_ref[...], m_ref[...], v_ref[...])
        d_ref[...] = delta
        mo_ref[...] = m2
        vo_ref[...] = v2

    o = _sds(w.shape, F32)
    return pl.pallas_call(body, name="adamw_small", out_shape=[o, o, o])(g, w, m, v)


def _place():
    x, y, c = lax.axis_index("x"), lax.axis_index("y"), lax.axis_index("c")
    chips = [(1 - x, y), (x, 1 - y), (1 - x, 1 - y)]
    return x, y, c, chips


def _rs_chips(name, ps):
    n = len(ps)

    def body(*refs):
        ins = refs[:n]
        got = refs[n:2 * n]
        send, recv = refs[2 * n:]
        x, y, c, chips = _place()
        cps = []
        for t in range(n):
            for k, chip in enumerate(chips):
                jk = 2 * chip[0] + chip[1]
                cp = pltpu.make_async_remote_copy(
                    src_ref=ins[t].at[jk], dst_ref=got[t].at[k],
                    send_sem=send.at[3 * t + k], recv_sem=recv.at[3 * t + k],
                    device_id=(*chip, c), device_id_type=MESH)
                cp.start()
                cps.append(cp)
        for cp in cps:
            cp.wait()

    dma = pltpu.SemaphoreType.DMA
    return pl.pallas_call(
        body, name=name, in_specs=[ANY] * n, out_specs=[ANY] * n,
        out_shape=[_sds((3,) + p.shape[1:], BF16) for p in ps],
        scratch_shapes=[dma((3 * n,)), dma((3 * n,))],
        compiler_params=pltpu.CompilerParams(has_side_effects=True))(*ps)


HBM = pl.BlockSpec(memory_space=pltpu.HBM)
SEM = pl.BlockSpec(memory_space=pltpu.SEMAPHORE)
EFFECT = pltpu.SideEffectType.DATAFLOW_SIDE_EFFECTING


def _in_hbm(a):
    return pltpu.with_memory_space_constraint(a, pltpu.HBM)


def _rs_chips_copies(ins, lands, send, recv):
    x, y, c, chips = _place()
    cps = []
    for t in range(len(ins)):
        for k, chip in enumerate(chips):
            jk = 2 * chip[0] + chip[1]
            cps.append(pltpu.make_async_remote_copy(
                src_ref=ins[t].at[jk], dst_ref=lands[t].at[k], send_sem=send.at[3 * t + k],
                recv_sem=recv.at[3 * t + k], device_id=(*chip, c), device_id_type=MESH))
    return cps


def _rs_chips_start(name, ps):
    n = len(ps)

    def body(*refs):
        ins, lands = refs[:n], refs[n:2 * n]
        send, recv = refs[2 * n], refs[2 * n + 1]
        token = refs[-1]
        for cp in _rs_chips_copies(ins, lands, send, recv):
            cp.start()
        token[...] = jnp.zeros_like(token)

    dma = pltpu.SemaphoreType.DMA
    lands = [lax.empty((3,) + p.shape[1:], BF16) for p in ps]
    out_shape = ([dma((3 * n,)), dma((3 * n,))] + [pltpu.HBM(p.shape, BF16) for p in ps]
                 + [pltpu.HBM(z.shape, BF16) for z in lands] + [_sds((8, LANE), F32)])
    outs = pl.pallas_call(
        body, name=name, out_shape=out_shape, in_specs=[HBM] * (2 * n),
        out_specs=[SEM, SEM] + [HBM] * (2 * n) + [pl.BlockSpec(memory_space=pltpu.VMEM)],
        input_output_aliases={i: 2 + i for i in range(2 * n)},
        compiler_params=pltpu.CompilerParams(has_side_effects=EFFECT))(
            *[_in_hbm(p) for p in ps], *[_in_hbm(z) for z in lands])
    return outs[0], outs[1], outs[2:2 + n], outs[2 + n:2 + 2 * n], outs[-1]


def _rs_chips_wait(name, send, recv, ps, lands, afters):
    n = len(ps)

    def body(*refs):
        ins, zones = refs[:n], refs[n:2 * n]
        send_ref, recv_ref = refs[2 * n], refs[2 * n + 1]
        for cp in _rs_chips_copies(ins, zones, send_ref, recv_ref):
            cp.wait_send()
            cp.wait_recv()

    outs = pl.pallas_call(
        body, name=name, out_shape=[pltpu.HBM(p.shape, BF16) for p in ps] + [pltpu.HBM(z.shape, BF16) for z in lands],
        in_specs=[HBM] * (2 * n) + [SEM, SEM] + [ANY] * len(afters), out_specs=[HBM] * (2 * n),
        input_output_aliases={i: i for i in range(2 * n)},
        compiler_params=pltpu.CompilerParams(has_side_effects=EFFECT))(*ps, *lands, send, recv, *afters)
    return outs[:n], outs[n:]


def _sibling():
    x, y, c, _ = _place()
    return (x, y, 1 - c)


def _pair_send(name, dw, pf, by_cols=False):
    n4, r, c = dw.shape
    blk = (1, r, c // 2) if by_cols else (1, r // 2, c)
    idx = (lambda s, pf: (s, 0, 1 - pf[0])) if by_cols else (lambda s, pf: (s, 1 - pf[0], 0))

    def body(pf_ref, x_ref, got_ref, ssem, rsem):
        s = pl.program_id(0)
        cp = pltpu.make_async_remote_copy(src_ref=x_ref, dst_ref=got_ref.at[pl.ds(s, 1)], send_sem=ssem,
                                          recv_sem=rsem, device_id=_sibling(), device_id_type=MESH)
        cp.start()
        cp.wait_send()

        @pl.when(s == n4 - 1)
        def _():
            pltpu.make_async_remote_copy(src_ref=got_ref, dst_ref=got_ref, send_sem=ssem, recv_sem=rsem,
                                         device_id=_sibling(), device_id_type=MESH).wait_recv()

    dma = pltpu.SemaphoreType.DMA
    return pl.pallas_call(
        body, name=name, out_shape=_sds((n4,) + blk[1:], BF16),
        grid_spec=_grid_spec((n4,), [pl.BlockSpec(blk, idx)], ANY, scratch=[dma(()), dma(())]),
        compiler_params=pltpu.CompilerParams(dimension_semantics=("arbitrary",), has_side_effects=True,
                                             vmem_limit_bytes=VMEM_LIMIT))(pf, dw)


def _dw_pair(name, x, dy, down):
    lp = x.shape[0]
    tk = DFF // NCHIP
    if down:
        grid = (NCHIP, 2)
        x_spec = pl.BlockSpec((lp, tk), lambda s, j: (0, s))
        dy_spec = pl.BlockSpec((lp, 1024), lambda s, j: (0, j))
        o_spec = pl.BlockSpec((1, tk, 1024), lambda s, j: (s, 0, j))
        dw_shape, got_shape = (NCHIP, tk, D), (NCHIP, tk // 2, D)
    else:
        grid = (2, NCHIP)
        x_spec = pl.BlockSpec((lp, 1024), lambda i, s: (0, i))
        dy_spec = pl.BlockSpec((lp, tk), lambda i, s: (0, s))
        o_spec = pl.BlockSpec((1, 1024, tk), lambda i, s: (s, i, 0))
        dw_shape, got_shape = (NCHIP, D, tk), (NCHIP, D // 2, tk)

    def body(x_ref, dy_ref, o_ref, got_ref, ssem, rsem):
        g0, g1 = pl.program_id(0), pl.program_id(1)
        o_ref[0] = lax.dot_general(x_ref[...], dy_ref[...], _DN["tn"], preferred_element_type=F32).astype(BF16)
        _, _, c, _ = _place()

        def send(src, dst):
            cp = pltpu.make_async_remote_copy(src_ref=src, dst_ref=dst, send_sem=ssem, recv_sem=rsem,
                                              device_id=_sibling(), device_id_type=MESH)
            cp.start()
            cp.wait_send()

        if down:
            hr = tk // 2
            rows = pl.ds(pl.multiple_of((1 - c) * hr, 16), hr)
            send(o_ref.at[:, rows, :], got_ref.at[pl.ds(g0, 1), :, pl.ds(pl.multiple_of(g1 * 1024, LANE), 1024)])
        else:
            pl.when(g0 == 1 - c)(lambda: send(o_ref, got_ref.at[pl.ds(g1, 1)]))

        @pl.when(jnp.logical_and(g0 == grid[0] - 1, g1 == grid[1] - 1))
        def _():
            pltpu.make_async_remote_copy(src_ref=got_ref, dst_ref=got_ref, send_sem=ssem, recv_sem=rsem,
                                         device_id=_sibling(), device_id_type=MESH).wait_recv()

    dma = pltpu.SemaphoreType.DMA
    return pl.pallas_call(
        body, name=name, grid=grid, in_specs=[x_spec, dy_spec], out_specs=[o_spec, ANY],
        out_shape=[_sds(dw_shape, BF16), _sds(got_shape, BF16)], scratch_shapes=[dma(()), dma(())],
        compiler_params=pltpu.CompilerParams(dimension_semantics=("arbitrary", "arbitrary"), has_side_effects=True,
                                             vmem_limit_bytes=VMEM_LIMIT))(x, dy)


def _add4_join(name, p, got, pf, by_cols=False):
    n4, r2, c = p.shape
    if by_cols:
        tr, nch = r2, c // TCOL
        blk, idx = (r2, TCOL), (lambda i: (0, i))
    else:
        tr = _row_tile(r2, c, 4)
        nch = r2 // tr
        blk, idx = (tr, c), (lambda i: (i, 0))

    def body(pf_ref, p_ref, g_ref, mine_ref, theirs_ref, ssem, rsem):
        i = pl.program_id(0)
        s = p_ref[...].astype(F32)
        for k in range(3):
            s = s + g_ref[k].astype(F32)
        mine_ref[...] = s
        if by_cols:
            dst = theirs_ref.at[:, pl.ds(pl.multiple_of(i * TCOL, LANE), TCOL)]
        else:
            dst = theirs_ref.at[pl.ds(pl.multiple_of(i * tr, 8), tr), :]
        cp = pltpu.make_async_remote_copy(src_ref=mine_ref, dst_ref=dst,
                                          send_sem=ssem, recv_sem=rsem, device_id=_sibling(), device_id_type=MESH)
        cp.start()
        cp.wait_send()

        @pl.when(i == nch - 1)
        def _():
            pltpu.make_async_remote_copy(src_ref=theirs_ref, dst_ref=theirs_ref, send_sem=ssem, recv_sem=rsem,
                                         device_id=_sibling(), device_id_type=MESH).wait_recv()

    dma = pltpu.SemaphoreType.DMA
    o = _sds((r2, c), F32)
    return pl.pallas_call(
        body, name=name, out_shape=[o, o],
        grid_spec=_grid_spec((nch,), [pl.BlockSpec((None,) + blk, lambda i, pf: (pf[1],) + idx(i)),
                                      pl.BlockSpec((3,) + blk, lambda i, pf: (0,) + idx(i))],
                             [pl.BlockSpec(blk, lambda i, pf: idx(i)), ANY], scratch=[dma(()), dma(())]),
        compiler_params=pltpu.CompilerParams(dimension_semantics=("arbitrary",), has_side_effects=True,
                                             vmem_limit_bytes=VMEM_LIMIT))(pf, p, got)


def _half_of(g, slot, which, axis):
    half = g.shape[axis] // 2
    if axis == 1:
        return g.at[slot, pl.ds(which * half, half), :]
    return g.at[slot, :, pl.ds(which * half, half)]


def _ag_copies(arrs, split, send, recv):
    x, y, c, chips = _place()
    j = 2 * x + y
    cps = []
    for t, g in enumerate(arrs):
        piece = _half_of(g, j, c, split[t]) if split[t] else g.at[j]
        for k, chip in enumerate(chips):
            cps.append(pltpu.make_async_remote_copy(
                src_ref=piece, dst_ref=piece, send_sem=send.at[3 * t + k], recv_sem=recv.at[3 * t + k],
                device_id=(*chip, c), device_id_type=MESH))
    return cps


def _ag_start(name, groups, splits):
    sizes = [len(g) for g in groups]
    flat = [a for g in groups for a in g]
    n = len(flat)

    def body(*refs):
        ins = refs[:n]
        sems = refs[n:n + 2 * len(groups)]
        o = 0
        for gi, sz in enumerate(sizes):
            for cp in _ag_copies(ins[o:o + sz], splits[gi], sems[2 * gi], sems[2 * gi + 1]):
                cp.start()
            o += sz
        refs[-1][...] = jnp.zeros_like(refs[-1])

    dma = pltpu.SemaphoreType.DMA
    sem_shapes = [dma((3 * sz,)) for sz in sizes for _ in range(2)]
    outs = pl.pallas_call(
        body, name=name, out_shape=sem_shapes + [pltpu.HBM(a.shape, a.dtype) for a in flat] + [_sds((8, LANE), F32)],
        in_specs=[HBM] * n,
        out_specs=[SEM] * len(sem_shapes) + [HBM] * n + [pl.BlockSpec(memory_space=pltpu.VMEM)],
        input_output_aliases={i: len(sem_shapes) + i for i in range(n)},
        compiler_params=pltpu.CompilerParams(has_side_effects=EFFECT))(*[_in_hbm(a) for a in flat])
    sems, arrs, o = [], [], len(sem_shapes)
    for gi, sz in enumerate(sizes):
        sems.append((outs[2 * gi], outs[2 * gi + 1]))
        arrs.append(list(outs[o:o + sz]))
        o += sz
    return sems, arrs, outs[-1]


def _ag_wait(name, arrs, split, send, recv, afters):
    n = len(arrs)

    def body(*refs):
        for cp in _ag_copies(refs[:n], split, refs[n], refs[n + 1]):
            cp.wait_send()
            cp.wait_recv()

    return pl.pallas_call(
        body, name=name, out_shape=[pltpu.HBM(a.shape, a.dtype) for a in arrs],
        in_specs=[HBM] * n + [SEM, SEM] + [ANY] * len(afters), out_specs=[HBM] * n,
        input_output_aliases={i: i for i in range(n)},
        compiler_params=pltpu.CompilerParams(has_side_effects=EFFECT))(*arrs, send, recv, *afters)


def _ag_forward(name, arrs, axes):
    n = len(arrs)

    def half_shape(t):
        _, r, cc = arrs[t].shape
        return (r // 2, cc) if axes[t] == 1 else (r, cc // 2)

    def body(*refs):
        g = refs[n:2 * n]
        bufs = refs[2 * n:3 * n]
        fsend, frecv, lsem = refs[3 * n:]
        x, y, c, chips = _place()
        for t in range(n):
            pend = [None, None]
            for k, chip in enumerate(chips):
                jk = 2 * chip[0] + chip[1]
                slot = k % 2
                if pend[slot] is not None:
                    pend[slot].wait_send()
                part = _half_of(g[t], jk, c, axes[t])
                ld = pltpu.make_async_copy(part, bufs[t].at[slot], lsem.at[2 * t + slot])
                ld.start()
                ld.wait()
                cp = pltpu.make_async_remote_copy(
                    src_ref=bufs[t].at[slot], dst_ref=part, send_sem=fsend.at[2 * t + slot],
                    recv_sem=frecv.at[t], device_id=(x, y, 1 - c), device_id_type=MESH)
                cp.start()
                pend[slot] = cp
            for cp in pend:
                cp.wait_send()
        for t in range(n):
            hr, hc = half_shape(t)
            passed = g[t].at[pl.ds(0, 3), pl.ds(0, hr), pl.ds(0, hc)]
            pltpu.make_async_remote_copy(
                src_ref=passed, dst_ref=passed, send_sem=fsend.at[2 * t], recv_sem=frecv.at[t],
                device_id=(x, y, 1 - c), device_id_type=MESH).wait_recv()

    dma = pltpu.SemaphoreType.DMA
    scratch = [pltpu.VMEM((2,) + half_shape(t), BF16) for t in range(n)]
    scratch += [dma((2 * n,)), dma((n,)), dma((2 * n,))]
    return pl.pallas_call(
        body, name=name, in_specs=[ANY] * n, out_specs=[ANY] * n, out_shape=[_sds(a.shape, a.dtype) for a in arrs],
        scratch_shapes=scratch, input_output_aliases={t: t for t in range(n)},
        compiler_params=pltpu.CompilerParams(has_side_effects=True, vmem_limit_bytes=VMEM_LIMIT))(*arrs)


def _allgather_blocking(gs, smalls):
    nb, ns = len(gs), len(smalls)
    halves = [g.shape[2] // 2 for g in gs]

    def body(*refs):
        s_in = refs[nb:nb + ns]
        g = refs[nb + ns:2 * nb + ns]
        s_out = refs[2 * nb + ns:2 * (nb + ns)]
        scr = refs[2 * (nb + ns):]
        bufs = scr[:nb]
        send, recv, fsend, frecv, lsem, ssend, srecv, slsem = scr[nb:]
        x, y, c, chips = _place()
        j = 2 * x + y
        sends, slocal = [], []
        for t in range(nb):
            rows = pl.ds(c * halves[t], halves[t])
            piece = g[t].at[:, j, rows, :]
            for k, chip in enumerate(chips):
                cp = pltpu.make_async_remote_copy(
                    src_ref=piece, dst_ref=piece, send_sem=send.at[3 * t + k], recv_sem=recv.at[3 * t + k],
                    device_id=(*chip, c), device_id_type=MESH)
                cp.start()
                sends.append(cp)
        for t in range(ns):
            cp = pltpu.make_async_copy(s_in[t], s_out[t].at[j], slsem.at[t])
            cp.start()
            slocal.append(cp)
            for k, chip in enumerate(chips):
                cp = pltpu.make_async_remote_copy(
                    src_ref=s_in[t], dst_ref=s_out[t].at[j], send_sem=ssend.at[3 * t + k],
                    recv_sem=srecv.at[3 * t + k], device_id=(*chip, c), device_id_type=MESH)
                cp.start()
                sends.append(cp)
        for t in range(nb):
            rows = pl.ds(c * halves[t], halves[t])
            pend = [None, None]
            n = 0
            for k, chip in enumerate(chips):
                jk = 2 * chip[0] + chip[1]
                landed = g[t].at[:, jk, rows, :]
                pltpu.make_async_remote_copy(
                    src_ref=landed, dst_ref=landed, send_sem=send.at[3 * t + k], recv_sem=recv.at[3 * t + k],
                    device_id=(*chip, c), device_id_type=MESH).wait_recv()
                for l in range(DEPTH):
                    slot = n % 2
                    if pend[slot] is not None:
                        pend[slot].wait_send()
                    part = g[t].at[l, jk, rows, :]
                    ld = pltpu.make_async_copy(part, bufs[t].at[slot], lsem.at[2 * t + slot])
                    ld.start()
                    ld.wait()
                    cp = pltpu.make_async_remote_copy(
                        src_ref=bufs[t].at[slot], dst_ref=part, send_sem=fsend.at[2 * t + slot],
                        recv_sem=frecv.at[t], device_id=(x, y, 1 - c), device_id_type=MESH)
                    cp.start()
                    pend[slot] = cp
                    n += 1
            for cp in pend:
                cp.wait_send()
        for t in range(ns):
            for k, chip in enumerate(chips):
                jk = 2 * chip[0] + chip[1]
                landed = s_out[t].at[jk]
                pltpu.make_async_remote_copy(
                    src_ref=landed, dst_ref=landed, send_sem=ssend.at[3 * t + k], recv_sem=srecv.at[3 * t + k],
                    device_id=(*chip, c), device_id_type=MESH).wait_recv()
        for t in range(nb):
            passed = g[t].at[:, pl.ds(0, 3), pl.ds((1 - c) * halves[t], halves[t]), :]
            pltpu.make_async_remote_copy(
                src_ref=passed, dst_ref=passed, send_sem=fsend.at[2 * t], recv_sem=frecv.at[t],
                device_id=(x, y, 1 - c), device_id_type=MESH).wait_recv()
        for cp in sends:
            cp.wait_send()
        for cp in slocal:
            cp.wait()

    dma = pltpu.SemaphoreType.DMA
    out_shape = [_sds(g.shape, g.dtype) for g in gs] + [_sds((NCHIP,) + s.shape, s.dtype) for s in smalls]
    scratch = [pltpu.VMEM((2, halves[t], gs[t].shape[3]), BF16) for t in range(nb)]
    scratch += [dma((3 * nb,)), dma((3 * nb,)), dma((2 * nb,)), dma((nb,)), dma((2 * nb,)),
                dma((3 * ns,)), dma((3 * ns,)), dma((ns,))]
    return pl.pallas_call(
        body, name="allgather_weights", in_specs=[ANY] * (nb + ns), out_specs=[ANY] * (nb + ns),
        out_shape=out_shape, scratch_shapes=scratch, input_output_aliases={t: t for t in range(nb)},
        compiler_params=pltpu.CompilerParams(has_side_effects=True, vmem_limit_bytes=VMEM_LIMIT))(*gs, *smalls)


def _reduce_scatter(tag, dws, pf):
    ps = []
    for t, dw in enumerate(dws):
        got = _pair_send("rs_pair_%s_%d" % (tag, t), dw, pf)
        ps.append(_add2_bf16("rs_add2_%s_%d" % (tag, t), dw, got, pf))
    got2 = _rs_chips("rs_chips_" + tag, ps)
    return [_add4_join("rs_add4_%s_%d" % (tag, t), p, g2, pf) for t, (p, g2) in enumerate(zip(ps, got2))]


def _rs_begin(tag, dws, by_cols, pf):
    ps = []
    for t, dw in enumerate(dws):
        if isinstance(dw, (list, tuple)):
            dw, got = dw
        else:
            got = _pair_send("rs_pair_%s_%d" % (tag, t), dw, pf, by_cols[t])
        ps.append(_add2_bf16("rs_add2_%s_%d" % (tag, t), dw, got, pf, by_cols[t]))
    send, recv, ps_thru, lands, token = _rs_chips_start("rs_chips_start_" + tag, ps)
    return (tag, send, recv, ps_thru, lands, by_cols), token


def _rs_end(handle, afters, pf):
    tag, send, recv, ps, lands, by_cols = handle
    ps, got2 = _rs_chips_wait("rs_chips_wait_" + tag, send, recv, ps, lands, afters)
    return [_add4_join("rs_add4_%s_%d" % (tag, t), p, g2, pf, by_cols[t]) for t, (p, g2) in enumerate(zip(ps, got2))]


def _allreduce_small(pack):
    r = pack.shape[0]
    flips = [(fx, fy, fc) for fx in (0, 1) for fy in (0, 1) for fc in (0, 1)][1:]

    def body(p_ref, o_ref, gat, send, recv):
        x, y, c, _ = _place()
        me = 4 * x + 2 * y + c
        gat[me] = p_ref[...]
        cps = []
        for k, (fx, fy, fc) in enumerate(flips):
            peer = ((1 - x) if fx else x, (1 - y) if fy else y, (1 - c) if fc else c)
            cp = pltpu.make_async_remote_copy(
                src_ref=p_ref, dst_ref=gat.at[me], send_sem=send.at[k], recv_sem=recv.at[k],
                device_id=peer, device_id_type=MESH)
            cp.start()
            cps.append(cp)
        for k, (fx, fy, fc) in enumerate(flips):
            peer = ((1 - x) if fx else x, (1 - y) if fy else y, (1 - c) if fc else c)
            src = 4 * peer[0] + 2 * peer[1] + peer[2]
            pltpu.make_async_remote_copy(
                src_ref=p_ref, dst_ref=gat.at[src], send_sem=send.at[k], recv_sem=recv.at[k],
                device_id=peer, device_id_type=MESH).wait_recv()
        for cp in cps:
            cp.wait_send()
        s = gat[0]
        for d in range(1, 8):
            s = s + gat[d]
        o_ref[...] = s

    dma = pltpu.SemaphoreType.DMA
    vm = pl.BlockSpec(memory_space=pltpu.VMEM)
    return pl.pallas_call(
        body, name="allreduce_small", in_specs=[vm], out_specs=vm, out_shape=_sds((r, LANE), F32),
        scratch_shapes=[pltpu.VMEM((8, r, LANE), F32), dma((7,)), dma((7,))],
        compiler_params=pltpu.CompilerParams(has_side_effects=True))(pack)


def _in_weights(win_g):
    full = jnp.concatenate([win_g[s] for s in range(NCHIP)], axis=0)
    wqkv = full[:2048]
    og = full[2048:3072]
    gates = jnp.pad(full[3072:3080], ((0, LANE - 8), (0, 0)))
    u = full[3080:4104]
    gb = full[4104:5128]
    gc = full[5128:6152]
    return wqkv, jnp.stack([og, u, gb, gc]), gates


def _in_grads(dwqkv, dwe, dwgt):
    full = jnp.concatenate([dwqkv, dwe[0], dwgt[:8], dwe[1], dwe[2], dwe[3]], axis=0)
    sw = DIN // NCHIP
    return jnp.stack([full[s * sw:(s + 1) * sw] for s in range(NCHIP)])


def _layer_fwd(l, h, get_mix, get_ffn, small):
    lp = h.shape[0]
    th = lp // 2
    wqkv, we, wgt, wout_g = get_mix(h)
    nmw, bias, mnw, cw, nfw = small
    tag = "_l%d" % l
    hn = _norm_fwd("norm_mix" + tag, h, nmw)
    qkv = _mm("proj_qkv" + tag, "nt", hn, wqkv,
              pl.BlockSpec((lp, D), lambda i, j, k: (0, 0)), pl.BlockSpec((512, D), lambda i, j, k: (j, 0)),
              pl.BlockSpec((lp, 512), lambda i, j, k: (0, j)), _sds((lp, 2048), BF16), (1, 4, 1))
    e = _mm("proj_e" + tag, "nt", hn, we,
            pl.BlockSpec((lp, D), lambda i, j, k: (0, 0)), pl.BlockSpec((None, 512, D), lambda i, j, k: (j // 2, j % 2, 0)),
            pl.BlockSpec((None, lp, 512), lambda i, j, k: (j // 2, 0, j % 2)), _sds((4, lp, 1024), BF16), (1, 8, 1))
    gpre = _mm("proj_gates" + tag, "nt", hn, wgt,
               pl.BlockSpec((lp, D), lambda i, j, k: (0, 0)), pl.BlockSpec((LANE, D), lambda i, j, k: (0, 0)),
               pl.BlockSpec((lp, LANE), lambda i, j, k: (0, 0)), _sds((lp, LANE), F32), (1, 1, 1))
    grow, mcol, ccol = _gate_prep("gate_prep" + tag, gpre, bias)
    ht = _mlstm_fwd("mlstm_fwd" + tag, qkv, grow, mcol, ccol, 4)
    mix = _mix_fwd("mix_fwd" + tag, ht, e, mnw, cw)
    wout = wout_g.reshape(D, D)
    h1 = _mm("out_proj" + tag, "nn", mix, wout,
             pl.BlockSpec((None, th, 1024), lambda i, j, k: (k, i, 0)),
             pl.BlockSpec((1024, 1024), lambda i, j, k: (k, j)),
             pl.BlockSpec((th, 1024), lambda i, j, k: (i, j)), _sds((lp, D), F32), (2, 2, 2),
             acc_shape=(th, 1024), res=h, res_spec=pl.BlockSpec((th, 1024), lambda i, j, k: (i, j)))
    wg_g, wu_g, wd_g = get_ffn(h1)
    hf = _norm_fwd("norm_ffn" + tag, h1, nfw)
    g, u, a = _ffn_up("ffn_up" + tag, hf, wg_g, wu_g)
    tk = DFF // NCHIP
    h2 = _mm("ffn_down" + tag, "nn", a, wd_g,
             pl.BlockSpec((th, tk), lambda i, j, k: (i, k)),
             pl.BlockSpec((None, tk, 1024), lambda i, j, k: (k, 0, j)),
             pl.BlockSpec((th, 1024), lambda i, j, k: (i, j)), _sds((lp, D), F32), (2, 2, NCHIP),
             acc_shape=(th, 1024), res=h1, res_spec=pl.BlockSpec((th, 1024), lambda i, j, k: (i, j)))
    saved = (h, hn, qkv, e, gpre, grow, mcol, ccol, ht, mix, h1, hf, g, u, a)
    return h2, saved, (wqkv, we, wgt, wout_g, wg_g, wu_g, wd_g)


def _layer_bwd(l, dh2, dh2b, saved, wts, small, ffn_done):
    h, hn, qkv, e, gpre, grow, mcol, ccol, ht, mix, h1, hf, g, u, a = saved
    wqkv, we, wgt, wout_g, wg_g, wu_g, wd_g = wts
    nmw, bias, mnw, cw, nfw = small
    lp = h.shape[0]
    th = lp // 2
    tk = DFF // NCHIP
    tag = "_l%d" % l
    half_rows = lambda i, j, k: (i, j)

    dwd = _dw_pair("dw_down" + tag, a, dh2b, True)
    dg, du = _ffn_bwd_act("ffn_bwd_act" + tag, dh2b, wd_g, g, u)
    dwg = _dw_pair("dw_gate" + tag, hf, dg, False)
    dwu = _dw_pair("dw_up" + tag, hf, du, False)
    dhf = None
    for nm, dact, wfull in (("gate", dg, wg_g), ("up", du, wu_g)):
        dhf = _mm("dhf_%s%s" % (nm, tag), "nt", dact, wfull,
                  pl.BlockSpec((th, tk), lambda i, j, k: (i, k)),
                  pl.BlockSpec((None, 1024, tk), lambda i, j, k: (k, j, 0)),
                  pl.BlockSpec((th, 1024), half_rows), _sds((lp, D), F32), (2, 2, NCHIP), acc_shape=(th, 1024),
                  res=dhf, res_spec=None if dhf is None else pl.BlockSpec((th, 1024), half_rows))
    dh1, dh1b, dnfw = _norm_bwd("norm_ffn_bwd" + tag, dhf, h1, nfw, dh2)
    dh1b = ffn_done([dwg, dwu, dwd], dh1b)

    dwout = _mm("dw_out" + tag, "tn", mix, dh1b,
                pl.BlockSpec((None, lp, 1024), lambda i, j, k: (i, 0, 0)), pl.BlockSpec((lp, 1024), lambda i, j, k: (0, j)),
                pl.BlockSpec((1024, 1024), half_rows), _sds((D, D), BF16), (2, 2, 1))
    wout = wout_g.reshape(D, D)
    dmix = _mm("dmix" + tag, "nt", dh1b, wout,
               pl.BlockSpec((th, D), lambda i, j, k: (i, 0)), pl.BlockSpec((1024, D), lambda i, j, k: (j, 0)),
               pl.BlockSpec((th, 1024), half_rows), _sds((lp, D), F32), (2, 2, 1))
    dht, de, dmnw, dcw = _mix_bwd("mix_bwd" + tag, dmix, ht, e, mnw, cw)
    dq, dk, dv, dgrow, dfx = _mlstm_bwd("mlstm_bwd" + tag, qkv, grow, mcol, ccol, ht, dht, 8)
    dgp, dgpb, dbias = _gate_bwd("gate_bwd" + tag, gpre, bias, dgrow, dfx)
    del dgp
    dqkv = jnp.concatenate([dq, dk, dv], axis=1)

    hn_cols = pl.BlockSpec((lp, 1024), lambda i, j, k: (0, j))
    dwqkv = _mm("dw_qkv" + tag, "tn", dqkv, hn, pl.BlockSpec((lp, 1024), lambda i, j, k: (0, i)), hn_cols,
                pl.BlockSpec((1024, 1024), half_rows), _sds((2048, D), BF16), (2, 2, 1))
    dwe = _mm("dw_e" + tag, "tn", de, hn, pl.BlockSpec((None, lp, 1024), lambda i, j, k: (i, 0, 0)), hn_cols,
              pl.BlockSpec((None, 1024, 1024), lambda i, j, k: (i, 0, j)), _sds((4, 1024, D), BF16), (4, 2, 1))
    dwgt = _mm("dw_gates" + tag, "tn", dgpb, hn, pl.BlockSpec((lp, LANE), lambda i, j, k: (0, 0)), hn_cols,
               pl.BlockSpec((LANE, 1024), lambda i, j, k: (0, j)), _sds((LANE, D), BF16), (1, 2, 1))
    dwin = _in_grads(dwqkv, dwe, dwgt)

    dhn = _mm("dhn_qkv" + tag, "nn", dqkv, wqkv,
              pl.BlockSpec((th, 2048), lambda i, j, k: (i, 0)), pl.BlockSpec((2048, 1024), lambda i, j, k: (0, j)),
              pl.BlockSpec((th, 1024), half_rows), _sds((lp, D), F32), (2, 2, 1))
    dhn = _mm("dhn_e" + tag, "nn", de, we,
              pl.BlockSpec((None, th, 1024), lambda i, j, k: (k, i, 0)),
              pl.BlockSpec((None, 1024, 1024), lambda i, j, k: (k, 0, j)),
              pl.BlockSpec((th, 1024), half_rows), _sds((lp, D), F32), (2, 2, 4), acc_shape=(th, 1024),
              res=dhn, res_spec=pl.BlockSpec((th, 1024), half_rows))
    dhn = _mm("dhn_gates" + tag, "nn", dgpb, wgt,
              pl.BlockSpec((th, LANE), lambda i, j, k: (i, 0)), pl.BlockSpec((LANE, 1024), lambda i, j, k: (0, j)),
              pl.BlockSpec((th, 1024), half_rows), _sds((lp, D), F32), (2, 2, 1),
              res=dhn, res_spec=pl.BlockSpec((th, 1024), half_rows))
    dh0, dh0b, dnmw = _norm_bwd("norm_mix_bwd" + tag, dhn, h, nmw, dh1)

    pieces = [dwin, dwout.reshape(NCHIP, D // NCHIP, D)]
    smalls = (dnmw[0], dbias[0, :8], dcw[:3], dmnw[0], dnfw[0])
    return dh0, dh0b, pieces, smalls


def _pack_rows(parts):
    rows = []
    for p in parts:
        f = p.reshape(-1)
        pad = (-f.shape[0]) % LANE
        if pad:
            f = jnp.pad(f, (0, pad))
        rows.append(f.reshape(-1, LANE))
    r = jnp.concatenate(rows, axis=0)
    pad = (-r.shape[0]) % 8
    if pad:
        r = jnp.pad(r, ((0, pad), (0, 0)))
    return r


def _unpack_rows(pack, shapes):
    out, r0 = [], 0
    for s in shapes:
        n = 1
        for d in s:
            n *= d
        nr = -(-n // LANE)
        out.append(pack[r0:r0 + nr].reshape(-1)[:n].reshape(s))
        r0 += nr
    return out


def kernel(x, meta_tokens, norm_mix_w, w_in, b_gates, conv_w, mlstm_norm_w, w_out, norm_ffn_w, w_gate, w_up, w_down, norm_final_w, loss_target, m_meta_tokens, m_norm_mix_w, m_w_in, m_b_gates, m_conv_w, m_mlstm_norm_w, m_w_out, m_norm_ffn_w, m_w_gate, m_w_up, m_w_down, m_norm_final_w, v_meta_tokens, v_norm_mix_w, v_w_in, v_b_gates, v_conv_w, v_mlstm_norm_w, v_w_out, v_norm_ffn_w, v_w_gate, v_w_up, v_w_down, v_norm_final_w):
    seq = x.shape[1]
    n_real = N_META + seq
    lp = -(-n_real // LANE) * LANE
    xi, yi, ci = lax.axis_index("x"), lax.axis_index("y"), lax.axis_index("c")
    jchip = 2 * xi + yi
    pf = jnp.stack([ci, jchip, 2 * (1 - xi) + yi, 2 * xi + (1 - yi), 2 * (1 - xi) + (1 - yi)]).astype(jnp.int32)

    big = {"w_in": w_in, "w_out": w_out, "w_gate": w_gate, "w_up": w_up, "w_down": w_down}
    cast = {n: [_cast_into("cast_%s_l%d" % (n, l), w, l, pf) for l in range(DEPTH)] for n, w in big.items()
            if n != "w_in"}
    in_t = lambda a: jnp.transpose(a, (2, 0, 1))
    cast["w_in"] = [_cast_into_t("cast_w_in_l%d" % l, in_t(w_in), l, pf) for l in range(DEPTH)]
    conv_flat = jnp.pad(conv_w.reshape(DEPTH * 3, CW // NCHIP), ((0, 8 - DEPTH * 3), (0, 0)))

    def own_slot(a):
        return lax.dynamic_update_slice(jnp.zeros((NCHIP,) + a.shape, a.dtype), a[None], (jchip, 0, 0))

    groups, splits = [], []
    for l in range(DEPTH):
        groups.append([cast["w_in"][l], cast["w_out"][l]] + ([own_slot(meta_tokens), own_slot(conv_flat)] if l == 0 else []))
        splits.append([2, 1] + ([0, 0] if l == 0 else []))
        groups.append([cast[n][l] for n in ("w_gate", "w_up", "w_down")])
        splits.append([1, 1, 1])
    sems, arrs, _ = _ag_start("ag_start_0", groups[:1], splits[:1])
    sems_rest, arrs_rest, all_started = _ag_start("ag_start_1", groups[1:], splits[1:])
    sems, arrs = sems + sems_rest, arrs + arrs_rest

    def gathered(gi, afters):
        got = _ag_wait("ag_wait_%d" % gi, arrs[gi], splits[gi], sems[gi][0], sems[gi][1], afters)
        axes = [s for s in splits[gi] if s]
        return list(_ag_forward("ag_forward_%d" % gi, got[:len(axes)], axes)) + list(got[len(axes):])

    win0_g, wout0_g, meta_g, conv_g = gathered(0, [all_started])
    meta_full = jnp.concatenate([meta_g[s] for s in range(NCHIP)], axis=1)
    conv_full = jnp.concatenate([conv_g[s][:DEPTH * 3] for s in range(NCHIP)], axis=1)
    conv_full = conv_full.reshape(DEPTH, 3, CW)

    bias_rows = jnp.pad(b_gates, ((0, 0), (0, LANE - 8)))
    smalls = []
    for l in range(DEPTH):
        smalls.append((norm_mix_w[l][None], bias_rows[l][None], mlstm_norm_w[l][None],
                       jnp.pad(conv_full[l], ((0, 5), (0, 0))), norm_ffn_w[l][None]))

    h = jnp.concatenate([meta_full, x[0], jnp.zeros((lp - n_real, D), F32)], axis=0)
    saved, wts = [], []
    for l in range(DEPTH):
        def get_mix(h_in, l=l):
            win_g, wout_g = (win0_g, wout0_g) if l == 0 else gathered(2 * l, [h_in])
            return _in_weights(win_g) + (wout_g,)

        def get_ffn(h1, l=l):
            return tuple(gathered(2 * l + 1, [h1]))

        h, sv, wt = _layer_fwd(l, h, get_mix, get_ffn, smalls[l])
        saved.append(sv)
        wts.append(wt)
    tgt = jnp.pad(loss_target[0], ((N_META, lp - n_real), (0, 0)))
    dh, dhb, dnorm_final, loss_part = _loss_head(h, tgt, norm_final_w[None], n_real)

    names = ["w_in", "w_out", "w_gate", "w_up", "w_down"]
    params = {"w_in": (w_in, m_w_in, v_w_in), "w_out": (w_out, m_w_out, v_w_out), "w_gate": (w_gate, m_w_gate, v_w_gate),
              "w_up": (w_up, m_w_up, v_w_up), "w_down": (w_down, m_w_down, v_w_down)}
    big_out = {n: None for n in names}
    small_grads = [None] * DEPTH
    g_in = [None] * DEPTH

    def finish(l, group, handle, afters):
        for n, (g_mine, g_theirs) in zip(group, _rs_end(handle, afters, pf)):
            if n == "w_in":
                g_in[l] = (g_mine, g_theirs)
                continue
            w, m, v = params[n]
            big_out[n] = _adamw_layer("adamw_%s_l%d" % (n, l), l, g_mine, g_theirs, w, m, v, big_out[n], pf)
        return [big_out[n][3] for n in group if n != "w_in"]

    groups = []
    token = None
    for l in reversed(range(DEPTH)):
        def ffn_done(pieces, dh1b, l=l):
            handle, tok = _rs_begin("l%df" % l, pieces, [False] * 3, pf)
            groups.append((l, names[2:], handle))
            return dh1b + tok[0, 0].astype(BF16)

        dh, dhb, pieces, small_grads[l] = _layer_bwd(l, dh, dhb, saved[l], wts[l], smalls[l], ffn_done)
        handle, token = _rs_begin("l%dm" % l, pieces, [True, False], pf)
        groups.append((l, names[:2], handle))
        if l > 0:
            dhb = dhb + token[0, 0].astype(BF16)
    afters = [token]
    for l, group, handle in groups:
        afters = finish(l, group, handle, afters)
    big_out["w_in"] = [jnp.transpose(a, (1, 2, 0))
                       for a in _adamw_t("adamw_w_in", g_in, in_t(w_in), in_t(m_w_in), in_t(v_w_in), pf)]

    dnmw = jnp.stack([small_grads[l][0] for l in range(DEPTH)])
    dbias = jnp.stack([small_grads[l][1] for l in range(DEPTH)])
    dconv = jnp.stack([small_grads[l][2] for l in range(DEPTH)])
    dmnw = jnp.stack([small_grads[l][3] for l in range(DEPTH)])
    dnfw = jnp.stack([small_grads[l][4] for l in range(DEPTH)])
    part_shapes = [(N_META, D), (DEPTH, D), (DEPTH, 8), (DEPTH, 3, CW), (DEPTH, MW), (DEPTH, D), (D,), (LANE,)]
    pack = _pack_rows([dh[:N_META], dnmw, dbias, dconv, dmnw, dnfw, dnorm_final[0], loss_part[0]])
    tot = _unpack_rows(_allreduce_small(pack), part_shapes)
    g_meta_full, g_nmw, g_bias, g_conv_full, g_mnw, g_nfw, g_final, loss_row = tot
    mcols = D // NCHIP
    ccols = CW // NCHIP
    g_meta = lax.dynamic_slice_in_dim(g_meta_full, jchip * mcols, mcols, axis=1)
    g_conv = lax.dynamic_slice_in_dim(g_conv_full, jchip * ccols, ccols, axis=2)
    sm_g = [g_meta, g_nmw, g_bias, g_conv, g_mnw, g_nfw, g_final]
    sm_w = [meta_tokens, norm_mix_w, b_gates, conv_w, mlstm_norm_w, norm_ffn_w, norm_final_w]
    sm_m = [m_meta_tokens, m_norm_mix_w, m_b_gates, m_conv_w, m_mlstm_norm_w, m_norm_ffn_w, m_norm_final_w]
    sm_v = [v_meta_tokens, v_norm_mix_w, v_b_gates, v_conv_w, v_mlstm_norm_w, v_norm_ffn_w, v_norm_final_w]
    sm_shapes = [w.shape for w in sm_w]
    d_p, m_p, v_p = _adamw_flat(_pack_rows(sm_g), _pack_rows(sm_w), _pack_rows(sm_m), _pack_rows(sm_v))
    sm_d = _unpack_rows(d_p, sm_shapes)
    sm_nm = _unpack_rows(m_p, sm_shapes)
    sm_nv = _unpack_rows(v_p, sm_shapes)

    loss = loss_row[0]
    grad_x = dh[N_META:n_real][None]

    def ordered(sm, which):
        bo = {n: big_out[n][which] for n in names}
        return [sm[0], sm[1], bo["w_in"], sm[2], sm[3], sm[4], bo["w_out"], sm[5], bo["w_gate"], bo["w_up"], bo["w_down"], sm[6]]

    return (loss, grad_x, *ordered(sm_g, 0), *ordered(sm_d, 1), *ordered(sm_nm, 2), *ordered(sm_nv, 3))
```

```python
import functools

import jax
import jax.numpy as jnp
from jax import lax
from jax.experimental import pallas as pl
from jax.experimental.pallas import tpu as pltpu

F32 = jnp.float32
BF16 = jnp.bfloat16

D = 2048
N_META = 16
HEADS = 4
DQK = 128
DV = 256
MW = HEADS * DV
CW = D - MW
QKW = HEADS * DQK
DFF = 5632
DIN = 6152
NCHIP = 4
DEPTH = 2
GATE_CAP = 15.0
EPS = 1e-6
QK_SCALE = DQK ** -0.5
LANE = 128
VMEM_LIMIT = 60 * 1024 * 1024

ADAM_LR = 0.001
ADAM_B1 = 0.9
ADAM_B2 = 0.999
ADAM_EPS = 1e-08
ADAM_WD = 0.01
ADAM_STEP = 10

MESH = pl.DeviceIdType.MESH
ANY = pl.BlockSpec(memory_space=pl.ANY)


def _cp(sem):
    return pltpu.CompilerParams(dimension_semantics=sem, vmem_limit_bytes=VMEM_LIMIT)


def _sds(shape, dtype):
    return jax.ShapeDtypeStruct(shape, dtype)


_DN = {"nn": (((1,), (0,)), ((), ())), "nt": (((1,), (1,)), ((), ())), "tn": (((0,), (0,)), ((), ()))}


def _mm(name, kind, a, b, a_spec, b_spec, o_spec, out_shape, grid, acc_shape=None, res=None, res_spec=None):
    nk = grid[2]
    has_res = res is not None

    def body(*refs):
        if has_res:
            a_ref, b_ref, r_ref, o_ref = refs[:4]
        else:
            a_ref, b_ref, o_ref = refs[:3]
            r_ref = None
        p = lax.dot_general(a_ref[...], b_ref[...], _DN[kind], preferred_element_type=F32)
        if nk == 1:
            if r_ref is not None:
                p = p + r_ref[...]
            o_ref[...] = p.astype(o_ref.dtype)
        else:
            acc = refs[-1]
            k = pl.program_id(2)

            @pl.when(k == 0)
            def _():
                acc[...] = p

            @pl.when(k > 0)
            def _():
                acc[...] += p

            @pl.when(k == nk - 1)
            def _():
                r = acc[...]
                if r_ref is not None:
                    r = r + r_ref[...]
                o_ref[...] = r.astype(o_ref.dtype)

    ins = [a, b] + ([res] if has_res else [])
    in_specs = [a_spec, b_spec] + ([res_spec] if has_res else [])
    scratch = [pltpu.VMEM(acc_shape, F32)] if nk > 1 else []
    return pl.pallas_call(
        body, name=name, grid=grid, in_specs=in_specs, out_specs=o_spec, out_shape=out_shape,
        scratch_shapes=scratch, compiler_params=_cp(("parallel", "parallel", "arbitrary")))(*ins)


def _norm_fwd(name, h, w):
    lp = h.shape[0]
    tm = lp // 4

    def body(h_ref, w_ref, o_ref):
        x = h_ref[...]
        r = lax.rsqrt(jnp.mean(x * x, axis=1, keepdims=True) + EPS)
        o_ref[...] = (x * r * w_ref[...]).astype(BF16)

    return pl.pallas_call(
        body, name=name, grid=(4,),
        in_specs=[pl.BlockSpec((tm, D), lambda i: (i, 0)), pl.BlockSpec((1, D), lambda i: (0, 0))],
        out_specs=pl.BlockSpec((tm, D), lambda i: (i, 0)), out_shape=_sds((lp, D), BF16),
        compiler_params=_cp(("parallel",)))(h, w)


def _norm_bwd(name, dy, h, w, dres):
    lp = h.shape[0]
    tm = lp // 8

    def body(dy_ref, h_ref, w_ref, dres_ref, dh_ref, dhb_ref, dw_ref):
        x = h_ref[...]
        r = lax.rsqrt(jnp.mean(x * x, axis=1, keepdims=True) + EPS)
        xh = x * r
        dy_v = dy_ref[...]
        dxh = dy_v * w_ref[...]
        dx = r * (dxh - xh * jnp.mean(dxh * xh, axis=1, keepdims=True))
        dh = dres_ref[...] + dx
        dh_ref[...] = dh
        dhb_ref[...] = dh.astype(BF16)

        @pl.when(pl.program_id(0) == 0)
        def _():
            dw_ref[...] = jnp.zeros_like(dw_ref)

        dw_ref[0:1, :] += jnp.sum(dy_v * xh, axis=0, keepdims=True)

    row = pl.BlockSpec((tm, D), lambda i: (i, 0))
    return pl.pallas_call(
        body, name=name, grid=(8,),
        in_specs=[row, row, pl.BlockSpec((1, D), lambda i: (0, 0)), row],
        out_specs=[row, row, pl.BlockSpec((8, D), lambda i: (0, 0))],
        out_shape=[_sds((lp, D), F32), _sds((lp, D), BF16), _sds((8, D), F32)],
        compiler_params=_cp(("arbitrary",)))(dy, h, w, dres)


def _loss_head(h, tgt, w, n_real):
    lp = h.shape[0]
    tm = lp // 8

    def body(h_ref, t_ref, w_ref, dh_ref, dhb_ref, dw_ref, loss_ref):
        i = pl.program_id(0)
        x = h_ref[...]
        r = lax.rsqrt(jnp.mean(x * x, axis=1, keepdims=True) + EPS)
        xh = x * r
        wv = w_ref[...]
        row = i * tm + lax.broadcasted_iota(jnp.int32, (tm, 1), 0)
        valid = jnp.logical_and(row >= N_META, row < n_real)
        err = jnp.where(valid, xh * wv - t_ref[...], 0.0)
        dy_v = err * (1.0 / D)
        dxh = dy_v * wv
        dx = r * (dxh - xh * jnp.mean(dxh * xh, axis=1, keepdims=True))
        dh_ref[...] = dx
        dhb_ref[...] = dx.astype(BF16)

        @pl.when(i == 0)
        def _():
            dw_ref[...] = jnp.zeros_like(dw_ref)
            loss_ref[...] = jnp.zeros_like(loss_ref)

        dw_ref[0:1, :] += jnp.sum(dy_v * xh, axis=0, keepdims=True)
        part = jnp.sum(jnp.sum(err * err, axis=1, keepdims=True), axis=0, keepdims=True) * (0.5 / D)
        loss_ref[...] += jnp.broadcast_to(part, loss_ref.shape)

    row = pl.BlockSpec((tm, D), lambda i: (i, 0))
    return pl.pallas_call(
        body, name="loss_head", grid=(8,),
        in_specs=[row, row, pl.BlockSpec((1, D), lambda i: (0, 0))],
        out_specs=[row, row, pl.BlockSpec((8, D), lambda i: (0, 0)), pl.BlockSpec((8, LANE), lambda i: (0, 0))],
        out_shape=[_sds((lp, D), F32), _sds((lp, D), BF16), _sds((8, D), F32), _sds((8, LANE), F32)],
        compiler_params=_cp(("arbitrary",)))(h, tgt, w)


def _ffn_up(name, hf, wg4, wu4):
    lp = hf.shape[0]
    tm = lp // 4
    tn = DFF // NCHIP

    def body(x_ref, wg_ref, wu_ref, g_ref, u_ref, a_ref):
        x = x_ref[...]
        g = jnp.dot(x, wg_ref[...], preferred_element_type=F32)
        u = jnp.dot(x, wu_ref[...], preferred_element_type=F32)
        g_ref[...] = g.astype(BF16)
        u_ref[...] = u.astype(BF16)
        a_ref[...] = (g * jax.nn.sigmoid(g) * u).astype(BF16)

    wspec = pl.BlockSpec((None, D, tn), lambda j, i: (j, 0, 0))
    ospec = pl.BlockSpec((tm, tn), lambda j, i: (i, j))
    o = _sds((lp, DFF), BF16)
    return pl.pallas_call(
        body, name=name, grid=(NCHIP, 4),
        in_specs=[pl.BlockSpec((tm, D), lambda j, i: (i, 0)), wspec, wspec],
        out_specs=[ospec, ospec, ospec], out_shape=[o, o, o],
        compiler_params=_cp(("parallel", "parallel")))(hf, wg4, wu4)


def _ffn_bwd_act(name, dhb, wd4, g, u):
    lp = dhb.shape[0]
    tm = lp // 4
    tn = DFF // NCHIP

    def body(d_ref, w_ref, g_ref, u_ref, dg_ref, du_ref):
        da = lax.dot_general(d_ref[...], w_ref[...], _DN["nt"], preferred_element_type=F32)
        gv = g_ref[...].astype(F32)
        uv = u_ref[...].astype(F32)
        sg = jax.nn.sigmoid(gv)
        dg_ref[...] = (da * uv * (sg * (1.0 + gv * (1.0 - sg)))).astype(BF16)
        du_ref[...] = (da * (gv * sg)).astype(BF16)

    ospec = pl.BlockSpec((tm, tn), lambda j, i: (i, j))
    o = _sds((lp, DFF), BF16)
    return pl.pallas_call(
        body, name=name, grid=(NCHIP, 4),
        in_specs=[pl.BlockSpec((tm, D), lambda j, i: (i, 0)),
                  pl.BlockSpec((None, tn, D), lambda j, i: (j, 0, 0)), ospec, ospec],
        out_specs=[ospec, ospec], out_shape=[o, o],
        compiler_params=_cp(("parallel", "parallel")))(dhb, wd4, g, u)


def _shift_rows(x, d, row):
    return jnp.where(row >= d, pltpu.roll(x, d, axis=0), 0.0)


def _scan_steps(lp):
    d = 1
    while d < lp:
        yield d
        d *= 2


def _gate_values(pre):
    t = GATE_CAP * jnp.tanh(pre * (1.0 / GATE_CAP))
    lf = jnp.minimum(t, 0.0) - jnp.log(1.0 + jnp.exp(-jnp.abs(t)))
    return t, lf


def _gate_prep(name, gates_pre, bias):
    lp = gates_pre.shape[0]

    def body(p_ref, b_ref, grow_ref, m_ref, c_ref):
        pre = p_ref[...] + b_ref[...]
        lane = lax.broadcasted_iota(jnp.int32, (lp, LANE), 1)
        row = lax.broadcasted_iota(jnp.int32, (lp, LANE), 0)
        t, lf = _gate_values(pre)
        f = jnp.where(jnp.logical_and(lane >= HEADS, lane < 2 * HEADS), lf, 0.0)
        for d in _scan_steps(lp):
            f = f + _shift_rows(f, d, row)
        fs = pltpu.roll(f, LANE - HEADS, axis=1)
        g = jnp.where(lane < HEADS, t - fs, 0.0)
        m = g
        for d in _scan_steps(lp):
            m = jnp.maximum(m, jnp.where(row >= d, pltpu.roll(m, d, axis=0), m))
        grow_ref[...] = g.T
        m_ref[...] = m
        c_ref[...] = jnp.where(lane < HEADS, -fs - m, 0.0)

    full = pl.BlockSpec((lp, LANE), lambda: (0, 0))
    return pl.pallas_call(
        body, name=name, in_specs=[full, pl.BlockSpec((1, LANE), lambda: (0, 0))],
        out_specs=[pl.BlockSpec((LANE, lp), lambda: (0, 0)), full, full],
        out_shape=[_sds((LANE, lp), F32), _sds((lp, LANE), F32), _sds((lp, LANE), F32)],
        compiler_params=pltpu.CompilerParams(vmem_limit_bytes=VMEM_LIMIT))(gates_pre, bias)


def _pick_lane(blk, h):
    lane = lax.broadcasted_iota(jnp.int32, blk.shape, 1)
    return jnp.sum(jnp.where(lane == h, blk, 0.0), axis=1, keepdims=True)


def _mlstm_weights(q, k, grow, mcol, i, bq, nk):
    s = lax.dot_general(q, k, _DN["nt"], preferred_element_type=F32) * QK_SCALE
    row = i * bq + lax.broadcasted_iota(jnp.int32, (bq, 1), 0)
    col = lax.broadcasted_iota(jnp.int32, (1, nk), 1)
    a = jnp.where(col <= row, jnp.exp(jnp.minimum(grow - mcol, 0.0)), 0.0)
    return s, a


def _per_query_tile(i, nq, bq, lp, compute):
    for ii in range(nq):
        nk = min(lp, -(-((ii + 1) * bq) // LANE) * LANE)
        pl.when(i == ii)(functools.partial(compute, nk))


def _mlstm_fwd(name, qkv, grow, mcol_all, ccol_all, nq):
    lp = qkv.shape[0]
    bq = lp // nq

    def body(q_ref, k_ref, v_ref, grow_ref, m_ref, c_ref, o_ref):
        h = pl.program_id(0)
        i = pl.program_id(1)
        mcol = _pick_lane(m_ref[...], h)
        ccol = _pick_lane(c_ref[...], h)

        def compute(nk):
            grow_h = grow_ref[pl.ds(h, 1), 0:nk]
            s, a = _mlstm_weights(q_ref[...], k_ref[0:nk, :], grow_h, mcol, i, bq, nk)
            p = a * s
            den = jnp.sum(p, axis=1, keepdims=True)
            num = jnp.dot(p.astype(BF16), v_ref[0:nk, :], preferred_element_type=F32)
            o_ref[...] = num / jnp.maximum(jnp.abs(den), jnp.exp(ccol))

        _per_query_tile(i, nq, bq, lp, compute)

    return pl.pallas_call(
        body, name=name, grid=(HEADS, nq),
        in_specs=[pl.BlockSpec((bq, DQK), lambda h, i: (i, h)),
                  pl.BlockSpec((lp, DQK), lambda h, i: (0, HEADS + h)),
                  pl.BlockSpec((lp, DV), lambda h, i: (0, HEADS + h)),
                  pl.BlockSpec((8, lp), lambda h, i: (0, 0)),
                  pl.BlockSpec((bq, LANE), lambda h, i: (i, 0)),
                  pl.BlockSpec((bq, LANE), lambda h, i: (i, 0))],
        out_specs=pl.BlockSpec((bq, DV), lambda h, i: (i, h)),
        out_shape=_sds((lp, MW), F32),
        compiler_params=_cp(("parallel", "parallel")))(qkv, qkv, qkv, grow, mcol_all, ccol_all)


def _mlstm_bwd(name, qkv, grow, mcol_all, ccol_all, ht, dht, nq):
    lp = qkv.shape[0]
    bq = lp // nq

    def body(q_ref, k_ref, v_ref, grow_ref, m_ref, c_ref, ht_ref, dht_ref,
             dq_ref, dk_ref, dv_ref, dgrow_ref, dfx_ref, dkt_acc, dvt_acc):
        h = pl.program_id(0)
        i = pl.program_id(1)

        @pl.when(jnp.logical_and(h == 0, i == 0))
        def _():
            dgrow_ref[...] = jnp.zeros_like(dgrow_ref)
            dfx_ref[...] = jnp.zeros_like(dfx_ref)

        @pl.when(i == 0)
        def _():
            dkt_acc[...] = jnp.zeros_like(dkt_acc)
            dvt_acc[...] = jnp.zeros_like(dvt_acc)

        mcol = _pick_lane(m_ref[...], h)
        ccol = _pick_lane(c_ref[...], h)

        def compute(nk):
            q = q_ref[...]
            k = k_ref[0:nk, :]
            v = v_ref[0:nk, :]
            grow_h = grow_ref[pl.ds(h, 1), 0:nk]
            s, a = _mlstm_weights(q, k, grow_h, mcol, i, bq, nk)
            p = a * s
            den = jnp.sum(p, axis=1, keepdims=True)
            clamp = jnp.exp(ccol)
            active = jnp.abs(den) < clamp
            dd = jnp.maximum(jnp.abs(den), clamp)
            dht_v = dht_ref[...]
            hdh = jnp.sum(dht_v * ht_ref[...], axis=1, keepdims=True)
            dn = (dht_v / dd).astype(BF16)
            dden = jnp.where(active, 0.0, -(hdh / dd) * jnp.sign(den))
            dp = lax.dot_general(dn, v, _DN["nt"], preferred_element_type=F32) + dden
            rmat = dp * p
            dgrow_ref[pl.ds(h, 1), 0:nk] += jnp.sum(rmat, axis=0, keepdims=True)
            ds = (dp * a * QK_SCALE).astype(BF16)
            dq_ref[...] = jnp.dot(ds, k, preferred_element_type=F32).astype(BF16)
            dkt_acc[:, 0:nk] += lax.dot_general(q, ds, _DN["tn"], preferred_element_type=F32)
            dvt_acc[:, 0:nk] += lax.dot_general(dn, p.astype(BF16), _DN["tn"], preferred_element_type=F32)
            lane = lax.broadcasted_iota(jnp.int32, (bq, LANE), 1)
            r0 = pl.multiple_of(i * bq, 16)
            dfx_ref[pl.ds(r0, bq), :] += jnp.where(lane == h, jnp.sum(rmat, axis=1, keepdims=True), 0.0)

        _per_query_tile(i, nq, bq, lp, compute)

        @pl.when(i == nq - 1)
        def _():
            dk_ref[...] = dkt_acc[...].T.astype(BF16)
            dv_ref[...] = dvt_acc[...].T.astype(BF16)

    return pl.pallas_call(
        body, name=name, grid=(HEADS, nq),
        in_specs=[pl.BlockSpec((bq, DQK), lambda h, i: (i, h)),
                  pl.BlockSpec((lp, DQK), lambda h, i: (0, HEADS + h)),
                  pl.BlockSpec((lp, DV), lambda h, i: (0, HEADS + h)),
                  pl.BlockSpec((8, lp), lambda h, i: (0, 0)),
                  pl.BlockSpec((bq, LANE), lambda h, i: (i, 0)),
                  pl.BlockSpec((bq, LANE), lambda h, i: (i, 0)),
                  pl.BlockSpec((bq, DV), lambda h, i: (i, h)),
                  pl.BlockSpec((bq, DV), lambda h, i: (i, h))],
        out_specs=[pl.BlockSpec((bq, DQK), lambda h, i: (i, h)),
                   pl.BlockSpec((lp, DQK), lambda h, i: (0, h)),
                   pl.BlockSpec((lp, DV), lambda h, i: (0, h)),
                   pl.BlockSpec((LANE, lp), lambda h, i: (0, 0)),
                   pl.BlockSpec((lp, LANE), lambda h, i: (0, 0))],
        out_shape=[_sds((lp, QKW), BF16), _sds((lp, QKW), BF16), _sds((lp, MW), BF16),
                   _sds((LANE, lp), F32), _sds((lp, LANE), F32)],
        scratch_shapes=[pltpu.VMEM((DQK, lp), F32), pltpu.VMEM((DV, lp), F32)],
        compiler_params=_cp(("arbitrary", "arbitrary")))(qkv, qkv, qkv, grow, mcol_all, ccol_all, ht, dht)


def _gate_bwd(name, gates_pre, bias, dgrow, dfx):
    lp = gates_pre.shape[0]

    def body(p_ref, b_ref, dgrow_ref, dfx_ref, dg_ref, dgb_ref, db_ref):
        pre = p_ref[...] + b_ref[...]
        lane = lax.broadcasted_iota(jnp.int32, (lp, LANE), 1)
        row = lax.broadcasted_iota(jnp.int32, (lp, LANE), 0)
        th = jnp.tanh(pre * (1.0 / GATE_CAP))
        t = GATE_CAP * th
        dgc = jnp.where(lane < HEADS, dgrow_ref[...].T, 0.0)
        df = jnp.where(lane < HEADS, dfx_ref[...] - dgc, 0.0)
        for d in _scan_steps(lp):
            df = df + jnp.where(row < lp - d, pltpu.roll(df, lp - d, axis=0), 0.0)
        dlf = pltpu.roll(df, HEADS, axis=1)
        dt = jnp.where(lane < HEADS, dgc, dlf * jax.nn.sigmoid(-t))
        dpre = jnp.where(lane < 2 * HEADS, dt * (1.0 - th * th), 0.0)
        dg_ref[...] = dpre
        dgb_ref[...] = dpre.astype(BF16)
        db_ref[...] = jnp.broadcast_to(jnp.sum(dpre, axis=0, keepdims=True), db_ref.shape)

    full = pl.BlockSpec((lp, LANE), lambda: (0, 0))
    return pl.pallas_call(
        body, name=name,
        in_specs=[full, pl.BlockSpec((1, LANE), lambda: (0, 0)), pl.BlockSpec((LANE, lp), lambda: (0, 0)), full],
        out_specs=[full, full, pl.BlockSpec((8, LANE), lambda: (0, 0))],
        out_shape=[_sds((lp, LANE), F32), _sds((lp, LANE), BF16), _sds((8, LANE), F32)],
        compiler_params=pltpu.CompilerParams(vmem_limit_bytes=VMEM_LIMIT))(gates_pre, bias, dgrow, dfx)


CB = 256


def _e_specs(lp):
    return [pl.BlockSpec((None, lp, CB), functools.partial(lambda c, j: (c, 0, j), c)) for c in range(4)]


def _mix_fwd(name, ht, e, mnw, cw):
    lp = ht.shape[0]

    def body(ht_ref, og_ref, u_ref, gb_ref, gc_ref, mnw_ref, cw_ref, o_ref):
        x = ht_ref[...]
        r = lax.rsqrt(jnp.mean(x * x, axis=1, keepdims=True) + EPS)
        o_ref[0] = (jax.nn.sigmoid(og_ref[...].astype(F32)) * (x * r * mnw_ref[...])).astype(BF16)
        row = lax.broadcasted_iota(jnp.int32, (lp, CB), 0)
        a = gc_ref[...].astype(F32) * u_ref[...].astype(F32)
        conv = cw_ref[2:3, :] * a + cw_ref[1:2, :] * _shift_rows(a, 1, row) + cw_ref[0:1, :] * _shift_rows(a, 2, row)
        o_ref[1] = (gb_ref[...].astype(F32) * conv).astype(BF16)

    col = pl.BlockSpec((lp, CB), lambda j: (0, j))
    return pl.pallas_call(
        body, name=name, grid=(4,),
        in_specs=[col] + _e_specs(lp) + [pl.BlockSpec((1, CB), lambda j: (0, j)), pl.BlockSpec((8, CB), lambda j: (0, j))],
        out_specs=pl.BlockSpec((2, lp, CB), lambda j: (0, 0, j)), out_shape=_sds((2, lp, MW), BF16),
        compiler_params=_cp(("parallel",)))(ht, e, e, e, e, mnw, cw)


def _mix_bwd(name, dmix, ht, e, mnw, cw):
    lp = ht.shape[0]

    def body(dhm_ref, dhc_ref, ht_ref, og_ref, u_ref, gb_ref, gc_ref, mnw_ref, cw_ref,
             dht_ref, de_ref, dmnw_ref, dcw_ref):
        x = ht_ref[...]
        r = lax.rsqrt(jnp.mean(x * x, axis=1, keepdims=True) + EPS)
        xh = x * r
        w = mnw_ref[...]
        sg = jax.nn.sigmoid(og_ref[...].astype(F32))
        dhm = dhm_ref[...]
        de_ref[0] = (dhm * (xh * w) * (sg * (1.0 - sg))).astype(BF16)
        dn = dhm * sg
        dmnw_ref[...] = jnp.broadcast_to(jnp.sum(dn * xh, axis=0, keepdims=True), dmnw_ref.shape)
        dxh = dn * w
        dht_ref[...] = r * (dxh - xh * jnp.mean(dxh * xh, axis=1, keepdims=True))

        row = lax.broadcasted_iota(jnp.int32, (lp, CB), 0)
        uv = u_ref[...].astype(F32)
        gcv = gc_ref[...].astype(F32)
        gbv = gb_ref[...].astype(F32)
        a = gcv * uv
        a1 = _shift_rows(a, 1, row)
        a2 = _shift_rows(a, 2, row)
        dhc = dhc_ref[...]
        conv = cw_ref[2:3, :] * a + cw_ref[1:2, :] * a1 + cw_ref[0:1, :] * a2
        de_ref[2] = (dhc * conv).astype(BF16)
        dconv = dhc * gbv
        dcw_ref[...] = jnp.zeros_like(dcw_ref)
        dcw_ref[0:1, :] = jnp.sum(dconv * a2, axis=0, keepdims=True)
        dcw_ref[1:2, :] = jnp.sum(dconv * a1, axis=0, keepdims=True)
        dcw_ref[2:3, :] = jnp.sum(dconv * a, axis=0, keepdims=True)
        up1 = jnp.where(row < lp - 1, pltpu.roll(dconv, lp - 1, axis=0), 0.0)
        up2 = jnp.where(row < lp - 2, pltpu.roll(dconv, lp - 2, axis=0), 0.0)
        da = cw_ref[2:3, :] * dconv + cw_ref[1:2, :] * up1 + cw_ref[0:1, :] * up2
        de_ref[1] = (da * gcv).astype(BF16)
        de_ref[3] = (da * uv).astype(BF16)

    col = pl.BlockSpec((lp, CB), lambda j: (0, j))
    small = pl.BlockSpec((8, CB), lambda j: (0, j))
    return pl.pallas_call(
        body, name=name, grid=(4,),
        in_specs=[col, pl.BlockSpec((lp, CB), lambda j: (0, 4 + j)), col] + _e_specs(lp)
                 + [pl.BlockSpec((1, CB), lambda j: (0, j)), small],
        out_specs=[col, pl.BlockSpec((4, lp, CB), lambda j: (0, 0, j)), small, small],
        out_shape=[_sds((lp, MW), F32), _sds((4, lp, MW), BF16), _sds((8, MW), F32), _sds((8, CW), F32)],
        compiler_params=_cp(("parallel",)))(dmix, dmix, ht, e, e, e, e, mnw, cw)


def _row_tile(r, c, itemsize, budget=1536 * 1024, mult=16):
    best = None
    for t in range(mult, r + 1, mult):
        if r % t == 0 and t * c * itemsize <= budget:
            best = t
    if best is None:
        best = r
    return best


def _grid_spec(grid, in_specs, out_specs, scratch=()):
    return pltpu.PrefetchScalarGridSpec(num_scalar_prefetch=1, grid=grid, in_specs=in_specs,
                                        out_specs=out_specs, scratch_shapes=list(scratch))


def _cast_into(name, w, layer, pf):
    _, r, c = w.shape
    tr = _row_tile(r, c, 4)

    def body(pf_ref, x_ref, o_ref):
        o_ref[...] = x_ref[...].astype(BF16)

    return pl.pallas_call(
        body, name=name, out_shape=_sds((NCHIP, r, c), BF16),
        grid_spec=_grid_spec((r // tr,), [pl.BlockSpec((None, tr, c), lambda i, pf: (layer, i, 0))],
                             pl.BlockSpec((None, tr, c), lambda i, pf: (pf[1], i, 0))),
        compiler_params=_cp(("parallel",)))(pf, w)


TCOL = 256


def _cast_into_t(name, w_t, layer, pf):
    c, nl, r = w_t.shape

    def body(pf_ref, x_ref, o_ref):
        o_ref[...] = x_ref[:, layer, :].astype(BF16)

    return pl.pallas_call(
        body, name=name, out_shape=_sds((NCHIP, c, r), BF16),
        grid_spec=_grid_spec((r // TCOL,), [pl.BlockSpec((c, nl, TCOL), lambda i, pf: (0, 0, i))],
                             pl.BlockSpec((None, c, TCOL), lambda i, pf: (pf[1], 0, i))),
        compiler_params=_cp(("parallel",)))(pf, w_t)


def _add2_bf16(name, dw, got, pf, by_cols=False):
    n4, r2, c2 = got.shape

    def body(pf_ref, a_ref, b_ref, o_ref):
        o_ref[...] = (a_ref[...].astype(F32) + b_ref[...].astype(F32)).astype(BF16)

    if by_cols:
        nch = c2 // TCOL
        spec = pl.BlockSpec((None, r2, TCOL), lambda s, i, pf: (s, 0, i))
        mine = pl.BlockSpec((None, r2, TCOL), lambda s, i, pf: (s, 0, pf[0] * nch + i))
    else:
        tr = _row_tile(r2, c2, 4)
        nch = r2 // tr
        spec = pl.BlockSpec((None, tr, c2), lambda s, i, pf: (s, i, 0))
        mine = pl.BlockSpec((None, tr, c2), lambda s, i, pf: (s, pf[0] * nch + i, 0))
    return pl.pallas_call(
        body, name=name, out_shape=_sds((n4, r2, c2), BF16),
        grid_spec=_grid_spec((n4, nch), [mine, spec], spec),
        compiler_params=_cp(("parallel", "parallel")))(pf, dw, got)


def _adam_math(w, g, m, v):
    m2 = ADAM_B1 * m + (1.0 - ADAM_B1) * g
    v2 = ADAM_B2 * v + (1.0 - ADAM_B2) * (g * g)
    m_hat = m2 / (1.0 - ADAM_B1 ** ADAM_STEP)
    v_hat = v2 / (1.0 - ADAM_B2 ** ADAM_STEP)
    delta = -ADAM_LR * (m_hat / (jnp.sqrt(v_hat) + ADAM_EPS) + ADAM_WD * w)
    return delta, m2, v2


def _adamw_layer(name, layer, g_mine, g_theirs, w, m, v, prev, pf):
    _, r, c = w.shape
    r2 = r // 2
    tr = _row_tile(r2, c, 4, budget=1024 * 1024, mult=8)
    nch = r2 // tr
    n_alias = 0 if prev is None else 4

    def body(*refs):
        pf_ref, gm_ref, gt_ref, w_ref, m_ref, v_ref = refs[:6]
        go_ref, d_ref, mo_ref, vo_ref = refs[6 + n_alias:]
        mine = (pl.program_id(0) // nch) == pf_ref[0]
        gv = jnp.where(mine, gm_ref[...], gt_ref[...])
        delta, m2, v2 = _adam_math(w_ref[...], gv, m_ref[...], v_ref[...])
        go_ref[...] = gv
        d_ref[...] = delta
        mo_ref[...] = m2
        vo_ref[...] = v2

    slab = pl.BlockSpec((None, tr, c), lambda i, pf: (layer, i, 0))
    ins = [g_mine, g_theirs, w, m, v] + (list(prev) if prev is not None else [])
    in_specs = [pl.BlockSpec((tr, c), lambda i, pf: (jnp.clip(i - pf[0] * nch, 0, nch - 1), 0)),
                pl.BlockSpec((tr, c), lambda i, pf: (jnp.clip(i - (1 - pf[0]) * nch, 0, nch - 1), 0)),
                slab, slab, slab] + [ANY] * n_alias
    o = _sds(w.shape, F32)
    return pl.pallas_call(
        body, name=name, out_shape=[o] * 4, grid_spec=_grid_spec((2 * nch,), in_specs, [slab] * 4),
        input_output_aliases={6 + k: k for k in range(n_alias)},
        compiler_params=_cp(("parallel",)))(pf, *ins)


def _adamw_t(name, gs, w_t, m_t, v_t, pf):
    c, nl, r = w_t.shape
    ta = LANE
    nch = (r // 2) // ta

    def body(*refs):
        pf_ref = refs[0]
        g_refs = refs[1:1 + 2 * nl]
        w_ref, m_ref, v_ref, go_ref, d_ref, mo_ref, vo_ref = refs[1 + 2 * nl:]
        mine = (pl.program_id(0) // nch) == pf_ref[0]
        gv = jnp.stack([jnp.where(mine, g_refs[2 * l][...], g_refs[2 * l + 1][...]) for l in range(nl)], axis=1)
        delta, m2, v2 = _adam_math(w_ref[...], gv, m_ref[...], v_ref[...])
        go_ref[...] = gv
        d_ref[...] = delta
        mo_ref[...] = m2
        vo_ref[...] = v2

    both = pl.BlockSpec((c, nl, ta), lambda i, pf: (0, 0, i))
    g_specs = []
    for l in range(nl):
        g_specs += [pl.BlockSpec((c, ta), lambda i, pf: (0, jnp.clip(i - pf[0] * nch, 0, nch - 1))),
                    pl.BlockSpec((c, ta), lambda i, pf: (0, jnp.clip(i - (1 - pf[0]) * nch, 0, nch - 1)))]
    o = _sds(w_t.shape, F32)
    flat_g = [a for pair in gs for a in pair]
    return pl.pallas_call(
        body, name=name, out_shape=[o] * 4, grid_spec=_grid_spec((2 * nch,), g_specs + [both] * 3, [both] * 4),
        compiler_params=_cp(("parallel",)))(pf, *flat_g, w_t, m_t, v_t)


def _adamw_flat(g, w, m, v):
    def body(g_ref, w_ref, m_ref, v_ref, d_ref, mo_ref, vo_ref):
        delta, m2, v2 = _adam_math(w_ref[...], g_ref[...], m_ref[...], v_ref[...])
        d_ref[...] = delta
        mo_ref[...] = m2
        vo_ref[...] = v2

    o = _sds(w.shape, F32)
    return pl.pallas_call(body, name="adamw_small", out_shape=[o, o, o])(g, w, m, v)


def _place():
    x, y, c = lax.axis_index("x"), lax.axis_index("y"), lax.axis_index("c")
    chips = [(1 - x, y), (x, 1 - y), (1 - x, 1 - y)]
    return x, y, c, chips


def _rs_chips(name, ps):
    n = len(ps)

    def body(*refs):
        ins = refs[:n]
        got = refs[n:2 * n]
        send, recv = refs[2 * n:]
        x, y, c, chips = _place()
        cps = []
        for t in range(n):
            for k, chip in enumerate(chips):
                jk = 2 * chip[0] + chip[1]
                cp = pltpu.make_async_remote_copy(
                    src_ref=ins[t].at[jk], dst_ref=got[t].at[k],
                    send_sem=send.at[3 * t + k], recv_sem=recv.at[3 * t + k],
                    device_id=(*chip, c), device_id_type=MESH)
                cp.start()
                cps.append(cp)
        for cp in cps:
            cp.wait()

    dma = pltpu.SemaphoreType.DMA
    return pl.pallas_call(
        body, name=name, in_specs=[ANY] * n, out_specs=[ANY] * n,
        out_shape=[_sds((3,) + p.shape[1:], BF16) for p in ps],
        scratch_shapes=[dma((3 * n,)), dma((3 * n,))],
        compiler_params=pltpu.CompilerParams(has_side_effects=True))(*ps)


HBM = pl.BlockSpec(memory_space=pltpu.HBM)
SEM = pl.BlockSpec(memory_space=pltpu.SEMAPHORE)
EFFECT = pltpu.SideEffectType.DATAFLOW_SIDE_EFFECTING


def _in_hbm(a):
    return pltpu.with_memory_space_constraint(a, pltpu.HBM)


def _rs_chips_copies(ins, lands, send, recv):
    x, y, c, chips = _place()
    cps = []
    for t in range(len(ins)):
        for k, chip in enumerate(chips):
            jk = 2 * chip[0] + chip[1]
            cps.append(pltpu.make_async_remote_copy(
                src_ref=ins[t].at[jk], dst_ref=lands[t].at[k], send_sem=send.at[3 * t + k],
                recv_sem=recv.at[3 * t + k], device_id=(*chip, c), device_id_type=MESH))
    return cps


def _rs_chips_start(name, ps):
    n = len(ps)

    def body(*refs):
        ins, lands = refs[:n], refs[n:2 * n]
        send, recv = refs[2 * n], refs[2 * n + 1]
        token = refs[-1]
        for cp in _rs_chips_copies(ins, lands, send, recv):
            cp.start()
        token[...] = jnp.zeros_like(token)

    dma = pltpu.SemaphoreType.DMA
    lands = [lax.empty((3,) + p.shape[1:], BF16) for p in ps]
    out_shape = ([dma((3 * n,)), dma((3 * n,))] + [pltpu.HBM(p.shape, BF16) for p in ps]
                 + [pltpu.HBM(z.shape, BF16) for z in lands] + [_sds((8, LANE), F32)])
    outs = pl.pallas_call(
        body, name=name, out_shape=out_shape, in_specs=[HBM] * (2 * n),
        out_specs=[SEM, SEM] + [HBM] * (2 * n) + [pl.BlockSpec(memory_space=pltpu.VMEM)],
        input_output_aliases={i: 2 + i for i in range(2 * n)},
        compiler_params=pltpu.CompilerParams(has_side_effects=EFFECT))(
            *[_in_hbm(p) for p in ps], *[_in_hbm(z) for z in lands])
    return outs[0], outs[1], outs[2:2 + n], outs[2 + n:2 + 2 * n], outs[-1]


def _rs_chips_wait(name, send, recv, ps, lands, afters):
    n = len(ps)

    def body(*refs):
        ins, zones = refs[:n], refs[n:2 * n]
        send_ref, recv_ref = refs[2 * n], refs[2 * n + 1]
        for cp in _rs_chips_copies(ins, zones, send_ref, recv_ref):
            cp.wait_send()
            cp.wait_recv()

    outs = pl.pallas_call(
        body, name=name, out_shape=[pltpu.HBM(p.shape, BF16) for p in ps] + [pltpu.HBM(z.shape, BF16) for z in lands],
        in_specs=[HBM] * (2 * n) + [SEM, SEM] + [ANY] * len(afters), out_specs=[HBM] * (2 * n),
        input_output_aliases={i: i for i in range(2 * n)},
        compiler_params=pltpu.CompilerParams(has_side_effects=EFFECT))(*ps, *lands, send, recv, *afters)
    return outs[:n], outs[n:]


def _sibling():
    x, y, c, _ = _place()
    return (x, y, 1 - c)


def _pair_send(name, dw, pf, by_cols=False):
    n4, r, c = dw.shape
    blk = (1, r, c // 2) if by_cols else (1, r // 2, c)
    idx = (lambda s, pf: (s, 0, 1 - pf[0])) if by_cols else (lambda s, pf: (s, 1 - pf[0], 0))

    def body(pf_ref, x_ref, got_ref, ssem, rsem):
        s = pl.program_id(0)
        cp = pltpu.make_async_remote_copy(src_ref=x_ref, dst_ref=got_ref.at[pl.ds(s, 1)], send_sem=ssem,
                                          recv_sem=rsem, device_id=_sibling(), device_id_type=MESH)
        cp.start()
        cp.wait_send()

        @pl.when(s == n4 - 1)
        def _():
            pltpu.make_async_remote_copy(src_ref=got_ref, dst_ref=got_ref, send_sem=ssem, recv_sem=rsem,
                                         device_id=_sibling(), device_id_type=MESH).wait_recv()

    dma = pltpu.SemaphoreType.DMA
    return pl.pallas_call(
        body, name=name, out_shape=_sds((n4,) + blk[1:], BF16),
        grid_spec=_grid_spec((n4,), [pl.BlockSpec(blk, idx)], ANY, scratch=[dma(()), dma(())]),
        compiler_params=pltpu.CompilerParams(dimension_semantics=("arbitrary",), has_side_effects=True,
                                             vmem_limit_bytes=VMEM_LIMIT))(pf, dw)


def _dw_pair(name, x, dy, down):
    lp = x.shape[0]
    tk = DFF // NCHIP
    if down:
        grid = (NCHIP, 2)
        x_spec = pl.BlockSpec((lp, tk), lambda s, j: (0, s))
        dy_spec = pl.BlockSpec((lp, 1024), lambda s, j: (0, j))
        o_spec = pl.BlockSpec((1, tk, 1024), lambda s, j: (s, 0, j))
        dw_shape, got_shape = (NCHIP, tk, D), (NCHIP, tk // 2, D)
    else:
        grid = (2, NCHIP)
        x_spec = pl.BlockSpec((lp, 1024), lambda i, s: (0, i))
        dy_spec = pl.BlockSpec((lp, tk), lambda i, s: (0, s))
        o_spec = pl.BlockSpec((1, 1024, tk), lambda i, s: (s, i, 0))
        dw_shape, got_shape = (NCHIP, D, tk), (NCHIP, D // 2, tk)

    send_shape = (tk // 2, 1024) if down else (1024, tk)
    n_sends = 2 * NCHIP if down else NCHIP

    def body(x_ref, dy_ref, o_ref, got_ref, sbuf, ssem, rsem):
        g0, g1 = pl.program_id(0), pl.program_id(1)
        tile = lax.dot_general(x_ref[...], dy_ref[...], _DN["tn"], preferred_element_type=F32).astype(BF16)
        o_ref[0] = tile
        _, _, c, _ = _place()

        def send(n, part, dst):
            for s in (0, 1):
                cp = pltpu.make_async_remote_copy(src_ref=sbuf.at[pl.ds(s, 1)], dst_ref=dst, send_sem=ssem.at[s],
                                                  recv_sem=rsem, device_id=_sibling(), device_id_type=MESH)

                @pl.when(n % 2 == s)
                def _():
                    pl.when(n >= 2)(cp.wait_send)
                    sbuf[s] = part()
                    cp.start()

            @pl.when(n == n_sends - 1)
            def _():
                for s in (0, 1):
                    pltpu.make_async_remote_copy(src_ref=sbuf.at[pl.ds(s, 1)], dst_ref=dst, send_sem=ssem.at[s],
                                                 recv_sem=rsem, device_id=_sibling(), device_id_type=MESH).wait_send()

        if down:
            hr = tk // 2
            rows = pl.ds(pl.multiple_of((1 - c) * hr, 16), hr)
            send(g0 * 2 + g1, lambda: o_ref[0, rows, :],
                 got_ref.at[pl.ds(g0, 1), :, pl.ds(pl.multiple_of(g1 * 1024, LANE), 1024)])
        else:
            pl.when(g0 == 1 - c)(lambda: send(g1, lambda: tile, got_ref.at[pl.ds(g1, 1)]))

        @pl.when(jnp.logical_and(g0 == grid[0] - 1, g1 == grid[1] - 1))
        def _():
            pltpu.make_async_remote_copy(src_ref=got_ref, dst_ref=got_ref, send_sem=ssem.at[0], recv_sem=rsem,
                                         device_id=_sibling(), device_id_type=MESH).wait_recv()

    dma = pltpu.SemaphoreType.DMA
    return pl.pallas_call(
        body, name=name, grid=grid, in_specs=[x_spec, dy_spec], out_specs=[o_spec, ANY],
        out_shape=[_sds(dw_shape, BF16), _sds(got_shape, BF16)],
        scratch_shapes=[pltpu.VMEM((2,) + send_shape, BF16), dma((2,)), dma(())],
        compiler_params=pltpu.CompilerParams(dimension_semantics=("arbitrary", "arbitrary"), has_side_effects=True,
                                             vmem_limit_bytes=VMEM_LIMIT))(x, dy)


def _add4_join(name, p, got, pf, by_cols=False):
    n4, r2, c = p.shape
    if by_cols:
        tr, nch = r2, c // TCOL
        blk, idx = (r2, TCOL), (lambda i: (0, i))
    else:
        tr = _row_tile(r2, c, 4)
        nch = r2 // tr
        blk, idx = (tr, c), (lambda i: (i, 0))

    def body(pf_ref, p_ref, g_ref, mine_ref, theirs_ref, ssem, rsem):
        i = pl.program_id(0)
        s = p_ref[...].astype(F32)
        for k in range(3):
            s = s + g_ref[k].astype(F32)
        mine_ref[...] = s
        if by_cols:
            dst = theirs_ref.at[:, pl.ds(pl.multiple_of(i * TCOL, LANE), TCOL)]
        else:
            dst = theirs_ref.at[pl.ds(pl.multiple_of(i * tr, 8), tr), :]
        cp = pltpu.make_async_remote_copy(src_ref=mine_ref, dst_ref=dst,
                                          send_sem=ssem, recv_sem=rsem, device_id=_sibling(), device_id_type=MESH)
        cp.start()
        cp.wait_send()

        @pl.when(i == nch - 1)
        def _():
            pltpu.make_async_remote_copy(src_ref=theirs_ref, dst_ref=theirs_ref, send_sem=ssem, recv_sem=rsem,
                                         device_id=_sibling(), device_id_type=MESH).wait_recv()

    dma = pltpu.SemaphoreType.DMA
    o = _sds((r2, c), F32)
    return pl.pallas_call(
        body, name=name, out_shape=[o, o],
        grid_spec=_grid_spec((nch,), [pl.BlockSpec((None,) + blk, lambda i, pf: (pf[1],) + idx(i)),
                                      pl.BlockSpec((3,) + blk, lambda i, pf: (0,) + idx(i))],
                             [pl.BlockSpec(blk, lambda i, pf: idx(i)), ANY], scratch=[dma(()), dma(())]),
        compiler_params=pltpu.CompilerParams(dimension_semantics=("arbitrary",), has_side_effects=True,
                                             vmem_limit_bytes=VMEM_LIMIT))(pf, p, got)


def _half_of(g, slot, which, axis):
    half = g.shape[axis] // 2
    if axis == 1:
        return g.at[slot, pl.ds(which * half, half), :]
    return g.at[slot, :, pl.ds(which * half, half)]


def _ag_copies(arrs, split, send, recv):
    x, y, c, chips = _place()
    j = 2 * x + y
    cps = []
    for t, g in enumerate(arrs):
        piece = _half_of(g, j, c, split[t]) if split[t] else g.at[j]
        for k, chip in enumerate(chips):
            cps.append(pltpu.make_async_remote_copy(
                src_ref=piece, dst_ref=piece, send_sem=send.at[3 * t + k], recv_sem=recv.at[3 * t + k],
                device_id=(*chip, c), device_id_type=MESH))
    return cps


def _ag_start(name, groups, splits):
    sizes = [len(g) for g in groups]
    flat = [a for g in groups for a in g]
    n = len(flat)

    def body(*refs):
        ins = refs[:n]
        sems = refs[n:n + 2 * len(groups)]
        o = 0
        for gi, sz in enumerate(sizes):
            for cp in _ag_copies(ins[o:o + sz], splits[gi], sems[2 * gi], sems[2 * gi + 1]):
                cp.start()
            o += sz
        refs[-1][...] = jnp.zeros_like(refs[-1])

    dma = pltpu.SemaphoreType.DMA
    sem_shapes = [dma((3 * sz,)) for sz in sizes for _ in range(2)]
    outs = pl.pallas_call(
        body, name=name, out_shape=sem_shapes + [pltpu.HBM(a.shape, a.dtype) for a in flat] + [_sds((8, LANE), F32)],
        in_specs=[HBM] * n,
        out_specs=[SEM] * len(sem_shapes) + [HBM] * n + [pl.BlockSpec(memory_space=pltpu.VMEM)],
        input_output_aliases={i: len(sem_shapes) + i for i in range(n)},
        compiler_params=pltpu.CompilerParams(has_side_effects=EFFECT))(*[_in_hbm(a) for a in flat])
    sems, arrs, o = [], [], len(sem_shapes)
    for gi, sz in enumerate(sizes):
        sems.append((outs[2 * gi], outs[2 * gi + 1]))
        arrs.append(list(outs[o:o + sz]))
        o += sz
    return sems, arrs, outs[-1]


def _ag_wait(name, arrs, split, send, recv, afters):
    n = len(arrs)

    def body(*refs):
        for cp in _ag_copies(refs[:n], split, refs[n], refs[n + 1]):
            cp.wait_send()
            cp.wait_recv()

    return pl.pallas_call(
        body, name=name, out_shape=[pltpu.HBM(a.shape, a.dtype) for a in arrs],
        in_specs=[HBM] * n + [SEM, SEM] + [ANY] * len(afters), out_specs=[HBM] * n,
        input_output_aliases={i: i for i in range(n)},
        compiler_params=pltpu.CompilerParams(has_side_effects=EFFECT))(*arrs, send, recv, *afters)


def _ag_forward(name, arrs, axes):
    n = len(arrs)

    def half_shape(t):
        _, r, cc = arrs[t].shape
        return (r // 2, cc) if axes[t] == 1 else (r, cc // 2)

    def body(*refs):
        g = refs[n:2 * n]
        bufs = refs[2 * n:3 * n]
        fsend, frecv, lsem = refs[3 * n:]
        x, y, c, chips = _place()
        for t in range(n):
            pend = [None, None]
            for k, chip in enumerate(chips):
                jk = 2 * chip[0] + chip[1]
                slot = k % 2
                if pend[slot] is not None:
                    pend[slot].wait_send()
                part = _half_of(g[t], jk, c, axes[t])
                ld = pltpu.make_async_copy(part, bufs[t].at[slot], lsem.at[2 * t + slot])
                ld.start()
                ld.wait()
                cp = pltpu.make_async_remote_copy(
                    src_ref=bufs[t].at[slot], dst_ref=part, send_sem=fsend.at[2 * t + slot],
                    recv_sem=frecv.at[t], device_id=(x, y, 1 - c), device_id_type=MESH)
                cp.start()
                pend[slot] = cp
            for cp in pend:
                cp.wait_send()
        for t in range(n):
            hr, hc = half_shape(t)
            passed = g[t].at[pl.ds(0, 3), pl.ds(0, hr), pl.ds(0, hc)]
            pltpu.make_async_remote_copy(
                src_ref=passed, dst_ref=passed, send_sem=fsend.at[2 * t], recv_sem=frecv.at[t],
                device_id=(x, y, 1 - c), device_id_type=MESH).wait_recv()

    dma = pltpu.SemaphoreType.DMA
    scratch = [pltpu.VMEM((2,) + half_shape(t), BF16) for t in range(n)]
    scratch += [dma((2 * n,)), dma((n,)), dma((2 * n,))]
    return pl.pallas_call(
        body, name=name, in_specs=[ANY] * n, out_specs=[ANY] * n, out_shape=[_sds(a.shape, a.dtype) for a in arrs],
        scratch_shapes=scratch, input_output_aliases={t: t for t in range(n)},
        compiler_params=pltpu.CompilerParams(has_side_effects=True, vmem_limit_bytes=VMEM_LIMIT))(*arrs)


def _allgather_blocking(gs, smalls):
    nb, ns = len(gs), len(smalls)
    halves = [g.shape[2] // 2 for g in gs]

    def body(*refs):
        s_in = refs[nb:nb + ns]
        g = refs[nb + ns:2 * nb + ns]
        s_out = refs[2 * nb + ns:2 * (nb + ns)]
        scr = refs[2 * (nb + ns):]
        bufs = scr[:nb]
        send, recv, fsend, frecv, lsem, ssend, srecv, slsem = scr[nb:]
        x, y, c, chips = _place()
        j = 2 * x + y
        sends, slocal = [], []
        for t in range(nb):
            rows = pl.ds(c * halves[t], halves[t])
            piece = g[t].at[:, j, rows, :]
            for k, chip in enumerate(chips):
                cp = pltpu.make_async_remote_copy(
                    src_ref=piece, dst_ref=piece, send_sem=send.at[3 * t + k], recv_sem=recv.at[3 * t + k],
                    device_id=(*chip, c), device_id_type=MESH)
                cp.start()
                sends.append(cp)
        for t in range(ns):
            cp = pltpu.make_async_copy(s_in[t], s_out[t].at[j], slsem.at[t])
            cp.start()
            slocal.append(cp)
            for k, chip in enumerate(chips):
                cp = pltpu.make_async_remote_copy(
                    src_ref=s_in[t], dst_ref=s_out[t].at[j], send_sem=ssend.at[3 * t + k],
                    recv_sem=srecv.at[3 * t + k], device_id=(*chip, c), device_id_type=MESH)
                cp.start()
                sends.append(cp)
        for t in range(nb):
            rows = pl.ds(c * halves[t], halves[t])
            pend = [None, None]
            n = 0
            for k, chip in enumerate(chips):
                jk = 2 * chip[0] + chip[1]
                landed = g[t].at[:, jk, rows, :]
                pltpu.make_async_remote_copy(
                    src_ref=landed, dst_ref=landed, send_sem=send.at[3 * t + k], recv_sem=recv.at[3 * t + k],
                    device_id=(*chip, c), device_id_type=MESH).wait_recv()
                for l in range(DEPTH):
                    slot = n % 2
                    if pend[slot] is not None:
                        pend[slot].wait_send()
                    part = g[t].at[l, jk, rows, :]
                    ld = pltpu.make_async_copy(part, bufs[t].at[slot], lsem.at[2 * t + slot])
                    ld.start()
                    ld.wait()
                    cp = pltpu.make_async_remote_copy(
                        src_ref=bufs[t].at[slot], dst_ref=part, send_sem=fsend.at[2 * t + slot],
                        recv_sem=frecv.at[t], device_id=(x, y, 1 - c), device_id_type=MESH)
                    cp.start()
                    pend[slot] = cp
                    n += 1
            for cp in pend:
                cp.wait_send()
        for t in range(ns):
            for k, chip in enumerate(chips):
                jk = 2 * chip[0] + chip[1]
                landed = s_out[t].at[jk]
                pltpu.make_async_remote_copy(
                    src_ref=landed, dst_ref=landed, send_sem=ssend.at[3 * t + k], recv_sem=srecv.at[3 * t + k],
                    device_id=(*chip, c), device_id_type=MESH).wait_recv()
        for t in range(nb):
            passed = g[t].at[:, pl.ds(0, 3), pl.ds((1 - c) * halves[t], halves[t]), :]
            pltpu.make_async_remote_copy(
                src_ref=passed, dst_ref=passed, send_sem=fsend.at[2 * t], recv_sem=frecv.at[t],
                device_id=(x, y, 1 - c), device_id_type=MESH).wait_recv()
        for cp in sends:
            cp.wait_send()
        for cp in slocal:
            cp.wait()

    dma = pltpu.SemaphoreType.DMA
    out_shape = [_sds(g.shape, g.dtype) for g in gs] + [_sds((NCHIP,) + s.shape, s.dtype) for s in smalls]
    scratch = [pltpu.VMEM((2, halves[t], gs[t].shape[3]), BF16) for t in range(nb)]
    scratch += [dma((3 * nb,)), dma((3 * nb,)), dma((2 * nb,)), dma((nb,)), dma((2 * nb,)),
                dma((3 * ns,)), dma((3 * ns,)), dma((ns,))]
    return pl.pallas_call(
        body, name="allgather_weights", in_specs=[ANY] * (nb + ns), out_specs=[ANY] * (nb + ns),
        out_shape=out_shape, scratch_shapes=scratch, input_output_aliases={t: t for t in range(nb)},
        compiler_params=pltpu.CompilerParams(has_side_effects=True, vmem_limit_bytes=VMEM_LIMIT))(*gs, *smalls)


def _reduce_scatter(tag, dws, pf):
    ps = []
    for t, dw in enumerate(dws):
        got = _pair_send("rs_pair_%s_%d" % (tag, t), dw, pf)
        ps.append(_add2_bf16("rs_add2_%s_%d" % (tag, t), dw, got, pf))
    got2 = _rs_chips("rs_chips_" + tag, ps)
    return [_add4_join("rs_add4_%s_%d" % (tag, t), p, g2, pf) for t, (p, g2) in enumerate(zip(ps, got2))]


def _rs_begin(tag, dws, by_cols, pf):
    ps = []
    for t, dw in enumerate(dws):
        if isinstance(dw, (list, tuple)):
            dw, got = dw
        else:
            got = _pair_send("rs_pair_%s_%d" % (tag, t), dw, pf, by_cols[t])
        ps.append(_add2_bf16("rs_add2_%s_%d" % (tag, t), dw, got, pf, by_cols[t]))
    send, recv, ps_thru, lands, token = _rs_chips_start("rs_chips_start_" + tag, ps)
    return (tag, send, recv, ps_thru, lands, by_cols), token


def _rs_end(handle, afters, pf):
    tag, send, recv, ps, lands, by_cols = handle
    ps, got2 = _rs_chips_wait("rs_chips_wait_" + tag, send, recv, ps, lands, afters)
    return [_add4_join("rs_add4_%s_%d" % (tag, t), p, g2, pf, by_cols[t]) for t, (p, g2) in enumerate(zip(ps, got2))]


def _allreduce_small(pack):
    r = pack.shape[0]
    flips = [(fx, fy, fc) for fx in (0, 1) for fy in (0, 1) for fc in (0, 1)][1:]

    def body(p_ref, o_ref, gat, send, recv):
        x, y, c, _ = _place()
        me = 4 * x + 2 * y + c
        gat[me] = p_ref[...]
        cps = []
        for k, (fx, fy, fc) in enumerate(flips):
            peer = ((1 - x) if fx else x, (1 - y) if fy else y, (1 - c) if fc else c)
            cp = pltpu.make_async_remote_copy(
                src_ref=p_ref, dst_ref=gat.at[me], send_sem=send.at[k], recv_sem=recv.at[k],
                device_id=peer, device_id_type=MESH)
            cp.start()
            cps.append(cp)
        for k, (fx, fy, fc) in enumerate(flips):
            peer = ((1 - x) if fx else x, (1 - y) if fy else y, (1 - c) if fc else c)
            src = 4 * peer[0] + 2 * peer[1] + peer[2]
            pltpu.make_async_remote_copy(
                src_ref=p_ref, dst_ref=gat.at[src], send_sem=send.at[k], recv_sem=recv.at[k],
                device_id=peer, device_id_type=MESH).wait_recv()
        for cp in cps:
            cp.wait_send()
        s = gat[0]
        for d in range(1, 8):
            s = s + gat[d]
        o_ref[...] = s

    dma = pltpu.SemaphoreType.DMA
    vm = pl.BlockSpec(memory_space=pltpu.VMEM)
    return pl.pallas_call(
        body, name="allreduce_small", in_specs=[vm], out_specs=vm, out_shape=_sds((r, LANE), F32),
        scratch_shapes=[pltpu.VMEM((8, r, LANE), F32), dma((7,)), dma((7,))],
        compiler_params=pltpu.CompilerParams(has_side_effects=True))(pack)


def _in_weights(win_g):
    full = jnp.concatenate([win_g[s] for s in range(NCHIP)], axis=0)
    wqkv = full[:2048]
    og = full[2048:3072]
    gates = jnp.pad(full[3072:3080], ((0, LANE - 8), (0, 0)))
    u = full[3080:4104]
    gb = full[4104:5128]
    gc = full[5128:6152]
    return wqkv, jnp.stack([og, u, gb, gc]), gates


def _in_grads(dwqkv, dwe, dwgt):
    full = jnp.concatenate([dwqkv, dwe[0], dwgt[:8], dwe[1], dwe[2], dwe[3]], axis=0)
    sw = DIN // NCHIP
    return jnp.stack([full[s * sw:(s + 1) * sw] for s in range(NCHIP)])


def _layer_fwd(l, h, get_mix, get_ffn, small):
    lp = h.shape[0]
    th = lp // 2
    wqkv, we, wgt, wout_g = get_mix(h)
    nmw, bias, mnw, cw, nfw = small
    tag = "_l%d" % l
    hn = _norm_fwd("norm_mix" + tag, h, nmw)
    qkv = _mm("proj_qkv" + tag, "nt", hn, wqkv,
              pl.BlockSpec((lp, D), lambda i, j, k: (0, 0)), pl.BlockSpec((512, D), lambda i, j, k: (j, 0)),
              pl.BlockSpec((lp, 512), lambda i, j, k: (0, j)), _sds((lp, 2048), BF16), (1, 4, 1))
    e = _mm("proj_e" + tag, "nt", hn, we,
            pl.BlockSpec((lp, D), lambda i, j, k: (0, 0)), pl.BlockSpec((None, 512, D), lambda i, j, k: (j // 2, j % 2, 0)),
            pl.BlockSpec((None, lp, 512), lambda i, j, k: (j // 2, 0, j % 2)), _sds((4, lp, 1024), BF16), (1, 8, 1))
    gpre = _mm("proj_gates" + tag, "nt", hn, wgt,
               pl.BlockSpec((lp, D), lambda i, j, k: (0, 0)), pl.BlockSpec((LANE, D), lambda i, j, k: (0, 0)),
               pl.BlockSpec((lp, LANE), lambda i, j, k: (0, 0)), _sds((lp, LANE), F32), (1, 1, 1))
    grow, mcol, ccol = _gate_prep("gate_prep" + tag, gpre, bias)
    ht = _mlstm_fwd("mlstm_fwd" + tag, qkv, grow, mcol, ccol, 4)
    mix = _mix_fwd("mix_fwd" + tag, ht, e, mnw, cw)
    wout = wout_g.reshape(D, D)
    h1 = _mm("out_proj" + tag, "nn", mix, wout,
             pl.BlockSpec((None, th, 1024), lambda i, j, k: (k, i, 0)),
             pl.BlockSpec((1024, 1024), lambda i, j, k: (k, j)),
             pl.BlockSpec((th, 1024), lambda i, j, k: (i, j)), _sds((lp, D), F32), (2, 2, 2),
             acc_shape=(th, 1024), res=h, res_spec=pl.BlockSpec((th, 1024), lambda i, j, k: (i, j)))
    wg_g, wu_g, wd_g = get_ffn(h1)
    hf = _norm_fwd("norm_ffn" + tag, h1, nfw)
    g, u, a = _ffn_up("ffn_up" + tag, hf, wg_g, wu_g)
    tk = DFF // NCHIP
    h2 = _mm("ffn_down" + tag, "nn", a, wd_g,
             pl.BlockSpec((th, tk), lambda i, j, k: (i, k)),
             pl.BlockSpec((None, tk, 1024), lambda i, j, k: (k, 0, j)),
             pl.BlockSpec((th, 1024), lambda i, j, k: (i, j)), _sds((lp, D), F32), (2, 2, NCHIP),
             acc_shape=(th, 1024), res=h1, res_spec=pl.BlockSpec((th, 1024), lambda i, j, k: (i, j)))
    saved = (h, hn, qkv, e, gpre, grow, mcol, ccol, ht, mix, h1, hf, g, u, a)
    return h2, saved, (wqkv, we, wgt, wout_g, wg_g, wu_g, wd_g)


def _layer_bwd(l, dh2, dh2b, saved, wts, small, ffn_done):
    h, hn, qkv, e, gpre, grow, mcol, ccol, ht, mix, h1, hf, g, u, a = saved
    wqkv, we, wgt, wout_g, wg_g, wu_g, wd_g = wts
    nmw, bias, mnw, cw, nfw = small
    lp = h.shape[0]
    th = lp // 2
    tk = DFF // NCHIP
    tag = "_l%d" % l
    half_rows = lambda i, j, k: (i, j)

    dwd = _dw_pair("dw_down" + tag, a, dh2b, True)
    dg, du = _ffn_bwd_act("ffn_bwd_act" + tag, dh2b, wd_g, g, u)
    dwg = _dw_pair("dw_gate" + tag, hf, dg, False)
    dwu = _dw_pair("dw_up" + tag, hf, du, False)
    dhf = None
    for nm, dact, wfull in (("gate", dg, wg_g), ("up", du, wu_g)):
        dhf = _mm("dhf_%s%s" % (nm, tag), "nt", dact, wfull,
                  pl.BlockSpec((th, tk), lambda i, j, k: (i, k)),
                  pl.BlockSpec((None, 1024, tk), lambda i, j, k: (k, j, 0)),
                  pl.BlockSpec((th, 1024), half_rows), _sds((lp, D), F32), (2, 2, NCHIP), acc_shape=(th, 1024),
                  res=dhf, res_spec=None if dhf is None else pl.BlockSpec((th, 1024), half_rows))
    dh1, dh1b, dnfw = _norm_bwd("norm_ffn_bwd" + tag, dhf, h1, nfw, dh2)
    dh1b = ffn_done([dwg, dwu, dwd], dh1b)

    dwout = _mm("dw_out" + tag, "tn", mix, dh1b,
                pl.BlockSpec((None, lp, 1024), lambda i, j, k: (i, 0, 0)), pl.BlockSpec((lp, 1024), lambda i, j, k: (0, j)),
                pl.BlockSpec((1024, 1024), half_rows), _sds((D, D), BF16), (2, 2, 1))
    wout = wout_g.reshape(D, D)
    dmix = _mm("dmix" + tag, "nt", dh1b, wout,
               pl.BlockSpec((th, D), lambda i, j, k: (i, 0)), pl.BlockSpec((1024, D), lambda i, j, k: (j, 0)),
               pl.BlockSpec((th, 1024), half_rows), _sds((lp, D), F32), (2, 2, 1))
    dht, de, dmnw, dcw = _mix_bwd("mix_bwd" + tag, dmix, ht, e, mnw, cw)
    dq, dk, dv, dgrow, dfx = _mlstm_bwd("mlstm_bwd" + tag, qkv, grow, mcol, ccol, ht, dht, 8)
    dgp, dgpb, dbias = _gate_bwd("gate_bwd" + tag, gpre, bias, dgrow, dfx)
    del dgp
    dqkv = jnp.concatenate([dq, dk, dv], axis=1)

    hn_cols = pl.BlockSpec((lp, 1024), lambda i, j, k: (0, j))
    dwqkv = _mm("dw_qkv" + tag, "tn", dqkv, hn, pl.BlockSpec((lp, 1024), lambda i, j, k: (0, i)), hn_cols,
                pl.BlockSpec((1024, 1024), half_rows), _sds((2048, D), BF16), (2, 2, 1))
    dwe = _mm("dw_e" + tag, "tn", de, hn, pl.BlockSpec((None, lp, 1024), lambda i, j, k: (i, 0, 0)), hn_cols,
              pl.BlockSpec((None, 1024, 1024), lambda i, j, k: (i, 0, j)), _sds((4, 1024, D), BF16), (4, 2, 1))
    dwgt = _mm("dw_gates" + tag, "tn", dgpb, hn, pl.BlockSpec((lp, LANE), lambda i, j, k: (0, 0)), hn_cols,
               pl.BlockSpec((LANE, 1024), lambda i, j, k: (0, j)), _sds((LANE, D), BF16), (1, 2, 1))
    dwin = _in_grads(dwqkv, dwe, dwgt)

    dhn = _mm("dhn_qkv" + tag, "nn", dqkv, wqkv,
              pl.BlockSpec((th, 2048), lambda i, j, k: (i, 0)), pl.BlockSpec((2048, 1024), lambda i, j, k: (0, j)),
              pl.BlockSpec((th, 1024), half_rows), _sds((lp, D), F32), (2, 2, 1))
    dhn = _mm("dhn_e" + tag, "nn", de, we,
              pl.BlockSpec((None, th, 1024), lambda i, j, k: (k, i, 0)),
              pl.BlockSpec((None, 1024, 1024), lambda i, j, k: (k, 0, j)),
              pl.BlockSpec((th, 1024), half_rows), _sds((lp, D), F32), (2, 2, 4), acc_shape=(th, 1024),
              res=dhn, res_spec=pl.BlockSpec((th, 1024), half_rows))
    dhn = _mm("dhn_gates" + tag, "nn", dgpb, wgt,
              pl.BlockSpec((th, LANE), lambda i, j, k: (i, 0)), pl.BlockSpec((LANE, 1024), lambda i, j, k: (0, j)),
              pl.BlockSpec((th, 1024), half_rows), _sds((lp, D), F32), (2, 2, 1),
              res=dhn, res_spec=pl.BlockSpec((th, 1024), half_rows))
    dh0, dh0b, dnmw = _norm_bwd("norm_mix_bwd" + tag, dhn, h, nmw, dh1)

    pieces = [dwin, dwout.reshape(NCHIP, D // NCHIP, D)]
    smalls = (dnmw[0], dbias[0, :8], dcw[:3], dmnw[0], dnfw[0])
    return dh0, dh0b, pieces, smalls


def _pack_rows(parts):
    rows = []
    for p in parts:
        f = p.reshape(-1)
        pad = (-f.shape[0]) % LANE
        if pad:
            f = jnp.pad(f, (0, pad))
        rows.append(f.reshape(-1, LANE))
    r = jnp.concatenate(rows, axis=0)
    pad = (-r.shape[0]) % 8
    if pad:
        r = jnp.pad(r, ((0, pad), (0, 0)))
    return r


def _unpack_rows(pack, shapes):
    out, r0 = [], 0
    for s in shapes:
        n = 1
        for d in s:
            n *= d
        nr = -(-n // LANE)
        out.append(pack[r0:r0 + nr].reshape(-1)[:n].reshape(s))
        r0 += nr
    return out


def kernel(x, meta_tokens, norm_mix_w, w_in, b_gates, conv_w, mlstm_norm_w, w_out, norm_ffn_w, w_gate, w_up, w_down, norm_final_w, loss_target, m_meta_tokens, m_norm_mix_w, m_w_in, m_b_gates, m_conv_w, m_mlstm_norm_w, m_w_out, m_norm_ffn_w, m_w_gate, m_w_up, m_w_down, m_norm_final_w, v_meta_tokens, v_norm_mix_w, v_w_in, v_b_gates, v_conv_w, v_mlstm_norm_w, v_w_out, v_norm_ffn_w, v_w_gate, v_w_up, v_w_down, v_norm_final_w):
    seq = x.shape[1]
    n_real = N_META + seq
    lp = -(-n_real // LANE) * LANE
    xi, yi, ci = lax.axis_index("x"), lax.axis_index("y"), lax.axis_index("c")
    jchip = 2 * xi + yi
    pf = jnp.stack([ci, jchip, 2 * (1 - xi) + yi, 2 * xi + (1 - yi), 2 * (1 - xi) + (1 - yi)]).astype(jnp.int32)

    big = {"w_in": w_in, "w_out": w_out, "w_gate": w_gate, "w_up": w_up, "w_down": w_down}
    cast = {n: [_cast_into("cast_%s_l%d" % (n, l), w, l, pf) for l in range(DEPTH)] for n, w in big.items()
            if n != "w_in"}
    in_t = lambda a: jnp.transpose(a, (2, 0, 1))
    cast["w_in"] = [_cast_into_t("cast_w_in_l%d" % l, in_t(w_in), l, pf) for l in range(DEPTH)]
    conv_flat = jnp.pad(conv_w.reshape(DEPTH * 3, CW // NCHIP), ((0, 8 - DEPTH * 3), (0, 0)))

    def own_slot(a):
        return lax.dynamic_update_slice(jnp.zeros((NCHIP,) + a.shape, a.dtype), a[None], (jchip, 0, 0))

    groups, splits = [], []
    for l in range(DEPTH):
        groups.append([cast["w_in"][l], cast["w_out"][l]] + ([own_slot(meta_tokens), own_slot(conv_flat)] if l == 0 else []))
        splits.append([2, 1] + ([0, 0] if l == 0 else []))
        groups.append([cast[n][l] for n in ("w_gate", "w_up", "w_down")])
        splits.append([1, 1, 1])
    sems, arrs, _ = _ag_start("ag_start_0", groups[:1], splits[:1])
    sems_rest, arrs_rest, all_started = _ag_start("ag_start_1", groups[1:], splits[1:])
    sems, arrs = sems + sems_rest, arrs + arrs_rest

    def gathered(gi, afters):
        got = _ag_wait("ag_wait_%d" % gi, arrs[gi], splits[gi], sems[gi][0], sems[gi][1], afters)
        axes = [s for s in splits[gi] if s]
        return list(_ag_forward("ag_forward_%d" % gi, got[:len(axes)], axes)) + list(got[len(axes):])

    win0_g, wout0_g, meta_g, conv_g = gathered(0, [all_started])
    meta_full = jnp.concatenate([meta_g[s] for s in range(NCHIP)], axis=1)
    conv_full = jnp.concatenate([conv_g[s][:DEPTH * 3] for s in range(NCHIP)], axis=1)
    conv_full = conv_full.reshape(DEPTH, 3, CW)

    bias_rows = jnp.pad(b_gates, ((0, 0), (0, LANE - 8)))
    smalls = []
    for l in range(DEPTH):
        smalls.append((norm_mix_w[l][None], bias_rows[l][None], mlstm_norm_w[l][None],
                       jnp.pad(conv_full[l], ((0, 5), (0, 0))), norm_ffn_w[l][None]))

    h = jnp.concatenate([meta_full, x[0], jnp.zeros((lp - n_real, D), F32)], axis=0)
    saved, wts = [], []
    for l in range(DEPTH):
        def get_mix(h_in, l=l):
            win_g, wout_g = (win0_g, wout0_g) if l == 0 else gathered(2 * l, [h_in])
            return _in_weights(win_g) + (wout_g,)

        def get_ffn(h1, l=l):
            return tuple(gathered(2 * l + 1, [h1]))

        h, sv, wt = _layer_fwd(l, h, get_mix, get_ffn, smalls[l])
        saved.append(sv)
        wts.append(wt)
    tgt = jnp.pad(loss_target[0], ((N_META, lp - n_real), (0, 0)))
    dh, dhb, dnorm_final, loss_part = _loss_head(h, tgt, norm_final_w[None], n_real)

    names = ["w_in", "w_out", "w_gate", "w_up", "w_down"]
    params = {"w_in": (w_in, m_w_in, v_w_in), "w_out": (w_out, m_w_out, v_w_out), "w_gate": (w_gate, m_w_gate, v_w_gate),
              "w_up": (w_up, m_w_up, v_w_up), "w_down": (w_down, m_w_down, v_w_down)}
    big_out = {n: None for n in names}
    small_grads = [None] * DEPTH
    g_in = [None] * DEPTH

    def finish(l, group, handle, afters):
        for n, (g_mine, g_theirs) in zip(group, _rs_end(handle, afters, pf)):
            if n == "w_in":
                g_in[l] = (g_mine, g_theirs)
                continue
            w, m, v = params[n]
            big_out[n] = _adamw_layer("adamw_%s_l%d" % (n, l), l, g_mine, g_theirs, w, m, v, big_out[n], pf)
        return [big_out[n][3] for n in group if n != "w_in"]

    groups = []
    token = None
    for l in reversed(range(DEPTH)):
        def ffn_done(pieces, dh1b, l=l):
            handle, tok = _rs_begin("l%df" % l, pieces, [False] * 3, pf)
            groups.append((l, names[2:], handle))
            return dh1b + tok[0, 0].astype(BF16)

        dh, dhb, pieces, small_grads[l] = _layer_bwd(l, dh, dhb, saved[l], wts[l], smalls[l], ffn_done)
        handle, token = _rs_begin("l%dm" % l, pieces, [True, False], pf)
        groups.append((l, names[:2], handle))
        if l > 0:
            dhb = dhb + token[0, 0].astype(BF16)
    afters = [token]
    for l, group, handle in groups:
        afters = finish(l, group, handle, afters)
    big_out["w_in"] = [jnp.transpose(a, (1, 2, 0))
                       for a in _adamw_t("adamw_w_in", g_in, in_t(w_in), in_t(m_w_in), in_t(v_w_in), pf)]

    dnmw = jnp.stack([small_grads[l][0] for l in range(DEPTH)])
    dbias = jnp.stack([small_grads[l][1] for l in range(DEPTH)])
    dconv = jnp.stack([small_grads[l][2] for l in range(DEPTH)])
    dmnw = jnp.stack([small_grads[l][3] for l in range(DEPTH)])
    dnfw = jnp.stack([small_grads[l][4] for l in range(DEPTH)])
    part_shapes = [(N_META, D), (DEPTH, D), (DEPTH, 8), (DEPTH, 3, CW), (DEPTH, MW), (DEPTH, D), (D,), (LANE,)]
    pack = _pack_rows([dh[:N_META], dnmw, dbias, dconv, dmnw, dnfw, dnorm_final[0], loss_part[0]])
    tot = _unpack_rows(_allreduce_small(pack), part_shapes)
    g_meta_full, g_nmw, g_bias, g_conv_full, g_mnw, g_nfw, g_final, loss_row = tot
    mcols = D // NCHIP
    ccols = CW // NCHIP
    g_meta = lax.dynamic_slice_in_dim(g_meta_full, jchip * mcols, mcols, axis=1)
    g_conv = lax.dynamic_slice_in_dim(g_conv_full, jchip * ccols, ccols, axis=2)
    sm_g = [g_meta, g_nmw, g_bias, g_conv, g_mnw, g_nfw, g_final]
    sm_w = [meta_tokens, norm_mix_w, b_gates, conv_w, mlstm_norm_w, norm_ffn_w, norm_final_w]
    sm_m = [m_meta_tokens, m_norm_mix_w, m_b_gates, m_conv_w, m_mlstm_norm_w, m_norm_ffn_w, m_norm_final_w]
    sm_v = [v_meta_tokens, v_norm_mix_w, v_b_gates, v_conv_w, v_mlstm_norm_w, v_norm_ffn_w, v_norm_final_w]
    sm_shapes = [w.shape for w in sm_w]
    d_p, m_p, v_p = _adamw_flat(_pack_rows(sm_g), _pack_rows(sm_w), _pack_rows(sm_m), _pack_rows(sm_v))
    sm_d = _unpack_rows(d_p, sm_shapes)
    sm_nm = _unpack_rows(m_p, sm_shapes)
    sm_nv = _unpack_rows(v_p, sm_shapes)

    loss = loss_row[0]
    grad_x = dh[N_META:n_real][None]

    def ordered(sm, which):
        bo = {n: big_out[n][which] for n in names}
        return [sm[0], sm[1], bo["w_in"], sm[2], sm[3], sm[4], bo["w_out"], sm[5], bo["w_gate"], bo["w_up"], bo["w_down"], sm[6]]

    return (loss, grad_x, *ordered(sm_g, 0), *ordered(sm_d, 1), *ordered(sm_nm, 2), *ordered(sm_nv, 3))
```

```python
import functools

import jax
import jax.numpy as jnp
from jax import lax
from jax.experimental import pallas as pl
from jax.experimental.pallas import tpu as pltpu

F32 = jnp.float32
BF16 = jnp.bfloat16

D = 2048
N_META = 16
HEADS = 4
DQK = 128
DV = 256
MW = HEADS * DV
CW = D - MW
QKW = HEADS * DQK
DFF = 5632
DIN = 6152
NCHIP = 4
DEPTH = 2
GATE_CAP = 15.0
EPS = 1e-6
QK_SCALE = DQK ** -0.5
LANE = 128
VMEM_LIMIT = 60 * 1024 * 1024

ADAM_LR = 0.001
ADAM_B1 = 0.9
ADAM_B2 = 0.999
ADAM_EPS = 1e-08
ADAM_WD = 0.01
ADAM_STEP = 10

MESH = pl.DeviceIdType.MESH
ANY = pl.BlockSpec(memory_space=pl.ANY)


def _cp(sem):
    return pltpu.CompilerParams(dimension_semantics=sem, vmem_limit_bytes=VMEM_LIMIT)


def _sds(shape, dtype):
    return jax.ShapeDtypeStruct(shape, dtype)


_DN = {"nn": (((1,), (0,)), ((), ())), "nt": (((1,), (1,)), ((), ())), "tn": (((0,), (0,)), ((), ()))}


def _mm(name, kind, a, b, a_spec, b_spec, o_spec, out_shape, grid, acc_shape=None, res=None, res_spec=None):
    nk = grid[2]
    has_res = res is not None

    def body(*refs):
        if has_res:
            a_ref, b_ref, r_ref, o_ref = refs[:4]
        else:
            a_ref, b_ref, o_ref = refs[:3]
            r_ref = None
        p = lax.dot_general(a_ref[...], b_ref[...], _DN[kind], preferred_element_type=F32)
        if nk == 1:
            if r_ref is not None:
                p = p + r_ref[...]
            o_ref[...] = p.astype(o_ref.dtype)
        else:
            acc = refs[-1]
            k = pl.program_id(2)

            @pl.when(k == 0)
            def _():
                acc[...] = p

            @pl.when(k > 0)
            def _():
                acc[...] += p

            @pl.when(k == nk - 1)
            def _():
                r = acc[...]
                if r_ref is not None:
                    r = r + r_ref[...]
                o_ref[...] = r.astype(o_ref.dtype)

    ins = [a, b] + ([res] if has_res else [])
    in_specs = [a_spec, b_spec] + ([res_spec] if has_res else [])
    scratch = [pltpu.VMEM(acc_shape, F32)] if nk > 1 else []
    return pl.pallas_call(
        body, name=name, grid=grid, in_specs=in_specs, out_specs=o_spec, out_shape=out_shape,
        scratch_shapes=scratch, compiler_params=_cp(("parallel", "parallel", "arbitrary")))(*ins)


def _norm_fwd(name, h, w):
    lp = h.shape[0]
    tm = lp // 4

    def body(h_ref, w_ref, o_ref):
        x = h_ref[...]
        r = lax.rsqrt(jnp.mean(x * x, axis=1, keepdims=True) + EPS)
        o_ref[...] = (x * r * w_ref[...]).astype(BF16)

    return pl.pallas_call(
        body, name=name, grid=(4,),
        in_specs=[pl.BlockSpec((tm, D), lambda i: (i, 0)), pl.BlockSpec((1, D), lambda i: (0, 0))],
        out_specs=pl.BlockSpec((tm, D), lambda i: (i, 0)), out_shape=_sds((lp, D), BF16),
        compiler_params=_cp(("parallel",)))(h, w)


def _norm_bwd(name, dy, h, w, dres):
    lp = h.shape[0]
    tm = lp // 8

    def body(dy_ref, h_ref, w_ref, dres_ref, dh_ref, dhb_ref, dw_ref):
        x = h_ref[...]
        r = lax.rsqrt(jnp.mean(x * x, axis=1, keepdims=True) + EPS)
        xh = x * r
        dy_v = dy_ref[...]
        dxh = dy_v * w_ref[...]
        dx = r * (dxh - xh * jnp.mean(dxh * xh, axis=1, keepdims=True))
        dh = dres_ref[...] + dx
        dh_ref[...] = dh
        dhb_ref[...] = dh.astype(BF16)

        @pl.when(pl.program_id(0) == 0)
        def _():
            dw_ref[...] = jnp.zeros_like(dw_ref)

        dw_ref[0:1, :] += jnp.sum(dy_v * xh, axis=0, keepdims=True)

    row = pl.BlockSpec((tm, D), lambda i: (i, 0))
    return pl.pallas_call(
        body, name=name, grid=(8,),
        in_specs=[row, row, pl.BlockSpec((1, D), lambda i: (0, 0)), row],
        out_specs=[row, row, pl.BlockSpec((8, D), lambda i: (0, 0))],
        out_shape=[_sds((lp, D), F32), _sds((lp, D), BF16), _sds((8, D), F32)],
        compiler_params=_cp(("arbitrary",)))(dy, h, w, dres)


def _loss_head(h, tgt, w, n_real):
    lp = h.shape[0]
    tm = lp // 8

    def body(h_ref, t_ref, w_ref, dh_ref, dhb_ref, dw_ref, loss_ref):
        i = pl.program_id(0)
        x = h_ref[...]
        r = lax.rsqrt(jnp.mean(x * x, axis=1, keepdims=True) + EPS)
        xh = x * r
        wv = w_ref[...]
        row = i * tm + lax.broadcasted_iota(jnp.int32, (tm, 1), 0)
        valid = jnp.logical_and(row >= N_META, row < n_real)
        err = jnp.where(valid, xh * wv - t_ref[...], 0.0)
        dy_v = err * (1.0 / D)
        dxh = dy_v * wv
        dx = r * (dxh - xh * jnp.mean(dxh * xh, axis=1, keepdims=True))
        dh_ref[...] = dx
        dhb_ref[...] = dx.astype(BF16)

        @pl.when(i == 0)
        def _():
            dw_ref[...] = jnp.zeros_like(dw_ref)
            loss_ref[...] = jnp.zeros_like(loss_ref)

        dw_ref[0:1, :] += jnp.sum(dy_v * xh, axis=0, keepdims=True)
        part = jnp.sum(jnp.sum(err * err, axis=1, keepdims=True), axis=0, keepdims=True) * (0.5 / D)
        loss_ref[...] += jnp.broadcast_to(part, loss_ref.shape)

    row = pl.BlockSpec((tm, D), lambda i: (i, 0))
    return pl.pallas_call(
        body, name="loss_head", grid=(8,),
        in_specs=[row, row, pl.BlockSpec((1, D), lambda i: (0, 0))],
        out_specs=[row, row, pl.BlockSpec((8, D), lambda i: (0, 0)), pl.BlockSpec((8, LANE), lambda i: (0, 0))],
        out_shape=[_sds((lp, D), F32), _sds((lp, D), BF16), _sds((8, D), F32), _sds((8, LANE), F32)],
        compiler_params=_cp(("arbitrary",)))(h, tgt, w)


def _ffn_up(name, hf, wg4, wu4):
    lp = hf.shape[0]
    tm = lp // 4
    tn = DFF // NCHIP

    def body(x_ref, wg_ref, wu_ref, g_ref, u_ref, a_ref):
        x = x_ref[...]
        g = jnp.dot(x, wg_ref[...], preferred_element_type=F32)
        u = jnp.dot(x, wu_ref[...], preferred_element_type=F32)
        g_ref[...] = g.astype(BF16)
        u_ref[...] = u.astype(BF16)
        a_ref[...] = (g * jax.nn.sigmoid(g) * u).astype(BF16)

    wspec = pl.BlockSpec((None, D, tn), lambda j, i: (j, 0, 0))
    ospec = pl.BlockSpec((tm, tn), lambda j, i: (i, j))
    o = _sds((lp, DFF), BF16)
    return pl.pallas_call(
        body, name=name, grid=(NCHIP, 4),
        in_specs=[pl.BlockSpec((tm, D), lambda j, i: (i, 0)), wspec, wspec],
        out_specs=[ospec, ospec, ospec], out_shape=[o, o, o],
        compiler_params=_cp(("parallel", "parallel")))(hf, wg4, wu4)


def _ffn_bwd_act(name, dhb, wd4, g, u):
    lp = dhb.shape[0]
    tm = lp // 4
    tn = DFF // NCHIP

    def body(d_ref, w_ref, g_ref, u_ref, dg_ref, du_ref):
        da = lax.dot_general(d_ref[...], w_ref[...], _DN["nt"], preferred_element_type=F32)
        gv = g_ref[...].astype(F32)
        uv = u_ref[...].astype(F32)
        sg = jax.nn.sigmoid(gv)
        dg_ref[...] = (da * uv * (sg * (1.0 + gv * (1.0 - sg)))).astype(BF16)
        du_ref[...] = (da * (gv * sg)).astype(BF16)

    ospec = pl.BlockSpec((tm, tn), lambda j, i: (i, j))
    o = _sds((lp, DFF), BF16)
    return pl.pallas_call(
        body, name=name, grid=(NCHIP, 4),
        in_specs=[pl.BlockSpec((tm, D), lambda j, i: (i, 0)),
                  pl.BlockSpec((None, tn, D), lambda j, i: (j, 0, 0)), ospec, ospec],
        out_specs=[ospec, ospec], out_shape=[o, o],
        compiler_params=_cp(("parallel", "parallel")))(dhb, wd4, g, u)


def _shift_rows(x, d, row):
    return jnp.where(row >= d, pltpu.roll(x, d, axis=0), 0.0)


def _scan_steps(lp):
    d = 1
    while d < lp:
        yield d
        d *= 2


def _gate_values(pre):
    t = GATE_CAP * jnp.tanh(pre * (1.0 / GATE_CAP))
    lf = jnp.minimum(t, 0.0) - jnp.log(1.0 + jnp.exp(-jnp.abs(t)))
    return t, lf


def _gate_prep(name, gates_pre, bias):
    lp = gates_pre.shape[0]

    def body(p_ref, b_ref, grow_ref, m_ref, c_ref):
        pre = p_ref[...] + b_ref[...]
        lane = lax.broadcasted_iota(jnp.int32, (lp, LANE), 1)
        row = lax.broadcasted_iota(jnp.int32, (lp, LANE), 0)
        t, lf = _gate_values(pre)
        f = jnp.where(jnp.logical_and(lane >= HEADS, lane < 2 * HEADS), lf, 0.0)
        for d in _scan_steps(lp):
            f = f + _shift_rows(f, d, row)
        fs = pltpu.roll(f, LANE - HEADS, axis=1)
        g = jnp.where(lane < HEADS, t - fs, 0.0)
        m = g
        for d in _scan_steps(lp):
            m = jnp.maximum(m, jnp.where(row >= d, pltpu.roll(m, d, axis=0), m))
        grow_ref[...] = g.T
        m_ref[...] = m
        c_ref[...] = jnp.where(lane < HEADS, -fs - m, 0.0)

    full = pl.BlockSpec((lp, LANE), lambda: (0, 0))
    return pl.pallas_call(
        body, name=name, in_specs=[full, pl.BlockSpec((1, LANE), lambda: (0, 0))],
        out_specs=[pl.BlockSpec((LANE, lp), lambda: (0, 0)), full, full],
        out_shape=[_sds((LANE, lp), F32), _sds((lp, LANE), F32), _sds((lp, LANE), F32)],
        compiler_params=pltpu.CompilerParams(vmem_limit_bytes=VMEM_LIMIT))(gates_pre, bias)


def _pick_lane(blk, h):
    lane = lax.broadcasted_iota(jnp.int32, blk.shape, 1)
    return jnp.sum(jnp.where(lane == h, blk, 0.0), axis=1, keepdims=True)


def _mlstm_weights(q, k, grow, mcol, i, bq, nk):
    s = lax.dot_general(q, k, _DN["nt"], preferred_element_type=F32) * QK_SCALE
    row = i * bq + lax.broadcasted_iota(jnp.int32, (bq, 1), 0)
    col = lax.broadcasted_iota(jnp.int32, (1, nk), 1)
    a = jnp.where(col <= row, jnp.exp(jnp.minimum(grow - mcol, 0.0)), 0.0)
    return s, a


def _per_query_tile(i, nq, bq, lp, compute):
    for ii in range(nq):
        nk = min(lp, -(-((ii + 1) * bq) // LANE) * LANE)
        pl.when(i == ii)(functools.partial(compute, nk))


def _mlstm_fwd(name, qkv, grow, mcol_all, ccol_all, nq):
    lp = qkv.shape[0]
    bq = lp // nq

    def body(q_ref, k_ref, v_ref, grow_ref, m_ref, c_ref, o_ref):
        h = pl.program_id(0)
        i = pl.program_id(1)
        mcol = _pick_lane(m_ref[...], h)
        ccol = _pick_lane(c_ref[...], h)

        def compute(nk):
            grow_h = grow_ref[pl.ds(h, 1), 0:nk]
            s, a = _mlstm_weights(q_ref[...], k_ref[0:nk, :], grow_h, mcol, i, bq, nk)
            p = a * s
            den = jnp.sum(p, axis=1, keepdims=True)
            num = jnp.dot(p.astype(BF16), v_ref[0:nk, :], preferred_element_type=F32)
            o_ref[...] = num / jnp.maximum(jnp.abs(den), jnp.exp(ccol))

        _per_query_tile(i, nq, bq, lp, compute)

    return pl.pallas_call(
        body, name=name, grid=(HEADS, nq),
        in_specs=[pl.BlockSpec((bq, DQK), lambda h, i: (i, h)),
                  pl.BlockSpec((lp, DQK), lambda h, i: (0, HEADS + h)),
                  pl.BlockSpec((lp, DV), lambda h, i: (0, HEADS + h)),
                  pl.BlockSpec((8, lp), lambda h, i: (0, 0)),
                  pl.BlockSpec((bq, LANE), lambda h, i: (i, 0)),
                  pl.BlockSpec((bq, LANE), lambda h, i: (i, 0))],
        out_specs=pl.BlockSpec((bq, DV), lambda h, i: (i, h)),
        out_shape=_sds((lp, MW), F32),
        compiler_params=_cp(("parallel", "parallel")))(qkv, qkv, qkv, grow, mcol_all, ccol_all)


def _mlstm_bwd(name, qkv, grow, mcol_all, ccol_all, ht, dht, nq):
    lp = qkv.shape[0]
    bq = lp // nq

    def body(q_ref, k_ref, v_ref, grow_ref, m_ref, c_ref, ht_ref, dht_ref,
             dq_ref, dk_ref, dv_ref, dgrow_ref, dfx_ref, dkt_acc, dvt_acc):
        h = pl.program_id(0)
        i = pl.program_id(1)

        @pl.when(jnp.logical_and(h == 0, i == 0))
        def _():
            dgrow_ref[...] = jnp.zeros_like(dgrow_ref)
            dfx_ref[...] = jnp.zeros_like(dfx_ref)

        @pl.when(i == 0)
        def _():
            dkt_acc[...] = jnp.zeros_like(dkt_acc)
            dvt_acc[...] = jnp.zeros_like(dvt_acc)

        mcol = _pick_lane(m_ref[...], h)
        ccol = _pick_lane(c_ref[...], h)

        def compute(nk):
            q = q_ref[...]
            k = k_ref[0:nk, :]
            v = v_ref[0:nk, :]
            grow_h = grow_ref[pl.ds(h, 1), 0:nk]
            s, a = _mlstm_weights(q, k, grow_h, mcol, i, bq, nk)
            p = a * s
            den = jnp.sum(p, axis=1, keepdims=True)
            clamp = jnp.exp(ccol)
            active = jnp.abs(den) < clamp
            dd = jnp.maximum(jnp.abs(den), clamp)
            dht_v = dht_ref[...]
            hdh = jnp.sum(dht_v * ht_ref[...], axis=1, keepdims=True)
            dn = (dht_v / dd).astype(BF16)
            dden = jnp.where(active, 0.0, -(hdh / dd) * jnp.sign(den))
            dp = lax.dot_general(dn, v, _DN["nt"], preferred_element_type=F32) + dden
            rmat = dp * p
            dgrow_ref[pl.ds(h, 1), 0:nk] += jnp.sum(rmat, axis=0, keepdims=True)
            ds = (dp * a * QK_SCALE).astype(BF16)
            dq_ref[...] = jnp.dot(ds, k, preferred_element_type=F32).astype(BF16)
            dkt_acc[:, 0:nk] += lax.dot_general(q, ds, _DN["tn"], preferred_element_type=F32)
            dvt_acc[:, 0:nk] += lax.dot_general(dn, p.astype(BF16), _DN["tn"], preferred_element_type=F32)
            lane = lax.broadcasted_iota(jnp.int32, (bq, LANE), 1)
            r0 = pl.multiple_of(i * bq, 16)
            dfx_ref[pl.ds(r0, bq), :] += jnp.where(lane == h, jnp.sum(rmat, axis=1, keepdims=True), 0.0)

        _per_query_tile(i, nq, bq, lp, compute)

        @pl.when(i == nq - 1)
        def _():
            dk_ref[...] = dkt_acc[...].T.astype(BF16)
            dv_ref[...] = dvt_acc[...].T.astype(BF16)

    return pl.pallas_call(
        body, name=name, grid=(HEADS, nq),
        in_specs=[pl.BlockSpec((bq, DQK), lambda h, i: (i, h)),
                  pl.BlockSpec((lp, DQK), lambda h, i: (0, HEADS + h)),
                  pl.BlockSpec((lp, DV), lambda h, i: (0, HEADS + h)),
                  pl.BlockSpec((8, lp), lambda h, i: (0, 0)),
                  pl.BlockSpec((bq, LANE), lambda h, i: (i, 0)),
                  pl.BlockSpec((bq, LANE), lambda h, i: (i, 0)),
                  pl.BlockSpec((bq, DV), lambda h, i: (i, h)),
                  pl.BlockSpec((bq, DV), lambda h, i: (i, h))],
        out_specs=[pl.BlockSpec((bq, DQK), lambda h, i: (i, h)),
                   pl.BlockSpec((lp, DQK), lambda h, i: (0, h)),
                   pl.BlockSpec((lp, DV), lambda h, i: (0, h)),
                   pl.BlockSpec((LANE, lp), lambda h, i: (0, 0)),
                   pl.BlockSpec((lp, LANE), lambda h, i: (0, 0))],
        out_shape=[_sds((lp, QKW), BF16), _sds((lp, QKW), BF16), _sds((lp, MW), BF16),
                   _sds((LANE, lp), F32), _sds((lp, LANE), F32)],
        scratch_shapes=[pltpu.VMEM((DQK, lp), F32), pltpu.VMEM((DV, lp), F32)],
        compiler_params=_cp(("arbitrary", "arbitrary")))(qkv, qkv, qkv, grow, mcol_all, ccol_all, ht, dht)


def _gate_bwd(name, gates_pre, bias, dgrow, dfx):
    lp = gates_pre.shape[0]

    def body(p_ref, b_ref, dgrow_ref, dfx_ref, dg_ref, dgb_ref, db_ref):
        pre = p_ref[...] + b_ref[...]
        lane = lax.broadcasted_iota(jnp.int32, (lp, LANE), 1)
        row = lax.broadcasted_iota(jnp.int32, (lp, LANE), 0)
        th = jnp.tanh(pre * (1.0 / GATE_CAP))
        t = GATE_CAP * th
        dgc = jnp.where(lane < HEADS, dgrow_ref[...].T, 0.0)
        df = jnp.where(lane < HEADS, dfx_ref[...] - dgc, 0.0)
        for d in _scan_steps(lp):
            df = df + jnp.where(row < lp - d, pltpu.roll(df, lp - d, axis=0), 0.0)
        dlf = pltpu.roll(df, HEADS, axis=1)
        dt = jnp.where(lane < HEADS, dgc, dlf * jax.nn.sigmoid(-t))
        dpre = jnp.where(lane < 2 * HEADS, dt * (1.0 - th * th), 0.0)
        dg_ref[...] = dpre
        dgb_ref[...] = dpre.astype(BF16)
        db_ref[...] = jnp.broadcast_to(jnp.sum(dpre, axis=0, keepdims=True), db_ref.shape)

    full = pl.BlockSpec((lp, LANE), lambda: (0, 0))
    return pl.pallas_call(
        body, name=name,
        in_specs=[full, pl.BlockSpec((1, LANE), lambda: (0, 0)), pl.BlockSpec((LANE, lp), lambda: (0, 0)), full],
        out_specs=[full, full, pl.BlockSpec((8, LANE), lambda: (0, 0))],
        out_shape=[_sds((lp, LANE), F32), _sds((lp, LANE), BF16), _sds((8, LANE), F32)],
        compiler_params=pltpu.CompilerParams(vmem_limit_bytes=VMEM_LIMIT))(gates_pre, bias, dgrow, dfx)


CB = 256


def _e_specs(lp):
    return [pl.BlockSpec((None, lp, CB), functools.partial(lambda c, j: (c, 0, j), c)) for c in range(4)]


def _mix_fwd(name, ht, e, mnw, cw):
    lp = ht.shape[0]

    def body(ht_ref, og_ref, u_ref, gb_ref, gc_ref, mnw_ref, cw_ref, o_ref):
        x = ht_ref[...]
        r = lax.rsqrt(jnp.mean(x * x, axis=1, keepdims=True) + EPS)
        o_ref[0] = (jax.nn.sigmoid(og_ref[...].astype(F32)) * (x * r * mnw_ref[...])).astype(BF16)
        row = lax.broadcasted_iota(jnp.int32, (lp, CB), 0)
        a = gc_ref[...].astype(F32) * u_ref[...].astype(F32)
        conv = cw_ref[2:3, :] * a + cw_ref[1:2, :] * _shift_rows(a, 1, row) + cw_ref[0:1, :] * _shift_rows(a, 2, row)
        o_ref[1] = (gb_ref[...].astype(F32) * conv).astype(BF16)

    col = pl.BlockSpec((lp, CB), lambda j: (0, j))
    return pl.pallas_call(
        body, name=name, grid=(4,),
        in_specs=[col] + _e_specs(lp) + [pl.BlockSpec((1, CB), lambda j: (0, j)), pl.BlockSpec((8, CB), lambda j: (0, j))],
        out_specs=pl.BlockSpec((2, lp, CB), lambda j: (0, 0, j)), out_shape=_sds((2, lp, MW), BF16),
        compiler_params=_cp(("parallel",)))(ht, e, e, e, e, mnw, cw)


def _mix_bwd(name, dmix, ht, e, mnw, cw):
    lp = ht.shape[0]

    def body(dhm_ref, dhc_ref, ht_ref, og_ref, u_ref, gb_ref, gc_ref, mnw_ref, cw_ref,
             dht_ref, de_ref, dmnw_ref, dcw_ref):
        x = ht_ref[...]
        r = lax.rsqrt(jnp.mean(x * x, axis=1, keepdims=True) + EPS)
        xh = x * r
        w = mnw_ref[...]
        sg = jax.nn.sigmoid(og_ref[...].astype(F32))
        dhm = dhm_ref[...]
        de_ref[0] = (dhm * (xh * w) * (sg * (1.0 - sg))).astype(BF16)
        dn = dhm * sg
        dmnw_ref[...] = jnp.broadcast_to(jnp.sum(dn * xh, axis=0, keepdims=True), dmnw_ref.shape)
        dxh = dn * w
        dht_ref[...] = r * (dxh - xh * jnp.mean(dxh * xh, axis=1, keepdims=True))

        row = lax.broadcasted_iota(jnp.int32, (lp, CB), 0)
        uv = u_ref[...].astype(F32)
        gcv = gc_ref[...].astype(F32)
        gbv = gb_ref[...].astype(F32)
        a = gcv * uv
        a1 = _shift_rows(a, 1, row)
        a2 = _shift_rows(a, 2, row)
        dhc = dhc_ref[...]
        conv = cw_ref[2:3, :] * a + cw_ref[1:2, :] * a1 + cw_ref[0:1, :] * a2
        de_ref[2] = (dhc * conv).astype(BF16)
        dconv = dhc * gbv
        dcw_ref[...] = jnp.zeros_like(dcw_ref)
        dcw_ref[0:1, :] = jnp.sum(dconv * a2, axis=0, keepdims=True)
        dcw_ref[1:2, :] = jnp.sum(dconv * a1, axis=0, keepdims=True)
        dcw_ref[2:3, :] = jnp.sum(dconv * a, axis=0, keepdims=True)
        up1 = jnp.where(row < lp - 1, pltpu.roll(dconv, lp - 1, axis=0), 0.0)
        up2 = jnp.where(row < lp - 2, pltpu.roll(dconv, lp - 2, axis=0), 0.0)
        da = cw_ref[2:3, :] * dconv + cw_ref[1:2, :] * up1 + cw_ref[0:1, :] * up2
        de_ref[1] = (da * gcv).astype(BF16)
        de_ref[3] = (da * uv).astype(BF16)

    col = pl.BlockSpec((lp, CB), lambda j: (0, j))
    small = pl.BlockSpec((8, CB), lambda j: (0, j))
    return pl.pallas_call(
        body, name=name, grid=(4,),
        in_specs=[col, pl.BlockSpec((lp, CB), lambda j: (0, 4 + j)), col] + _e_specs(lp)
                 + [pl.BlockSpec((1, CB), lambda j: (0, j)), small],
        out_specs=[col, pl.BlockSpec((4, lp, CB), lambda j: (0, 0, j)), small, small],
        out_shape=[_sds((lp, MW), F32), _sds((4, lp, MW), BF16), _sds((8, MW), F32), _sds((8, CW), F32)],
        compiler_params=_cp(("parallel",)))(dmix, dmix, ht, e, e, e, e, mnw, cw)


def _row_tile(r, c, itemsize, budget=1536 * 1024, mult=16):
    best = None
    for t in range(mult, r + 1, mult):
        if r % t == 0 and t * c * itemsize <= budget:
            best = t
    if best is None:
        best = r
    return best


def _grid_spec(grid, in_specs, out_specs, scratch=()):
    return pltpu.PrefetchScalarGridSpec(num_scalar_prefetch=1, grid=grid, in_specs=in_specs,
                                        out_specs=out_specs, scratch_shapes=list(scratch))


def _cast_into(name, w, layer, pf):
    _, r, c = w.shape
    tr = _row_tile(r, c, 4)

    def body(pf_ref, x_ref, o_ref):
        o_ref[...] = x_ref[...].astype(BF16)

    return pl.pallas_call(
        body, name=name, out_shape=_sds((NCHIP, r, c), BF16),
        grid_spec=_grid_spec((r // tr,), [pl.BlockSpec((None, tr, c), lambda i, pf: (layer, i, 0))],
                             pl.BlockSpec((None, tr, c), lambda i, pf: (pf[1], i, 0))),
        compiler_params=_cp(("parallel",)))(pf, w)


TCOL = 256


def _cast_into_t(name, w_t, layer, pf):
    c, nl, r = w_t.shape

    def body(pf_ref, x_ref, o_ref):
        o_ref[...] = x_ref[:, layer, :].astype(BF16)

    return pl.pallas_call(
        body, name=name, out_shape=_sds((NCHIP, c, r), BF16),
        grid_spec=_grid_spec((r // TCOL,), [pl.BlockSpec((c, nl, TCOL), lambda i, pf: (0, 0, i))],
                             pl.BlockSpec((None, c, TCOL), lambda i, pf: (pf[1], 0, i))),
        compiler_params=_cp(("parallel",)))(pf, w_t)


def _add2_bf16(name, dw, got, pf, by_cols=False):
    n4, r2, c2 = got.shape

    def body(pf_ref, a_ref, b_ref, o_ref):
        o_ref[...] = (a_ref[...].astype(F32) + b_ref[...].astype(F32)).astype(BF16)

    if by_cols:
        nch = c2 // TCOL
        spec = pl.BlockSpec((None, r2, TCOL), lambda s, i, pf: (s, 0, i))
        mine = pl.BlockSpec((None, r2, TCOL), lambda s, i, pf: (s, 0, pf[0] * nch + i))
    else:
        tr = _row_tile(r2, c2, 4)
        nch = r2 // tr
        spec = pl.BlockSpec((None, tr, c2), lambda s, i, pf: (s, i, 0))
        mine = pl.BlockSpec((None, tr, c2), lambda s, i, pf: (s, pf[0] * nch + i, 0))
    return pl.pallas_call(
        body, name=name, out_shape=_sds((n4, r2, c2), BF16),
        grid_spec=_grid_spec((n4, nch), [mine, spec], spec),
        compiler_params=_cp(("parallel", "parallel")))(pf, dw, got)


def _adam_math(w, g, m, v):
    m2 = ADAM_B1 * m + (1.0 - ADAM_B1) * g
    v2 = ADAM_B2 * v + (1.0 - ADAM_B2) * (g * g)
    m_hat = m2 / (1.0 - ADAM_B1 ** ADAM_STEP)
    v_hat = v2 / (1.0 - ADAM_B2 ** ADAM_STEP)
    delta = -ADAM_LR * (m_hat / (jnp.sqrt(v_hat) + ADAM_EPS) + ADAM_WD * w)
    return delta, m2, v2


def _adamw_layer(name, layer, g_mine, g_theirs, w, m, v, prev, pf):
    _, r, c = w.shape
    r2 = r // 2
    tr = _row_tile(r2, c, 4, budget=1024 * 1024, mult=8)
    nch = r2 // tr
    n_alias = 0 if prev is None else 4

    def body(*refs):
        pf_ref, gm_ref, gt_ref, w_ref, m_ref, v_ref = refs[:6]
        go_ref, d_ref, mo_ref, vo_ref = refs[6 + n_alias:]
        mine = (pl.program_id(0) // nch) == pf_ref[0]
        gv = jnp.where(mine, gm_ref[...], gt_ref[...])
        delta, m2, v2 = _adam_math(w_ref[...], gv, m_ref[...], v_ref[...])
        go_ref[...] = gv
        d_ref[...] = delta
        mo_ref[...] = m2
        vo_ref[...] = v2

    slab = pl.BlockSpec((None, tr, c), lambda i, pf: (layer, i, 0))
    ins = [g_mine, g_theirs, w, m, v] + (list(prev) if prev is not None else [])
    in_specs = [pl.BlockSpec((tr, c), lambda i, pf: (jnp.clip(i - pf[0] * nch, 0, nch - 1), 0)),
                pl.BlockSpec((tr, c), lambda i, pf: (jnp.clip(i - (1 - pf[0]) * nch, 0, nch - 1), 0)),
                slab, slab, slab] + [ANY] * n_alias
    o = _sds(w.shape, F32)
    return pl.pallas_call(
        body, name=name, out_shape=[o] * 4, grid_spec=_grid_spec((2 * nch,), in_specs, [slab] * 4),
        input_output_aliases={6 + k: k for k in range(n_alias)},
        compiler_params=_cp(("parallel",)))(pf, *ins)


def _adamw_t(name, gs, w_t, m_t, v_t, pf):
    c, nl, r = w_t.shape
    ta = LANE
    nch = (r // 2) // ta

    def body(*refs):
        pf_ref = refs[0]
        g_refs = refs[1:1 + 2 * nl]
        w_ref, m_ref, v_ref, go_ref, d_ref, mo_ref, vo_ref = refs[1 + 2 * nl:]
        mine = (pl.program_id(0) // nch) == pf_ref[0]
        gv = jnp.stack([jnp.where(mine, g_refs[2 * l][...], g_refs[2 * l + 1][...]) for l in range(nl)], axis=1)
        delta, m2, v2 = _adam_math(w_ref[...], gv, m_ref[...], v_ref[...])
        go_ref[...] = gv
        d_ref[...] = delta
        mo_ref[...] = m2
        vo_ref[...] = v2

    both = pl.BlockSpec((c, nl, ta), lambda i, pf: (0, 0, i))
    g_specs = []
    for l in range(nl):
        g_specs += [pl.BlockSpec((c, ta), lambda i, pf: (0, jnp.clip(i - pf[0] * nch, 0, nch - 1))),
                    pl.BlockSpec((c, ta), lambda i, pf: (0, jnp.clip(i - (1 - pf[0]) * nch, 0, nch - 1)))]
    o = _sds(w_t.shape, F32)
    flat_g = [a for pair in gs for a in pair]
    return pl.pallas_call(
        body, name=name, out_shape=[o] * 4, grid_spec=_grid_spec((2 * nch,), g_specs + [both] * 3, [both] * 4),
        compiler_params=_cp(("parallel",)))(pf, *flat_g, w_t, m_t, v_t)


def _adamw_flat(g, w, m, v):
    def body(g_ref, w_ref, m_ref, v_ref, d_ref, mo_ref, vo_ref):
        delta, m2, v2 = _adam_math(w_ref[...], g_ref[...], m_ref[...], v_ref[...])
        d_ref[...] = delta
        mo_ref[...] = m2
        vo_ref[...] = v2

    o = _sds(w.shape, F32)
    return pl.pallas_call(body, name="adamw_small", out_shape=[o, o, o])(g, w, m, v)


def _place():
    x, y, c = lax.axis_index("x"), lax.axis_index("y"), lax.axis_index("c")
    chips = [(1 - x, y), (x, 1 - y), (1 - x, 1 - y)]
    return x, y, c, chips


def _rs_chips(name, ps):
    n = len(ps)

    def body(*refs):
        ins = refs[:n]
        got = refs[n:2 * n]
        send, recv = refs[2 * n:]
        x, y, c, chips = _place()
        cps = []
        for t in range(n):
            for k, chip in enumerate(chips):
                jk = 2 * chip[0] + chip[1]
                cp = pltpu.make_async_remote_copy(
                    src_ref=ins[t].at[jk], dst_ref=got[t].at[k],
                    send_sem=send.at[3 * t + k], recv_sem=recv.at[3 * t + k],
                    device_id=(*chip, c), device_id_type=MESH)
                cp.start()
                cps.append(cp)
        for cp in cps:
            cp.wait()

    dma = pltpu.SemaphoreType.DMA
    return pl.pallas_call(
        body, name=name, in_specs=[ANY] * n, out_specs=[ANY] * n,
        out_shape=[_sds((3,) + p.shape[1:], BF16) for p in ps],
        scratch_shapes=[dma((3 * n,)), dma((3 * n,))],
        compiler_params=pltpu.CompilerParams(has_side_effects=True))(*ps)


HBM = pl.BlockSpec(memory_space=pltpu.HBM)
SEM = pl.BlockSpec(memory_space=pltpu.SEMAPHORE)
EFFECT = pltpu.SideEffectType.DATAFLOW_SIDE_EFFECTING


def _in_hbm(a):
    return pltpu.with_memory_space_constraint(a, pltpu.HBM)


def _rs_chips_copies(ins, lands, send, recv):
    x, y, c, chips = _place()
    cps = []
    for t in range(len(ins)):
        for k, chip in enumerate(chips):
            jk = 2 * chip[0] + chip[1]
            cps.append(pltpu.make_async_remote_copy(
                src_ref=ins[t].at[jk], dst_ref=lands[t].at[k], send_sem=send.at[3 * t + k],
                recv_sem=recv.at[3 * t + k], device_id=(*chip, c), device_id_type=MESH))
    return cps


def _rs_chips_start(name, ps):
    n = len(ps)

    def body(*refs):
        ins, lands = refs[:n], refs[n:2 * n]
        send, recv = refs[2 * n], refs[2 * n + 1]
        token = refs[-1]
        for cp in _rs_chips_copies(ins, lands, send, recv):
            cp.start()
        token[...] = jnp.zeros_like(token)

    dma = pltpu.SemaphoreType.DMA
    lands = [lax.empty((3,) + p.shape[1:], BF16) for p in ps]
    out_shape = ([dma((3 * n,)), dma((3 * n,))] + [pltpu.HBM(p.shape, BF16) for p in ps]
                 + [pltpu.HBM(z.shape, BF16) for z in lands] + [_sds((8, LANE), F32)])
    outs = pl.pallas_call(
        body, name=name, out_shape=out_shape, in_specs=[HBM] * (2 * n),
        out_specs=[SEM, SEM] + [HBM] * (2 * n) + [pl.BlockSpec(memory_space=pltpu.VMEM)],
        input_output_aliases={i: 2 + i for i in range(2 * n)},
        compiler_params=pltpu.CompilerParams(has_side_effects=EFFECT))(
            *[_in_hbm(p) for p in ps], *[_in_hbm(z) for z in lands])
    return outs[0], outs[1], outs[2:2 + n], outs[2 + n:2 + 2 * n], outs[-1]


def _rs_chips_wait(name, send, recv, ps, lands, afters):
    n = len(ps)

    def body(*refs):
        ins, zones = refs[:n], refs[n:2 * n]
        send_ref, recv_ref = refs[2 * n], refs[2 * n + 1]
        for cp in _rs_chips_copies(ins, zones, send_ref, recv_ref):
            cp.wait_send()
            cp.wait_recv()

    outs = pl.pallas_call(
        body, name=name, out_shape=[pltpu.HBM(p.shape, BF16) for p in ps] + [pltpu.HBM(z.shape, BF16) for z in lands],
        in_specs=[HBM] * (2 * n) + [SEM, SEM] + [ANY] * len(afters), out_specs=[HBM] * (2 * n),
        input_output_aliases={i: i for i in range(2 * n)},
        compiler_params=pltpu.CompilerParams(has_side_effects=EFFECT))(*ps, *lands, send, recv, *afters)
    return outs[:n], outs[n:]


def _sibling():
    x, y, c, _ = _place()
    return (x, y, 1 - c)


def _pair_send(name, dw, pf, by_cols=False):
    n4, r, c = dw.shape
    blk = (1, r, c // 2) if by_cols else (1, r // 2, c)
    idx = (lambda s, pf: (s, 0, 1 - pf[0])) if by_cols else (lambda s, pf: (s, 1 - pf[0], 0))

    def body(pf_ref, x_ref, got_ref, ssem, rsem):
        s = pl.program_id(0)
        cp = pltpu.make_async_remote_copy(src_ref=x_ref, dst_ref=got_ref.at[pl.ds(s, 1)], send_sem=ssem,
                                          recv_sem=rsem, device_id=_sibling(), device_id_type=MESH)
        cp.start()
        cp.wait_send()

        @pl.when(s == n4 - 1)
        def _():
            pltpu.make_async_remote_copy(src_ref=got_ref, dst_ref=got_ref, send_sem=ssem, recv_sem=rsem,
                                         device_id=_sibling(), device_id_type=MESH).wait_recv()

    dma = pltpu.SemaphoreType.DMA
    return pl.pallas_call(
        body, name=name, out_shape=_sds((n4,) + blk[1:], BF16),
        grid_spec=_grid_spec((n4,), [pl.BlockSpec(blk, idx)], ANY, scratch=[dma(()), dma(())]),
        compiler_params=pltpu.CompilerParams(dimension_semantics=("arbitrary",), has_side_effects=True,
                                             vmem_limit_bytes=VMEM_LIMIT))(pf, dw)


def _dw_pair(name, x, dy, down):
    lp = x.shape[0]
    tk = DFF // NCHIP
    if down:
        grid = (NCHIP, 2)
        x_spec = pl.BlockSpec((lp, tk), lambda s, j: (0, s))
        dy_spec = pl.BlockSpec((lp, 1024), lambda s, j: (0, j))
        o_spec = pl.BlockSpec((1, tk, 1024), lambda s, j: (s, 0, j))
        dw_shape, got_shape = (NCHIP, tk, D), (NCHIP, tk // 2, D)
    else:
        grid = (2, NCHIP)
        x_spec = pl.BlockSpec((lp, 1024), lambda i, s: (0, i))
        dy_spec = pl.BlockSpec((lp, tk), lambda i, s: (0, s))
        o_spec = pl.BlockSpec((1, 1024, tk), lambda i, s: (s, i, 0))
        dw_shape, got_shape = (NCHIP, D, tk), (NCHIP, D // 2, tk)

    send_shape = (tk // 2, 1024) if down else (1024, tk)
    n_sends = 2 * NCHIP if down else NCHIP

    def body(x_ref, dy_ref, o_ref, got_ref, sbuf, ssem, rsem):
        g0, g1 = pl.program_id(0), pl.program_id(1)
        tile = lax.dot_general(x_ref[...], dy_ref[...], _DN["tn"], preferred_element_type=F32).astype(BF16)
        o_ref[0] = tile
        _, _, c, _ = _place()

        def send(n, part, dst):
            for s in (0, 1):
                cp = pltpu.make_async_remote_copy(src_ref=sbuf.at[pl.ds(s, 1)], dst_ref=dst, send_sem=ssem.at[s],
                                                  recv_sem=rsem, device_id=_sibling(), device_id_type=MESH)

                @pl.when(n % 2 == s)
                def _():
                    pl.when(n >= 2)(cp.wait_send)
                    sbuf[s] = part()
                    cp.start()

            @pl.when(n == n_sends - 1)
            def _():
                for s in (0, 1):
                    pltpu.make_async_remote_copy(src_ref=sbuf.at[pl.ds(s, 1)], dst_ref=dst, send_sem=ssem.at[s],
                                                 recv_sem=rsem, device_id=_sibling(), device_id_type=MESH).wait_send()

        if down:
            hr = tk // 2
            rows = pl.ds(pl.multiple_of((1 - c) * hr, 16), hr)
            send(g0 * 2 + g1, lambda: o_ref[0, rows, :],
                 got_ref.at[pl.ds(g0, 1), :, pl.ds(pl.multiple_of(g1 * 1024, LANE), 1024)])
        else:
            pl.when(g0 == 1 - c)(lambda: send(g1, lambda: tile, got_ref.at[pl.ds(g1, 1)]))

        @pl.when(jnp.logical_and(g0 == grid[0] - 1, g1 == grid[1] - 1))
        def _():
            pltpu.make_async_remote_copy(src_ref=got_ref, dst_ref=got_ref, send_sem=ssem.at[0], recv_sem=rsem,
                                         device_id=_sibling(), device_id_type=MESH).wait_recv()

    dma = pltpu.SemaphoreType.DMA
    return pl.pallas_call(
        body, name=name, grid=grid, in_specs=[x_spec, dy_spec], out_specs=[o_spec, ANY],
        out_shape=[_sds(dw_shape, BF16), _sds(got_shape, BF16)],
        scratch_shapes=[pltpu.VMEM((2,) + send_shape, BF16), dma((2,)), dma(())],
        compiler_params=pltpu.CompilerParams(dimension_semantics=("arbitrary", "arbitrary"), has_side_effects=True,
                                             vmem_limit_bytes=VMEM_LIMIT))(x, dy)


def _add4_join(name, p, got, pf, by_cols=False):
    n4, r2, c = p.shape
    if by_cols:
        tr, nch = r2, c // TCOL
        blk, idx = (r2, TCOL), (lambda i: (0, i))
    else:
        tr = _row_tile(r2, c, 4)
        nch = r2 // tr
        blk, idx = (tr, c), (lambda i: (i, 0))

    def body(pf_ref, p_ref, g_ref, mine_ref, theirs_ref, sbuf, ssem, rsem):
        i = pl.program_id(0)
        s = p_ref[...].astype(F32)
        for k in range(3):
            s = s + g_ref[k].astype(F32)
        mine_ref[...] = s
        if by_cols:
            dst = theirs_ref.at[:, pl.ds(pl.multiple_of(i * TCOL, LANE), TCOL)]
        else:
            dst = theirs_ref.at[pl.ds(pl.multiple_of(i * tr, 8), tr), :]

        def copy(slot):
            return pltpu.make_async_remote_copy(src_ref=sbuf.at[slot], dst_ref=dst, send_sem=ssem.at[slot],
                                                recv_sem=rsem, device_id=_sibling(), device_id_type=MESH)

        for slot in (0, 1):
            @pl.when(i % 2 == slot)
            def _():
                pl.when(i >= 2)(copy(slot).wait_send)
                sbuf[slot] = s
                copy(slot).start()

        @pl.when(i == nch - 1)
        def _():
            for slot in range(min(nch, 2)):
                copy(slot).wait_send()
            pltpu.make_async_remote_copy(src_ref=theirs_ref, dst_ref=theirs_ref, send_sem=ssem.at[0], recv_sem=rsem,
                                         device_id=_sibling(), device_id_type=MESH).wait_recv()

    dma = pltpu.SemaphoreType.DMA
    o = _sds((r2, c), F32)
    return pl.pallas_call(
        body, name=name, out_shape=[o, o],
        grid_spec=_grid_spec((nch,), [pl.BlockSpec((None,) + blk, lambda i, pf: (pf[1],) + idx(i)),
                                      pl.BlockSpec((3,) + blk, lambda i, pf: (0,) + idx(i))],
                             [pl.BlockSpec(blk, lambda i, pf: idx(i)), ANY],
                             scratch=[pltpu.VMEM((2,) + blk, F32), dma((2,)), dma(())]),
        compiler_params=pltpu.CompilerParams(dimension_semantics=("arbitrary",), has_side_effects=True,
                                             vmem_limit_bytes=VMEM_LIMIT))(pf, p, got)


def _half_of(g, slot, which, axis):
    half = g.shape[axis] // 2
    if axis == 1:
        return g.at[slot, pl.ds(which * half, half), :]
    return g.at[slot, :, pl.ds(which * half, half)]


def _ag_copies(arrs, split, send, recv):
    x, y, c, chips = _place()
    j = 2 * x + y
    cps = []
    for t, g in enumerate(arrs):
        piece = _half_of(g, j, c, split[t]) if split[t] else g.at[j]
        for k, chip in enumerate(chips):
            cps.append(pltpu.make_async_remote_copy(
                src_ref=piece, dst_ref=piece, send_sem=send.at[3 * t + k], recv_sem=recv.at[3 * t + k],
                device_id=(*chip, c), device_id_type=MESH))
    return cps


def _ag_start(name, groups, splits):
    sizes = [len(g) for g in groups]
    flat = [a for g in groups for a in g]
    n = len(flat)

    def body(*refs):
        ins = refs[:n]
        sems = refs[n:n + 2 * len(groups)]
        o = 0
        for gi, sz in enumerate(sizes):
            for cp in _ag_copies(ins[o:o + sz], splits[gi], sems[2 * gi], sems[2 * gi + 1]):
                cp.start()
            o += sz
        refs[-1][...] = jnp.zeros_like(refs[-1])

    dma = pltpu.SemaphoreType.DMA
    sem_shapes = [dma((3 * sz,)) for sz in sizes for _ in range(2)]
    outs = pl.pallas_call(
        body, name=name, out_shape=sem_shapes + [pltpu.HBM(a.shape, a.dtype) for a in flat] + [_sds((8, LANE), F32)],
        in_specs=[HBM] * n,
        out_specs=[SEM] * len(sem_shapes) + [HBM] * n + [pl.BlockSpec(memory_space=pltpu.VMEM)],
        input_output_aliases={i: len(sem_shapes) + i for i in range(n)},
        compiler_params=pltpu.CompilerParams(has_side_effects=EFFECT))(*[_in_hbm(a) for a in flat])
    sems, arrs, o = [], [], len(sem_shapes)
    for gi, sz in enumerate(sizes):
        sems.append((outs[2 * gi], outs[2 * gi + 1]))
        arrs.append(list(outs[o:o + sz]))
        o += sz
    return sems, arrs, outs[-1]


def _ag_wait(name, arrs, split, send, recv, afters):
    n = len(arrs)

    def body(*refs):
        for cp in _ag_copies(refs[:n], split, refs[n], refs[n + 1]):
            cp.wait_send()
            cp.wait_recv()

    return pl.pallas_call(
        body, name=name, out_shape=[pltpu.HBM(a.shape, a.dtype) for a in arrs],
        in_specs=[HBM] * n + [SEM, SEM] + [ANY] * len(afters), out_specs=[HBM] * n,
        input_output_aliases={i: i for i in range(n)},
        compiler_params=pltpu.CompilerParams(has_side_effects=EFFECT))(*arrs, send, recv, *afters)


def _ag_forward(name, arrs, axes):
    n = len(arrs)

    def half_shape(t):
        _, r, cc = arrs[t].shape
        return (r // 2, cc) if axes[t] == 1 else (r, cc // 2)

    def body(*refs):
        g = refs[n:2 * n]
        bufs = refs[2 * n:3 * n]
        fsend, frecv, lsem = refs[3 * n:]
        x, y, c, chips = _place()
        for t in range(n):
            pend = [None, None]
            for k, chip in enumerate(chips):
                jk = 2 * chip[0] + chip[1]
                slot = k % 2
                if pend[slot] is not None:
                    pend[slot].wait_send()
                part = _half_of(g[t], jk, c, axes[t])
                ld = pltpu.make_async_copy(part, bufs[t].at[slot], lsem.at[2 * t + slot])
                ld.start()
                ld.wait()
                cp = pltpu.make_async_remote_copy(
                    src_ref=bufs[t].at[slot], dst_ref=part, send_sem=fsend.at[2 * t + slot],
                    recv_sem=frecv.at[t], device_id=(x, y, 1 - c), device_id_type=MESH)
                cp.start()
                pend[slot] = cp
            for cp in pend:
                cp.wait_send()
        for t in range(n):
            hr, hc = half_shape(t)
            passed = g[t].at[pl.ds(0, 3), pl.ds(0, hr), pl.ds(0, hc)]
            pltpu.make_async_remote_copy(
                src_ref=passed, dst_ref=passed, send_sem=fsend.at[2 * t], recv_sem=frecv.at[t],
                device_id=(x, y, 1 - c), device_id_type=MESH).wait_recv()

    dma = pltpu.SemaphoreType.DMA
    scratch = [pltpu.VMEM((2,) + half_shape(t), BF16) for t in range(n)]
    scratch += [dma((2 * n,)), dma((n,)), dma((2 * n,))]
    return pl.pallas_call(
        body, name=name, in_specs=[ANY] * n, out_specs=[ANY] * n, out_shape=[_sds(a.shape, a.dtype) for a in arrs],
        scratch_shapes=scratch, input_output_aliases={t: t for t in range(n)},
        compiler_params=pltpu.CompilerParams(has_side_effects=True, vmem_limit_bytes=VMEM_LIMIT))(*arrs)


def _allgather_blocking(gs, smalls):
    nb, ns = len(gs), len(smalls)
    halves = [g.shape[2] // 2 for g in gs]

    def body(*refs):
        s_in = refs[nb:nb + ns]
        g = refs[nb + ns:2 * nb + ns]
        s_out = refs[2 * nb + ns:2 * (nb + ns)]
        scr = refs[2 * (nb + ns):]
        bufs = scr[:nb]
        send, recv, fsend, frecv, lsem, ssend, srecv, slsem = scr[nb:]
        x, y, c, chips = _place()
        j = 2 * x + y
        sends, slocal = [], []
        for t in range(nb):
            rows = pl.ds(c * halves[t], halves[t])
            piece = g[t].at[:, j, rows, :]
            for k, chip in enumerate(chips):
                cp = pltpu.make_async_remote_copy(
                    src_ref=piece, dst_ref=piece, send_sem=send.at[3 * t + k], recv_sem=recv.at[3 * t + k],
                    device_id=(*chip, c), device_id_type=MESH)
                cp.start()
                sends.append(cp)
        for t in range(ns):
            cp = pltpu.make_async_copy(s_in[t], s_out[t].at[j], slsem.at[t])
            cp.start()
            slocal.append(cp)
            for k, chip in enumerate(chips):
                cp = pltpu.make_async_remote_copy(
                    src_ref=s_in[t], dst_ref=s_out[t].at[j], send_sem=ssend.at[3 * t + k],
                    recv_sem=srecv.at[3 * t + k], device_id=(*chip, c), device_id_type=MESH)
                cp.start()
                sends.append(cp)
        for t in range(nb):
            rows = pl.ds(c * halves[t], halves[t])
            pend = [None, None]
            n = 0
            for k, chip in enumerate(chips):
                jk = 2 * chip[0] + chip[1]
                landed = g[t].at[:, jk, rows, :]
                pltpu.make_async_remote_copy(
                    src_ref=landed, dst_ref=landed, send_sem=send.at[3 * t + k], recv_sem=recv.at[3 * t + k],
                    device_id=(*chip, c), device_id_type=MESH).wait_recv()
                for l in range(DEPTH):
                    slot = n % 2
                    if pend[slot] is not None:
                        pend[slot].wait_send()
                    part = g[t].at[l, jk, rows, :]
                    ld = pltpu.make_async_copy(part, bufs[t].at[slot], lsem.at[2 * t + slot])
                    ld.start()
                    ld.wait()
                    cp = pltpu.make_async_remote_copy(
                        src_ref=bufs[t].at[slot], dst_ref=part, send_sem=fsend.at[2 * t + slot],
                        recv_sem=frecv.at[t], device_id=(x, y, 1 - c), device_id_type=MESH)
                    cp.start()
                    pend[slot] = cp
                    n += 1
            for cp in pend:
                cp.wait_send()
        for t in range(ns):
            for k, chip in enumerate(chips):
                jk = 2 * chip[0] + chip[1]
                landed = s_out[t].at[jk]
                pltpu.make_async_remote_copy(
                    src_ref=landed, dst_ref=landed, send_sem=ssend.at[3 * t + k], recv_sem=srecv.at[3 * t + k],
                    device_id=(*chip, c), device_id_type=MESH).wait_recv()
        for t in range(nb):
            passed = g[t].at[:, pl.ds(0, 3), pl.ds((1 - c) * halves[t], halves[t]), :]
            pltpu.make_async_remote_copy(
                src_ref=passed, dst_ref=passed, send_sem=fsend.at[2 * t], recv_sem=frecv.at[t],
                device_id=(x, y, 1 - c), device_id_type=MESH).wait_recv()
        for cp in sends:
            cp.wait_send()
        for cp in slocal:
            cp.wait()

    dma = pltpu.SemaphoreType.DMA
    out_shape = [_sds(g.shape, g.dtype) for g in gs] + [_sds((NCHIP,) + s.shape, s.dtype) for s in smalls]
    scratch = [pltpu.VMEM((2, halves[t], gs[t].shape[3]), BF16) for t in range(nb)]
    scratch += [dma((3 * nb,)), dma((3 * nb,)), dma((2 * nb,)), dma((nb,)), dma((2 * nb,)),
                dma((3 * ns,)), dma((3 * ns,)), dma((ns,))]
    return pl.pallas_call(
        body, name="allgather_weights", in_specs=[ANY] * (nb + ns), out_specs=[ANY] * (nb + ns),
        out_shape=out_shape, scratch_shapes=scratch, input_output_aliases={t: t for t in range(nb)},
        compiler_params=pltpu.CompilerParams(has_side_effects=True, vmem_limit_bytes=VMEM_LIMIT))(*gs, *smalls)


def _reduce_scatter(tag, dws, pf):
    ps = []
    for t, dw in enumerate(dws):
        got = _pair_send("rs_pair_%s_%d" % (tag, t), dw, pf)
        ps.append(_add2_bf16("rs_add2_%s_%d" % (tag, t), dw, got, pf))
    got2 = _rs_chips("rs_chips_" + tag, ps)
    return [_add4_join("rs_add4_%s_%d" % (tag, t), p, g2, pf) for t, (p, g2) in enumerate(zip(ps, got2))]


def _rs_begin(tag, dws, by_cols, pf):
    ps = []
    for t, dw in enumerate(dws):
        if isinstance(dw, (list, tuple)):
            dw, got = dw
        else:
            got = _pair_send("rs_pair_%s_%d" % (tag, t), dw, pf, by_cols[t])
        ps.append(_add2_bf16("rs_add2_%s_%d" % (tag, t), dw, got, pf, by_cols[t]))
    send, recv, ps_thru, lands, token = _rs_chips_start("rs_chips_start_" + tag, ps)
    return (tag, send, recv, ps_thru, lands, by_cols), token


def _rs_end(handle, afters, pf):
    tag, send, recv, ps, lands, by_cols = handle
    ps, got2 = _rs_chips_wait("rs_chips_wait_" + tag, send, recv, ps, lands, afters)
    return [_add4_join("rs_add4_%s_%d" % (tag, t), p, g2, pf, by_cols[t]) for t, (p, g2) in enumerate(zip(ps, got2))]


def _allreduce_small(pack):
    r = pack.shape[0]
    flips = [(fx, fy, fc) for fx in (0, 1) for fy in (0, 1) for fc in (0, 1)][1:]

    def body(p_ref, o_ref, gat, send, recv):
        x, y, c, _ = _place()
        me = 4 * x + 2 * y + c
        gat[me] = p_ref[...]
        cps = []
        for k, (fx, fy, fc) in enumerate(flips):
            peer = ((1 - x) if fx else x, (1 - y) if fy else y, (1 - c) if fc else c)
            cp = pltpu.make_async_remote_copy(
                src_ref=p_ref, dst_ref=gat.at[me], send_sem=send.at[k], recv_sem=recv.at[k],
                device_id=peer, device_id_type=MESH)
            cp.start()
            cps.append(cp)
        for k, (fx, fy, fc) in enumerate(flips):
            peer = ((1 - x) if fx else x, (1 - y) if fy else y, (1 - c) if fc else c)
            src = 4 * peer[0] + 2 * peer[1] + peer[2]
            pltpu.make_async_remote_copy(
                src_ref=p_ref, dst_ref=gat.at[src], send_sem=send.at[k], recv_sem=recv.at[k],
                device_id=peer, device_id_type=MESH).wait_recv()
        for cp in cps:
            cp.wait_send()
        s = gat[0]
        for d in range(1, 8):
            s = s + gat[d]
        o_ref[...] = s

    dma = pltpu.SemaphoreType.DMA
    vm = pl.BlockSpec(memory_space=pltpu.VMEM)
    return pl.pallas_call(
        body, name="allreduce_small", in_specs=[vm], out_specs=vm, out_shape=_sds((r, LANE), F32),
        scratch_shapes=[pltpu.VMEM((8, r, LANE), F32), dma((7,)), dma((7,))],
        compiler_params=pltpu.CompilerParams(has_side_effects=True))(pack)


def _in_weights(win_g):
    full = jnp.concatenate([win_g[s] for s in range(NCHIP)], axis=0)
    wqkv = full[:2048]
    og = full[2048:3072]
    gates = jnp.pad(full[3072:3080], ((0, LANE - 8), (0, 0)))
    u = full[3080:4104]
    gb = full[4104:5128]
    gc = full[5128:6152]
    return wqkv, jnp.stack([og, u, gb, gc]), gates


def _in_grads(dwqkv, dwe, dwgt):
    full = jnp.concatenate([dwqkv, dwe[0], dwgt[:8], dwe[1], dwe[2], dwe[3]], axis=0)
    sw = DIN // NCHIP
    return jnp.stack([full[s * sw:(s + 1) * sw] for s in range(NCHIP)])


def _layer_fwd(l, h, get, small):
    lp = h.shape[0]
    th = lp // 2
    wqkv, we, wgt = _in_weights(get("in", h)[0])
    nmw, bias, mnw, cw, nfw = small
    tag = "_l%d" % l
    hn = _norm_fwd("norm_mix" + tag, h, nmw)
    qkv = _mm("proj_qkv" + tag, "nt", hn, wqkv,
              pl.BlockSpec((lp, D), lambda i, j, k: (0, 0)), pl.BlockSpec((512, D), lambda i, j, k: (j, 0)),
              pl.BlockSpec((lp, 512), lambda i, j, k: (0, j)), _sds((lp, 2048), BF16), (1, 4, 1))
    e = _mm("proj_e" + tag, "nt", hn, we,
            pl.BlockSpec((lp, D), lambda i, j, k: (0, 0)), pl.BlockSpec((None, 512, D), lambda i, j, k: (j // 2, j % 2, 0)),
            pl.BlockSpec((None, lp, 512), lambda i, j, k: (j // 2, 0, j % 2)), _sds((4, lp, 1024), BF16), (1, 8, 1))
    gpre = _mm("proj_gates" + tag, "nt", hn, wgt,
               pl.BlockSpec((lp, D), lambda i, j, k: (0, 0)), pl.BlockSpec((LANE, D), lambda i, j, k: (0, 0)),
               pl.BlockSpec((lp, LANE), lambda i, j, k: (0, 0)), _sds((lp, LANE), F32), (1, 1, 1))
    grow, mcol, ccol = _gate_prep("gate_prep" + tag, gpre, bias)
    ht = _mlstm_fwd("mlstm_fwd" + tag, qkv, grow, mcol, ccol, 4)
    mix = _mix_fwd("mix_fwd" + tag, ht, e, mnw, cw)
    wout_g = get("out", mix)[0]
    wout = wout_g.reshape(D, D)
    h1 = _mm("out_proj" + tag, "nn", mix, wout,
             pl.BlockSpec((None, th, 1024), lambda i, j, k: (k, i, 0)),
             pl.BlockSpec((1024, 1024), lambda i, j, k: (k, j)),
             pl.BlockSpec((th, 1024), lambda i, j, k: (i, j)), _sds((lp, D), F32), (2, 2, 2),
             acc_shape=(th, 1024), res=h, res_spec=pl.BlockSpec((th, 1024), lambda i, j, k: (i, j)))
    wg_g, wu_g = get("ffn", h1)
    hf = _norm_fwd("norm_ffn" + tag, h1, nfw)
    g, u, a = _ffn_up("ffn_up" + tag, hf, wg_g, wu_g)
    wd_g = get("down", a)[0]
    tk = DFF // NCHIP
    h2 = _mm("ffn_down" + tag, "nn", a, wd_g,
             pl.BlockSpec((th, tk), lambda i, j, k: (i, k)),
             pl.BlockSpec((None, tk, 1024), lambda i, j, k: (k, 0, j)),
             pl.BlockSpec((th, 1024), lambda i, j, k: (i, j)), _sds((lp, D), F32), (2, 2, NCHIP),
             acc_shape=(th, 1024), res=h1, res_spec=pl.BlockSpec((th, 1024), lambda i, j, k: (i, j)))
    saved = (h, hn, qkv, e, gpre, grow, mcol, ccol, ht, mix, h1, hf, g, u, a)
    return h2, saved, (wqkv, we, wgt, wout_g, wg_g, wu_g, wd_g)


def _layer_bwd(l, dh2, dh2b, saved, wts, small, ffn_done):
    h, hn, qkv, e, gpre, grow, mcol, ccol, ht, mix, h1, hf, g, u, a = saved
    wqkv, we, wgt, wout_g, wg_g, wu_g, wd_g = wts
    nmw, bias, mnw, cw, nfw = small
    lp = h.shape[0]
    th = lp // 2
    tk = DFF // NCHIP
    tag = "_l%d" % l
    half_rows = lambda i, j, k: (i, j)

    dwd = _dw_pair("dw_down" + tag, a, dh2b, True)
    dg, du = _ffn_bwd_act("ffn_bwd_act" + tag, dh2b, wd_g, g, u)
    dwg = _dw_pair("dw_gate" + tag, hf, dg, False)
    dwu = _dw_pair("dw_up" + tag, hf, du, False)
    dhf = None
    for nm, dact, wfull in (("gate", dg, wg_g), ("up", du, wu_g)):
        dhf = _mm("dhf_%s%s" % (nm, tag), "nt", dact, wfull,
                  pl.BlockSpec((th, tk), lambda i, j, k: (i, k)),
                  pl.BlockSpec((None, 1024, tk), lambda i, j, k: (k, j, 0)),
                  pl.BlockSpec((th, 1024), half_rows), _sds((lp, D), F32), (2, 2, NCHIP), acc_shape=(th, 1024),
                  res=dhf, res_spec=None if dhf is None else pl.BlockSpec((th, 1024), half_rows))
    dh1, dh1b, dnfw = _norm_bwd("norm_ffn_bwd" + tag, dhf, h1, nfw, dh2)
    dh1b = ffn_done([dwg, dwu, dwd], dh1b)

    dwout = _mm("dw_out" + tag, "tn", mix, dh1b,
                pl.BlockSpec((None, lp, 1024), lambda i, j, k: (i, 0, 0)), pl.BlockSpec((lp, 1024), lambda i, j, k: (0, j)),
                pl.BlockSpec((1024, 1024), half_rows), _sds((D, D), BF16), (2, 2, 1))
    wout = wout_g.reshape(D, D)
    dmix = _mm("dmix" + tag, "nt", dh1b, wout,
               pl.BlockSpec((th, D), lambda i, j, k: (i, 0)), pl.BlockSpec((1024, D), lambda i, j, k: (j, 0)),
               pl.BlockSpec((th, 1024), half_rows), _sds((lp, D), F32), (2, 2, 1))
    dht, de, dmnw, dcw = _mix_bwd("mix_bwd" + tag, dmix, ht, e, mnw, cw)
    dq, dk, dv, dgrow, dfx = _mlstm_bwd("mlstm_bwd" + tag, qkv, grow, mcol, ccol, ht, dht, 8)
    dgp, dgpb, dbias = _gate_bwd("gate_bwd" + tag, gpre, bias, dgrow, dfx)
    del dgp
    dqkv = jnp.concatenate([dq, dk, dv], axis=1)

    hn_cols = pl.BlockSpec((lp, 1024), lambda i, j, k: (0, j))
    dwqkv = _mm("dw_qkv" + tag, "tn", dqkv, hn, pl.BlockSpec((lp, 1024), lambda i, j, k: (0, i)), hn_cols,
                pl.BlockSpec((1024, 1024), half_rows), _sds((2048, D), BF16), (2, 2, 1))
    dwe = _mm("dw_e" + tag, "tn", de, hn, pl.BlockSpec((None, lp, 1024), lambda i, j, k: (i, 0, 0)), hn_cols,
              pl.BlockSpec((None, 1024, 1024), lambda i, j, k: (i, 0, j)), _sds((4, 1024, D), BF16), (4, 2, 1))
    dwgt = _mm("dw_gates" + tag, "tn", dgpb, hn, pl.BlockSpec((lp, LANE), lambda i, j, k: (0, 0)), hn_cols,
               pl.BlockSpec((LANE, 1024), lambda i, j, k: (0, j)), _sds((LANE, D), BF16), (1, 2, 1))
    dwin = _in_grads(dwqkv, dwe, dwgt)

    dhn = _mm("dhn_qkv" + tag, "nn", dqkv, wqkv,
              pl.BlockSpec((th, 2048), lambda i, j, k: (i, 0)), pl.BlockSpec((2048, 1024), lambda i, j, k: (0, j)),
              pl.BlockSpec((th, 1024), half_rows), _sds((lp, D), F32), (2, 2, 1))
    dhn = _mm("dhn_e" + tag, "nn", de, we,
              pl.BlockSpec((None, th, 1024), lambda i, j, k: (k, i, 0)),
              pl.BlockSpec((None, 1024, 1024), lambda i, j, k: (k, 0, j)),
              pl.BlockSpec((th, 1024), half_rows), _sds((lp, D), F32), (2, 2, 4), acc_shape=(th, 1024),
              res=dhn, res_spec=pl.BlockSpec((th, 1024), half_rows))
    dhn = _mm("dhn_gates" + tag, "nn", dgpb, wgt,
              pl.BlockSpec((th, LANE), lambda i, j, k: (i, 0)), pl.BlockSpec((LANE, 1024), lambda i, j, k: (0, j)),
              pl.BlockSpec((th, 1024), half_rows), _sds((lp, D), F32), (2, 2, 1),
              res=dhn, res_spec=pl.BlockSpec((th, 1024), half_rows))
    dh0, dh0b, dnmw = _norm_bwd("norm_mix_bwd" + tag, dhn, h, nmw, dh1)

    pieces = [dwin, dwout.reshape(NCHIP, D // NCHIP, D)]
    smalls = (dnmw[0], dbias[0, :8], dcw[:3], dmnw[0], dnfw[0])
    return dh0, dh0b, pieces, smalls


def _pack_rows(parts):
    rows = []
    for p in parts:
        f = p.reshape(-1)
        pad = (-f.shape[0]) % LANE
        if pad:
            f = jnp.pad(f, (0, pad))
        rows.append(f.reshape(-1, LANE))
    r = jnp.concatenate(rows, axis=0)
    pad = (-r.shape[0]) % 8
    if pad:
        r = jnp.pad(r, ((0, pad), (0, 0)))
    return r


def _unpack_rows(pack, shapes):
    out, r0 = [], 0
    for s in shapes:
        n = 1
        for d in s:
            n *= d
        nr = -(-n // LANE)
        out.append(pack[r0:r0 + nr].reshape(-1)[:n].reshape(s))
        r0 += nr
    return out


def kernel(x, meta_tokens, norm_mix_w, w_in, b_gates, conv_w, mlstm_norm_w, w_out, norm_ffn_w, w_gate, w_up, w_down, norm_final_w, loss_target, m_meta_tokens, m_norm_mix_w, m_w_in, m_b_gates, m_conv_w, m_mlstm_norm_w, m_w_out, m_norm_ffn_w, m_w_gate, m_w_up, m_w_down, m_norm_final_w, v_meta_tokens, v_norm_mix_w, v_w_in, v_b_gates, v_conv_w, v_mlstm_norm_w, v_w_out, v_norm_ffn_w, v_w_gate, v_w_up, v_w_down, v_norm_final_w):
    seq = x.shape[1]
    n_real = N_META + seq
    lp = -(-n_real // LANE) * LANE
    xi, yi, ci = lax.axis_index("x"), lax.axis_index("y"), lax.axis_index("c")
    jchip = 2 * xi + yi
    pf = jnp.stack([ci, jchip, 2 * (1 - xi) + yi, 2 * xi + (1 - yi), 2 * (1 - xi) + (1 - yi)]).astype(jnp.int32)

    big = {"w_in": w_in, "w_out": w_out, "w_gate": w_gate, "w_up": w_up, "w_down": w_down}
    cast = {n: [_cast_into("cast_%s_l%d" % (n, l), w, l, pf) for l in range(DEPTH)] for n, w in big.items()
            if n != "w_in"}
    in_t = lambda a: jnp.transpose(a, (2, 0, 1))
    cast["w_in"] = [_cast_into_t("cast_w_in_l%d" % l, in_t(w_in), l, pf) for l in range(DEPTH)]
    conv_flat = jnp.pad(conv_w.reshape(DEPTH * 3, CW // NCHIP), ((0, 8 - DEPTH * 3), (0, 0)))

    def own_slot(a):
        return lax.dynamic_update_slice(jnp.zeros((NCHIP,) + a.shape, a.dtype), a[None], (jchip, 0, 0))

    groups, splits = [], []
    for l in range(DEPTH):
        groups.append([cast["w_in"][l]] + ([own_slot(meta_tokens), own_slot(conv_flat)] if l == 0 else []))
        splits.append([2] + ([0, 0] if l == 0 else []))
        groups += [[cast["w_out"][l]], [cast["w_gate"][l], cast["w_up"][l]], [cast["w_down"][l]]]
        splits += [[1], [1, 1], [1]]
    group_of = {"in": 0, "out": 1, "ffn": 2, "down": 3}
    sems, arrs, _ = _ag_start("ag_start_0", groups[:1], splits[:1])
    sems_rest, arrs_rest, all_started = _ag_start("ag_start_1", groups[1:], splits[1:])
    sems, arrs = sems + sems_rest, arrs + arrs_rest

    def gathered(gi, afters):
        got = _ag_wait("ag_wait_%d" % gi, arrs[gi], splits[gi], sems[gi][0], sems[gi][1], afters)
        axes = [s for s in splits[gi] if s]
        return list(_ag_forward("ag_forward_%d" % gi, got[:len(axes)], axes)) + list(got[len(axes):])

    win0_g, meta_g, conv_g = gathered(0, [all_started])
    meta_full = jnp.concatenate([meta_g[s] for s in range(NCHIP)], axis=1)
    conv_full = jnp.concatenate([conv_g[s][:DEPTH * 3] for s in range(NCHIP)], axis=1)
    conv_full = conv_full.reshape(DEPTH, 3, CW)

    bias_rows = jnp.pad(b_gates, ((0, 0), (0, LANE - 8)))
    smalls = []
    for l in range(DEPTH):
        smalls.append((norm_mix_w[l][None], bias_rows[l][None], mlstm_norm_w[l][None],
                       jnp.pad(conv_full[l], ((0, 5), (0, 0))), norm_ffn_w[l][None]))

    h = jnp.concatenate([meta_full, x[0], jnp.zeros((lp - n_real, D), F32)], axis=0)
    saved, wts = [], []
    for l in range(DEPTH):
        def get(which, after, l=l):
            if l == 0 and which == "in":
                return [win0_g]
            return gathered(4 * l + group_of[which], [after])

        h, sv, wt = _layer_fwd(l, h, get, smalls[l])
        saved.append(sv)
        wts.append(wt)
    tgt = jnp.pad(loss_target[0], ((N_META, lp - n_real), (0, 0)))
    dh, dhb, dnorm_final, loss_part = _loss_head(h, tgt, norm_final_w[None], n_real)

    names = ["w_in", "w_out", "w_gate", "w_up", "w_down"]
    params = {"w_in": (w_in, m_w_in, v_w_in), "w_out": (w_out, m_w_out, v_w_out), "w_gate": (w_gate, m_w_gate, v_w_gate),
              "w_up": (w_up, m_w_up, v_w_up), "w_down": (w_down, m_w_down, v_w_down)}
    big_out = {n: None for n in names}
    small_grads = [None] * DEPTH
    g_in = [None] * DEPTH

    def finish(l, group, handle, afters):
        for n, (g_mine, g_theirs) in zip(group, _rs_end(handle, afters, pf)):
            if n == "w_in":
                g_in[l] = (g_mine, g_theirs)
                continue
            w, m, v = params[n]
            big_out[n] = _adamw_layer("adamw_%s_l%d" % (n, l), l, g_mine, g_theirs, w, m, v, big_out[n], pf)
        return [big_out[n][3] for n in group if n != "w_in"]

    groups = []
    token = None
    for l in reversed(range(DEPTH)):
        def ffn_done(pieces, dh1b, l=l):
            handle, tok = _rs_begin("l%df" % l, pieces, [False] * 3, pf)
            groups.append((l, names[2:], handle))
            return dh1b + tok[0, 0].astype(BF16)

        dh, dhb, pieces, small_grads[l] = _layer_bwd(l, dh, dhb, saved[l], wts[l], smalls[l], ffn_done)
        handle, token = _rs_begin("l%dm" % l, pieces, [True, False], pf)
        groups.append((l, names[:2], handle))
        if l > 0:
            dhb = dhb + token[0, 0].astype(BF16)
    afters = [token]
    for l, group, handle in groups:
        afters = finish(l, group, handle, afters)
    big_out["w_in"] = [jnp.transpose(a, (1, 2, 0))
                       for a in _adamw_t("adamw_w_in", g_in, in_t(w_in), in_t(m_w_in), in_t(v_w_in), pf)]

    dnmw = jnp.stack([small_grads[l][0] for l in range(DEPTH)])
    dbias = jnp.stack([small_grads[l][1] for l in range(DEPTH)])
    dconv = jnp.stack([small_grads[l][2] for l in range(DEPTH)])
    dmnw = jnp.stack([small_grads[l][3] for l in range(DEPTH)])
    dnfw = jnp.stack([small_grads[l][4] for l in range(DEPTH)])
    part_shapes = [(N_META, D), (DEPTH, D), (DEPTH, 8), (DEPTH, 3, CW), (DEPTH, MW), (DEPTH, D), (D,), (LANE,)]
    pack = _pack_rows([dh[:N_META], dnmw, dbias, dconv, dmnw, dnfw, dnorm_final[0], loss_part[0]])
    tot = _unpack_rows(_allreduce_small(pack), part_shapes)
    g_meta_full, g_nmw, g_bias, g_conv_full, g_mnw, g_nfw, g_final, loss_row = tot
    mcols = D // NCHIP
    ccols = CW // NCHIP
    g_meta = lax.dynamic_slice_in_dim(g_meta_full, jchip * mcols, mcols, axis=1)
    g_conv = lax.dynamic_slice_in_dim(g_conv_full, jchip * ccols, ccols, axis=2)
    sm_g = [g_meta, g_nmw, g_bias, g_conv, g_mnw, g_nfw, g_final]
    sm_w = [meta_tokens, norm_mix_w, b_gates, conv_w, mlstm_norm_w, norm_ffn_w, norm_final_w]
    sm_m = [m_meta_tokens, m_norm_mix_w, m_b_gates, m_conv_w, m_mlstm_norm_w, m_norm_ffn_w, m_norm_final_w]
    sm_v = [v_meta_tokens, v_norm_mix_w, v_b_gates, v_conv_w, v_mlstm_norm_w, v_norm_ffn_w, v_norm_final_w]
    sm_shapes = [w.shape for w in sm_w]
    d_p, m_p, v_p = _adamw_flat(_pack_rows(sm_g), _pack_rows(sm_w), _pack_rows(sm_m), _pack_rows(sm_v))
    sm_d = _unpack_rows(d_p, sm_shapes)
    sm_nm = _unpack_rows(m_p, sm_shapes)
    sm_nv = _unpack_rows(v_p, sm_shapes)

    loss = loss_row[0]
    grad_x = dh[N_META:n_real][None]

    def ordered(sm, which):
        bo = {n: big_out[n][which] for n in names}
        return [sm[0], sm[1], bo["w_in"], sm[2], sm[3], sm[4], bo["w_out"], sm[5], bo["w_gate"], bo["w_up"], bo["w_down"], sm[6]]

    return (loss, grad_x, *ordered(sm_g, 0), *ordered(sm_d, 1), *ordered(sm_nm, 2), *ordered(sm_nv, 3))
```

```python
import functools

import jax
import jax.numpy as jnp
from jax import lax
from jax.experimental import pallas as pl
from jax.experimental.pallas import tpu as pltpu

F32 = jnp.float32
BF16 = jnp.bfloat16

D = 2048
N_META = 16
HEADS = 4
DQK = 128
DV = 256
MW = HEADS * DV
CW = D - MW
QKW = HEADS * DQK
DFF = 5632
DIN = 6152
NCHIP = 4
DEPTH = 2
GATE_CAP = 15.0
EPS = 1e-6
QK_SCALE = DQK ** -0.5
LANE = 128
VMEM_LIMIT = 60 * 1024 * 1024

ADAM_LR = 0.001
ADAM_B1 = 0.9
ADAM_B2 = 0.999
ADAM_EPS = 1e-08
ADAM_WD = 0.01
ADAM_STEP = 10

MESH = pl.DeviceIdType.MESH
ANY = pl.BlockSpec(memory_space=pl.ANY)


def _cp(sem):
    return pltpu.CompilerParams(dimension_semantics=sem, vmem_limit_bytes=VMEM_LIMIT)


def _sds(shape, dtype):
    return jax.ShapeDtypeStruct(shape, dtype)


_DN = {"nn": (((1,), (0,)), ((), ())), "nt": (((1,), (1,)), ((), ())), "tn": (((0,), (0,)), ((), ()))}


def _mm(name, kind, a, b, a_spec, b_spec, o_spec, out_shape, grid, acc_shape=None, res=None, res_spec=None,
        dep=None):
    nk = grid[2]
    has_res = res is not None
    n_in = 2 + has_res + (dep is not None)

    def body(*refs):
        a_ref, b_ref = refs[0], refs[1]
        r_ref = refs[2] if has_res else None
        o_ref = refs[n_in]
        p = lax.dot_general(a_ref[...], b_ref[...], _DN[kind], preferred_element_type=F32)
        if nk == 1:
            if r_ref is not None:
                p = p + r_ref[...]
            o_ref[...] = p.astype(o_ref.dtype)
        else:
            acc = refs[-1]
            k = pl.program_id(2)

            @pl.when(k == 0)
            def _():
                acc[...] = p

            @pl.when(jnp.logical_and(k > 0, k < nk - 1))
            def _():
                acc[...] += p

            @pl.when(k == nk - 1)
            def _():
                r = acc[...] + p
                if r_ref is not None:
                    r = r + r_ref[...]
                o_ref[...] = r.astype(o_ref.dtype)

    ins = [a, b] + ([res] if has_res else []) + ([dep] if dep is not None else [])
    in_specs = [a_spec, b_spec] + ([res_spec] if has_res else []) + ([ANY] if dep is not None else [])
    scratch = [pltpu.VMEM(acc_shape, F32)] if nk > 1 else []
    return pl.pallas_call(
        body, name=name, grid=grid, in_specs=in_specs, out_specs=o_spec, out_shape=out_shape,
        scratch_shapes=scratch, compiler_params=_cp(("parallel", "parallel", "arbitrary")))(*ins)


def _norm_fwd(name, h, w):
    lp = h.shape[0]
    tm = lp // 4

    def body(h_ref, w_ref, o_ref):
        x = h_ref[...]
        r = lax.rsqrt(jnp.mean(x * x, axis=1, keepdims=True) + EPS)
        o_ref[...] = (x * r * w_ref[...]).astype(BF16)

    return pl.pallas_call(
        body, name=name, grid=(4,),
        in_specs=[pl.BlockSpec((tm, D), lambda i: (i, 0)), pl.BlockSpec((1, D), lambda i: (0, 0))],
        out_specs=pl.BlockSpec((tm, D), lambda i: (i, 0)), out_shape=_sds((lp, D), BF16),
        compiler_params=_cp(("parallel",)))(h, w)


def _norm_bwd(name, dy, h, w, dres):
    lp = h.shape[0]
    tm = lp // 8

    def body(dy_ref, h_ref, w_ref, dres_ref, dh_ref, dhb_ref, dw_ref):
        x = h_ref[...]
        r = lax.rsqrt(jnp.mean(x * x, axis=1, keepdims=True) + EPS)
        xh = x * r
        dy_v = dy_ref[...]
        dxh = dy_v * w_ref[...]
        dx = r * (dxh - xh * jnp.mean(dxh * xh, axis=1, keepdims=True))
        dh = dres_ref[...] + dx
        dh_ref[...] = dh
        dhb_ref[...] = dh.astype(BF16)

        @pl.when(pl.program_id(0) == 0)
        def _():
            dw_ref[...] = jnp.zeros_like(dw_ref)

        dw_ref[0:1, :] += jnp.sum(dy_v * xh, axis=0, keepdims=True)

    row = pl.BlockSpec((tm, D), lambda i: (i, 0))
    return pl.pallas_call(
        body, name=name, grid=(8,),
        in_specs=[row, row, pl.BlockSpec((1, D), lambda i: (0, 0)), row],
        out_specs=[row, row, pl.BlockSpec((8, D), lambda i: (0, 0))],
        out_shape=[_sds((lp, D), F32), _sds((lp, D), BF16), _sds((8, D), F32)],
        compiler_params=_cp(("arbitrary",)))(dy, h, w, dres)


def _loss_head(h, tgt, w, n_real):
    lp = h.shape[0]
    tm = lp // 8

    def body(h_ref, t_ref, w_ref, dh_ref, dhb_ref, dw_ref, loss_ref):
        i = pl.program_id(0)
        x = h_ref[...]
        r = lax.rsqrt(jnp.mean(x * x, axis=1, keepdims=True) + EPS)
        xh = x * r
        wv = w_ref[...]
        row = i * tm + lax.broadcasted_iota(jnp.int32, (tm, 1), 0)
        valid = jnp.logical_and(row >= N_META, row < n_real)
        err = jnp.where(valid, xh * wv - t_ref[...], 0.0)
        dy_v = err * (1.0 / D)
        dxh = dy_v * wv
        dx = r * (dxh - xh * jnp.mean(dxh * xh, axis=1, keepdims=True))
        dh_ref[...] = dx
        dhb_ref[...] = dx.astype(BF16)

        @pl.when(i == 0)
        def _():
            dw_ref[...] = jnp.zeros_like(dw_ref)
            loss_ref[...] = jnp.zeros_like(loss_ref)

        dw_ref[0:1, :] += jnp.sum(dy_v * xh, axis=0, keepdims=True)
        part = jnp.sum(jnp.sum(err * err, axis=1, keepdims=True), axis=0, keepdims=True) * (0.5 / D)
        loss_ref[...] += jnp.broadcast_to(part, loss_ref.shape)

    row = pl.BlockSpec((tm, D), lambda i: (i, 0))
    return pl.pallas_call(
        body, name="loss_head", grid=(8,),
        in_specs=[row, row, pl.BlockSpec((1, D), lambda i: (0, 0))],
        out_specs=[row, row, pl.BlockSpec((8, D), lambda i: (0, 0)), pl.BlockSpec((8, LANE), lambda i: (0, 0))],
        out_shape=[_sds((lp, D), F32), _sds((lp, D), BF16), _sds((8, D), F32), _sds((8, LANE), F32)],
        compiler_params=_cp(("arbitrary",)))(h, tgt, w)


def _ffn_up(name, hf, wg4, wu4):
    lp = hf.shape[0]
    tm = lp // 4
    tn = DFF // NCHIP

    def body(x_ref, wg_ref, wu_ref, g_ref, u_ref, a_ref):
        x = x_ref[...]
        g = jnp.dot(x, wg_ref[...], preferred_element_type=F32)
        u = jnp.dot(x, wu_ref[...], preferred_element_type=F32)
        g_ref[...] = g.astype(BF16)
        u_ref[...] = u.astype(BF16)
        a_ref[...] = (g * jax.nn.sigmoid(g) * u).astype(BF16)

    wspec = pl.BlockSpec((None, D, tn), lambda j, i: (j, 0, 0))
    ospec = pl.BlockSpec((tm, tn), lambda j, i: (i, j))
    o = _sds((lp, DFF), BF16)
    return pl.pallas_call(
        body, name=name, grid=(NCHIP, 4),
        in_specs=[pl.BlockSpec((tm, D), lambda j, i: (i, 0)), wspec, wspec],
        out_specs=[ospec, ospec, ospec], out_shape=[o, o, o],
        compiler_params=_cp(("parallel", "parallel")))(hf, wg4, wu4)


def _ffn_bwd_act(name, dhb, wd4, g, u, dep=None):
    lp = dhb.shape[0]
    tm = lp // 4
    tn = DFF // NCHIP
    deps = [] if dep is None else [dep]

    def body(d_ref, w_ref, g_ref, u_ref, *rest):
        dg_ref, du_ref = rest[len(deps):]
        da = lax.dot_general(d_ref[...], w_ref[...], _DN["nt"], preferred_element_type=F32)
        gv = g_ref[...].astype(F32)
        uv = u_ref[...].astype(F32)
        sg = jax.nn.sigmoid(gv)
        dg_ref[...] = (da * uv * (sg * (1.0 + gv * (1.0 - sg)))).astype(BF16)
        du_ref[...] = (da * (gv * sg)).astype(BF16)

    ospec = pl.BlockSpec((tm, tn), lambda j, i: (i, j))
    o = _sds((lp, DFF), BF16)
    return pl.pallas_call(
        body, name=name, grid=(NCHIP, 4),
        in_specs=[pl.BlockSpec((tm, D), lambda j, i: (i, 0)),
                  pl.BlockSpec((None, tn, D), lambda j, i: (j, 0, 0)), ospec, ospec] + [ANY] * len(deps),
        out_specs=[ospec, ospec], out_shape=[o, o],
        compiler_params=_cp(("parallel", "parallel")))(dhb, wd4, g, u, *deps)


def _ffn_bwd_in(name, dg, du, wg4, wu4):
    lp = dg.shape[0]
    th = lp // 2
    tk = DFF // NCHIP

    def body(dg_ref, du_ref, wg_ref, wu_ref, o_ref, acc):
        k = pl.program_id(2)
        p = (lax.dot_general(dg_ref[...], wg_ref[...], _DN["nt"], preferred_element_type=F32)
             + lax.dot_general(du_ref[...], wu_ref[...], _DN["nt"], preferred_element_type=F32))

        @pl.when(k == 0)
        def _():
            acc[...] = p

        @pl.when(jnp.logical_and(k > 0, k < NCHIP - 1))
        def _():
            acc[...] += p

        @pl.when(k == NCHIP - 1)
        def _():
            o_ref[...] = acc[...] + p

    a_spec = pl.BlockSpec((th, tk), lambda i, j, k: (i, k))
    w_spec = pl.BlockSpec((None, 1024, tk), lambda i, j, k: (k, j, 0))
    return pl.pallas_call(
        body, name=name, grid=(2, 2, NCHIP), in_specs=[a_spec, a_spec, w_spec, w_spec],
        out_specs=pl.BlockSpec((th, 1024), lambda i, j, k: (i, j)), out_shape=_sds((lp, D), F32),
        scratch_shapes=[pltpu.VMEM((th, 1024), F32)],
        compiler_params=_cp(("parallel", "parallel", "arbitrary")))(dg, du, wg4, wu4)


def _shift_rows(x, d, row):
    return jnp.where(row >= d, pltpu.roll(x, d, axis=0), 0.0)


def _scan_steps(lp):
    d = 1
    while d < lp:
        yield d
        d *= 2


def _gate_values(pre):
    t = GATE_CAP * jnp.tanh(pre * (1.0 / GATE_CAP))
    lf = jnp.minimum(t, 0.0) - jnp.log(1.0 + jnp.exp(-jnp.abs(t)))
    return t, lf


def _gate_prep(name, gates_pre, bias):
    lp = gates_pre.shape[0]

    def body(p_ref, b_ref, grow_ref, m_ref, c_ref):
        pre = p_ref[...] + b_ref[...]
        lane = lax.broadcasted_iota(jnp.int32, (lp, LANE), 1)
        row = lax.broadcasted_iota(jnp.int32, (lp, LANE), 0)
        t, lf = _gate_values(pre)
        f = jnp.where(jnp.logical_and(lane >= HEADS, lane < 2 * HEADS), lf, 0.0)
        for d in _scan_steps(lp):
            f = f + _shift_rows(f, d, row)
        fs = pltpu.roll(f, LANE - HEADS, axis=1)
        g = jnp.where(lane < HEADS, t - fs, 0.0)
        m = g
        for d in _scan_steps(lp):
            m = jnp.maximum(m, jnp.where(row >= d, pltpu.roll(m, d, axis=0), m))
        grow_ref[...] = g.T
        m_ref[...] = m
        c_ref[...] = jnp.where(lane < HEADS, -fs - m, 0.0)

    full = pl.BlockSpec((lp, LANE), lambda: (0, 0))
    return pl.pallas_call(
        body, name=name, in_specs=[full, pl.BlockSpec((1, LANE), lambda: (0, 0))],
        out_specs=[pl.BlockSpec((LANE, lp), lambda: (0, 0)), full, full],
        out_shape=[_sds((LANE, lp), F32), _sds((lp, LANE), F32), _sds((lp, LANE), F32)],
        compiler_params=pltpu.CompilerParams(vmem_limit_bytes=VMEM_LIMIT))(gates_pre, bias)


def _pick_lane(blk, h):
    lane = lax.broadcasted_iota(jnp.int32, blk.shape, 1)
    return jnp.sum(jnp.where(lane == h, blk, 0.0), axis=1, keepdims=True)


def _mlstm_weights(q, k, grow, mcol, i, bq, nk):
    s = lax.dot_general(q, k, _DN["nt"], preferred_element_type=F32) * QK_SCALE
    row = i * bq + lax.broadcasted_iota(jnp.int32, (bq, 1), 0)
    col = lax.broadcasted_iota(jnp.int32, (1, nk), 1)
    a = jnp.where(col <= row, jnp.exp(jnp.minimum(grow - mcol, 0.0)), 0.0)
    return s, a


def _per_query_tile(i, nq, bq, lp, compute):
    for ii in range(nq):
        nk = min(lp, -(-((ii + 1) * bq) // LANE) * LANE)
        pl.when(i == ii)(functools.partial(compute, nk))


def _mlstm_fwd(name, qkv, grow, mcol_all, ccol_all, nq):
    lp = qkv.shape[0]
    bq = lp // nq

    def body(q_ref, k_ref, v_ref, grow_ref, m_ref, c_ref, o_ref):
        h = pl.program_id(0)
        i = pl.program_id(1)
        mcol = _pick_lane(m_ref[...], h)
        ccol = _pick_lane(c_ref[...], h)

        def compute(nk):
            grow_h = grow_ref[pl.ds(h, 1), 0:nk]
            s, a = _mlstm_weights(q_ref[...], k_ref[0:nk, :], grow_h, mcol, i, bq, nk)
            p = a * s
            den = jnp.sum(p, axis=1, keepdims=True)
            num = jnp.dot(p.astype(BF16), v_ref[0:nk, :], preferred_element_type=F32)
            o_ref[...] = num / jnp.maximum(jnp.abs(den), jnp.exp(ccol))

        _per_query_tile(i, nq, bq, lp, compute)

    return pl.pallas_call(
        body, name=name, grid=(HEADS, nq),
        in_specs=[pl.BlockSpec((bq, DQK), lambda h, i: (i, h)),
                  pl.BlockSpec((lp, DQK), lambda h, i: (0, HEADS + h)),
                  pl.BlockSpec((lp, DV), lambda h, i: (0, HEADS + h)),
                  pl.BlockSpec((8, lp), lambda h, i: (0, 0)),
                  pl.BlockSpec((bq, LANE), lambda h, i: (i, 0)),
                  pl.BlockSpec((bq, LANE), lambda h, i: (i, 0))],
        out_specs=pl.BlockSpec((bq, DV), lambda h, i: (i, h)),
        out_shape=_sds((lp, MW), F32),
        compiler_params=_cp(("parallel", "parallel")))(qkv, qkv, qkv, grow, mcol_all, ccol_all)


def _mlstm_bwd(name, qkv, grow, mcol_all, ccol_all, ht, dht, nq):
    lp = qkv.shape[0]
    bq = lp // nq

    def body(q_ref, k_ref, v_ref, grow_ref, m_ref, c_ref, ht_ref, dht_ref,
             dq_ref, dk_ref, dv_ref, dgrow_ref, dfx_ref, dkt_acc, dvt_acc):
        h = pl.program_id(0)
        i = pl.program_id(1)

        @pl.when(jnp.logical_and(h == 0, i == 0))
        def _():
            dgrow_ref[...] = jnp.zeros_like(dgrow_ref)
            dfx_ref[...] = jnp.zeros_like(dfx_ref)

        @pl.when(i == 0)
        def _():
            dkt_acc[...] = jnp.zeros_like(dkt_acc)
            dvt_acc[...] = jnp.zeros_like(dvt_acc)

        mcol = _pick_lane(m_ref[...], h)
        ccol = _pick_lane(c_ref[...], h)

        def compute(nk):
            q = q_ref[...]
            k = k_ref[0:nk, :]
            v = v_ref[0:nk, :]
            grow_h = grow_ref[pl.ds(h, 1), 0:nk]
            s, a = _mlstm_weights(q, k, grow_h, mcol, i, bq, nk)
            p = a * s
            den = jnp.sum(p, axis=1, keepdims=True)
            clamp = jnp.exp(ccol)
            active = jnp.abs(den) < clamp
            dd = jnp.maximum(jnp.abs(den), clamp)
            dht_v = dht_ref[...]
            hdh = jnp.sum(dht_v * ht_ref[...], axis=1, keepdims=True)
            dn = (dht_v / dd).astype(BF16)
            dden = jnp.where(active, 0.0, -(hdh / dd) * jnp.sign(den))
            dp = lax.dot_general(dn, v, _DN["nt"], preferred_element_type=F32) + dden
            rmat = dp * p
            dgrow_ref[pl.ds(h, 1), 0:nk] += jnp.sum(rmat, axis=0, keepdims=True)
            ds = (dp * a * QK_SCALE).astype(BF16)
            dq_ref[...] = jnp.dot(ds, k, preferred_element_type=F32).astype(BF16)
            dkt_acc[:, 0:nk] += lax.dot_general(q, ds, _DN["tn"], preferred_element_type=F32)
            dvt_acc[:, 0:nk] += lax.dot_general(dn, p.astype(BF16), _DN["tn"], preferred_element_type=F32)
            lane = lax.broadcasted_iota(jnp.int32, (bq, LANE), 1)
            r0 = pl.multiple_of(i * bq, 16)
            dfx_ref[pl.ds(r0, bq), :] += jnp.where(lane == h, jnp.sum(rmat, axis=1, keepdims=True), 0.0)

        _per_query_tile(i, nq, bq, lp, compute)

        @pl.when(i == nq - 1)
        def _():
            dk_ref[...] = dkt_acc[...].T.astype(BF16)
            dv_ref[...] = dvt_acc[...].T.astype(BF16)

    return pl.pallas_call(
        body, name=name, grid=(HEADS, nq),
        in_specs=[pl.BlockSpec((bq, DQK), lambda h, i: (i, h)),
                  pl.BlockSpec((lp, DQK), lambda h, i: (0, HEADS + h)),
                  pl.BlockSpec((lp, DV), lambda h, i: (0, HEADS + h)),
                  pl.BlockSpec((8, lp), lambda h, i: (0, 0)),
                  pl.BlockSpec((bq, LANE), lambda h, i: (i, 0)),
                  pl.BlockSpec((bq, LANE), lambda h, i: (i, 0)),
                  pl.BlockSpec((bq, DV), lambda h, i: (i, h)),
                  pl.BlockSpec((bq, DV), lambda h, i: (i, h))],
        out_specs=[pl.BlockSpec((bq, DQK), lambda h, i: (i, h)),
                   pl.BlockSpec((lp, DQK), lambda h, i: (0, h)),
                   pl.BlockSpec((lp, DV), lambda h, i: (0, h)),
                   pl.BlockSpec((LANE, lp), lambda h, i: (0, 0)),
                   pl.BlockSpec((lp, LANE), lambda h, i: (0, 0))],
        out_shape=[_sds((lp, QKW), BF16), _sds((lp, QKW), BF16), _sds((lp, MW), BF16),
                   _sds((LANE, lp), F32), _sds((lp, LANE), F32)],
        scratch_shapes=[pltpu.VMEM((DQK, lp), F32), pltpu.VMEM((DV, lp), F32)],
        compiler_params=_cp(("arbitrary", "arbitrary")))(qkv, qkv, qkv, grow, mcol_all, ccol_all, ht, dht)


def _gate_bwd(name, gates_pre, bias, dgrow, dfx):
    lp = gates_pre.shape[0]

    def body(p_ref, b_ref, dgrow_ref, dfx_ref, dg_ref, dgb_ref, db_ref):
        pre = p_ref[...] + b_ref[...]
        lane = lax.broadcasted_iota(jnp.int32, (lp, LANE), 1)
        row = lax.broadcasted_iota(jnp.int32, (lp, LANE), 0)
        th = jnp.tanh(pre * (1.0 / GATE_CAP))
        t = GATE_CAP * th
        dgc = jnp.where(lane < HEADS, dgrow_ref[...].T, 0.0)
        df = jnp.where(lane < HEADS, dfx_ref[...] - dgc, 0.0)
        for d in _scan_steps(lp):
            df = df + jnp.where(row < lp - d, pltpu.roll(df, lp - d, axis=0), 0.0)
        dlf = pltpu.roll(df, HEADS, axis=1)
        dt = jnp.where(lane < HEADS, dgc, dlf * jax.nn.sigmoid(-t))
        dpre = jnp.where(lane < 2 * HEADS, dt * (1.0 - th * th), 0.0)
        dg_ref[...] = dpre
        dgb_ref[...] = dpre.astype(BF16)
        db_ref[...] = jnp.broadcast_to(jnp.sum(dpre, axis=0, keepdims=True), db_ref.shape)

    full = pl.BlockSpec((lp, LANE), lambda: (0, 0))
    return pl.pallas_call(
        body, name=name,
        in_specs=[full, pl.BlockSpec((1, LANE), lambda: (0, 0)), pl.BlockSpec((LANE, lp), lambda: (0, 0)), full],
        out_specs=[full, full, pl.BlockSpec((8, LANE), lambda: (0, 0))],
        out_shape=[_sds((lp, LANE), F32), _sds((lp, LANE), BF16), _sds((8, LANE), F32)],
        compiler_params=pltpu.CompilerParams(vmem_limit_bytes=VMEM_LIMIT))(gates_pre, bias, dgrow, dfx)


CB = 256


def _e_specs(lp):
    return [pl.BlockSpec((None, lp, CB), functools.partial(lambda c, j: (c, 0, j), c)) for c in range(4)]


def _mix_fwd(name, ht, e, mnw, cw):
    lp = ht.shape[0]

    def body(ht_ref, og_ref, u_ref, gb_ref, gc_ref, mnw_ref, cw_ref, o_ref):
        x = ht_ref[...]
        r = lax.rsqrt(jnp.mean(x * x, axis=1, keepdims=True) + EPS)
        o_ref[0] = (jax.nn.sigmoid(og_ref[...].astype(F32)) * (x * r * mnw_ref[...])).astype(BF16)
        row = lax.broadcasted_iota(jnp.int32, (lp, CB), 0)
        a = gc_ref[...].astype(F32) * u_ref[...].astype(F32)
        conv = cw_ref[2:3, :] * a + cw_ref[1:2, :] * _shift_rows(a, 1, row) + cw_ref[0:1, :] * _shift_rows(a, 2, row)
        o_ref[1] = (gb_ref[...].astype(F32) * conv).astype(BF16)

    col = pl.BlockSpec((lp, CB), lambda j: (0, j))
    return pl.pallas_call(
        body, name=name, grid=(4,),
        in_specs=[col] + _e_specs(lp) + [pl.BlockSpec((1, CB), lambda j: (0, j)), pl.BlockSpec((8, CB), lambda j: (0, j))],
        out_specs=pl.BlockSpec((2, lp, CB), lambda j: (0, 0, j)), out_shape=_sds((2, lp, MW), BF16),
        compiler_params=_cp(("parallel",)))(ht, e, e, e, e, mnw, cw)


def _mix_bwd(name, dmix, ht, e, mnw, cw):
    lp = ht.shape[0]

    def body(dhm_ref, dhc_ref, ht_ref, og_ref, u_ref, gb_ref, gc_ref, mnw_ref, cw_ref,
             dht_ref, de_ref, dmnw_ref, dcw_ref):
        x = ht_ref[...]
        r = lax.rsqrt(jnp.mean(x * x, axis=1, keepdims=True) + EPS)
        xh = x * r
        w = mnw_ref[...]
        sg = jax.nn.sigmoid(og_ref[...].astype(F32))
        dhm = dhm_ref[...]
        de_ref[0] = (dhm * (xh * w) * (sg * (1.0 - sg))).astype(BF16)
        dn = dhm * sg
        dmnw_ref[...] = jnp.broadcast_to(jnp.sum(dn * xh, axis=0, keepdims=True), dmnw_ref.shape)
        dxh = dn * w
        dht_ref[...] = r * (dxh - xh * jnp.mean(dxh * xh, axis=1, keepdims=True))

        row = lax.broadcasted_iota(jnp.int32, (lp, CB), 0)
        uv = u_ref[...].astype(F32)
        gcv = gc_ref[...].astype(F32)
        gbv = gb_ref[...].astype(F32)
        a = gcv * uv
        a1 = _shift_rows(a, 1, row)
        a2 = _shift_rows(a, 2, row)
        dhc = dhc_ref[...]
        conv = cw_ref[2:3, :] * a + cw_ref[1:2, :] * a1 + cw_ref[0:1, :] * a2
        de_ref[2] = (dhc * conv).astype(BF16)
        dconv = dhc * gbv
        dcw_ref[...] = jnp.zeros_like(dcw_ref)
        dcw_ref[0:1, :] = jnp.sum(dconv * a2, axis=0, keepdims=True)
        dcw_ref[1:2, :] = jnp.sum(dconv * a1, axis=0, keepdims=True)
        dcw_ref[2:3, :] = jnp.sum(dconv * a, axis=0, keepdims=True)
        up1 = jnp.where(row < lp - 1, pltpu.roll(dconv, lp - 1, axis=0), 0.0)
        up2 = jnp.where(row < lp - 2, pltpu.roll(dconv, lp - 2, axis=0), 0.0)
        da = cw_ref[2:3, :] * dconv + cw_ref[1:2, :] * up1 + cw_ref[0:1, :] * up2
        de_ref[1] = (da * gcv).astype(BF16)
        de_ref[3] = (da * uv).astype(BF16)

    col = pl.BlockSpec((lp, CB), lambda j: (0, j))
    small = pl.BlockSpec((8, CB), lambda j: (0, j))
    return pl.pallas_call(
        body, name=name, grid=(4,),
        in_specs=[col, pl.BlockSpec((lp, CB), lambda j: (0, 4 + j)), col] + _e_specs(lp)
                 + [pl.BlockSpec((1, CB), lambda j: (0, j)), small],
        out_specs=[col, pl.BlockSpec((4, lp, CB), lambda j: (0, 0, j)), small, small],
        out_shape=[_sds((lp, MW), F32), _sds((4, lp, MW), BF16), _sds((8, MW), F32), _sds((8, CW), F32)],
        compiler_params=_cp(("parallel",)))(dmix, dmix, ht, e, e, e, e, mnw, cw)


def _row_tile(r, c, itemsize, budget=1536 * 1024, mult=16):
    best = None
    for t in range(mult, r + 1, mult):
        if r % t == 0 and t * c * itemsize <= budget:
            best = t
    if best is None:
        best = r
    return best


def _grid_spec(grid, in_specs, out_specs, scratch=()):
    return pltpu.PrefetchScalarGridSpec(num_scalar_prefetch=1, grid=grid, in_specs=in_specs,
                                        out_specs=out_specs, scratch_shapes=list(scratch))


def _cast_into(name, w, layer, pf):
    _, r, c = w.shape
    tr = _row_tile(r, c, 4)

    def body(pf_ref, x_ref, o_ref):
        o_ref[...] = x_ref[...].astype(BF16)

    return pl.pallas_call(
        body, name=name, out_shape=_sds((NCHIP, r, c), BF16),
        grid_spec=_grid_spec((r // tr,), [pl.BlockSpec((None, tr, c), lambda i, pf: (layer, i, 0))],
                             pl.BlockSpec((None, tr, c), lambda i, pf: (pf[1], i, 0))),
        compiler_params=_cp(("parallel",)))(pf, w)


TCOL = 256


def _cast_into_t(name, w_t, layer, pf):
    c, nl, r = w_t.shape

    def body(pf_ref, x_ref, o_ref):
        o_ref[...] = x_ref[:, layer, :].astype(BF16)

    return pl.pallas_call(
        body, name=name, out_shape=_sds((NCHIP, c, r), BF16),
        grid_spec=_grid_spec((r // TCOL,), [pl.BlockSpec((c, nl, TCOL), lambda i, pf: (0, 0, i))],
                             pl.BlockSpec((None, c, TCOL), lambda i, pf: (pf[1], 0, i))),
        compiler_params=_cp(("parallel",)))(pf, w_t)


def _add2_bf16(name, dw, got, pf, by_cols=False):
    n4, r2, c2 = got.shape

    def body(pf_ref, a_ref, b_ref, o_ref):
        o_ref[...] = (a_ref[...].astype(F32) + b_ref[...].astype(F32)).astype(BF16)

    if by_cols:
        nch = c2 // TCOL
        spec = pl.BlockSpec((None, r2, TCOL), lambda s, i, pf: (s, 0, i))
        mine = pl.BlockSpec((None, r2, TCOL), lambda s, i, pf: (s, 0, pf[0] * nch + i))
    else:
        tr = _row_tile(r2, c2, 4)
        nch = r2 // tr
        spec = pl.BlockSpec((None, tr, c2), lambda s, i, pf: (s, i, 0))
        mine = pl.BlockSpec((None, tr, c2), lambda s, i, pf: (s, pf[0] * nch + i, 0))
    return pl.pallas_call(
        body, name=name, out_shape=_sds((n4, r2, c2), BF16),
        grid_spec=_grid_spec((n4, nch), [mine, spec], spec),
        compiler_params=_cp(("parallel", "parallel")))(pf, dw, got)


def _adam_math(w, g, m, v):
    m2 = ADAM_B1 * m + (1.0 - ADAM_B1) * g
    v2 = ADAM_B2 * v + (1.0 - ADAM_B2) * (g * g)
    m_hat = m2 / (1.0 - ADAM_B1 ** ADAM_STEP)
    v_hat = v2 / (1.0 - ADAM_B2 ** ADAM_STEP)
    delta = -ADAM_LR * (m_hat / (jnp.sqrt(v_hat) + ADAM_EPS) + ADAM_WD * w)
    return delta, m2, v2


def _adamw_layer(name, layer, g_mine, g_theirs, w, m, v, prev, pf):
    _, r, c = w.shape
    r2 = r // 2
    tr = _row_tile(r2, c, 4, budget=1024 * 1024, mult=8)
    nch = r2 // tr
    n_alias = 0 if prev is None else 4

    def body(*refs):
        pf_ref, gm_ref, gt_ref, w_ref, m_ref, v_ref = refs[:6]
        go_ref, d_ref, mo_ref, vo_ref = refs[6 + n_alias:]
        mine = (pl.program_id(0) // nch) == pf_ref[0]
        gv = jnp.where(mine, gm_ref[...], gt_ref[...])
        delta, m2, v2 = _adam_math(w_ref[...], gv, m_ref[...], v_ref[...])
        go_ref[...] = gv
        d_ref[...] = delta
        mo_ref[...] = m2
        vo_ref[...] = v2

    slab = pl.BlockSpec((None, tr, c), lambda i, pf: (layer, i, 0))
    ins = [g_mine, g_theirs, w, m, v] + (list(prev) if prev is not None else [])
    in_specs = [pl.BlockSpec((tr, c), lambda i, pf: (jnp.clip(i - pf[0] * nch, 0, nch - 1), 0)),
                pl.BlockSpec((tr, c), lambda i, pf: (jnp.clip(i - (1 - pf[0]) * nch, 0, nch - 1), 0)),
                slab, slab, slab] + [ANY] * n_alias
    o = _sds(w.shape, F32)
    return pl.pallas_call(
        body, name=name, out_shape=[o] * 4, grid_spec=_grid_spec((2 * nch,), in_specs, [slab] * 4),
        input_output_aliases={6 + k: k for k in range(n_alias)},
        compiler_params=_cp(("parallel",)))(pf, *ins)


def _adamw_t(name, gs, w_t, m_t, v_t, pf):
    c, nl, r = w_t.shape
    ta = LANE
    nch = (r // 2) // ta

    def body(*refs):
        pf_ref = refs[0]
        g_refs = refs[1:1 + 2 * nl]
        w_ref, m_ref, v_ref, go_ref, d_ref, mo_ref, vo_ref = refs[1 + 2 * nl:]
        mine = (pl.program_id(0) // nch) == pf_ref[0]
        gv = jnp.stack([jnp.where(mine, g_refs[2 * l][...], g_refs[2 * l + 1][...]) for l in range(nl)], axis=1)
        delta, m2, v2 = _adam_math(w_ref[...], gv, m_ref[...], v_ref[...])
        go_ref[...] = gv
        d_ref[...] = delta
        mo_ref[...] = m2
        vo_ref[...] = v2

    both = pl.BlockSpec((c, nl, ta), lambda i, pf: (0, 0, i))
    g_specs = []
    for l in range(nl):
        g_specs += [pl.BlockSpec((c, ta), lambda i, pf: (0, jnp.clip(i - pf[0] * nch, 0, nch - 1))),
                    pl.BlockSpec((c, ta), lambda i, pf: (0, jnp.clip(i - (1 - pf[0]) * nch, 0, nch - 1)))]
    o = _sds(w_t.shape, F32)
    flat_g = [a for pair in gs for a in pair]
    return pl.pallas_call(
        body, name=name, out_shape=[o] * 4, grid_spec=_grid_spec((2 * nch,), g_specs + [both] * 3, [both] * 4),
        compiler_params=_cp(("parallel",)))(pf, *flat_g, w_t, m_t, v_t)


def _adamw_flat(g, w, m, v):
    def body(g_ref, w_ref, m_ref, v_ref, d_ref, mo_ref, vo_ref):
        delta, m2, v2 = _adam_math(w_ref[...], g_ref[...], m_ref[...], v_ref[...])
        d_ref[...] = delta
        mo_ref[...] = m2
        vo_ref[...] = v2

    o = _sds(w.shape, F32)
    return pl.pallas_call(body, name="adamw_small", out_shape=[o, o, o])(g, w, m, v)


def _place():
    x, y, c = lax.axis_index("x"), lax.axis_index("y"), lax.axis_index("c")
    chips = [(1 - x, y), (x, 1 - y), (1 - x, 1 - y)]
    return x, y, c, chips


HBM = pl.BlockSpec(memory_space=pltpu.HBM)
SEM = pl.BlockSpec(memory_space=pltpu.SEMAPHORE)
EFFECT = pltpu.SideEffectType.DATAFLOW_SIDE_EFFECTING


def _in_hbm(a):
    return pltpu.with_memory_space_constraint(a, pltpu.HBM)


def _rs_chips_copies(ins, lands, send, recv):
    x, y, c, chips = _place()
    cps = []
    for t in range(len(ins)):
        for k, chip in enumerate(chips):
            jk = 2 * chip[0] + chip[1]
            cps.append(pltpu.make_async_remote_copy(
                src_ref=ins[t].at[jk], dst_ref=lands[t].at[k], send_sem=send.at[3 * t + k],
                recv_sem=recv.at[3 * t + k], device_id=(*chip, c), device_id_type=MESH))
    return cps


def _rs_chips_start(name, ps):
    n = len(ps)

    def body(*refs):
        ins, lands = refs[:n], refs[n:2 * n]
        send, recv = refs[2 * n], refs[2 * n + 1]
        token = refs[-1]
        for cp in _rs_chips_copies(ins, lands, send, recv):
            cp.start()
        token[...] = jnp.zeros_like(token)

    dma = pltpu.SemaphoreType.DMA
    lands = [lax.empty((3,) + p.shape[1:], BF16) for p in ps]
    out_shape = ([dma((3 * n,)), dma((3 * n,))] + [pltpu.HBM(p.shape, BF16) for p in ps]
                 + [pltpu.HBM(z.shape, BF16) for z in lands] + [_sds((8, LANE), F32)])
    outs = pl.pallas_call(
        body, name=name, out_shape=out_shape, in_specs=[HBM] * (2 * n),
        out_specs=[SEM, SEM] + [HBM] * (2 * n) + [pl.BlockSpec(memory_space=pltpu.VMEM)],
        input_output_aliases={i: 2 + i for i in range(2 * n)},
        compiler_params=pltpu.CompilerParams(has_side_effects=EFFECT))(
            *[_in_hbm(p) for p in ps], *[_in_hbm(z) for z in lands])
    return outs[0], outs[1], outs[2:2 + n], outs[2 + n:2 + 2 * n], outs[-1]


def _rs_chips_wait(name, send, recv, ps, lands, afters):
    n = len(ps)

    def body(*refs):
        ins, zones = refs[:n], refs[n:2 * n]
        send_ref, recv_ref = refs[2 * n], refs[2 * n + 1]
        for cp in _rs_chips_copies(ins, zones, send_ref, recv_ref):
            cp.wait_send()
            cp.wait_recv()

    outs = pl.pallas_call(
        body, name=name, out_shape=[pltpu.HBM(p.shape, BF16) for p in ps] + [pltpu.HBM(z.shape, BF16) for z in lands],
        in_specs=[HBM] * (2 * n) + [SEM, SEM] + [ANY] * len(afters), out_specs=[HBM] * (2 * n),
        input_output_aliases={i: i for i in range(2 * n)},
        compiler_params=pltpu.CompilerParams(has_side_effects=EFFECT))(*ps, *lands, send, recv, *afters)
    return outs[:n], outs[n:]


def _sibling():
    x, y, c, _ = _place()
    return (x, y, 1 - c)


def _pair_send(name, dw, pf, by_cols=False):
    n4, r, c = dw.shape
    blk = (1, r, c // 2) if by_cols else (1, r // 2, c)
    idx = (lambda s, pf: (s, 0, 1 - pf[0])) if by_cols else (lambda s, pf: (s, 1 - pf[0], 0))

    def body(pf_ref, x_ref, got_ref, ssem, rsem):
        s = pl.program_id(0)
        cp = pltpu.make_async_remote_copy(src_ref=x_ref, dst_ref=got_ref.at[pl.ds(s, 1)], send_sem=ssem,
                                          recv_sem=rsem, device_id=_sibling(), device_id_type=MESH)
        cp.start()
        cp.wait_send()

        @pl.when(s == n4 - 1)
        def _():
            pltpu.make_async_remote_copy(src_ref=got_ref, dst_ref=got_ref, send_sem=ssem, recv_sem=rsem,
                                         device_id=_sibling(), device_id_type=MESH).wait_recv()

    dma = pltpu.SemaphoreType.DMA
    return pl.pallas_call(
        body, name=name, out_shape=_sds((n4,) + blk[1:], BF16),
        grid_spec=_grid_spec((n4,), [pl.BlockSpec(blk, idx)], ANY, scratch=[dma(()), dma(())]),
        compiler_params=pltpu.CompilerParams(dimension_semantics=("arbitrary",), has_side_effects=True,
                                             vmem_limit_bytes=VMEM_LIMIT))(pf, dw)


def _dw_pair(name, x, dy, down, dep=None):
    lp = x.shape[0]
    tk = DFF // NCHIP
    if down:
        grid = (NCHIP, 2)
        x_spec = pl.BlockSpec((lp, tk), lambda s, j: (0, s))
        dy_spec = pl.BlockSpec((lp, 1024), lambda s, j: (0, j))
        o_spec = pl.BlockSpec((1, tk, 1024), lambda s, j: (s, 0, j))
        dw_shape, got_shape = (NCHIP, tk, D), (NCHIP, tk // 2, D)
    else:
        grid = (2, NCHIP)
        x_spec = pl.BlockSpec((lp, 1024), lambda i, s: (0, i))
        dy_spec = pl.BlockSpec((lp, tk), lambda i, s: (0, s))
        o_spec = pl.BlockSpec((1, 1024, tk), lambda i, s: (s, i, 0))
        dw_shape, got_shape = (NCHIP, D, tk), (NCHIP, D // 2, tk)

    send_shape = (tk // 2, 1024) if down else (1024, tk)
    n_sends = 2 * NCHIP if down else NCHIP

    deps = [] if dep is None else [dep]

    def body(x_ref, dy_ref, *rest):
        o_ref, got_ref, sbuf, ssem, rsem = rest[len(deps):]
        g0, g1 = pl.program_id(0), pl.program_id(1)
        tile = lax.dot_general(x_ref[...], dy_ref[...], _DN["tn"], preferred_element_type=F32).astype(BF16)
        o_ref[0] = tile
        _, _, c, _ = _place()

        def send(n, part, dst):
            for s in (0, 1):
                cp = pltpu.make_async_remote_copy(src_ref=sbuf.at[pl.ds(s, 1)], dst_ref=dst, send_sem=ssem.at[s],
                                                  recv_sem=rsem, device_id=_sibling(), device_id_type=MESH)

                @pl.when(n % 2 == s)
                def _():
                    pl.when(n >= 2)(cp.wait_send)
                    sbuf[s] = part()
                    cp.start()

            @pl.when(n == n_sends - 1)
            def _():
                for s in (0, 1):
                    pltpu.make_async_remote_copy(src_ref=sbuf.at[pl.ds(s, 1)], dst_ref=dst, send_sem=ssem.at[s],
                                                 recv_sem=rsem, device_id=_sibling(), device_id_type=MESH).wait_send()

        if down:
            hr = tk // 2
            rows = pl.ds(pl.multiple_of((1 - c) * hr, 16), hr)
            send(g0 * 2 + g1, lambda: o_ref[0, rows, :],
                 got_ref.at[pl.ds(g0, 1), :, pl.ds(pl.multiple_of(g1 * 1024, LANE), 1024)])
        else:
            pl.when(g0 == 1 - c)(lambda: send(g1, lambda: tile, got_ref.at[pl.ds(g1, 1)]))

        @pl.when(jnp.logical_and(g0 == grid[0] - 1, g1 == grid[1] - 1))
        def _():
            pltpu.make_async_remote_copy(src_ref=got_ref, dst_ref=got_ref, send_sem=ssem.at[0], recv_sem=rsem,
                                         device_id=_sibling(), device_id_type=MESH).wait_recv()

    dma = pltpu.SemaphoreType.DMA
    return pl.pallas_call(
        body, name=name, grid=grid, in_specs=[x_spec, dy_spec] + [ANY] * len(deps), out_specs=[o_spec, ANY],
        out_shape=[_sds(dw_shape, BF16), _sds(got_shape, BF16)],
        scratch_shapes=[pltpu.VMEM((2,) + send_shape, BF16), dma((2,)), dma(())],
        compiler_params=pltpu.CompilerParams(dimension_semantics=("arbitrary", "arbitrary"), has_side_effects=True,
                                             vmem_limit_bytes=VMEM_LIMIT))(x, dy, *deps)


def _add4_join(name, p, got, pf, by_cols=False):
    n4, r2, c = p.shape
    if by_cols:
        tr, nch = r2, c // TCOL
        blk, idx = (r2, TCOL), (lambda i: (0, i))
    else:
        tr = _row_tile(r2, c, 4)
        nch = r2 // tr
        blk, idx = (tr, c), (lambda i: (i, 0))

    def body(pf_ref, p_ref, g_ref, mine_ref, theirs_ref, sbuf, ssem, rsem):
        i = pl.program_id(0)
        s = p_ref[...].astype(F32)
        for k in range(3):
            s = s + g_ref[k].astype(F32)
        mine_ref[...] = s
        if by_cols:
            dst = theirs_ref.at[:, pl.ds(pl.multiple_of(i * TCOL, LANE), TCOL)]
        else:
            dst = theirs_ref.at[pl.ds(pl.multiple_of(i * tr, 8), tr), :]

        def copy(slot):
            return pltpu.make_async_remote_copy(src_ref=sbuf.at[slot], dst_ref=dst, send_sem=ssem.at[slot],
                                                recv_sem=rsem, device_id=_sibling(), device_id_type=MESH)

        for slot in (0, 1):
            @pl.when(i % 2 == slot)
            def _():
                pl.when(i >= 2)(copy(slot).wait_send)
                sbuf[slot] = s
                copy(slot).start()

        @pl.when(i == nch - 1)
        def _():
            for slot in range(min(nch, 2)):
                copy(slot).wait_send()
            pltpu.make_async_remote_copy(src_ref=theirs_ref, dst_ref=theirs_ref, send_sem=ssem.at[0], recv_sem=rsem,
                                         device_id=_sibling(), device_id_type=MESH).wait_recv()

    dma = pltpu.SemaphoreType.DMA
    o = _sds((r2, c), F32)
    return pl.pallas_call(
        body, name=name, out_shape=[o, o],
        grid_spec=_grid_spec((nch,), [pl.BlockSpec((None,) + blk, lambda i, pf: (pf[1],) + idx(i)),
                                      pl.BlockSpec((3,) + blk, lambda i, pf: (0,) + idx(i))],
                             [pl.BlockSpec(blk, lambda i, pf: idx(i)), ANY],
                             scratch=[pltpu.VMEM((2,) + blk, F32), dma((2,)), dma(())]),
        compiler_params=pltpu.CompilerParams(dimension_semantics=("arbitrary",), has_side_effects=True,
                                             vmem_limit_bytes=VMEM_LIMIT))(pf, p, got)


def _half_of(g, slot, which, axis):
    half = g.shape[axis] // 2
    if axis == 1:
        return g.at[slot, pl.ds(which * half, half), :]
    return g.at[slot, :, pl.ds(which * half, half)]


def _ag_copies(arrs, split, send, recv):
    x, y, c, chips = _place()
    j = 2 * x + y
    cps = []
    for t, g in enumerate(arrs):
        piece = _half_of(g, j, c, split[t]) if split[t] else g.at[j]
        for k, chip in enumerate(chips):
            cps.append(pltpu.make_async_remote_copy(
                src_ref=piece, dst_ref=piece, send_sem=send.at[3 * t + k], recv_sem=recv.at[3 * t + k],
                device_id=(*chip, c), device_id_type=MESH))
    return cps


def _ag_start(name, groups, splits):
    sizes = [len(g) for g in groups]
    flat = [a for g in groups for a in g]
    n = len(flat)

    def body(*refs):
        ins = refs[:n]
        sems = refs[n:n + 2 * len(groups)]
        o = 0
        for gi, sz in enumerate(sizes):
            for cp in _ag_copies(ins[o:o + sz], splits[gi], sems[2 * gi], sems[2 * gi + 1]):
                cp.start()
            o += sz
        refs[-1][...] = jnp.zeros_like(refs[-1])

    dma = pltpu.SemaphoreType.DMA
    sem_shapes = [dma((3 * sz,)) for sz in sizes for _ in range(2)]
    outs = pl.pallas_call(
        body, name=name, out_shape=sem_shapes + [pltpu.HBM(a.shape, a.dtype) for a in flat] + [_sds((8, LANE), F32)],
        in_specs=[HBM] * n,
        out_specs=[SEM] * len(sem_shapes) + [HBM] * n + [pl.BlockSpec(memory_space=pltpu.VMEM)],
        input_output_aliases={i: len(sem_shapes) + i for i in range(n)},
        compiler_params=pltpu.CompilerParams(has_side_effects=EFFECT))(*[_in_hbm(a) for a in flat])
    sems, arrs, o = [], [], len(sem_shapes)
    for gi, sz in enumerate(sizes):
        sems.append((outs[2 * gi], outs[2 * gi + 1]))
        arrs.append(list(outs[o:o + sz]))
        o += sz
    return sems, arrs, outs[-1]


def _ag_wait(name, arrs, split, send, recv, afters):
    n = len(arrs)

    def body(*refs):
        for cp in _ag_copies(refs[:n], split, refs[n], refs[n + 1]):
            cp.wait_send()
            cp.wait_recv()

    return pl.pallas_call(
        body, name=name, out_shape=[pltpu.HBM(a.shape, a.dtype) for a in arrs],
        in_specs=[HBM] * n + [SEM, SEM] + [ANY] * len(afters), out_specs=[HBM] * n,
        input_output_aliases={i: i for i in range(n)},
        compiler_params=pltpu.CompilerParams(has_side_effects=EFFECT))(*arrs, send, recv, *afters)


def _ag_forward(name, arrs, axes):
    n = len(arrs)

    def half_shape(t):
        _, r, cc = arrs[t].shape
        return (r // 2, cc) if axes[t] == 1 else (r, cc // 2)

    def body(*refs):
        g = refs[n:2 * n]
        bufs = refs[2 * n:3 * n]
        fsend, frecv, lsem = refs[3 * n:]
        x, y, c, chips = _place()
        for t in range(n):
            pend = [None, None]
            for k, chip in enumerate(chips):
                jk = 2 * chip[0] + chip[1]
                slot = k % 2
                if pend[slot] is not None:
                    pend[slot].wait_send()
                part = _half_of(g[t], jk, c, axes[t])
                ld = pltpu.make_async_copy(part, bufs[t].at[slot], lsem.at[2 * t + slot])
                ld.start()
                ld.wait()
                cp = pltpu.make_async_remote_copy(
                    src_ref=bufs[t].at[slot], dst_ref=part, send_sem=fsend.at[2 * t + slot],
                    recv_sem=frecv.at[t], device_id=(x, y, 1 - c), device_id_type=MESH)
                cp.start()
                pend[slot] = cp
            for cp in pend:
                cp.wait_send()
        for t in range(n):
            hr, hc = half_shape(t)
            passed = g[t].at[pl.ds(0, 3), pl.ds(0, hr), pl.ds(0, hc)]
            pltpu.make_async_remote_copy(
                src_ref=passed, dst_ref=passed, send_sem=fsend.at[2 * t], recv_sem=frecv.at[t],
                device_id=(x, y, 1 - c), device_id_type=MESH).wait_recv()

    dma = pltpu.SemaphoreType.DMA
    scratch = [pltpu.VMEM((2,) + half_shape(t), BF16) for t in range(n)]
    scratch += [dma((2 * n,)), dma((n,)), dma((2 * n,))]
    return pl.pallas_call(
        body, name=name, in_specs=[ANY] * n, out_specs=[ANY] * n, out_shape=[_sds(a.shape, a.dtype) for a in arrs],
        scratch_shapes=scratch, input_output_aliases={t: t for t in range(n)},
        compiler_params=pltpu.CompilerParams(has_side_effects=True, vmem_limit_bytes=VMEM_LIMIT))(*arrs)


def _rs_begin(tag, dws, by_cols, pf):
    ps = []
    for t, dw in enumerate(dws):
        if isinstance(dw, (list, tuple)):
            dw, got = dw
        else:
            got = _pair_send("rs_pair_%s_%d" % (tag, t), dw, pf, by_cols[t])
        ps.append(_add2_bf16("rs_add2_%s_%d" % (tag, t), dw, got, pf, by_cols[t]))
    send, recv, ps_thru, lands, token = _rs_chips_start("rs_chips_start_" + tag, ps)
    return (tag, send, recv, ps_thru, lands, by_cols), token


def _rs_end(handle, afters, pf):
    tag, send, recv, ps, lands, by_cols = handle
    ps, got2 = _rs_chips_wait("rs_chips_wait_" + tag, send, recv, ps, lands, afters)
    return [_add4_join("rs_add4_%s_%d" % (tag, t), p, g2, pf, by_cols[t]) for t, (p, g2) in enumerate(zip(ps, got2))]


def _allreduce_small(pack):
    r = pack.shape[0]
    flips = [(fx, fy, fc) for fx in (0, 1) for fy in (0, 1) for fc in (0, 1)][1:]

    def body(p_ref, o_ref, gat, send, recv):
        x, y, c, _ = _place()
        me = 4 * x + 2 * y + c
        gat[me] = p_ref[...]
        cps = []
        for k, (fx, fy, fc) in enumerate(flips):
            peer = ((1 - x) if fx else x, (1 - y) if fy else y, (1 - c) if fc else c)
            cp = pltpu.make_async_remote_copy(
                src_ref=p_ref, dst_ref=gat.at[me], send_sem=send.at[k], recv_sem=recv.at[k],
                device_id=peer, device_id_type=MESH)
            cp.start()
            cps.append(cp)
        for k, (fx, fy, fc) in enumerate(flips):
            peer = ((1 - x) if fx else x, (1 - y) if fy else y, (1 - c) if fc else c)
            src = 4 * peer[0] + 2 * peer[1] + peer[2]
            pltpu.make_async_remote_copy(
                src_ref=p_ref, dst_ref=gat.at[src], send_sem=send.at[k], recv_sem=recv.at[k],
                device_id=peer, device_id_type=MESH).wait_recv()
        for cp in cps:
            cp.wait_send()
        s = gat[0]
        for d in range(1, 8):
            s = s + gat[d]
        o_ref[...] = s

    dma = pltpu.SemaphoreType.DMA
    vm = pl.BlockSpec(memory_space=pltpu.VMEM)
    return pl.pallas_call(
        body, name="allreduce_small", in_specs=[vm], out_specs=vm, out_shape=_sds((r, LANE), F32),
        scratch_shapes=[pltpu.VMEM((8, r, LANE), F32), dma((7,)), dma((7,))],
        compiler_params=pltpu.CompilerParams(has_side_effects=True))(pack)


def _in_weights(win_g):
    full = jnp.concatenate([win_g[s] for s in range(NCHIP)], axis=0)
    wqkv = full[:2048]
    og = full[2048:3072]
    gates = jnp.pad(full[3072:3080], ((0, LANE - 8), (0, 0)))
    u = full[3080:4104]
    gb = full[4104:5128]
    gc = full[5128:6152]
    return wqkv, jnp.stack([og, u, gb, gc]), gates


def _in_grads(dwqkv, dwe, dwgt):
    full = jnp.concatenate([dwqkv, dwe[0], dwgt[:8], dwe[1], dwe[2], dwe[3]], axis=0)
    sw = DIN // NCHIP
    return jnp.stack([full[s * sw:(s + 1) * sw] for s in range(NCHIP)])


def _layer_fwd(l, h, get, small):
    lp = h.shape[0]
    th = lp // 2
    wqkv, we, wgt = _in_weights(get("in", h)[0])
    nmw, bias, mnw, cw, nfw = small
    tag = "_l%d" % l
    hn = _norm_fwd("norm_mix" + tag, h, nmw)
    qkv = _mm("proj_qkv" + tag, "nt", hn, wqkv,
              pl.BlockSpec((lp, D), lambda i, j, k: (0, 0)), pl.BlockSpec((512, D), lambda i, j, k: (j, 0)),
              pl.BlockSpec((lp, 512), lambda i, j, k: (0, j)), _sds((lp, 2048), BF16), (1, 4, 1))
    e = _mm("proj_e" + tag, "nt", hn, we,
            pl.BlockSpec((lp, D), lambda i, j, k: (0, 0)), pl.BlockSpec((None, 512, D), lambda i, j, k: (j // 2, j % 2, 0)),
            pl.BlockSpec((None, lp, 512), lambda i, j, k: (j // 2, 0, j % 2)), _sds((4, lp, 1024), BF16), (1, 8, 1))
    gpre = _mm("proj_gates" + tag, "nt", hn, wgt,
               pl.BlockSpec((lp, D), lambda i, j, k: (0, 0)), pl.BlockSpec((LANE, D), lambda i, j, k: (0, 0)),
               pl.BlockSpec((lp, LANE), lambda i, j, k: (0, 0)), _sds((lp, LANE), F32), (1, 1, 1))
    grow, mcol, ccol = _gate_prep("gate_prep" + tag, gpre, bias)
    ht = _mlstm_fwd("mlstm_fwd" + tag, qkv, grow, mcol, ccol, 4)
    mix = _mix_fwd("mix_fwd" + tag, ht, e, mnw, cw)
    wout_g = get("out", mix)[0]
    wout = wout_g.reshape(D, D)
    h1 = _mm("out_proj" + tag, "nn", mix, wout,
             pl.BlockSpec((None, th, 1024), lambda i, j, k: (k, i, 0)),
             pl.BlockSpec((1024, 1024), lambda i, j, k: (k, j)),
             pl.BlockSpec((th, 1024), lambda i, j, k: (i, j)), _sds((lp, D), F32), (2, 2, 2),
             acc_shape=(th, 1024), res=h, res_spec=pl.BlockSpec((th, 1024), lambda i, j, k: (i, j)))
    wg_g, wu_g = get("ffn", h1)
    hf = _norm_fwd("norm_ffn" + tag, h1, nfw)
    g, u, a = _ffn_up("ffn_up" + tag, hf, wg_g, wu_g)
    wd_g = get("down", a)[0]
    tk = DFF // NCHIP
    h2 = _mm("ffn_down" + tag, "nn", a, wd_g,
             pl.BlockSpec((th, tk), lambda i, j, k: (i, k)),
             pl.BlockSpec((None, tk, 1024), lambda i, j, k: (k, 0, j)),
             pl.BlockSpec((th, 1024), lambda i, j, k: (i, j)), _sds((lp, D), F32), (2, 2, NCHIP),
             acc_shape=(th, 1024), res=h1, res_spec=pl.BlockSpec((th, 1024), lambda i, j, k: (i, j)))
    saved = (h, hn, qkv, e, gpre, grow, mcol, ccol, ht, mix, h1, hf, g, u, a)
    return h2, saved, (wqkv, we, wgt, wout_g, wg_g, wu_g, wd_g)


def _layer_bwd(l, dh2, dh2b, saved, wts, small, ffn_done, dep):
    h, hn, qkv, e, gpre, grow, mcol, ccol, ht, mix, h1, hf, g, u, a = saved
    wqkv, we, wgt, wout_g, wg_g, wu_g, wd_g = wts
    nmw, bias, mnw, cw, nfw = small
    lp = h.shape[0]
    th = lp // 2
    tk = DFF // NCHIP
    tag = "_l%d" % l
    half_rows = lambda i, j, k: (i, j)

    dwd = _dw_pair("dw_down" + tag, a, dh2b, True, dep)
    dg, du = _ffn_bwd_act("ffn_bwd_act" + tag, dh2b, wd_g, g, u, dep)
    dwg = _dw_pair("dw_gate" + tag, hf, dg, False)
    dwu = _dw_pair("dw_up" + tag, hf, du, False)
    dhf = _ffn_bwd_in("dhf" + tag, dg, du, wg_g, wu_g)
    dh1, dh1b, dnfw = _norm_bwd("norm_ffn_bwd" + tag, dhf, h1, nfw, dh2)
    ffn_started = ffn_done([dwg, dwu, dwd])

    dwout = _mm("dw_out" + tag, "tn", mix, dh1b,
                pl.BlockSpec((None, lp, 1024), lambda i, j, k: (i, 0, 0)), pl.BlockSpec((lp, 1024), lambda i, j, k: (0, j)),
                pl.BlockSpec((1024, 1024), half_rows), _sds((D, D), BF16), (2, 2, 1), dep=ffn_started)
    wout = wout_g.reshape(D, D)
    dmix = _mm("dmix" + tag, "nt", dh1b, wout,
               pl.BlockSpec((th, D), lambda i, j, k: (i, 0)), pl.BlockSpec((1024, D), lambda i, j, k: (j, 0)),
               pl.BlockSpec((th, 1024), half_rows), _sds((lp, D), F32), (2, 2, 1), dep=ffn_started)
    dht, de, dmnw, dcw = _mix_bwd("mix_bwd" + tag, dmix, ht, e, mnw, cw)
    dq, dk, dv, dgrow, dfx = _mlstm_bwd("mlstm_bwd" + tag, qkv, grow, mcol, ccol, ht, dht, 8)
    dgp, dgpb, dbias = _gate_bwd("gate_bwd" + tag, gpre, bias, dgrow, dfx)
    del dgp
    dqkv = jnp.concatenate([dq, dk, dv], axis=1)

    hn_cols = pl.BlockSpec((lp, 1024), lambda i, j, k: (0, j))
    dwqkv = _mm("dw_qkv" + tag, "tn", dqkv, hn, pl.BlockSpec((lp, 1024), lambda i, j, k: (0, i)), hn_cols,
                pl.BlockSpec((1024, 1024), half_rows), _sds((2048, D), BF16), (2, 2, 1))
    dwe = _mm("dw_e" + tag, "tn", de, hn, pl.BlockSpec((None, lp, 1024), lambda i, j, k: (i, 0, 0)), hn_cols,
              pl.BlockSpec((None, 1024, 1024), lambda i, j, k: (i, 0, j)), _sds((4, 1024, D), BF16), (4, 2, 1))
    dwgt = _mm("dw_gates" + tag, "tn", dgpb, hn, pl.BlockSpec((lp, LANE), lambda i, j, k: (0, 0)), hn_cols,
               pl.BlockSpec((LANE, 1024), lambda i, j, k: (0, j)), _sds((LANE, D), BF16), (1, 2, 1))
    dwin = _in_grads(dwqkv, dwe, dwgt)

    dhn = _mm("dhn_qkv" + tag, "nn", dqkv, wqkv,
              pl.BlockSpec((th, 2048), lambda i, j, k: (i, 0)), pl.BlockSpec((2048, 1024), lambda i, j, k: (0, j)),
              pl.BlockSpec((th, 1024), half_rows), _sds((lp, D), F32), (2, 2, 1))
    dhn = _mm("dhn_e" + tag, "nn", de, we,
              pl.BlockSpec((None, th, 1024), lambda i, j, k: (k, i, 0)),
              pl.BlockSpec((None, 1024, 1024), lambda i, j, k: (k, 0, j)),
              pl.BlockSpec((th, 1024), half_rows), _sds((lp, D), F32), (2, 2, 4), acc_shape=(th, 1024),
              res=dhn, res_spec=pl.BlockSpec((th, 1024), half_rows))
    dhn = _mm("dhn_gates" + tag, "nn", dgpb, wgt,
              pl.BlockSpec((th, LANE), lambda i, j, k: (i, 0)), pl.BlockSpec((LANE, 1024), lambda i, j, k: (0, j)),
              pl.BlockSpec((th, 1024), half_rows), _sds((lp, D), F32), (2, 2, 1),
              res=dhn, res_spec=pl.BlockSpec((th, 1024), half_rows))
    dh0, dh0b, dnmw = _norm_bwd("norm_mix_bwd" + tag, dhn, h, nmw, dh1)

    pieces = [dwin, dwout.reshape(NCHIP, D // NCHIP, D)]
    smalls = (dnmw[0], dbias[0, :8], dcw[:3], dmnw[0], dnfw[0])
    return dh0, dh0b, pieces, smalls


def _pack_rows(parts):
    rows = []
    for p in parts:
        f = p.reshape(-1)
        pad = (-f.shape[0]) % LANE
        if pad:
            f = jnp.pad(f, (0, pad))
        rows.append(f.reshape(-1, LANE))
    r = jnp.concatenate(rows, axis=0)
    pad = (-r.shape[0]) % 8
    if pad:
        r = jnp.pad(r, ((0, pad), (0, 0)))
    return r


def _unpack_rows(pack, shapes):
    out, r0 = [], 0
    for s in shapes:
        n = 1
        for d in s:
            n *= d
        nr = -(-n // LANE)
        out.append(pack[r0:r0 + nr].reshape(-1)[:n].reshape(s))
        r0 += nr
    return out


def kernel(x, meta_tokens, norm_mix_w, w_in, b_gates, conv_w, mlstm_norm_w, w_out, norm_ffn_w, w_gate, w_up, w_down, norm_final_w, loss_target, m_meta_tokens, m_norm_mix_w, m_w_in, m_b_gates, m_conv_w, m_mlstm_norm_w, m_w_out, m_norm_ffn_w, m_w_gate, m_w_up, m_w_down, m_norm_final_w, v_meta_tokens, v_norm_mix_w, v_w_in, v_b_gates, v_conv_w, v_mlstm_norm_w, v_w_out, v_norm_ffn_w, v_w_gate, v_w_up, v_w_down, v_norm_final_w):
    seq = x.shape[1]
    n_real = N_META + seq
    lp = -(-n_real // LANE) * LANE
    xi, yi, ci = lax.axis_index("x"), lax.axis_index("y"), lax.axis_index("c")
    jchip = 2 * xi + yi
    pf = jnp.stack([ci, jchip, 2 * (1 - xi) + yi, 2 * xi + (1 - yi), 2 * (1 - xi) + (1 - yi)]).astype(jnp.int32)

    big = {"w_in": w_in, "w_out": w_out, "w_gate": w_gate, "w_up": w_up, "w_down": w_down}
    cast = {n: [_cast_into("cast_%s_l%d" % (n, l), w, l, pf) for l in range(DEPTH)] for n, w in big.items()
            if n != "w_in"}
    in_t = lambda a: jnp.transpose(a, (2, 0, 1))
    cast["w_in"] = [_cast_into_t("cast_w_in_l%d" % l, in_t(w_in), l, pf) for l in range(DEPTH)]
    conv_flat = jnp.pad(conv_w.reshape(DEPTH * 3, CW // NCHIP), ((0, 8 - DEPTH * 3), (0, 0)))

    def own_slot(a):
        return lax.dynamic_update_slice(jnp.zeros((NCHIP,) + a.shape, a.dtype), a[None], (jchip, 0, 0))

    groups, splits = [], []
    for l in range(DEPTH):
        groups.append([cast["w_in"][l]] + ([own_slot(meta_tokens), own_slot(conv_flat)] if l == 0 else []))
        splits.append([2] + ([0, 0] if l == 0 else []))
        groups += [[cast["w_out"][l]], [cast["w_gate"][l], cast["w_up"][l]], [cast["w_down"][l]]]
        splits += [[1], [1, 1], [1]]
    group_of = {"in": 0, "out": 1, "ffn": 2, "down": 3}
    sems, arrs, _ = _ag_start("ag_start_0", groups[:1], splits[:1])
    sems_rest, arrs_rest, all_started = _ag_start("ag_start_1", groups[1:], splits[1:])
    sems, arrs = sems + sems_rest, arrs + arrs_rest

    def gathered(gi, afters):
        got = _ag_wait("ag_wait_%d" % gi, arrs[gi], splits[gi], sems[gi][0], sems[gi][1], afters)
        axes = [s for s in splits[gi] if s]
        return list(_ag_forward("ag_forward_%d" % gi, got[:len(axes)], axes)) + list(got[len(axes):])

    win0_g, meta_g, conv_g = gathered(0, [all_started])
    meta_full = jnp.concatenate([meta_g[s] for s in range(NCHIP)], axis=1)
    conv_full = jnp.concatenate([conv_g[s][:DEPTH * 3] for s in range(NCHIP)], axis=1)
    conv_full = conv_full.reshape(DEPTH, 3, CW)

    bias_rows = jnp.pad(b_gates, ((0, 0), (0, LANE - 8)))
    smalls = []
    for l in range(DEPTH):
        smalls.append((norm_mix_w[l][None], bias_rows[l][None], mlstm_norm_w[l][None],
                       jnp.pad(conv_full[l], ((0, 5), (0, 0))), norm_ffn_w[l][None]))

    h = jnp.concatenate([meta_full, x[0], jnp.zeros((lp - n_real, D), F32)], axis=0)
    saved, wts = [], []
    for l in range(DEPTH):
        def get(which, after, l=l):
            if l == 0 and which == "in":
                return [win0_g]
            return gathered(4 * l + group_of[which], [after])

        h, sv, wt = _layer_fwd(l, h, get, smalls[l])
        saved.append(sv)
        wts.append(wt)
    tgt = jnp.pad(loss_target[0], ((N_META, lp - n_real), (0, 0)))
    dh, dhb, dnorm_final, loss_part = _loss_head(h, tgt, norm_final_w[None], n_real)

    names = ["w_in", "w_out", "w_gate", "w_up", "w_down"]
    params = {"w_in": (w_in, m_w_in, v_w_in), "w_out": (w_out, m_w_out, v_w_out), "w_gate": (w_gate, m_w_gate, v_w_gate),
              "w_up": (w_up, m_w_up, v_w_up), "w_down": (w_down, m_w_down, v_w_down)}
    big_out = {n: None for n in names}
    small_grads = [None] * DEPTH
    g_in = [None] * DEPTH

    def finish(l, group, handle, afters):
        for n, (g_mine, g_theirs) in zip(group, _rs_end(handle, afters, pf)):
            if n == "w_in":
                g_in[l] = (g_mine, g_theirs)
                continue
            w, m, v = params[n]
            big_out[n] = _adamw_layer("adamw_%s_l%d" % (n, l), l, g_mine, g_theirs, w, m, v, big_out[n], pf)
        return [big_out[n][3] for n in group if n != "w_in"]

    groups = []
    token = None
    for l in reversed(range(DEPTH)):
        def ffn_done(pieces, l=l):
            handle, tok = _rs_begin("l%df" % l, pieces, [False] * 3, pf)
            groups.append((l, names[2:], handle))
            return tok

        dh, dhb, pieces, small_grads[l] = _layer_bwd(l, dh, dhb, saved[l], wts[l], smalls[l], ffn_done, token)
        handle, token = _rs_begin("l%dm" % l, pieces, [True, False], pf)
        groups.append((l, names[:2], handle))
    afters = [token]
    for l, group, handle in groups:
        afters = finish(l, group, handle, afters)
    big_out["w_in"] = [jnp.transpose(a, (1, 2, 0))
                       for a in _adamw_t("adamw_w_in", g_in, in_t(w_in), in_t(m_w_in), in_t(v_w_in), pf)]

    dnmw = jnp.stack([small_grads[l][0] for l in range(DEPTH)])
    dbias = jnp.stack([small_grads[l][1] for l in range(DEPTH)])
    dconv = jnp.stack([small_grads[l][2] for l in range(DEPTH)])
    dmnw = jnp.stack([small_grads[l][3] for l in range(DEPTH)])
    dnfw = jnp.stack([small_grads[l][4] for l in range(DEPTH)])
    part_shapes = [(N_META, D), (DEPTH, D), (DEPTH, 8), (DEPTH, 3, CW), (DEPTH, MW), (DEPTH, D), (D,), (LANE,)]
    pack = _pack_rows([dh[:N_META], dnmw, dbias, dconv, dmnw, dnfw, dnorm_final[0], loss_part[0]])
    tot = _unpack_rows(_allreduce_small(pack), part_shapes)
    g_meta_full, g_nmw, g_bias, g_conv_full, g_mnw, g_nfw, g_final, loss_row = tot
    mcols = D // NCHIP
    ccols = CW // NCHIP
    g_meta = lax.dynamic_slice_in_dim(g_meta_full, jchip * mcols, mcols, axis=1)
    g_conv = lax.dynamic_slice_in_dim(g_conv_full, jchip * ccols, ccols, axis=2)
    sm_g = [g_meta, g_nmw, g_bias, g_conv, g_mnw, g_nfw, g_final]
    sm_w = [meta_tokens, norm_mix_w, b_gates, conv_w, mlstm_norm_w, norm_ffn_w, norm_final_w]
    sm_m = [m_meta_tokens, m_norm_mix_w, m_b_gates, m_conv_w, m_mlstm_norm_w, m_norm_ffn_w, m_norm_final_w]
    sm_v = [v_meta_tokens, v_norm_mix_w, v_b_gates, v_conv_w, v_mlstm_norm_w, v_norm_ffn_w, v_norm_final_w]
    sm_shapes = [w.shape for w in sm_w]
    d_p, m_p, v_p = _adamw_flat(_pack_rows(sm_g), _pack_rows(sm_w), _pack_rows(sm_m), _pack_rows(sm_v))
    sm_d = _unpack_rows(d_p, sm_shapes)
    sm_nm = _unpack_rows(m_p, sm_shapes)
    sm_nv = _unpack_rows(v_p, sm_shapes)

    loss = loss_row[0]
    grad_x = dh[N_META:n_real][None]

    def ordered(sm, which):
        bo = {n: big_out[n][which] for n in names}
        return [sm[0], sm[1], bo["w_in"], sm[2], sm[3], sm[4], bo["w_out"], sm[5], bo["w_gate"], bo["w_up"], bo["w_down"], sm[6]]

    return (loss, grad_x, *ordered(sm_g, 0), *ordered(sm_d, 1), *ordered(sm_nm, 2), *ordered(sm_nv, 3))
```

```python
import functools

import jax
import jax.numpy as jnp
from jax import lax
from jax.experimental import pallas as pl
from jax.experimental.pallas import tpu as pltpu

F32 = jnp.float32
BF16 = jnp.bfloat16

D = 2048
N_META = 16
HEADS = 4
DQK = 128
DV = 256
MW = HEADS * DV
CW = D - MW
QKW = HEADS * DQK
DFF = 5632
DIN = 6152
NCHIP = 4
DEPTH = 2
GATE_CAP = 15.0
EPS = 1e-6
QK_SCALE = DQK ** -0.5
LANE = 128
VMEM_LIMIT = 60 * 1024 * 1024

ADAM_LR = 0.001
ADAM_B1 = 0.9
ADAM_B2 = 0.999
ADAM_EPS = 1e-08
ADAM_WD = 0.01
ADAM_STEP = 10

MESH = pl.DeviceIdType.MESH
ANY = pl.BlockSpec(memory_space=pl.ANY)


def _cp(sem):
    return pltpu.CompilerParams(dimension_semantics=sem, vmem_limit_bytes=VMEM_LIMIT)


def _sds(shape, dtype):
    return jax.ShapeDtypeStruct(shape, dtype)


_DN = {"nn": (((1,), (0,)), ((), ())), "nt": (((1,), (1,)), ((), ())), "tn": (((0,), (0,)), ((), ()))}


def _mm(name, kind, a, b, a_spec, b_spec, o_spec, out_shape, grid, acc_shape=None, res=None, res_spec=None,
        dep=None):
    nk = grid[2]
    has_res = res is not None
    n_in = 2 + has_res + (dep is not None)

    def body(*refs):
        a_ref, b_ref = refs[0], refs[1]
        r_ref = refs[2] if has_res else None
        o_ref = refs[n_in]
        p = lax.dot_general(a_ref[...], b_ref[...], _DN[kind], preferred_element_type=F32)
        if nk == 1:
            if r_ref is not None:
                p = p + r_ref[...]
            o_ref[...] = p.astype(o_ref.dtype)
        else:
            acc = refs[-1]
            k = pl.program_id(2)

            @pl.when(k == 0)
            def _():
                acc[...] = p

            @pl.when(jnp.logical_and(k > 0, k < nk - 1))
            def _():
                acc[...] += p

            @pl.when(k == nk - 1)
            def _():
                r = acc[...] + p
                if r_ref is not None:
                    r = r + r_ref[...]
                o_ref[...] = r.astype(o_ref.dtype)

    ins = [a, b] + ([res] if has_res else []) + ([dep] if dep is not None else [])
    in_specs = [a_spec, b_spec] + ([res_spec] if has_res else []) + ([ANY] if dep is not None else [])
    scratch = [pltpu.VMEM(acc_shape, F32)] if nk > 1 else []
    return pl.pallas_call(
        body, name=name, grid=grid, in_specs=in_specs, out_specs=o_spec, out_shape=out_shape,
        scratch_shapes=scratch, compiler_params=_cp(("parallel", "parallel", "arbitrary")))(*ins)


def _norm_fwd(name, h, w):
    lp = h.shape[0]
    tm = lp // 4

    def body(h_ref, w_ref, o_ref):
        x = h_ref[...]
        r = lax.rsqrt(jnp.mean(x * x, axis=1, keepdims=True) + EPS)
        o_ref[...] = (x * r * w_ref[...]).astype(BF16)

    return pl.pallas_call(
        body, name=name, grid=(4,),
        in_specs=[pl.BlockSpec((tm, D), lambda i: (i, 0)), pl.BlockSpec((1, D), lambda i: (0, 0))],
        out_specs=pl.BlockSpec((tm, D), lambda i: (i, 0)), out_shape=_sds((lp, D), BF16),
        compiler_params=_cp(("parallel",)))(h, w)


def _norm_bwd(name, dy, h, w, dres):
    lp = h.shape[0]
    tm = lp // 8

    def body(dy_ref, h_ref, w_ref, dres_ref, dh_ref, dhb_ref, dw_ref):
        x = h_ref[...]
        r = lax.rsqrt(jnp.mean(x * x, axis=1, keepdims=True) + EPS)
        xh = x * r
        dy_v = dy_ref[...]
        dxh = dy_v * w_ref[...]
        dx = r * (dxh - xh * jnp.mean(dxh * xh, axis=1, keepdims=True))
        dh = dres_ref[...] + dx
        dh_ref[...] = dh
        dhb_ref[...] = dh.astype(BF16)

        @pl.when(pl.program_id(0) == 0)
        def _():
            dw_ref[...] = jnp.zeros_like(dw_ref)

        dw_ref[0:1, :] += jnp.sum(dy_v * xh, axis=0, keepdims=True)

    row = pl.BlockSpec((tm, D), lambda i: (i, 0))
    return pl.pallas_call(
        body, name=name, grid=(8,),
        in_specs=[row, row, pl.BlockSpec((1, D), lambda i: (0, 0)), row],
        out_specs=[row, row, pl.BlockSpec((8, D), lambda i: (0, 0))],
        out_shape=[_sds((lp, D), F32), _sds((lp, D), BF16), _sds((8, D), F32)],
        compiler_params=_cp(("arbitrary",)))(dy, h, w, dres)


def _loss_head(h, tgt, w, n_real):
    lp = h.shape[0]
    tm = lp // 8

    def body(h_ref, t_ref, w_ref, dh_ref, dhb_ref, dw_ref, loss_ref):
        i = pl.program_id(0)
        x = h_ref[...]
        r = lax.rsqrt(jnp.mean(x * x, axis=1, keepdims=True) + EPS)
        xh = x * r
        wv = w_ref[...]
        row = i * tm + lax.broadcasted_iota(jnp.int32, (tm, 1), 0)
        valid = jnp.logical_and(row >= N_META, row < n_real)
        err = jnp.where(valid, xh * wv - t_ref[...], 0.0)
        dy_v = err * (1.0 / D)
        dxh = dy_v * wv
        dx = r * (dxh - xh * jnp.mean(dxh * xh, axis=1, keepdims=True))
        dh_ref[...] = dx
        dhb_ref[...] = dx.astype(BF16)

        @pl.when(i == 0)
        def _():
            dw_ref[...] = jnp.zeros_like(dw_ref)
            loss_ref[...] = jnp.zeros_like(loss_ref)

        dw_ref[0:1, :] += jnp.sum(dy_v * xh, axis=0, keepdims=True)
        part = jnp.sum(jnp.sum(err * err, axis=1, keepdims=True), axis=0, keepdims=True) * (0.5 / D)
        loss_ref[...] += jnp.broadcast_to(part, loss_ref.shape)

    row = pl.BlockSpec((tm, D), lambda i: (i, 0))
    return pl.pallas_call(
        body, name="loss_head", grid=(8,),
        in_specs=[row, row, pl.BlockSpec((1, D), lambda i: (0, 0))],
        out_specs=[row, row, pl.BlockSpec((8, D), lambda i: (0, 0)), pl.BlockSpec((8, LANE), lambda i: (0, 0))],
        out_shape=[_sds((lp, D), F32), _sds((lp, D), BF16), _sds((8, D), F32), _sds((8, LANE), F32)],
        compiler_params=_cp(("arbitrary",)))(h, tgt, w)


def _ffn_up(name, hf, wg4, wu4):
    lp = hf.shape[0]
    tm = lp // 4
    tn = DFF // NCHIP

    def body(x_ref, wg_ref, wu_ref, g_ref, u_ref, a_ref):
        x = x_ref[...]
        g = jnp.dot(x, wg_ref[...], preferred_element_type=F32)
        u = jnp.dot(x, wu_ref[...], preferred_element_type=F32)
        g_ref[...] = g.astype(BF16)
        u_ref[...] = u.astype(BF16)
        a_ref[...] = (g * jax.nn.sigmoid(g) * u).astype(BF16)

    wspec = pl.BlockSpec((None, D, tn), lambda j, i: (j, 0, 0))
    ospec = pl.BlockSpec((tm, tn), lambda j, i: (i, j))
    o = _sds((lp, DFF), BF16)
    return pl.pallas_call(
        body, name=name, grid=(NCHIP, 4),
        in_specs=[pl.BlockSpec((tm, D), lambda j, i: (i, 0)), wspec, wspec],
        out_specs=[ospec, ospec, ospec], out_shape=[o, o, o],
        compiler_params=_cp(("parallel", "parallel")))(hf, wg4, wu4)


def _ffn_bwd_act(name, dhb, wd4, g, u, dep=None):
    lp = dhb.shape[0]
    tm = lp // 4
    tn = DFF // NCHIP
    deps = [] if dep is None else [dep]

    def body(d_ref, w_ref, g_ref, u_ref, *rest):
        dg_ref, du_ref = rest[len(deps):]
        da = lax.dot_general(d_ref[...], w_ref[...], _DN["nt"], preferred_element_type=F32)
        gv = g_ref[...].astype(F32)
        uv = u_ref[...].astype(F32)
        sg = jax.nn.sigmoid(gv)
        dg_ref[...] = (da * uv * (sg * (1.0 + gv * (1.0 - sg)))).astype(BF16)
        du_ref[...] = (da * (gv * sg)).astype(BF16)

    ospec = pl.BlockSpec((tm, tn), lambda j, i: (i, j))
    o = _sds((lp, DFF), BF16)
    return pl.pallas_call(
        body, name=name, grid=(NCHIP, 4),
        in_specs=[pl.BlockSpec((tm, D), lambda j, i: (i, 0)),
                  pl.BlockSpec((None, tn, D), lambda j, i: (j, 0, 0)), ospec, ospec] + [ANY] * len(deps),
        out_specs=[ospec, ospec], out_shape=[o, o],
        compiler_params=_cp(("parallel", "parallel")))(dhb, wd4, g, u, *deps)


def _ffn_bwd_in(name, dg, du, wg4, wu4):
    lp = dg.shape[0]
    th = lp // 2
    tk = DFF // NCHIP

    def body(dg_ref, du_ref, wg_ref, wu_ref, o_ref, acc):
        k = pl.program_id(2)
        p = (lax.dot_general(dg_ref[...], wg_ref[...], _DN["nt"], preferred_element_type=F32)
             + lax.dot_general(du_ref[...], wu_ref[...], _DN["nt"], preferred_element_type=F32))

        @pl.when(k == 0)
        def _():
            acc[...] = p

        @pl.when(jnp.logical_and(k > 0, k < NCHIP - 1))
        def _():
            acc[...] += p

        @pl.when(k == NCHIP - 1)
        def _():
            o_ref[...] = acc[...] + p

    a_spec = pl.BlockSpec((th, tk), lambda i, j, k: (i, k))
    w_spec = pl.BlockSpec((None, 1024, tk), lambda i, j, k: (k, j, 0))
    return pl.pallas_call(
        body, name=name, grid=(2, 2, NCHIP), in_specs=[a_spec, a_spec, w_spec, w_spec],
        out_specs=pl.BlockSpec((th, 1024), lambda i, j, k: (i, j)), out_shape=_sds((lp, D), F32),
        scratch_shapes=[pltpu.VMEM((th, 1024), F32)],
        compiler_params=_cp(("parallel", "parallel", "arbitrary")))(dg, du, wg4, wu4)


def _shift_rows(x, d, row):
    return jnp.where(row >= d, pltpu.roll(x, d, axis=0), 0.0)


def _scan_steps(lp):
    d = 1
    while d < lp:
        yield d
        d *= 2


def _gate_values(pre):
    t = GATE_CAP * jnp.tanh(pre * (1.0 / GATE_CAP))
    lf = jnp.minimum(t, 0.0) - jnp.log(1.0 + jnp.exp(-jnp.abs(t)))
    return t, lf


def _gate_prep(name, gates_pre, bias):
    lp = gates_pre.shape[0]

    def body(p_ref, b_ref, grow_ref, m_ref, c_ref):
        pre = p_ref[...] + b_ref[...]
        lane = lax.broadcasted_iota(jnp.int32, (lp, LANE), 1)
        row = lax.broadcasted_iota(jnp.int32, (lp, LANE), 0)
        t, lf = _gate_values(pre)
        f = jnp.where(jnp.logical_and(lane >= HEADS, lane < 2 * HEADS), lf, 0.0)
        for d in _scan_steps(lp):
            f = f + _shift_rows(f, d, row)
        fs = pltpu.roll(f, LANE - HEADS, axis=1)
        g = jnp.where(lane < HEADS, t - fs, 0.0)
        m = g
        for d in _scan_steps(lp):
            m = jnp.maximum(m, jnp.where(row >= d, pltpu.roll(m, d, axis=0), m))
        grow_ref[...] = g.T
        m_ref[...] = m
        c_ref[...] = jnp.where(lane < HEADS, -fs - m, 0.0)

    full = pl.BlockSpec((lp, LANE), lambda: (0, 0))
    return pl.pallas_call(
        body, name=name, in_specs=[full, pl.BlockSpec((1, LANE), lambda: (0, 0))],
        out_specs=[pl.BlockSpec((LANE, lp), lambda: (0, 0)), full, full],
        out_shape=[_sds((LANE, lp), F32), _sds((lp, LANE), F32), _sds((lp, LANE), F32)],
        compiler_params=pltpu.CompilerParams(vmem_limit_bytes=VMEM_LIMIT))(gates_pre, bias)


def _pick_lane(blk, h):
    lane = lax.broadcasted_iota(jnp.int32, blk.shape, 1)
    return jnp.sum(jnp.where(lane == h, blk, 0.0), axis=1, keepdims=True)


def _mlstm_weights(q, k, grow, mcol, i, bq, nk):
    s = lax.dot_general(q, k, _DN["nt"], preferred_element_type=F32) * QK_SCALE
    row = i * bq + lax.broadcasted_iota(jnp.int32, (bq, 1), 0)
    col = lax.broadcasted_iota(jnp.int32, (1, nk), 1)
    a = jnp.where(col <= row, jnp.exp(jnp.minimum(grow - mcol, 0.0)), 0.0)
    return s, a


def _per_query_tile(i, nq, bq, lp, compute):
    for ii in range(nq):
        nk = min(lp, -(-((ii + 1) * bq) // LANE) * LANE)
        pl.when(i == ii)(functools.partial(compute, nk))


def _mlstm_fwd(name, qkv, grow, mcol_all, ccol_all, nq):
    lp = qkv.shape[0]
    bq = lp // nq

    def body(q_ref, k_ref, v_ref, grow_ref, m_ref, c_ref, o_ref):
        h = pl.program_id(0)
        i = pl.program_id(1)
        mcol = _pick_lane(m_ref[...], h)
        ccol = _pick_lane(c_ref[...], h)

        def compute(nk):
            grow_h = grow_ref[pl.ds(h, 1), 0:nk]
            s, a = _mlstm_weights(q_ref[...], k_ref[0:nk, :], grow_h, mcol, i, bq, nk)
            p = a * s
            den = jnp.sum(p, axis=1, keepdims=True)
            num = jnp.dot(p.astype(BF16), v_ref[0:nk, :], preferred_element_type=F32)
            o_ref[...] = num / jnp.maximum(jnp.abs(den), jnp.exp(ccol))

        _per_query_tile(i, nq, bq, lp, compute)

    return pl.pallas_call(
        body, name=name, grid=(HEADS, nq),
        in_specs=[pl.BlockSpec((bq, DQK), lambda h, i: (i, h)),
                  pl.BlockSpec((lp, DQK), lambda h, i: (0, HEADS + h)),
                  pl.BlockSpec((lp, DV), lambda h, i: (0, HEADS + h)),
                  pl.BlockSpec((8, lp), lambda h, i: (0, 0)),
                  pl.BlockSpec((bq, LANE), lambda h, i: (i, 0)),
                  pl.BlockSpec((bq, LANE), lambda h, i: (i, 0))],
        out_specs=pl.BlockSpec((bq, DV), lambda h, i: (i, h)),
        out_shape=_sds((lp, MW), F32),
        compiler_params=_cp(("parallel", "parallel")))(qkv, qkv, qkv, grow, mcol_all, ccol_all)


def _mlstm_bwd(name, qkv, grow, mcol_all, ccol_all, ht, dht, nq):
    lp = qkv.shape[0]
    bq = lp // nq

    def body(q_ref, k_ref, v_ref, grow_ref, m_ref, c_ref, ht_ref, dht_ref,
             dq_ref, dk_ref, dv_ref, dgrow_ref, dfx_ref, dkt_acc, dvt_acc):
        h = pl.program_id(0)
        i = pl.program_id(1)

        @pl.when(jnp.logical_and(h == 0, i == 0))
        def _():
            dgrow_ref[...] = jnp.zeros_like(dgrow_ref)
            dfx_ref[...] = jnp.zeros_like(dfx_ref)

        @pl.when(i == 0)
        def _():
            dkt_acc[...] = jnp.zeros_like(dkt_acc)
            dvt_acc[...] = jnp.zeros_like(dvt_acc)

        mcol = _pick_lane(m_ref[...], h)
        ccol = _pick_lane(c_ref[...], h)

        def compute(nk):
            q = q_ref[...]
            k = k_ref[0:nk, :]
            v = v_ref[0:nk, :]
            grow_h = grow_ref[pl.ds(h, 1), 0:nk]
            s, a = _mlstm_weights(q, k, grow_h, mcol, i, bq, nk)
            p = a * s
            den = jnp.sum(p, axis=1, keepdims=True)
            clamp = jnp.exp(ccol)
            active = jnp.abs(den) < clamp
            dd = jnp.maximum(jnp.abs(den), clamp)
            dht_v = dht_ref[...]
            hdh = jnp.sum(dht_v * ht_ref[...], axis=1, keepdims=True)
            dn = (dht_v / dd).astype(BF16)
            dden = jnp.where(active, 0.0, -(hdh / dd) * jnp.sign(den))
            dp = lax.dot_general(dn, v, _DN["nt"], preferred_element_type=F32) + dden
            rmat = dp * p
            dgrow_ref[pl.ds(h, 1), 0:nk] += jnp.sum(rmat, axis=0, keepdims=True)
            ds = (dp * a * QK_SCALE).astype(BF16)
            dq_ref[...] = jnp.dot(ds, k, preferred_element_type=F32).astype(BF16)
            dkt_acc[:, 0:nk] += lax.dot_general(q, ds, _DN["tn"], preferred_element_type=F32)
            dvt_acc[:, 0:nk] += lax.dot_general(dn, p.astype(BF16), _DN["tn"], preferred_element_type=F32)
            lane = lax.broadcasted_iota(jnp.int32, (bq, LANE), 1)
            r0 = pl.multiple_of(i * bq, 16)
            dfx_ref[pl.ds(r0, bq), :] += jnp.where(lane == h, jnp.sum(rmat, axis=1, keepdims=True), 0.0)

        _per_query_tile(i, nq, bq, lp, compute)

        @pl.when(i == nq - 1)
        def _():
            dk_ref[...] = dkt_acc[...].T.astype(BF16)
            dv_ref[...] = dvt_acc[...].T.astype(BF16)

    return pl.pallas_call(
        body, name=name, grid=(HEADS, nq),
        in_specs=[pl.BlockSpec((bq, DQK), lambda h, i: (i, h)),
                  pl.BlockSpec((lp, DQK), lambda h, i: (0, HEADS + h)),
                  pl.BlockSpec((lp, DV), lambda h, i: (0, HEADS + h)),
                  pl.BlockSpec((8, lp), lambda h, i: (0, 0)),
                  pl.BlockSpec((bq, LANE), lambda h, i: (i, 0)),
                  pl.BlockSpec((bq, LANE), lambda h, i: (i, 0)),
                  pl.BlockSpec((bq, DV), lambda h, i: (i, h)),
                  pl.BlockSpec((bq, DV), lambda h, i: (i, h))],
        out_specs=[pl.BlockSpec((bq, DQK), lambda h, i: (i, h)),
                   pl.BlockSpec((lp, DQK), lambda h, i: (0, h)),
                   pl.BlockSpec((lp, DV), lambda h, i: (0, h)),
                   pl.BlockSpec((LANE, lp), lambda h, i: (0, 0)),
                   pl.BlockSpec((lp, LANE), lambda h, i: (0, 0))],
        out_shape=[_sds((lp, QKW), BF16), _sds((lp, QKW), BF16), _sds((lp, MW), BF16),
                   _sds((LANE, lp), F32), _sds((lp, LANE), F32)],
        scratch_shapes=[pltpu.VMEM((DQK, lp), F32), pltpu.VMEM((DV, lp), F32)],
        compiler_params=_cp(("arbitrary", "arbitrary")))(qkv, qkv, qkv, grow, mcol_all, ccol_all, ht, dht)


def _gate_bwd(name, gates_pre, bias, dgrow, dfx):
    lp = gates_pre.shape[0]

    def body(p_ref, b_ref, dgrow_ref, dfx_ref, dg_ref, dgb_ref, db_ref):
        pre = p_ref[...] + b_ref[...]
        lane = lax.broadcasted_iota(jnp.int32, (lp, LANE), 1)
        row = lax.broadcasted_iota(jnp.int32, (lp, LANE), 0)
        th = jnp.tanh(pre * (1.0 / GATE_CAP))
        t = GATE_CAP * th
        dgc = jnp.where(lane < HEADS, dgrow_ref[...].T, 0.0)
        df = jnp.where(lane < HEADS, dfx_ref[...] - dgc, 0.0)
        for d in _scan_steps(lp):
            df = df + jnp.where(row < lp - d, pltpu.roll(df, lp - d, axis=0), 0.0)
        dlf = pltpu.roll(df, HEADS, axis=1)
        dt = jnp.where(lane < HEADS, dgc, dlf * jax.nn.sigmoid(-t))
        dpre = jnp.where(lane < 2 * HEADS, dt * (1.0 - th * th), 0.0)
        dg_ref[...] = dpre
        dgb_ref[...] = dpre.astype(BF16)
        db_ref[...] = jnp.broadcast_to(jnp.sum(dpre, axis=0, keepdims=True), db_ref.shape)

    full = pl.BlockSpec((lp, LANE), lambda: (0, 0))
    return pl.pallas_call(
        body, name=name,
        in_specs=[full, pl.BlockSpec((1, LANE), lambda: (0, 0)), pl.BlockSpec((LANE, lp), lambda: (0, 0)), full],
        out_specs=[full, full, pl.BlockSpec((8, LANE), lambda: (0, 0))],
        out_shape=[_sds((lp, LANE), F32), _sds((lp, LANE), BF16), _sds((8, LANE), F32)],
        compiler_params=pltpu.CompilerParams(vmem_limit_bytes=VMEM_LIMIT))(gates_pre, bias, dgrow, dfx)


CB = 256


def _e_specs(lp):
    return [pl.BlockSpec((None, lp, CB), functools.partial(lambda c, j: (c, 0, j), c)) for c in range(4)]


def _mix_fwd(name, ht, e, mnw, cw):
    lp = ht.shape[0]

    def body(ht_ref, og_ref, u_ref, gb_ref, gc_ref, mnw_ref, cw_ref, o_ref):
        x = ht_ref[...]
        r = lax.rsqrt(jnp.mean(x * x, axis=1, keepdims=True) + EPS)
        o_ref[0] = (jax.nn.sigmoid(og_ref[...].astype(F32)) * (x * r * mnw_ref[...])).astype(BF16)
        row = lax.broadcasted_iota(jnp.int32, (lp, CB), 0)
        a = gc_ref[...].astype(F32) * u_ref[...].astype(F32)
        conv = cw_ref[2:3, :] * a + cw_ref[1:2, :] * _shift_rows(a, 1, row) + cw_ref[0:1, :] * _shift_rows(a, 2, row)
        o_ref[1] = (gb_ref[...].astype(F32) * conv).astype(BF16)

    col = pl.BlockSpec((lp, CB), lambda j: (0, j))
    return pl.pallas_call(
        body, name=name, grid=(4,),
        in_specs=[col] + _e_specs(lp) + [pl.BlockSpec((1, CB), lambda j: (0, j)), pl.BlockSpec((8, CB), lambda j: (0, j))],
        out_specs=pl.BlockSpec((2, lp, CB), lambda j: (0, 0, j)), out_shape=_sds((2, lp, MW), BF16),
        compiler_params=_cp(("parallel",)))(ht, e, e, e, e, mnw, cw)


def _mix_bwd(name, dmix, ht, e, mnw, cw):
    lp = ht.shape[0]

    def body(dhm_ref, dhc_ref, ht_ref, og_ref, u_ref, gb_ref, gc_ref, mnw_ref, cw_ref,
             dht_ref, de_ref, dmnw_ref, dcw_ref):
        x = ht_ref[...]
        r = lax.rsqrt(jnp.mean(x * x, axis=1, keepdims=True) + EPS)
        xh = x * r
        w = mnw_ref[...]
        sg = jax.nn.sigmoid(og_ref[...].astype(F32))
        dhm = dhm_ref[...]
        de_ref[0] = (dhm * (xh * w) * (sg * (1.0 - sg))).astype(BF16)
        dn = dhm * sg
        dmnw_ref[...] = jnp.broadcast_to(jnp.sum(dn * xh, axis=0, keepdims=True), dmnw_ref.shape)
        dxh = dn * w
        dht_ref[...] = r * (dxh - xh * jnp.mean(dxh * xh, axis=1, keepdims=True))

        row = lax.broadcasted_iota(jnp.int32, (lp, CB), 0)
        uv = u_ref[...].astype(F32)
        gcv = gc_ref[...].astype(F32)
        gbv = gb_ref[...].astype(F32)
        a = gcv * uv
        a1 = _shift_rows(a, 1, row)
        a2 = _shift_rows(a, 2, row)
        dhc = dhc_ref[...]
        conv = cw_ref[2:3, :] * a + cw_ref[1:2, :] * a1 + cw_ref[0:1, :] * a2
        de_ref[2] = (dhc * conv).astype(BF16)
        dconv = dhc * gbv
        dcw_ref[...] = jnp.zeros_like(dcw_ref)
        dcw_ref[0:1, :] = jnp.sum(dconv * a2, axis=0, keepdims=True)
        dcw_ref[1:2, :] = jnp.sum(dconv * a1, axis=0, keepdims=True)
        dcw_ref[2:3, :] = jnp.sum(dconv * a, axis=0, keepdims=True)
        up1 = jnp.where(row < lp - 1, pltpu.roll(dconv, lp - 1, axis=0), 0.0)
        up2 = jnp.where(row < lp - 2, pltpu.roll(dconv, lp - 2, axis=0), 0.0)
        da = cw_ref[2:3, :] * dconv + cw_ref[1:2, :] * up1 + cw_ref[0:1, :] * up2
        de_ref[1] = (da * gcv).astype(BF16)
        de_ref[3] = (da * uv).astype(BF16)

    col = pl.BlockSpec((lp, CB), lambda j: (0, j))
    small = pl.BlockSpec((8, CB), lambda j: (0, j))
    return pl.pallas_call(
        body, name=name, grid=(4,),
        in_specs=[col, pl.BlockSpec((lp, CB), lambda j: (0, 4 + j)), col] + _e_specs(lp)
                 + [pl.BlockSpec((1, CB), lambda j: (0, j)), small],
        out_specs=[col, pl.BlockSpec((4, lp, CB), lambda j: (0, 0, j)), small, small],
        out_shape=[_sds((lp, MW), F32), _sds((4, lp, MW), BF16), _sds((8, MW), F32), _sds((8, CW), F32)],
        compiler_params=_cp(("parallel",)))(dmix, dmix, ht, e, e, e, e, mnw, cw)


def _row_tile(r, c, itemsize, budget=1536 * 1024, mult=16):
    best = None
    for t in range(mult, r + 1, mult):
        if r % t == 0 and t * c * itemsize <= budget:
            best = t
    if best is None:
        best = r
    return best


def _grid_spec(grid, in_specs, out_specs, scratch=()):
    return pltpu.PrefetchScalarGridSpec(num_scalar_prefetch=1, grid=grid, in_specs=in_specs,
                                        out_specs=out_specs, scratch_shapes=list(scratch))


def _cast_into(name, w, layer, pf):
    _, r, c = w.shape
    tr = _row_tile(r, c, 4)

    def body(pf_ref, x_ref, o_ref):
        o_ref[...] = x_ref[...].astype(BF16)

    return pl.pallas_call(
        body, name=name, out_shape=_sds((NCHIP, r, c), BF16),
        grid_spec=_grid_spec((r // tr,), [pl.BlockSpec((None, tr, c), lambda i, pf: (layer, i, 0))],
                             pl.BlockSpec((None, tr, c), lambda i, pf: (pf[1], i, 0))),
        compiler_params=_cp(("parallel",)))(pf, w)


TCOL = 256


def _cast_into_t(name, w_t, layer, pf):
    c, nl, r = w_t.shape

    def body(pf_ref, x_ref, o_ref):
        o_ref[...] = x_ref[:, layer, :].astype(BF16)

    return pl.pallas_call(
        body, name=name, out_shape=_sds((NCHIP, c, r), BF16),
        grid_spec=_grid_spec((r // TCOL,), [pl.BlockSpec((c, nl, TCOL), lambda i, pf: (0, 0, i))],
                             pl.BlockSpec((None, c, TCOL), lambda i, pf: (pf[1], 0, i))),
        compiler_params=_cp(("parallel",)))(pf, w_t)


def _add2_bf16(name, dw, got, pf, by_cols=False):
    n4, r2, c2 = got.shape

    def body(pf_ref, a_ref, b_ref, o_ref):
        o_ref[...] = (a_ref[...].astype(F32) + b_ref[...].astype(F32)).astype(BF16)

    if by_cols:
        nch = c2 // TCOL
        spec = pl.BlockSpec((None, r2, TCOL), lambda s, i, pf: (s, 0, i))
        mine = pl.BlockSpec((None, r2, TCOL), lambda s, i, pf: (s, 0, pf[0] * nch + i))
    else:
        tr = _row_tile(r2, c2, 4)
        nch = r2 // tr
        spec = pl.BlockSpec((None, tr, c2), lambda s, i, pf: (s, i, 0))
        mine = pl.BlockSpec((None, tr, c2), lambda s, i, pf: (s, pf[0] * nch + i, 0))
    return pl.pallas_call(
        body, name=name, out_shape=_sds((n4, r2, c2), BF16),
        grid_spec=_grid_spec((n4, nch), [mine, spec], spec),
        compiler_params=_cp(("parallel", "parallel")))(pf, dw, got)


def _adam_math(w, g, m, v):
    m2 = ADAM_B1 * m + (1.0 - ADAM_B1) * g
    v2 = ADAM_B2 * v + (1.0 - ADAM_B2) * (g * g)
    m_hat = m2 / (1.0 - ADAM_B1 ** ADAM_STEP)
    v_hat = v2 / (1.0 - ADAM_B2 ** ADAM_STEP)
    delta = -ADAM_LR * (m_hat / (jnp.sqrt(v_hat) + ADAM_EPS) + ADAM_WD * w)
    return delta, m2, v2


def _adamw_layer(name, layer, g_mine, g_theirs, w, m, v, prev, pf):
    _, r, c = w.shape
    r2 = r // 2
    tr = _row_tile(r2, c, 4, budget=1024 * 1024, mult=8)
    nch = r2 // tr
    n_alias = 0 if prev is None else 4

    def body(*refs):
        pf_ref, gm_ref, gt_ref, w_ref, m_ref, v_ref = refs[:6]
        go_ref, d_ref, mo_ref, vo_ref = refs[6 + n_alias:]
        mine = (pl.program_id(0) // nch) == pf_ref[0]
        gv = jnp.where(mine, gm_ref[...], gt_ref[...])
        delta, m2, v2 = _adam_math(w_ref[...], gv, m_ref[...], v_ref[...])
        go_ref[...] = gv
        d_ref[...] = delta
        mo_ref[...] = m2
        vo_ref[...] = v2

    slab = pl.BlockSpec((None, tr, c), lambda i, pf: (layer, i, 0))
    ins = [g_mine, g_theirs, w, m, v] + (list(prev) if prev is not None else [])
    in_specs = [pl.BlockSpec((tr, c), lambda i, pf: (jnp.clip(i - pf[0] * nch, 0, nch - 1), 0)),
                pl.BlockSpec((tr, c), lambda i, pf: (jnp.clip(i - (1 - pf[0]) * nch, 0, nch - 1), 0)),
                slab, slab, slab] + [ANY] * n_alias
    o = _sds(w.shape, F32)
    return pl.pallas_call(
        body, name=name, out_shape=[o] * 4, grid_spec=_grid_spec((2 * nch,), in_specs, [slab] * 4),
        input_output_aliases={6 + k: k for k in range(n_alias)},
        compiler_params=_cp(("parallel",)))(pf, *ins)


def _adamw_t(name, gs, w_t, m_t, v_t, pf):
    c, nl, r = w_t.shape
    ta = LANE
    nch = (r // 2) // ta

    def body(*refs):
        pf_ref = refs[0]
        g_refs = refs[1:1 + 2 * nl]
        w_ref, m_ref, v_ref, go_ref, d_ref, mo_ref, vo_ref = refs[1 + 2 * nl:]
        mine = (pl.program_id(0) // nch) == pf_ref[0]
        gv = jnp.stack([jnp.where(mine, g_refs[2 * l][...], g_refs[2 * l + 1][...]) for l in range(nl)], axis=1)
        delta, m2, v2 = _adam_math(w_ref[...], gv, m_ref[...], v_ref[...])
        go_ref[...] = gv
        d_ref[...] = delta
        mo_ref[...] = m2
        vo_ref[...] = v2

    both = pl.BlockSpec((c, nl, ta), lambda i, pf: (0, 0, i))
    g_specs = []
    for l in range(nl):
        g_specs += [pl.BlockSpec((c, ta), lambda i, pf: (0, jnp.clip(i - pf[0] * nch, 0, nch - 1))),
                    pl.BlockSpec((c, ta), lambda i, pf: (0, jnp.clip(i - (1 - pf[0]) * nch, 0, nch - 1)))]
    o = _sds(w_t.shape, F32)
    flat_g = [a for pair in gs for a in pair]
    return pl.pallas_call(
        body, name=name, out_shape=[o] * 4, grid_spec=_grid_spec((2 * nch,), g_specs + [both] * 3, [both] * 4),
        compiler_params=_cp(("parallel",)))(pf, *flat_g, w_t, m_t, v_t)


def _adamw_flat(g, w, m, v):
    def body(g_ref, w_ref, m_ref, v_ref, d_ref, mo_ref, vo_ref):
        delta, m2, v2 = _adam_math(w_ref[...], g_ref[...], m_ref[...], v_ref[...])
        d_ref[...] = delta
        mo_ref[...] = m2
        vo_ref[...] = v2

    o = _sds(w.shape, F32)
    return pl.pallas_call(body, name="adamw_small", out_shape=[o, o, o])(g, w, m, v)


def _place():
    x, y, c = lax.axis_index("x"), lax.axis_index("y"), lax.axis_index("c")
    chips = [(1 - x, y), (x, 1 - y), (1 - x, 1 - y)]
    return x, y, c, chips


HBM = pl.BlockSpec(memory_space=pltpu.HBM)
SEM = pl.BlockSpec(memory_space=pltpu.SEMAPHORE)
EFFECT = pltpu.SideEffectType.DATAFLOW_SIDE_EFFECTING


def _in_hbm(a):
    return pltpu.with_memory_space_constraint(a, pltpu.HBM)


def _rs_chips_copies(ins, lands, send, recv):
    x, y, c, chips = _place()
    cps = []
    for t in range(len(ins)):
        for k, chip in enumerate(chips):
            jk = 2 * chip[0] + chip[1]
            cps.append(pltpu.make_async_remote_copy(
                src_ref=ins[t].at[jk], dst_ref=lands[t].at[k], send_sem=send.at[3 * t + k],
                recv_sem=recv.at[3 * t + k], device_id=(*chip, c), device_id_type=MESH))
    return cps


def _rs_chips_start(name, ps):
    n = len(ps)

    def body(*refs):
        ins, lands = refs[:n], refs[n:2 * n]
        send, recv = refs[2 * n], refs[2 * n + 1]
        token = refs[-1]
        for cp in _rs_chips_copies(ins, lands, send, recv):
            cp.start()
        token[...] = jnp.zeros_like(token)

    dma = pltpu.SemaphoreType.DMA
    lands = [lax.empty((3,) + p.shape[1:], BF16) for p in ps]
    out_shape = ([dma((3 * n,)), dma((3 * n,))] + [pltpu.HBM(p.shape, BF16) for p in ps]
                 + [pltpu.HBM(z.shape, BF16) for z in lands] + [_sds((8, LANE), F32)])
    outs = pl.pallas_call(
        body, name=name, out_shape=out_shape, in_specs=[HBM] * (2 * n),
        out_specs=[SEM, SEM] + [HBM] * (2 * n) + [pl.BlockSpec(memory_space=pltpu.VMEM)],
        input_output_aliases={i: 2 + i for i in range(2 * n)},
        compiler_params=pltpu.CompilerParams(has_side_effects=EFFECT))(
            *[_in_hbm(p) for p in ps], *[_in_hbm(z) for z in lands])
    return outs[0], outs[1], outs[2:2 + n], outs[2 + n:2 + 2 * n], outs[-1]


def _rs_chips_wait(name, send, recv, ps, lands, afters):
    n = len(ps)

    def body(*refs):
        ins, zones = refs[:n], refs[n:2 * n]
        send_ref, recv_ref = refs[2 * n], refs[2 * n + 1]
        for cp in _rs_chips_copies(ins, zones, send_ref, recv_ref):
            cp.wait_send()
            cp.wait_recv()

    outs = pl.pallas_call(
        body, name=name, out_shape=[pltpu.HBM(p.shape, BF16) for p in ps] + [pltpu.HBM(z.shape, BF16) for z in lands],
        in_specs=[HBM] * (2 * n) + [SEM, SEM] + [ANY] * len(afters), out_specs=[HBM] * (2 * n),
        input_output_aliases={i: i for i in range(2 * n)},
        compiler_params=pltpu.CompilerParams(has_side_effects=EFFECT))(*ps, *lands, send, recv, *afters)
    return outs[:n], outs[n:]


def _sibling():
    x, y, c, _ = _place()
    return (x, y, 1 - c)


PAIR_ID = 0


def _pair_barrier():
    bar = pltpu.get_barrier_semaphore()
    pl.semaphore_signal(bar, inc=1, device_id=_sibling(), device_id_type=MESH)
    pl.semaphore_wait(bar, 1)


def _pair_send(name, dw, pf, by_cols=False):
    n4, r, c = dw.shape
    blk = (1, r, c // 2) if by_cols else (1, r // 2, c)
    idx = (lambda s, pf: (s, 0, 1 - pf[0])) if by_cols else (lambda s, pf: (s, 1 - pf[0], 0))

    def body(pf_ref, x_ref, got_ref, ssem, rsem):
        s = pl.program_id(0)
        pl.when(s == 0)(_pair_barrier)
        cp = pltpu.make_async_remote_copy(src_ref=x_ref, dst_ref=got_ref.at[pl.ds(s, 1)], send_sem=ssem,
                                          recv_sem=rsem, device_id=_sibling(), device_id_type=MESH)
        cp.start()
        cp.wait_send()

        @pl.when(s == n4 - 1)
        def _():
            pltpu.make_async_remote_copy(src_ref=got_ref, dst_ref=got_ref, send_sem=ssem, recv_sem=rsem,
                                         device_id=_sibling(), device_id_type=MESH).wait_recv()

    dma = pltpu.SemaphoreType.DMA
    return pl.pallas_call(
        body, name=name, out_shape=_sds((n4,) + blk[1:], BF16),
        grid_spec=_grid_spec((n4,), [pl.BlockSpec(blk, idx)], ANY, scratch=[dma(()), dma(())]),
        compiler_params=pltpu.CompilerParams(dimension_semantics=("arbitrary",), has_side_effects=True,
                                             collective_id=PAIR_ID, vmem_limit_bytes=VMEM_LIMIT))(pf, dw)


def _dw_pair(name, x, dy, down, dep=None):
    lp = x.shape[0]
    tk = DFF // NCHIP
    if down:
        grid = (NCHIP, 2)
        x_spec = pl.BlockSpec((lp, tk), lambda s, j: (0, s))
        dy_spec = pl.BlockSpec((lp, 1024), lambda s, j: (0, j))
        o_spec = pl.BlockSpec((1, tk, 1024), lambda s, j: (s, 0, j))
        dw_shape, got_shape = (NCHIP, tk, D), (NCHIP, tk // 2, D)
    else:
        grid = (2, NCHIP)
        x_spec = pl.BlockSpec((lp, 1024), lambda i, s: (0, i))
        dy_spec = pl.BlockSpec((lp, tk), lambda i, s: (0, s))
        o_spec = pl.BlockSpec((1, 1024, tk), lambda i, s: (s, i, 0))
        dw_shape, got_shape = (NCHIP, D, tk), (NCHIP, D // 2, tk)

    send_shape = (tk // 2, 1024) if down else (1024, tk)
    n_sends = 2 * NCHIP if down else NCHIP

    deps = [] if dep is None else [dep]

    def body(x_ref, dy_ref, *rest):
        o_ref, got_ref, sbuf, ssem, rsem = rest[len(deps):]
        g0, g1 = pl.program_id(0), pl.program_id(1)
        pl.when(jnp.logical_and(g0 == 0, g1 == 0))(_pair_barrier)
        tile = lax.dot_general(x_ref[...], dy_ref[...], _DN["tn"], preferred_element_type=F32).astype(BF16)
        o_ref[0] = tile
        _, _, c, _ = _place()

        def send(n, part, dst):
            for s in (0, 1):
                cp = pltpu.make_async_remote_copy(src_ref=sbuf.at[pl.ds(s, 1)], dst_ref=dst, send_sem=ssem.at[s],
                                                  recv_sem=rsem, device_id=_sibling(), device_id_type=MESH)

                @pl.when(n % 2 == s)
                def _():
                    pl.when(n >= 2)(cp.wait_send)
                    sbuf[s] = part()
                    cp.start()

            @pl.when(n == n_sends - 1)
            def _():
                for s in (0, 1):
                    pltpu.make_async_remote_copy(src_ref=sbuf.at[pl.ds(s, 1)], dst_ref=dst, send_sem=ssem.at[s],
                                                 recv_sem=rsem, device_id=_sibling(), device_id_type=MESH).wait_send()

        if down:
            hr = tk // 2
            rows = pl.ds(pl.multiple_of((1 - c) * hr, 16), hr)
            send(g0 * 2 + g1, lambda: o_ref[0, rows, :],
                 got_ref.at[pl.ds(g0, 1), :, pl.ds(pl.multiple_of(g1 * 1024, LANE), 1024)])
        else:
            pl.when(g0 == 1 - c)(lambda: send(g1, lambda: tile, got_ref.at[pl.ds(g1, 1)]))

        @pl.when(jnp.logical_and(g0 == grid[0] - 1, g1 == grid[1] - 1))
        def _():
            pltpu.make_async_remote_copy(src_ref=got_ref, dst_ref=got_ref, send_sem=ssem.at[0], recv_sem=rsem,
                                         device_id=_sibling(), device_id_type=MESH).wait_recv()

    dma = pltpu.SemaphoreType.DMA
    return pl.pallas_call(
        body, name=name, grid=grid, in_specs=[x_spec, dy_spec] + [ANY] * len(deps), out_specs=[o_spec, ANY],
        out_shape=[_sds(dw_shape, BF16), _sds(got_shape, BF16)],
        scratch_shapes=[pltpu.VMEM((2,) + send_shape, BF16), dma((2,)), dma(())],
        compiler_params=pltpu.CompilerParams(dimension_semantics=("arbitrary", "arbitrary"), has_side_effects=True,
                                             collective_id=PAIR_ID, vmem_limit_bytes=VMEM_LIMIT))(x, dy, *deps)


def _add4_join(name, p, got, pf, by_cols=False):
    n4, r2, c = p.shape
    if by_cols:
        tr, nch = r2, c // TCOL
        blk, idx = (r2, TCOL), (lambda i: (0, i))
    else:
        tr = _row_tile(r2, c, 4)
        nch = r2 // tr
        blk, idx = (tr, c), (lambda i: (i, 0))

    def body(pf_ref, p_ref, g_ref, mine_ref, theirs_ref, sbuf, ssem, rsem):
        i = pl.program_id(0)
        pl.when(i == 0)(_pair_barrier)
        s = p_ref[...].astype(F32)
        for k in range(3):
            s = s + g_ref[k].astype(F32)
        mine_ref[...] = s
        if by_cols:
            dst = theirs_ref.at[:, pl.ds(pl.multiple_of(i * TCOL, LANE), TCOL)]
        else:
            dst = theirs_ref.at[pl.ds(pl.multiple_of(i * tr, 8), tr), :]

        def copy(slot):
            return pltpu.make_async_remote_copy(src_ref=sbuf.at[slot], dst_ref=dst, send_sem=ssem.at[slot],
                                                recv_sem=rsem, device_id=_sibling(), device_id_type=MESH)

        for slot in (0, 1):
            @pl.when(i % 2 == slot)
            def _():
                pl.when(i >= 2)(copy(slot).wait_send)
                sbuf[slot] = s
                copy(slot).start()

        @pl.when(i == nch - 1)
        def _():
            for slot in range(min(nch, 2)):
                copy(slot).wait_send()
            pltpu.make_async_remote_copy(src_ref=theirs_ref, dst_ref=theirs_ref, send_sem=ssem.at[0], recv_sem=rsem,
                                         device_id=_sibling(), device_id_type=MESH).wait_recv()

    dma = pltpu.SemaphoreType.DMA
    o = _sds((r2, c), F32)
    return pl.pallas_call(
        body, name=name, out_shape=[o, o],
        grid_spec=_grid_spec((nch,), [pl.BlockSpec((None,) + blk, lambda i, pf: (pf[1],) + idx(i)),
                                      pl.BlockSpec((3,) + blk, lambda i, pf: (0,) + idx(i))],
                             [pl.BlockSpec(blk, lambda i, pf: idx(i)), ANY],
                             scratch=[pltpu.VMEM((2,) + blk, F32), dma((2,)), dma(())]),
        compiler_params=pltpu.CompilerParams(dimension_semantics=("arbitrary",), has_side_effects=True,
                                             collective_id=PAIR_ID, vmem_limit_bytes=VMEM_LIMIT))(pf, p, got)


def _half_of(g, slot, which, axis):
    half = g.shape[axis] // 2
    if axis == 1:
        return g.at[slot, pl.ds(which * half, half), :]
    return g.at[slot, :, pl.ds(which * half, half)]


def _ag_copies(arrs, split, send, recv):
    x, y, c, chips = _place()
    j = 2 * x + y
    cps = []
    for t, g in enumerate(arrs):
        piece = _half_of(g, j, c, split[t]) if split[t] else g.at[j]
        for k, chip in enumerate(chips):
            cps.append(pltpu.make_async_remote_copy(
                src_ref=piece, dst_ref=piece, send_sem=send.at[3 * t + k], recv_sem=recv.at[3 * t + k],
                device_id=(*chip, c), device_id_type=MESH))
    return cps


def _ag_start(name, groups, splits):
    sizes = [len(g) for g in groups]
    flat = [a for g in groups for a in g]
    n = len(flat)

    def body(*refs):
        ins = refs[:n]
        sems = refs[n:n + 2 * len(groups)]
        o = 0
        for gi, sz in enumerate(sizes):
            for cp in _ag_copies(ins[o:o + sz], splits[gi], sems[2 * gi], sems[2 * gi + 1]):
                cp.start()
            o += sz
        refs[-1][...] = jnp.zeros_like(refs[-1])

    dma = pltpu.SemaphoreType.DMA
    sem_shapes = [dma((3 * sz,)) for sz in sizes for _ in range(2)]
    outs = pl.pallas_call(
        body, name=name, out_shape=sem_shapes + [pltpu.HBM(a.shape, a.dtype) for a in flat] + [_sds((8, LANE), F32)],
        in_specs=[HBM] * n,
        out_specs=[SEM] * len(sem_shapes) + [HBM] * n + [pl.BlockSpec(memory_space=pltpu.VMEM)],
        input_output_aliases={i: len(sem_shapes) + i for i in range(n)},
        compiler_params=pltpu.CompilerParams(has_side_effects=EFFECT))(*[_in_hbm(a) for a in flat])
    sems, arrs, o = [], [], len(sem_shapes)
    for gi, sz in enumerate(sizes):
        sems.append((outs[2 * gi], outs[2 * gi + 1]))
        arrs.append(list(outs[o:o + sz]))
        o += sz
    return sems, arrs, outs[-1]


def _ag_wait(name, arrs, split, send, recv, afters):
    n = len(arrs)

    def body(*refs):
        for cp in _ag_copies(refs[:n], split, refs[n], refs[n + 1]):
            cp.wait_send()
            cp.wait_recv()

    return pl.pallas_call(
        body, name=name, out_shape=[pltpu.HBM(a.shape, a.dtype) for a in arrs],
        in_specs=[HBM] * n + [SEM, SEM] + [ANY] * len(afters), out_specs=[HBM] * n,
        input_output_aliases={i: i for i in range(n)},
        compiler_params=pltpu.CompilerParams(has_side_effects=EFFECT))(*arrs, send, recv, *afters)


def _ag_forward(name, arrs, axes):
    n = len(arrs)

    def half_shape(t):
        _, r, cc = arrs[t].shape
        return (r // 2, cc) if axes[t] == 1 else (r, cc // 2)

    def body(*refs):
        g = refs[n:2 * n]
        bufs = refs[2 * n:3 * n]
        fsend, frecv, lsem = refs[3 * n:]
        x, y, c, chips = _place()
        _pair_barrier()
        for t in range(n):
            pend = [None, None]
            for k, chip in enumerate(chips):
                jk = 2 * chip[0] + chip[1]
                slot = k % 2
                if pend[slot] is not None:
                    pend[slot].wait_send()
                part = _half_of(g[t], jk, c, axes[t])
                ld = pltpu.make_async_copy(part, bufs[t].at[slot], lsem.at[2 * t + slot])
                ld.start()
                ld.wait()
                cp = pltpu.make_async_remote_copy(
                    src_ref=bufs[t].at[slot], dst_ref=part, send_sem=fsend.at[2 * t + slot],
                    recv_sem=frecv.at[t], device_id=(x, y, 1 - c), device_id_type=MESH)
                cp.start()
                pend[slot] = cp
            for cp in pend:
                cp.wait_send()
        for t in range(n):
            hr, hc = half_shape(t)
            passed = g[t].at[pl.ds(0, 3), pl.ds(0, hr), pl.ds(0, hc)]
            pltpu.make_async_remote_copy(
                src_ref=passed, dst_ref=passed, send_sem=fsend.at[2 * t], recv_sem=frecv.at[t],
                device_id=(x, y, 1 - c), device_id_type=MESH).wait_recv()

    dma = pltpu.SemaphoreType.DMA
    scratch = [pltpu.VMEM((2,) + half_shape(t), BF16) for t in range(n)]
    scratch += [dma((2 * n,)), dma((n,)), dma((2 * n,))]
    return pl.pallas_call(
        body, name=name, in_specs=[ANY] * n, out_specs=[ANY] * n, out_shape=[_sds(a.shape, a.dtype) for a in arrs],
        scratch_shapes=scratch, input_output_aliases={t: t for t in range(n)},
        compiler_params=pltpu.CompilerParams(has_side_effects=True, collective_id=PAIR_ID,
                                             vmem_limit_bytes=VMEM_LIMIT))(*arrs)


def _rs_begin(tag, dws, by_cols, pf):
    ps = []
    for t, dw in enumerate(dws):
        if isinstance(dw, (list, tuple)):
            dw, got = dw
        else:
            got = _pair_send("rs_pair_%s_%d" % (tag, t), dw, pf, by_cols[t])
        ps.append(_add2_bf16("rs_add2_%s_%d" % (tag, t), dw, got, pf, by_cols[t]))
    send, recv, ps_thru, lands, token = _rs_chips_start("rs_chips_start_" + tag, ps)
    return (tag, send, recv, ps_thru, lands, by_cols), token


def _rs_end(handle, afters, pf):
    tag, send, recv, ps, lands, by_cols = handle
    ps, got2 = _rs_chips_wait("rs_chips_wait_" + tag, send, recv, ps, lands, afters)
    return [_add4_join("rs_add4_%s_%d" % (tag, t), p, g2, pf, by_cols[t]) for t, (p, g2) in enumerate(zip(ps, got2))]


def _allreduce_small(pack):
    r = pack.shape[0]
    flips = [(fx, fy, fc) for fx in (0, 1) for fy in (0, 1) for fc in (0, 1)][1:]

    def body(p_ref, o_ref, gat, send, recv):
        x, y, c, _ = _place()
        me = 4 * x + 2 * y + c
        gat[me] = p_ref[...]
        cps = []
        for k, (fx, fy, fc) in enumerate(flips):
            peer = ((1 - x) if fx else x, (1 - y) if fy else y, (1 - c) if fc else c)
            cp = pltpu.make_async_remote_copy(
                src_ref=p_ref, dst_ref=gat.at[me], send_sem=send.at[k], recv_sem=recv.at[k],
                device_id=peer, device_id_type=MESH)
            cp.start()
            cps.append(cp)
        for k, (fx, fy, fc) in enumerate(flips):
            peer = ((1 - x) if fx else x, (1 - y) if fy else y, (1 - c) if fc else c)
            src = 4 * peer[0] + 2 * peer[1] + peer[2]
            pltpu.make_async_remote_copy(
                src_ref=p_ref, dst_ref=gat.at[src], send_sem=send.at[k], recv_sem=recv.at[k],
                device_id=peer, device_id_type=MESH).wait_recv()
        for cp in cps:
            cp.wait_send()
        s = gat[0]
        for d in range(1, 8):
            s = s + gat[d]
        o_ref[...] = s

    dma = pltpu.SemaphoreType.DMA
    vm = pl.BlockSpec(memory_space=pltpu.VMEM)
    return pl.pallas_call(
        body, name="allreduce_small", in_specs=[vm], out_specs=vm, out_shape=_sds((r, LANE), F32),
        scratch_shapes=[pltpu.VMEM((8, r, LANE), F32), dma((7,)), dma((7,))],
        compiler_params=pltpu.CompilerParams(has_side_effects=True))(pack)


def _in_weights(win_g):
    full = jnp.concatenate([win_g[s] for s in range(NCHIP)], axis=0)
    wqkv = full[:2048]
    og = full[2048:3072]
    gates = jnp.pad(full[3072:3080], ((0, LANE - 8), (0, 0)))
    u = full[3080:4104]
    gb = full[4104:5128]
    gc = full[5128:6152]
    return wqkv, jnp.stack([og, u, gb, gc]), gates


def _in_grads(dwqkv, dwe, dwgt):
    full = jnp.concatenate([dwqkv, dwe[0], dwgt[:8], dwe[1], dwe[2], dwe[3]], axis=0)
    sw = DIN // NCHIP
    return jnp.stack([full[s * sw:(s + 1) * sw] for s in range(NCHIP)])


def _layer_fwd(l, h, get, small):
    lp = h.shape[0]
    th = lp // 2
    wqkv, we, wgt = _in_weights(get("in", h)[0])
    nmw, bias, mnw, cw, nfw = small
    tag = "_l%d" % l
    hn = _norm_fwd("norm_mix" + tag, h, nmw)
    qkv = _mm("proj_qkv" + tag, "nt", hn, wqkv,
              pl.BlockSpec((lp, D), lambda i, j, k: (0, 0)), pl.BlockSpec((512, D), lambda i, j, k: (j, 0)),
              pl.BlockSpec((lp, 512), lambda i, j, k: (0, j)), _sds((lp, 2048), BF16), (1, 4, 1))
    e = _mm("proj_e" + tag, "nt", hn, we,
            pl.BlockSpec((lp, D), lambda i, j, k: (0, 0)), pl.BlockSpec((None, 512, D), lambda i, j, k: (j // 2, j % 2, 0)),
            pl.BlockSpec((None, lp, 512), lambda i, j, k: (j // 2, 0, j % 2)), _sds((4, lp, 1024), BF16), (1, 8, 1))
    gpre = _mm("proj_gates" + tag, "nt", hn, wgt,
               pl.BlockSpec((lp, D), lambda i, j, k: (0, 0)), pl.BlockSpec((LANE, D), lambda i, j, k: (0, 0)),
               pl.BlockSpec((lp, LANE), lambda i, j, k: (0, 0)), _sds((lp, LANE), F32), (1, 1, 1))
    grow, mcol, ccol = _gate_prep("gate_prep" + tag, gpre, bias)
    ht = _mlstm_fwd("mlstm_fwd" + tag, qkv, grow, mcol, ccol, 4)
    mix = _mix_fwd("mix_fwd" + tag, ht, e, mnw, cw)
    wout_g = get("out", mix)[0]
    wout = wout_g.reshape(D, D)
    h1 = _mm("out_proj" + tag, "nn", mix, wout,
             pl.BlockSpec((None, th, 1024), lambda i, j, k: (k, i, 0)),
             pl.BlockSpec((1024, 1024), lambda i, j, k: (k, j)),
             pl.BlockSpec((th, 1024), lambda i, j, k: (i, j)), _sds((lp, D), F32), (2, 2, 2),
             acc_shape=(th, 1024), res=h, res_spec=pl.BlockSpec((th, 1024), lambda i, j, k: (i, j)))
    wg_g, wu_g = get("ffn", h1)
    hf = _norm_fwd("norm_ffn" + tag, h1, nfw)
    g, u, a = _ffn_up("ffn_up" + tag, hf, wg_g, wu_g)
    wd_g = get("down", a)[0]
    tk = DFF // NCHIP
    h2 = _mm("ffn_down" + tag, "nn", a, wd_g,
             pl.BlockSpec((th, tk), lambda i, j, k: (i, k)),
             pl.BlockSpec((None, tk, 1024), lambda i, j, k: (k, 0, j)),
             pl.BlockSpec((th, 1024), lambda i, j, k: (i, j)), _sds((lp, D), F32), (2, 2, NCHIP),
             acc_shape=(th, 1024), res=h1, res_spec=pl.BlockSpec((th, 1024), lambda i, j, k: (i, j)))
    saved = (h, hn, qkv, e, gpre, grow, mcol, ccol, ht, mix, h1, hf, g, u, a)
    return h2, saved, (wqkv, we, wgt, wout_g, wg_g, wu_g, wd_g)


def _layer_bwd(l, dh2, dh2b, saved, wts, small, ffn_done, dep):
    h, hn, qkv, e, gpre, grow, mcol, ccol, ht, mix, h1, hf, g, u, a = saved
    wqkv, we, wgt, wout_g, wg_g, wu_g, wd_g = wts
    nmw, bias, mnw, cw, nfw = small
    lp = h.shape[0]
    th = lp // 2
    tk = DFF // NCHIP
    tag = "_l%d" % l
    half_rows = lambda i, j, k: (i, j)

    dwd = _dw_pair("dw_down" + tag, a, dh2b, True, dep)
    dg, du = _ffn_bwd_act("ffn_bwd_act" + tag, dh2b, wd_g, g, u, dep)
    dwg = _dw_pair("dw_gate" + tag, hf, dg, False)
    dwu = _dw_pair("dw_up" + tag, hf, du, False)
    dhf = _ffn_bwd_in("dhf" + tag, dg, du, wg_g, wu_g)
    dh1, dh1b, dnfw = _norm_bwd("norm_ffn_bwd" + tag, dhf, h1, nfw, dh2)
    ffn_started = ffn_done([dwg, dwu, dwd])

    dwout = _mm("dw_out" + tag, "tn", mix, dh1b,
                pl.BlockSpec((None, lp, 1024), lambda i, j, k: (i, 0, 0)), pl.BlockSpec((lp, 1024), lambda i, j, k: (0, j)),
                pl.BlockSpec((1024, 1024), half_rows), _sds((D, D), BF16), (2, 2, 1), dep=ffn_started)
    wout = wout_g.reshape(D, D)
    dmix = _mm("dmix" + tag, "nt", dh1b, wout,
               pl.BlockSpec((th, D), lambda i, j, k: (i, 0)), pl.BlockSpec((1024, D), lambda i, j, k: (j, 0)),
               pl.BlockSpec((th, 1024), half_rows), _sds((lp, D), F32), (2, 2, 1), dep=ffn_started)
    dht, de, dmnw, dcw = _mix_bwd("mix_bwd" + tag, dmix, ht, e, mnw, cw)
    dq, dk, dv, dgrow, dfx = _mlstm_bwd("mlstm_bwd" + tag, qkv, grow, mcol, ccol, ht, dht, 8)
    dgp, dgpb, dbias = _gate_bwd("gate_bwd" + tag, gpre, bias, dgrow, dfx)
    del dgp
    dqkv = jnp.concatenate([dq, dk, dv], axis=1)

    hn_cols = pl.BlockSpec((lp, 1024), lambda i, j, k: (0, j))
    dwqkv = _mm("dw_qkv" + tag, "tn", dqkv, hn, pl.BlockSpec((lp, 1024), lambda i, j, k: (0, i)), hn_cols,
                pl.BlockSpec((1024, 1024), half_rows), _sds((2048, D), BF16), (2, 2, 1))
    dwe = _mm("dw_e" + tag, "tn", de, hn, pl.BlockSpec((None, lp, 1024), lambda i, j, k: (i, 0, 0)), hn_cols,
              pl.BlockSpec((None, 1024, 1024), lambda i, j, k: (i, 0, j)), _sds((4, 1024, D), BF16), (4, 2, 1))
    dwgt = _mm("dw_gates" + tag, "tn", dgpb, hn, pl.BlockSpec((lp, LANE), lambda i, j, k: (0, 0)), hn_cols,
               pl.BlockSpec((LANE, 1024), lambda i, j, k: (0, j)), _sds((LANE, D), BF16), (1, 2, 1))
    dwin = _in_grads(dwqkv, dwe, dwgt)

    dhn = _mm("dhn_qkv" + tag, "nn", dqkv, wqkv,
              pl.BlockSpec((th, 2048), lambda i, j, k: (i, 0)), pl.BlockSpec((2048, 1024), lambda i, j, k: (0, j)),
              pl.BlockSpec((th, 1024), half_rows), _sds((lp, D), F32), (2, 2, 1))
    dhn = _mm("dhn_e" + tag, "nn", de, we,
              pl.BlockSpec((None, th, 1024), lambda i, j, k: (k, i, 0)),
              pl.BlockSpec((None, 1024, 1024), lambda i, j, k: (k, 0, j)),
              pl.BlockSpec((th, 1024), half_rows), _sds((lp, D), F32), (2, 2, 4), acc_shape=(th, 1024),
              res=dhn, res_spec=pl.BlockSpec((th, 1024), half_rows))
    dhn = _mm("dhn_gates" + tag, "nn", dgpb, wgt,
              pl.BlockSpec((th, LANE), lambda i, j, k: (i, 0)), pl.BlockSpec((LANE, 1024), lambda i, j, k: (0, j)),
              pl.BlockSpec((th, 1024), half_rows), _sds((lp, D), F32), (2, 2, 1),
              res=dhn, res_spec=pl.BlockSpec((th, 1024), half_rows))
    dh0, dh0b, dnmw = _norm_bwd("norm_mix_bwd" + tag, dhn, h, nmw, dh1)

    pieces = [dwin, dwout.reshape(NCHIP, D // NCHIP, D)]
    smalls = (dnmw[0], dbias[0, :8], dcw[:3], dmnw[0], dnfw[0])
    return dh0, dh0b, pieces, smalls


def _pack_rows(parts):
    rows = []
    for p in parts:
        f = p.reshape(-1)
        pad = (-f.shape[0]) % LANE
        if pad:
            f = jnp.pad(f, (0, pad))
        rows.append(f.reshape(-1, LANE))
    r = jnp.concatenate(rows, axis=0)
    pad = (-r.shape[0]) % 8
    if pad:
        r = jnp.pad(r, ((0, pad), (0, 0)))
    return r


def _unpack_rows(pack, shapes):
    out, r0 = [], 0
    for s in shapes:
        n = 1
        for d in s:
            n *= d
        nr = -(-n // LANE)
        out.append(pack[r0:r0 + nr].reshape(-1)[:n].reshape(s))
        r0 += nr
    return out


def kernel(x, meta_tokens, norm_mix_w, w_in, b_gates, conv_w, mlstm_norm_w, w_out, norm_ffn_w, w_gate, w_up, w_down, norm_final_w, loss_target, m_meta_tokens, m_norm_mix_w, m_w_in, m_b_gates, m_conv_w, m_mlstm_norm_w, m_w_out, m_norm_ffn_w, m_w_gate, m_w_up, m_w_down, m_norm_final_w, v_meta_tokens, v_norm_mix_w, v_w_in, v_b_gates, v_conv_w, v_mlstm_norm_w, v_w_out, v_norm_ffn_w, v_w_gate, v_w_up, v_w_down, v_norm_final_w):
    seq = x.shape[1]
    n_real = N_META + seq
    lp = -(-n_real // LANE) * LANE
    xi, yi, ci = lax.axis_index("x"), lax.axis_index("y"), lax.axis_index("c")
    jchip = 2 * xi + yi
    pf = jnp.stack([ci, jchip, 2 * (1 - xi) + yi, 2 * xi + (1 - yi), 2 * (1 - xi) + (1 - yi)]).astype(jnp.int32)

    big = {"w_in": w_in, "w_out": w_out, "w_gate": w_gate, "w_up": w_up, "w_down": w_down}
    cast = {n: [_cast_into("cast_%s_l%d" % (n, l), w, l, pf) for l in range(DEPTH)] for n, w in big.items()
            if n != "w_in"}
    in_t = lambda a: jnp.transpose(a, (2, 0, 1))
    cast["w_in"] = [_cast_into_t("cast_w_in_l%d" % l, in_t(w_in), l, pf) for l in range(DEPTH)]
    conv_flat = jnp.pad(conv_w.reshape(DEPTH * 3, CW // NCHIP), ((0, 8 - DEPTH * 3), (0, 0)))

    def own_slot(a):
        return lax.dynamic_update_slice(jnp.zeros((NCHIP,) + a.shape, a.dtype), a[None], (jchip, 0, 0))

    groups, splits = [], []
    for l in range(DEPTH):
        groups.append([cast["w_in"][l]] + ([own_slot(meta_tokens), own_slot(conv_flat)] if l == 0 else []))
        splits.append([2] + ([0, 0] if l == 0 else []))
        groups += [[cast["w_out"][l]], [cast["w_gate"][l], cast["w_up"][l]], [cast["w_down"][l]]]
        splits += [[1], [1, 1], [1]]
    group_of = {"in": 0, "out": 1, "ffn": 2, "down": 3}
    sems, arrs, _ = _ag_start("ag_start_0", groups[:1], splits[:1])
    sems_rest, arrs_rest, all_started = _ag_start("ag_start_1", groups[1:], splits[1:])
    sems, arrs = sems + sems_rest, arrs + arrs_rest

    def gathered(gi, afters):
        got = _ag_wait("ag_wait_%d" % gi, arrs[gi], splits[gi], sems[gi][0], sems[gi][1], afters)
        axes = [s for s in splits[gi] if s]
        return list(_ag_forward("ag_forward_%d" % gi, got[:len(axes)], axes)) + list(got[len(axes):])

    win0_g, meta_g, conv_g = gathered(0, [all_started])
    meta_full = jnp.concatenate([meta_g[s] for s in range(NCHIP)], axis=1)
    conv_full = jnp.concatenate([conv_g[s][:DEPTH * 3] for s in range(NCHIP)], axis=1)
    conv_full = conv_full.reshape(DEPTH, 3, CW)

    bias_rows = jnp.pad(b_gates, ((0, 0), (0, LANE - 8)))
    smalls = []
    for l in range(DEPTH):
        smalls.append((norm_mix_w[l][None], bias_rows[l][None], mlstm_norm_w[l][None],
                       jnp.pad(conv_full[l], ((0, 5), (0, 0))), norm_ffn_w[l][None]))

    h = jnp.concatenate([meta_full, x[0], jnp.zeros((lp - n_real, D), F32)], axis=0)
    saved, wts = [], []
    for l in range(DEPTH):
        def get(which, after, l=l):
            if l == 0 and which == "in":
                return [win0_g]
            return gathered(4 * l + group_of[which], [after])

        h, sv, wt = _layer_fwd(l, h, get, smalls[l])
        saved.append(sv)
        wts.append(wt)
    tgt = jnp.pad(loss_target[0], ((N_META, lp - n_real), (0, 0)))
    dh, dhb, dnorm_final, loss_part = _loss_head(h, tgt, norm_final_w[None], n_real)

    names = ["w_in", "w_out", "w_gate", "w_up", "w_down"]
    params = {"w_in": (w_in, m_w_in, v_w_in), "w_out": (w_out, m_w_out, v_w_out), "w_gate": (w_gate, m_w_gate, v_w_gate),
              "w_up": (w_up, m_w_up, v_w_up), "w_down": (w_down, m_w_down, v_w_down)}
    big_out = {n: None for n in names}
    small_grads = [None] * DEPTH
    g_in = [None] * DEPTH

    def finish(l, group, handle, afters):
        for n, (g_mine, g_theirs) in zip(group, _rs_end(handle, afters, pf)):
            if n == "w_in":
                g_in[l] = (g_mine, g_theirs)
                continue
            w, m, v = params[n]
            big_out[n] = _adamw_layer("adamw_%s_l%d" % (n, l), l, g_mine, g_theirs, w, m, v, big_out[n], pf)
        return [big_out[n][3] for n in group if n != "w_in"]

    groups = []
    token = None
    for l in reversed(range(DEPTH)):
        def ffn_done(pieces, l=l):
            handle, tok = _rs_begin("l%df" % l, pieces, [False] * 3, pf)
            groups.append((l, names[2:], handle))
            return tok

        dh, dhb, pieces, small_grads[l] = _layer_bwd(l, dh, dhb, saved[l], wts[l], smalls[l], ffn_done, token)
        handle, token = _rs_begin("l%dm" % l, pieces, [True, False], pf)
        groups.append((l, names[:2], handle))
    afters = [token]
    for l, group, handle in groups:
        afters = finish(l, group, handle, afters)
    big_out["w_in"] = [jnp.transpose(a, (1, 2, 0))
                       for a in _adamw_t("adamw_w_in", g_in, in_t(w_in), in_t(m_w_in), in_t(v_w_in), pf)]

    dnmw = jnp.stack([small_grads[l][0] for l in range(DEPTH)])
    dbias = jnp.stack([small_grads[l][1] for l in range(DEPTH)])
    dconv = jnp.stack([small_grads[l][2] for l in range(DEPTH)])
    dmnw = jnp.stack([small_grads[l][3] for l in range(DEPTH)])
    dnfw = jnp.stack([small_grads[l][4] for l in range(DEPTH)])
    part_shapes = [(N_META, D), (DEPTH, D), (DEPTH, 8), (DEPTH, 3, CW), (DEPTH, MW), (DEPTH, D), (D,), (LANE,)]
    pack = _pack_rows([dh[:N_META], dnmw, dbias, dconv, dmnw, dnfw, dnorm_final[0], loss_part[0]])
    tot = _unpack_rows(_allreduce_small(pack), part_shapes)
    g_meta_full, g_nmw, g_bias, g_conv_full, g_mnw, g_nfw, g_final, loss_row = tot
    mcols = D // NCHIP
    ccols = CW // NCHIP
    g_meta = lax.dynamic_slice_in_dim(g_meta_full, jchip * mcols, mcols, axis=1)
    g_conv = lax.dynamic_slice_in_dim(g_conv_full, jchip * ccols, ccols, axis=2)
    sm_g = [g_meta, g_nmw, g_bias, g_conv, g_mnw, g_nfw, g_final]
    sm_w = [meta_tokens, norm_mix_w, b_gates, conv_w, mlstm_norm_w, norm_ffn_w, norm_final_w]
    sm_m = [m_meta_tokens, m_norm_mix_w, m_b_gates, m_conv_w, m_mlstm_norm_w, m_norm_ffn_w, m_norm_final_w]
    sm_v = [v_meta_tokens, v_norm_mix_w, v_b_gates, v_conv_w, v_mlstm_norm_w, v_norm_ffn_w, v_norm_final_w]
    sm_shapes = [w.shape for w in sm_w]
    d_p, m_p, v_p = _adamw_flat(_pack_rows(sm_g), _pack_rows(sm_w), _pack_rows(sm_m), _pack_rows(sm_v))
    sm_d = _unpack_rows(d_p, sm_shapes)
    sm_nm = _unpack_rows(m_p, sm_shapes)
    sm_nv = _unpack_rows(v_p, sm_shapes)

    loss = loss_row[0]
    grad_x = dh[N_META:n_real][None]

    def ordered(sm, which):
        bo = {n: big_out[n][which] for n in names}
        return [sm[0], sm[1], bo["w_in"], sm[2], sm[3], sm[4], bo["w_out"], sm[5], bo["w_gate"], bo["w_up"], bo["w_down"], sm[6]]

    return (loss, grad_x, *ordered(sm_g, 0), *ordered(sm_d, 1), *ordered(sm_nm, 2), *ordered(sm_nv, 3))
```

```python
import functools

import jax
import jax.numpy as jnp
from jax import lax
from jax.experimental import pallas as pl
from jax.experimental.pallas import tpu as pltpu

F32 = jnp.float32
BF16 = jnp.bfloat16

D = 2048
N_META = 16
HEADS = 4
DQK = 128
DV = 256
MW = HEADS * DV
CW = D - MW
QKW = HEADS * DQK
DFF = 5632
DIN = 6152
NCHIP = 4
DEPTH = 2
GATE_CAP = 15.0
EPS = 1e-6
QK_SCALE = DQK ** -0.5
LANE = 128
VMEM_LIMIT = 60 * 1024 * 1024

ADAM_LR = 0.001
ADAM_B1 = 0.9
ADAM_B2 = 0.999
ADAM_EPS = 1e-08
ADAM_WD = 0.01
ADAM_STEP = 10

MESH = pl.DeviceIdType.MESH
ANY = pl.BlockSpec(memory_space=pl.ANY)


def _cp(sem):
    return pltpu.CompilerParams(dimension_semantics=sem, vmem_limit_bytes=VMEM_LIMIT)


def _sds(shape, dtype):
    return jax.ShapeDtypeStruct(shape, dtype)


_DN = {"nn": (((1,), (0,)), ((), ())), "nt": (((1,), (1,)), ((), ())), "tn": (((0,), (0,)), ((), ()))}


def _mm(name, kind, a, b, a_spec, b_spec, o_spec, out_shape, grid, acc_shape=None, res=None, res_spec=None,
        dep=None):
    nk = grid[2]
    has_res = res is not None
    n_in = 2 + has_res + (dep is not None)

    def body(*refs):
        a_ref, b_ref = refs[0], refs[1]
        r_ref = refs[2] if has_res else None
        o_ref = refs[n_in]
        p = lax.dot_general(a_ref[...], b_ref[...], _DN[kind], preferred_element_type=F32)
        if nk == 1:
            if r_ref is not None:
                p = p + r_ref[...]
            o_ref[...] = p.astype(o_ref.dtype)
        else:
            acc = refs[-1]
            k = pl.program_id(2)

            @pl.when(k == 0)
            def _():
                acc[...] = p

            @pl.when(jnp.logical_and(k > 0, k < nk - 1))
            def _():
                acc[...] += p

            @pl.when(k == nk - 1)
            def _():
                r = acc[...] + p
                if r_ref is not None:
                    r = r + r_ref[...]
                o_ref[...] = r.astype(o_ref.dtype)

    ins = [a, b] + ([res] if has_res else []) + ([dep] if dep is not None else [])
    in_specs = [a_spec, b_spec] + ([res_spec] if has_res else []) + ([ANY] if dep is not None else [])
    scratch = [pltpu.VMEM(acc_shape, F32)] if nk > 1 else []
    return pl.pallas_call(
        body, name=name, grid=grid, in_specs=in_specs, out_specs=o_spec, out_shape=out_shape,
        scratch_shapes=scratch, compiler_params=_cp(("parallel", "parallel", "arbitrary")))(*ins)


def _norm_fwd(name, h, w):
    lp = h.shape[0]
    tm = lp // 4

    def body(h_ref, w_ref, o_ref):
        x = h_ref[...]
        r = lax.rsqrt(jnp.mean(x * x, axis=1, keepdims=True) + EPS)
        o_ref[...] = (x * r * w_ref[...]).astype(BF16)

    return pl.pallas_call(
        body, name=name, grid=(4,),
        in_specs=[pl.BlockSpec((tm, D), lambda i: (i, 0)), pl.BlockSpec((1, D), lambda i: (0, 0))],
        out_specs=pl.BlockSpec((tm, D), lambda i: (i, 0)), out_shape=_sds((lp, D), BF16),
        compiler_params=_cp(("parallel",)))(h, w)


def _norm_bwd(name, dy, h, w, dres):
    lp = h.shape[0]
    tm = lp // 8

    def body(dy_ref, h_ref, w_ref, dres_ref, dh_ref, dhb_ref, dw_ref):
        x = h_ref[...]
        r = lax.rsqrt(jnp.mean(x * x, axis=1, keepdims=True) + EPS)
        xh = x * r
        dy_v = dy_ref[...]
        dxh = dy_v * w_ref[...]
        dx = r * (dxh - xh * jnp.mean(dxh * xh, axis=1, keepdims=True))
        dh = dres_ref[...] + dx
        dh_ref[...] = dh
        dhb_ref[...] = dh.astype(BF16)

        @pl.when(pl.program_id(0) == 0)
        def _():
            dw_ref[...] = jnp.zeros_like(dw_ref)

        dw_ref[0:1, :] += jnp.sum(dy_v * xh, axis=0, keepdims=True)

    row = pl.BlockSpec((tm, D), lambda i: (i, 0))
    return pl.pallas_call(
        body, name=name, grid=(8,),
        in_specs=[row, row, pl.BlockSpec((1, D), lambda i: (0, 0)), row],
        out_specs=[row, row, pl.BlockSpec((8, D), lambda i: (0, 0))],
        out_shape=[_sds((lp, D), F32), _sds((lp, D), BF16), _sds((8, D), F32)],
        compiler_params=_cp(("arbitrary",)))(dy, h, w, dres)


def _loss_head(h, tgt, w, n_real):
    lp = h.shape[0]
    tm = lp // 8

    def body(h_ref, t_ref, w_ref, dh_ref, dhb_ref, dw_ref, loss_ref):
        i = pl.program_id(0)
        x = h_ref[...]
        r = lax.rsqrt(jnp.mean(x * x, axis=1, keepdims=True) + EPS)
        xh = x * r
        wv = w_ref[...]
        row = i * tm + lax.broadcasted_iota(jnp.int32, (tm, 1), 0)
        valid = jnp.logical_and(row >= N_META, row < n_real)
        err = jnp.where(valid, xh * wv - t_ref[...], 0.0)
        dy_v = err * (1.0 / D)
        dxh = dy_v * wv
        dx = r * (dxh - xh * jnp.mean(dxh * xh, axis=1, keepdims=True))
        dh_ref[...] = dx
        dhb_ref[...] = dx.astype(BF16)

        @pl.when(i == 0)
        def _():
            dw_ref[...] = jnp.zeros_like(dw_ref)
            loss_ref[...] = jnp.zeros_like(loss_ref)

        dw_ref[0:1, :] += jnp.sum(dy_v * xh, axis=0, keepdims=True)
        part = jnp.sum(jnp.sum(err * err, axis=1, keepdims=True), axis=0, keepdims=True) * (0.5 / D)
        loss_ref[...] += jnp.broadcast_to(part, loss_ref.shape)

    row = pl.BlockSpec((tm, D), lambda i: (i, 0))
    return pl.pallas_call(
        body, name="loss_head", grid=(8,),
        in_specs=[row, row, pl.BlockSpec((1, D), lambda i: (0, 0))],
        out_specs=[row, row, pl.BlockSpec((8, D), lambda i: (0, 0)), pl.BlockSpec((8, LANE), lambda i: (0, 0))],
        out_shape=[_sds((lp, D), F32), _sds((lp, D), BF16), _sds((8, D), F32), _sds((8, LANE), F32)],
        compiler_params=_cp(("arbitrary",)))(h, tgt, w)


def _ffn_up(name, hf, wg4, wu4):
    lp = hf.shape[0]
    tm = lp // 4
    tn = DFF // NCHIP

    def body(x_ref, wg_ref, wu_ref, g_ref, u_ref, a_ref):
        x = x_ref[...]
        g = jnp.dot(x, wg_ref[...], preferred_element_type=F32)
        u = jnp.dot(x, wu_ref[...], preferred_element_type=F32)
        g_ref[...] = g.astype(BF16)
        u_ref[...] = u.astype(BF16)
        a_ref[...] = (g * jax.nn.sigmoid(g) * u).astype(BF16)

    wspec = pl.BlockSpec((None, D, tn), lambda j, i: (j, 0, 0))
    ospec = pl.BlockSpec((tm, tn), lambda j, i: (i, j))
    o = _sds((lp, DFF), BF16)
    return pl.pallas_call(
        body, name=name, grid=(NCHIP, 4),
        in_specs=[pl.BlockSpec((tm, D), lambda j, i: (i, 0)), wspec, wspec],
        out_specs=[ospec, ospec, ospec], out_shape=[o, o, o],
        compiler_params=_cp(("parallel", "parallel")))(hf, wg4, wu4)


def _ffn_bwd_act(name, dhb, wd4, g, u, dep=None):
    lp = dhb.shape[0]
    tm = lp // 4
    tn = DFF // NCHIP
    deps = [] if dep is None else [dep]

    def body(d_ref, w_ref, g_ref, u_ref, *rest):
        dg_ref, du_ref = rest[len(deps):]
        da = lax.dot_general(d_ref[...], w_ref[...], _DN["nt"], preferred_element_type=F32)
        gv = g_ref[...].astype(F32)
        uv = u_ref[...].astype(F32)
        sg = jax.nn.sigmoid(gv)
        dg_ref[...] = (da * uv * (sg * (1.0 + gv * (1.0 - sg)))).astype(BF16)
        du_ref[...] = (da * (gv * sg)).astype(BF16)

    ospec = pl.BlockSpec((tm, tn), lambda j, i: (i, j))
    o = _sds((lp, DFF), BF16)
    return pl.pallas_call(
        body, name=name, grid=(NCHIP, 4),
        in_specs=[pl.BlockSpec((tm, D), lambda j, i: (i, 0)),
                  pl.BlockSpec((None, tn, D), lambda j, i: (j, 0, 0)), ospec, ospec] + [ANY] * len(deps),
        out_specs=[ospec, ospec], out_shape=[o, o],
        compiler_params=_cp(("parallel", "parallel")))(dhb, wd4, g, u, *deps)


def _ffn_bwd_in(name, dg, du, wg4, wu4):
    lp = dg.shape[0]
    th = lp // 2
    tk = DFF // NCHIP

    def body(dg_ref, du_ref, wg_ref, wu_ref, o_ref, acc):
        k = pl.program_id(2)
        p = (lax.dot_general(dg_ref[...], wg_ref[...], _DN["nt"], preferred_element_type=F32)
             + lax.dot_general(du_ref[...], wu_ref[...], _DN["nt"], preferred_element_type=F32))

        @pl.when(k == 0)
        def _():
            acc[...] = p

        @pl.when(jnp.logical_and(k > 0, k < NCHIP - 1))
        def _():
            acc[...] += p

        @pl.when(k == NCHIP - 1)
        def _():
            o_ref[...] = acc[...] + p

    a_spec = pl.BlockSpec((th, tk), lambda i, j, k: (i, k))
    w_spec = pl.BlockSpec((None, 1024, tk), lambda i, j, k: (k, j, 0))
    return pl.pallas_call(
        body, name=name, grid=(2, 2, NCHIP), in_specs=[a_spec, a_spec, w_spec, w_spec],
        out_specs=pl.BlockSpec((th, 1024), lambda i, j, k: (i, j)), out_shape=_sds((lp, D), F32),
        scratch_shapes=[pltpu.VMEM((th, 1024), F32)],
        compiler_params=_cp(("parallel", "parallel", "arbitrary")))(dg, du, wg4, wu4)


def _shift_rows(x, d, row):
    return jnp.where(row >= d, pltpu.roll(x, d, axis=0), 0.0)


def _scan_steps(lp):
    d = 1
    while d < lp:
        yield d
        d *= 2


def _gate_values(pre):
    t = GATE_CAP * jnp.tanh(pre * (1.0 / GATE_CAP))
    lf = jnp.minimum(t, 0.0) - jnp.log(1.0 + jnp.exp(-jnp.abs(t)))
    return t, lf


def _gate_prep(name, gates_pre, bias):
    lp = gates_pre.shape[0]

    def body(p_ref, b_ref, grow_ref, m_ref, c_ref):
        pre = p_ref[...] + b_ref[...]
        lane = lax.broadcasted_iota(jnp.int32, (lp, LANE), 1)
        row = lax.broadcasted_iota(jnp.int32, (lp, LANE), 0)
        t, lf = _gate_values(pre)
        f = jnp.where(jnp.logical_and(lane >= HEADS, lane < 2 * HEADS), lf, 0.0)
        for d in _scan_steps(lp):
            f = f + _shift_rows(f, d, row)
        fs = pltpu.roll(f, LANE - HEADS, axis=1)
        g = jnp.where(lane < HEADS, t - fs, 0.0)
        m = g
        for d in _scan_steps(lp):
            m = jnp.maximum(m, jnp.where(row >= d, pltpu.roll(m, d, axis=0), m))
        grow_ref[...] = g.T
        m_ref[...] = m
        c_ref[...] = jnp.where(lane < HEADS, -fs - m, 0.0)

    full = pl.BlockSpec((lp, LANE), lambda: (0, 0))
    return pl.pallas_call(
        body, name=name, in_specs=[full, pl.BlockSpec((1, LANE), lambda: (0, 0))],
        out_specs=[pl.BlockSpec((LANE, lp), lambda: (0, 0)), full, full],
        out_shape=[_sds((LANE, lp), F32), _sds((lp, LANE), F32), _sds((lp, LANE), F32)],
        compiler_params=pltpu.CompilerParams(vmem_limit_bytes=VMEM_LIMIT))(gates_pre, bias)


def _pick_lane(blk, h):
    lane = lax.broadcasted_iota(jnp.int32, blk.shape, 1)
    return jnp.sum(jnp.where(lane == h, blk, 0.0), axis=1, keepdims=True)


def _mlstm_weights(q, k, grow, mcol, i, bq, nk):
    s = lax.dot_general(q, k, _DN["nt"], preferred_element_type=F32) * QK_SCALE
    row = i * bq + lax.broadcasted_iota(jnp.int32, (bq, 1), 0)
    col = lax.broadcasted_iota(jnp.int32, (1, nk), 1)
    a = jnp.where(col <= row, jnp.exp(jnp.minimum(grow - mcol, 0.0)), 0.0)
    return s, a


def _per_query_tile(i, nq, bq, lp, compute):
    for ii in range(nq):
        nk = min(lp, -(-((ii + 1) * bq) // LANE) * LANE)
        pl.when(i == ii)(functools.partial(compute, nk))


def _mlstm_fwd(name, qkv, grow, mcol_all, ccol_all, nq):
    lp = qkv.shape[0]
    bq = lp // nq

    def body(q_ref, k_ref, v_ref, grow_ref, m_ref, c_ref, o_ref):
        h = pl.program_id(0)
        i = pl.program_id(1)
        mcol = _pick_lane(m_ref[...], h)
        ccol = _pick_lane(c_ref[...], h)

        def compute(nk):
            grow_h = grow_ref[pl.ds(h, 1), 0:nk]
            s, a = _mlstm_weights(q_ref[...], k_ref[0:nk, :], grow_h, mcol, i, bq, nk)
            p = a * s
            den = jnp.sum(p, axis=1, keepdims=True)
            num = jnp.dot(p.astype(BF16), v_ref[0:nk, :], preferred_element_type=F32)
            o_ref[...] = num / jnp.maximum(jnp.abs(den), jnp.exp(ccol))

        _per_query_tile(i, nq, bq, lp, compute)

    return pl.pallas_call(
        body, name=name, grid=(HEADS, nq),
        in_specs=[pl.BlockSpec((bq, DQK), lambda h, i: (i, h)),
                  pl.BlockSpec((lp, DQK), lambda h, i: (0, HEADS + h)),
                  pl.BlockSpec((lp, DV), lambda h, i: (0, HEADS + h)),
                  pl.BlockSpec((8, lp), lambda h, i: (0, 0)),
                  pl.BlockSpec((bq, LANE), lambda h, i: (i, 0)),
                  pl.BlockSpec((bq, LANE), lambda h, i: (i, 0))],
        out_specs=pl.BlockSpec((bq, DV), lambda h, i: (i, h)),
        out_shape=_sds((lp, MW), F32),
        compiler_params=_cp(("parallel", "parallel")))(qkv, qkv, qkv, grow, mcol_all, ccol_all)


def _mlstm_bwd(name, qkv, grow, mcol_all, ccol_all, ht, dht, nq):
    lp = qkv.shape[0]
    bq = lp // nq

    def body(q_ref, k_ref, v_ref, grow_ref, m_ref, c_ref, ht_ref, dht_ref,
             dq_ref, dk_ref, dv_ref, dgrow_ref, dfx_ref, dkt_acc, dvt_acc):
        h = pl.program_id(0)
        i = pl.program_id(1)

        @pl.when(jnp.logical_and(h == 0, i == 0))
        def _():
            dgrow_ref[...] = jnp.zeros_like(dgrow_ref)
            dfx_ref[...] = jnp.zeros_like(dfx_ref)

        @pl.when(i == 0)
        def _():
            dkt_acc[...] = jnp.zeros_like(dkt_acc)
            dvt_acc[...] = jnp.zeros_like(dvt_acc)

        mcol = _pick_lane(m_ref[...], h)
        ccol = _pick_lane(c_ref[...], h)

        def compute(nk):
            q = q_ref[...]
            k = k_ref[0:nk, :]
            v = v_ref[0:nk, :]
            grow_h = grow_ref[pl.ds(h, 1), 0:nk]
            s, a = _mlstm_weights(q, k, grow_h, mcol, i, bq, nk)
            p = a * s
            den = jnp.sum(p, axis=1, keepdims=True)
            clamp = jnp.exp(ccol)
            active = jnp.abs(den) < clamp
            dd = jnp.maximum(jnp.abs(den), clamp)
            dht_v = dht_ref[...]
            hdh = jnp.sum(dht_v * ht_ref[...], axis=1, keepdims=True)
            dn = (dht_v / dd).astype(BF16)
            dden = jnp.where(active, 0.0, -(hdh / dd) * jnp.sign(den))
            dp = lax.dot_general(dn, v, _DN["nt"], preferred_element_type=F32) + dden
            rmat = dp * p
            dgrow_ref[pl.ds(h, 1), 0:nk] += jnp.sum(rmat, axis=0, keepdims=True)
            ds = (dp * a * QK_SCALE).astype(BF16)
            dq_ref[...] = jnp.dot(ds, k, preferred_element_type=F32).astype(BF16)
            dkt_acc[:, 0:nk] += lax.dot_general(q, ds, _DN["tn"], preferred_element_type=F32)
            dvt_acc[:, 0:nk] += lax.dot_general(dn, p.astype(BF16), _DN["tn"], preferred_element_type=F32)
            lane = lax.broadcasted_iota(jnp.int32, (bq, LANE), 1)
            r0 = pl.multiple_of(i * bq, 16)
            dfx_ref[pl.ds(r0, bq), :] += jnp.where(lane == h, jnp.sum(rmat, axis=1, keepdims=True), 0.0)

        _per_query_tile(i, nq, bq, lp, compute)

        @pl.when(i == nq - 1)
        def _():
            dk_ref[...] = dkt_acc[...].T.astype(BF16)
            dv_ref[...] = dvt_acc[...].T.astype(BF16)

    return pl.pallas_call(
        body, name=name, grid=(HEADS, nq),
        in_specs=[pl.BlockSpec((bq, DQK), lambda h, i: (i, h)),
                  pl.BlockSpec((lp, DQK), lambda h, i: (0, HEADS + h)),
                  pl.BlockSpec((lp, DV), lambda h, i: (0, HEADS + h)),
                  pl.BlockSpec((8, lp), lambda h, i: (0, 0)),
                  pl.BlockSpec((bq, LANE), lambda h, i: (i, 0)),
                  pl.BlockSpec((bq, LANE), lambda h, i: (i, 0)),
                  pl.BlockSpec((bq, DV), lambda h, i: (i, h)),
                  pl.BlockSpec((bq, DV), lambda h, i: (i, h))],
        out_specs=[pl.BlockSpec((bq, DQK), lambda h, i: (i, h)),
                   pl.BlockSpec((lp, DQK), lambda h, i: (0, h)),
                   pl.BlockSpec((lp, DV), lambda h, i: (0, h)),
                   pl.BlockSpec((LANE, lp), lambda h, i: (0, 0)),
                   pl.BlockSpec((lp, LANE), lambda h, i: (0, 0))],
        out_shape=[_sds((lp, QKW), BF16), _sds((lp, QKW), BF16), _sds((lp, MW), BF16),
                   _sds((LANE, lp), F32), _sds((lp, LANE), F32)],
        scratch_shapes=[pltpu.VMEM((DQK, lp), F32), pltpu.VMEM((DV, lp), F32)],
        compiler_params=_cp(("arbitrary", "arbitrary")))(qkv, qkv, qkv, grow, mcol_all, ccol_all, ht, dht)


def _gate_bwd(name, gates_pre, bias, dgrow, dfx):
    lp = gates_pre.shape[0]

    def body(p_ref, b_ref, dgrow_ref, dfx_ref, dg_ref, dgb_ref, db_ref):
        pre = p_ref[...] + b_ref[...]
        lane = lax.broadcasted_iota(jnp.int32, (lp, LANE), 1)
        row = lax.broadcasted_iota(jnp.int32, (lp, LANE), 0)
        th = jnp.tanh(pre * (1.0 / GATE_CAP))
        t = GATE_CAP * th
        dgc = jnp.where(lane < HEADS, dgrow_ref[...].T, 0.0)
        df = jnp.where(lane < HEADS, dfx_ref[...] - dgc, 0.0)
        for d in _scan_steps(lp):
            df = df + jnp.where(row < lp - d, pltpu.roll(df, lp - d, axis=0), 0.0)
        dlf = pltpu.roll(df, HEADS, axis=1)
        dt = jnp.where(lane < HEADS, dgc, dlf * jax.nn.sigmoid(-t))
        dpre = jnp.where(lane < 2 * HEADS, dt * (1.0 - th * th), 0.0)
        dg_ref[...] = dpre
        dgb_ref[...] = dpre.astype(BF16)
        db_ref[...] = jnp.broadcast_to(jnp.sum(dpre, axis=0, keepdims=True), db_ref.shape)

    full = pl.BlockSpec((lp, LANE), lambda: (0, 0))
    return pl.pallas_call(
        body, name=name,
        in_specs=[full, pl.BlockSpec((1, LANE), lambda: (0, 0)), pl.BlockSpec((LANE, lp), lambda: (0, 0)), full],
        out_specs=[full, full, pl.BlockSpec((8, LANE), lambda: (0, 0))],
        out_shape=[_sds((lp, LANE), F32), _sds((lp, LANE), BF16), _sds((8, LANE), F32)],
        compiler_params=pltpu.CompilerParams(vmem_limit_bytes=VMEM_LIMIT))(gates_pre, bias, dgrow, dfx)


CB = 256


def _e_specs(lp):
    return [pl.BlockSpec((None, lp, CB), functools.partial(lambda c, j: (c, 0, j), c)) for c in range(4)]


def _mix_fwd(name, ht, e, mnw, cw):
    lp = ht.shape[0]

    def body(ht_ref, og_ref, u_ref, gb_ref, gc_ref, mnw_ref, cw_ref, o_ref):
        x = ht_ref[...]
        r = lax.rsqrt(jnp.mean(x * x, axis=1, keepdims=True) + EPS)
        o_ref[0] = (jax.nn.sigmoid(og_ref[...].astype(F32)) * (x * r * mnw_ref[...])).astype(BF16)
        row = lax.broadcasted_iota(jnp.int32, (lp, CB), 0)
        a = gc_ref[...].astype(F32) * u_ref[...].astype(F32)
        conv = cw_ref[2:3, :] * a + cw_ref[1:2, :] * _shift_rows(a, 1, row) + cw_ref[0:1, :] * _shift_rows(a, 2, row)
        o_ref[1] = (gb_ref[...].astype(F32) * conv).astype(BF16)

    col = pl.BlockSpec((lp, CB), lambda j: (0, j))
    return pl.pallas_call(
        body, name=name, grid=(4,),
        in_specs=[col] + _e_specs(lp) + [pl.BlockSpec((1, CB), lambda j: (0, j)), pl.BlockSpec((8, CB), lambda j: (0, j))],
        out_specs=pl.BlockSpec((2, lp, CB), lambda j: (0, 0, j)), out_shape=_sds((2, lp, MW), BF16),
        compiler_params=_cp(("parallel",)))(ht, e, e, e, e, mnw, cw)


def _mix_bwd(name, dmix, ht, e, mnw, cw):
    lp = ht.shape[0]

    def body(dhm_ref, dhc_ref, ht_ref, og_ref, u_ref, gb_ref, gc_ref, mnw_ref, cw_ref,
             dht_ref, de_ref, dmnw_ref, dcw_ref):
        x = ht_ref[...]
        r = lax.rsqrt(jnp.mean(x * x, axis=1, keepdims=True) + EPS)
        xh = x * r
        w = mnw_ref[...]
        sg = jax.nn.sigmoid(og_ref[...].astype(F32))
        dhm = dhm_ref[...]
        de_ref[0] = (dhm * (xh * w) * (sg * (1.0 - sg))).astype(BF16)
        dn = dhm * sg
        dmnw_ref[...] = jnp.broadcast_to(jnp.sum(dn * xh, axis=0, keepdims=True), dmnw_ref.shape)
        dxh = dn * w
        dht_ref[...] = r * (dxh - xh * jnp.mean(dxh * xh, axis=1, keepdims=True))

        row = lax.broadcasted_iota(jnp.int32, (lp, CB), 0)
        uv = u_ref[...].astype(F32)
        gcv = gc_ref[...].astype(F32)
        gbv = gb_ref[...].astype(F32)
        a = gcv * uv
        a1 = _shift_rows(a, 1, row)
        a2 = _shift_rows(a, 2, row)
        dhc = dhc_ref[...]
        conv = cw_ref[2:3, :] * a + cw_ref[1:2, :] * a1 + cw_ref[0:1, :] * a2
        de_ref[2] = (dhc * conv).astype(BF16)
        dconv = dhc * gbv
        dcw_ref[...] = jnp.zeros_like(dcw_ref)
        dcw_ref[0:1, :] = jnp.sum(dconv * a2, axis=0, keepdims=True)
        dcw_ref[1:2, :] = jnp.sum(dconv * a1, axis=0, keepdims=True)
        dcw_ref[2:3, :] = jnp.sum(dconv * a, axis=0, keepdims=True)
        up1 = jnp.where(row < lp - 1, pltpu.roll(dconv, lp - 1, axis=0), 0.0)
        up2 = jnp.where(row < lp - 2, pltpu.roll(dconv, lp - 2, axis=0), 0.0)
        da = cw_ref[2:3, :] * dconv + cw_ref[1:2, :] * up1 + cw_ref[0:1, :] * up2
        de_ref[1] = (da * gcv).astype(BF16)
        de_ref[3] = (da * uv).astype(BF16)

    col = pl.BlockSpec((lp, CB), lambda j: (0, j))
    small = pl.BlockSpec((8, CB), lambda j: (0, j))
    return pl.pallas_call(
        body, name=name, grid=(4,),
        in_specs=[col, pl.BlockSpec((lp, CB), lambda j: (0, 4 + j)), col] + _e_specs(lp)
                 + [pl.BlockSpec((1, CB), lambda j: (0, j)), small],
        out_specs=[col, pl.BlockSpec((4, lp, CB), lambda j: (0, 0, j)), small, small],
        out_shape=[_sds((lp, MW), F32), _sds((4, lp, MW), BF16), _sds((8, MW), F32), _sds((8, CW), F32)],
        compiler_params=_cp(("parallel",)))(dmix, dmix, ht, e, e, e, e, mnw, cw)


def _row_tile(r, c, itemsize, budget=1536 * 1024, mult=16):
    best = None
    for t in range(mult, r + 1, mult):
        if r % t == 0 and t * c * itemsize <= budget:
            best = t
    if best is None:
        best = r
    return best


def _grid_spec(grid, in_specs, out_specs, scratch=()):
    return pltpu.PrefetchScalarGridSpec(num_scalar_prefetch=1, grid=grid, in_specs=in_specs,
                                        out_specs=out_specs, scratch_shapes=list(scratch))


def _cast_into(name, w, layer, pf):
    _, r, c = w.shape
    tr = _row_tile(r, c, 4)

    def body(pf_ref, x_ref, o_ref):
        o_ref[...] = x_ref[...].astype(BF16)

    return pl.pallas_call(
        body, name=name, out_shape=_sds((NCHIP, r, c), BF16),
        grid_spec=_grid_spec((r // tr,), [pl.BlockSpec((None, tr, c), lambda i, pf: (layer, i, 0))],
                             pl.BlockSpec((None, tr, c), lambda i, pf: (pf[1], i, 0))),
        compiler_params=_cp(("parallel",)))(pf, w)


TCOL = 256


def _cast_into_t(name, w_t, layer, pf):
    c, nl, r = w_t.shape

    def body(pf_ref, x_ref, o_ref):
        o_ref[...] = x_ref[:, layer, :].astype(BF16)

    return pl.pallas_call(
        body, name=name, out_shape=_sds((NCHIP, c, r), BF16),
        grid_spec=_grid_spec((r // TCOL,), [pl.BlockSpec((c, nl, TCOL), lambda i, pf: (0, 0, i))],
                             pl.BlockSpec((None, c, TCOL), lambda i, pf: (pf[1], 0, i))),
        compiler_params=_cp(("parallel",)))(pf, w_t)


def _add2_bf16(name, dw, got, pf, by_cols=False):
    n4, r2, c2 = got.shape

    def body(pf_ref, a_ref, b_ref, o_ref):
        o_ref[...] = (a_ref[...].astype(F32) + b_ref[...].astype(F32)).astype(BF16)

    if by_cols:
        nch = c2 // TCOL
        spec = pl.BlockSpec((None, r2, TCOL), lambda s, i, pf: (s, 0, i))
        mine = pl.BlockSpec((None, r2, TCOL), lambda s, i, pf: (s, 0, pf[0] * nch + i))
    else:
        tr = _row_tile(r2, c2, 4)
        nch = r2 // tr
        spec = pl.BlockSpec((None, tr, c2), lambda s, i, pf: (s, i, 0))
        mine = pl.BlockSpec((None, tr, c2), lambda s, i, pf: (s, pf[0] * nch + i, 0))
    return pl.pallas_call(
        body, name=name, out_shape=_sds((n4, r2, c2), BF16),
        grid_spec=_grid_spec((n4, nch), [mine, spec], spec),
        compiler_params=_cp(("parallel", "parallel")))(pf, dw, got)


def _adam_math(w, g, m, v):
    m2 = ADAM_B1 * m + (1.0 - ADAM_B1) * g
    v2 = ADAM_B2 * v + (1.0 - ADAM_B2) * (g * g)
    m_hat = m2 / (1.0 - ADAM_B1 ** ADAM_STEP)
    v_hat = v2 / (1.0 - ADAM_B2 ** ADAM_STEP)
    delta = -ADAM_LR * (m_hat / (jnp.sqrt(v_hat) + ADAM_EPS) + ADAM_WD * w)
    return delta, m2, v2


def _adamw_layer(name, layer, g_mine, g_theirs, w, m, v, prev, pf):
    _, r, c = w.shape
    r2 = r // 2
    tr = _row_tile(r2, c, 4, budget=1024 * 1024, mult=8)
    nch = r2 // tr
    n_alias = 0 if prev is None else 4

    def body(*refs):
        pf_ref, gm_ref, gt_ref, w_ref, m_ref, v_ref = refs[:6]
        go_ref, d_ref, mo_ref, vo_ref = refs[6 + n_alias:]
        mine = (pl.program_id(0) // nch) == pf_ref[0]
        gv = jnp.where(mine, gm_ref[...], gt_ref[...])
        delta, m2, v2 = _adam_math(w_ref[...], gv, m_ref[...], v_ref[...])
        go_ref[...] = gv
        d_ref[...] = delta
        mo_ref[...] = m2
        vo_ref[...] = v2

    slab = pl.BlockSpec((None, tr, c), lambda i, pf: (layer, i, 0))
    ins = [g_mine, g_theirs, w, m, v] + (list(prev) if prev is not None else [])
    in_specs = [pl.BlockSpec((tr, c), lambda i, pf: (jnp.clip(i - pf[0] * nch, 0, nch - 1), 0)),
                pl.BlockSpec((tr, c), lambda i, pf: (jnp.clip(i - (1 - pf[0]) * nch, 0, nch - 1), 0)),
                slab, slab, slab] + [ANY] * n_alias
    o = _sds(w.shape, F32)
    return pl.pallas_call(
        body, name=name, out_shape=[o] * 4, grid_spec=_grid_spec((2 * nch,), in_specs, [slab] * 4),
        input_output_aliases={6 + k: k for k in range(n_alias)},
        compiler_params=_cp(("parallel",)))(pf, *ins)


def _adamw_t(name, gs, w_t, m_t, v_t, pf):
    c, nl, r = w_t.shape
    ta = LANE
    nch = (r // 2) // ta

    def body(*refs):
        pf_ref = refs[0]
        g_refs = refs[1:1 + 2 * nl]
        w_ref, m_ref, v_ref, go_ref, d_ref, mo_ref, vo_ref = refs[1 + 2 * nl:]
        mine = (pl.program_id(0) // nch) == pf_ref[0]
        gv = jnp.stack([jnp.where(mine, g_refs[2 * l][...], g_refs[2 * l + 1][...]) for l in range(nl)], axis=1)
        delta, m2, v2 = _adam_math(w_ref[...], gv, m_ref[...], v_ref[...])
        go_ref[...] = gv
        d_ref[...] = delta
        mo_ref[...] = m2
        vo_ref[...] = v2

    both = pl.BlockSpec((c, nl, ta), lambda i, pf: (0, 0, i))
    g_specs = []
    for l in range(nl):
        g_specs += [pl.BlockSpec((c, ta), lambda i, pf: (0, jnp.clip(i - pf[0] * nch, 0, nch - 1))),
                    pl.BlockSpec((c, ta), lambda i, pf: (0, jnp.clip(i - (1 - pf[0]) * nch, 0, nch - 1)))]
    o = _sds(w_t.shape, F32)
    flat_g = [a for pair in gs for a in pair]
    return pl.pallas_call(
        body, name=name, out_shape=[o] * 4, grid_spec=_grid_spec((2 * nch,), g_specs + [both] * 3, [both] * 4),
        compiler_params=_cp(("parallel",)))(pf, *flat_g, w_t, m_t, v_t)


def _adamw_flat(g, w, m, v):
    def body(g_ref, w_ref, m_ref, v_ref, d_ref, mo_ref, vo_ref):
        delta, m2, v2 = _adam_math(w_ref[...], g_ref[...], m_ref[...], v_ref[...])
        d_ref[...] = delta
        mo_ref[...] = m2
        vo_ref[...] = v2

    o = _sds(w.shape, F32)
    return pl.pallas_call(body, name="adamw_small", out_shape=[o, o, o])(g, w, m, v)


def _place():
    x, y, c = lax.axis_index("x"), lax.axis_index("y"), lax.axis_index("c")
    chips = [(1 - x, y), (x, 1 - y), (1 - x, 1 - y)]
    return x, y, c, chips


HBM = pl.BlockSpec(memory_space=pltpu.HBM)
SEM = pl.BlockSpec(memory_space=pltpu.SEMAPHORE)
EFFECT = pltpu.SideEffectType.DATAFLOW_SIDE_EFFECTING


def _in_hbm(a):
    return pltpu.with_memory_space_constraint(a, pltpu.HBM)


def _rs_chips_copies(ins, lands, send, recv):
    x, y, c, chips = _place()
    cps = []
    for t in range(len(ins)):
        for k, chip in enumerate(chips):
            jk = 2 * chip[0] + chip[1]
            cps.append(pltpu.make_async_remote_copy(
                src_ref=ins[t].at[jk], dst_ref=lands[t].at[k], send_sem=send.at[3 * t + k],
                recv_sem=recv.at[3 * t + k], device_id=(*chip, c), device_id_type=MESH))
    return cps


def _rs_chips_start(name, ps):
    n = len(ps)

    def body(*refs):
        ins, lands = refs[:n], refs[n:2 * n]
        send, recv = refs[2 * n], refs[2 * n + 1]
        token = refs[-1]
        for cp in _rs_chips_copies(ins, lands, send, recv):
            cp.start()
        token[...] = jnp.zeros_like(token)

    dma = pltpu.SemaphoreType.DMA
    lands = [lax.empty((3,) + p.shape[1:], BF16) for p in ps]
    out_shape = ([dma((3 * n,)), dma((3 * n,))] + [pltpu.HBM(p.shape, BF16) for p in ps]
                 + [pltpu.HBM(z.shape, BF16) for z in lands] + [_sds((8, LANE), F32)])
    outs = pl.pallas_call(
        body, name=name, out_shape=out_shape, in_specs=[HBM] * (2 * n),
        out_specs=[SEM, SEM] + [HBM] * (2 * n) + [pl.BlockSpec(memory_space=pltpu.VMEM)],
        input_output_aliases={i: 2 + i for i in range(2 * n)},
        compiler_params=pltpu.CompilerParams(has_side_effects=EFFECT))(
            *[_in_hbm(p) for p in ps], *[_in_hbm(z) for z in lands])
    return outs[0], outs[1], outs[2:2 + n], outs[2 + n:2 + 2 * n], outs[-1]


def _rs_chips_wait(name, send, recv, ps, lands, afters):
    n = len(ps)

    def body(*refs):
        ins, zones = refs[:n], refs[n:2 * n]
        send_ref, recv_ref = refs[2 * n], refs[2 * n + 1]
        for cp in _rs_chips_copies(ins, zones, send_ref, recv_ref):
            cp.wait_send()
            cp.wait_recv()

    outs = pl.pallas_call(
        body, name=name, out_shape=[pltpu.HBM(p.shape, BF16) for p in ps] + [pltpu.HBM(z.shape, BF16) for z in lands],
        in_specs=[HBM] * (2 * n) + [SEM, SEM] + [ANY] * len(afters), out_specs=[HBM] * (2 * n),
        input_output_aliases={i: i for i in range(2 * n)},
        compiler_params=pltpu.CompilerParams(has_side_effects=EFFECT))(*ps, *lands, send, recv, *afters)
    return outs[:n], outs[n:]


def _sibling():
    x, y, c, _ = _place()
    return (x, y, 1 - c)


PAIR_ID = 0


def _pair_barrier():
    bar = pltpu.get_barrier_semaphore()
    pl.semaphore_signal(bar, inc=1, device_id=_sibling(), device_id_type=MESH)
    pl.semaphore_wait(bar, 1)


def _pair_send(name, dw, pf, by_cols=False):
    n4, r, c = dw.shape
    blk = (1, r, c // 2) if by_cols else (1, r // 2, c)
    idx = (lambda s, pf: (s, 0, 1 - pf[0])) if by_cols else (lambda s, pf: (s, 1 - pf[0], 0))

    def body(pf_ref, x_ref, got_ref, ssem, rsem):
        s = pl.program_id(0)
        pl.when(s == 0)(_pair_barrier)
        cp = pltpu.make_async_remote_copy(src_ref=x_ref, dst_ref=got_ref.at[pl.ds(s, 1)], send_sem=ssem,
                                          recv_sem=rsem, device_id=_sibling(), device_id_type=MESH)
        cp.start()
        cp.wait_send()

        @pl.when(s == n4 - 1)
        def _():
            pltpu.make_async_remote_copy(src_ref=got_ref, dst_ref=got_ref, send_sem=ssem, recv_sem=rsem,
                                         device_id=_sibling(), device_id_type=MESH).wait_recv()

    dma = pltpu.SemaphoreType.DMA
    return pl.pallas_call(
        body, name=name, out_shape=_sds((n4,) + blk[1:], BF16),
        grid_spec=_grid_spec((n4,), [pl.BlockSpec(blk, idx)], ANY, scratch=[dma(()), dma(())]),
        compiler_params=pltpu.CompilerParams(dimension_semantics=("arbitrary",), has_side_effects=True,
                                             collective_id=PAIR_ID, vmem_limit_bytes=VMEM_LIMIT))(pf, dw)


def _dw_pair(name, x, dy, down, dep=None):
    lp = x.shape[0]
    tk = DFF // NCHIP
    if down:
        grid = (NCHIP, 2)
        x_spec = pl.BlockSpec((lp, tk), lambda s, j: (0, s))
        dy_spec = pl.BlockSpec((lp, 1024), lambda s, j: (0, j))
        o_spec = pl.BlockSpec((1, tk, 1024), lambda s, j: (s, 0, j))
        dw_shape, got_shape = (NCHIP, tk, D), (NCHIP, tk // 2, D)
    else:
        grid = (2, NCHIP)
        x_spec = pl.BlockSpec((lp, 1024), lambda i, s: (0, i))
        dy_spec = pl.BlockSpec((lp, tk), lambda i, s: (0, s))
        o_spec = pl.BlockSpec((1, 1024, tk), lambda i, s: (s, i, 0))
        dw_shape, got_shape = (NCHIP, D, tk), (NCHIP, D // 2, tk)

    send_shape = (tk // 2, 1024) if down else (1024, tk)
    n_sends = 2 * NCHIP if down else NCHIP

    deps = [] if dep is None else [dep]

    def body(x_ref, dy_ref, *rest):
        o_ref, got_ref, sbuf, ssem, rsem = rest[len(deps):]
        g0, g1 = pl.program_id(0), pl.program_id(1)
        pl.when(jnp.logical_and(g0 == 0, g1 == 0))(_pair_barrier)
        tile = lax.dot_general(x_ref[...], dy_ref[...], _DN["tn"], preferred_element_type=F32).astype(BF16)
        o_ref[0] = tile
        _, _, c, _ = _place()

        def send(n, part, dst):
            for s in (0, 1):
                cp = pltpu.make_async_remote_copy(src_ref=sbuf.at[pl.ds(s, 1)], dst_ref=dst, send_sem=ssem.at[s],
                                                  recv_sem=rsem, device_id=_sibling(), device_id_type=MESH)

                @pl.when(n % 2 == s)
                def _():
                    pl.when(n >= 2)(cp.wait_send)
                    sbuf[s] = part()
                    cp.start()

            @pl.when(n == n_sends - 1)
            def _():
                for s in (0, 1):
                    pltpu.make_async_remote_copy(src_ref=sbuf.at[pl.ds(s, 1)], dst_ref=dst, send_sem=ssem.at[s],
                                                 recv_sem=rsem, device_id=_sibling(), device_id_type=MESH).wait_send()

        if down:
            hr = tk // 2
            rows = pl.ds(pl.multiple_of((1 - c) * hr, 16), hr)
            send(g0 * 2 + g1, lambda: o_ref[0, rows, :],
                 got_ref.at[pl.ds(g0, 1), :, pl.ds(pl.multiple_of(g1 * 1024, LANE), 1024)])
        else:
            pl.when(g0 == 1 - c)(lambda: send(g1, lambda: tile, got_ref.at[pl.ds(g1, 1)]))

        @pl.when(jnp.logical_and(g0 == grid[0] - 1, g1 == grid[1] - 1))
        def _():
            pltpu.make_async_remote_copy(src_ref=got_ref, dst_ref=got_ref, send_sem=ssem.at[0], recv_sem=rsem,
                                         device_id=_sibling(), device_id_type=MESH).wait_recv()

    dma = pltpu.SemaphoreType.DMA
    return pl.pallas_call(
        body, name=name, grid=grid, in_specs=[x_spec, dy_spec] + [ANY] * len(deps), out_specs=[o_spec, ANY],
        out_shape=[_sds(dw_shape, BF16), _sds(got_shape, BF16)],
        scratch_shapes=[pltpu.VMEM((2,) + send_shape, BF16), dma((2,)), dma(())],
        compiler_params=pltpu.CompilerParams(dimension_semantics=("arbitrary", "arbitrary"), has_side_effects=True,
                                             collective_id=PAIR_ID, vmem_limit_bytes=VMEM_LIMIT))(x, dy, *deps)


def _add4_join(name, p, got, pf, by_cols=False):
    n4, r2, c = p.shape
    if by_cols:
        tr, nch = r2, c // TCOL
        blk, idx = (r2, TCOL), (lambda i: (0, i))
    else:
        tr = _row_tile(r2, c, 4)
        nch = r2 // tr
        blk, idx = (tr, c), (lambda i: (i, 0))

    def body(pf_ref, p_ref, g_ref, mine_ref, theirs_ref, sbuf, ssem, rsem):
        i = pl.program_id(0)
        pl.when(i == 0)(_pair_barrier)
        s = p_ref[...].astype(F32)
        for k in range(3):
            s = s + g_ref[k].astype(F32)
        mine_ref[...] = s
        if by_cols:
            dst = theirs_ref.at[:, pl.ds(pl.multiple_of(i * TCOL, LANE), TCOL)]
        else:
            dst = theirs_ref.at[pl.ds(pl.multiple_of(i * tr, 8), tr), :]

        def copy(slot):
            return pltpu.make_async_remote_copy(src_ref=sbuf.at[slot], dst_ref=dst, send_sem=ssem.at[slot],
                                                recv_sem=rsem, device_id=_sibling(), device_id_type=MESH)

        for slot in (0, 1):
            @pl.when(i % 2 == slot)
            def _():
                pl.when(i >= 2)(copy(slot).wait_send)
                sbuf[slot] = s
                copy(slot).start()

        @pl.when(i == nch - 1)
        def _():
            for slot in range(min(nch, 2)):
                copy(slot).wait_send()
            pltpu.make_async_remote_copy(src_ref=theirs_ref, dst_ref=theirs_ref, send_sem=ssem.at[0], recv_sem=rsem,
                                         device_id=_sibling(), device_id_type=MESH).wait_recv()

    dma = pltpu.SemaphoreType.DMA
    o = _sds((r2, c), F32)
    return pl.pallas_call(
        body, name=name, out_shape=[o, o],
        grid_spec=_grid_spec((nch,), [pl.BlockSpec((None,) + blk, lambda i, pf: (pf[1],) + idx(i)),
                                      pl.BlockSpec((3,) + blk, lambda i, pf: (0,) + idx(i))],
                             [pl.BlockSpec(blk, lambda i, pf: idx(i)), ANY],
                             scratch=[pltpu.VMEM((2,) + blk, F32), dma((2,)), dma(())]),
        compiler_params=pltpu.CompilerParams(dimension_semantics=("arbitrary",), has_side_effects=True,
                                             collective_id=PAIR_ID, vmem_limit_bytes=VMEM_LIMIT))(pf, p, got)


def _half_of(g, slot, which, axis):
    half = g.shape[axis] // 2
    if axis == 1:
        return g.at[slot, pl.ds(which * half, half), :]
    return g.at[slot, :, pl.ds(which * half, half)]


def _ag_copies(arrs, split, send, recv):
    x, y, c, chips = _place()
    j = 2 * x + y
    cps = []
    for t, g in enumerate(arrs):
        piece = _half_of(g, j, c, split[t]) if split[t] else g.at[j]
        for k, chip in enumerate(chips):
            cps.append(pltpu.make_async_remote_copy(
                src_ref=piece, dst_ref=piece, send_sem=send.at[3 * t + k], recv_sem=recv.at[3 * t + k],
                device_id=(*chip, c), device_id_type=MESH))
    return cps


def _ag_start(name, groups, splits):
    sizes = [len(g) for g in groups]
    flat = [a for g in groups for a in g]
    n = len(flat)

    def body(*refs):
        ins = refs[:n]
        sems = refs[n:n + 2 * len(groups)]
        o = 0
        for gi, sz in enumerate(sizes):
            for cp in _ag_copies(ins[o:o + sz], splits[gi], sems[2 * gi], sems[2 * gi + 1]):
                cp.start()
            o += sz
        refs[-1][...] = jnp.zeros_like(refs[-1])

    dma = pltpu.SemaphoreType.DMA
    sem_shapes = [dma((3 * sz,)) for sz in sizes for _ in range(2)]
    outs = pl.pallas_call(
        body, name=name, out_shape=sem_shapes + [pltpu.HBM(a.shape, a.dtype) for a in flat] + [_sds((8, LANE), F32)],
        in_specs=[HBM] * n,
        out_specs=[SEM] * len(sem_shapes) + [HBM] * n + [pl.BlockSpec(memory_space=pltpu.VMEM)],
        input_output_aliases={i: len(sem_shapes) + i for i in range(n)},
        compiler_params=pltpu.CompilerParams(has_side_effects=EFFECT))(*[_in_hbm(a) for a in flat])
    sems, arrs, o = [], [], len(sem_shapes)
    for gi, sz in enumerate(sizes):
        sems.append((outs[2 * gi], outs[2 * gi + 1]))
        arrs.append(list(outs[o:o + sz]))
        o += sz
    return sems, arrs, outs[-1]


def _ag_wait(name, arrs, split, send, recv, afters):
    n = len(arrs)

    def body(*refs):
        for cp in _ag_copies(refs[:n], split, refs[n], refs[n + 1]):
            cp.wait_send()
            cp.wait_recv()

    return pl.pallas_call(
        body, name=name, out_shape=[pltpu.HBM(a.shape, a.dtype) for a in arrs],
        in_specs=[HBM] * n + [SEM, SEM] + [ANY] * len(afters), out_specs=[HBM] * n,
        input_output_aliases={i: i for i in range(n)},
        compiler_params=pltpu.CompilerParams(has_side_effects=EFFECT))(*arrs, send, recv, *afters)


def _ag_forward(name, arrs, axes):
    n = len(arrs)

    def half_shape(t):
        _, r, cc = arrs[t].shape
        return (r // 2, cc) if axes[t] == 1 else (r, cc // 2)

    def body(*refs):
        g = refs[n:2 * n]
        bufs = refs[2 * n:3 * n]
        fsend, frecv, lsem = refs[3 * n:]
        x, y, c, chips = _place()
        _pair_barrier()
        for t in range(n):
            pend = [None, None]
            for k, chip in enumerate(chips):
                jk = 2 * chip[0] + chip[1]
                slot = k % 2
                if pend[slot] is not None:
                    pend[slot].wait_send()
                part = _half_of(g[t], jk, c, axes[t])
                ld = pltpu.make_async_copy(part, bufs[t].at[slot], lsem.at[2 * t + slot])
                ld.start()
                ld.wait()
                cp = pltpu.make_async_remote_copy(
                    src_ref=bufs[t].at[slot], dst_ref=part, send_sem=fsend.at[2 * t + slot],
                    recv_sem=frecv.at[t], device_id=(x, y, 1 - c), device_id_type=MESH)
                cp.start()
                pend[slot] = cp
            for cp in pend:
                cp.wait_send()
        for t in range(n):
            hr, hc = half_shape(t)
            passed = g[t].at[pl.ds(0, 3), pl.ds(0, hr), pl.ds(0, hc)]
            pltpu.make_async_remote_copy(
                src_ref=passed, dst_ref=passed, send_sem=fsend.at[2 * t], recv_sem=frecv.at[t],
                device_id=(x, y, 1 - c), device_id_type=MESH).wait_recv()

    dma = pltpu.SemaphoreType.DMA
    scratch = [pltpu.VMEM((2,) + half_shape(t), BF16) for t in range(n)]
    scratch += [dma((2 * n,)), dma((n,)), dma((2 * n,))]
    return pl.pallas_call(
        body, name=name, in_specs=[ANY] * n, out_specs=[ANY] * n, out_shape=[_sds(a.shape, a.dtype) for a in arrs],
        scratch_shapes=scratch, input_output_aliases={t: t for t in range(n)},
        compiler_params=pltpu.CompilerParams(has_side_effects=True, collective_id=PAIR_ID,
                                             vmem_limit_bytes=VMEM_LIMIT))(*arrs)


def _rs_begin(tag, dws, by_cols, pf):
    ps = []
    for t, dw in enumerate(dws):
        if isinstance(dw, (list, tuple)):
            dw, got = dw
        else:
            got = _pair_send("rs_pair_%s_%d" % (tag, t), dw, pf, by_cols[t])
        ps.append(_add2_bf16("rs_add2_%s_%d" % (tag, t), dw, got, pf, by_cols[t]))
    send, recv, ps_thru, lands, token = _rs_chips_start("rs_chips_start_" + tag, ps)
    return (tag, send, recv, ps_thru, lands, by_cols), token


def _rs_end(handle, afters, pf):
    tag, send, recv, ps, lands, by_cols = handle
    ps, got2 = _rs_chips_wait("rs_chips_wait_" + tag, send, recv, ps, lands, afters)
    return [_add4_join("rs_add4_%s_%d" % (tag, t), p, g2, pf, by_cols[t]) for t, (p, g2) in enumerate(zip(ps, got2))]


def _allreduce_small(pack):
    r = pack.shape[0]
    rp = -(-r // 16) * 16
    half = rp // 2
    if rp != r:
        pack = jnp.pad(pack, ((0, rp - r), (0, 0)))

    def body(p_ref, o_ref, sib, chipbuf, s1, r1, s2, r2, s3, r3):
        x, y, c, chips = _place()
        j = 2 * x + y
        mine = pl.ds(pl.multiple_of(c * half, 8), half)
        other = pl.ds(pl.multiple_of((1 - c) * half, 8), half)
        swap = pltpu.make_async_remote_copy(src_ref=p_ref.at[other, :], dst_ref=sib, send_sem=s1, recv_sem=r1,
                                            device_id=_sibling(), device_id_type=MESH)
        swap.start()
        swap.wait()
        chipbuf[j] = p_ref[mine, :] + sib[...]
        cps = []
        for k, chip in enumerate(chips):
            cp = pltpu.make_async_remote_copy(src_ref=chipbuf.at[j], dst_ref=chipbuf.at[j], send_sem=s2.at[k],
                                              recv_sem=r2.at[k], device_id=(*chip, c), device_id_type=MESH)
            cp.start()
            cps.append(cp)
        for k, chip in enumerate(chips):
            jk = 2 * chip[0] + chip[1]
            pltpu.make_async_remote_copy(src_ref=chipbuf.at[jk], dst_ref=chipbuf.at[jk], send_sem=s2.at[k],
                                         recv_sem=r2.at[k], device_id=(*chip, c), device_id_type=MESH).wait_recv()
        for cp in cps:
            cp.wait_send()
        o_ref[mine, :] = ((chipbuf[0] + chipbuf[1]) + chipbuf[2]) + chipbuf[3]
        join = pltpu.make_async_remote_copy(src_ref=o_ref.at[mine, :], dst_ref=o_ref.at[mine, :], send_sem=s3,
                                            recv_sem=r3, device_id=_sibling(), device_id_type=MESH)
        join.start()
        join.wait()

    dma = pltpu.SemaphoreType.DMA
    vm = pl.BlockSpec(memory_space=pltpu.VMEM)
    out = pl.pallas_call(
        body, name="allreduce_small", in_specs=[vm], out_specs=vm, out_shape=_sds((rp, LANE), F32),
        scratch_shapes=[pltpu.VMEM((half, LANE), F32), pltpu.VMEM((NCHIP, half, LANE), F32),
                        dma(()), dma(()), dma((3,)), dma((3,)), dma(()), dma(())],
        compiler_params=pltpu.CompilerParams(has_side_effects=True))(pack)
    return out[:r]


def _in_weights(win_g):
    full = jnp.concatenate([win_g[s] for s in range(NCHIP)], axis=0)
    wqkv = full[:2048]
    og = full[2048:3072]
    gates = jnp.pad(full[3072:3080], ((0, LANE - 8), (0, 0)))
    u = full[3080:4104]
    gb = full[4104:5128]
    gc = full[5128:6152]
    return wqkv, jnp.stack([og, u, gb, gc]), gates


def _in_grads(dwqkv, dwe, dwgt):
    full = jnp.concatenate([dwqkv, dwe[0], dwgt[:8], dwe[1], dwe[2], dwe[3]], axis=0)
    sw = DIN // NCHIP
    return jnp.stack([full[s * sw:(s + 1) * sw] for s in range(NCHIP)])


def _layer_fwd(l, h, get, small):
    lp = h.shape[0]
    th = lp // 2
    wqkv, we, wgt = _in_weights(get("in", h)[0])
    nmw, bias, mnw, cw, nfw = small
    tag = "_l%d" % l
    hn = _norm_fwd("norm_mix" + tag, h, nmw)
    qkv = _mm("proj_qkv" + tag, "nt", hn, wqkv,
              pl.BlockSpec((lp, D), lambda i, j, k: (0, 0)), pl.BlockSpec((512, D), lambda i, j, k: (j, 0)),
              pl.BlockSpec((lp, 512), lambda i, j, k: (0, j)), _sds((lp, 2048), BF16), (1, 4, 1))
    e = _mm("proj_e" + tag, "nt", hn, we,
            pl.BlockSpec((lp, D), lambda i, j, k: (0, 0)), pl.BlockSpec((None, 512, D), lambda i, j, k: (j // 2, j % 2, 0)),
            pl.BlockSpec((None, lp, 512), lambda i, j, k: (j // 2, 0, j % 2)), _sds((4, lp, 1024), BF16), (1, 8, 1))
    gpre = _mm("proj_gates" + tag, "nt", hn, wgt,
               pl.BlockSpec((lp, D), lambda i, j, k: (0, 0)), pl.BlockSpec((LANE, D), lambda i, j, k: (0, 0)),
               pl.BlockSpec((lp, LANE), lambda i, j, k: (0, 0)), _sds((lp, LANE), F32), (1, 1, 1))
    grow, mcol, ccol = _gate_prep("gate_prep" + tag, gpre, bias)
    ht = _mlstm_fwd("mlstm_fwd" + tag, qkv, grow, mcol, ccol, 4)
    mix = _mix_fwd("mix_fwd" + tag, ht, e, mnw, cw)
    wout_g = get("out", mix)[0]
    wout = wout_g.reshape(D, D)
    h1 = _mm("out_proj" + tag, "nn", mix, wout,
             pl.BlockSpec((None, th, 1024), lambda i, j, k: (k, i, 0)),
             pl.BlockSpec((1024, 1024), lambda i, j, k: (k, j)),
             pl.BlockSpec((th, 1024), lambda i, j, k: (i, j)), _sds((lp, D), F32), (2, 2, 2),
             acc_shape=(th, 1024), res=h, res_spec=pl.BlockSpec((th, 1024), lambda i, j, k: (i, j)))
    wg_g, wu_g = get("ffn", h1)
    hf = _norm_fwd("norm_ffn" + tag, h1, nfw)
    g, u, a = _ffn_up("ffn_up" + tag, hf, wg_g, wu_g)
    wd_g = get("down", a)[0]
    tk = DFF // NCHIP
    h2 = _mm("ffn_down" + tag, "nn", a, wd_g,
             pl.BlockSpec((th, tk), lambda i, j, k: (i, k)),
             pl.BlockSpec((None, tk, 1024), lambda i, j, k: (k, 0, j)),
             pl.BlockSpec((th, 1024), lambda i, j, k: (i, j)), _sds((lp, D), F32), (2, 2, NCHIP),
             acc_shape=(th, 1024), res=h1, res_spec=pl.BlockSpec((th, 1024), lambda i, j, k: (i, j)))
    saved = (h, hn, qkv, e, gpre, grow, mcol, ccol, ht, mix, h1, hf, g, u, a)
    return h2, saved, (wqkv, we, wgt, wout_g, wg_g, wu_g, wd_g)


def _layer_bwd(l, dh2, dh2b, saved, wts, small, ffn_done, dep):
    h, hn, qkv, e, gpre, grow, mcol, ccol, ht, mix, h1, hf, g, u, a = saved
    wqkv, we, wgt, wout_g, wg_g, wu_g, wd_g = wts
    nmw, bias, mnw, cw, nfw = small
    lp = h.shape[0]
    th = lp // 2
    tk = DFF // NCHIP
    tag = "_l%d" % l
    half_rows = lambda i, j, k: (i, j)

    dwd = _dw_pair("dw_down" + tag, a, dh2b, True, dep)
    dg, du = _ffn_bwd_act("ffn_bwd_act" + tag, dh2b, wd_g, g, u, dep)
    dwg = _dw_pair("dw_gate" + tag, hf, dg, False)
    dwu = _dw_pair("dw_up" + tag, hf, du, False)
    dhf = _ffn_bwd_in("dhf" + tag, dg, du, wg_g, wu_g)
    dh1, dh1b, dnfw = _norm_bwd("norm_ffn_bwd" + tag, dhf, h1, nfw, dh2)
    ffn_started = ffn_done([dwg, dwu, dwd])

    dwout = _mm("dw_out" + tag, "tn", mix, dh1b,
                pl.BlockSpec((None, lp, 1024), lambda i, j, k: (i, 0, 0)), pl.BlockSpec((lp, 1024), lambda i, j, k: (0, j)),
                pl.BlockSpec((1024, 1024), half_rows), _sds((D, D), BF16), (2, 2, 1), dep=ffn_started)
    wout = wout_g.reshape(D, D)
    dmix = _mm("dmix" + tag, "nt", dh1b, wout,
               pl.BlockSpec((th, D), lambda i, j, k: (i, 0)), pl.BlockSpec((1024, D), lambda i, j, k: (j, 0)),
               pl.BlockSpec((th, 1024), half_rows), _sds((lp, D), F32), (2, 2, 1), dep=ffn_started)
    dht, de, dmnw, dcw = _mix_bwd("mix_bwd" + tag, dmix, ht, e, mnw, cw)
    dq, dk, dv, dgrow, dfx = _mlstm_bwd("mlstm_bwd" + tag, qkv, grow, mcol, ccol, ht, dht, 8)
    dgp, dgpb, dbias = _gate_bwd("gate_bwd" + tag, gpre, bias, dgrow, dfx)
    del dgp
    dqkv = jnp.concatenate([dq, dk, dv], axis=1)

    hn_cols = pl.BlockSpec((lp, 1024), lambda i, j, k: (0, j))
    dwqkv = _mm("dw_qkv" + tag, "tn", dqkv, hn, pl.BlockSpec((lp, 1024), lambda i, j, k: (0, i)), hn_cols,
                pl.BlockSpec((1024, 1024), half_rows), _sds((2048, D), BF16), (2, 2, 1))
    dwe = _mm("dw_e" + tag, "tn", de, hn, pl.BlockSpec((None, lp, 1024), lambda i, j, k: (i, 0, 0)), hn_cols,
              pl.BlockSpec((None, 1024, 1024), lambda i, j, k: (i, 0, j)), _sds((4, 1024, D), BF16), (4, 2, 1))
    dwgt = _mm("dw_gates" + tag, "tn", dgpb, hn, pl.BlockSpec((lp, LANE), lambda i, j, k: (0, 0)), hn_cols,
               pl.BlockSpec((LANE, 1024), lambda i, j, k: (0, j)), _sds((LANE, D), BF16), (1, 2, 1))
    dwin = _in_grads(dwqkv, dwe, dwgt)

    dhn = _mm("dhn_qkv" + tag, "nn", dqkv, wqkv,
              pl.BlockSpec((th, 2048), lambda i, j, k: (i, 0)), pl.BlockSpec((2048, 1024), lambda i, j, k: (0, j)),
              pl.BlockSpec((th, 1024), half_rows), _sds((lp, D), F32), (2, 2, 1))
    dhn = _mm("dhn_e" + tag, "nn", de, we,
              pl.BlockSpec((None, th, 1024), lambda i, j, k: (k, i, 0)),
              pl.BlockSpec((None, 1024, 1024), lambda i, j, k: (k, 0, j)),
              pl.BlockSpec((th, 1024), half_rows), _sds((lp, D), F32), (2, 2, 4), acc_shape=(th, 1024),
              res=dhn, res_spec=pl.BlockSpec((th, 1024), half_rows))
    dhn = _mm("dhn_gates" + tag, "nn", dgpb, wgt,
              pl.BlockSpec((th, LANE), lambda i, j, k: (i, 0)), pl.BlockSpec((LANE, 1024), lambda i, j, k: (0, j)),
              pl.BlockSpec((th, 1024), half_rows), _sds((lp, D), F32), (2, 2, 1),
              res=dhn, res_spec=pl.BlockSpec((th, 1024), half_rows))
    dh0, dh0b, dnmw = _norm_bwd("norm_mix_bwd" + tag, dhn, h, nmw, dh1)

    pieces = [dwin, dwout.reshape(NCHIP, D // NCHIP, D)]
    smalls = (dnmw[0], dbias[0, :8], dcw[:3], dmnw[0], dnfw[0])
    return dh0, dh0b, pieces, smalls


def _pack_rows(parts):
    rows = []
    for p in parts:
        f = p.reshape(-1)
        pad = (-f.shape[0]) % LANE
        if pad:
            f = jnp.pad(f, (0, pad))
        rows.append(f.reshape(-1, LANE))
    r = jnp.concatenate(rows, axis=0)
    pad = (-r.shape[0]) % 8
    if pad:
        r = jnp.pad(r, ((0, pad), (0, 0)))
    return r


def _unpack_rows(pack, shapes):
    out, r0 = [], 0
    for s in shapes:
        n = 1
        for d in s:
            n *= d
        nr = -(-n // LANE)
        out.append(pack[r0:r0 + nr].reshape(-1)[:n].reshape(s))
        r0 += nr
    return out


def kernel(x, meta_tokens, norm_mix_w, w_in, b_gates, conv_w, mlstm_norm_w, w_out, norm_ffn_w, w_gate, w_up, w_down, norm_final_w, loss_target, m_meta_tokens, m_norm_mix_w, m_w_in, m_b_gates, m_conv_w, m_mlstm_norm_w, m_w_out, m_norm_ffn_w, m_w_gate, m_w_up, m_w_down, m_norm_final_w, v_meta_tokens, v_norm_mix_w, v_w_in, v_b_gates, v_conv_w, v_mlstm_norm_w, v_w_out, v_norm_ffn_w, v_w_gate, v_w_up, v_w_down, v_norm_final_w):
    seq = x.shape[1]
    n_real = N_META + seq
    lp = -(-n_real // LANE) * LANE
    xi, yi, ci = lax.axis_index("x"), lax.axis_index("y"), lax.axis_index("c")
    jchip = 2 * xi + yi
    pf = jnp.stack([ci, jchip, 2 * (1 - xi) + yi, 2 * xi + (1 - yi), 2 * (1 - xi) + (1 - yi)]).astype(jnp.int32)

    big = {"w_in": w_in, "w_out": w_out, "w_gate": w_gate, "w_up": w_up, "w_down": w_down}
    cast = {n: [_cast_into("cast_%s_l%d" % (n, l), w, l, pf) for l in range(DEPTH)] for n, w in big.items()
            if n != "w_in"}
    in_t = lambda a: jnp.transpose(a, (2, 0, 1))
    cast["w_in"] = [_cast_into_t("cast_w_in_l%d" % l, in_t(w_in), l, pf) for l in range(DEPTH)]
    conv_flat = jnp.pad(conv_w.reshape(DEPTH * 3, CW // NCHIP), ((0, 8 - DEPTH * 3), (0, 0)))

    def own_slot(a):
        return lax.dynamic_update_slice(jnp.zeros((NCHIP,) + a.shape, a.dtype), a[None], (jchip, 0, 0))

    groups, splits = [], []
    for l in range(DEPTH):
        groups.append([cast["w_in"][l]] + ([own_slot(meta_tokens), own_slot(conv_flat)] if l == 0 else []))
        splits.append([2] + ([0, 0] if l == 0 else []))
        groups += [[cast["w_out"][l]], [cast["w_gate"][l], cast["w_up"][l]], [cast["w_down"][l]]]
        splits += [[1], [1, 1], [1]]
    group_of = {"in": 0, "out": 1, "ffn": 2, "down": 3}
    sems, arrs, _ = _ag_start("ag_start_0", groups[:1], splits[:1])
    sems_rest, arrs_rest, all_started = _ag_start("ag_start_1", groups[1:], splits[1:])
    sems, arrs = sems + sems_rest, arrs + arrs_rest

    def gathered(gi, afters):
        got = _ag_wait("ag_wait_%d" % gi, arrs[gi], splits[gi], sems[gi][0], sems[gi][1], afters)
        axes = [s for s in splits[gi] if s]
        return list(_ag_forward("ag_forward_%d" % gi, got[:len(axes)], axes)) + list(got[len(axes):])

    win0_g, meta_g, conv_g = gathered(0, [all_started])
    meta_full = jnp.concatenate([meta_g[s] for s in range(NCHIP)], axis=1)
    conv_full = jnp.concatenate([conv_g[s][:DEPTH * 3] for s in range(NCHIP)], axis=1)
    conv_full = conv_full.reshape(DEPTH, 3, CW)

    bias_rows = jnp.pad(b_gates, ((0, 0), (0, LANE - 8)))
    smalls = []
    for l in range(DEPTH):
        smalls.append((norm_mix_w[l][None], bias_rows[l][None], mlstm_norm_w[l][None],
                       jnp.pad(conv_full[l], ((0, 5), (0, 0))), norm_ffn_w[l][None]))

    h = jnp.concatenate([meta_full, x[0], jnp.zeros((lp - n_real, D), F32)], axis=0)
    saved, wts = [], []
    for l in range(DEPTH):
        def get(which, after, l=l):
            if l == 0 and which == "in":
                return [win0_g]
            return gathered(4 * l + group_of[which], [after])

        h, sv, wt = _layer_fwd(l, h, get, smalls[l])
        saved.append(sv)
        wts.append(wt)
    tgt = jnp.pad(loss_target[0], ((N_META, lp - n_real), (0, 0)))
    dh, dhb, dnorm_final, loss_part = _loss_head(h, tgt, norm_final_w[None], n_real)

    names = ["w_in", "w_out", "w_gate", "w_up", "w_down"]
    params = {"w_in": (w_in, m_w_in, v_w_in), "w_out": (w_out, m_w_out, v_w_out), "w_gate": (w_gate, m_w_gate, v_w_gate),
              "w_up": (w_up, m_w_up, v_w_up), "w_down": (w_down, m_w_down, v_w_down)}
    big_out = {n: None for n in names}
    small_grads = [None] * DEPTH
    g_in = [None] * DEPTH

    def finish(l, group, handle, afters):
        for n, (g_mine, g_theirs) in zip(group, _rs_end(handle, afters, pf)):
            if n == "w_in":
                g_in[l] = (g_mine, g_theirs)
                continue
            w, m, v = params[n]
            big_out[n] = _adamw_layer("adamw_%s_l%d" % (n, l), l, g_mine, g_theirs, w, m, v, big_out[n], pf)
        return [big_out[n][3] for n in group if n != "w_in"]

    groups = []
    token = None
    for l in reversed(range(DEPTH)):
        def ffn_done(pieces, l=l):
            handle, tok = _rs_begin("l%df" % l, pieces, [False] * 3, pf)
            groups.append((l, names[2:], handle))
            return tok

        dh, dhb, pieces, small_grads[l] = _layer_bwd(l, dh, dhb, saved[l], wts[l], smalls[l], ffn_done, token)
        handle, token = _rs_begin("l%dm" % l, pieces, [True, False], pf)
        groups.append((l, names[:2], handle))
    afters = [token]
    for l, group, handle in groups:
        afters = finish(l, group, handle, afters)
    big_out["w_in"] = [jnp.transpose(a, (1, 2, 0))
                       for a in _adamw_t("adamw_w_in", g_in, in_t(w_in), in_t(m_w_in), in_t(v_w_in), pf)]

    dnmw = jnp.stack([small_grads[l][0] for l in range(DEPTH)])
    dbias = jnp.stack([small_grads[l][1] for l in range(DEPTH)])
    dconv = jnp.stack([small_grads[l][2] for l in range(DEPTH)])
    dmnw = jnp.stack([small_grads[l][3] for l in range(DEPTH)])
    dnfw = jnp.stack([small_grads[l][4] for l in range(DEPTH)])
    part_shapes = [(N_META, D), (DEPTH, D), (DEPTH, 8), (DEPTH, 3, CW), (DEPTH, MW), (DEPTH, D), (D,), (LANE,)]
    pack = _pack_rows([dh[:N_META], dnmw, dbias, dconv, dmnw, dnfw, dnorm_final[0], loss_part[0]])
    tot = _unpack_rows(_allreduce_small(pack), part_shapes)
    g_meta_full, g_nmw, g_bias, g_conv_full, g_mnw, g_nfw, g_final, loss_row = tot
    mcols = D // NCHIP
    ccols = CW // NCHIP
    g_meta = lax.dynamic_slice_in_dim(g_meta_full, jchip * mcols, mcols, axis=1)
    g_conv = lax.dynamic_slice_in_dim(g_conv_full, jchip * ccols, ccols, axis=2)
    sm_g = [g_meta, g_nmw, g_bias, g_conv, g_mnw, g_nfw, g_final]
    sm_w = [meta_tokens, norm_mix_w, b_gates, conv_w, mlstm_norm_w, norm_ffn_w, norm_final_w]
    sm_m = [m_meta_tokens, m_norm_mix_w, m_b_gates, m_conv_w, m_mlstm_norm_w, m_norm_ffn_w, m_norm_final_w]
    sm_v = [v_meta_tokens, v_norm_mix_w, v_b_gates, v_conv_w, v_mlstm_norm_w, v_norm_ffn_w, v_norm_final_w]
    sm_shapes = [w.shape for w in sm_w]
    d_p, m_p, v_p = _adamw_flat(_pack_rows(sm_g), _pack_rows(sm_w), _pack_rows(sm_m), _pack_rows(sm_v))
    sm_d = _unpack_rows(d_p, sm_shapes)
    sm_nm = _unpack_rows(m_p, sm_shapes)
    sm_nv = _unpack_rows(v_p, sm_shapes)

    loss = loss_row[0]
    grad_x = dh[N_META:n_real][None]

    def ordered(sm, which):
        bo = {n: big_out[n][which] for n in names}
        return [sm[0], sm[1], bo["w_in"], sm[2], sm[3], sm[4], bo["w_out"], sm[5], bo["w_gate"], bo["w_up"], bo["w_down"], sm[6]]

    return (loss, grad_x, *ordered(sm_g, 0), *ordered(sm_d, 1), *ordered(sm_nm, 2), *ordered(sm_nv, 3))
```

```python
import functools

import jax
import jax.numpy as jnp
from jax import lax
from jax.experimental import pallas as pl
from jax.experimental.pallas import tpu as pltpu

F32 = jnp.float32
BF16 = jnp.bfloat16

D = 2048
N_META = 16
HEADS = 4
DQK = 128
DV = 256
MW = HEADS * DV
CW = D - MW
QKW = HEADS * DQK
DFF = 5632
DIN = 6152
NCHIP = 4
DEPTH = 2
GATE_CAP = 15.0
EPS = 1e-6
QK_SCALE = DQK ** -0.5
LANE = 128
VMEM_LIMIT = 60 * 1024 * 1024

ADAM_LR = 0.001
ADAM_B1 = 0.9
ADAM_B2 = 0.999
ADAM_EPS = 1e-08
ADAM_WD = 0.01
ADAM_STEP = 10

MESH = pl.DeviceIdType.MESH
ANY = pl.BlockSpec(memory_space=pl.ANY)


def _cp(sem):
    return pltpu.CompilerParams(dimension_semantics=sem, vmem_limit_bytes=VMEM_LIMIT)


def _sds(shape, dtype):
    return jax.ShapeDtypeStruct(shape, dtype)


_DN = {"nn": (((1,), (0,)), ((), ())), "nt": (((1,), (1,)), ((), ())), "tn": (((0,), (0,)), ((), ()))}


def _mm(name, kind, a, b, a_spec, b_spec, o_spec, out_shape, grid, acc_shape=None, res=None, res_spec=None,
        dep=None):
    nk = grid[2]
    has_res = res is not None
    n_in = 2 + has_res + (dep is not None)

    def body(*refs):
        a_ref, b_ref = refs[0], refs[1]
        r_ref = refs[2] if has_res else None
        o_ref = refs[n_in]
        p = lax.dot_general(a_ref[...], b_ref[...], _DN[kind], preferred_element_type=F32)
        if nk == 1:
            if r_ref is not None:
                p = p + r_ref[...]
            o_ref[...] = p.astype(o_ref.dtype)
        else:
            acc = refs[-1]
            k = pl.program_id(2)

            @pl.when(k == 0)
            def _():
                acc[...] = p

            @pl.when(jnp.logical_and(k > 0, k < nk - 1))
            def _():
                acc[...] += p

            @pl.when(k == nk - 1)
            def _():
                r = acc[...] + p
                if r_ref is not None:
                    r = r + r_ref[...]
                o_ref[...] = r.astype(o_ref.dtype)

    ins = [a, b] + ([res] if has_res else []) + ([dep] if dep is not None else [])
    in_specs = [a_spec, b_spec] + ([res_spec] if has_res else []) + ([ANY] if dep is not None else [])
    scratch = [pltpu.VMEM(acc_shape, F32)] if nk > 1 else []
    return pl.pallas_call(
        body, name=name, grid=grid, in_specs=in_specs, out_specs=o_spec, out_shape=out_shape,
        scratch_shapes=scratch, compiler_params=_cp(("parallel", "parallel", "arbitrary")))(*ins)


def _norm_fwd(name, h, w):
    lp = h.shape[0]
    tm = lp // 4

    def body(h_ref, w_ref, o_ref):
        x = h_ref[...]
        r = lax.rsqrt(jnp.mean(x * x, axis=1, keepdims=True) + EPS)
        o_ref[...] = (x * r * w_ref[...]).astype(BF16)

    return pl.pallas_call(
        body, name=name, grid=(4,),
        in_specs=[pl.BlockSpec((tm, D), lambda i: (i, 0)), pl.BlockSpec((1, D), lambda i: (0, 0))],
        out_specs=pl.BlockSpec((tm, D), lambda i: (i, 0)), out_shape=_sds((lp, D), BF16),
        compiler_params=_cp(("parallel",)))(h, w)


def _norm_bwd(name, dy, h, w, dres):
    lp = h.shape[0]
    tm = lp // 8

    def body(dy_ref, h_ref, w_ref, dres_ref, dh_ref, dhb_ref, dw_ref):
        x = h_ref[...]
        r = lax.rsqrt(jnp.mean(x * x, axis=1, keepdims=True) + EPS)
        xh = x * r
        dy_v = dy_ref[...]
        dxh = dy_v * w_ref[...]
        dx = r * (dxh - xh * jnp.mean(dxh * xh, axis=1, keepdims=True))
        dh = dres_ref[...] + dx
        dh_ref[...] = dh
        dhb_ref[...] = dh.astype(BF16)

        @pl.when(pl.program_id(0) == 0)
        def _():
            dw_ref[...] = jnp.zeros_like(dw_ref)

        dw_ref[0:1, :] += jnp.sum(dy_v * xh, axis=0, keepdims=True)

    row = pl.BlockSpec((tm, D), lambda i: (i, 0))
    return pl.pallas_call(
        body, name=name, grid=(8,),
        in_specs=[row, row, pl.BlockSpec((1, D), lambda i: (0, 0)), row],
        out_specs=[row, row, pl.BlockSpec((8, D), lambda i: (0, 0))],
        out_shape=[_sds((lp, D), F32), _sds((lp, D), BF16), _sds((8, D), F32)],
        compiler_params=_cp(("arbitrary",)))(dy, h, w, dres)


def _loss_head(h, tgt, w, n_real):
    lp = h.shape[0]
    tm = lp // 8

    def body(h_ref, t_ref, w_ref, dh_ref, dhb_ref, dw_ref, loss_ref):
        i = pl.program_id(0)
        x = h_ref[...]
        r = lax.rsqrt(jnp.mean(x * x, axis=1, keepdims=True) + EPS)
        xh = x * r
        wv = w_ref[...]
        row = i * tm + lax.broadcasted_iota(jnp.int32, (tm, 1), 0)
        valid = jnp.logical_and(row >= N_META, row < n_real)
        err = jnp.where(valid, xh * wv - t_ref[...], 0.0)
        dy_v = err * (1.0 / D)
        dxh = dy_v * wv
        dx = r * (dxh - xh * jnp.mean(dxh * xh, axis=1, keepdims=True))
        dh_ref[...] = dx
        dhb_ref[...] = dx.astype(BF16)

        @pl.when(i == 0)
        def _():
            dw_ref[...] = jnp.zeros_like(dw_ref)
            loss_ref[...] = jnp.zeros_like(loss_ref)

        dw_ref[0:1, :] += jnp.sum(dy_v * xh, axis=0, keepdims=True)
        part = jnp.sum(jnp.sum(err * err, axis=1, keepdims=True), axis=0, keepdims=True) * (0.5 / D)
        loss_ref[...] += jnp.broadcast_to(part, loss_ref.shape)

    row = pl.BlockSpec((tm, D), lambda i: (i, 0))
    return pl.pallas_call(
        body, name="loss_head", grid=(8,),
        in_specs=[row, row, pl.BlockSpec((1, D), lambda i: (0, 0))],
        out_specs=[row, row, pl.BlockSpec((8, D), lambda i: (0, 0)), pl.BlockSpec((8, LANE), lambda i: (0, 0))],
        out_shape=[_sds((lp, D), F32), _sds((lp, D), BF16), _sds((8, D), F32), _sds((8, LANE), F32)],
        compiler_params=_cp(("arbitrary",)))(h, tgt, w)


def _ffn_up(name, hf, wg4, wu4):
    lp = hf.shape[0]
    tm = lp // 4
    tn = DFF // NCHIP

    def body(x_ref, wg_ref, wu_ref, g_ref, u_ref, a_ref):
        x = x_ref[...]
        g = jnp.dot(x, wg_ref[...], preferred_element_type=F32)
        u = jnp.dot(x, wu_ref[...], preferred_element_type=F32)
        g_ref[...] = g.astype(BF16)
        u_ref[...] = u.astype(BF16)
        a_ref[...] = (g * jax.nn.sigmoid(g) * u).astype(BF16)

    wspec = pl.BlockSpec((None, D, tn), lambda j, i: (j, 0, 0))
    ospec = pl.BlockSpec((tm, tn), lambda j, i: (i, j))
    o = _sds((lp, DFF), BF16)
    return pl.pallas_call(
        body, name=name, grid=(NCHIP, 4),
        in_specs=[pl.BlockSpec((tm, D), lambda j, i: (i, 0)), wspec, wspec],
        out_specs=[ospec, ospec, ospec], out_shape=[o, o, o],
        compiler_params=_cp(("parallel", "parallel")))(hf, wg4, wu4)


def _ffn_down(name, a, wd4, h1):
    lp = a.shape[0]
    th = lp // 2
    tn = 512

    def body(a_ref, w_ref, r_ref, o_ref):
        w = w_ref[...].reshape(DFF, tn)
        o_ref[...] = r_ref[...] + jnp.dot(a_ref[...], w, preferred_element_type=F32)

    row = pl.BlockSpec((th, tn), lambda i, j: (i, j))
    return pl.pallas_call(
        body, name=name, grid=(2, D // tn),
        in_specs=[pl.BlockSpec((th, DFF), lambda i, j: (i, 0)),
                  pl.BlockSpec((NCHIP, DFF // NCHIP, tn), lambda i, j: (0, 0, j)), row],
        out_specs=row, out_shape=_sds((lp, D), F32), compiler_params=_cp(("parallel", "parallel")))(a, wd4, h1)


def _ffn_bwd_act(name, dhb, wd4, g, u, dep=None):
    lp = dhb.shape[0]
    tm = lp // 4
    tn = DFF // NCHIP
    deps = [] if dep is None else [dep]

    def body(d_ref, w_ref, g_ref, u_ref, *rest):
        dg_ref, du_ref = rest[len(deps):]
        da = lax.dot_general(d_ref[...], w_ref[...], _DN["nt"], preferred_element_type=F32)
        gv = g_ref[...].astype(F32)
        uv = u_ref[...].astype(F32)
        sg = jax.nn.sigmoid(gv)
        dg_ref[...] = (da * uv * (sg * (1.0 + gv * (1.0 - sg)))).astype(BF16)
        du_ref[...] = (da * (gv * sg)).astype(BF16)

    ospec = pl.BlockSpec((tm, tn), lambda j, i: (i, j))
    o = _sds((lp, DFF), BF16)
    return pl.pallas_call(
        body, name=name, grid=(NCHIP, 4),
        in_specs=[pl.BlockSpec((tm, D), lambda j, i: (i, 0)),
                  pl.BlockSpec((None, tn, D), lambda j, i: (j, 0, 0)), ospec, ospec] + [ANY] * len(deps),
        out_specs=[ospec, ospec], out_shape=[o, o],
        compiler_params=_cp(("parallel", "parallel")))(dhb, wd4, g, u, *deps)


def _ffn_bwd_in(name, dg, du, wg4, wu4):
    lp = dg.shape[0]
    th = lp // 2
    tk = DFF // NCHIP

    def body(dg_ref, du_ref, wg_ref, wu_ref, o_ref, acc):
        k = pl.program_id(2)
        p = (lax.dot_general(dg_ref[...], wg_ref[...], _DN["nt"], preferred_element_type=F32)
             + lax.dot_general(du_ref[...], wu_ref[...], _DN["nt"], preferred_element_type=F32))

        @pl.when(k == 0)
        def _():
            acc[...] = p

        @pl.when(jnp.logical_and(k > 0, k < NCHIP - 1))
        def _():
            acc[...] += p

        @pl.when(k == NCHIP - 1)
        def _():
            o_ref[...] = acc[...] + p

    a_spec = pl.BlockSpec((th, tk), lambda i, j, k: (i, k))
    w_spec = pl.BlockSpec((None, 1024, tk), lambda i, j, k: (k, j, 0))
    return pl.pallas_call(
        body, name=name, grid=(2, 2, NCHIP), in_specs=[a_spec, a_spec, w_spec, w_spec],
        out_specs=pl.BlockSpec((th, 1024), lambda i, j, k: (i, j)), out_shape=_sds((lp, D), F32),
        scratch_shapes=[pltpu.VMEM((th, 1024), F32)],
        compiler_params=_cp(("parallel", "parallel", "arbitrary")))(dg, du, wg4, wu4)


def _shift_rows(x, d, row):
    return jnp.where(row >= d, pltpu.roll(x, d, axis=0), 0.0)


def _scan_steps(lp):
    d = 1
    while d < lp:
        yield d
        d *= 2


def _gate_values(pre):
    t = GATE_CAP * jnp.tanh(pre * (1.0 / GATE_CAP))
    lf = jnp.minimum(t, 0.0) - jnp.log(1.0 + jnp.exp(-jnp.abs(t)))
    return t, lf


def _gate_prep(name, gates_pre, bias):
    lp = gates_pre.shape[0]

    def body(p_ref, b_ref, grow_ref, m_ref, c_ref):
        pre = p_ref[...] + b_ref[...]
        lane = lax.broadcasted_iota(jnp.int32, (lp, LANE), 1)
        row = lax.broadcasted_iota(jnp.int32, (lp, LANE), 0)
        t, lf = _gate_values(pre)
        f = jnp.where(jnp.logical_and(lane >= HEADS, lane < 2 * HEADS), lf, 0.0)
        for d in _scan_steps(lp):
            f = f + _shift_rows(f, d, row)
        fs = pltpu.roll(f, LANE - HEADS, axis=1)
        g = jnp.where(lane < HEADS, t - fs, 0.0)
        m = g
        for d in _scan_steps(lp):
            m = jnp.maximum(m, jnp.where(row >= d, pltpu.roll(m, d, axis=0), m))
        grow_ref[...] = g.T
        m_ref[...] = m
        c_ref[...] = jnp.where(lane < HEADS, -fs - m, 0.0)

    full = pl.BlockSpec((lp, LANE), lambda: (0, 0))
    return pl.pallas_call(
        body, name=name, in_specs=[full, pl.BlockSpec((1, LANE), lambda: (0, 0))],
        out_specs=[pl.BlockSpec((LANE, lp), lambda: (0, 0)), full, full],
        out_shape=[_sds((LANE, lp), F32), _sds((lp, LANE), F32), _sds((lp, LANE), F32)],
        compiler_params=pltpu.CompilerParams(vmem_limit_bytes=VMEM_LIMIT))(gates_pre, bias)


def _pick_lane(blk, h):
    lane = lax.broadcasted_iota(jnp.int32, blk.shape, 1)
    return jnp.sum(jnp.where(lane == h, blk, 0.0), axis=1, keepdims=True)


def _mlstm_weights(q, k, grow, mcol, i, bq, nk):
    s = lax.dot_general(q, k, _DN["nt"], preferred_element_type=F32) * QK_SCALE
    row = i * bq + lax.broadcasted_iota(jnp.int32, (bq, 1), 0)
    col = lax.broadcasted_iota(jnp.int32, (1, nk), 1)
    a = jnp.where(col <= row, jnp.exp(jnp.minimum(grow - mcol, 0.0)), 0.0)
    return s, a


def _per_query_tile(i, nq, bq, lp, compute):
    for ii in range(nq):
        nk = min(lp, -(-((ii + 1) * bq) // LANE) * LANE)
        pl.when(i == ii)(functools.partial(compute, nk))


def _mlstm_fwd(name, qkv, grow, mcol_all, ccol_all, nq):
    lp = qkv.shape[0]
    bq = lp // nq

    def body(q_ref, k_ref, v_ref, grow_ref, m_ref, c_ref, o_ref):
        h = pl.program_id(0)
        i = pl.program_id(1)
        mcol = _pick_lane(m_ref[...], h)
        ccol = _pick_lane(c_ref[...], h)

        def compute(nk):
            grow_h = grow_ref[pl.ds(h, 1), 0:nk]
            s, a = _mlstm_weights(q_ref[...], k_ref[0:nk, :], grow_h, mcol, i, bq, nk)
            p = a * s
            den = jnp.sum(p, axis=1, keepdims=True)
            num = jnp.dot(p.astype(BF16), v_ref[0:nk, :], preferred_element_type=F32)
            o_ref[...] = num / jnp.maximum(jnp.abs(den), jnp.exp(ccol))

        _per_query_tile(i, nq, bq, lp, compute)

    return pl.pallas_call(
        body, name=name, grid=(HEADS, nq),
        in_specs=[pl.BlockSpec((bq, DQK), lambda h, i: (i, h)),
                  pl.BlockSpec((lp, DQK), lambda h, i: (0, HEADS + h)),
                  pl.BlockSpec((lp, DV), lambda h, i: (0, HEADS + h)),
                  pl.BlockSpec((8, lp), lambda h, i: (0, 0)),
                  pl.BlockSpec((bq, LANE), lambda h, i: (i, 0)),
                  pl.BlockSpec((bq, LANE), lambda h, i: (i, 0))],
        out_specs=pl.BlockSpec((bq, DV), lambda h, i: (i, h)),
        out_shape=_sds((lp, MW), F32),
        compiler_params=_cp(("parallel", "parallel")))(qkv, qkv, qkv, grow, mcol_all, ccol_all)


def _mlstm_bwd(name, qkv, grow, mcol_all, ccol_all, ht, dht, nq):
    lp = qkv.shape[0]
    bq = lp // nq

    def body(q_ref, k_ref, v_ref, grow_ref, m_ref, c_ref, ht_ref, dht_ref,
             dq_ref, dk_ref, dv_ref, dgrow_ref, dfx_ref, dkt_acc, dvt_acc):
        h = pl.program_id(0)
        i = pl.program_id(1)

        @pl.when(jnp.logical_and(h == 0, i == 0))
        def _():
            dgrow_ref[...] = jnp.zeros_like(dgrow_ref)
            dfx_ref[...] = jnp.zeros_like(dfx_ref)

        @pl.when(i == 0)
        def _():
            dkt_acc[...] = jnp.zeros_like(dkt_acc)
            dvt_acc[...] = jnp.zeros_like(dvt_acc)

        mcol = _pick_lane(m_ref[...], h)
        ccol = _pick_lane(c_ref[...], h)

        def compute(nk):
            q = q_ref[...]
            k = k_ref[0:nk, :]
            v = v_ref[0:nk, :]
            grow_h = grow_ref[pl.ds(h, 1), 0:nk]
            s, a = _mlstm_weights(q, k, grow_h, mcol, i, bq, nk)
            p = a * s
            den = jnp.sum(p, axis=1, keepdims=True)
            clamp = jnp.exp(ccol)
            active = jnp.abs(den) < clamp
            dd = jnp.maximum(jnp.abs(den), clamp)
            dht_v = dht_ref[...]
            hdh = jnp.sum(dht_v * ht_ref[...], axis=1, keepdims=True)
            dn = (dht_v / dd).astype(BF16)
            dden = jnp.where(active, 0.0, -(hdh / dd) * jnp.sign(den))
            dp = lax.dot_general(dn, v, _DN["nt"], preferred_element_type=F32) + dden
            rmat = dp * p
            dgrow_ref[pl.ds(h, 1), 0:nk] += jnp.sum(rmat, axis=0, keepdims=True)
            ds = (dp * a * QK_SCALE).astype(BF16)
            dq_ref[...] = jnp.dot(ds, k, preferred_element_type=F32).astype(BF16)
            dkt_acc[:, 0:nk] += lax.dot_general(q, ds, _DN["tn"], preferred_element_type=F32)
            dvt_acc[:, 0:nk] += lax.dot_general(dn, p.astype(BF16), _DN["tn"], preferred_element_type=F32)
            lane = lax.broadcasted_iota(jnp.int32, (bq, LANE), 1)
            r0 = pl.multiple_of(i * bq, 16)
            dfx_ref[pl.ds(r0, bq), :] += jnp.where(lane == h, jnp.sum(rmat, axis=1, keepdims=True), 0.0)

        _per_query_tile(i, nq, bq, lp, compute)

        @pl.when(i == nq - 1)
        def _():
            dk_ref[...] = dkt_acc[...].T.astype(BF16)
            dv_ref[...] = dvt_acc[...].T.astype(BF16)

    return pl.pallas_call(
        body, name=name, grid=(HEADS, nq),
        in_specs=[pl.BlockSpec((bq, DQK), lambda h, i: (i, h)),
                  pl.BlockSpec((lp, DQK), lambda h, i: (0, HEADS + h)),
                  pl.BlockSpec((lp, DV), lambda h, i: (0, HEADS + h)),
                  pl.BlockSpec((8, lp), lambda h, i: (0, 0)),
                  pl.BlockSpec((bq, LANE), lambda h, i: (i, 0)),
                  pl.BlockSpec((bq, LANE), lambda h, i: (i, 0)),
                  pl.BlockSpec((bq, DV), lambda h, i: (i, h)),
                  pl.BlockSpec((bq, DV), lambda h, i: (i, h))],
        out_specs=[pl.BlockSpec((bq, DQK), lambda h, i: (i, h)),
                   pl.BlockSpec((lp, DQK), lambda h, i: (0, h)),
                   pl.BlockSpec((lp, DV), lambda h, i: (0, h)),
                   pl.BlockSpec((LANE, lp), lambda h, i: (0, 0)),
                   pl.BlockSpec((lp, LANE), lambda h, i: (0, 0))],
        out_shape=[_sds((lp, QKW), BF16), _sds((lp, QKW), BF16), _sds((lp, MW), BF16),
                   _sds((LANE, lp), F32), _sds((lp, LANE), F32)],
        scratch_shapes=[pltpu.VMEM((DQK, lp), F32), pltpu.VMEM((DV, lp), F32)],
        compiler_params=_cp(("arbitrary", "arbitrary")))(qkv, qkv, qkv, grow, mcol_all, ccol_all, ht, dht)


def _gate_bwd(name, gates_pre, bias, dgrow, dfx):
    lp = gates_pre.shape[0]

    def body(p_ref, b_ref, dgrow_ref, dfx_ref, dg_ref, dgb_ref, db_ref):
        pre = p_ref[...] + b_ref[...]
        lane = lax.broadcasted_iota(jnp.int32, (lp, LANE), 1)
        row = lax.broadcasted_iota(jnp.int32, (lp, LANE), 0)
        th = jnp.tanh(pre * (1.0 / GATE_CAP))
        t = GATE_CAP * th
        dgc = jnp.where(lane < HEADS, dgrow_ref[...].T, 0.0)
        df = jnp.where(lane < HEADS, dfx_ref[...] - dgc, 0.0)
        for d in _scan_steps(lp):
            df = df + jnp.where(row < lp - d, pltpu.roll(df, lp - d, axis=0), 0.0)
        dlf = pltpu.roll(df, HEADS, axis=1)
        dt = jnp.where(lane < HEADS, dgc, dlf * jax.nn.sigmoid(-t))
        dpre = jnp.where(lane < 2 * HEADS, dt * (1.0 - th * th), 0.0)
        dg_ref[...] = dpre
        dgb_ref[...] = dpre.astype(BF16)
        db_ref[...] = jnp.broadcast_to(jnp.sum(dpre, axis=0, keepdims=True), db_ref.shape)

    full = pl.BlockSpec((lp, LANE), lambda: (0, 0))
    return pl.pallas_call(
        body, name=name,
        in_specs=[full, pl.BlockSpec((1, LANE), lambda: (0, 0)), pl.BlockSpec((LANE, lp), lambda: (0, 0)), full],
        out_specs=[full, full, pl.BlockSpec((8, LANE), lambda: (0, 0))],
        out_shape=[_sds((lp, LANE), F32), _sds((lp, LANE), BF16), _sds((8, LANE), F32)],
        compiler_params=pltpu.CompilerParams(vmem_limit_bytes=VMEM_LIMIT))(gates_pre, bias, dgrow, dfx)


CB = 256


def _e_specs(lp):
    return [pl.BlockSpec((None, lp, CB), functools.partial(lambda c, j: (c, 0, j), c)) for c in range(4)]


def _mix_fwd(name, ht, e, mnw, cw):
    lp = ht.shape[0]

    def body(ht_ref, og_ref, u_ref, gb_ref, gc_ref, mnw_ref, cw_ref, o_ref):
        x = ht_ref[...]
        r = lax.rsqrt(jnp.mean(x * x, axis=1, keepdims=True) + EPS)
        o_ref[0] = (jax.nn.sigmoid(og_ref[...].astype(F32)) * (x * r * mnw_ref[...])).astype(BF16)
        row = lax.broadcasted_iota(jnp.int32, (lp, CB), 0)
        a = gc_ref[...].astype(F32) * u_ref[...].astype(F32)
        conv = cw_ref[2:3, :] * a + cw_ref[1:2, :] * _shift_rows(a, 1, row) + cw_ref[0:1, :] * _shift_rows(a, 2, row)
        o_ref[1] = (gb_ref[...].astype(F32) * conv).astype(BF16)

    col = pl.BlockSpec((lp, CB), lambda j: (0, j))
    return pl.pallas_call(
        body, name=name, grid=(4,),
        in_specs=[col] + _e_specs(lp) + [pl.BlockSpec((1, CB), lambda j: (0, j)), pl.BlockSpec((8, CB), lambda j: (0, j))],
        out_specs=pl.BlockSpec((2, lp, CB), lambda j: (0, 0, j)), out_shape=_sds((2, lp, MW), BF16),
        compiler_params=_cp(("parallel",)))(ht, e, e, e, e, mnw, cw)


def _mix_bwd(name, dmix, ht, e, mnw, cw):
    lp = ht.shape[0]

    def body(dhm_ref, dhc_ref, ht_ref, og_ref, u_ref, gb_ref, gc_ref, mnw_ref, cw_ref,
             dht_ref, de_ref, dmnw_ref, dcw_ref):
        x = ht_ref[...]
        r = lax.rsqrt(jnp.mean(x * x, axis=1, keepdims=True) + EPS)
        xh = x * r
        w = mnw_ref[...]
        sg = jax.nn.sigmoid(og_ref[...].astype(F32))
        dhm = dhm_ref[...]
        de_ref[0] = (dhm * (xh * w) * (sg * (1.0 - sg))).astype(BF16)
        dn = dhm * sg
        dmnw_ref[...] = jnp.broadcast_to(jnp.sum(dn * xh, axis=0, keepdims=True), dmnw_ref.shape)
        dxh = dn * w
        dht_ref[...] = r * (dxh - xh * jnp.mean(dxh * xh, axis=1, keepdims=True))

        row = lax.broadcasted_iota(jnp.int32, (lp, CB), 0)
        uv = u_ref[...].astype(F32)
        gcv = gc_ref[...].astype(F32)
        gbv = gb_ref[...].astype(F32)
        a = gcv * uv
        a1 = _shift_rows(a, 1, row)
        a2 = _shift_rows(a, 2, row)
        dhc = dhc_ref[...]
        conv = cw_ref[2:3, :] * a + cw_ref[1:2, :] * a1 + cw_ref[0:1, :] * a2
        de_ref[2] = (dhc * conv).astype(BF16)
        dconv = dhc * gbv
        dcw_ref[...] = jnp.zeros_like(dcw_ref)
        dcw_ref[0:1, :] = jnp.sum(dconv * a2, axis=0, keepdims=True)
        dcw_ref[1:2, :] = jnp.sum(dconv * a1, axis=0, keepdims=True)
        dcw_ref[2:3, :] = jnp.sum(dconv * a, axis=0, keepdims=True)
        up1 = jnp.where(row < lp - 1, pltpu.roll(dconv, lp - 1, axis=0), 0.0)
        up2 = jnp.where(row < lp - 2, pltpu.roll(dconv, lp - 2, axis=0), 0.0)
        da = cw_ref[2:3, :] * dconv + cw_ref[1:2, :] * up1 + cw_ref[0:1, :] * up2
        de_ref[1] = (da * gcv).astype(BF16)
        de_ref[3] = (da * uv).astype(BF16)

    col = pl.BlockSpec((lp, CB), lambda j: (0, j))
    small = pl.BlockSpec((8, CB), lambda j: (0, j))
    return pl.pallas_call(
        body, name=name, grid=(4,),
        in_specs=[col, pl.BlockSpec((lp, CB), lambda j: (0, 4 + j)), col] + _e_specs(lp)
                 + [pl.BlockSpec((1, CB), lambda j: (0, j)), small],
        out_specs=[col, pl.BlockSpec((4, lp, CB), lambda j: (0, 0, j)), small, small],
        out_shape=[_sds((lp, MW), F32), _sds((4, lp, MW), BF16), _sds((8, MW), F32), _sds((8, CW), F32)],
        compiler_params=_cp(("parallel",)))(dmix, dmix, ht, e, e, e, e, mnw, cw)


def _row_tile(r, c, itemsize, budget=1536 * 1024, mult=16):
    best = None
    for t in range(mult, r + 1, mult):
        if r % t == 0 and t * c * itemsize <= budget:
            best = t
    if best is None:
        best = r
    return best


def _grid_spec(grid, in_specs, out_specs, scratch=()):
    return pltpu.PrefetchScalarGridSpec(num_scalar_prefetch=1, grid=grid, in_specs=in_specs,
                                        out_specs=out_specs, scratch_shapes=list(scratch))


def _cast_into(name, w, layer, pf):
    _, r, c = w.shape
    tr = _row_tile(r, c, 4)

    def body(pf_ref, x_ref, o_ref):
        o_ref[...] = x_ref[...].astype(BF16)

    return pl.pallas_call(
        body, name=name, out_shape=_sds((NCHIP, r, c), BF16),
        grid_spec=_grid_spec((r // tr,), [pl.BlockSpec((None, tr, c), lambda i, pf: (layer, i, 0))],
                             pl.BlockSpec((None, tr, c), lambda i, pf: (pf[1], i, 0))),
        compiler_params=_cp(("parallel",)))(pf, w)


TCOL = 256


def _cast_into_t(name, w_t, layer, pf):
    c, nl, r = w_t.shape

    def body(pf_ref, x_ref, o_ref):
        o_ref[...] = x_ref[:, layer, :].astype(BF16)

    return pl.pallas_call(
        body, name=name, out_shape=_sds((NCHIP, c, r), BF16),
        grid_spec=_grid_spec((r // TCOL,), [pl.BlockSpec((c, nl, TCOL), lambda i, pf: (0, 0, i))],
                             pl.BlockSpec((None, c, TCOL), lambda i, pf: (pf[1], 0, i))),
        compiler_params=_cp(("parallel",)))(pf, w_t)


def _add2_bf16(name, dw, got, pf, by_cols=False):
    n4, r2, c2 = got.shape

    def body(pf_ref, a_ref, b_ref, o_ref):
        o_ref[...] = (a_ref[...].astype(F32) + b_ref[...].astype(F32)).astype(BF16)

    if by_cols:
        nch = c2 // TCOL
        spec = pl.BlockSpec((None, r2, TCOL), lambda s, i, pf: (s, 0, i))
        mine = pl.BlockSpec((None, r2, TCOL), lambda s, i, pf: (s, 0, pf[0] * nch + i))
    else:
        tr = _row_tile(r2, c2, 4)
        nch = r2 // tr
        spec = pl.BlockSpec((None, tr, c2), lambda s, i, pf: (s, i, 0))
        mine = pl.BlockSpec((None, tr, c2), lambda s, i, pf: (s, pf[0] * nch + i, 0))
    return pl.pallas_call(
        body, name=name, out_shape=_sds((n4, r2, c2), BF16),
        grid_spec=_grid_spec((n4, nch), [mine, spec], spec),
        compiler_params=_cp(("parallel", "parallel")))(pf, dw, got)


def _adam_math(w, g, m, v):
    m2 = ADAM_B1 * m + (1.0 - ADAM_B1) * g
    v2 = ADAM_B2 * v + (1.0 - ADAM_B2) * (g * g)
    m_hat = m2 / (1.0 - ADAM_B1 ** ADAM_STEP)
    v_hat = v2 / (1.0 - ADAM_B2 ** ADAM_STEP)
    delta = -ADAM_LR * (m_hat / (jnp.sqrt(v_hat) + ADAM_EPS) + ADAM_WD * w)
    return delta, m2, v2


def _adamw_layer(name, layer, g_mine, g_theirs, w, m, v, prev, pf):
    _, r, c = w.shape
    r2 = r // 2
    tr = _row_tile(r2, c, 4, budget=1024 * 1024, mult=8)
    nch = r2 // tr
    n_alias = 0 if prev is None else 4

    def body(*refs):
        pf_ref, gm_ref, gt_ref, w_ref, m_ref, v_ref = refs[:6]
        go_ref, d_ref, mo_ref, vo_ref = refs[6 + n_alias:]
        mine = (pl.program_id(0) // nch) == pf_ref[0]
        gv = jnp.where(mine, gm_ref[...], gt_ref[...])
        delta, m2, v2 = _adam_math(w_ref[...], gv, m_ref[...], v_ref[...])
        go_ref[...] = gv
        d_ref[...] = delta
        mo_ref[...] = m2
        vo_ref[...] = v2

    slab = pl.BlockSpec((None, tr, c), lambda i, pf: (layer, i, 0))
    ins = [g_mine, g_theirs, w, m, v] + (list(prev) if prev is not None else [])
    in_specs = [pl.BlockSpec((tr, c), lambda i, pf: (jnp.clip(i - pf[0] * nch, 0, nch - 1), 0)),
                pl.BlockSpec((tr, c), lambda i, pf: (jnp.clip(i - (1 - pf[0]) * nch, 0, nch - 1), 0)),
                slab, slab, slab] + [ANY] * n_alias
    o = _sds(w.shape, F32)
    return pl.pallas_call(
        body, name=name, out_shape=[o] * 4, grid_spec=_grid_spec((2 * nch,), in_specs, [slab] * 4),
        input_output_aliases={6 + k: k for k in range(n_alias)},
        compiler_params=_cp(("parallel",)))(pf, *ins)


def _adamw_t(name, gs, w_t, m_t, v_t, pf):
    c, nl, r = w_t.shape
    ta = LANE
    nch = (r // 2) // ta

    def body(*refs):
        pf_ref = refs[0]
        g_refs = refs[1:1 + 2 * nl]
        w_ref, m_ref, v_ref, go_ref, d_ref, mo_ref, vo_ref = refs[1 + 2 * nl:]
        mine = (pl.program_id(0) // nch) == pf_ref[0]
        gv = jnp.stack([jnp.where(mine, g_refs[2 * l][...], g_refs[2 * l + 1][...]) for l in range(nl)], axis=1)
        delta, m2, v2 = _adam_math(w_ref[...], gv, m_ref[...], v_ref[...])
        go_ref[...] = gv
        d_ref[...] = delta
        mo_ref[...] = m2
        vo_ref[...] = v2

    both = pl.BlockSpec((c, nl, ta), lambda i, pf: (0, 0, i))
    g_specs = []
    for l in range(nl):
        g_specs += [pl.BlockSpec((c, ta), lambda i, pf: (0, jnp.clip(i - pf[0] * nch, 0, nch - 1))),
                    pl.BlockSpec((c, ta), lambda i, pf: (0, jnp.clip(i - (1 - pf[0]) * nch, 0, nch - 1)))]
    o = _sds(w_t.shape, F32)
    flat_g = [a for pair in gs for a in pair]
    return pl.pallas_call(
        body, name=name, out_shape=[o] * 4, grid_spec=_grid_spec((2 * nch,), g_specs + [both] * 3, [both] * 4),
        compiler_params=_cp(("parallel",)))(pf, *flat_g, w_t, m_t, v_t)


def _adamw_flat(g, w, m, v):
    def body(g_ref, w_ref, m_ref, v_ref, d_ref, mo_ref, vo_ref):
        delta, m2, v2 = _adam_math(w_ref[...], g_ref[...], m_ref[...], v_ref[...])
        d_ref[...] = delta
        mo_ref[...] = m2
        vo_ref[...] = v2

    o = _sds(w.shape, F32)
    return pl.pallas_call(body, name="adamw_small", out_shape=[o, o, o])(g, w, m, v)


def _place():
    x, y, c = lax.axis_index("x"), lax.axis_index("y"), lax.axis_index("c")
    chips = [(1 - x, y), (x, 1 - y), (1 - x, 1 - y)]
    return x, y, c, chips


HBM = pl.BlockSpec(memory_space=pltpu.HBM)
SEM = pl.BlockSpec(memory_space=pltpu.SEMAPHORE)
EFFECT = pltpu.SideEffectType.DATAFLOW_SIDE_EFFECTING


def _in_hbm(a):
    return pltpu.with_memory_space_constraint(a, pltpu.HBM)


def _rs_chips_copies(ins, lands, send, recv):
    x, y, c, chips = _place()
    cps = []
    for t in range(len(ins)):
        for k, chip in enumerate(chips):
            jk = 2 * chip[0] + chip[1]
            cps.append(pltpu.make_async_remote_copy(
                src_ref=ins[t].at[jk], dst_ref=lands[t].at[k], send_sem=send.at[3 * t + k],
                recv_sem=recv.at[3 * t + k], device_id=(*chip, c), device_id_type=MESH))
    return cps


def _rs_chips_start(name, ps):
    n = len(ps)

    def body(*refs):
        ins, lands = refs[:n], refs[n:2 * n]
        send, recv = refs[2 * n], refs[2 * n + 1]
        token = refs[-1]
        for cp in _rs_chips_copies(ins, lands, send, recv):
            cp.start()
        token[...] = jnp.zeros_like(token)

    dma = pltpu.SemaphoreType.DMA
    lands = [lax.empty((3,) + p.shape[1:], BF16) for p in ps]
    out_shape = ([dma((3 * n,)), dma((3 * n,))] + [pltpu.HBM(p.shape, BF16) for p in ps]
                 + [pltpu.HBM(z.shape, BF16) for z in lands] + [_sds((8, LANE), F32)])
    outs = pl.pallas_call(
        body, name=name, out_shape=out_shape, in_specs=[HBM] * (2 * n),
        out_specs=[SEM, SEM] + [HBM] * (2 * n) + [pl.BlockSpec(memory_space=pltpu.VMEM)],
        input_output_aliases={i: 2 + i for i in range(2 * n)},
        compiler_params=pltpu.CompilerParams(has_side_effects=EFFECT))(
            *[_in_hbm(p) for p in ps], *[_in_hbm(z) for z in lands])
    return outs[0], outs[1], outs[2:2 + n], outs[2 + n:2 + 2 * n], outs[-1]


def _rs_chips_wait(name, send, recv, ps, lands, afters):
    n = len(ps)

    def body(*refs):
        ins, zones = refs[:n], refs[n:2 * n]
        send_ref, recv_ref = refs[2 * n], refs[2 * n + 1]
        for cp in _rs_chips_copies(ins, zones, send_ref, recv_ref):
            cp.wait_send()
            cp.wait_recv()

    outs = pl.pallas_call(
        body, name=name, out_shape=[pltpu.HBM(p.shape, BF16) for p in ps] + [pltpu.HBM(z.shape, BF16) for z in lands],
        in_specs=[HBM] * (2 * n) + [SEM, SEM] + [ANY] * len(afters), out_specs=[HBM] * (2 * n),
        input_output_aliases={i: i for i in range(2 * n)},
        compiler_params=pltpu.CompilerParams(has_side_effects=EFFECT))(*ps, *lands, send, recv, *afters)
    return outs[:n], outs[n:]


def _sibling():
    x, y, c, _ = _place()
    return (x, y, 1 - c)


PAIR_ID = 0


def _pair_barrier():
    bar = pltpu.get_barrier_semaphore()
    pl.semaphore_signal(bar, inc=1, device_id=_sibling(), device_id_type=MESH)
    pl.semaphore_wait(bar, 1)


def _pair_send(name, dw, pf, by_cols=False):
    n4, r, c = dw.shape
    blk = (1, r, c // 2) if by_cols else (1, r // 2, c)
    idx = (lambda s, pf: (s, 0, 1 - pf[0])) if by_cols else (lambda s, pf: (s, 1 - pf[0], 0))

    def body(pf_ref, x_ref, got_ref, ssem, rsem):
        s = pl.program_id(0)
        pl.when(s == 0)(_pair_barrier)
        cp = pltpu.make_async_remote_copy(src_ref=x_ref, dst_ref=got_ref.at[pl.ds(s, 1)], send_sem=ssem,
                                          recv_sem=rsem, device_id=_sibling(), device_id_type=MESH)
        cp.start()
        cp.wait_send()

        @pl.when(s == n4 - 1)
        def _():
            pltpu.make_async_remote_copy(src_ref=got_ref, dst_ref=got_ref, send_sem=ssem, recv_sem=rsem,
                                         device_id=_sibling(), device_id_type=MESH).wait_recv()

    dma = pltpu.SemaphoreType.DMA
    return pl.pallas_call(
        body, name=name, out_shape=_sds((n4,) + blk[1:], BF16),
        grid_spec=_grid_spec((n4,), [pl.BlockSpec(blk, idx)], ANY, scratch=[dma(()), dma(())]),
        compiler_params=pltpu.CompilerParams(dimension_semantics=("arbitrary",), has_side_effects=True,
                                             collective_id=PAIR_ID, vmem_limit_bytes=VMEM_LIMIT))(pf, dw)


def _dw_pair(name, x, dy, down, dep=None):
    lp = x.shape[0]
    tk = DFF // NCHIP
    if down:
        grid = (NCHIP, 2)
        x_spec = pl.BlockSpec((lp, tk), lambda s, j: (0, s))
        dy_spec = pl.BlockSpec((lp, 1024), lambda s, j: (0, j))
        o_spec = pl.BlockSpec((1, tk, 1024), lambda s, j: (s, 0, j))
        dw_shape, got_shape = (NCHIP, tk, D), (NCHIP, tk // 2, D)
    else:
        grid = (2, NCHIP)
        x_spec = pl.BlockSpec((lp, 1024), lambda i, s: (0, i))
        dy_spec = pl.BlockSpec((lp, tk), lambda i, s: (0, s))
        o_spec = pl.BlockSpec((1, 1024, tk), lambda i, s: (s, i, 0))
        dw_shape, got_shape = (NCHIP, D, tk), (NCHIP, D // 2, tk)

    send_shape = (tk // 2, 1024) if down else (1024, tk)
    n_sends = 2 * NCHIP if down else NCHIP

    deps = [] if dep is None else [dep]

    def body(x_ref, dy_ref, *rest):
        o_ref, got_ref, sbuf, ssem, rsem = rest[len(deps):]
        g0, g1 = pl.program_id(0), pl.program_id(1)
        pl.when(jnp.logical_and(g0 == 0, g1 == 0))(_pair_barrier)
        tile = lax.dot_general(x_ref[...], dy_ref[...], _DN["tn"], preferred_element_type=F32).astype(BF16)
        o_ref[0] = tile
        _, _, c, _ = _place()

        def send(n, part, dst):
            for s in (0, 1):
                cp = pltpu.make_async_remote_copy(src_ref=sbuf.at[pl.ds(s, 1)], dst_ref=dst, send_sem=ssem.at[s],
                                                  recv_sem=rsem, device_id=_sibling(), device_id_type=MESH)

                @pl.when(n % 2 == s)
                def _():
                    pl.when(n >= 2)(cp.wait_send)
                    sbuf[s] = part()
                    cp.start()

            @pl.when(n == n_sends - 1)
            def _():
                for s in (0, 1):
                    pltpu.make_async_remote_copy(src_ref=sbuf.at[pl.ds(s, 1)], dst_ref=dst, send_sem=ssem.at[s],
                                                 recv_sem=rsem, device_id=_sibling(), device_id_type=MESH).wait_send()

        if down:
            hr = tk // 2
            rows = pl.ds(pl.multiple_of((1 - c) * hr, 16), hr)
            send(g0 * 2 + g1, lambda: o_ref[0, rows, :],
                 got_ref.at[pl.ds(g0, 1), :, pl.ds(pl.multiple_of(g1 * 1024, LANE), 1024)])
        else:
            pl.when(g0 == 1 - c)(lambda: send(g1, lambda: tile, got_ref.at[pl.ds(g1, 1)]))

        @pl.when(jnp.logical_and(g0 == grid[0] - 1, g1 == grid[1] - 1))
        def _():
            pltpu.make_async_remote_copy(src_ref=got_ref, dst_ref=got_ref, send_sem=ssem.at[0], recv_sem=rsem,
                                         device_id=_sibling(), device_id_type=MESH).wait_recv()

    dma = pltpu.SemaphoreType.DMA
    return pl.pallas_call(
        body, name=name, grid=grid, in_specs=[x_spec, dy_spec] + [ANY] * len(deps), out_specs=[o_spec, ANY],
        out_shape=[_sds(dw_shape, BF16), _sds(got_shape, BF16)],
        scratch_shapes=[pltpu.VMEM((2,) + send_shape, BF16), dma((2,)), dma(())],
        compiler_params=pltpu.CompilerParams(dimension_semantics=("arbitrary", "arbitrary"), has_side_effects=True,
                                             collective_id=PAIR_ID, vmem_limit_bytes=VMEM_LIMIT))(x, dy, *deps)


def _add4_join(name, p, got, pf, by_cols=False):
    n4, r2, c = p.shape
    if by_cols:
        tr, nch = r2, c // TCOL
        blk, idx = (r2, TCOL), (lambda i: (0, i))
    else:
        tr = _row_tile(r2, c, 4)
        nch = r2 // tr
        blk, idx = (tr, c), (lambda i: (i, 0))

    def body(pf_ref, p_ref, g_ref, mine_ref, theirs_ref, sbuf, ssem, rsem):
        i = pl.program_id(0)
        pl.when(i == 0)(_pair_barrier)
        s = p_ref[...].astype(F32)
        for k in range(3):
            s = s + g_ref[k].astype(F32)
        mine_ref[...] = s
        if by_cols:
            dst = theirs_ref.at[:, pl.ds(pl.multiple_of(i * TCOL, LANE), TCOL)]
        else:
            dst = theirs_ref.at[pl.ds(pl.multiple_of(i * tr, 8), tr), :]

        def copy(slot):
            return pltpu.make_async_remote_copy(src_ref=sbuf.at[slot], dst_ref=dst, send_sem=ssem.at[slot],
                                                recv_sem=rsem, device_id=_sibling(), device_id_type=MESH)

        for slot in (0, 1):
            @pl.when(i % 2 == slot)
            def _():
                pl.when(i >= 2)(copy(slot).wait_send)
                sbuf[slot] = s
                copy(slot).start()

        @pl.when(i == nch - 1)
        def _():
            for slot in range(min(nch, 2)):
                copy(slot).wait_send()
            pltpu.make_async_remote_copy(src_ref=theirs_ref, dst_ref=theirs_ref, send_sem=ssem.at[0], recv_sem=rsem,
                                         device_id=_sibling(), device_id_type=MESH).wait_recv()

    dma = pltpu.SemaphoreType.DMA
    o = _sds((r2, c), F32)
    return pl.pallas_call(
        body, name=name, out_shape=[o, o],
        grid_spec=_grid_spec((nch,), [pl.BlockSpec((None,) + blk, lambda i, pf: (pf[1],) + idx(i)),
                                      pl.BlockSpec((3,) + blk, lambda i, pf: (0,) + idx(i))],
                             [pl.BlockSpec(blk, lambda i, pf: idx(i)), ANY],
                             scratch=[pltpu.VMEM((2,) + blk, F32), dma((2,)), dma(())]),
        compiler_params=pltpu.CompilerParams(dimension_semantics=("arbitrary",), has_side_effects=True,
                                             collective_id=PAIR_ID, vmem_limit_bytes=VMEM_LIMIT))(pf, p, got)


def _half_of(g, slot, which, axis):
    half = g.shape[axis] // 2
    if axis == 1:
        return g.at[slot, pl.ds(which * half, half), :]
    return g.at[slot, :, pl.ds(which * half, half)]


def _ag_copies(arrs, split, send, recv):
    x, y, c, chips = _place()
    j = 2 * x + y
    cps = []
    for t, g in enumerate(arrs):
        piece = _half_of(g, j, c, split[t]) if split[t] else g.at[j]
        for k, chip in enumerate(chips):
            cps.append(pltpu.make_async_remote_copy(
                src_ref=piece, dst_ref=piece, send_sem=send.at[3 * t + k], recv_sem=recv.at[3 * t + k],
                device_id=(*chip, c), device_id_type=MESH))
    return cps


def _ag_start(name, groups, splits):
    sizes = [len(g) for g in groups]
    flat = [a for g in groups for a in g]
    n = len(flat)

    def body(*refs):
        ins = refs[:n]
        sems = refs[n:n + 2 * len(groups)]
        o = 0
        for gi, sz in enumerate(sizes):
            for cp in _ag_copies(ins[o:o + sz], splits[gi], sems[2 * gi], sems[2 * gi + 1]):
                cp.start()
            o += sz
        refs[-1][...] = jnp.zeros_like(refs[-1])

    dma = pltpu.SemaphoreType.DMA
    sem_shapes = [dma((3 * sz,)) for sz in sizes for _ in range(2)]
    outs = pl.pallas_call(
        body, name=name, out_shape=sem_shapes + [pltpu.HBM(a.shape, a.dtype) for a in flat] + [_sds((8, LANE), F32)],
        in_specs=[HBM] * n,
        out_specs=[SEM] * len(sem_shapes) + [HBM] * n + [pl.BlockSpec(memory_space=pltpu.VMEM)],
        input_output_aliases={i: len(sem_shapes) + i for i in range(n)},
        compiler_params=pltpu.CompilerParams(has_side_effects=EFFECT))(*[_in_hbm(a) for a in flat])
    sems, arrs, o = [], [], len(sem_shapes)
    for gi, sz in enumerate(sizes):
        sems.append((outs[2 * gi], outs[2 * gi + 1]))
        arrs.append(list(outs[o:o + sz]))
        o += sz
    return sems, arrs, outs[-1]


def _ag_wait(name, arrs, split, send, recv, afters):
    n = len(arrs)

    def body(*refs):
        for cp in _ag_copies(refs[:n], split, refs[n], refs[n + 1]):
            cp.wait_send()
            cp.wait_recv()

    return pl.pallas_call(
        body, name=name, out_shape=[pltpu.HBM(a.shape, a.dtype) for a in arrs],
        in_specs=[HBM] * n + [SEM, SEM] + [ANY] * len(afters), out_specs=[HBM] * n,
        input_output_aliases={i: i for i in range(n)},
        compiler_params=pltpu.CompilerParams(has_side_effects=EFFECT))(*arrs, send, recv, *afters)


def _ag_forward(name, arrs, axes):
    n = len(arrs)

    def half_shape(t):
        _, r, cc = arrs[t].shape
        return (r // 2, cc) if axes[t] == 1 else (r, cc // 2)

    def body(*refs):
        g = refs[n:2 * n]
        bufs = refs[2 * n:3 * n]
        fsend, frecv, lsem = refs[3 * n:]
        x, y, c, chips = _place()
        _pair_barrier()
        for t in range(n):
            pend = [None, None]
            for k, chip in enumerate(chips):
                jk = 2 * chip[0] + chip[1]
                slot = k % 2
                if pend[slot] is not None:
                    pend[slot].wait_send()
                part = _half_of(g[t], jk, c, axes[t])
                ld = pltpu.make_async_copy(part, bufs[t].at[slot], lsem.at[2 * t + slot])
                ld.start()
                ld.wait()
                cp = pltpu.make_async_remote_copy(
                    src_ref=bufs[t].at[slot], dst_ref=part, send_sem=fsend.at[2 * t + slot],
                    recv_sem=frecv.at[t], device_id=(x, y, 1 - c), device_id_type=MESH)
                cp.start()
                pend[slot] = cp
            for cp in pend:
                cp.wait_send()
        for t in range(n):
            hr, hc = half_shape(t)
            passed = g[t].at[pl.ds(0, 3), pl.ds(0, hr), pl.ds(0, hc)]
            pltpu.make_async_remote_copy(
                src_ref=passed, dst_ref=passed, send_sem=fsend.at[2 * t], recv_sem=frecv.at[t],
                device_id=(x, y, 1 - c), device_id_type=MESH).wait_recv()

    dma = pltpu.SemaphoreType.DMA
    scratch = [pltpu.VMEM((2,) + half_shape(t), BF16) for t in range(n)]
    scratch += [dma((2 * n,)), dma((n,)), dma((2 * n,))]
    return pl.pallas_call(
        body, name=name, in_specs=[ANY] * n, out_specs=[ANY] * n, out_shape=[_sds(a.shape, a.dtype) for a in arrs],
        scratch_shapes=scratch, input_output_aliases={t: t for t in range(n)},
        compiler_params=pltpu.CompilerParams(has_side_effects=True, collective_id=PAIR_ID,
                                             vmem_limit_bytes=VMEM_LIMIT))(*arrs)


def _rs_begin(tag, dws, by_cols, pf):
    ps = []
    for t, dw in enumerate(dws):
        if isinstance(dw, (list, tuple)):
            dw, got = dw
        else:
            got = _pair_send("rs_pair_%s_%d" % (tag, t), dw, pf, by_cols[t])
        ps.append(_add2_bf16("rs_add2_%s_%d" % (tag, t), dw, got, pf, by_cols[t]))
    send, recv, ps_thru, lands, token = _rs_chips_start("rs_chips_start_" + tag, ps)
    return (tag, send, recv, ps_thru, lands, by_cols), token


def _rs_end(handle, afters, pf):
    tag, send, recv, ps, lands, by_cols = handle
    ps, got2 = _rs_chips_wait("rs_chips_wait_" + tag, send, recv, ps, lands, afters)
    return [_add4_join("rs_add4_%s_%d" % (tag, t), p, g2, pf, by_cols[t]) for t, (p, g2) in enumerate(zip(ps, got2))]


def _allreduce_small(pack):
    r = pack.shape[0]
    rp = -(-r // 16) * 16
    half = rp // 2
    if rp != r:
        pack = jnp.pad(pack, ((0, rp - r), (0, 0)))

    def body(p_ref, o_ref, sib, chipbuf, s1, r1, s2, r2, s3, r3):
        x, y, c, chips = _place()
        j = 2 * x + y
        mine = pl.ds(pl.multiple_of(c * half, 8), half)
        other = pl.ds(pl.multiple_of((1 - c) * half, 8), half)
        swap = pltpu.make_async_remote_copy(src_ref=p_ref.at[other, :], dst_ref=sib, send_sem=s1, recv_sem=r1,
                                            device_id=_sibling(), device_id_type=MESH)
        swap.start()
        swap.wait()
        chipbuf[j] = p_ref[mine, :] + sib[...]
        cps = []
        for k, chip in enumerate(chips):
            cp = pltpu.make_async_remote_copy(src_ref=chipbuf.at[j], dst_ref=chipbuf.at[j], send_sem=s2.at[k],
                                              recv_sem=r2.at[k], device_id=(*chip, c), device_id_type=MESH)
            cp.start()
            cps.append(cp)
        for k, chip in enumerate(chips):
            jk = 2 * chip[0] + chip[1]
            pltpu.make_async_remote_copy(src_ref=chipbuf.at[jk], dst_ref=chipbuf.at[jk], send_sem=s2.at[k],
                                         recv_sem=r2.at[k], device_id=(*chip, c), device_id_type=MESH).wait_recv()
        for cp in cps:
            cp.wait_send()
        o_ref[mine, :] = ((chipbuf[0] + chipbuf[1]) + chipbuf[2]) + chipbuf[3]
        join = pltpu.make_async_remote_copy(src_ref=o_ref.at[mine, :], dst_ref=o_ref.at[mine, :], send_sem=s3,
                                            recv_sem=r3, device_id=_sibling(), device_id_type=MESH)
        join.start()
        join.wait()

    dma = pltpu.SemaphoreType.DMA
    vm = pl.BlockSpec(memory_space=pltpu.VMEM)
    out = pl.pallas_call(
        body, name="allreduce_small", in_specs=[vm], out_specs=vm, out_shape=_sds((rp, LANE), F32),
        scratch_shapes=[pltpu.VMEM((half, LANE), F32), pltpu.VMEM((NCHIP, half, LANE), F32),
                        dma(()), dma(()), dma((3,)), dma((3,)), dma(()), dma(())],
        compiler_params=pltpu.CompilerParams(has_side_effects=True))(pack)
    return out[:r]


def _in_weights(win_g):
    full = jnp.concatenate([win_g[s] for s in range(NCHIP)], axis=0)
    wqkv = full[:2048]
    og = full[2048:3072]
    gates = jnp.pad(full[3072:3080], ((0, LANE - 8), (0, 0)))
    u = full[3080:4104]
    gb = full[4104:5128]
    gc = full[5128:6152]
    return wqkv, jnp.stack([og, u, gb, gc]), gates


def _in_grads(dwqkv, dwe, dwgt):
    full = jnp.concatenate([dwqkv, dwe[0], dwgt[:8], dwe[1], dwe[2], dwe[3]], axis=0)
    sw = DIN // NCHIP
    return jnp.stack([full[s * sw:(s + 1) * sw] for s in range(NCHIP)])


def _layer_fwd(l, h, get, small):
    lp = h.shape[0]
    th = lp // 2
    wqkv, we, wgt = _in_weights(get("in", h)[0])
    nmw, bias, mnw, cw, nfw = small
    tag = "_l%d" % l
    hn = _norm_fwd("norm_mix" + tag, h, nmw)
    qkv = _mm("proj_qkv" + tag, "nt", hn, wqkv,
              pl.BlockSpec((lp, D), lambda i, j, k: (0, 0)), pl.BlockSpec((512, D), lambda i, j, k: (j, 0)),
              pl.BlockSpec((lp, 512), lambda i, j, k: (0, j)), _sds((lp, 2048), BF16), (1, 4, 1))
    e = _mm("proj_e" + tag, "nt", hn, we,
            pl.BlockSpec((lp, D), lambda i, j, k: (0, 0)), pl.BlockSpec((None, 512, D), lambda i, j, k: (j // 2, j % 2, 0)),
            pl.BlockSpec((None, lp, 512), lambda i, j, k: (j // 2, 0, j % 2)), _sds((4, lp, 1024), BF16), (1, 8, 1))
    gpre = _mm("proj_gates" + tag, "nt", hn, wgt,
               pl.BlockSpec((lp, D), lambda i, j, k: (0, 0)), pl.BlockSpec((LANE, D), lambda i, j, k: (0, 0)),
               pl.BlockSpec((lp, LANE), lambda i, j, k: (0, 0)), _sds((lp, LANE), F32), (1, 1, 1))
    grow, mcol, ccol = _gate_prep("gate_prep" + tag, gpre, bias)
    ht = _mlstm_fwd("mlstm_fwd" + tag, qkv, grow, mcol, ccol, 4)
    mix = _mix_fwd("mix_fwd" + tag, ht, e, mnw, cw)
    wout_g = get("out", mix)[0]
    wout = wout_g.reshape(D, D)
    h1 = _mm("out_proj" + tag, "nn", mix, wout,
             pl.BlockSpec((None, th, 1024), lambda i, j, k: (k, i, 0)),
             pl.BlockSpec((1024, 1024), lambda i, j, k: (k, j)),
             pl.BlockSpec((th, 1024), lambda i, j, k: (i, j)), _sds((lp, D), F32), (2, 2, 2),
             acc_shape=(th, 1024), res=h, res_spec=pl.BlockSpec((th, 1024), lambda i, j, k: (i, j)))
    wg_g, wu_g = get("ffn", h1)
    hf = _norm_fwd("norm_ffn" + tag, h1, nfw)
    g, u, a = _ffn_up("ffn_up" + tag, hf, wg_g, wu_g)
    wd_g = get("down", a)[0]
    h2 = _ffn_down("ffn_down" + tag, a, wd_g, h1)
    saved = (h, hn, qkv, e, gpre, grow, mcol, ccol, ht, mix, h1, hf, g, u, a)
    return h2, saved, (wqkv, we, wgt, wout_g, wg_g, wu_g, wd_g)


def _layer_bwd(l, dh2, dh2b, saved, wts, small, ffn_done, dep):
    h, hn, qkv, e, gpre, grow, mcol, ccol, ht, mix, h1, hf, g, u, a = saved
    wqkv, we, wgt, wout_g, wg_g, wu_g, wd_g = wts
    nmw, bias, mnw, cw, nfw = small
    lp = h.shape[0]
    th = lp // 2
    tk = DFF // NCHIP
    tag = "_l%d" % l
    half_rows = lambda i, j, k: (i, j)

    dwd = _dw_pair("dw_down" + tag, a, dh2b, True, dep)
    dg, du = _ffn_bwd_act("ffn_bwd_act" + tag, dh2b, wd_g, g, u, dep)
    dwg = _dw_pair("dw_gate" + tag, hf, dg, False)
    dwu = _dw_pair("dw_up" + tag, hf, du, False)
    dhf = _ffn_bwd_in("dhf" + tag, dg, du, wg_g, wu_g)
    dh1, dh1b, dnfw = _norm_bwd("norm_ffn_bwd" + tag, dhf, h1, nfw, dh2)
    ffn_started = ffn_done([dwg, dwu, dwd])

    dwout = _mm("dw_out" + tag, "tn", mix, dh1b,
                pl.BlockSpec((None, lp, 1024), lambda i, j, k: (i, 0, 0)), pl.BlockSpec((lp, 1024), lambda i, j, k: (0, j)),
                pl.BlockSpec((1024, 1024), half_rows), _sds((D, D), BF16), (2, 2, 1), dep=ffn_started)
    wout = wout_g.reshape(D, D)
    dmix = _mm("dmix" + tag, "nt", dh1b, wout,
               pl.BlockSpec((th, D), lambda i, j, k: (i, 0)), pl.BlockSpec((1024, D), lambda i, j, k: (j, 0)),
               pl.BlockSpec((th, 1024), half_rows), _sds((lp, D), F32), (2, 2, 1), dep=ffn_started)
    dht, de, dmnw, dcw = _mix_bwd("mix_bwd" + tag, dmix, ht, e, mnw, cw)
    dq, dk, dv, dgrow, dfx = _mlstm_bwd("mlstm_bwd" + tag, qkv, grow, mcol, ccol, ht, dht, 8)
    dgp, dgpb, dbias = _gate_bwd("gate_bwd" + tag, gpre, bias, dgrow, dfx)
    del dgp
    dqkv = jnp.concatenate([dq, dk, dv], axis=1)

    hn_cols = pl.BlockSpec((lp, 1024), lambda i, j, k: (0, j))
    dwqkv = _mm("dw_qkv" + tag, "tn", dqkv, hn, pl.BlockSpec((lp, 1024), lambda i, j, k: (0, i)), hn_cols,
                pl.BlockSpec((1024, 1024), half_rows), _sds((2048, D), BF16), (2, 2, 1))
    dwe = _mm("dw_e" + tag, "tn", de, hn, pl.BlockSpec((None, lp, 1024), lambda i, j, k: (i, 0, 0)), hn_cols,
              pl.BlockSpec((None, 1024, 1024), lambda i, j, k: (i, 0, j)), _sds((4, 1024, D), BF16), (4, 2, 1))
    dwgt = _mm("dw_gates" + tag, "tn", dgpb, hn, pl.BlockSpec((lp, LANE), lambda i, j, k: (0, 0)), hn_cols,
               pl.BlockSpec((LANE, 1024), lambda i, j, k: (0, j)), _sds((LANE, D), BF16), (1, 2, 1))
    dwin = _in_grads(dwqkv, dwe, dwgt)

    dhn = _mm("dhn_qkv" + tag, "nn", dqkv, wqkv,
              pl.BlockSpec((th, 2048), lambda i, j, k: (i, 0)), pl.BlockSpec((2048, 1024), lambda i, j, k: (0, j)),
              pl.BlockSpec((th, 1024), half_rows), _sds((lp, D), F32), (2, 2, 1))
    dhn = _mm("dhn_e" + tag, "nn", de, we,
              pl.BlockSpec((None, th, 1024), lambda i, j, k: (k, i, 0)),
              pl.BlockSpec((None, 1024, 1024), lambda i, j, k: (k, 0, j)),
              pl.BlockSpec((th, 1024), half_rows), _sds((lp, D), F32), (2, 2, 4), acc_shape=(th, 1024),
              res=dhn, res_spec=pl.BlockSpec((th, 1024), half_rows))
    dhn = _mm("dhn_gates" + tag, "nn", dgpb, wgt,
              pl.BlockSpec((th, LANE), lambda i, j, k: (i, 0)), pl.BlockSpec((LANE, 1024), lambda i, j, k: (0, j)),
              pl.BlockSpec((th, 1024), half_rows), _sds((lp, D), F32), (2, 2, 1),
              res=dhn, res_spec=pl.BlockSpec((th, 1024), half_rows))
    dh0, dh0b, dnmw = _norm_bwd("norm_mix_bwd" + tag, dhn, h, nmw, dh1)

    pieces = [dwin, dwout.reshape(NCHIP, D // NCHIP, D)]
    smalls = (dnmw[0], dbias[0, :8], dcw[:3], dmnw[0], dnfw[0])
    return dh0, dh0b, pieces, smalls


def _pack_rows(parts):
    rows = []
    for p in parts:
        f = p.reshape(-1)
        pad = (-f.shape[0]) % LANE
        if pad:
            f = jnp.pad(f, (0, pad))
        rows.append(f.reshape(-1, LANE))
    r = jnp.concatenate(rows, axis=0)
    pad = (-r.shape[0]) % 8
    if pad:
        r = jnp.pad(r, ((0, pad), (0, 0)))
    return r


def _unpack_rows(pack, shapes):
    out, r0 = [], 0
    for s in shapes:
        n = 1
        for d in s:
            n *= d
        nr = -(-n // LANE)
        out.append(pack[r0:r0 + nr].reshape(-1)[:n].reshape(s))
        r0 += nr
    return out


def kernel(x, meta_tokens, norm_mix_w, w_in, b_gates, conv_w, mlstm_norm_w, w_out, norm_ffn_w, w_gate, w_up, w_down, norm_final_w, loss_target, m_meta_tokens, m_norm_mix_w, m_w_in, m_b_gates, m_conv_w, m_mlstm_norm_w, m_w_out, m_norm_ffn_w, m_w_gate, m_w_up, m_w_down, m_norm_final_w, v_meta_tokens, v_norm_mix_w, v_w_in, v_b_gates, v_conv_w, v_mlstm_norm_w, v_w_out, v_norm_ffn_w, v_w_gate, v_w_up, v_w_down, v_norm_final_w):
    seq = x.shape[1]
    n_real = N_META + seq
    lp = -(-n_real // LANE) * LANE
    xi, yi, ci = lax.axis_index("x"), lax.axis_index("y"), lax.axis_index("c")
    jchip = 2 * xi + yi
    pf = jnp.stack([ci, jchip, 2 * (1 - xi) + yi, 2 * xi + (1 - yi), 2 * (1 - xi) + (1 - yi)]).astype(jnp.int32)

    big = {"w_in": w_in, "w_out": w_out, "w_gate": w_gate, "w_up": w_up, "w_down": w_down}
    cast = {n: [_cast_into("cast_%s_l%d" % (n, l), w, l, pf) for l in range(DEPTH)] for n, w in big.items()
            if n != "w_in"}
    in_t = lambda a: jnp.transpose(a, (2, 0, 1))
    cast["w_in"] = [_cast_into_t("cast_w_in_l%d" % l, in_t(w_in), l, pf) for l in range(DEPTH)]
    conv_flat = jnp.pad(conv_w.reshape(DEPTH * 3, CW // NCHIP), ((0, 8 - DEPTH * 3), (0, 0)))

    def own_slot(a):
        return lax.dynamic_update_slice(jnp.zeros((NCHIP,) + a.shape, a.dtype), a[None], (jchip, 0, 0))

    groups, splits = [], []
    for l in range(DEPTH):
        groups.append([cast["w_in"][l]] + ([own_slot(meta_tokens), own_slot(conv_flat)] if l == 0 else []))
        splits.append([2] + ([0, 0] if l == 0 else []))
        groups += [[cast["w_out"][l]], [cast["w_gate"][l], cast["w_up"][l]], [cast["w_down"][l]]]
        splits += [[1], [1, 1], [1]]
    group_of = {"in": 0, "out": 1, "ffn": 2, "down": 3}
    sems, arrs, _ = _ag_start("ag_start_0", groups[:1], splits[:1])
    sems_rest, arrs_rest, all_started = _ag_start("ag_start_1", groups[1:], splits[1:])
    sems, arrs = sems + sems_rest, arrs + arrs_rest

    def gathered(gi, afters):
        got = _ag_wait("ag_wait_%d" % gi, arrs[gi], splits[gi], sems[gi][0], sems[gi][1], afters)
        axes = [s for s in splits[gi] if s]
        return list(_ag_forward("ag_forward_%d" % gi, got[:len(axes)], axes)) + list(got[len(axes):])

    win0_g, meta_g, conv_g = gathered(0, [all_started])
    meta_full = jnp.concatenate([meta_g[s] for s in range(NCHIP)], axis=1)
    conv_full = jnp.concatenate([conv_g[s][:DEPTH * 3] for s in range(NCHIP)], axis=1)
    conv_full = conv_full.reshape(DEPTH, 3, CW)

    bias_rows = jnp.pad(b_gates, ((0, 0), (0, LANE - 8)))
    smalls = []
    for l in range(DEPTH):
        smalls.append((norm_mix_w[l][None], bias_rows[l][None], mlstm_norm_w[l][None],
                       jnp.pad(conv_full[l], ((0, 5), (0, 0))), norm_ffn_w[l][None]))

    h = jnp.concatenate([meta_full, x[0], jnp.zeros((lp - n_real, D), F32)], axis=0)
    saved, wts = [], []
    for l in range(DEPTH):
        def get(which, after, l=l):
            if l == 0 and which == "in":
                return [win0_g]
            return gathered(4 * l + group_of[which], [after])

        h, sv, wt = _layer_fwd(l, h, get, smalls[l])
        saved.append(sv)
        wts.append(wt)
    tgt = jnp.pad(loss_target[0], ((N_META, lp - n_real), (0, 0)))
    dh, dhb, dnorm_final, loss_part = _loss_head(h, tgt, norm_final_w[None], n_real)

    names = ["w_in", "w_out", "w_gate", "w_up", "w_down"]
    params = {"w_in": (w_in, m_w_in, v_w_in), "w_out": (w_out, m_w_out, v_w_out), "w_gate": (w_gate, m_w_gate, v_w_gate),
              "w_up": (w_up, m_w_up, v_w_up), "w_down": (w_down, m_w_down, v_w_down)}
    big_out = {n: None for n in names}
    small_grads = [None] * DEPTH
    g_in = [None] * DEPTH

    def finish(l, group, handle, afters):
        for n, (g_mine, g_theirs) in zip(group, _rs_end(handle, afters, pf)):
            if n == "w_in":
                g_in[l] = (g_mine, g_theirs)
                continue
            w, m, v = params[n]
            big_out[n] = _adamw_layer("adamw_%s_l%d" % (n, l), l, g_mine, g_theirs, w, m, v, big_out[n], pf)
        return [big_out[n][3] for n in group if n != "w_in"]

    groups = []
    token = None
    for l in reversed(range(DEPTH)):
        def ffn_done(pieces, l=l):
            handle, tok = _rs_begin("l%df" % l, pieces, [False] * 3, pf)
            groups.append((l, names[2:], handle))
            return tok

        dh, dhb, pieces, small_grads[l] = _layer_bwd(l, dh, dhb, saved[l], wts[l], smalls[l], ffn_done, token)
        handle, token = _rs_begin("l%dm" % l, pieces, [True, False], pf)
        groups.append((l, names[:2], handle))
    afters = [token]
    for l, group, handle in groups:
        afters = finish(l, group, handle, afters)
    big_out["w_in"] = [jnp.transpose(a, (1, 2, 0))
                       for a in _adamw_t("adamw_w_in", g_in, in_t(w_in), in_t(m_w_in), in_t(v_w_in), pf)]

    dnmw = jnp.stack([small_grads[l][0] for l in range(DEPTH)])
    dbias = jnp.stack([small_grads[l][1] for l in range(DEPTH)])
    dconv = jnp.stack([small_grads[l][2] for l in range(DEPTH)])
    dmnw = jnp.stack([small_grads[l][3] for l in range(DEPTH)])
    dnfw = jnp.stack([small_grads[l][4] for l in range(DEPTH)])
    part_shapes = [(N_META, D), (DEPTH, D), (DEPTH, 8), (DEPTH, 3, CW), (DEPTH, MW), (DEPTH, D), (D,), (LANE,)]
    pack = _pack_rows([dh[:N_META], dnmw, dbias, dconv, dmnw, dnfw, dnorm_final[0], loss_part[0]])
    tot = _unpack_rows(_allreduce_small(pack), part_shapes)
    g_meta_full, g_nmw, g_bias, g_conv_full, g_mnw, g_nfw, g_final, loss_row = tot
    mcols = D // NCHIP
    ccols = CW // NCHIP
    g_meta = lax.dynamic_slice_in_dim(g_meta_full, jchip * mcols, mcols, axis=1)
    g_conv = lax.dynamic_slice_in_dim(g_conv_full, jchip * ccols, ccols, axis=2)
    sm_g = [g_meta, g_nmw, g_bias, g_conv, g_mnw, g_nfw, g_final]
    sm_w = [meta_tokens, norm_mix_w, b_gates, conv_w, mlstm_norm_w, norm_ffn_w, norm_final_w]
    sm_m = [m_meta_tokens, m_norm_mix_w, m_b_gates, m_conv_w, m_mlstm_norm_w, m_norm_ffn_w, m_norm_final_w]
    sm_v = [v_meta_tokens, v_norm_mix_w, v_b_gates, v_conv_w, v_mlstm_norm_w, v_norm_ffn_w, v_norm_final_w]
    sm_shapes = [w.shape for w in sm_w]
    d_p, m_p, v_p = _adamw_flat(_pack_rows(sm_g), _pack_rows(sm_w), _pack_rows(sm_m), _pack_rows(sm_v))
    sm_d = _unpack_rows(d_p, sm_shapes)
    sm_nm = _unpack_rows(m_p, sm_shapes)
    sm_nv = _unpack_rows(v_p, sm_shapes)

    loss = loss_row[0]
    grad_x = dh[N_META:n_real][None]

    def ordered(sm, which):
        bo = {n: big_out[n][which] for n in names}
        return [sm[0], sm[1], bo["w_in"], sm[2], sm[3], sm[4], bo["w_out"], sm[5], bo["w_gate"], bo["w_up"], bo["w_down"], sm[6]]

    return (loss, grad_x, *ordered(sm_g, 0), *ordered(sm_d, 1), *ordered(sm_nm, 2), *ordered(sm_nv, 3))
```

```python
import functools

import jax
import jax.numpy as jnp
from jax import lax
from jax.experimental import pallas as pl
from jax.experimental.pallas import tpu as pltpu

F32 = jnp.float32
BF16 = jnp.bfloat16

D = 2048
N_META = 16
HEADS = 4
DQK = 128
DV = 256
MW = HEADS * DV
CW = D - MW
QKW = HEADS * DQK
DFF = 5632
DIN = 6152
NCHIP = 4
DEPTH = 2
GATE_CAP = 15.0
EPS = 1e-6
QK_SCALE = DQK ** -0.5
LANE = 128
VMEM_LIMIT = 60 * 1024 * 1024

ADAM_LR = 0.001
ADAM_B1 = 0.9
ADAM_B2 = 0.999
ADAM_EPS = 1e-08
ADAM_WD = 0.01
ADAM_STEP = 10

MESH = pl.DeviceIdType.MESH
ANY = pl.BlockSpec(memory_space=pl.ANY)


def _cp(sem):
    return pltpu.CompilerParams(dimension_semantics=sem, vmem_limit_bytes=VMEM_LIMIT)


def _sds(shape, dtype):
    return jax.ShapeDtypeStruct(shape, dtype)


_DN = {"nn": (((1,), (0,)), ((), ())), "nt": (((1,), (1,)), ((), ())), "tn": (((0,), (0,)), ((), ()))}


def _mm(name, kind, a, b, a_spec, b_spec, o_spec, out_shape, grid, acc_shape=None, res=None, res_spec=None,
        dep=None):
    nk = grid[2]
    has_res = res is not None
    n_in = 2 + has_res + (dep is not None)

    def body(*refs):
        a_ref, b_ref = refs[0], refs[1]
        r_ref = refs[2] if has_res else None
        o_ref = refs[n_in]
        p = lax.dot_general(a_ref[...], b_ref[...], _DN[kind], preferred_element_type=F32)
        if nk == 1:
            if r_ref is not None:
                p = p + r_ref[...]
            o_ref[...] = p.astype(o_ref.dtype)
        else:
            acc = refs[-1]
            k = pl.program_id(2)

            @pl.when(k == 0)
            def _():
                acc[...] = p

            @pl.when(jnp.logical_and(k > 0, k < nk - 1))
            def _():
                acc[...] += p

            @pl.when(k == nk - 1)
            def _():
                r = acc[...] + p
                if r_ref is not None:
                    r = r + r_ref[...]
                o_ref[...] = r.astype(o_ref.dtype)

    ins = [a, b] + ([res] if has_res else []) + ([dep] if dep is not None else [])
    in_specs = [a_spec, b_spec] + ([res_spec] if has_res else []) + ([ANY] if dep is not None else [])
    scratch = [pltpu.VMEM(acc_shape, F32)] if nk > 1 else []
    return pl.pallas_call(
        body, name=name, grid=grid, in_specs=in_specs, out_specs=o_spec, out_shape=out_shape,
        scratch_shapes=scratch, compiler_params=_cp(("parallel", "parallel", "arbitrary")))(*ins)


def _norm_fwd(name, h, w):
    lp = h.shape[0]
    tm = lp // 4

    def body(h_ref, w_ref, o_ref):
        x = h_ref[...]
        r = lax.rsqrt(jnp.mean(x * x, axis=1, keepdims=True) + EPS)
        o_ref[...] = (x * r * w_ref[...]).astype(BF16)

    return pl.pallas_call(
        body, name=name, grid=(4,),
        in_specs=[pl.BlockSpec((tm, D), lambda i: (i, 0)), pl.BlockSpec((1, D), lambda i: (0, 0))],
        out_specs=pl.BlockSpec((tm, D), lambda i: (i, 0)), out_shape=_sds((lp, D), BF16),
        compiler_params=_cp(("parallel",)))(h, w)


def _norm_bwd(name, dy, h, w, dres):
    lp = h.shape[0]
    tm = lp // 8

    def body(dy_ref, h_ref, w_ref, dres_ref, dh_ref, dhb_ref, dw_ref):
        x = h_ref[...]
        r = lax.rsqrt(jnp.mean(x * x, axis=1, keepdims=True) + EPS)
        xh = x * r
        dy_v = dy_ref[...]
        dxh = dy_v * w_ref[...]
        dx = r * (dxh - xh * jnp.mean(dxh * xh, axis=1, keepdims=True))
        dh = dres_ref[...] + dx
        dh_ref[...] = dh
        dhb_ref[...] = dh.astype(BF16)

        @pl.when(pl.program_id(0) == 0)
        def _():
            dw_ref[...] = jnp.zeros_like(dw_ref)

        dw_ref[0:1, :] += jnp.sum(dy_v * xh, axis=0, keepdims=True)

    row = pl.BlockSpec((tm, D), lambda i: (i, 0))
    return pl.pallas_call(
        body, name=name, grid=(8,),
        in_specs=[row, row, pl.BlockSpec((1, D), lambda i: (0, 0)), row],
        out_specs=[row, row, pl.BlockSpec((8, D), lambda i: (0, 0))],
        out_shape=[_sds((lp, D), F32), _sds((lp, D), BF16), _sds((8, D), F32)],
        compiler_params=_cp(("arbitrary",)))(dy, h, w, dres)


def _loss_head(h, tgt, w, n_real):
    lp = h.shape[0]
    tm = lp // 8

    def body(h_ref, t_ref, w_ref, dh_ref, dhb_ref, dw_ref, loss_ref):
        i = pl.program_id(0)
        x = h_ref[...]
        r = lax.rsqrt(jnp.mean(x * x, axis=1, keepdims=True) + EPS)
        xh = x * r
        wv = w_ref[...]
        row = i * tm + lax.broadcasted_iota(jnp.int32, (tm, 1), 0)
        valid = jnp.logical_and(row >= N_META, row < n_real)
        err = jnp.where(valid, xh * wv - t_ref[...], 0.0)
        dy_v = err * (1.0 / D)
        dxh = dy_v * wv
        dx = r * (dxh - xh * jnp.mean(dxh * xh, axis=1, keepdims=True))
        dh_ref[...] = dx
        dhb_ref[...] = dx.astype(BF16)

        @pl.when(i == 0)
        def _():
            dw_ref[...] = jnp.zeros_like(dw_ref)
            loss_ref[...] = jnp.zeros_like(loss_ref)

        dw_ref[0:1, :] += jnp.sum(dy_v * xh, axis=0, keepdims=True)
        part = jnp.sum(jnp.sum(err * err, axis=1, keepdims=True), axis=0, keepdims=True) * (0.5 / D)
        loss_ref[...] += jnp.broadcast_to(part, loss_ref.shape)

    row = pl.BlockSpec((tm, D), lambda i: (i, 0))
    return pl.pallas_call(
        body, name="loss_head", grid=(8,),
        in_specs=[row, row, pl.BlockSpec((1, D), lambda i: (0, 0))],
        out_specs=[row, row, pl.BlockSpec((8, D), lambda i: (0, 0)), pl.BlockSpec((8, LANE), lambda i: (0, 0))],
        out_shape=[_sds((lp, D), F32), _sds((lp, D), BF16), _sds((8, D), F32), _sds((8, LANE), F32)],
        compiler_params=_cp(("arbitrary",)))(h, tgt, w)


def _ffn_up(name, hf, wg4, wu4):
    lp = hf.shape[0]
    tm = lp // 4
    tn = DFF // NCHIP

    def body(x_ref, wg_ref, wu_ref, g_ref, u_ref, a_ref):
        x = x_ref[...]
        g = jnp.dot(x, wg_ref[...], preferred_element_type=F32)
        u = jnp.dot(x, wu_ref[...], preferred_element_type=F32)
        g_ref[...] = g.astype(BF16)
        u_ref[...] = u.astype(BF16)
        a_ref[...] = (g * jax.nn.sigmoid(g) * u).astype(BF16)

    wspec = pl.BlockSpec((None, D, tn), lambda j, i: (j, 0, 0))
    ospec = pl.BlockSpec((tm, tn), lambda j, i: (i, j))
    o = _sds((lp, DFF), BF16)
    return pl.pallas_call(
        body, name=name, grid=(NCHIP, 4),
        in_specs=[pl.BlockSpec((tm, D), lambda j, i: (i, 0)), wspec, wspec],
        out_specs=[ospec, ospec, ospec], out_shape=[o, o, o],
        compiler_params=_cp(("parallel", "parallel")))(hf, wg4, wu4)


def _ffn_down(name, a, wd4, h1):
    lp = a.shape[0]
    th = lp // 2
    tn = 512

    def body(a_ref, w_ref, r_ref, o_ref):
        w = w_ref[...].reshape(DFF, tn)
        o_ref[...] = r_ref[...] + jnp.dot(a_ref[...], w, preferred_element_type=F32)

    row = pl.BlockSpec((th, tn), lambda i, j: (i, j))
    return pl.pallas_call(
        body, name=name, grid=(2, D // tn),
        in_specs=[pl.BlockSpec((th, DFF), lambda i, j: (i, 0)),
                  pl.BlockSpec((NCHIP, DFF // NCHIP, tn), lambda i, j: (0, 0, j)), row],
        out_specs=row, out_shape=_sds((lp, D), F32), compiler_params=_cp(("parallel", "parallel")))(a, wd4, h1)


def _ffn_bwd_act(name, dhb, wd4, g, u, dep=None):
    lp = dhb.shape[0]
    tm = lp // 4
    tn = DFF // NCHIP
    deps = [] if dep is None else [dep]

    def body(d_ref, w_ref, g_ref, u_ref, *rest):
        dg_ref, du_ref = rest[len(deps):]
        da = lax.dot_general(d_ref[...], w_ref[...], _DN["nt"], preferred_element_type=F32)
        gv = g_ref[...].astype(F32)
        uv = u_ref[...].astype(F32)
        sg = jax.nn.sigmoid(gv)
        dg_ref[...] = (da * uv * (sg * (1.0 + gv * (1.0 - sg)))).astype(BF16)
        du_ref[...] = (da * (gv * sg)).astype(BF16)

    ospec = pl.BlockSpec((tm, tn), lambda j, i: (i, j))
    o = _sds((lp, DFF), BF16)
    return pl.pallas_call(
        body, name=name, grid=(NCHIP, 4),
        in_specs=[pl.BlockSpec((tm, D), lambda j, i: (i, 0)),
                  pl.BlockSpec((None, tn, D), lambda j, i: (j, 0, 0)), ospec, ospec] + [ANY] * len(deps),
        out_specs=[ospec, ospec], out_shape=[o, o],
        compiler_params=_cp(("parallel", "parallel")))(dhb, wd4, g, u, *deps)


def _ffn_bwd_in(name, dg, du, wg4, wu4):
    lp = dg.shape[0]
    th = lp // 2
    tk = DFF // NCHIP

    def body(dg_ref, du_ref, wg_ref, wu_ref, o_ref, acc):
        k = pl.program_id(2)
        p = (lax.dot_general(dg_ref[...], wg_ref[...], _DN["nt"], preferred_element_type=F32)
             + lax.dot_general(du_ref[...], wu_ref[...], _DN["nt"], preferred_element_type=F32))

        @pl.when(k == 0)
        def _():
            acc[...] = p

        @pl.when(jnp.logical_and(k > 0, k < NCHIP - 1))
        def _():
            acc[...] += p

        @pl.when(k == NCHIP - 1)
        def _():
            o_ref[...] = acc[...] + p

    a_spec = pl.BlockSpec((th, tk), lambda i, j, k: (i, k))
    w_spec = pl.BlockSpec((None, 1024, tk), lambda i, j, k: (k, j, 0))
    return pl.pallas_call(
        body, name=name, grid=(2, 2, NCHIP), in_specs=[a_spec, a_spec, w_spec, w_spec],
        out_specs=pl.BlockSpec((th, 1024), lambda i, j, k: (i, j)), out_shape=_sds((lp, D), F32),
        scratch_shapes=[pltpu.VMEM((th, 1024), F32)],
        compiler_params=_cp(("parallel", "parallel", "arbitrary")))(dg, du, wg4, wu4)


def _shift_rows(x, d, row):
    return jnp.where(row >= d, pltpu.roll(x, d, axis=0), 0.0)


def _scan_steps(lp):
    d = 1
    while d < lp:
        yield d
        d *= 2


def _gate_values(pre):
    t = GATE_CAP * jnp.tanh(pre * (1.0 / GATE_CAP))
    lf = jnp.minimum(t, 0.0) - jnp.log(1.0 + jnp.exp(-jnp.abs(t)))
    return t, lf


def _gate_prep(name, gates_pre, bias):
    lp = gates_pre.shape[0]

    def body(p_ref, b_ref, grow_ref, m_ref, c_ref):
        pre = p_ref[...] + b_ref[...]
        lane = lax.broadcasted_iota(jnp.int32, (lp, LANE), 1)
        row = lax.broadcasted_iota(jnp.int32, (lp, LANE), 0)
        t, lf = _gate_values(pre)
        f = jnp.where(jnp.logical_and(lane >= HEADS, lane < 2 * HEADS), lf, 0.0)
        for d in _scan_steps(lp):
            f = f + _shift_rows(f, d, row)
        fs = pltpu.roll(f, LANE - HEADS, axis=1)
        g = jnp.where(lane < HEADS, t - fs, 0.0)
        m = g
        for d in _scan_steps(lp):
            m = jnp.maximum(m, jnp.where(row >= d, pltpu.roll(m, d, axis=0), m))
        grow_ref[...] = g.T
        m_ref[...] = m
        c_ref[...] = jnp.where(lane < HEADS, -fs - m, 0.0)

    full = pl.BlockSpec((lp, LANE), lambda: (0, 0))
    return pl.pallas_call(
        body, name=name, in_specs=[full, pl.BlockSpec((1, LANE), lambda: (0, 0))],
        out_specs=[pl.BlockSpec((LANE, lp), lambda: (0, 0)), full, full],
        out_shape=[_sds((LANE, lp), F32), _sds((lp, LANE), F32), _sds((lp, LANE), F32)],
        compiler_params=pltpu.CompilerParams(vmem_limit_bytes=VMEM_LIMIT))(gates_pre, bias)


def _pick_lane(blk, h):
    lane = lax.broadcasted_iota(jnp.int32, blk.shape, 1)
    return jnp.sum(jnp.where(lane == h, blk, 0.0), axis=1, keepdims=True)


def _mlstm_weights(q, k, grow, mcol, i, bq, nk):
    s = lax.dot_general(q, k, _DN["nt"], preferred_element_type=F32) * QK_SCALE
    row = i * bq + lax.broadcasted_iota(jnp.int32, (bq, 1), 0)
    col = lax.broadcasted_iota(jnp.int32, (1, nk), 1)
    a = jnp.where(col <= row, jnp.exp(jnp.minimum(grow - mcol, 0.0)), 0.0)
    return s, a


def _per_query_tile(i, nq, bq, lp, compute):
    for ii in range(nq):
        nk = min(lp, -(-((ii + 1) * bq) // LANE) * LANE)
        pl.when(i == ii)(functools.partial(compute, nk))


def _mlstm_fwd(name, qkv, grow, mcol_all, ccol_all, nq):
    lp = qkv.shape[0]
    bq = lp // nq

    def body(q_ref, k_ref, v_ref, grow_ref, m_ref, c_ref, o_ref):
        h = pl.program_id(0)
        i = pl.program_id(1)
        mcol = _pick_lane(m_ref[...], h)
        ccol = _pick_lane(c_ref[...], h)

        def compute(nk):
            grow_h = grow_ref[pl.ds(h, 1), 0:nk]
            s, a = _mlstm_weights(q_ref[...], k_ref[0:nk, :], grow_h, mcol, i, bq, nk)
            p = a * s
            den = jnp.sum(p, axis=1, keepdims=True)
            num = jnp.dot(p.astype(BF16), v_ref[0:nk, :], preferred_element_type=F32)
            o_ref[...] = num / jnp.maximum(jnp.abs(den), jnp.exp(ccol))

        _per_query_tile(i, nq, bq, lp, compute)

    return pl.pallas_call(
        body, name=name, grid=(HEADS, nq),
        in_specs=[pl.BlockSpec((bq, DQK), lambda h, i: (i, h)),
                  pl.BlockSpec((lp, DQK), lambda h, i: (0, HEADS + h)),
                  pl.BlockSpec((lp, DV), lambda h, i: (0, HEADS + h)),
                  pl.BlockSpec((8, lp), lambda h, i: (0, 0)),
                  pl.BlockSpec((bq, LANE), lambda h, i: (i, 0)),
                  pl.BlockSpec((bq, LANE), lambda h, i: (i, 0))],
        out_specs=pl.BlockSpec((bq, DV), lambda h, i: (i, h)),
        out_shape=_sds((lp, MW), F32),
        compiler_params=_cp(("parallel", "parallel")))(qkv, qkv, qkv, grow, mcol_all, ccol_all)


def _mlstm_bwd(name, qkv, grow, mcol_all, ccol_all, ht, dht, nq):
    lp = qkv.shape[0]
    bq = lp // nq

    def body(q_ref, k_ref, v_ref, grow_ref, m_ref, c_ref, ht_ref, dht_ref,
             dq_ref, dk_ref, dv_ref, dgrow_ref, dfx_ref, dkt_acc, dvt_acc):
        h = pl.program_id(0)
        i = pl.program_id(1)

        @pl.when(jnp.logical_and(h == 0, i == 0))
        def _():
            dgrow_ref[...] = jnp.zeros_like(dgrow_ref)
            dfx_ref[...] = jnp.zeros_like(dfx_ref)

        @pl.when(i == 0)
        def _():
            dkt_acc[...] = jnp.zeros_like(dkt_acc)
            dvt_acc[...] = jnp.zeros_like(dvt_acc)

        mcol = _pick_lane(m_ref[...], h)
        ccol = _pick_lane(c_ref[...], h)

        def compute(nk):
            q = q_ref[...]
            k = k_ref[0:nk, :]
            v = v_ref[0:nk, :]
            grow_h = grow_ref[pl.ds(h, 1), 0:nk]
            s, a = _mlstm_weights(q, k, grow_h, mcol, i, bq, nk)
            p = a * s
            den = jnp.sum(p, axis=1, keepdims=True)
            clamp = jnp.exp(ccol)
            active = jnp.abs(den) < clamp
            dd = jnp.maximum(jnp.abs(den), clamp)
            dht_v = dht_ref[...]
            hdh = jnp.sum(dht_v * ht_ref[...], axis=1, keepdims=True)
            dn = (dht_v / dd).astype(BF16)
            dden = jnp.where(active, 0.0, -(hdh / dd) * jnp.sign(den))
            dp = lax.dot_general(dn, v, _DN["nt"], preferred_element_type=F32) + dden
            rmat = dp * p
            dgrow_ref[pl.ds(h, 1), 0:nk] += jnp.sum(rmat, axis=0, keepdims=True)
            ds = (dp * a * QK_SCALE).astype(BF16)
            dq_ref[...] = jnp.dot(ds, k, preferred_element_type=F32).astype(BF16)
            dkt_acc[:, 0:nk] += lax.dot_general(q, ds, _DN["tn"], preferred_element_type=F32)
            dvt_acc[:, 0:nk] += lax.dot_general(dn, p.astype(BF16), _DN["tn"], preferred_element_type=F32)
            lane = lax.broadcasted_iota(jnp.int32, (bq, LANE), 1)
            r0 = pl.multiple_of(i * bq, 16)
            dfx_ref[pl.ds(r0, bq), :] += jnp.where(lane == h, jnp.sum(rmat, axis=1, keepdims=True), 0.0)

        _per_query_tile(i, nq, bq, lp, compute)

        @pl.when(i == nq - 1)
        def _():
            dk_ref[...] = dkt_acc[...].T.astype(BF16)
            dv_ref[...] = dvt_acc[...].T.astype(BF16)

    return pl.pallas_call(
        body, name=name, grid=(HEADS, nq),
        in_specs=[pl.BlockSpec((bq, DQK), lambda h, i: (i, h)),
                  pl.BlockSpec((lp, DQK), lambda h, i: (0, HEADS + h)),
                  pl.BlockSpec((lp, DV), lambda h, i: (0, HEADS + h)),
                  pl.BlockSpec((8, lp), lambda h, i: (0, 0)),
                  pl.BlockSpec((bq, LANE), lambda h, i: (i, 0)),
                  pl.BlockSpec((bq, LANE), lambda h, i: (i, 0)),
                  pl.BlockSpec((bq, DV), lambda h, i: (i, h)),
                  pl.BlockSpec((bq, DV), lambda h, i: (i, h))],
        out_specs=[pl.BlockSpec((bq, DQK), lambda h, i: (i, h)),
                   pl.BlockSpec((lp, DQK), lambda h, i: (0, h)),
                   pl.BlockSpec((lp, DV), lambda h, i: (0, h)),
                   pl.BlockSpec((LANE, lp), lambda h, i: (0, 0)),
                   pl.BlockSpec((lp, LANE), lambda h, i: (0, 0))],
        out_shape=[_sds((lp, QKW), BF16), _sds((lp, QKW), BF16), _sds((lp, MW), BF16),
                   _sds((LANE, lp), F32), _sds((lp, LANE), F32)],
        scratch_shapes=[pltpu.VMEM((DQK, lp), F32), pltpu.VMEM((DV, lp), F32)],
        compiler_params=_cp(("arbitrary", "arbitrary")))(qkv, qkv, qkv, grow, mcol_all, ccol_all, ht, dht)


def _gate_bwd(name, gates_pre, bias, dgrow, dfx):
    lp = gates_pre.shape[0]

    def body(p_ref, b_ref, dgrow_ref, dfx_ref, dg_ref, dgb_ref, db_ref):
        pre = p_ref[...] + b_ref[...]
        lane = lax.broadcasted_iota(jnp.int32, (lp, LANE), 1)
        row = lax.broadcasted_iota(jnp.int32, (lp, LANE), 0)
        th = jnp.tanh(pre * (1.0 / GATE_CAP))
        t = GATE_CAP * th
        dgc = jnp.where(lane < HEADS, dgrow_ref[...].T, 0.0)
        df = jnp.where(lane < HEADS, dfx_ref[...] - dgc, 0.0)
        for d in _scan_steps(lp):
            df = df + jnp.where(row < lp - d, pltpu.roll(df, lp - d, axis=0), 0.0)
        dlf = pltpu.roll(df, HEADS, axis=1)
        dt = jnp.where(lane < HEADS, dgc, dlf * jax.nn.sigmoid(-t))
        dpre = jnp.where(lane < 2 * HEADS, dt * (1.0 - th * th), 0.0)
        dg_ref[...] = dpre
        dgb_ref[...] = dpre.astype(BF16)
        db_ref[...] = jnp.broadcast_to(jnp.sum(dpre, axis=0, keepdims=True), db_ref.shape)

    full = pl.BlockSpec((lp, LANE), lambda: (0, 0))
    return pl.pallas_call(
        body, name=name,
        in_specs=[full, pl.BlockSpec((1, LANE), lambda: (0, 0)), pl.BlockSpec((LANE, lp), lambda: (0, 0)), full],
        out_specs=[full, full, pl.BlockSpec((8, LANE), lambda: (0, 0))],
        out_shape=[_sds((lp, LANE), F32), _sds((lp, LANE), BF16), _sds((8, LANE), F32)],
        compiler_params=pltpu.CompilerParams(vmem_limit_bytes=VMEM_LIMIT))(gates_pre, bias, dgrow, dfx)


CB = 256


def _e_specs(lp):
    return [pl.BlockSpec((None, lp, CB), functools.partial(lambda c, j: (c, 0, j), c)) for c in range(4)]


def _mix_fwd(name, ht, e, mnw, cw):
    lp = ht.shape[0]

    def body(ht_ref, og_ref, u_ref, gb_ref, gc_ref, mnw_ref, cw_ref, o_ref):
        x = ht_ref[...]
        r = lax.rsqrt(jnp.mean(x * x, axis=1, keepdims=True) + EPS)
        o_ref[0] = (jax.nn.sigmoid(og_ref[...].astype(F32)) * (x * r * mnw_ref[...])).astype(BF16)
        row = lax.broadcasted_iota(jnp.int32, (lp, CB), 0)
        a = gc_ref[...].astype(F32) * u_ref[...].astype(F32)
        conv = cw_ref[2:3, :] * a + cw_ref[1:2, :] * _shift_rows(a, 1, row) + cw_ref[0:1, :] * _shift_rows(a, 2, row)
        o_ref[1] = (gb_ref[...].astype(F32) * conv).astype(BF16)

    col = pl.BlockSpec((lp, CB), lambda j: (0, j))
    return pl.pallas_call(
        body, name=name, grid=(4,),
        in_specs=[col] + _e_specs(lp) + [pl.BlockSpec((1, CB), lambda j: (0, j)), pl.BlockSpec((8, CB), lambda j: (0, j))],
        out_specs=pl.BlockSpec((2, lp, CB), lambda j: (0, 0, j)), out_shape=_sds((2, lp, MW), BF16),
        compiler_params=_cp(("parallel",)))(ht, e, e, e, e, mnw, cw)


def _mix_bwd(name, dmix, ht, e, mnw, cw):
    lp = ht.shape[0]

    def body(dhm_ref, dhc_ref, ht_ref, og_ref, u_ref, gb_ref, gc_ref, mnw_ref, cw_ref,
             dht_ref, de_ref, dmnw_ref, dcw_ref):
        x = ht_ref[...]
        r = lax.rsqrt(jnp.mean(x * x, axis=1, keepdims=True) + EPS)
        xh = x * r
        w = mnw_ref[...]
        sg = jax.nn.sigmoid(og_ref[...].astype(F32))
        dhm = dhm_ref[...]
        de_ref[0] = (dhm * (xh * w) * (sg * (1.0 - sg))).astype(BF16)
        dn = dhm * sg
        dmnw_ref[...] = jnp.broadcast_to(jnp.sum(dn * xh, axis=0, keepdims=True), dmnw_ref.shape)
        dxh = dn * w
        dht_ref[...] = r * (dxh - xh * jnp.mean(dxh * xh, axis=1, keepdims=True))

        row = lax.broadcasted_iota(jnp.int32, (lp, CB), 0)
        uv = u_ref[...].astype(F32)
        gcv = gc_ref[...].astype(F32)
        gbv = gb_ref[...].astype(F32)
        a = gcv * uv
        a1 = _shift_rows(a, 1, row)
        a2 = _shift_rows(a, 2, row)
        dhc = dhc_ref[...]
        conv = cw_ref[2:3, :] * a + cw_ref[1:2, :] * a1 + cw_ref[0:1, :] * a2
        de_ref[2] = (dhc * conv).astype(BF16)
        dconv = dhc * gbv
        dcw_ref[...] = jnp.zeros_like(dcw_ref)
        dcw_ref[0:1, :] = jnp.sum(dconv * a2, axis=0, keepdims=True)
        dcw_ref[1:2, :] = jnp.sum(dconv * a1, axis=0, keepdims=True)
        dcw_ref[2:3, :] = jnp.sum(dconv * a, axis=0, keepdims=True)
        up1 = jnp.where(row < lp - 1, pltpu.roll(dconv, lp - 1, axis=0), 0.0)
        up2 = jnp.where(row < lp - 2, pltpu.roll(dconv, lp - 2, axis=0), 0.0)
        da = cw_ref[2:3, :] * dconv + cw_ref[1:2, :] * up1 + cw_ref[0:1, :] * up2
        de_ref[1] = (da * gcv).astype(BF16)
        de_ref[3] = (da * uv).astype(BF16)

    col = pl.BlockSpec((lp, CB), lambda j: (0, j))
    small = pl.BlockSpec((8, CB), lambda j: (0, j))
    return pl.pallas_call(
        body, name=name, grid=(4,),
        in_specs=[col, pl.BlockSpec((lp, CB), lambda j: (0, 4 + j)), col] + _e_specs(lp)
                 + [pl.BlockSpec((1, CB), lambda j: (0, j)), small],
        out_specs=[col, pl.BlockSpec((4, lp, CB), lambda j: (0, 0, j)), small, small],
        out_shape=[_sds((lp, MW), F32), _sds((4, lp, MW), BF16), _sds((8, MW), F32), _sds((8, CW), F32)],
        compiler_params=_cp(("parallel",)))(dmix, dmix, ht, e, e, e, e, mnw, cw)


def _row_tile(r, c, itemsize, budget=1536 * 1024, mult=16):
    best = None
    for t in range(mult, r + 1, mult):
        if r % t == 0 and t * c * itemsize <= budget:
            best = t
    if best is None:
        best = r
    return best


def _grid_spec(grid, in_specs, out_specs, scratch=()):
    return pltpu.PrefetchScalarGridSpec(num_scalar_prefetch=1, grid=grid, in_specs=in_specs,
                                        out_specs=out_specs, scratch_shapes=list(scratch))


def _cast_into(name, w, layer, pf):
    _, r, c = w.shape
    tr = _row_tile(r, c, 4)

    def body(pf_ref, x_ref, o_ref):
        o_ref[...] = x_ref[...].astype(BF16)

    return pl.pallas_call(
        body, name=name, out_shape=_sds((NCHIP, r, c), BF16),
        grid_spec=_grid_spec((r // tr,), [pl.BlockSpec((None, tr, c), lambda i, pf: (layer, i, 0))],
                             pl.BlockSpec((None, tr, c), lambda i, pf: (pf[1], i, 0))),
        compiler_params=_cp(("parallel",)))(pf, w)


TCOL = 256


def _cast_into_t(name, w_t, layer, pf):
    c, nl, r = w_t.shape

    def body(pf_ref, x_ref, o_ref):
        o_ref[...] = x_ref[:, layer, :].astype(BF16)

    return pl.pallas_call(
        body, name=name, out_shape=_sds((NCHIP, c, r), BF16),
        grid_spec=_grid_spec((r // TCOL,), [pl.BlockSpec((c, nl, TCOL), lambda i, pf: (0, 0, i))],
                             pl.BlockSpec((None, c, TCOL), lambda i, pf: (pf[1], 0, i))),
        compiler_params=_cp(("parallel",)))(pf, w_t)


def _add2_bf16(name, dw, got, pf, by_cols=False):
    n4, r2, c2 = got.shape

    def body(pf_ref, a_ref, b_ref, o_ref):
        o_ref[...] = (a_ref[...].astype(F32) + b_ref[...].astype(F32)).astype(BF16)

    if by_cols:
        nch = c2 // TCOL
        spec = pl.BlockSpec((None, r2, TCOL), lambda s, i, pf: (s, 0, i))
        mine = pl.BlockSpec((None, r2, TCOL), lambda s, i, pf: (s, 0, pf[0] * nch + i))
    else:
        tr = _row_tile(r2, c2, 4)
        nch = r2 // tr
        spec = pl.BlockSpec((None, tr, c2), lambda s, i, pf: (s, i, 0))
        mine = pl.BlockSpec((None, tr, c2), lambda s, i, pf: (s, pf[0] * nch + i, 0))
    return pl.pallas_call(
        body, name=name, out_shape=_sds((n4, r2, c2), BF16),
        grid_spec=_grid_spec((n4, nch), [mine, spec], spec),
        compiler_params=_cp(("parallel", "parallel")))(pf, dw, got)


def _adam_math(w, g, m, v):
    m2 = ADAM_B1 * m + (1.0 - ADAM_B1) * g
    v2 = ADAM_B2 * v + (1.0 - ADAM_B2) * (g * g)
    m_hat = m2 / (1.0 - ADAM_B1 ** ADAM_STEP)
    v_hat = v2 / (1.0 - ADAM_B2 ** ADAM_STEP)
    delta = -ADAM_LR * (m_hat / (jnp.sqrt(v_hat) + ADAM_EPS) + ADAM_WD * w)
    return delta, m2, v2


def _adamw_layer(name, layer, g_mine, g_theirs, w, m, v, prev, pf):
    _, r, c = w.shape
    r2 = r // 2
    tr = _row_tile(r2, c, 4, budget=1024 * 1024, mult=8)
    nch = r2 // tr
    n_alias = 0 if prev is None else 4

    def body(*refs):
        pf_ref, gm_ref, gt_ref, w_ref, m_ref, v_ref = refs[:6]
        go_ref, d_ref, mo_ref, vo_ref = refs[6 + n_alias:]
        mine = (pl.program_id(0) // nch) == pf_ref[0]
        gv = jnp.where(mine, gm_ref[...], gt_ref[...])
        delta, m2, v2 = _adam_math(w_ref[...], gv, m_ref[...], v_ref[...])
        go_ref[...] = gv
        d_ref[...] = delta
        mo_ref[...] = m2
        vo_ref[...] = v2

    slab = pl.BlockSpec((None, tr, c), lambda i, pf: (layer, i, 0))
    ins = [g_mine, g_theirs, w, m, v] + (list(prev) if prev is not None else [])
    in_specs = [pl.BlockSpec((tr, c), lambda i, pf: (jnp.clip(i - pf[0] * nch, 0, nch - 1), 0)),
                pl.BlockSpec((tr, c), lambda i, pf: (jnp.clip(i - (1 - pf[0]) * nch, 0, nch - 1), 0)),
                slab, slab, slab] + [ANY] * n_alias
    o = _sds(w.shape, F32)
    return pl.pallas_call(
        body, name=name, out_shape=[o] * 4, grid_spec=_grid_spec((2 * nch,), in_specs, [slab] * 4),
        input_output_aliases={6 + k: k for k in range(n_alias)},
        compiler_params=_cp(("parallel",)))(pf, *ins)


def _adamw_t(name, gs, w_t, m_t, v_t, pf):
    c, nl, r = w_t.shape
    ta = LANE
    nch = (r // 2) // ta

    def body(*refs):
        pf_ref = refs[0]
        g_refs = refs[1:1 + 2 * nl]
        w_ref, m_ref, v_ref, go_ref, d_ref, mo_ref, vo_ref = refs[1 + 2 * nl:]
        mine = (pl.program_id(0) // nch) == pf_ref[0]
        gv = jnp.stack([jnp.where(mine, g_refs[2 * l][...], g_refs[2 * l + 1][...]) for l in range(nl)], axis=1)
        delta, m2, v2 = _adam_math(w_ref[...], gv, m_ref[...], v_ref[...])
        go_ref[...] = gv
        d_ref[...] = delta
        mo_ref[...] = m2
        vo_ref[...] = v2

    both = pl.BlockSpec((c, nl, ta), lambda i, pf: (0, 0, i))
    g_specs = []
    for l in range(nl):
        g_specs += [pl.BlockSpec((c, ta), lambda i, pf: (0, jnp.clip(i - pf[0] * nch, 0, nch - 1))),
                    pl.BlockSpec((c, ta), lambda i, pf: (0, jnp.clip(i - (1 - pf[0]) * nch, 0, nch - 1)))]
    o = _sds(w_t.shape, F32)
    flat_g = [a for pair in gs for a in pair]
    return pl.pallas_call(
        body, name=name, out_shape=[o] * 4, grid_spec=_grid_spec((2 * nch,), g_specs + [both] * 3, [both] * 4),
        compiler_params=_cp(("parallel",)))(pf, *flat_g, w_t, m_t, v_t)


def _adamw_flat(g, w, m, v):
    def body(g_ref, w_ref, m_ref, v_ref, d_ref, mo_ref, vo_ref):
        delta, m2, v2 = _adam_math(w_ref[...], g_ref[...], m_ref[...], v_ref[...])
        d_ref[...] = delta
        mo_ref[...] = m2
        vo_ref[...] = v2

    o = _sds(w.shape, F32)
    return pl.pallas_call(body, name="adamw_small", out_shape=[o, o, o])(g, w, m, v)


def _place():
    x, y, c = lax.axis_index("x"), lax.axis_index("y"), lax.axis_index("c")
    chips = [(1 - x, y), (x, 1 - y), (1 - x, 1 - y)]
    return x, y, c, chips


HBM = pl.BlockSpec(memory_space=pltpu.HBM)
SEM = pl.BlockSpec(memory_space=pltpu.SEMAPHORE)
EFFECT = pltpu.SideEffectType.DATAFLOW_SIDE_EFFECTING


def _in_hbm(a):
    return pltpu.with_memory_space_constraint(a, pltpu.HBM)


def _rs_chips_copies(ins, lands, send, recv):
    x, y, c, chips = _place()
    cps = []
    for t in range(len(ins)):
        for k, chip in enumerate(chips):
            jk = 2 * chip[0] + chip[1]
            cps.append(pltpu.make_async_remote_copy(
                src_ref=ins[t].at[jk], dst_ref=lands[t].at[k], send_sem=send.at[3 * t + k],
                recv_sem=recv.at[3 * t + k], device_id=(*chip, c), device_id_type=MESH))
    return cps


def _rs_chips_start(name, ps):
    n = len(ps)

    def body(*refs):
        ins, lands = refs[:n], refs[n:2 * n]
        send, recv = refs[2 * n], refs[2 * n + 1]
        token = refs[-1]
        for cp in _rs_chips_copies(ins, lands, send, recv):
            cp.start()
        token[...] = jnp.zeros_like(token)

    dma = pltpu.SemaphoreType.DMA
    lands = [lax.empty((3,) + p.shape[1:], BF16) for p in ps]
    out_shape = ([dma((3 * n,)), dma((3 * n,))] + [pltpu.HBM(p.shape, BF16) for p in ps]
                 + [pltpu.HBM(z.shape, BF16) for z in lands] + [_sds((8, LANE), F32)])
    outs = pl.pallas_call(
        body, name=name, out_shape=out_shape, in_specs=[HBM] * (2 * n),
        out_specs=[SEM, SEM] + [HBM] * (2 * n) + [pl.BlockSpec(memory_space=pltpu.VMEM)],
        input_output_aliases={i: 2 + i for i in range(2 * n)},
        compiler_params=pltpu.CompilerParams(has_side_effects=EFFECT))(
            *[_in_hbm(p) for p in ps], *[_in_hbm(z) for z in lands])
    return outs[0], outs[1], outs[2:2 + n], outs[2 + n:2 + 2 * n], outs[-1]


def _rs_chips_wait(name, send, recv, ps, lands, afters):
    n = len(ps)

    def body(*refs):
        ins, zones = refs[:n], refs[n:2 * n]
        send_ref, recv_ref = refs[2 * n], refs[2 * n + 1]
        for cp in _rs_chips_copies(ins, zones, send_ref, recv_ref):
            cp.wait_send()
            cp.wait_recv()

    outs = pl.pallas_call(
        body, name=name, out_shape=[pltpu.HBM(p.shape, BF16) for p in ps] + [pltpu.HBM(z.shape, BF16) for z in lands],
        in_specs=[HBM] * (2 * n) + [SEM, SEM] + [ANY] * len(afters), out_specs=[HBM] * (2 * n),
        input_output_aliases={i: i for i in range(2 * n)},
        compiler_params=pltpu.CompilerParams(has_side_effects=EFFECT))(*ps, *lands, send, recv, *afters)
    return outs[:n], outs[n:]


def _sibling():
    x, y, c, _ = _place()
    return (x, y, 1 - c)


PAIR_ID = 0


def _pair_barrier():
    bar = pltpu.get_barrier_semaphore()
    pl.semaphore_signal(bar, inc=1, device_id=_sibling(), device_id_type=MESH)
    pl.semaphore_wait(bar, 1)


def _pair_send(name, dw, pf, by_cols=False):
    n4, r, c = dw.shape
    blk = (1, r, c // 2) if by_cols else (1, r // 2, c)
    idx = (lambda s, pf: (s, 0, 1 - pf[0])) if by_cols else (lambda s, pf: (s, 1 - pf[0], 0))

    def body(pf_ref, x_ref, got_ref, ssem, rsem):
        s = pl.program_id(0)
        pl.when(s == 0)(_pair_barrier)
        cp = pltpu.make_async_remote_copy(src_ref=x_ref, dst_ref=got_ref.at[pl.ds(s, 1)], send_sem=ssem,
                                          recv_sem=rsem, device_id=_sibling(), device_id_type=MESH)
        cp.start()
        cp.wait_send()

        @pl.when(s == n4 - 1)
        def _():
            pltpu.make_async_remote_copy(src_ref=got_ref, dst_ref=got_ref, send_sem=ssem, recv_sem=rsem,
                                         device_id=_sibling(), device_id_type=MESH).wait_recv()

    dma = pltpu.SemaphoreType.DMA
    return pl.pallas_call(
        body, name=name, out_shape=_sds((n4,) + blk[1:], BF16),
        grid_spec=_grid_spec((n4,), [pl.BlockSpec(blk, idx)], ANY, scratch=[dma(()), dma(())]),
        compiler_params=pltpu.CompilerParams(dimension_semantics=("arbitrary",), has_side_effects=True,
                                             collective_id=PAIR_ID, vmem_limit_bytes=VMEM_LIMIT))(pf, dw)


def _dw_pair(name, x, dy, down, dep=None):
    lp = x.shape[0]
    tk = DFF // NCHIP
    if down:
        grid = (NCHIP, 2)
        x_spec = pl.BlockSpec((lp, tk), lambda s, j: (0, s))
        dy_spec = pl.BlockSpec((lp, 1024), lambda s, j: (0, j))
        o_spec = pl.BlockSpec((1, tk, 1024), lambda s, j: (s, 0, j))
        dw_shape, got_shape = (NCHIP, tk, D), (NCHIP, tk // 2, D)
    else:
        grid = (2, NCHIP)
        x_spec = pl.BlockSpec((lp, 1024), lambda i, s: (0, i))
        dy_spec = pl.BlockSpec((lp, tk), lambda i, s: (0, s))
        o_spec = pl.BlockSpec((1, 1024, tk), lambda i, s: (s, i, 0))
        dw_shape, got_shape = (NCHIP, D, tk), (NCHIP, D // 2, tk)

    send_shape = (tk // 2, 1024) if down else (1024, tk)
    n_sends = 2 * NCHIP if down else NCHIP

    deps = [] if dep is None else [dep]

    def body(x_ref, dy_ref, *rest):
        o_ref, got_ref, sbuf, ssem, rsem = rest[len(deps):]
        g0, g1 = pl.program_id(0), pl.program_id(1)
        pl.when(jnp.logical_and(g0 == 0, g1 == 0))(_pair_barrier)
        tile = lax.dot_general(x_ref[...], dy_ref[...], _DN["tn"], preferred_element_type=F32).astype(BF16)
        o_ref[0] = tile
        _, _, c, _ = _place()

        def send(n, part, dst):
            for s in (0, 1):
                cp = pltpu.make_async_remote_copy(src_ref=sbuf.at[pl.ds(s, 1)], dst_ref=dst, send_sem=ssem.at[s],
                                                  recv_sem=rsem, device_id=_sibling(), device_id_type=MESH)

                @pl.when(n % 2 == s)
                def _():
                    pl.when(n >= 2)(cp.wait_send)
                    sbuf[s] = part()
                    cp.start()

            @pl.when(n == n_sends - 1)
            def _():
                for s in (0, 1):
                    pltpu.make_async_remote_copy(src_ref=sbuf.at[pl.ds(s, 1)], dst_ref=dst, send_sem=ssem.at[s],
                                                 recv_sem=rsem, device_id=_sibling(), device_id_type=MESH).wait_send()

        if down:
            hr = tk // 2
            rows = pl.ds(pl.multiple_of((1 - c) * hr, 16), hr)
            send(g0 * 2 + g1, lambda: o_ref[0, rows, :],
                 got_ref.at[pl.ds(g0, 1), :, pl.ds(pl.multiple_of(g1 * 1024, LANE), 1024)])
        else:
            pl.when(g0 == 1 - c)(lambda: send(g1, lambda: tile, got_ref.at[pl.ds(g1, 1)]))

        @pl.when(jnp.logical_and(g0 == grid[0] - 1, g1 == grid[1] - 1))
        def _():
            pltpu.make_async_remote_copy(src_ref=got_ref, dst_ref=got_ref, send_sem=ssem.at[0], recv_sem=rsem,
                                         device_id=_sibling(), device_id_type=MESH).wait_recv()

    dma = pltpu.SemaphoreType.DMA
    return pl.pallas_call(
        body, name=name, grid=grid, in_specs=[x_spec, dy_spec] + [ANY] * len(deps), out_specs=[o_spec, ANY],
        out_shape=[_sds(dw_shape, BF16), _sds(got_shape, BF16)],
        scratch_shapes=[pltpu.VMEM((2,) + send_shape, BF16), dma((2,)), dma(())],
        compiler_params=pltpu.CompilerParams(dimension_semantics=("arbitrary", "arbitrary"), has_side_effects=True,
                                             collective_id=PAIR_ID, vmem_limit_bytes=VMEM_LIMIT))(x, dy, *deps)


def _add4_join(name, p, got, pf, by_cols=False):
    n4, r2, c = p.shape
    if by_cols:
        tr, nch = r2, c // TCOL
        blk, idx = (r2, TCOL), (lambda i: (0, i))
    else:
        tr = _row_tile(r2, c, 4)
        nch = r2 // tr
        blk, idx = (tr, c), (lambda i: (i, 0))

    def body(pf_ref, p_ref, g_ref, mine_ref, theirs_ref, sbuf, ssem, rsem):
        i = pl.program_id(0)
        pl.when(i == 0)(_pair_barrier)
        s = p_ref[...].astype(F32)
        for k in range(3):
            s = s + g_ref[k].astype(F32)
        mine_ref[...] = s
        if by_cols:
            dst = theirs_ref.at[:, pl.ds(pl.multiple_of(i * TCOL, LANE), TCOL)]
        else:
            dst = theirs_ref.at[pl.ds(pl.multiple_of(i * tr, 8), tr), :]

        def copy(slot):
            return pltpu.make_async_remote_copy(src_ref=sbuf.at[slot], dst_ref=dst, send_sem=ssem.at[slot],
                                                recv_sem=rsem, device_id=_sibling(), device_id_type=MESH)

        for slot in (0, 1):
            @pl.when(i % 2 == slot)
            def _():
                pl.when(i >= 2)(copy(slot).wait_send)
                sbuf[slot] = s
                copy(slot).start()

        @pl.when(i == nch - 1)
        def _():
            for slot in range(min(nch, 2)):
                copy(slot).wait_send()
            pltpu.make_async_remote_copy(src_ref=theirs_ref, dst_ref=theirs_ref, send_sem=ssem.at[0], recv_sem=rsem,
                                         device_id=_sibling(), device_id_type=MESH).wait_recv()

    dma = pltpu.SemaphoreType.DMA
    o = _sds((r2, c), F32)
    return pl.pallas_call(
        body, name=name, out_shape=[o, o],
        grid_spec=_grid_spec((nch,), [pl.BlockSpec((None,) + blk, lambda i, pf: (pf[1],) + idx(i)),
                                      pl.BlockSpec((3,) + blk, lambda i, pf: (0,) + idx(i))],
                             [pl.BlockSpec(blk, lambda i, pf: idx(i)), ANY],
                             scratch=[pltpu.VMEM((2,) + blk, F32), dma((2,)), dma(())]),
        compiler_params=pltpu.CompilerParams(dimension_semantics=("arbitrary",), has_side_effects=True,
                                             collective_id=PAIR_ID, vmem_limit_bytes=VMEM_LIMIT))(pf, p, got)


def _half_of(g, slot, which, axis):
    half = g.shape[axis] // 2
    if axis == 1:
        return g.at[slot, pl.ds(which * half, half), :]
    return g.at[slot, :, pl.ds(which * half, half)]


def _ag_copies(arrs, split, send, recv):
    x, y, c, chips = _place()
    j = 2 * x + y
    cps = []
    for t, g in enumerate(arrs):
        piece = _half_of(g, j, c, split[t]) if split[t] else g.at[j]
        for k, chip in enumerate(chips):
            cps.append(pltpu.make_async_remote_copy(
                src_ref=piece, dst_ref=piece, send_sem=send.at[3 * t + k], recv_sem=recv.at[3 * t + k],
                device_id=(*chip, c), device_id_type=MESH))
    return cps


def _ag_start(name, groups, splits):
    sizes = [len(g) for g in groups]
    flat = [a for g in groups for a in g]
    n = len(flat)

    def body(*refs):
        ins = refs[:n]
        sems = refs[n:n + 2 * len(groups)]
        o = 0
        for gi, sz in enumerate(sizes):
            for cp in _ag_copies(ins[o:o + sz], splits[gi], sems[2 * gi], sems[2 * gi + 1]):
                cp.start()
            o += sz
        refs[-1][...] = jnp.zeros_like(refs[-1])

    dma = pltpu.SemaphoreType.DMA
    sem_shapes = [dma((3 * sz,)) for sz in sizes for _ in range(2)]
    outs = pl.pallas_call(
        body, name=name, out_shape=sem_shapes + [pltpu.HBM(a.shape, a.dtype) for a in flat] + [_sds((8, LANE), F32)],
        in_specs=[HBM] * n,
        out_specs=[SEM] * len(sem_shapes) + [HBM] * n + [pl.BlockSpec(memory_space=pltpu.VMEM)],
        input_output_aliases={i: len(sem_shapes) + i for i in range(n)},
        compiler_params=pltpu.CompilerParams(has_side_effects=EFFECT))(*[_in_hbm(a) for a in flat])
    sems, arrs, o = [], [], len(sem_shapes)
    for gi, sz in enumerate(sizes):
        sems.append((outs[2 * gi], outs[2 * gi + 1]))
        arrs.append(list(outs[o:o + sz]))
        o += sz
    return sems, arrs, outs[-1]


def _ag_wait(name, arrs, split, send, recv, afters):
    n = len(arrs)

    def body(*refs):
        for cp in _ag_copies(refs[:n], split, refs[n], refs[n + 1]):
            cp.wait_send()
            cp.wait_recv()

    return pl.pallas_call(
        body, name=name, out_shape=[pltpu.HBM(a.shape, a.dtype) for a in arrs],
        in_specs=[HBM] * n + [SEM, SEM] + [ANY] * len(afters), out_specs=[HBM] * n,
        input_output_aliases={i: i for i in range(n)},
        compiler_params=pltpu.CompilerParams(has_side_effects=EFFECT))(*arrs, send, recv, *afters)


def _ag_forward(name, arrs, axes):
    n = len(arrs)

    def half_shape(t):
        _, r, cc = arrs[t].shape
        return (r // 2, cc) if axes[t] == 1 else (r, cc // 2)

    def body(*refs):
        g = refs[n:2 * n]
        bufs = refs[2 * n:3 * n]
        fsend, frecv, lsem = refs[3 * n:]
        x, y, c, chips = _place()
        _pair_barrier()
        for t in range(n):
            pend = [None, None]
            for k, chip in enumerate(chips):
                jk = 2 * chip[0] + chip[1]
                slot = k % 2
                if pend[slot] is not None:
                    pend[slot].wait_send()
                part = _half_of(g[t], jk, c, axes[t])
                ld = pltpu.make_async_copy(part, bufs[t].at[slot], lsem.at[2 * t + slot])
                ld.start()
                ld.wait()
                cp = pltpu.make_async_remote_copy(
                    src_ref=bufs[t].at[slot], dst_ref=part, send_sem=fsend.at[2 * t + slot],
                    recv_sem=frecv.at[t], device_id=(x, y, 1 - c), device_id_type=MESH)
                cp.start()
                pend[slot] = cp
            for cp in pend:
                cp.wait_send()
        for t in range(n):
            hr, hc = half_shape(t)
            passed = g[t].at[pl.ds(0, 3), pl.ds(0, hr), pl.ds(0, hc)]
            pltpu.make_async_remote_copy(
                src_ref=passed, dst_ref=passed, send_sem=fsend.at[2 * t], recv_sem=frecv.at[t],
                device_id=(x, y, 1 - c), device_id_type=MESH).wait_recv()

    dma = pltpu.SemaphoreType.DMA
    scratch = [pltpu.VMEM((2,) + half_shape(t), BF16) for t in range(n)]
    scratch += [dma((2 * n,)), dma((n,)), dma((2 * n,))]
    return pl.pallas_call(
        body, name=name, in_specs=[ANY] * n, out_specs=[ANY] * n, out_shape=[_sds(a.shape, a.dtype) for a in arrs],
        scratch_shapes=scratch, input_output_aliases={t: t for t in range(n)},
        compiler_params=pltpu.CompilerParams(has_side_effects=True, collective_id=PAIR_ID,
                                             vmem_limit_bytes=VMEM_LIMIT))(*arrs)


def _rs_begin(tag, dws, by_cols, pf):
    ps = []
    for t, dw in enumerate(dws):
        if isinstance(dw, (list, tuple)):
            dw, got = dw
        else:
            got = _pair_send("rs_pair_%s_%d" % (tag, t), dw, pf, by_cols[t])
        ps.append(_add2_bf16("rs_add2_%s_%d" % (tag, t), dw, got, pf, by_cols[t]))
    send, recv, ps_thru, lands, token = _rs_chips_start("rs_chips_start_" + tag, ps)
    return (tag, send, recv, ps_thru, lands, by_cols), token


def _rs_end(handle, afters, pf):
    tag, send, recv, ps, lands, by_cols = handle
    ps, got2 = _rs_chips_wait("rs_chips_wait_" + tag, send, recv, ps, lands, afters)
    return [_add4_join("rs_add4_%s_%d" % (tag, t), p, g2, pf, by_cols[t]) for t, (p, g2) in enumerate(zip(ps, got2))]


def _allreduce_small(pack):
    r = pack.shape[0]
    rp = -(-r // 16) * 16
    half = rp // 2
    if rp != r:
        pack = jnp.pad(pack, ((0, rp - r), (0, 0)))

    def body(p_ref, o_ref, sib, chipbuf, s1, r1, s2, r2, s3, r3):
        x, y, c, chips = _place()
        j = 2 * x + y
        mine = pl.ds(pl.multiple_of(c * half, 8), half)
        other = pl.ds(pl.multiple_of((1 - c) * half, 8), half)
        swap = pltpu.make_async_remote_copy(src_ref=p_ref.at[other, :], dst_ref=sib, send_sem=s1, recv_sem=r1,
                                            device_id=_sibling(), device_id_type=MESH)
        swap.start()
        swap.wait()
        chipbuf[j] = p_ref[mine, :] + sib[...]
        cps = []
        for k, chip in enumerate(chips):
            cp = pltpu.make_async_remote_copy(src_ref=chipbuf.at[j], dst_ref=chipbuf.at[j], send_sem=s2.at[k],
                                              recv_sem=r2.at[k], device_id=(*chip, c), device_id_type=MESH)
            cp.start()
            cps.append(cp)
        for k, chip in enumerate(chips):
            jk = 2 * chip[0] + chip[1]
            pltpu.make_async_remote_copy(src_ref=chipbuf.at[jk], dst_ref=chipbuf.at[jk], send_sem=s2.at[k],
                                         recv_sem=r2.at[k], device_id=(*chip, c), device_id_type=MESH).wait_recv()
        for cp in cps:
            cp.wait_send()
        o_ref[mine, :] = ((chipbuf[0] + chipbuf[1]) + chipbuf[2]) + chipbuf[3]
        join = pltpu.make_async_remote_copy(src_ref=o_ref.at[mine, :], dst_ref=o_ref.at[mine, :], send_sem=s3,
                                            recv_sem=r3, device_id=_sibling(), device_id_type=MESH)
        join.start()
        join.wait()

    dma = pltpu.SemaphoreType.DMA
    vm = pl.BlockSpec(memory_space=pltpu.VMEM)
    out = pl.pallas_call(
        body, name="allreduce_small", in_specs=[vm], out_specs=vm, out_shape=_sds((rp, LANE), F32),
        scratch_shapes=[pltpu.VMEM((half, LANE), F32), pltpu.VMEM((NCHIP, half, LANE), F32),
                        dma(()), dma(()), dma((3,)), dma((3,)), dma(()), dma(())],
        compiler_params=pltpu.CompilerParams(has_side_effects=True))(pack)
    return out[:r]


def _in_weights(win_g):
    full = jnp.concatenate([win_g[s] for s in range(NCHIP)], axis=0)
    wqkv = full[:2048]
    og = full[2048:3072]
    gates = jnp.pad(full[3072:3080], ((0, LANE - 8), (0, 0)))
    u = full[3080:4104]
    gb = full[4104:5128]
    gc = full[5128:6152]
    return wqkv, jnp.stack([og, u, gb, gc]), gates


def _in_grads(dwqkv, dwe, dwgt):
    full = jnp.concatenate([dwqkv, dwe[0], dwgt[:8], dwe[1], dwe[2], dwe[3]], axis=0)
    sw = DIN // NCHIP
    return jnp.stack([full[s * sw:(s + 1) * sw] for s in range(NCHIP)])


def _layer_fwd(l, h, get, small):
    lp = h.shape[0]
    th = lp // 2
    wqkv, we, wgt = _in_weights(get("in", h)[0])
    nmw, bias, mnw, cw, nfw = small
    tag = "_l%d" % l
    hn = _norm_fwd("norm_mix" + tag, h, nmw)
    qkv = _mm("proj_qkv" + tag, "nt", hn, wqkv,
              pl.BlockSpec((lp, D), lambda i, j, k: (0, 0)), pl.BlockSpec((512, D), lambda i, j, k: (j, 0)),
              pl.BlockSpec((lp, 512), lambda i, j, k: (0, j)), _sds((lp, 2048), BF16), (1, 4, 1))
    e = _mm("proj_e" + tag, "nt", hn, we,
            pl.BlockSpec((lp, D), lambda i, j, k: (0, 0)), pl.BlockSpec((None, 512, D), lambda i, j, k: (j // 2, j % 2, 0)),
            pl.BlockSpec((None, lp, 512), lambda i, j, k: (j // 2, 0, j % 2)), _sds((4, lp, 1024), BF16), (1, 8, 1))
    gpre = _mm("proj_gates" + tag, "nt", hn, wgt,
               pl.BlockSpec((lp, D), lambda i, j, k: (0, 0)), pl.BlockSpec((LANE, D), lambda i, j, k: (0, 0)),
               pl.BlockSpec((lp, LANE), lambda i, j, k: (0, 0)), _sds((lp, LANE), F32), (1, 1, 1))
    grow, mcol, ccol = _gate_prep("gate_prep" + tag, gpre, bias)
    ht = _mlstm_fwd("mlstm_fwd" + tag, qkv, grow, mcol, ccol, 4)
    mix = _mix_fwd("mix_fwd" + tag, ht, e, mnw, cw)
    wout_g = get("out", mix)[0]
    wout = wout_g.reshape(D, D)
    h1 = _mm("out_proj" + tag, "nn", mix, wout,
             pl.BlockSpec((None, th, 1024), lambda i, j, k: (k, i, 0)),
             pl.BlockSpec((1024, 1024), lambda i, j, k: (k, j)),
             pl.BlockSpec((th, 1024), lambda i, j, k: (i, j)), _sds((lp, D), F32), (2, 2, 2),
             acc_shape=(th, 1024), res=h, res_spec=pl.BlockSpec((th, 1024), lambda i, j, k: (i, j)))
    wg_g, wu_g = get("ffn", h1)
    hf = _norm_fwd("norm_ffn" + tag, h1, nfw)
    g, u, a = _ffn_up("ffn_up" + tag, hf, wg_g, wu_g)
    wd_g = get("down", a)[0]
    h2 = _ffn_down("ffn_down" + tag, a, wd_g, h1)
    saved = (h, hn, qkv, e, gpre, grow, mcol, ccol, ht, mix, h1, hf, g, u, a)
    return h2, saved, (wqkv, we, wgt, wout_g, wg_g, wu_g, wd_g)


def _layer_bwd(l, dh2, dh2b, saved, wts, small, ffn_done, dep):
    h, hn, qkv, e, gpre, grow, mcol, ccol, ht, mix, h1, hf, g, u, a = saved
    wqkv, we, wgt, wout_g, wg_g, wu_g, wd_g = wts
    nmw, bias, mnw, cw, nfw = small
    lp = h.shape[0]
    th = lp // 2
    tk = DFF // NCHIP
    tag = "_l%d" % l
    half_rows = lambda i, j, k: (i, j)

    dwd = _dw_pair("dw_down" + tag, a, dh2b, True, dep)
    dg, du = _ffn_bwd_act("ffn_bwd_act" + tag, dh2b, wd_g, g, u, dep)
    dwg = _dw_pair("dw_gate" + tag, hf, dg, False)
    dwu = _dw_pair("dw_up" + tag, hf, du, False)
    dhf = _ffn_bwd_in("dhf" + tag, dg, du, wg_g, wu_g)
    dh1, dh1b, dnfw = _norm_bwd("norm_ffn_bwd" + tag, dhf, h1, nfw, dh2)
    ffn_started = ffn_done([dwg, dwu, dwd])

    dwout = _mm("dw_out" + tag, "tn", mix, dh1b,
                pl.BlockSpec((None, lp, 1024), lambda i, j, k: (i, 0, 0)), pl.BlockSpec((lp, 1024), lambda i, j, k: (0, j)),
                pl.BlockSpec((1024, 1024), half_rows), _sds((D, D), BF16), (2, 2, 1), dep=ffn_started)
    wout = wout_g.reshape(D, D)
    dmix = _mm("dmix" + tag, "nt", dh1b, wout,
               pl.BlockSpec((th, D), lambda i, j, k: (i, 0)), pl.BlockSpec((1024, D), lambda i, j, k: (j, 0)),
               pl.BlockSpec((th, 1024), half_rows), _sds((lp, D), F32), (2, 2, 1), dep=ffn_started)
    dht, de, dmnw, dcw = _mix_bwd("mix_bwd" + tag, dmix, ht, e, mnw, cw)
    dq, dk, dv, dgrow, dfx = _mlstm_bwd("mlstm_bwd" + tag, qkv, grow, mcol, ccol, ht, dht, 4)
    dgp, dgpb, dbias = _gate_bwd("gate_bwd" + tag, gpre, bias, dgrow, dfx)
    del dgp
    dqkv = jnp.concatenate([dq, dk, dv], axis=1)

    hn_cols = pl.BlockSpec((lp, 1024), lambda i, j, k: (0, j))
    dwqkv = _mm("dw_qkv" + tag, "tn", dqkv, hn, pl.BlockSpec((lp, 1024), lambda i, j, k: (0, i)), hn_cols,
                pl.BlockSpec((1024, 1024), half_rows), _sds((2048, D), BF16), (2, 2, 1))
    dwe = _mm("dw_e" + tag, "tn", de, hn, pl.BlockSpec((None, lp, 1024), lambda i, j, k: (i, 0, 0)), hn_cols,
              pl.BlockSpec((None, 1024, 1024), lambda i, j, k: (i, 0, j)), _sds((4, 1024, D), BF16), (4, 2, 1))
    dwgt = _mm("dw_gates" + tag, "tn", dgpb, hn, pl.BlockSpec((lp, LANE), lambda i, j, k: (0, 0)), hn_cols,
               pl.BlockSpec((LANE, 1024), lambda i, j, k: (0, j)), _sds((LANE, D), BF16), (1, 2, 1))
    dwin = _in_grads(dwqkv, dwe, dwgt)

    dhn = _mm("dhn_qkv" + tag, "nn", dqkv, wqkv,
              pl.BlockSpec((th, 2048), lambda i, j, k: (i, 0)), pl.BlockSpec((2048, 1024), lambda i, j, k: (0, j)),
              pl.BlockSpec((th, 1024), half_rows), _sds((lp, D), F32), (2, 2, 1))
    dhn = _mm("dhn_e" + tag, "nn", de, we,
              pl.BlockSpec((None, th, 1024), lambda i, j, k: (k, i, 0)),
              pl.BlockSpec((None, 1024, 1024), lambda i, j, k: (k, 0, j)),
              pl.BlockSpec((th, 1024), half_rows), _sds((lp, D), F32), (2, 2, 4), acc_shape=(th, 1024),
              res=dhn, res_spec=pl.BlockSpec((th, 1024), half_rows))
    dhn = _mm("dhn_gates" + tag, "nn", dgpb, wgt,
              pl.BlockSpec((th, LANE), lambda i, j, k: (i, 0)), pl.BlockSpec((LANE, 1024), lambda i, j, k: (0, j)),
              pl.BlockSpec((th, 1024), half_rows), _sds((lp, D), F32), (2, 2, 1),
              res=dhn, res_spec=pl.BlockSpec((th, 1024), half_rows))
    dh0, dh0b, dnmw = _norm_bwd("norm_mix_bwd" + tag, dhn, h, nmw, dh1)

    pieces = [dwin, dwout.reshape(NCHIP, D // NCHIP, D)]
    smalls = (dnmw[0], dbias[0, :8], dcw[:3], dmnw[0], dnfw[0])
    return dh0, dh0b, pieces, smalls


def _pack_rows(parts):
    rows = []
    for p in parts:
        f = p.reshape(-1)
        pad = (-f.shape[0]) % LANE
        if pad:
            f = jnp.pad(f, (0, pad))
        rows.append(f.reshape(-1, LANE))
    r = jnp.concatenate(rows, axis=0)
    pad = (-r.shape[0]) % 8
    if pad:
        r = jnp.pad(r, ((0, pad), (0, 0)))
    return r


def _unpack_rows(pack, shapes):
    out, r0 = [], 0
    for s in shapes:
        n = 1
        for d in s:
            n *= d
        nr = -(-n // LANE)
        out.append(pack[r0:r0 + nr].reshape(-1)[:n].reshape(s))
        r0 += nr
    return out


def kernel(x, meta_tokens, norm_mix_w, w_in, b_gates, conv_w, mlstm_norm_w, w_out, norm_ffn_w, w_gate, w_up, w_down, norm_final_w, loss_target, m_meta_tokens, m_norm_mix_w, m_w_in, m_b_gates, m_conv_w, m_mlstm_norm_w, m_w_out, m_norm_ffn_w, m_w_gate, m_w_up, m_w_down, m_norm_final_w, v_meta_tokens, v_norm_mix_w, v_w_in, v_b_gates, v_conv_w, v_mlstm_norm_w, v_w_out, v_norm_ffn_w, v_w_gate, v_w_up, v_w_down, v_norm_final_w):
    seq = x.shape[1]
    n_real = N_META + seq
    lp = -(-n_real // LANE) * LANE
    xi, yi, ci = lax.axis_index("x"), lax.axis_index("y"), lax.axis_index("c")
    jchip = 2 * xi + yi
    pf = jnp.stack([ci, jchip, 2 * (1 - xi) + yi, 2 * xi + (1 - yi), 2 * (1 - xi) + (1 - yi)]).astype(jnp.int32)

    big = {"w_in": w_in, "w_out": w_out, "w_gate": w_gate, "w_up": w_up, "w_down": w_down}
    cast = {n: [_cast_into("cast_%s_l%d" % (n, l), w, l, pf) for l in range(DEPTH)] for n, w in big.items()
            if n != "w_in"}
    in_t = lambda a: jnp.transpose(a, (2, 0, 1))
    cast["w_in"] = [_cast_into_t("cast_w_in_l%d" % l, in_t(w_in), l, pf) for l in range(DEPTH)]
    conv_flat = jnp.pad(conv_w.reshape(DEPTH * 3, CW // NCHIP), ((0, 8 - DEPTH * 3), (0, 0)))

    def own_slot(a):
        return lax.dynamic_update_slice(jnp.zeros((NCHIP,) + a.shape, a.dtype), a[None], (jchip, 0, 0))

    groups, splits = [], []
    for l in range(DEPTH):
        groups.append([cast["w_in"][l]] + ([own_slot(meta_tokens), own_slot(conv_flat)] if l == 0 else []))
        splits.append([2] + ([0, 0] if l == 0 else []))
        groups += [[cast["w_out"][l]], [cast["w_gate"][l], cast["w_up"][l]], [cast["w_down"][l]]]
        splits += [[1], [1, 1], [1]]
    group_of = {"in": 0, "out": 1, "ffn": 2, "down": 3}
    sems, arrs, _ = _ag_start("ag_start_0", groups[:1], splits[:1])
    sems_rest, arrs_rest, all_started = _ag_start("ag_start_1", groups[1:], splits[1:])
    sems, arrs = sems + sems_rest, arrs + arrs_rest

    def gathered(gi, afters):
        got = _ag_wait("ag_wait_%d" % gi, arrs[gi], splits[gi], sems[gi][0], sems[gi][1], afters)
        axes = [s for s in splits[gi] if s]
        return list(_ag_forward("ag_forward_%d" % gi, got[:len(axes)], axes)) + list(got[len(axes):])

    win0_g, meta_g, conv_g = gathered(0, [all_started])
    meta_full = jnp.concatenate([meta_g[s] for s in range(NCHIP)], axis=1)
    conv_full = jnp.concatenate([conv_g[s][:DEPTH * 3] for s in range(NCHIP)], axis=1)
    conv_full = conv_full.reshape(DEPTH, 3, CW)

    bias_rows = jnp.pad(b_gates, ((0, 0), (0, LANE - 8)))
    smalls = []
    for l in range(DEPTH):
        smalls.append((norm_mix_w[l][None], bias_rows[l][None], mlstm_norm_w[l][None],
                       jnp.pad(conv_full[l], ((0, 5), (0, 0))), norm_ffn_w[l][None]))

    h = jnp.concatenate([meta_full, x[0], jnp.zeros((lp - n_real, D), F32)], axis=0)
    saved, wts = [], []
    for l in range(DEPTH):
        def get(which, after, l=l):
            if l == 0 and which == "in":
                return [win0_g]
            return gathered(4 * l + group_of[which], [after])

        h, sv, wt = _layer_fwd(l, h, get, smalls[l])
        saved.append(sv)
        wts.append(wt)
    tgt = jnp.pad(loss_target[0], ((N_META, lp - n_real), (0, 0)))
    dh, dhb, dnorm_final, loss_part = _loss_head(h, tgt, norm_final_w[None], n_real)

    names = ["w_in", "w_out", "w_gate", "w_up", "w_down"]
    params = {"w_in": (w_in, m_w_in, v_w_in), "w_out": (w_out, m_w_out, v_w_out), "w_gate": (w_gate, m_w_gate, v_w_gate),
              "w_up": (w_up, m_w_up, v_w_up), "w_down": (w_down, m_w_down, v_w_down)}
    big_out = {n: None for n in names}
    small_grads = [None] * DEPTH
    g_in = [None] * DEPTH

    def finish(l, group, handle, afters):
        for n, (g_mine, g_theirs) in zip(group, _rs_end(handle, afters, pf)):
            if n == "w_in":
                g_in[l] = (g_mine, g_theirs)
                continue
            w, m, v = params[n]
            big_out[n] = _adamw_layer("adamw_%s_l%d" % (n, l), l, g_mine, g_theirs, w, m, v, big_out[n], pf)
        return [big_out[n][3] for n in group if n != "w_in"]

    groups = []
    token = None
    for l in reversed(range(DEPTH)):
        def ffn_done(pieces, l=l):
            handle, tok = _rs_begin("l%df" % l, pieces, [False] * 3, pf)
            groups.append((l, names[2:], handle))
            return tok

        dh, dhb, pieces, small_grads[l] = _layer_bwd(l, dh, dhb, saved[l], wts[l], smalls[l], ffn_done, token)
        handle, token = _rs_begin("l%dm" % l, pieces, [True, False], pf)
        groups.append((l, names[:2], handle))
    afters = [token]
    for l, group, handle in groups:
        afters = finish(l, group, handle, afters)
    big_out["w_in"] = [jnp.transpose(a, (1, 2, 0))
                       for a in _adamw_t("adamw_w_in", g_in, in_t(w_in), in_t(m_w_in), in_t(v_w_in), pf)]

    dnmw = jnp.stack([small_grads[l][0] for l in range(DEPTH)])
    dbias = jnp.stack([small_grads[l][1] for l in range(DEPTH)])
    dconv = jnp.stack([small_grads[l][2] for l in range(DEPTH)])
    dmnw = jnp.stack([small_grads[l][3] for l in range(DEPTH)])
    dnfw = jnp.stack([small_grads[l][4] for l in range(DEPTH)])
    part_shapes = [(N_META, D), (DEPTH, D), (DEPTH, 8), (DEPTH, 3, CW), (DEPTH, MW), (DEPTH, D), (D,), (LANE,)]
    pack = _pack_rows([dh[:N_META], dnmw, dbias, dconv, dmnw, dnfw, dnorm_final[0], loss_part[0]])
    tot = _unpack_rows(_allreduce_small(pack), part_shapes)
    g_meta_full, g_nmw, g_bias, g_conv_full, g_mnw, g_nfw, g_final, loss_row = tot
    mcols = D // NCHIP
    ccols = CW // NCHIP
    g_meta = lax.dynamic_slice_in_dim(g_meta_full, jchip * mcols, mcols, axis=1)
    g_conv = lax.dynamic_slice_in_dim(g_conv_full, jchip * ccols, ccols, axis=2)
    sm_g = [g_meta, g_nmw, g_bias, g_conv, g_mnw, g_nfw, g_final]
    sm_w = [meta_tokens, norm_mix_w, b_gates, conv_w, mlstm_norm_w, norm_ffn_w, norm_final_w]
    sm_m = [m_meta_tokens, m_norm_mix_w, m_b_gates, m_conv_w, m_mlstm_norm_w, m_norm_ffn_w, m_norm_final_w]
    sm_v = [v_meta_tokens, v_norm_mix_w, v_b_gates, v_conv_w, v_mlstm_norm_w, v_norm_ffn_w, v_norm_final_w]
    sm_shapes = [w.shape for w in sm_w]
    d_p, m_p, v_p = _adamw_flat(_pack_rows(sm_g), _pack_rows(sm_w), _pack_rows(sm_m), _pack_rows(sm_v))
    sm_d = _unpack_rows(d_p, sm_shapes)
    sm_nm = _unpack_rows(m_p, sm_shapes)
    sm_nv = _unpack_rows(v_p, sm_shapes)

    loss = loss_row[0]
    grad_x = dh[N_META:n_real][None]

    def ordered(sm, which):
        bo = {n: big_out[n][which] for n in names}
        return [sm[0], sm[1], bo["w_in"], sm[2], sm[3], sm[4], bo["w_out"], sm[5], bo["w_gate"], bo["w_up"], bo["w_down"], sm[6]]

    return (loss, grad_x, *ordered(sm_g, 0), *ordered(sm_d, 1), *ordered(sm_nm, 2), *ordered(sm_nv, 3))
```

```python
import functools

import jax
import jax.numpy as jnp
from jax import lax
from jax.experimental import pallas as pl
from jax.experimental.pallas import tpu as pltpu

F32 = jnp.float32
BF16 = jnp.bfloat16

D = 2048
N_META = 16
HEADS = 4
DQK = 128
DV = 256
MW = HEADS * DV
CW = D - MW
QKW = HEADS * DQK
DFF = 5632
DIN = 6152
NCHIP = 4
DEPTH = 2
GATE_CAP = 15.0
EPS = 1e-6
QK_SCALE = DQK ** -0.5
LANE = 128
VMEM_LIMIT = 60 * 1024 * 1024

ADAM_LR = 0.001
ADAM_B1 = 0.9
ADAM_B2 = 0.999
ADAM_EPS = 1e-08
ADAM_WD = 0.01
ADAM_STEP = 10

MESH = pl.DeviceIdType.MESH
ANY = pl.BlockSpec(memory_space=pl.ANY)


def _cp(sem):
    return pltpu.CompilerParams(dimension_semantics=sem, vmem_limit_bytes=VMEM_LIMIT)


def _sds(shape, dtype):
    return jax.ShapeDtypeStruct(shape, dtype)


_DN = {"nn": (((1,), (0,)), ((), ())), "nt": (((1,), (1,)), ((), ())), "tn": (((0,), (0,)), ((), ()))}


def _mm(name, kind, a, b, a_spec, b_spec, o_spec, out_shape, grid, acc_shape=None, res=None, res_spec=None,
        dep=None):
    nk = grid[2]
    has_res = res is not None
    n_in = 2 + has_res + (dep is not None)

    def body(*refs):
        a_ref, b_ref = refs[0], refs[1]
        r_ref = refs[2] if has_res else None
        o_ref = refs[n_in]
        p = lax.dot_general(a_ref[...], b_ref[...], _DN[kind], preferred_element_type=F32)
        if nk == 1:
            if r_ref is not None:
                p = p + r_ref[...]
            o_ref[...] = p.astype(o_ref.dtype)
        else:
            acc = refs[-1]
            k = pl.program_id(2)

            @pl.when(k == 0)
            def _():
                acc[...] = p

            @pl.when(jnp.logical_and(k > 0, k < nk - 1))
            def _():
                acc[...] += p

            @pl.when(k == nk - 1)
            def _():
                r = acc[...] + p
                if r_ref is not None:
                    r = r + r_ref[...]
                o_ref[...] = r.astype(o_ref.dtype)

    ins = [a, b] + ([res] if has_res else []) + ([dep] if dep is not None else [])
    in_specs = [a_spec, b_spec] + ([res_spec] if has_res else []) + ([ANY] if dep is not None else [])
    scratch = [pltpu.VMEM(acc_shape, F32)] if nk > 1 else []
    return pl.pallas_call(
        body, name=name, grid=grid, in_specs=in_specs, out_specs=o_spec, out_shape=out_shape,
        scratch_shapes=scratch, compiler_params=_cp(("parallel", "parallel", "arbitrary")))(*ins)


def _norm_fwd(name, h, w):
    lp = h.shape[0]
    tm = lp // 4

    def body(h_ref, w_ref, o_ref):
        x = h_ref[...]
        r = lax.rsqrt(jnp.mean(x * x, axis=1, keepdims=True) + EPS)
        o_ref[...] = (x * r * w_ref[...]).astype(BF16)

    return pl.pallas_call(
        body, name=name, grid=(4,),
        in_specs=[pl.BlockSpec((tm, D), lambda i: (i, 0)), pl.BlockSpec((1, D), lambda i: (0, 0))],
        out_specs=pl.BlockSpec((tm, D), lambda i: (i, 0)), out_shape=_sds((lp, D), BF16),
        compiler_params=_cp(("parallel",)))(h, w)


def _norm_bwd(name, dy, h, w, dres):
    lp = h.shape[0]
    tm = lp // 8

    def body(dy_ref, h_ref, w_ref, dres_ref, dh_ref, dhb_ref, dw_ref):
        x = h_ref[...]
        r = lax.rsqrt(jnp.mean(x * x, axis=1, keepdims=True) + EPS)
        xh = x * r
        dy_v = dy_ref[...]
        dxh = dy_v * w_ref[...]
        dx = r * (dxh - xh * jnp.mean(dxh * xh, axis=1, keepdims=True))
        dh = dres_ref[...] + dx
        dh_ref[...] = dh
        dhb_ref[...] = dh.astype(BF16)

        @pl.when(pl.program_id(0) == 0)
        def _():
            dw_ref[...] = jnp.zeros_like(dw_ref)

        dw_ref[0:1, :] += jnp.sum(dy_v * xh, axis=0, keepdims=True)

    row = pl.BlockSpec((tm, D), lambda i: (i, 0))
    return pl.pallas_call(
        body, name=name, grid=(8,),
        in_specs=[row, row, pl.BlockSpec((1, D), lambda i: (0, 0)), row],
        out_specs=[row, row, pl.BlockSpec((8, D), lambda i: (0, 0))],
        out_shape=[_sds((lp, D), F32), _sds((lp, D), BF16), _sds((8, D), F32)],
        compiler_params=_cp(("arbitrary",)))(dy, h, w, dres)


def _loss_head(h, tgt, w, n_real):
    lp = h.shape[0]
    tm = lp // 8

    def body(h_ref, t_ref, w_ref, dh_ref, dhb_ref, dw_ref, loss_ref):
        i = pl.program_id(0)
        x = h_ref[...]
        r = lax.rsqrt(jnp.mean(x * x, axis=1, keepdims=True) + EPS)
        xh = x * r
        wv = w_ref[...]
        row = i * tm + lax.broadcasted_iota(jnp.int32, (tm, 1), 0)
        valid = jnp.logical_and(row >= N_META, row < n_real)
        err = jnp.where(valid, xh * wv - t_ref[...], 0.0)
        dy_v = err * (1.0 / D)
        dxh = dy_v * wv
        dx = r * (dxh - xh * jnp.mean(dxh * xh, axis=1, keepdims=True))
        dh_ref[...] = dx
        dhb_ref[...] = dx.astype(BF16)

        @pl.when(i == 0)
        def _():
            dw_ref[...] = jnp.zeros_like(dw_ref)
            loss_ref[...] = jnp.zeros_like(loss_ref)

        dw_ref[0:1, :] += jnp.sum(dy_v * xh, axis=0, keepdims=True)
        part = jnp.sum(jnp.sum(err * err, axis=1, keepdims=True), axis=0, keepdims=True) * (0.5 / D)
        loss_ref[...] += jnp.broadcast_to(part, loss_ref.shape)

    row = pl.BlockSpec((tm, D), lambda i: (i, 0))
    return pl.pallas_call(
        body, name="loss_head", grid=(8,),
        in_specs=[row, row, pl.BlockSpec((1, D), lambda i: (0, 0))],
        out_specs=[row, row, pl.BlockSpec((8, D), lambda i: (0, 0)), pl.BlockSpec((8, LANE), lambda i: (0, 0))],
        out_shape=[_sds((lp, D), F32), _sds((lp, D), BF16), _sds((8, D), F32), _sds((8, LANE), F32)],
        compiler_params=_cp(("arbitrary",)))(h, tgt, w)


def _ffn_up(name, hf, wg4, wu4):
    lp = hf.shape[0]
    tm = lp // 4
    tn = DFF // NCHIP

    def body(x_ref, wg_ref, wu_ref, g_ref, u_ref, a_ref):
        x = x_ref[...]
        g = jnp.dot(x, wg_ref[...], preferred_element_type=F32)
        u = jnp.dot(x, wu_ref[...], preferred_element_type=F32)
        g_ref[...] = g.astype(BF16)
        u_ref[...] = u.astype(BF16)
        a_ref[...] = (g * jax.nn.sigmoid(g) * u).astype(BF16)

    wspec = pl.BlockSpec((None, D, tn), lambda j, i: (j, 0, 0))
    ospec = pl.BlockSpec((tm, tn), lambda j, i: (i, j))
    o = _sds((lp, DFF), BF16)
    return pl.pallas_call(
        body, name=name, grid=(NCHIP, 4),
        in_specs=[pl.BlockSpec((tm, D), lambda j, i: (i, 0)), wspec, wspec],
        out_specs=[ospec, ospec, ospec], out_shape=[o, o, o],
        compiler_params=_cp(("parallel", "parallel")))(hf, wg4, wu4)


def _dhn(name, dqkv, de, dgpb, wqkv_t, we_t, wgt_t):
    lp = dqkv.shape[0]
    th = lp // 2
    tn = 512

    def body(q_ref, e_ref, g_ref, wq_ref, we_ref, wg_ref, o_ref):
        acc = jnp.dot(q_ref[...], wq_ref[...], preferred_element_type=F32)
        for c in range(4):
            acc = acc + jnp.dot(e_ref[c], we_ref[c], preferred_element_type=F32)
        o_ref[...] = acc + jnp.dot(g_ref[...], wg_ref[...], preferred_element_type=F32)

    return pl.pallas_call(
        body, name=name, grid=(2, D // tn),
        in_specs=[pl.BlockSpec((th, 2048), lambda i, j: (i, 0)), pl.BlockSpec((4, th, 1024), lambda i, j: (0, i, 0)),
                  pl.BlockSpec((th, LANE), lambda i, j: (i, 0)), pl.BlockSpec((2048, tn), lambda i, j: (0, j)),
                  pl.BlockSpec((4, 1024, tn), lambda i, j: (0, 0, j)), pl.BlockSpec((LANE, tn), lambda i, j: (0, j))],
        out_specs=pl.BlockSpec((th, tn), lambda i, j: (i, j)), out_shape=_sds((lp, D), F32),
        compiler_params=_cp(("parallel", "parallel")))(dqkv, de, dgpb, wqkv_t, we_t, wgt_t)


def _ffn_down(name, a, wd4, h1):
    lp = a.shape[0]
    th = lp // 2
    tn = 512

    def body(a_ref, w_ref, r_ref, o_ref):
        w = w_ref[...].reshape(DFF, tn)
        o_ref[...] = r_ref[...] + jnp.dot(a_ref[...], w, preferred_element_type=F32)

    row = pl.BlockSpec((th, tn), lambda i, j: (i, j))
    return pl.pallas_call(
        body, name=name, grid=(2, D // tn),
        in_specs=[pl.BlockSpec((th, DFF), lambda i, j: (i, 0)),
                  pl.BlockSpec((NCHIP, DFF // NCHIP, tn), lambda i, j: (0, 0, j)), row],
        out_specs=row, out_shape=_sds((lp, D), F32), compiler_params=_cp(("parallel", "parallel")))(a, wd4, h1)


def _ffn_bwd_act(name, dhb, wd4, g, u, dep=None):
    lp = dhb.shape[0]
    tm = lp // 4
    tn = DFF // NCHIP
    deps = [] if dep is None else [dep]

    def body(d_ref, w_ref, g_ref, u_ref, *rest):
        dg_ref, du_ref = rest[len(deps):]
        da = lax.dot_general(d_ref[...], w_ref[...], _DN["nt"], preferred_element_type=F32)
        gv = g_ref[...].astype(F32)
        uv = u_ref[...].astype(F32)
        sg = jax.nn.sigmoid(gv)
        dg_ref[...] = (da * uv * (sg * (1.0 + gv * (1.0 - sg)))).astype(BF16)
        du_ref[...] = (da * (gv * sg)).astype(BF16)

    ospec = pl.BlockSpec((tm, tn), lambda j, i: (i, j))
    o = _sds((lp, DFF), BF16)
    return pl.pallas_call(
        body, name=name, grid=(NCHIP, 4),
        in_specs=[pl.BlockSpec((tm, D), lambda j, i: (i, 0)),
                  pl.BlockSpec((None, tn, D), lambda j, i: (j, 0, 0)), ospec, ospec] + [ANY] * len(deps),
        out_specs=[ospec, ospec], out_shape=[o, o],
        compiler_params=_cp(("parallel", "parallel")))(dhb, wd4, g, u, *deps)


def _ffn_bwd_in(name, dg, du, wg4, wu4):
    lp = dg.shape[0]
    th = lp // 2
    tk = DFF // NCHIP

    def body(dg_ref, du_ref, wg_ref, wu_ref, o_ref, acc):
        k = pl.program_id(2)
        p = (lax.dot_general(dg_ref[...], wg_ref[...], _DN["nt"], preferred_element_type=F32)
             + lax.dot_general(du_ref[...], wu_ref[...], _DN["nt"], preferred_element_type=F32))

        @pl.when(k == 0)
        def _():
            acc[...] = p

        @pl.when(jnp.logical_and(k > 0, k < NCHIP - 1))
        def _():
            acc[...] += p

        @pl.when(k == NCHIP - 1)
        def _():
            o_ref[...] = acc[...] + p

    a_spec = pl.BlockSpec((th, tk), lambda i, j, k: (i, k))
    w_spec = pl.BlockSpec((None, 1024, tk), lambda i, j, k: (k, j, 0))
    return pl.pallas_call(
        body, name=name, grid=(2, 2, NCHIP), in_specs=[a_spec, a_spec, w_spec, w_spec],
        out_specs=pl.BlockSpec((th, 1024), lambda i, j, k: (i, j)), out_shape=_sds((lp, D), F32),
        scratch_shapes=[pltpu.VMEM((th, 1024), F32)],
        compiler_params=_cp(("parallel", "parallel", "arbitrary")))(dg, du, wg4, wu4)


def _shift_rows(x, d, row):
    return jnp.where(row >= d, pltpu.roll(x, d, axis=0), 0.0)


def _scan_steps(lp):
    d = 1
    while d < lp:
        yield d
        d *= 2


def _gate_values(pre):
    t = GATE_CAP * jnp.tanh(pre * (1.0 / GATE_CAP))
    lf = jnp.minimum(t, 0.0) - jnp.log(1.0 + jnp.exp(-jnp.abs(t)))
    return t, lf


def _gate_prep(name, gates_pre, bias):
    lp = gates_pre.shape[0]

    def body(p_ref, b_ref, grow_ref, m_ref, c_ref):
        pre = p_ref[...] + b_ref[...]
        lane = lax.broadcasted_iota(jnp.int32, (lp, LANE), 1)
        row = lax.broadcasted_iota(jnp.int32, (lp, LANE), 0)
        t, lf = _gate_values(pre)
        f = jnp.where(jnp.logical_and(lane >= HEADS, lane < 2 * HEADS), lf, 0.0)
        for d in _scan_steps(lp):
            f = f + _shift_rows(f, d, row)
        fs = pltpu.roll(f, LANE - HEADS, axis=1)
        g = jnp.where(lane < HEADS, t - fs, 0.0)
        m = g
        for d in _scan_steps(lp):
            m = jnp.maximum(m, jnp.where(row >= d, pltpu.roll(m, d, axis=0), m))
        grow_ref[...] = g.T
        m_ref[...] = m
        c_ref[...] = jnp.where(lane < HEADS, -fs - m, 0.0)

    full = pl.BlockSpec((lp, LANE), lambda: (0, 0))
    return pl.pallas_call(
        body, name=name, in_specs=[full, pl.BlockSpec((1, LANE), lambda: (0, 0))],
        out_specs=[pl.BlockSpec((LANE, lp), lambda: (0, 0)), full, full],
        out_shape=[_sds((LANE, lp), F32), _sds((lp, LANE), F32), _sds((lp, LANE), F32)],
        compiler_params=pltpu.CompilerParams(vmem_limit_bytes=VMEM_LIMIT))(gates_pre, bias)


def _pick_lane(blk, h):
    lane = lax.broadcasted_iota(jnp.int32, blk.shape, 1)
    return jnp.sum(jnp.where(lane == h, blk, 0.0), axis=1, keepdims=True)


def _mlstm_weights(q, k, grow, mcol, i, bq, nk):
    s = lax.dot_general(q, k, _DN["nt"], preferred_element_type=F32) * QK_SCALE
    row = i * bq + lax.broadcasted_iota(jnp.int32, (bq, 1), 0)
    col = lax.broadcasted_iota(jnp.int32, (1, nk), 1)
    a = jnp.where(col <= row, jnp.exp(jnp.minimum(grow - mcol, 0.0)), 0.0)
    return s, a


def _per_query_tile(i, nq, bq, lp, compute):
    for ii in range(nq):
        nk = min(lp, -(-((ii + 1) * bq) // LANE) * LANE)
        pl.when(i == ii)(functools.partial(compute, nk))


def _mlstm_fwd(name, qkv, grow, mcol_all, ccol_all, nq):
    lp = qkv.shape[0]
    bq = lp // nq

    def body(q_ref, k_ref, v_ref, grow_ref, m_ref, c_ref, o_ref):
        h = pl.program_id(0)
        i = pl.program_id(1)
        mcol = _pick_lane(m_ref[...], h)
        ccol = _pick_lane(c_ref[...], h)

        def compute(nk):
            grow_h = grow_ref[pl.ds(h, 1), 0:nk]
            s, a = _mlstm_weights(q_ref[...], k_ref[0:nk, :], grow_h, mcol, i, bq, nk)
            p = a * s
            den = jnp.sum(p, axis=1, keepdims=True)
            num = jnp.dot(p.astype(BF16), v_ref[0:nk, :], preferred_element_type=F32)
            o_ref[...] = num / jnp.maximum(jnp.abs(den), jnp.exp(ccol))

        _per_query_tile(i, nq, bq, lp, compute)

    return pl.pallas_call(
        body, name=name, grid=(HEADS, nq),
        in_specs=[pl.BlockSpec((bq, DQK), lambda h, i: (i, h)),
                  pl.BlockSpec((lp, DQK), lambda h, i: (0, HEADS + h)),
                  pl.BlockSpec((lp, DV), lambda h, i: (0, HEADS + h)),
                  pl.BlockSpec((8, lp), lambda h, i: (0, 0)),
                  pl.BlockSpec((bq, LANE), lambda h, i: (i, 0)),
                  pl.BlockSpec((bq, LANE), lambda h, i: (i, 0))],
        out_specs=pl.BlockSpec((bq, DV), lambda h, i: (i, h)),
        out_shape=_sds((lp, MW), F32),
        compiler_params=_cp(("parallel", "parallel")))(qkv, qkv, qkv, grow, mcol_all, ccol_all)


def _mlstm_bwd(name, qkv, grow, mcol_all, ccol_all, ht, dht, nq):
    lp = qkv.shape[0]
    bq = lp // nq

    def body(q_ref, k_ref, v_ref, grow_ref, m_ref, c_ref, ht_ref, dht_ref,
             dq_ref, dk_ref, dv_ref, dgrow_ref, dfx_ref, dkt_acc, dvt_acc):
        h = pl.program_id(0)
        i = pl.program_id(1)

        @pl.when(jnp.logical_and(h == 0, i == 0))
        def _():
            dgrow_ref[...] = jnp.zeros_like(dgrow_ref)
            dfx_ref[...] = jnp.zeros_like(dfx_ref)

        @pl.when(i == 0)
        def _():
            dkt_acc[...] = jnp.zeros_like(dkt_acc)
            dvt_acc[...] = jnp.zeros_like(dvt_acc)

        mcol = _pick_lane(m_ref[...], h)
        ccol = _pick_lane(c_ref[...], h)

        def compute(nk):
            q = q_ref[...]
            k = k_ref[0:nk, :]
            v = v_ref[0:nk, :]
            grow_h = grow_ref[pl.ds(h, 1), 0:nk]
            s, a = _mlstm_weights(q, k, grow_h, mcol, i, bq, nk)
            p = a * s
            den = jnp.sum(p, axis=1, keepdims=True)
            clamp = jnp.exp(ccol)
            active = jnp.abs(den) < clamp
            dd = jnp.maximum(jnp.abs(den), clamp)
            dht_v = dht_ref[...]
            hdh = jnp.sum(dht_v * ht_ref[...], axis=1, keepdims=True)
            dn = (dht_v / dd).astype(BF16)
            dden = jnp.where(active, 0.0, -(hdh / dd) * jnp.sign(den))
            dp = lax.dot_general(dn, v, _DN["nt"], preferred_element_type=F32) + dden
            rmat = dp * p
            dgrow_ref[pl.ds(h, 1), 0:nk] += jnp.sum(rmat, axis=0, keepdims=True)
            ds = (dp * a * QK_SCALE).astype(BF16)
            dq_ref[...] = jnp.dot(ds, k, preferred_element_type=F32).astype(BF16)
            dkt_acc[:, 0:nk] += lax.dot_general(q, ds, _DN["tn"], preferred_element_type=F32)
            dvt_acc[:, 0:nk] += lax.dot_general(dn, p.astype(BF16), _DN["tn"], preferred_element_type=F32)
            lane = lax.broadcasted_iota(jnp.int32, (bq, LANE), 1)
            r0 = pl.multiple_of(i * bq, 16)
            dfx_ref[pl.ds(r0, bq), :] += jnp.where(lane == h, jnp.sum(rmat, axis=1, keepdims=True), 0.0)

        _per_query_tile(i, nq, bq, lp, compute)

        @pl.when(i == nq - 1)
        def _():
            dk_ref[...] = dkt_acc[...].T.astype(BF16)
            dv_ref[...] = dvt_acc[...].T.astype(BF16)

    return pl.pallas_call(
        body, name=name, grid=(HEADS, nq),
        in_specs=[pl.BlockSpec((bq, DQK), lambda h, i: (i, h)),
                  pl.BlockSpec((lp, DQK), lambda h, i: (0, HEADS + h)),
                  pl.BlockSpec((lp, DV), lambda h, i: (0, HEADS + h)),
                  pl.BlockSpec((8, lp), lambda h, i: (0, 0)),
                  pl.BlockSpec((bq, LANE), lambda h, i: (i, 0)),
                  pl.BlockSpec((bq, LANE), lambda h, i: (i, 0)),
                  pl.BlockSpec((bq, DV), lambda h, i: (i, h)),
                  pl.BlockSpec((bq, DV), lambda h, i: (i, h))],
        out_specs=[pl.BlockSpec((bq, DQK), lambda h, i: (i, h)),
                   pl.BlockSpec((lp, DQK), lambda h, i: (0, h)),
                   pl.BlockSpec((lp, DV), lambda h, i: (0, h)),
                   pl.BlockSpec((LANE, lp), lambda h, i: (0, 0)),
                   pl.BlockSpec((lp, LANE), lambda h, i: (0, 0))],
        out_shape=[_sds((lp, QKW), BF16), _sds((lp, QKW), BF16), _sds((lp, MW), BF16),
                   _sds((LANE, lp), F32), _sds((lp, LANE), F32)],
        scratch_shapes=[pltpu.VMEM((DQK, lp), F32), pltpu.VMEM((DV, lp), F32)],
        compiler_params=_cp(("arbitrary", "arbitrary")))(qkv, qkv, qkv, grow, mcol_all, ccol_all, ht, dht)


def _gate_bwd(name, gates_pre, bias, dgrow, dfx):
    lp = gates_pre.shape[0]

    def body(p_ref, b_ref, dgrow_ref, dfx_ref, dg_ref, dgb_ref, db_ref):
        pre = p_ref[...] + b_ref[...]
        lane = lax.broadcasted_iota(jnp.int32, (lp, LANE), 1)
        row = lax.broadcasted_iota(jnp.int32, (lp, LANE), 0)
        th = jnp.tanh(pre * (1.0 / GATE_CAP))
        t = GATE_CAP * th
        dgc = jnp.where(lane < HEADS, dgrow_ref[...].T, 0.0)
        df = jnp.where(lane < HEADS, dfx_ref[...] - dgc, 0.0)
        for d in _scan_steps(lp):
            df = df + jnp.where(row < lp - d, pltpu.roll(df, lp - d, axis=0), 0.0)
        dlf = pltpu.roll(df, HEADS, axis=1)
        dt = jnp.where(lane < HEADS, dgc, dlf * jax.nn.sigmoid(-t))
        dpre = jnp.where(lane < 2 * HEADS, dt * (1.0 - th * th), 0.0)
        dg_ref[...] = dpre
        dgb_ref[...] = dpre.astype(BF16)
        db_ref[...] = jnp.broadcast_to(jnp.sum(dpre, axis=0, keepdims=True), db_ref.shape)

    full = pl.BlockSpec((lp, LANE), lambda: (0, 0))
    return pl.pallas_call(
        body, name=name,
        in_specs=[full, pl.BlockSpec((1, LANE), lambda: (0, 0)), pl.BlockSpec((LANE, lp), lambda: (0, 0)), full],
        out_specs=[full, full, pl.BlockSpec((8, LANE), lambda: (0, 0))],
        out_shape=[_sds((lp, LANE), F32), _sds((lp, LANE), BF16), _sds((8, LANE), F32)],
        compiler_params=pltpu.CompilerParams(vmem_limit_bytes=VMEM_LIMIT))(gates_pre, bias, dgrow, dfx)


CB = 256


def _e_specs(lp):
    return [pl.BlockSpec((None, lp, CB), functools.partial(lambda c, j: (c, 0, j), c)) for c in range(4)]


def _mix_fwd(name, ht, e, mnw, cw):
    lp = ht.shape[0]

    def body(ht_ref, og_ref, u_ref, gb_ref, gc_ref, mnw_ref, cw_ref, o_ref):
        x = ht_ref[...]
        r = lax.rsqrt(jnp.mean(x * x, axis=1, keepdims=True) + EPS)
        o_ref[0] = (jax.nn.sigmoid(og_ref[...].astype(F32)) * (x * r * mnw_ref[...])).astype(BF16)
        row = lax.broadcasted_iota(jnp.int32, (lp, CB), 0)
        a = gc_ref[...].astype(F32) * u_ref[...].astype(F32)
        conv = cw_ref[2:3, :] * a + cw_ref[1:2, :] * _shift_rows(a, 1, row) + cw_ref[0:1, :] * _shift_rows(a, 2, row)
        o_ref[1] = (gb_ref[...].astype(F32) * conv).astype(BF16)

    col = pl.BlockSpec((lp, CB), lambda j: (0, j))
    return pl.pallas_call(
        body, name=name, grid=(4,),
        in_specs=[col] + _e_specs(lp) + [pl.BlockSpec((1, CB), lambda j: (0, j)), pl.BlockSpec((8, CB), lambda j: (0, j))],
        out_specs=pl.BlockSpec((2, lp, CB), lambda j: (0, 0, j)), out_shape=_sds((2, lp, MW), BF16),
        compiler_params=_cp(("parallel",)))(ht, e, e, e, e, mnw, cw)


def _mix_bwd(name, dmix, ht, e, mnw, cw):
    lp = ht.shape[0]

    def body(dhm_ref, dhc_ref, ht_ref, og_ref, u_ref, gb_ref, gc_ref, mnw_ref, cw_ref,
             dht_ref, de_ref, dmnw_ref, dcw_ref):
        x = ht_ref[...]
        r = lax.rsqrt(jnp.mean(x * x, axis=1, keepdims=True) + EPS)
        xh = x * r
        w = mnw_ref[...]
        sg = jax.nn.sigmoid(og_ref[...].astype(F32))
        dhm = dhm_ref[...]
        de_ref[0] = (dhm * (xh * w) * (sg * (1.0 - sg))).astype(BF16)
        dn = dhm * sg
        dmnw_ref[...] = jnp.broadcast_to(jnp.sum(dn * xh, axis=0, keepdims=True), dmnw_ref.shape)
        dxh = dn * w
        dht_ref[...] = r * (dxh - xh * jnp.mean(dxh * xh, axis=1, keepdims=True))

        row = lax.broadcasted_iota(jnp.int32, (lp, CB), 0)
        uv = u_ref[...].astype(F32)
        gcv = gc_ref[...].astype(F32)
        gbv = gb_ref[...].astype(F32)
        a = gcv * uv
        a1 = _shift_rows(a, 1, row)
        a2 = _shift_rows(a, 2, row)
        dhc = dhc_ref[...]
        conv = cw_ref[2:3, :] * a + cw_ref[1:2, :] * a1 + cw_ref[0:1, :] * a2
        de_ref[2] = (dhc * conv).astype(BF16)
        dconv = dhc * gbv
        dcw_ref[...] = jnp.zeros_like(dcw_ref)
        dcw_ref[0:1, :] = jnp.sum(dconv * a2, axis=0, keepdims=True)
        dcw_ref[1:2, :] = jnp.sum(dconv * a1, axis=0, keepdims=True)
        dcw_ref[2:3, :] = jnp.sum(dconv * a, axis=0, keepdims=True)
        up1 = jnp.where(row < lp - 1, pltpu.roll(dconv, lp - 1, axis=0), 0.0)
        up2 = jnp.where(row < lp - 2, pltpu.roll(dconv, lp - 2, axis=0), 0.0)
        da = cw_ref[2:3, :] * dconv + cw_ref[1:2, :] * up1 + cw_ref[0:1, :] * up2
        de_ref[1] = (da * gcv).astype(BF16)
        de_ref[3] = (da * uv).astype(BF16)

    col = pl.BlockSpec((lp, CB), lambda j: (0, j))
    small = pl.BlockSpec((8, CB), lambda j: (0, j))
    return pl.pallas_call(
        body, name=name, grid=(4,),
        in_specs=[col, pl.BlockSpec((lp, CB), lambda j: (0, 4 + j)), col] + _e_specs(lp)
                 + [pl.BlockSpec((1, CB), lambda j: (0, j)), small],
        out_specs=[col, pl.BlockSpec((4, lp, CB), lambda j: (0, 0, j)), small, small],
        out_shape=[_sds((lp, MW), F32), _sds((4, lp, MW), BF16), _sds((8, MW), F32), _sds((8, CW), F32)],
        compiler_params=_cp(("parallel",)))(dmix, dmix, ht, e, e, e, e, mnw, cw)


def _row_tile(r, c, itemsize, budget=1536 * 1024, mult=16):
    best = None
    for t in range(mult, r + 1, mult):
        if r % t == 0 and t * c * itemsize <= budget:
            best = t
    if best is None:
        best = r
    return best


def _grid_spec(grid, in_specs, out_specs, scratch=()):
    return pltpu.PrefetchScalarGridSpec(num_scalar_prefetch=1, grid=grid, in_specs=in_specs,
                                        out_specs=out_specs, scratch_shapes=list(scratch))


def _cast_into(name, w, layer, pf):
    _, r, c = w.shape
    tr = _row_tile(r, c, 4)

    def body(pf_ref, x_ref, o_ref):
        o_ref[...] = x_ref[...].astype(BF16)

    return pl.pallas_call(
        body, name=name, out_shape=_sds((NCHIP, r, c), BF16),
        grid_spec=_grid_spec((r // tr,), [pl.BlockSpec((None, tr, c), lambda i, pf: (layer, i, 0))],
                             pl.BlockSpec((None, tr, c), lambda i, pf: (pf[1], i, 0))),
        compiler_params=_cp(("parallel",)))(pf, w)


TCOL = 256


def _cast_into_t(name, w_t, layer, pf):
    c, nl, r = w_t.shape

    def body(pf_ref, x_ref, o_ref):
        o_ref[...] = x_ref[:, layer, :].astype(BF16)

    return pl.pallas_call(
        body, name=name, out_shape=_sds((NCHIP, c, r), BF16),
        grid_spec=_grid_spec((r // TCOL,), [pl.BlockSpec((c, nl, TCOL), lambda i, pf: (0, 0, i))],
                             pl.BlockSpec((None, c, TCOL), lambda i, pf: (pf[1], 0, i))),
        compiler_params=_cp(("parallel",)))(pf, w_t)


def _add2_bf16(name, dw, got, pf, by_cols=False):
    n4, r2, c2 = got.shape

    def body(pf_ref, a_ref, b_ref, o_ref):
        o_ref[...] = (a_ref[...].astype(F32) + b_ref[...].astype(F32)).astype(BF16)

    if by_cols:
        nch = c2 // TCOL
        spec = pl.BlockSpec((None, r2, TCOL), lambda s, i, pf: (s, 0, i))
        mine = pl.BlockSpec((None, r2, TCOL), lambda s, i, pf: (s, 0, pf[0] * nch + i))
    else:
        tr = _row_tile(r2, c2, 4)
        nch = r2 // tr
        spec = pl.BlockSpec((None, tr, c2), lambda s, i, pf: (s, i, 0))
        mine = pl.BlockSpec((None, tr, c2), lambda s, i, pf: (s, pf[0] * nch + i, 0))
    return pl.pallas_call(
        body, name=name, out_shape=_sds((n4, r2, c2), BF16),
        grid_spec=_grid_spec((n4, nch), [mine, spec], spec),
        compiler_params=_cp(("parallel", "parallel")))(pf, dw, got)


def _adam_math(w, g, m, v):
    m2 = ADAM_B1 * m + (1.0 - ADAM_B1) * g
    v2 = ADAM_B2 * v + (1.0 - ADAM_B2) * (g * g)
    m_hat = m2 / (1.0 - ADAM_B1 ** ADAM_STEP)
    v_hat = v2 / (1.0 - ADAM_B2 ** ADAM_STEP)
    delta = -ADAM_LR * (m_hat / (jnp.sqrt(v_hat) + ADAM_EPS) + ADAM_WD * w)
    return delta, m2, v2


def _adamw_layer(name, layer, g_mine, g_theirs, w, m, v, prev, pf):
    _, r, c = w.shape
    r2 = r // 2
    tr = _row_tile(r2, c, 4, budget=1024 * 1024, mult=8)
    nch = r2 // tr
    n_alias = 0 if prev is None else 4

    def body(*refs):
        pf_ref, gm_ref, gt_ref, w_ref, m_ref, v_ref = refs[:6]
        go_ref, d_ref, mo_ref, vo_ref = refs[6 + n_alias:]
        mine = (pl.program_id(0) // nch) == pf_ref[0]
        gv = jnp.where(mine, gm_ref[...], gt_ref[...])
        delta, m2, v2 = _adam_math(w_ref[...], gv, m_ref[...], v_ref[...])
        go_ref[...] = gv
        d_ref[...] = delta
        mo_ref[...] = m2
        vo_ref[...] = v2

    slab = pl.BlockSpec((None, tr, c), lambda i, pf: (layer, i, 0))
    ins = [g_mine, g_theirs, w, m, v] + (list(prev) if prev is not None else [])
    in_specs = [pl.BlockSpec((tr, c), lambda i, pf: (jnp.clip(i - pf[0] * nch, 0, nch - 1), 0)),
                pl.BlockSpec((tr, c), lambda i, pf: (jnp.clip(i - (1 - pf[0]) * nch, 0, nch - 1), 0)),
                slab, slab, slab] + [ANY] * n_alias
    o = _sds(w.shape, F32)
    return pl.pallas_call(
        body, name=name, out_shape=[o] * 4, grid_spec=_grid_spec((2 * nch,), in_specs, [slab] * 4),
        input_output_aliases={6 + k: k for k in range(n_alias)},
        compiler_params=_cp(("parallel",)))(pf, *ins)


def _adamw_t(name, gs, w_t, m_t, v_t, pf):
    c, nl, r = w_t.shape
    ta = LANE
    nch = (r // 2) // ta

    def body(*refs):
        pf_ref = refs[0]
        g_refs = refs[1:1 + 2 * nl]
        w_ref, m_ref, v_ref, go_ref, d_ref, mo_ref, vo_ref = refs[1 + 2 * nl:]
        mine = (pl.program_id(0) // nch) == pf_ref[0]
        gv = jnp.stack([jnp.where(mine, g_refs[2 * l][...], g_refs[2 * l + 1][...]) for l in range(nl)], axis=1)
        delta, m2, v2 = _adam_math(w_ref[...], gv, m_ref[...], v_ref[...])
        go_ref[...] = gv
        d_ref[...] = delta
        mo_ref[...] = m2
        vo_ref[...] = v2

    both = pl.BlockSpec((c, nl, ta), lambda i, pf: (0, 0, i))
    g_specs = []
    for l in range(nl):
        g_specs += [pl.BlockSpec((c, ta), lambda i, pf: (0, jnp.clip(i - pf[0] * nch, 0, nch - 1))),
                    pl.BlockSpec((c, ta), lambda i, pf: (0, jnp.clip(i - (1 - pf[0]) * nch, 0, nch - 1)))]
    o = _sds(w_t.shape, F32)
    flat_g = [a for pair in gs for a in pair]
    return pl.pallas_call(
        body, name=name, out_shape=[o] * 4, grid_spec=_grid_spec((2 * nch,), g_specs + [both] * 3, [both] * 4),
        compiler_params=_cp(("parallel",)))(pf, *flat_g, w_t, m_t, v_t)


def _adamw_flat(g, w, m, v):
    def body(g_ref, w_ref, m_ref, v_ref, d_ref, mo_ref, vo_ref):
        delta, m2, v2 = _adam_math(w_ref[...], g_ref[...], m_ref[...], v_ref[...])
        d_ref[...] = delta
        mo_ref[...] = m2
        vo_ref[...] = v2

    o = _sds(w.shape, F32)
    return pl.pallas_call(body, name="adamw_small", out_shape=[o, o, o])(g, w, m, v)


def _place():
    x, y, c = lax.axis_index("x"), lax.axis_index("y"), lax.axis_index("c")
    chips = [(1 - x, y), (x, 1 - y), (1 - x, 1 - y)]
    return x, y, c, chips


HBM = pl.BlockSpec(memory_space=pltpu.HBM)
SEM = pl.BlockSpec(memory_space=pltpu.SEMAPHORE)
EFFECT = pltpu.SideEffectType.DATAFLOW_SIDE_EFFECTING


def _in_hbm(a):
    return pltpu.with_memory_space_constraint(a, pltpu.HBM)


def _rs_chips_copies(ins, lands, send, recv):
    x, y, c, chips = _place()
    cps = []
    for t in range(len(ins)):
        for k, chip in enumerate(chips):
            jk = 2 * chip[0] + chip[1]
            cps.append(pltpu.make_async_remote_copy(
                src_ref=ins[t].at[jk], dst_ref=lands[t].at[k], send_sem=send.at[3 * t + k],
                recv_sem=recv.at[3 * t + k], device_id=(*chip, c), device_id_type=MESH))
    return cps


def _rs_chips_start(name, ps):
    n = len(ps)

    def body(*refs):
        ins, lands = refs[:n], refs[n:2 * n]
        send, recv = refs[2 * n], refs[2 * n + 1]
        token = refs[-1]
        for cp in _rs_chips_copies(ins, lands, send, recv):
            cp.start()
        token[...] = jnp.zeros_like(token)

    dma = pltpu.SemaphoreType.DMA
    lands = [lax.empty((3,) + p.shape[1:], BF16) for p in ps]
    out_shape = ([dma((3 * n,)), dma((3 * n,))] + [pltpu.HBM(p.shape, BF16) for p in ps]
                 + [pltpu.HBM(z.shape, BF16) for z in lands] + [_sds((8, LANE), F32)])
    outs = pl.pallas_call(
        body, name=name, out_shape=out_shape, in_specs=[HBM] * (2 * n),
        out_specs=[SEM, SEM] + [HBM] * (2 * n) + [pl.BlockSpec(memory_space=pltpu.VMEM)],
        input_output_aliases={i: 2 + i for i in range(2 * n)},
        compiler_params=pltpu.CompilerParams(has_side_effects=EFFECT))(
            *[_in_hbm(p) for p in ps], *[_in_hbm(z) for z in lands])
    return outs[0], outs[1], outs[2:2 + n], outs[2 + n:2 + 2 * n], outs[-1]


def _rs_chips_wait(name, send, recv, ps, lands, afters):
    n = len(ps)

    def body(*refs):
        ins, zones = refs[:n], refs[n:2 * n]
        send_ref, recv_ref = refs[2 * n], refs[2 * n + 1]
        for cp in _rs_chips_copies(ins, zones, send_ref, recv_ref):
            cp.wait_send()
            cp.wait_recv()

    outs = pl.pallas_call(
        body, name=name, out_shape=[pltpu.HBM(p.shape, BF16) for p in ps] + [pltpu.HBM(z.shape, BF16) for z in lands],
        in_specs=[HBM] * (2 * n) + [SEM, SEM] + [ANY] * len(afters), out_specs=[HBM] * (2 * n),
        input_output_aliases={i: i for i in range(2 * n)},
        compiler_params=pltpu.CompilerParams(has_side_effects=EFFECT))(*ps, *lands, send, recv, *afters)
    return outs[:n], outs[n:]


def _sibling():
    x, y, c, _ = _place()
    return (x, y, 1 - c)


PAIR_ID = 0


def _pair_barrier():
    bar = pltpu.get_barrier_semaphore()
    pl.semaphore_signal(bar, inc=1, device_id=_sibling(), device_id_type=MESH)
    pl.semaphore_wait(bar, 1)


def _pair_send(name, dw, pf, by_cols=False):
    n4, r, c = dw.shape
    blk = (1, r, c // 2) if by_cols else (1, r // 2, c)
    idx = (lambda s, pf: (s, 0, 1 - pf[0])) if by_cols else (lambda s, pf: (s, 1 - pf[0], 0))

    def body(pf_ref, x_ref, got_ref, ssem, rsem):
        s = pl.program_id(0)
        pl.when(s == 0)(_pair_barrier)
        cp = pltpu.make_async_remote_copy(src_ref=x_ref, dst_ref=got_ref.at[pl.ds(s, 1)], send_sem=ssem,
                                          recv_sem=rsem, device_id=_sibling(), device_id_type=MESH)
        cp.start()
        cp.wait_send()

        @pl.when(s == n4 - 1)
        def _():
            pltpu.make_async_remote_copy(src_ref=got_ref, dst_ref=got_ref, send_sem=ssem, recv_sem=rsem,
                                         device_id=_sibling(), device_id_type=MESH).wait_recv()

    dma = pltpu.SemaphoreType.DMA
    return pl.pallas_call(
        body, name=name, out_shape=_sds((n4,) + blk[1:], BF16),
        grid_spec=_grid_spec((n4,), [pl.BlockSpec(blk, idx)], ANY, scratch=[dma(()), dma(())]),
        compiler_params=pltpu.CompilerParams(dimension_semantics=("arbitrary",), has_side_effects=True,
                                             collective_id=PAIR_ID, vmem_limit_bytes=VMEM_LIMIT))(pf, dw)


def _dw_pair(name, x, dy, down, dep=None):
    lp = x.shape[0]
    tk = DFF // NCHIP
    if down:
        grid = (NCHIP, 2)
        x_spec = pl.BlockSpec((lp, tk), lambda s, j: (0, s))
        dy_spec = pl.BlockSpec((lp, 1024), lambda s, j: (0, j))
        o_spec = pl.BlockSpec((1, tk, 1024), lambda s, j: (s, 0, j))
        dw_shape, got_shape = (NCHIP, tk, D), (NCHIP, tk // 2, D)
    else:
        grid = (2, NCHIP)
        x_spec = pl.BlockSpec((lp, 1024), lambda i, s: (0, i))
        dy_spec = pl.BlockSpec((lp, tk), lambda i, s: (0, s))
        o_spec = pl.BlockSpec((1, 1024, tk), lambda i, s: (s, i, 0))
        dw_shape, got_shape = (NCHIP, D, tk), (NCHIP, D // 2, tk)

    send_shape = (tk // 2, 1024) if down else (1024, tk)
    n_sends = 2 * NCHIP if down else NCHIP

    deps = [] if dep is None else [dep]

    def body(x_ref, dy_ref, *rest):
        o_ref, got_ref, sbuf, ssem, rsem = rest[len(deps):]
        g0, g1 = pl.program_id(0), pl.program_id(1)
        pl.when(jnp.logical_and(g0 == 0, g1 == 0))(_pair_barrier)
        tile = lax.dot_general(x_ref[...], dy_ref[...], _DN["tn"], preferred_element_type=F32).astype(BF16)
        o_ref[0] = tile
        _, _, c, _ = _place()

        def send(n, part, dst):
            for s in (0, 1):
                cp = pltpu.make_async_remote_copy(src_ref=sbuf.at[pl.ds(s, 1)], dst_ref=dst, send_sem=ssem.at[s],
                                                  recv_sem=rsem, device_id=_sibling(), device_id_type=MESH)

                @pl.when(n % 2 == s)
                def _():
                    pl.when(n >= 2)(cp.wait_send)
                    sbuf[s] = part()
                    cp.start()

            @pl.when(n == n_sends - 1)
            def _():
                for s in (0, 1):
                    pltpu.make_async_remote_copy(src_ref=sbuf.at[pl.ds(s, 1)], dst_ref=dst, send_sem=ssem.at[s],
                                                 recv_sem=rsem, device_id=_sibling(), device_id_type=MESH).wait_send()

        if down:
            hr = tk // 2
            rows = pl.ds(pl.multiple_of((1 - c) * hr, 16), hr)
            send(g0 * 2 + g1, lambda: o_ref[0, rows, :],
                 got_ref.at[pl.ds(g0, 1), :, pl.ds(pl.multiple_of(g1 * 1024, LANE), 1024)])
        else:
            pl.when(g0 == 1 - c)(lambda: send(g1, lambda: tile, got_ref.at[pl.ds(g1, 1)]))

        @pl.when(jnp.logical_and(g0 == grid[0] - 1, g1 == grid[1] - 1))
        def _():
            pltpu.make_async_remote_copy(src_ref=got_ref, dst_ref=got_ref, send_sem=ssem.at[0], recv_sem=rsem,
                                         device_id=_sibling(), device_id_type=MESH).wait_recv()

    dma = pltpu.SemaphoreType.DMA
    return pl.pallas_call(
        body, name=name, grid=grid, in_specs=[x_spec, dy_spec] + [ANY] * len(deps), out_specs=[o_spec, ANY],
        out_shape=[_sds(dw_shape, BF16), _sds(got_shape, BF16)],
        scratch_shapes=[pltpu.VMEM((2,) + send_shape, BF16), dma((2,)), dma(())],
        compiler_params=pltpu.CompilerParams(dimension_semantics=("arbitrary", "arbitrary"), has_side_effects=True,
                                             collective_id=PAIR_ID, vmem_limit_bytes=VMEM_LIMIT))(x, dy, *deps)


def _add4_join(name, p, got, pf, by_cols=False):
    n4, r2, c = p.shape
    if by_cols:
        tr, nch = r2, c // TCOL
        blk, idx = (r2, TCOL), (lambda i: (0, i))
    else:
        tr = _row_tile(r2, c, 4)
        nch = r2 // tr
        blk, idx = (tr, c), (lambda i: (i, 0))

    def body(pf_ref, p_ref, g_ref, mine_ref, theirs_ref, sbuf, ssem, rsem):
        i = pl.program_id(0)
        pl.when(i == 0)(_pair_barrier)
        s = p_ref[...].astype(F32)
        for k in range(3):
            s = s + g_ref[k].astype(F32)
        mine_ref[...] = s
        if by_cols:
            dst = theirs_ref.at[:, pl.ds(pl.multiple_of(i * TCOL, LANE), TCOL)]
        else:
            dst = theirs_ref.at[pl.ds(pl.multiple_of(i * tr, 8), tr), :]

        def copy(slot):
            return pltpu.make_async_remote_copy(src_ref=sbuf.at[slot], dst_ref=dst, send_sem=ssem.at[slot],
                                                recv_sem=rsem, device_id=_sibling(), device_id_type=MESH)

        for slot in (0, 1):
            @pl.when(i % 2 == slot)
            def _():
                pl.when(i >= 2)(copy(slot).wait_send)
                sbuf[slot] = s
                copy(slot).start()

        @pl.when(i == nch - 1)
        def _():
            for slot in range(min(nch, 2)):
                copy(slot).wait_send()
            pltpu.make_async_remote_copy(src_ref=theirs_ref, dst_ref=theirs_ref, send_sem=ssem.at[0], recv_sem=rsem,
                                         device_id=_sibling(), device_id_type=MESH).wait_recv()

    dma = pltpu.SemaphoreType.DMA
    o = _sds((r2, c), F32)
    return pl.pallas_call(
        body, name=name, out_shape=[o, o],
        grid_spec=_grid_spec((nch,), [pl.BlockSpec((None,) + blk, lambda i, pf: (pf[1],) + idx(i)),
                                      pl.BlockSpec((3,) + blk, lambda i, pf: (0,) + idx(i))],
                             [pl.BlockSpec(blk, lambda i, pf: idx(i)), ANY],
                             scratch=[pltpu.VMEM((2,) + blk, F32), dma((2,)), dma(())]),
        compiler_params=pltpu.CompilerParams(dimension_semantics=("arbitrary",), has_side_effects=True,
                                             collective_id=PAIR_ID, vmem_limit_bytes=VMEM_LIMIT))(pf, p, got)


def _half_of(g, slot, which, axis):
    half = g.shape[axis] // 2
    if axis == 1:
        return g.at[slot, pl.ds(which * half, half), :]
    return g.at[slot, :, pl.ds(which * half, half)]


def _ag_copies(arrs, split, send, recv):
    x, y, c, chips = _place()
    j = 2 * x + y
    cps = []
    for t, g in enumerate(arrs):
        piece = _half_of(g, j, c, split[t]) if split[t] else g.at[j]
        for k, chip in enumerate(chips):
            cps.append(pltpu.make_async_remote_copy(
                src_ref=piece, dst_ref=piece, send_sem=send.at[3 * t + k], recv_sem=recv.at[3 * t + k],
                device_id=(*chip, c), device_id_type=MESH))
    return cps


def _ag_start(name, groups, splits):
    sizes = [len(g) for g in groups]
    flat = [a for g in groups for a in g]
    n = len(flat)

    def body(*refs):
        ins = refs[:n]
        sems = refs[n:n + 2 * len(groups)]
        o = 0
        for gi, sz in enumerate(sizes):
            for cp in _ag_copies(ins[o:o + sz], splits[gi], sems[2 * gi], sems[2 * gi + 1]):
                cp.start()
            o += sz
        refs[-1][...] = jnp.zeros_like(refs[-1])

    dma = pltpu.SemaphoreType.DMA
    sem_shapes = [dma((3 * sz,)) for sz in sizes for _ in range(2)]
    outs = pl.pallas_call(
        body, name=name, out_shape=sem_shapes + [pltpu.HBM(a.shape, a.dtype) for a in flat] + [_sds((8, LANE), F32)],
        in_specs=[HBM] * n,
        out_specs=[SEM] * len(sem_shapes) + [HBM] * n + [pl.BlockSpec(memory_space=pltpu.VMEM)],
        input_output_aliases={i: len(sem_shapes) + i for i in range(n)},
        compiler_params=pltpu.CompilerParams(has_side_effects=EFFECT))(*[_in_hbm(a) for a in flat])
    sems, arrs, o = [], [], len(sem_shapes)
    for gi, sz in enumerate(sizes):
        sems.append((outs[2 * gi], outs[2 * gi + 1]))
        arrs.append(list(outs[o:o + sz]))
        o += sz
    return sems, arrs, outs[-1]


def _ag_wait(name, arrs, split, send, recv, afters):
    n = len(arrs)

    def body(*refs):
        for cp in _ag_copies(refs[:n], split, refs[n], refs[n + 1]):
            cp.wait_send()
            cp.wait_recv()

    return pl.pallas_call(
        body, name=name, out_shape=[pltpu.HBM(a.shape, a.dtype) for a in arrs],
        in_specs=[HBM] * n + [SEM, SEM] + [ANY] * len(afters), out_specs=[HBM] * n,
        input_output_aliases={i: i for i in range(n)},
        compiler_params=pltpu.CompilerParams(has_side_effects=EFFECT))(*arrs, send, recv, *afters)


def _ag_forward(name, arrs, axes):
    n = len(arrs)

    def half_shape(t):
        _, r, cc = arrs[t].shape
        return (r // 2, cc) if axes[t] == 1 else (r, cc // 2)

    def body(*refs):
        g = refs[n:2 * n]
        bufs = refs[2 * n:3 * n]
        fsend, frecv, lsem = refs[3 * n:]
        x, y, c, chips = _place()
        _pair_barrier()
        for t in range(n):
            pend = [None, None]
            for k, chip in enumerate(chips):
                jk = 2 * chip[0] + chip[1]
                slot = k % 2
                if pend[slot] is not None:
                    pend[slot].wait_send()
                part = _half_of(g[t], jk, c, axes[t])
                ld = pltpu.make_async_copy(part, bufs[t].at[slot], lsem.at[2 * t + slot])
                ld.start()
                ld.wait()
                cp = pltpu.make_async_remote_copy(
                    src_ref=bufs[t].at[slot], dst_ref=part, send_sem=fsend.at[2 * t + slot],
                    recv_sem=frecv.at[t], device_id=(x, y, 1 - c), device_id_type=MESH)
                cp.start()
                pend[slot] = cp
            for cp in pend:
                cp.wait_send()
        for t in range(n):
            hr, hc = half_shape(t)
            passed = g[t].at[pl.ds(0, 3), pl.ds(0, hr), pl.ds(0, hc)]
            pltpu.make_async_remote_copy(
                src_ref=passed, dst_ref=passed, send_sem=fsend.at[2 * t], recv_sem=frecv.at[t],
                device_id=(x, y, 1 - c), device_id_type=MESH).wait_recv()

    dma = pltpu.SemaphoreType.DMA
    scratch = [pltpu.VMEM((2,) + half_shape(t), BF16) for t in range(n)]
    scratch += [dma((2 * n,)), dma((n,)), dma((2 * n,))]
    return pl.pallas_call(
        body, name=name, in_specs=[ANY] * n, out_specs=[ANY] * n, out_shape=[_sds(a.shape, a.dtype) for a in arrs],
        scratch_shapes=scratch, input_output_aliases={t: t for t in range(n)},
        compiler_params=pltpu.CompilerParams(has_side_effects=True, collective_id=PAIR_ID,
                                             vmem_limit_bytes=VMEM_LIMIT))(*arrs)


def _rs_begin(tag, dws, by_cols, pf):
    ps = []
    for t, dw in enumerate(dws):
        if isinstance(dw, (list, tuple)):
            dw, got = dw
        else:
            got = _pair_send("rs_pair_%s_%d" % (tag, t), dw, pf, by_cols[t])
        ps.append(_add2_bf16("rs_add2_%s_%d" % (tag, t), dw, got, pf, by_cols[t]))
    send, recv, ps_thru, lands, token = _rs_chips_start("rs_chips_start_" + tag, ps)
    return (tag, send, recv, ps_thru, lands, by_cols), token


def _rs_end(handle, afters, pf):
    tag, send, recv, ps, lands, by_cols = handle
    ps, got2 = _rs_chips_wait("rs_chips_wait_" + tag, send, recv, ps, lands, afters)
    return [_add4_join("rs_add4_%s_%d" % (tag, t), p, g2, pf, by_cols[t]) for t, (p, g2) in enumerate(zip(ps, got2))]


def _allreduce_small(pack):
    r = pack.shape[0]
    rp = -(-r // 16) * 16
    half = rp // 2
    if rp != r:
        pack = jnp.pad(pack, ((0, rp - r), (0, 0)))

    def body(p_ref, o_ref, sib, chipbuf, s1, r1, s2, r2, s3, r3):
        x, y, c, chips = _place()
        j = 2 * x + y
        mine = pl.ds(pl.multiple_of(c * half, 8), half)
        other = pl.ds(pl.multiple_of((1 - c) * half, 8), half)
        swap = pltpu.make_async_remote_copy(src_ref=p_ref.at[other, :], dst_ref=sib, send_sem=s1, recv_sem=r1,
                                            device_id=_sibling(), device_id_type=MESH)
        swap.start()
        swap.wait()
        chipbuf[j] = p_ref[mine, :] + sib[...]
        cps = []
        for k, chip in enumerate(chips):
            cp = pltpu.make_async_remote_copy(src_ref=chipbuf.at[j], dst_ref=chipbuf.at[j], send_sem=s2.at[k],
                                              recv_sem=r2.at[k], device_id=(*chip, c), device_id_type=MESH)
            cp.start()
            cps.append(cp)
        for k, chip in enumerate(chips):
            jk = 2 * chip[0] + chip[1]
            pltpu.make_async_remote_copy(src_ref=chipbuf.at[jk], dst_ref=chipbuf.at[jk], send_sem=s2.at[k],
                                         recv_sem=r2.at[k], device_id=(*chip, c), device_id_type=MESH).wait_recv()
        for cp in cps:
            cp.wait_send()
        o_ref[mine, :] = ((chipbuf[0] + chipbuf[1]) + chipbuf[2]) + chipbuf[3]
        join = pltpu.make_async_remote_copy(src_ref=o_ref.at[mine, :], dst_ref=o_ref.at[mine, :], send_sem=s3,
                                            recv_sem=r3, device_id=_sibling(), device_id_type=MESH)
        join.start()
        join.wait()

    dma = pltpu.SemaphoreType.DMA
    vm = pl.BlockSpec(memory_space=pltpu.VMEM)
    out = pl.pallas_call(
        body, name="allreduce_small", in_specs=[vm], out_specs=vm, out_shape=_sds((rp, LANE), F32),
        scratch_shapes=[pltpu.VMEM((half, LANE), F32), pltpu.VMEM((NCHIP, half, LANE), F32),
                        dma(()), dma(()), dma((3,)), dma((3,)), dma(()), dma(())],
        compiler_params=pltpu.CompilerParams(has_side_effects=True))(pack)
    return out[:r]


def _in_weights(win_g):
    full = jnp.concatenate([win_g[s] for s in range(NCHIP)], axis=0)
    wqkv = full[:2048]
    og = full[2048:3072]
    gates = jnp.pad(full[3072:3080], ((0, LANE - 8), (0, 0)))
    u = full[3080:4104]
    gb = full[4104:5128]
    gc = full[5128:6152]
    return wqkv, jnp.stack([og, u, gb, gc]), gates


def _in_grads(dwqkv, dwe, dwgt):
    full = jnp.concatenate([dwqkv, dwe[0], dwgt[:8], dwe[1], dwe[2], dwe[3]], axis=0)
    sw = DIN // NCHIP
    return jnp.stack([full[s * sw:(s + 1) * sw] for s in range(NCHIP)])


def _layer_fwd(l, h, get, small):
    lp = h.shape[0]
    th = lp // 2
    wqkv, we, wgt = _in_weights(get("in", h)[0])
    nmw, bias, mnw, cw, nfw = small
    tag = "_l%d" % l
    hn = _norm_fwd("norm_mix" + tag, h, nmw)
    qkv = _mm("proj_qkv" + tag, "nt", hn, wqkv,
              pl.BlockSpec((lp, D), lambda i, j, k: (0, 0)), pl.BlockSpec((512, D), lambda i, j, k: (j, 0)),
              pl.BlockSpec((lp, 512), lambda i, j, k: (0, j)), _sds((lp, 2048), BF16), (1, 4, 1))
    e = _mm("proj_e" + tag, "nt", hn, we,
            pl.BlockSpec((lp, D), lambda i, j, k: (0, 0)), pl.BlockSpec((None, 512, D), lambda i, j, k: (j // 2, j % 2, 0)),
            pl.BlockSpec((None, lp, 512), lambda i, j, k: (j // 2, 0, j % 2)), _sds((4, lp, 1024), BF16), (1, 8, 1))
    gpre = _mm("proj_gates" + tag, "nt", hn, wgt,
               pl.BlockSpec((lp, D), lambda i, j, k: (0, 0)), pl.BlockSpec((LANE, D), lambda i, j, k: (0, 0)),
               pl.BlockSpec((lp, LANE), lambda i, j, k: (0, 0)), _sds((lp, LANE), F32), (1, 1, 1))
    grow, mcol, ccol = _gate_prep("gate_prep" + tag, gpre, bias)
    ht = _mlstm_fwd("mlstm_fwd" + tag, qkv, grow, mcol, ccol, 4)
    mix = _mix_fwd("mix_fwd" + tag, ht, e, mnw, cw)
    wout_g = get("out", mix)[0]
    wout = wout_g.reshape(D, D)
    h1 = _mm("out_proj" + tag, "nn", mix, wout,
             pl.BlockSpec((None, th, 1024), lambda i, j, k: (k, i, 0)),
             pl.BlockSpec((1024, 1024), lambda i, j, k: (k, j)),
             pl.BlockSpec((th, 1024), lambda i, j, k: (i, j)), _sds((lp, D), F32), (2, 2, 2),
             acc_shape=(th, 1024), res=h, res_spec=pl.BlockSpec((th, 1024), lambda i, j, k: (i, j)))
    wg_g, wu_g = get("ffn", h1)
    hf = _norm_fwd("norm_ffn" + tag, h1, nfw)
    g, u, a = _ffn_up("ffn_up" + tag, hf, wg_g, wu_g)
    wd_g = get("down", a)[0]
    h2 = _ffn_down("ffn_down" + tag, a, wd_g, h1)
    saved = (h, hn, qkv, e, gpre, grow, mcol, ccol, ht, mix, h1, hf, g, u, a)
    return h2, saved, (wqkv, we, wgt, wout_g, wg_g, wu_g, wd_g)


def _layer_bwd(l, dh2, dh2b, saved, wts, small, ffn_done, dep):
    h, hn, qkv, e, gpre, grow, mcol, ccol, ht, mix, h1, hf, g, u, a = saved
    wqkv, we, wgt, wout_g, wg_g, wu_g, wd_g = wts
    nmw, bias, mnw, cw, nfw = small
    lp = h.shape[0]
    th = lp // 2
    tk = DFF // NCHIP
    tag = "_l%d" % l
    half_rows = lambda i, j, k: (i, j)

    dwd = _dw_pair("dw_down" + tag, a, dh2b, True, dep)
    dg, du = _ffn_bwd_act("ffn_bwd_act" + tag, dh2b, wd_g, g, u, dep)
    dwg = _dw_pair("dw_gate" + tag, hf, dg, False)
    dwu = _dw_pair("dw_up" + tag, hf, du, False)
    dhf = _ffn_bwd_in("dhf" + tag, dg, du, wg_g, wu_g)
    dh1, dh1b, dnfw = _norm_bwd("norm_ffn_bwd" + tag, dhf, h1, nfw, dh2)
    ffn_started = ffn_done([dwg, dwu, dwd])

    dwout = _mm("dw_out" + tag, "tn", mix, dh1b,
                pl.BlockSpec((None, lp, 1024), lambda i, j, k: (i, 0, 0)), pl.BlockSpec((lp, 1024), lambda i, j, k: (0, j)),
                pl.BlockSpec((1024, 1024), half_rows), _sds((D, D), BF16), (2, 2, 1), dep=ffn_started)
    wout = wout_g.reshape(D, D)
    dmix = _mm("dmix" + tag, "nt", dh1b, wout,
               pl.BlockSpec((th, D), lambda i, j, k: (i, 0)), pl.BlockSpec((1024, D), lambda i, j, k: (j, 0)),
               pl.BlockSpec((th, 1024), half_rows), _sds((lp, D), F32), (2, 2, 1), dep=ffn_started)
    dht, de, dmnw, dcw = _mix_bwd("mix_bwd" + tag, dmix, ht, e, mnw, cw)
    dq, dk, dv, dgrow, dfx = _mlstm_bwd("mlstm_bwd" + tag, qkv, grow, mcol, ccol, ht, dht, 4)
    dgp, dgpb, dbias = _gate_bwd("gate_bwd" + tag, gpre, bias, dgrow, dfx)
    del dgp
    dqkv = jnp.concatenate([dq, dk, dv], axis=1)

    hn_cols = pl.BlockSpec((lp, 1024), lambda i, j, k: (0, j))
    dwqkv = _mm("dw_qkv" + tag, "tn", dqkv, hn, pl.BlockSpec((lp, 1024), lambda i, j, k: (0, i)), hn_cols,
                pl.BlockSpec((1024, 1024), half_rows), _sds((2048, D), BF16), (2, 2, 1))
    dwe = _mm("dw_e" + tag, "tn", de, hn, pl.BlockSpec((None, lp, 1024), lambda i, j, k: (i, 0, 0)), hn_cols,
              pl.BlockSpec((None, 1024, 1024), lambda i, j, k: (i, 0, j)), _sds((4, 1024, D), BF16), (4, 2, 1))
    dwgt = _mm("dw_gates" + tag, "tn", dgpb, hn, pl.BlockSpec((lp, LANE), lambda i, j, k: (0, 0)), hn_cols,
               pl.BlockSpec((LANE, 1024), lambda i, j, k: (0, j)), _sds((LANE, D), BF16), (1, 2, 1))
    dwin = _in_grads(dwqkv, dwe, dwgt)

    dhn = _dhn("dhn" + tag, dqkv, de, dgpb, wqkv, we, wgt)
    dh0, dh0b, dnmw = _norm_bwd("norm_mix_bwd" + tag, dhn, h, nmw, dh1)

    pieces = [dwin, dwout.reshape(NCHIP, D // NCHIP, D)]
    smalls = (dnmw[0], dbias[0, :8], dcw[:3], dmnw[0], dnfw[0])
    return dh0, dh0b, pieces, smalls


def _pack_rows(parts):
    rows = []
    for p in parts:
        f = p.reshape(-1)
        pad = (-f.shape[0]) % LANE
        if pad:
            f = jnp.pad(f, (0, pad))
        rows.append(f.reshape(-1, LANE))
    r = jnp.concatenate(rows, axis=0)
    pad = (-r.shape[0]) % 8
    if pad:
        r = jnp.pad(r, ((0, pad), (0, 0)))
    return r


def _unpack_rows(pack, shapes):
    out, r0 = [], 0
    for s in shapes:
        n = 1
        for d in s:
            n *= d
        nr = -(-n // LANE)
        out.append(pack[r0:r0 + nr].reshape(-1)[:n].reshape(s))
        r0 += nr
    return out


def kernel(x, meta_tokens, norm_mix_w, w_in, b_gates, conv_w, mlstm_norm_w, w_out, norm_ffn_w, w_gate, w_up, w_down, norm_final_w, loss_target, m_meta_tokens, m_norm_mix_w, m_w_in, m_b_gates, m_conv_w, m_mlstm_norm_w, m_w_out, m_norm_ffn_w, m_w_gate, m_w_up, m_w_down, m_norm_final_w, v_meta_tokens, v_norm_mix_w, v_w_in, v_b_gates, v_conv_w, v_mlstm_norm_w, v_w_out, v_norm_ffn_w, v_w_gate, v_w_up, v_w_down, v_norm_final_w):
    seq = x.shape[1]
    n_real = N_META + seq
    lp = -(-n_real // LANE) * LANE
    xi, yi, ci = lax.axis_index("x"), lax.axis_index("y"), lax.axis_index("c")
    jchip = 2 * xi + yi
    pf = jnp.stack([ci, jchip, 2 * (1 - xi) + yi, 2 * xi + (1 - yi), 2 * (1 - xi) + (1 - yi)]).astype(jnp.int32)

    big = {"w_in": w_in, "w_out": w_out, "w_gate": w_gate, "w_up": w_up, "w_down": w_down}
    cast = {n: [_cast_into("cast_%s_l%d" % (n, l), w, l, pf) for l in range(DEPTH)] for n, w in big.items()
            if n != "w_in"}
    in_t = lambda a: jnp.transpose(a, (2, 0, 1))
    cast["w_in"] = [_cast_into_t("cast_w_in_l%d" % l, in_t(w_in), l, pf) for l in range(DEPTH)]
    conv_flat = jnp.pad(conv_w.reshape(DEPTH * 3, CW // NCHIP), ((0, 8 - DEPTH * 3), (0, 0)))

    def own_slot(a):
        return lax.dynamic_update_slice(jnp.zeros((NCHIP,) + a.shape, a.dtype), a[None], (jchip, 0, 0))

    groups, splits = [], []
    for l in range(DEPTH):
        groups.append([cast["w_in"][l]] + ([own_slot(meta_tokens), own_slot(conv_flat)] if l == 0 else []))
        splits.append([2] + ([0, 0] if l == 0 else []))
        groups += [[cast["w_out"][l]], [cast["w_gate"][l], cast["w_up"][l]], [cast["w_down"][l]]]
        splits += [[1], [1, 1], [1]]
    group_of = {"in": 0, "out": 1, "ffn": 2, "down": 3}
    sems, arrs, _ = _ag_start("ag_start_0", groups[:1], splits[:1])
    sems_rest, arrs_rest, all_started = _ag_start("ag_start_1", groups[1:], splits[1:])
    sems, arrs = sems + sems_rest, arrs + arrs_rest

    def gathered(gi, afters):
        got = _ag_wait("ag_wait_%d" % gi, arrs[gi], splits[gi], sems[gi][0], sems[gi][1], afters)
        axes = [s for s in splits[gi] if s]
        return list(_ag_forward("ag_forward_%d" % gi, got[:len(axes)], axes)) + list(got[len(axes):])

    win0_g, meta_g, conv_g = gathered(0, [all_started])
    meta_full = jnp.concatenate([meta_g[s] for s in range(NCHIP)], axis=1)
    conv_full = jnp.concatenate([conv_g[s][:DEPTH * 3] for s in range(NCHIP)], axis=1)
    conv_full = conv_full.reshape(DEPTH, 3, CW)

    bias_rows = jnp.pad(b_gates, ((0, 0), (0, LANE - 8)))
    smalls = []
    for l in range(DEPTH):
        smalls.append((norm_mix_w[l][None], bias_rows[l][None], mlstm_norm_w[l][None],
                       jnp.pad(conv_full[l], ((0, 5), (0, 0))), norm_ffn_w[l][None]))

    h = jnp.concatenate([meta_full, x[0], jnp.zeros((lp - n_real, D), F32)], axis=0)
    saved, wts = [], []
    for l in range(DEPTH):
        def get(which, after, l=l):
            if l == 0 and which == "in":
                return [win0_g]
            return gathered(4 * l + group_of[which], [after])

        h, sv, wt = _layer_fwd(l, h, get, smalls[l])
        saved.append(sv)
        wts.append(wt)
    tgt = jnp.pad(loss_target[0], ((N_META, lp - n_real), (0, 0)))
    dh, dhb, dnorm_final, loss_part = _loss_head(h, tgt, norm_final_w[None], n_real)

    names = ["w_in", "w_out", "w_gate", "w_up", "w_down"]
    params = {"w_in": (w_in, m_w_in, v_w_in), "w_out": (w_out, m_w_out, v_w_out), "w_gate": (w_gate, m_w_gate, v_w_gate),
              "w_up": (w_up, m_w_up, v_w_up), "w_down": (w_down, m_w_down, v_w_down)}
    big_out = {n: None for n in names}
    small_grads = [None] * DEPTH
    g_in = [None] * DEPTH

    def finish(l, group, handle, afters):
        for n, (g_mine, g_theirs) in zip(group, _rs_end(handle, afters, pf)):
            if n == "w_in":
                g_in[l] = (g_mine, g_theirs)
                continue
            w, m, v = params[n]
            big_out[n] = _adamw_layer("adamw_%s_l%d" % (n, l), l, g_mine, g_theirs, w, m, v, big_out[n], pf)
        return [big_out[n][3] for n in group if n != "w_in"]

    groups = []
    token = None
    for l in reversed(range(DEPTH)):
        def ffn_done(pieces, l=l):
            handle, tok = _rs_begin("l%df" % l, pieces, [False] * 3, pf)
            groups.append((l, names[2:], handle))
            return tok

        dh, dhb, pieces, small_grads[l] = _layer_bwd(l, dh, dhb, saved[l], wts[l], smalls[l], ffn_done, token)
        handle, token = _rs_begin("l%dm" % l, pieces, [True, False], pf)
        groups.append((l, names[:2], handle))
    afters = [token]
    for l, group, handle in groups:
        afters = finish(l, group, handle, afters)
    big_out["w_in"] = [jnp.transpose(a, (1, 2, 0))
                       for a in _adamw_t("adamw_w_in", g_in, in_t(w_in), in_t(m_w_in), in_t(v_w_in), pf)]

    dnmw = jnp.stack([small_grads[l][0] for l in range(DEPTH)])
    dbias = jnp.stack([small_grads[l][1] for l in range(DEPTH)])
    dconv = jnp.stack([small_grads[l][2] for l in range(DEPTH)])
    dmnw = jnp.stack([small_grads[l][3] for l in range(DEPTH)])
    dnfw = jnp.stack([small_grads[l][4] for l in range(DEPTH)])
    part_shapes = [(N_META, D), (DEPTH, D), (DEPTH, 8), (DEPTH, 3, CW), (DEPTH, MW), (DEPTH, D), (D,), (LANE,)]
    pack = _pack_rows([dh[:N_META], dnmw, dbias, dconv, dmnw, dnfw, dnorm_final[0], loss_part[0]])
    tot = _unpack_rows(_allreduce_small(pack), part_shapes)
    g_meta_full, g_nmw, g_bias, g_conv_full, g_mnw, g_nfw, g_final, loss_row = tot
    mcols = D // NCHIP
    ccols = CW // NCHIP
    g_meta = lax.dynamic_slice_in_dim(g_meta_full, jchip * mcols, mcols, axis=1)
    g_conv = lax.dynamic_slice_in_dim(g_conv_full, jchip * ccols, ccols, axis=2)
    sm_g = [g_meta, g_nmw, g_bias, g_conv, g_mnw, g_nfw, g_final]
    sm_w = [meta_tokens, norm_mix_w, b_gates, conv_w, mlstm_norm_w, norm_ffn_w, norm_final_w]
    sm_m = [m_meta_tokens, m_norm_mix_w, m_b_gates, m_conv_w, m_mlstm_norm_w, m_norm_ffn_w, m_norm_final_w]
    sm_v = [v_meta_tokens, v_norm_mix_w, v_b_gates, v_conv_w, v_mlstm_norm_w, v_norm_ffn_w, v_norm_final_w]
    sm_shapes = [w.shape for w in sm_w]
    d_p, m_p, v_p = _adamw_flat(_pack_rows(sm_g), _pack_rows(sm_w), _pack_rows(sm_m), _pack_rows(sm_v))
    sm_d = _unpack_rows(d_p, sm_shapes)
    sm_nm = _unpack_rows(m_p, sm_shapes)
    sm_nv = _unpack_rows(v_p, sm_shapes)

    loss = loss_row[0]
    grad_x = dh[N_META:n_real][None]

    def ordered(sm, which):
        bo = {n: big_out[n][which] for n in names}
        return [sm[0], sm[1], bo["w_in"], sm[2], sm[3], sm[4], bo["w_out"], sm[5], bo["w_gate"], bo["w_up"], bo["w_down"], sm[6]]

    return (loss, grad_x, *ordered(sm_g, 0), *ordered(sm_d, 1), *ordered(sm_nm, 2), *ordered(sm_nv, 3))
```
